```python
import jax, jax.numpy as jnp
from jax import lax
import numpy as np

D_MODEL = 2048
BATCH = 8
SEQ = 2048
DEPTH = 1

RET_HEADS = 4
RET_DK = 256
RET_DV = 256
RET_CHUNK = 128
RET_WIDTH = RET_HEADS * RET_DV
NSA_HEADS = 8
NSA_KV_GROUPS = 2
NSA_HPG = NSA_HEADS // NSA_KV_GROUPS
NSA_DK = 128
NSA_DV = 128
NSA_WIDTH = NSA_HEADS * NSA_DV
MIX_WIDTH = RET_WIDTH + NSA_WIDTH
CMP_BLOCK = 32
CMP_STRIDE = 16
CMP_HIDDEN = 256
SEL_BLOCK = 64
SEL_COUNT = 16
WIN = 512
WIN_QBLOCK = 128
SEL_QCHUNK = 64
N_GROUPS = 8
EXP_PER_GROUP = 8
N_EXPERTS = N_GROUPS * EXP_PER_GROUP
TOP_K_IN_GROUP = 2
D_EXPERT = 512
MOE_BLOCK = 128
ROPE_BASE = 10000.0
EPS = 1e-6
NEG = -1e30
FORCE_BONUS = 1e4
PROJ_SIZES = (
    RET_HEADS * RET_DK, RET_HEADS * RET_DK, RET_WIDTH, RET_WIDTH,
    NSA_HEADS * NSA_DK,
    NSA_KV_GROUPS * NSA_DK, NSA_KV_GROUPS * NSA_DV,
    NSA_KV_GROUPS * NSA_DK, NSA_KV_GROUPS * NSA_DV,
    NSA_KV_GROUPS * NSA_DK, NSA_KV_GROUPS * NSA_DV,
    NSA_HEADS * 3,
)
PROJ_TOTAL = sum(PROJ_SIZES)

kernel_name = 'hybrid_retention_nsa_hmoe_block'


def split_cols(a):
    outs, off = [], 0
    for s in PROJ_SIZES:
        outs.append(a[..., off:off + s])
        off += s
    return outs


def rms_norm(x, g):
    xf = x.astype(jnp.float32)
    y = xf * lax.rsqrt(jnp.mean(xf * xf, axis=-1, keepdims=True) + EPS)
    return (y * g).astype(x.dtype)


def masked_softmax(s, mask):
    s = jnp.where(mask, s.astype(jnp.float32), NEG)
    p = jax.nn.softmax(s, axis=-1)
    return jnp.where(mask, p, 0.0)


def rotary(x, pos):
    half = x.shape[-1] // 2
    inv = ROPE_BASE ** (-jnp.arange(half, dtype=jnp.float32) / half)
    ang = pos.astype(jnp.float32)[:, None] * inv[None, :]
    cos = jnp.cos(ang)[None, :, None, :]
    sin = jnp.sin(ang)[None, :, None, :]
    x1, x2 = x[..., :half], x[..., half:]
    return jnp.concatenate([x1 * cos - x2 * sin, x1 * sin + x2 * cos], axis=-1).astype(x.dtype)


def retention(q, k, v, g, gn_g):
    B, T, H, dk = q.shape
    dv = v.shape[-1]
    C = RET_CHUNK
    NC = T // C
    pos = jnp.arange(T)
    q = rotary(q, pos)
    k = rotary(k, pos) * (dk ** -0.5)
    log_gamma = jnp.log1p(-jnp.exp2(-5.0 - jnp.arange(H, dtype=jnp.float32)))
    idx = jnp.arange(C, dtype=jnp.float32)
    rel = idx[:, None] - idx[None, :]
    decay_in = jnp.where(rel >= 0, jnp.exp(log_gamma[:, None, None] * jnp.maximum(rel, 0.0)), 0.0)
    zeta = jnp.exp(log_gamma[:, None] * (C - 1 - idx)[None, :])
    q_decay = jnp.exp(log_gamma[:, None] * (idx + 1)[None, :])
    chunk_decay = jnp.exp(log_gamma * C)

    def to_chunks(a):
        return a.reshape(B, NC, C, H, a.shape[-1]).transpose(0, 3, 1, 2, 4)

    qc, kc, vc = to_chunks(q), to_chunks(k), to_chunks(v)
    s = jnp.einsum('bhncd,bhnmd->bhncm', qc, kc) * decay_in[None, :, None]
    inner = jnp.einsum('bhncm,bhnme->bhnce', s, vc)
    kv = jnp.einsum('bhnmd,bhnme->nbhde', kc * zeta[None, :, None, :, None], vc).astype(jnp.float32)

    def step(S, kv_n):
        return chunk_decay[None, :, None, None] * S + kv_n, S

    _, S_prev = lax.scan(step, jnp.zeros((B, H, dk, dv), jnp.float32), kv)
    cross = jnp.einsum('bhncd,nbhde->bhnce', qc.astype(jnp.float32), S_prev) * q_decay[None, :, None, :, None]
    o = (inner + cross).astype(jnp.float32).transpose(0, 2, 3, 1, 4).reshape(B, T, H, dv)
    mu = jnp.mean(o, axis=-1, keepdims=True)
    var = jnp.mean(jnp.square(o - mu), axis=-1, keepdims=True)
    o = ((o - mu) * lax.rsqrt(var + EPS)).reshape(B, T, H * dv) * gn_g
    return o * jax.nn.silu(g.astype(jnp.float32))


def nsa(q, k_cmp, v_cmp, k_slc, v_slc, k_win, v_win, gate_logits,
        pos_k, w1_k, w2_k, pos_v, w1_v, w2_v):
    B, T = q.shape[:2]
    G, Hg, d = NSA_KV_GROUPS, NSA_HPG, NSA_DK
    qg = (q * (d ** -0.5)).reshape(B, T, G, Hg, d).transpose(0, 2, 3, 1, 4)
    t_pos = jnp.arange(T)

    def kv_heads(a):
        return a.reshape(B, T, G, -1).transpose(0, 2, 1, 3)

    Nc = (T - CMP_BLOCK) // CMP_STRIDE + 1
    blk_idx = jnp.arange(Nc)[:, None] * CMP_STRIDE + jnp.arange(CMP_BLOCK)[None, :]

    def compress(a, pe, w1, w2):
        blocks = kv_heads(a)[:, :, blk_idx] + pe
        flat = blocks.reshape(B, G, Nc, -1)
        return jax.nn.silu(flat @ w1) @ w2

    kc = compress(k_cmp, pos_k, w1_k, w2_k)
    vc = compress(v_cmp, pos_v, w1_v, w2_v)
    cmp_end = jnp.arange(Nc) * CMP_STRIDE + CMP_BLOCK - 1
    cmask = cmp_end[None, :] <= t_pos[:, None]
    p_cmp = masked_softmax(jnp.einsum('bghtd,bgcd->bghtc', qg, kc), cmask)
    o_cmp = jnp.einsum('bghtc,bgcd->bghtd', p_cmp.astype(vc.dtype), vc)

    Ns = T // SEL_BLOCK
    n_sel = min(SEL_COUNT, Ns)
    cmp_start = jnp.arange(Nc) * CMP_STRIDE
    sel_start = jnp.arange(Ns) * SEL_BLOCK
    overlap = ((cmp_start[:, None] < sel_start[None, :] + SEL_BLOCK)
               & (cmp_start[:, None] + CMP_BLOCK > sel_start[None, :])).astype(jnp.float32)
    imp = jnp.einsum('bghtc,cs->bgts', p_cmp, overlap)
    cur = t_pos // SEL_BLOCK
    sblk = jnp.arange(Ns)
    valid = sblk[None, :] <= cur[:, None]
    forced = (sblk[None, :] == 0) | (sblk[None, :] == cur[:, None]) | (sblk[None, :] == cur[:, None] - 1)
    score = jnp.where(valid, imp + jnp.where(forced, FORCE_BONUS, 0.0), -1.0)
    _, sel_idx = lax.top_k(score, n_sel)

    ks_blocks = kv_heads(k_slc).reshape(B, G, Ns, SEL_BLOCK, -1)
    vs_blocks = kv_heads(v_slc).reshape(B, G, Ns, SEL_BLOCK, -1)
    QC = min(SEL_QCHUNK, T)
    NQ = T // QC
    q_ch = qg.reshape(B, G, Hg, NQ, QC, d).transpose(3, 0, 1, 2, 4, 5)
    idx_ch = sel_idx.reshape(B, G, NQ, QC, n_sel).transpose(2, 0, 1, 3, 4)
    t_ch = t_pos.reshape(NQ, QC)
    b_ix = jnp.arange(B)[:, None, None, None]
    g_ix = jnp.arange(G)[None, :, None, None]

    def sel_chunk(args):
        qq, ii, tt = args
        kk = ks_blocks[b_ix, g_ix, ii]
        vv = vs_blocks[b_ix, g_ix, ii]
        kpos = ii[..., None] * SEL_BLOCK + jnp.arange(SEL_BLOCK)
        mask = (kpos <= tt[None, None, :, None, None]).reshape(B, G, 1, QC, -1)
        s = jnp.einsum('bghqd,bgqnld->bghqnl', qq, kk).reshape(B, G, Hg, QC, -1)
        p = masked_softmax(s, mask).reshape(B, G, Hg, QC, n_sel, SEL_BLOCK)
        return jnp.einsum('bghqnl,bgqnld->bghqd', p.astype(vv.dtype), vv)

    o_slc = lax.map(sel_chunk, (q_ch, idx_ch, t_ch))
    o_slc = o_slc.transpose(1, 2, 3, 0, 4, 5).reshape(B, G, Hg, T, -1)

    WB = WIN_QBLOCK
    NB = T // WB
    nprev = WIN // WB

    def windows(a):
        ap = jnp.pad(kv_heads(a), ((0, 0), (0, 0), (WIN, 0), (0, 0))).reshape(B, G, NB + nprev, WB, -1)
        return jnp.stack([ap[:, :, i:i + NB] for i in range(nprev + 1)], axis=3).reshape(B, G, NB, (nprev + 1) * WB, -1)

    kw, vw = windows(k_win), windows(v_win)
    qpos = t_pos.reshape(NB, WB)
    kpos = (jnp.arange(NB) * WB - WIN)[:, None] + jnp.arange((nprev + 1) * WB)[None, :]
    delta = qpos[:, :, None] - kpos[:, None, :]
    wmask = (delta >= 0) & (delta < WIN) & (kpos[:, None, :] >= 0)
    qb = qg.reshape(B, G, Hg, NB, WB, d)
    p_win = masked_softmax(jnp.einsum('bghnqd,bgnkd->bghnqk', qb, kw), wmask)
    o_win = jnp.einsum('bghnqk,bgnkd->bghnqd', p_win.astype(vw.dtype), vw).reshape(B, G, Hg, T, -1)

    gt = jax.nn.sigmoid(gate_logits.astype(jnp.float32)).reshape(B, T, G, Hg, 3).transpose(0, 2, 3, 1, 4)
    o = gt[..., 0:1] * o_cmp + gt[..., 1:2] * o_slc + gt[..., 2:3] * o_win
    return o.transpose(0, 3, 1, 2, 4).reshape(B, T, NSA_HEADS * NSA_DV)


def hier_moe(h, w_grp, b_grp, w_exp, b_exp, w_gate, w_up, w_down):
    B, T, D = h.shape
    N = B * T
    K = TOP_K_IN_GROUP
    hf = h.reshape(N, D)
    pg = jax.nn.softmax((hf @ w_grp).astype(jnp.float32) + b_grp, axis=-1)
    pg_top, grp = lax.top_k(pg, 1)
    le = ((hf @ w_exp).astype(jnp.float32) + b_exp).reshape(N, N_GROUPS, EXP_PER_GROUP)
    le_g = le[jnp.arange(N), grp[:, 0]]
    pe_top, loc = lax.top_k(jax.nn.softmax(le_g, axis=-1), K)
    wts = pg_top * pe_top / jnp.sum(pe_top, axis=-1, keepdims=True)
    eid = grp * EXP_PER_GROUP + loc

    A = N * K
    e_flat = eid.reshape(A)
    tok = jnp.repeat(jnp.arange(N, dtype=jnp.int32), K)
    w_flat = wts.reshape(A)
    order = jnp.argsort(e_flat)
    e_sorted = e_flat[order]
    counts = jnp.bincount(e_flat, length=N_EXPERTS)
    padded = (counts + MOE_BLOCK - 1) // MOE_BLOCK * MOE_BLOCK
    pad_end = jnp.cumsum(padded)
    pad_start = pad_end - padded
    start = jnp.cumsum(counts) - counts
    dest = pad_start[e_sorted] + (jnp.arange(A) - start[e_sorted])
    n_blocks = -(-A // MOE_BLOCK) + N_EXPERTS
    P = n_blocks * MOE_BLOCK
    row_tok = jnp.zeros((P,), jnp.int32).at[dest].set(tok[order])
    row_w = jnp.zeros((P,), jnp.float32).at[dest].set(w_flat[order])
    blk_exp = jnp.minimum(jnp.searchsorted(pad_end, jnp.arange(n_blocks) * MOE_BLOCK, side='right'), N_EXPERTS - 1)
    xs = hf[row_tok].reshape(n_blocks, MOE_BLOCK, D)

    def expert_block(args):
        xb, e = args
        a = jax.nn.silu(xb @ w_gate[e]) * (xb @ w_up[e])
        return a @ w_down[e]

    ys = lax.map(expert_block, (xs, blk_exp)).reshape(P, D)
    out = jnp.zeros((N, D), jnp.float32).at[row_tok].add(ys.astype(jnp.float32) * row_w[:, None])
    return out.reshape(B, T, D).astype(h.dtype)


def setup_inputs(seed: int = 0) -> dict:
    key = jax.random.key(seed)
    ks = jax.random.split(key, 24)
    L, D = DEPTH, D_MODEL

    def nrm(k, shape, scale):
        return jax.random.normal(k, shape, jnp.float32) * scale

    return {
        'x': nrm(ks[0], (BATCH, SEQ, D), 1.0),
        'c': nrm(ks[1], (BATCH, D), 1.0),
        'w_ada': nrm(ks[2], (L, D, 6 * D), 0.5 * D ** -0.5),
        'b_ada': nrm(ks[3], (L, 6 * D), 0.02),
        'norm1_g': 1.0 + nrm(ks[4], (L, D), 0.02),
        'norm2_g': 1.0 + nrm(ks[5], (L, D), 0.02),
        'final_g': 1.0 + nrm(ks[6], (D,), 0.02),
        'w_in': nrm(ks[7], (L, D, PROJ_TOTAL), D ** -0.5),
        'ret_gn_g': 1.0 + nrm(ks[8], (L, RET_WIDTH), 0.02),
        'cmp_pos_k': nrm(ks[9], (L, CMP_BLOCK, NSA_DK), 0.1),
        'cmp_w1_k': nrm(ks[10], (L, CMP_BLOCK * NSA_DK, CMP_HIDDEN), (CMP_BLOCK * NSA_DK) ** -0.5),
        'cmp_w2_k': nrm(ks[11], (L, CMP_HIDDEN, NSA_DK), CMP_HIDDEN ** -0.5),
        'cmp_pos_v': nrm(ks[12], (L, CMP_BLOCK, NSA_DV), 0.1),
        'cmp_w1_v': nrm(ks[13], (L, CMP_BLOCK * NSA_DV, CMP_HIDDEN), (CMP_BLOCK * NSA_DV) ** -0.5),
        'cmp_w2_v': nrm(ks[14], (L, CMP_HIDDEN, NSA_DV), CMP_HIDDEN ** -0.5),
        'w_out': nrm(ks[15], (L, MIX_WIDTH, D), MIX_WIDTH ** -0.5),
        'w_grp': nrm(ks[16], (L, D, N_GROUPS), D ** -0.5),
        'b_grp': nrm(ks[17], (L, N_GROUPS), 0.01),
        'w_exp': nrm(ks[18], (L, D, N_EXPERTS), D ** -0.5),
        'b_exp': nrm(ks[19], (L, N_EXPERTS), 0.01),
        'w_gate': nrm(ks[20], (L, N_EXPERTS, D, D_EXPERT), D ** -0.5),
        'w_up': nrm(ks[21], (L, N_EXPERTS, D, D_EXPERT), D ** -0.5),
        'w_down': nrm(ks[22], (L, N_EXPERTS, D_EXPERT, D), D_EXPERT ** -0.5),
    }


def reference(x, c, w_ada, b_ada, norm1_g, norm2_g, final_g, w_in, ret_gn_g,
              cmp_pos_k, cmp_w1_k, cmp_w2_k, cmp_pos_v, cmp_w1_v, cmp_w2_v,
              w_out, w_grp, b_grp, w_exp, b_exp, w_gate, w_up, w_down):
    B, T, D = x.shape
    c_act = jax.nn.silu(c)
    for l in range(DEPTH):
        mod = c_act @ w_ada[l] + b_ada[l]
        sh1, sc1, g1, sh2, sc2, g2 = jnp.split(mod[:, None, :], 6, axis=-1)
        h = rms_norm(x, norm1_g[l]) * (1.0 + sc1) + sh1
        rq, rk, rv, rg, nq, kc, vc, ks_, vs_, kw, vw, ng = split_cols(h @ w_in[l])
        o_ret = retention(rq.reshape(B, T, RET_HEADS, RET_DK), rk.reshape(B, T, RET_HEADS, RET_DK),
                          rv.reshape(B, T, RET_HEADS, RET_DV), rg, ret_gn_g[l])
        o_nsa = nsa(nq, kc, vc, ks_, vs_, kw, vw, ng,
                    cmp_pos_k[l], cmp_w1_k[l], cmp_w2_k[l], cmp_pos_v[l], cmp_w1_v[l], cmp_w2_v[l])
        mix = jnp.concatenate([o_ret.astype(x.dtype), o_nsa.astype(x.dtype)], axis=-1) @ w_out[l]
        x = x + g1 * mix
        h2 = rms_norm(x, norm2_g[l]) * (1.0 + sc2) + sh2
        x = x + g2 * hier_moe(h2, w_grp[l], b_grp[l], w_exp[l], b_exp[l], w_gate[l], w_up[l], w_down[l])
    return rms_norm(x, final_g)
```

```python
import functools
import math

import numpy as np
import jax
import jax.numpy as jnp
from jax import lax
from jax.experimental import pallas as pl
from jax.experimental.pallas import tpu as pltpu

F32 = jnp.float32
BF16 = jnp.bfloat16
I32 = jnp.int32

RET_HEADS = 4
RET_DK = 256
RET_DV = 256
RET_CHUNK = 128
NSA_HEADS = 8
NSA_KV_GROUPS = 2
NSA_HPG = NSA_HEADS // NSA_KV_GROUPS
NSA_D = 128
CMP_BLOCK = 32
CMP_STRIDE = 16
CMP_HIDDEN = 256
SEL_BLOCK = 64
SEL_COUNT = 16
WIN = 512
N_GROUPS = 8
EXP_PER_GROUP = 8
N_EXPERTS = N_GROUPS * EXP_PER_GROUP
D_EXPERT = 512
ROPE_BASE = 10000.0
EPS = 1e-6
NEG = -1e30
FORCE_BONUS = 1e4

LANES = 128
VMEM_LIMIT = 52 * 1024 * 1024

_C_RQ, _C_RK, _C_RV, _C_RG = 0, 8, 16, 24
_C_NQ = 32
_C_KC, _C_VC, _C_KS, _C_VS, _C_KW, _C_VW = 40, 42, 44, 46, 48, 50
_C_GATE = 52
PROJ_MAIN = _C_GATE * LANES
N_GATE_COLS = NSA_HEADS * 3


def _cp(sem, vmem=VMEM_LIMIT):
    return pltpu.CompilerParams(dimension_semantics=sem, vmem_limit_bytes=vmem)


def _silu(v):
    return v * jax.nn.sigmoid(v)


def _dot(a, b):
    return jnp.dot(a, b, preferred_element_type=F32)


def _dot_nt(a, b):
    return lax.dot_general(a, b, (((1,), (1,)), ((), ())), preferred_element_type=F32)


def _dot_tn(a, b):
    return lax.dot_general(a, b, (((0,), (0,)), ((), ())), preferred_element_type=F32)


def _ada_kernel(c_ref, w_ref, b_ref, o_ref):
    ca = _silu(c_ref[...])
    o_ref[...] = jnp.dot(ca, w_ref[...], preferred_element_type=F32,
                         precision=lax.Precision.HIGHEST) + b_ref[...]


def _ada(c, w, b):
    bsz, d = c.shape
    n = w.shape[1]
    tn = min(1024, n)
    return pl.pallas_call(
        _ada_kernel,
        grid=(n // tn,),
        in_specs=[pl.BlockSpec((bsz, d), lambda j: (0, 0)),
                  pl.BlockSpec((d, tn), lambda j: (0, j)),
                  pl.BlockSpec((1, tn), lambda j: (0, j))],
        out_specs=pl.BlockSpec((bsz, tn), lambda j: (0, j)),
        out_shape=jax.ShapeDtypeStruct((bsz, n), F32),
        compiler_params=_cp(("arbitrary",)),
        name="ada",
    )(c, w, b.reshape(1, n))


def _inproj_kernel(x_ref, mod_ref, g_ref, w_ref, wg_ref, proj_ref, gate_ref, h_ref):
    @pl.when(pl.program_id(1) == 0)
    def _():
        x = x_ref[...]
        y = x * lax.rsqrt(jnp.mean(x * x, axis=-1, keepdims=True) + EPS) * g_ref[...]
        h = (y * (1.0 + mod_ref[0, 1:2, :]) + mod_ref[0, 0:1, :]).astype(BF16)
        h_ref[...] = h
        gate_ref[...] = _dot(h, wg_ref[...])

    proj_ref[...] = _dot(h_ref[...], w_ref[...]).astype(BF16)


def _inproj(x2, mod3, g, w_bf, wg_bf, seq):
    n, d = x2.shape
    tm = min(1024, seq)
    tn = 512
    nj = PROJ_MAIN // tn
    return pl.pallas_call(
        _inproj_kernel,
        grid=(n // tm, nj),
        in_specs=[pl.BlockSpec((tm, d), lambda i, j: (i, 0)),
                  pl.BlockSpec((1, 6, d), lambda i, j: ((i * tm) // seq, 0, 0)),
                  pl.BlockSpec((1, d), lambda i, j: (0, 0)),
                  pl.BlockSpec((d, tn), lambda i, j: (0, j)),
                  pl.BlockSpec((d, LANES), lambda i, j: (0, 0))],
        out_specs=[pl.BlockSpec((tm, tn), lambda i, j: (i, j)),
                   pl.BlockSpec((tm, LANES), lambda i, j: (i, 0))],
        out_shape=[jax.ShapeDtypeStruct((n, PROJ_MAIN), BF16),
                   jax.ShapeDtypeStruct((n, LANES), F32)],
        scratch_shapes=[pltpu.VMEM((tm, d), BF16)],
        compiler_params=_cp(("arbitrary", "arbitrary")),
        name="inproj",
    )(x2, mod3, g.reshape(1, d), w_bf, wg_bf)


def _retention_kernel(q_ref, k_ref, v_ref, g_ref, cos_ref, sin_ref, din_ref, zeta_ref,
                      qdec_ref, cdec_ref, gn_ref, o_ref, s_ref):
    @pl.when(pl.program_id(1) == 0)
    def _():
        s_ref[...] = jnp.zeros_like(s_ref)

    cos = cos_ref[...]
    sin = sin_ref[...]
    half = RET_DK // 2

    def rot(a):
        a1, a2 = a[:, :half], a[:, half:]
        return jnp.concatenate([a1 * cos - a2 * sin, a1 * sin + a2 * cos], axis=1)

    for h in range(RET_HEADS):
        qs = slice(h * RET_DK, (h + 1) * RET_DK)
        vs = slice(h * RET_DV, (h + 1) * RET_DV)
        q = rot(q_ref[:, qs].astype(F32))
        k = rot(k_ref[:, qs].astype(F32)) * (RET_DK ** -0.5)
        v = v_ref[:, vs]
        qb = q.astype(BF16)
        kb = k.astype(BF16)
        s = _dot_nt(qb, kb) * din_ref[h]
        inner = _dot(s.astype(BF16), v)
        s_prev = s_ref[h]
        cross = _dot(qb, s_prev.astype(BF16)) * qdec_ref[h]
        kv = _dot_tn((k * zeta_ref[h]).astype(BF16), v)
        s_ref[h] = cdec_ref[h] * s_prev + kv
        o = inner + cross
        mu = jnp.mean(o, axis=-1, keepdims=True)
        oc = o - mu
        var = jnp.mean(oc * oc, axis=-1, keepdims=True)
        o = oc * lax.rsqrt(var + EPS) * gn_ref[:, vs]
        o_ref[:, vs] = (o * _silu(g_ref[:, vs].astype(F32))).astype(BF16)


def _retention(proj, gn_g, bsz, seq):
    n = proj.shape[0]
    c = RET_CHUNK
    nc = seq // c
    hw = RET_HEADS * RET_DK
    half = RET_DK // 2
    pos = jnp.arange(seq, dtype=F32)
    inv = ROPE_BASE ** (-jnp.arange(half, dtype=F32) / half)
    ang = pos[:, None] * inv[None, :]
    cos, sin = jnp.cos(ang), jnp.sin(ang)
    log_gamma = jnp.log1p(-jnp.exp2(-5.0 - jnp.arange(RET_HEADS, dtype=F32)))
    idx = jnp.arange(c, dtype=F32)
    rel = idx[:, None] - idx[None, :]
    decay_in = jnp.where(rel >= 0, jnp.exp(log_gamma[:, None, None] * jnp.maximum(rel, 0.0)), 0.0)
    zeta = jnp.exp(log_gamma[:, None] * (c - 1 - idx)[None, :])[:, :, None]
    q_decay = jnp.exp(log_gamma[:, None] * (idx + 1)[None, :])[:, :, None]
    chunk_decay = jnp.exp(log_gamma * c)[:, None, None]
    row = lambda b, t: (b * nc + t)
    return pl.pallas_call(
        _retention_kernel,
        grid=(bsz, nc),
        in_specs=[pl.BlockSpec((c, hw), lambda b, t: (row(b, t), 0)),
                  pl.BlockSpec((c, hw), lambda b, t: (row(b, t), 1)),
                  pl.BlockSpec((c, hw), lambda b, t: (row(b, t), 2)),
                  pl.BlockSpec((c, hw), lambda b, t: (row(b, t), 3)),
                  pl.BlockSpec((c, half), lambda b, t: (t, 0)),
                  pl.BlockSpec((c, half), lambda b, t: (t, 0)),
                  pl.BlockSpec((RET_HEADS, c, c), lambda b, t: (0, 0, 0)),
                  pl.BlockSpec((RET_HEADS, c, 1), lambda b, t: (0, 0, 0)),
                  pl.BlockSpec((RET_HEADS, c, 1), lambda b, t: (0, 0, 0)),
                  pl.BlockSpec((RET_HEADS, 1, 1), lambda b, t: (0, 0, 0)),
                  pl.BlockSpec((1, hw), lambda b, t: (0, 0))],
        out_specs=pl.BlockSpec((c, hw), lambda b, t: (row(b, t), 0)),
        out_shape=jax.ShapeDtypeStruct((n, hw), BF16),
        scratch_shapes=[pltpu.VMEM((RET_HEADS, RET_DK, RET_DV), F32)],
        compiler_params=_cp(("arbitrary", "arbitrary")),
        name="retention",
    )(proj, proj, proj, proj, cos, sin, decay_in, zeta, q_decay, chunk_decay,
      gn_g.reshape(1, hw))


def _compress_kernel(ak_ref, av_ref, pek_ref, pev_ref, w1k_ref, w1v_ref, w2k_ref, w2v_ref,
                     kc_ref, vc_ref):
    half = CMP_STRIDE * NSA_D

    def one(a_ref, pe_ref, w1_ref, w2_ref, o_ref):
        a = a_ref[...].astype(F32)
        lo = _dot((a + pe_ref[0:1, :]).astype(BF16), w1_ref[:half, :].astype(BF16))
        hi = _dot((a + pe_ref[1:2, :]).astype(BF16), w1_ref[half:, :].astype(BF16))
        rows = hi.shape[0]
        pre = lo + pltpu.roll(hi, rows - 1, 0)
        o_ref[...] = _dot(_silu(pre).astype(BF16), w2_ref[...].astype(BF16)).astype(BF16)

    one(ak_ref, pek_ref, w1k_ref, w2k_ref, kc_ref)
    one(av_ref, pev_ref, w1v_ref, w2v_ref, vc_ref)


def _compress(ak, av, pek, pev, w1k, w1v, w2k, w2v, rows_per):
    r, w = ak.shape
    full = lambda a: pl.BlockSpec(a.shape, lambda i: (0,) * a.ndim)
    blk = pl.BlockSpec((rows_per, w), lambda i: (i, 0))
    oblk = pl.BlockSpec((rows_per, NSA_D), lambda i: (i, 0))
    return pl.pallas_call(
        _compress_kernel,
        grid=(r // rows_per,),
        in_specs=[blk, blk, full(pek), full(pev), full(w1k), full(w1v), full(w2k), full(w2v)],
        out_specs=[oblk, oblk],
        out_shape=[jax.ShapeDtypeStruct((r, NSA_D), BF16)] * 2,
        compiler_params=_cp(("arbitrary",)),
        name="compress",
    )(ak, av, pek, pev, w1k, w1v, w2k, w2v)


def _masked_softmax(s, mask):
    s = jnp.where(mask, s, NEG)
    m = jnp.max(s, axis=-1, keepdims=True)
    e = jnp.where(mask, jnp.exp(s - m), 0.0)
    den = jnp.sum(e, axis=-1, keepdims=True)
    return jnp.where(den > 0.0, e / jnp.where(den > 0.0, den, 1.0), 0.0)


def _nsa_kernel(q_ref, kc_ref, vc_ref, ks_ref, vs_ref, kw_ref, vw_ref, gate_ref, ov_ref,
                o_ref, kaug_ref, *, tq, tk, seq, ncmp, wlen):
    crow = kc_ref.shape[0]
    i = pl.program_id(2)
    hg = NSA_HPG
    d = NSA_D
    r = hg * tq

    @pl.when(i == 0)
    def _():
        kaug_ref[:, :d] = ks_ref[...]
        blk = lax.broadcasted_iota(I32, (seq, LANES), 0) // SEL_BLOCK
        lane = lax.broadcasted_iota(I32, (seq, LANES), 1)
        kaug_ref[:, d:] = (blk == lane).astype(BF16)

    q = q_ref[...]
    qh = [(q[:, h * d:(h + 1) * d].astype(F32) * (d ** -0.5)).astype(BF16) for h in range(hg)]
    qa = jnp.concatenate(qh, axis=0)
    t1 = i * tq + lax.broadcasted_iota(I32, (tq, 1), 0)
    tpos = jnp.concatenate([t1] * hg, axis=0)
    lane = lax.broadcasted_iota(I32, (1, LANES), 1)

    sc = _dot_nt(qa, kc_ref[...])
    clane = lax.broadcasted_iota(I32, (1, crow), 1)
    cmask = (clane * CMP_STRIDE + (CMP_BLOCK - 1) <= tpos) & (clane < ncmp)
    p_cmp = _masked_softmax(sc, cmask)
    o_cmp = _dot(p_cmp.astype(BF16), vc_ref[...])

    psum = p_cmp[0:tq]
    for h in range(1, hg):
        psum = psum + p_cmp[h * tq:(h + 1) * tq]
    imp = jnp.dot(psum, ov_ref[...], preferred_element_type=F32, precision=lax.Precision.HIGHEST)
    cur = t1 // SEL_BLOCK
    valid = lane <= cur
    forced = (lane == 0) | (lane == cur) | (lane == cur - 1)
    score = jnp.where(valid, imp + jnp.where(forced, FORCE_BONUS, 0.0), -1.0)
    nsel_blocks = seq // SEL_BLOCK
    rank = jnp.zeros((tq, LANES), F32)
    for s2 in range(nsel_blocks):
        col = score[:, s2:s2 + 1]
        beats = (col > score) | ((col == score) & (lane > s2))
        rank = rank + beats.astype(F32)
    sel = valid & (rank < float(min(SEL_COUNT, nsel_blocks)))
    selb = jnp.where(sel, 0.0, NEG).astype(BF16)
    q_aug = jnp.concatenate([jnp.concatenate([qh[h], selb], axis=1) for h in range(hg)], axis=0)

    n_tiles = (i * tq + tq + tk - 1) // tk
    kl = lax.broadcasted_iota(I32, (1, tk), 1)

    def body(j, carry):
        m, l, acc = carry
        k0 = pl.multiple_of(j * tk, tk)
        s = _dot_nt(q_aug, kaug_ref[pl.ds(k0, tk), :])
        s = jnp.where(k0 + kl <= tpos, s, NEG)
        m_new = jnp.maximum(m, jnp.max(s, axis=-1, keepdims=True))
        alpha = jnp.exp(m - m_new)
        p = jnp.exp(s - m_new)
        l = alpha * l + jnp.sum(p, axis=-1, keepdims=True)
        acc = alpha * acc + _dot(p.astype(BF16), vs_ref[pl.ds(k0, tk), :])
        return m_new, l, acc

    m0 = jnp.full((r, 1), NEG, F32)
    l0 = jnp.zeros((r, 1), F32)
    a0 = jnp.zeros((r, d), F32)
    _, l_s, acc_s = lax.fori_loop(0, n_tiles, body, (m0, l0, a0))
    o_slc = acc_s / l_s

    w0 = pl.multiple_of(jnp.maximum(i * tq + tq - wlen, 0), tq)
    sw = _dot_nt(qa, kw_ref[pl.ds(w0, wlen), :])
    delta = tpos - (w0 + lax.broadcasted_iota(I32, (1, wlen), 1))
    p_win = _masked_softmax(sw, (delta >= 0) & (delta < WIN))
    o_win = _dot(p_win.astype(BF16), vw_ref[pl.ds(w0, wlen), :])

    gt = jax.nn.sigmoid(gate_ref[0])
    for h in range(hg):
        rows = slice(h * tq, (h + 1) * tq)
        o = (gt[:, 3 * h:3 * h + 1] * o_cmp[rows] + gt[:, 3 * h + 1:3 * h + 2] * o_slc[rows]
             + gt[:, 3 * h + 2:3 * h + 3] * o_win[rows])
        o_ref[:, h * d:(h + 1) * d] = o.astype(BF16)


def _nsa(proj, kc, vc, gates_g, bsz, seq):
    n = proj.shape[0]
    g_ = NSA_KV_GROUPS
    tq = 128
    tk = 256
    nq = seq // tq
    ncmp = (seq - CMP_BLOCK) // CMP_STRIDE + 1
    wlen = min(WIN + tq, seq)
    nsel = seq // SEL_BLOCK
    crow = seq // CMP_STRIDE
    assert nsel <= LANES and crow <= LANES
    cs = np.arange(crow)[:, None] * CMP_STRIDE
    ss = np.arange(LANES)[None, :] * SEL_BLOCK
    ov = ((cs < ss + SEL_BLOCK) & (cs + CMP_BLOCK > ss)
          & (np.arange(crow)[:, None] < ncmp) & (np.arange(LANES)[None, :] < nsel))
    ov = jnp.asarray(ov.astype(np.float32))
    kvspec = lambda c0: pl.BlockSpec((seq, NSA_D), lambda b, g, i: (b, c0 + g))
    kern = functools.partial(_nsa_kernel, tq=tq, tk=tk, seq=seq, ncmp=ncmp, wlen=wlen)
    return pl.pallas_call(
        kern,
        grid=(bsz, g_, nq),
        in_specs=[pl.BlockSpec((tq, NSA_HPG * NSA_D), lambda b, g, i: (b * nq + i, _C_NQ // NSA_HPG + g)),
                  pl.BlockSpec((crow, NSA_D), lambda b, g, i: (b * g_ + g, 0)),
                  pl.BlockSpec((crow, NSA_D), lambda b, g, i: (b * g_ + g, 0)),
                  kvspec(_C_KS), kvspec(_C_VS), kvspec(_C_KW), kvspec(_C_VW),
                  pl.BlockSpec((1, tq, LANES), lambda b, g, i: (g, b * nq + i, 0)),
                  pl.BlockSpec((crow, LANES), lambda b, g, i: (0, 0))],
        out_specs=pl.BlockSpec((tq, NSA_HPG * NSA_D), lambda b, g, i: (b * nq + i, g)),
        out_shape=jax.ShapeDtypeStruct((n, NSA_HEADS * NSA_D), BF16),
        scratch_shapes=[pltpu.VMEM((seq, 2 * NSA_D), BF16)],
        compiler_params=_cp(("arbitrary", "arbitrary", "arbitrary")),
        name="nsa",
    )(proj, kc, vc, proj, proj, proj, proj, gates_g, ov)


def _outproj_kernel(oret_ref, onsa_ref, x_ref, mod_ref, g_ref, w_ref, wr_ref,
                    x1_ref, h2_ref, lt_ref):
    hw = oret_ref.shape[1]
    mix = _dot(oret_ref[...], w_ref[:hw, :]) + _dot(onsa_ref[...], w_ref[hw:, :])
    x1 = x_ref[...] + mod_ref[0, 2:3, :] * mix
    x1_ref[...] = x1
    y = x1 * lax.rsqrt(jnp.mean(x1 * x1, axis=-1, keepdims=True) + EPS) * g_ref[...]
    h2 = y * (1.0 + mod_ref[0, 4:5, :]) + mod_ref[0, 3:4, :]
    h2_ref[...] = h2
    lt_ref[...] = _dot_nt(wr_ref[...], h2.astype(BF16))


def _outproj(o_ret, o_nsa, x2, mod3, g, w_bf, wr_bf, seq):
    n, d = x2.shape
    tm = min(512, seq)
    hw = o_ret.shape[1]
    return pl.pallas_call(
        _outproj_kernel,
        grid=(n // tm,),
        in_specs=[pl.BlockSpec((tm, hw), lambda i: (i, 0)),
                  pl.BlockSpec((tm, o_nsa.shape[1]), lambda i: (i, 0)),
                  pl.BlockSpec((tm, d), lambda i: (i, 0)),
                  pl.BlockSpec((1, 6, d), lambda i: ((i * tm) // seq, 0, 0)),
                  pl.BlockSpec((1, d), lambda i: (0, 0)),
                  pl.BlockSpec(w_bf.shape, lambda i: (0, 0)),
                  pl.BlockSpec(wr_bf.shape, lambda i: (0, 0))],
        out_specs=[pl.BlockSpec((tm, d), lambda i: (i, 0)),
                   pl.BlockSpec((tm, d), lambda i: (i, 0)),
                   pl.BlockSpec((LANES, tm), lambda i: (0, i))],
        out_shape=[jax.ShapeDtypeStruct((n, d), F32),
                   jax.ShapeDtypeStruct((n, d), F32),
                   jax.ShapeDtypeStruct((LANES, n), F32)],
        compiler_params=_cp(("arbitrary",)),
        name="outproj",
    )(o_ret, o_nsa, x2, mod3, g.reshape(1, d), w_bf, wr_bf)


def _route_kernel(lt_ref, b_ref, tri_ref, ids_ref, wts_ref, cnt_ref, carry_ref, *, sub):
    @pl.when(pl.program_id(0) == 0)
    def _():
        carry_ref[...] = jnp.zeros_like(carry_ref)

    ng, ne = N_GROUPS, EXP_PER_GROUP
    l = lt_ref[...] + b_ref[:, 0:1]
    tc = l.shape[1]
    ridx = lax.broadcasted_iota(I32, (ng, tc), 0).astype(F32)

    def softmax0(v):
        e = jnp.exp(v - jnp.max(v, axis=0, keepdims=True))
        return e / jnp.sum(e, axis=0, keepdims=True)

    def top1(p):
        top = jnp.max(p, axis=0, keepdims=True)
        idx = jnp.min(jnp.where(p == top, ridx, float(ng)), axis=0, keepdims=True)
        return top, idx

    pg_top, grp = top1(softmax0(l[0:ng]))
    leg = jnp.zeros((ne, tc), F32)
    for g in range(ng):
        leg = jnp.where(grp == float(g), l[ng + g * ne:ng + (g + 1) * ne], leg)
    pe = softmax0(leg)
    p1, i1 = top1(pe)
    p2, i2 = top1(jnp.where(ridx == i1, -1.0, pe))
    den = p1 + p2
    w1 = pg_top * p1 / den
    w2 = pg_top * p2 / den
    e1 = grp * float(ne) + i1
    e2 = grp * float(ne) + i2

    eio = lax.broadcasted_iota(I32, (N_EXPERTS, sub), 0).astype(F32)
    r1 = []
    r2 = []
    carry = carry_ref[:, 0:1]
    for c in range(tc // sub):
        cs = slice(c * sub, (c + 1) * sub)
        oh1 = (eio == e1[:, cs]).astype(F32)
        oh2 = (eio == e2[:, cs]).astype(F32)
        oh = oh1 + oh2
        before = carry + _dot(oh.astype(BF16), tri_ref[...])
        r1.append(jnp.sum(oh1 * before, axis=0, keepdims=True))
        r2.append(jnp.sum(oh2 * before, axis=0, keepdims=True))
        carry = carry + jnp.sum(oh, axis=1, keepdims=True)
    carry_ref[...] = jnp.broadcast_to(carry, carry_ref.shape)
    cnt_ref[...] = jnp.broadcast_to(carry, cnt_ref.shape).astype(I32)
    r1 = jnp.concatenate(r1, axis=1)
    r2 = jnp.concatenate(r2, axis=1)
    zf = jnp.zeros((4, tc), F32)
    ids_ref[...] = jnp.concatenate([e1, e2, r1, r2, zf], axis=0).astype(I32)
    wts_ref[...] = jnp.concatenate([w1, w2, jnp.zeros((6, tc), F32)], axis=0)


def _route(lt, bias_col):
    n = lt.shape[1]
    tc = min(2048, n)
    sub = min(512, tc)
    tri = jnp.asarray(np.triu(np.ones((sub, sub), np.float32), 1), BF16)
    return pl.pallas_call(
        functools.partial(_route_kernel, sub=sub),
        grid=(n // tc,),
        in_specs=[pl.BlockSpec((LANES, tc), lambda i: (0, i)),
                  pl.BlockSpec((LANES, LANES), lambda i: (0, 0)),
                  pl.BlockSpec((sub, sub), lambda i: (0, 0))],
        out_specs=[pl.BlockSpec((8, tc), lambda i: (0, i)),
                   pl.BlockSpec((8, tc), lambda i: (0, i)),
                   pl.BlockSpec((N_EXPERTS, LANES), lambda i: (0, 0))],
        out_shape=[jax.ShapeDtypeStruct((8, n), I32),
                   jax.ShapeDtypeStruct((8, n), F32),
                   jax.ShapeDtypeStruct((N_EXPERTS, LANES), I32)],
        scratch_shapes=[pltpu.VMEM((N_EXPERTS, LANES), F32)],
        compiler_params=_cp(("arbitrary",)),
        name="route",
    )(lt, bias_col, tri)


def _dispatch_kernel(ids_ref, ps_ref, h_ref, xs_in_ref, xs_ref, sem, *, tcd):
    del xs_in_ref
    base = pl.program_id(0) * tcd

    def copy(n, dst):
        return pltpu.make_async_copy(h_ref.at[pl.ds(n, 1)], xs_ref.at[pl.ds(dst, 1)], sem)

    def issue(t, _):
        for k in range(2):
            dst = ps_ref[ids_ref[0, k, t]] + ids_ref[0, 2 + k, t]
            copy(base + t, dst).start()
        return 0

    lax.fori_loop(0, tcd, issue, 0)

    def drain(t, _):
        copy(0, 0).wait()
        copy(0, 0).wait()
        return 0

    lax.fori_loop(0, tcd, drain, 0)


def _dispatch(ids_blk, pad_start, h2, xs_zero, tcd):
    n, d = h2.shape
    return pl.pallas_call(
        functools.partial(_dispatch_kernel, tcd=tcd),
        grid=(n // tcd,),
        in_specs=[pl.BlockSpec((1, 4, tcd), lambda i: (i, 0, 0), memory_space=pltpu.SMEM),
                  pl.BlockSpec(memory_space=pltpu.SMEM),
                  pl.BlockSpec(memory_space=pl.ANY),
                  pl.BlockSpec(memory_space=pl.ANY)],
        out_specs=pl.BlockSpec(memory_space=pl.ANY),
        out_shape=jax.ShapeDtypeStruct(xs_zero.shape, xs_zero.dtype),
        scratch_shapes=[pltpu.SemaphoreType.DMA(())],
        input_output_aliases={3: 0},
        compiler_params=_cp(("arbitrary",)),
        name="dispatch",
    )(ids_blk, pad_start, h2, xs_zero)


def _experts_kernel(be_ref, nu_ref, xs_ref, wg_ref, wu_ref, wd_ref, ys_ref, g_s, u_s, d_s):
    i = pl.program_id(0)
    used = i < nu_ref[0]
    prev = be_ref[jnp.maximum(i - 1, 0)]
    fresh = (i == 0) | (be_ref[i] != prev)

    @pl.when(used & fresh)
    def _():
        g_s[...] = wg_ref[0].astype(BF16)
        u_s[...] = wu_ref[0].astype(BF16)
        d_s[...] = wd_ref[0].astype(BF16)

    @pl.when(used)
    def _():
        x = xs_ref[...].astype(BF16)
        a = _silu(_dot(x, g_s[...])) * _dot(x, u_s[...])
        ys_ref[...] = _dot(a.astype(BF16), d_s[...])

    @pl.when(jnp.logical_not(used))
    def _():
        ys_ref[...] = jnp.zeros_like(ys_ref)


def _experts(blk_exp, n_used, xs, w_gate, w_up, w_down, tm):
    p, d = xs.shape
    de = w_gate.shape[2]
    nb = p // tm
    row = lambda i, be, nu: (jnp.minimum(i, nu[0] - 1), 0)
    wsel = lambda i, be, nu: (be[i], 0, 0)
    grid_spec = pltpu.PrefetchScalarGridSpec(
        num_scalar_prefetch=2,
        grid=(nb,),
        in_specs=[pl.BlockSpec((tm, d), row),
                  pl.BlockSpec((1, d, de), wsel),
                  pl.BlockSpec((1, d, de), wsel),
                  pl.BlockSpec((1, de, d), wsel)],
        out_specs=pl.BlockSpec((tm, d), lambda i, be, nu: (i, 0)),
        scratch_shapes=[pltpu.VMEM((d, de), BF16), pltpu.VMEM((d, de), BF16),
                        pltpu.VMEM((de, d), BF16)],
    )
    return pl.pallas_call(
        _experts_kernel,
        grid_spec=grid_spec,
        out_shape=jax.ShapeDtypeStruct((p, d), F32),
        compiler_params=_cp(("arbitrary",)),
        name="experts",
    )(blk_exp, n_used, xs, w_gate, w_up, w_down)


def _combine_kernel(ids_ref, ps_ref, ys_ref, x1_ref, wt_ref, mod_ref, g_ref, o_ref, buf, sem, *, tc):
    def copy(src, k, t):
        return pltpu.make_async_copy(ys_ref.at[pl.ds(src, 1)], buf.at[k, pl.ds(t, 1)], sem)

    def issue(t, _):
        for k in range(2):
            src = ps_ref[ids_ref[0, k, t]] + ids_ref[0, 2 + k, t]
            copy(src, k, t).start()
        return 0

    lax.fori_loop(0, tc, issue, 0)

    def drain(t, _):
        copy(0, 0, 0).wait()
        copy(0, 1, 0).wait()
        return 0

    lax.fori_loop(0, tc, drain, 0)

    moe = buf[0] * wt_ref[:, 0:1] + buf[1] * wt_ref[:, 1:2]
    x2 = x1_ref[...] + mod_ref[0, 5:6, :] * moe
    o_ref[...] = x2 * lax.rsqrt(jnp.mean(x2 * x2, axis=-1, keepdims=True) + EPS) * g_ref[...]


def _combine(ids_blk, pad_start, ys, x1, wts_t, mod3, final_g, seq, tc):
    n, d = x1.shape
    return pl.pallas_call(
        functools.partial(_combine_kernel, tc=tc),
        grid=(n // tc,),
        in_specs=[pl.BlockSpec((1, 4, tc), lambda i: (i, 0, 0), memory_space=pltpu.SMEM),
                  pl.BlockSpec(memory_space=pltpu.SMEM),
                  pl.BlockSpec(memory_space=pl.ANY),
                  pl.BlockSpec((tc, d), lambda i: (i, 0)),
                  pl.BlockSpec((tc, LANES), lambda i: (i, 0)),
                  pl.BlockSpec((1, 6, d), lambda i: ((i * tc) // seq, 0, 0)),
                  pl.BlockSpec((1, d), lambda i: (0, 0))],
        out_specs=pl.BlockSpec((tc, d), lambda i: (i, 0)),
        out_shape=jax.ShapeDtypeStruct((n, d), F32),
        scratch_shapes=[pltpu.VMEM((2, tc, d), F32), pltpu.SemaphoreType.DMA(())],
        compiler_params=_cp(("arbitrary",)),
        name="combine",
    )(ids_blk, pad_start, ys, x1, wts_t, mod3, final_g.reshape(1, d))


def _token_mixer(x2, mod3, norm1_g, w_in, ret_gn_g, cmp_pos_k, cmp_w1_k, cmp_w2_k,
                 cmp_pos_v, cmp_w1_v, cmp_w2_v, bsz, seq):
    n, d = x2.shape
    w_main = w_in[:, :PROJ_MAIN].astype(BF16)
    w_gate_cols = jnp.pad(w_in[:, PROJ_MAIN:], ((0, 0), (0, LANES - N_GATE_COLS))).astype(BF16)
    proj, gate_logits = _inproj(x2, mod3, norm1_g, w_main, w_gate_cols, seq)

    o_ret = _retention(proj, ret_gn_g, bsz, seq)

    g_ = NSA_KV_GROUPS
    crow = seq // CMP_STRIDE

    def cmp_src(c0):
        a = proj[:, c0 * LANES:(c0 + g_) * LANES].reshape(bsz, seq, g_, NSA_D)
        return a.transpose(0, 2, 1, 3).reshape(bsz * g_ * crow, CMP_STRIDE * NSA_D)

    pe2 = lambda pe: pe.reshape(2, CMP_STRIDE * NSA_D)
    kc, vc = _compress(cmp_src(_C_KC), cmp_src(_C_VC), pe2(cmp_pos_k), pe2(cmp_pos_v),
                       cmp_w1_k, cmp_w1_v, cmp_w2_k, cmp_w2_v, crow)

    gl = gate_logits[:, :N_GATE_COLS].reshape(n, g_, NSA_HPG * 3).transpose(1, 0, 2)
    gates_g = jnp.pad(gl, ((0, 0), (0, 0), (0, LANES - NSA_HPG * 3)))
    o_nsa = _nsa(proj, kc, vc, gates_g, bsz, seq)
    return o_ret, o_nsa


def _moe(h2, lt, x1, mod3, final_g, b_grp, b_exp, w_gate, w_up, w_down, seq, tm):
    n, d = h2.shape
    bias_col = jnp.zeros((LANES,), F32).at[:N_GROUPS].set(b_grp).at[N_GROUPS:N_GROUPS + N_EXPERTS].set(b_exp)
    bias_col = jnp.broadcast_to(bias_col[:, None], (LANES, LANES))
    ids, wts, cnt = _route(lt, bias_col)

    counts = cnt[:, 0]
    padded = (counts + tm - 1) // tm * tm
    pad_end = jnp.cumsum(padded)
    pad_start = (pad_end - padded).astype(I32)
    nb = (2 * n) // tm + N_EXPERTS
    n_used = (pad_end[-1] // tm).astype(I32).reshape(1)
    blk_exp = jnp.minimum(jnp.searchsorted(pad_end, jnp.arange(nb, dtype=I32) * tm, side='right'),
                          N_EXPERTS - 1).astype(I32)
    last_exp = blk_exp[jnp.maximum(n_used[0] - 1, 0)]
    blk_exp = jnp.where(jnp.arange(nb) < n_used[0], blk_exp, last_exp)

    tcd = min(1024, n)
    ids_d = ids[:4].reshape(4, n // tcd, tcd).transpose(1, 0, 2)
    xs = _dispatch(ids_d, pad_start, h2, jnp.zeros((nb * tm, d), F32), tcd)
    ys = _experts(blk_exp, n_used, xs, w_gate, w_up, w_down, tm)

    tc = min(256, n)
    ids_c = ids[:4].reshape(4, n // tc, tc).transpose(1, 0, 2)
    wts_t = jnp.pad(wts[:2].T, ((0, 0), (0, LANES - 2)))
    return _combine(ids_c, pad_start, ys, x1, wts_t, mod3, final_g, seq, tc)


def kernel(x, c, w_ada, b_ada, norm1_g, norm2_g, final_g, w_in, ret_gn_g, cmp_pos_k, cmp_w1_k,
           cmp_w2_k, cmp_pos_v, cmp_w1_v, cmp_w2_v, w_out, w_grp, b_grp, w_exp, b_exp, w_gate,
           w_up, w_down):
    bsz, seq, d = x.shape
    assert w_ada.shape[0] == 1, "single-layer block"
    n = bsz * seq
    x2 = x.reshape(n, d)
    mod3 = _ada(c, w_ada[0], b_ada[0]).reshape(bsz, 6, d)

    o_ret, o_nsa = _token_mixer(x2, mod3, norm1_g[0], w_in[0], ret_gn_g[0], cmp_pos_k[0],
                                cmp_w1_k[0], cmp_w2_k[0], cmp_pos_v[0], cmp_w1_v[0], cmp_w2_v[0],
                                bsz, seq)

    w_route = jnp.concatenate([w_grp[0], w_exp[0]], axis=1)
    w_route = jnp.pad(w_route, ((0, 0), (0, LANES - w_route.shape[1]))).T.astype(BF16)
    x1, h2, lt = _outproj(o_ret, o_nsa, x2, mod3, norm2_g[0], w_out[0].astype(BF16), w_route, seq)

    out = _moe(h2, lt, x1, mod3, final_g, b_grp[0], b_exp[0], w_gate[0], w_up[0], w_down[0],
               seq, 256)
    return out.reshape(bsz, seq, d)
```

```python
import functools
import math

import numpy as np
import jax
import jax.numpy as jnp
from jax import lax
from jax.experimental import pallas as pl
from jax.experimental.pallas import tpu as pltpu

F32 = jnp.float32
BF16 = jnp.bfloat16
I32 = jnp.int32

RET_HEADS = 4
RET_DK = 256
RET_DV = 256
RET_CHUNK = 128
NSA_HEADS = 8
NSA_KV_GROUPS = 2
NSA_HPG = NSA_HEADS // NSA_KV_GROUPS
NSA_D = 128
CMP_BLOCK = 32
CMP_STRIDE = 16
CMP_HIDDEN = 256
SEL_BLOCK = 64
SEL_COUNT = 16
WIN = 512
N_GROUPS = 8
EXP_PER_GROUP = 8
N_EXPERTS = N_GROUPS * EXP_PER_GROUP
D_EXPERT = 512
ROPE_BASE = 10000.0
EPS = 1e-6
NEG = -1e30
FORCE_BONUS = 1e4

LANES = 128
VMEM_LIMIT = 52 * 1024 * 1024

_C_RQ, _C_RK, _C_RV, _C_RG = 0, 8, 16, 24
_C_NQ = 32
_C_KC, _C_VC, _C_KS, _C_VS, _C_KW, _C_VW = 40, 42, 44, 46, 48, 50
_C_GATE = 52
PROJ_MAIN = _C_GATE * LANES
N_GATE_COLS = NSA_HEADS * 3


def _cp(sem, vmem=VMEM_LIMIT):
    return pltpu.CompilerParams(dimension_semantics=sem, vmem_limit_bytes=vmem)


def _silu(v):
    return v * jax.nn.sigmoid(v)


def _dot(a, b):
    return jnp.dot(a, b, preferred_element_type=F32)


def _dot_nt(a, b):
    return lax.dot_general(a, b, (((1,), (1,)), ((), ())), preferred_element_type=F32)


def _dot_tn(a, b):
    return lax.dot_general(a, b, (((0,), (0,)), ((), ())), preferred_element_type=F32)


def _ada_kernel(c_ref, w_ref, b_ref, o_ref):
    ca = _silu(c_ref[...])
    o_ref[...] = jnp.dot(ca, w_ref[...], preferred_element_type=F32,
                         precision=lax.Precision.HIGHEST) + b_ref[...]


def _ada(c, w, b):
    bsz, d = c.shape
    n = w.shape[1]
    tn = min(1024, n)
    return pl.pallas_call(
        _ada_kernel,
        grid=(n // tn,),
        in_specs=[pl.BlockSpec((bsz, d), lambda j: (0, 0)),
                  pl.BlockSpec((d, tn), lambda j: (0, j)),
                  pl.BlockSpec((1, tn), lambda j: (0, j))],
        out_specs=pl.BlockSpec((bsz, tn), lambda j: (0, j)),
        out_shape=jax.ShapeDtypeStruct((bsz, n), F32),
        compiler_params=_cp(("arbitrary",)),
        name="ada",
    )(c, w, b.reshape(1, n))


def _inproj_kernel(x_ref, mod_ref, g_ref, w_ref, wg_ref, proj_ref, gate_ref, h_ref):
    @pl.when(pl.program_id(1) == 0)
    def _():
        x = x_ref[...]
        y = x * lax.rsqrt(jnp.mean(x * x, axis=-1, keepdims=True) + EPS) * g_ref[...]
        h = (y * (1.0 + mod_ref[0, 1:2, :]) + mod_ref[0, 0:1, :]).astype(BF16)
        h_ref[...] = h
        gate_ref[...] = _dot(h, wg_ref[...])

    proj_ref[...] = _dot(h_ref[...], w_ref[...]).astype(BF16)


def _inproj(x2, mod3, g, w_bf, wg_bf, seq):
    n, d = x2.shape
    tm = min(1024, seq)
    tn = 512
    nj = PROJ_MAIN // tn
    return pl.pallas_call(
        _inproj_kernel,
        grid=(n // tm, nj),
        in_specs=[pl.BlockSpec((tm, d), lambda i, j: (i, 0)),
                  pl.BlockSpec((1, 6, d), lambda i, j: ((i * tm) // seq, 0, 0)),
                  pl.BlockSpec((1, d), lambda i, j: (0, 0)),
                  pl.BlockSpec((d, tn), lambda i, j: (0, j)),
                  pl.BlockSpec((d, LANES), lambda i, j: (0, 0))],
        out_specs=[pl.BlockSpec((tm, tn), lambda i, j: (i, j)),
                   pl.BlockSpec((tm, LANES), lambda i, j: (i, 0))],
        out_shape=[jax.ShapeDtypeStruct((n, PROJ_MAIN), BF16),
                   jax.ShapeDtypeStruct((n, LANES), F32)],
        scratch_shapes=[pltpu.VMEM((tm, d), BF16)],
        compiler_params=_cp(("arbitrary", "arbitrary")),
        name="inproj",
    )(x2, mod3, g.reshape(1, d), w_bf, wg_bf)


def _retention_kernel(q_ref, k_ref, v_ref, g_ref, cos_ref, sin_ref, din_ref, zeta_ref,
                      qdec_ref, cdec_ref, gn_ref, o_ref, s_ref):
    @pl.when(pl.program_id(1) == 0)
    def _():
        s_ref[...] = jnp.zeros_like(s_ref)

    cos = cos_ref[...]
    sin = sin_ref[...]
    half = RET_DK // 2

    def rot(a):
        a1, a2 = a[:, :half], a[:, half:]
        return jnp.concatenate([a1 * cos - a2 * sin, a1 * sin + a2 * cos], axis=1)

    for h in range(RET_HEADS):
        qs = slice(h * RET_DK, (h + 1) * RET_DK)
        vs = slice(h * RET_DV, (h + 1) * RET_DV)
        q = rot(q_ref[:, qs].astype(F32))
        k = rot(k_ref[:, qs].astype(F32)) * (RET_DK ** -0.5)
        v = v_ref[:, vs]
        qb = q.astype(BF16)
        kb = k.astype(BF16)
        s = _dot_nt(qb, kb) * din_ref[h]
        inner = _dot(s.astype(BF16), v)
        s_prev = s_ref[h]
        cross = _dot(qb, s_prev.astype(BF16)) * qdec_ref[h]
        kv = _dot_tn((k * zeta_ref[h]).astype(BF16), v)
        s_ref[h] = cdec_ref[h] * s_prev + kv
        o = inner + cross
        mu = jnp.mean(o, axis=-1, keepdims=True)
        oc = o - mu
        var = jnp.mean(oc * oc, axis=-1, keepdims=True)
        o = oc * lax.rsqrt(var + EPS) * gn_ref[:, vs]
        o_ref[:, vs] = (o * _silu(g_ref[:, vs].astype(F32))).astype(BF16)


def _retention(proj, gn_g, bsz, seq):
    n = proj.shape[0]
    c = RET_CHUNK
    nc = seq // c
    hw = RET_HEADS * RET_DK
    half = RET_DK // 2
    pos = jnp.arange(seq, dtype=F32)
    inv = ROPE_BASE ** (-jnp.arange(half, dtype=F32) / half)
    ang = pos[:, None] * inv[None, :]
    cos, sin = jnp.cos(ang), jnp.sin(ang)
    log_gamma = jnp.log1p(-jnp.exp2(-5.0 - jnp.arange(RET_HEADS, dtype=F32)))
    idx = jnp.arange(c, dtype=F32)
    rel = idx[:, None] - idx[None, :]
    decay_in = jnp.where(rel >= 0, jnp.exp(log_gamma[:, None, None] * jnp.maximum(rel, 0.0)), 0.0)
    zeta = jnp.exp(log_gamma[:, None] * (c - 1 - idx)[None, :])[:, :, None]
    q_decay = jnp.exp(log_gamma[:, None] * (idx + 1)[None, :])[:, :, None]
    chunk_decay = jnp.exp(log_gamma * c)[:, None, None]
    row = lambda b, t: (b * nc + t)
    return pl.pallas_call(
        _retention_kernel,
        grid=(bsz, nc),
        in_specs=[pl.BlockSpec((c, hw), lambda b, t: (row(b, t), 0)),
                  pl.BlockSpec((c, hw), lambda b, t: (row(b, t), 1)),
                  pl.BlockSpec((c, hw), lambda b, t: (row(b, t), 2)),
                  pl.BlockSpec((c, hw), lambda b, t: (row(b, t), 3)),
                  pl.BlockSpec((c, half), lambda b, t: (t, 0)),
                  pl.BlockSpec((c, half), lambda b, t: (t, 0)),
                  pl.BlockSpec((RET_HEADS, c, c), lambda b, t: (0, 0, 0)),
                  pl.BlockSpec((RET_HEADS, c, 1), lambda b, t: (0, 0, 0)),
                  pl.BlockSpec((RET_HEADS, c, 1), lambda b, t: (0, 0, 0)),
                  pl.BlockSpec((RET_HEADS, 1, 1), lambda b, t: (0, 0, 0)),
                  pl.BlockSpec((1, hw), lambda b, t: (0, 0))],
        out_specs=pl.BlockSpec((c, hw), lambda b, t: (row(b, t), 0)),
        out_shape=jax.ShapeDtypeStruct((n, hw), BF16),
        scratch_shapes=[pltpu.VMEM((RET_HEADS, RET_DK, RET_DV), F32)],
        compiler_params=_cp(("arbitrary", "arbitrary")),
        name="retention",
    )(proj, proj, proj, proj, cos, sin, decay_in, zeta, q_decay, chunk_decay,
      gn_g.reshape(1, hw))


def _compress_kernel(ak_ref, av_ref, pek_ref, pev_ref, w1k_ref, w1v_ref, w2k_ref, w2v_ref,
                     kc_ref, vc_ref):
    half = CMP_STRIDE * NSA_D

    def one(a_ref, pe_ref, w1_ref, w2_ref, o_ref):
        a = a_ref[...].astype(F32)
        lo = _dot((a + pe_ref[0:1, :]).astype(BF16), w1_ref[:half, :].astype(BF16))
        hi = _dot((a + pe_ref[1:2, :]).astype(BF16), w1_ref[half:, :].astype(BF16))
        rows = hi.shape[0]
        pre = lo + pltpu.roll(hi, rows - 1, 0)
        o_ref[...] = _dot(_silu(pre).astype(BF16), w2_ref[...].astype(BF16)).astype(BF16)

    one(ak_ref, pek_ref, w1k_ref, w2k_ref, kc_ref)
    one(av_ref, pev_ref, w1v_ref, w2v_ref, vc_ref)


def _compress(ak, av, pek, pev, w1k, w1v, w2k, w2v, rows_per):
    r, w = ak.shape
    full = lambda a: pl.BlockSpec(a.shape, lambda i: (0,) * a.ndim)
    blk = pl.BlockSpec((rows_per, w), lambda i: (i, 0))
    oblk = pl.BlockSpec((rows_per, NSA_D), lambda i: (i, 0))
    return pl.pallas_call(
        _compress_kernel,
        grid=(r // rows_per,),
        in_specs=[blk, blk, full(pek), full(pev), full(w1k), full(w1v), full(w2k), full(w2v)],
        out_specs=[oblk, oblk],
        out_shape=[jax.ShapeDtypeStruct((r, NSA_D), BF16)] * 2,
        compiler_params=_cp(("arbitrary",)),
        name="compress",
    )(ak, av, pek, pev, w1k, w1v, w2k, w2v)


def _masked_softmax(s, mask):
    s = jnp.where(mask, s, NEG)
    m = jnp.max(s, axis=-1, keepdims=True)
    e = jnp.where(mask, jnp.exp(s - m), 0.0)
    den = jnp.sum(e, axis=-1, keepdims=True)
    return jnp.where(den > 0.0, e / jnp.where(den > 0.0, den, 1.0), 0.0)


def _nsa_kernel(q_ref, kc_ref, vc_ref, ks_ref, vs_ref, kw_ref, vw_ref, gate_ref, ov_ref,
                o_ref, kaug_ref, *, tq, tk, seq, ncmp, wlen):
    crow = kc_ref.shape[0]
    i = pl.program_id(2)
    hg = NSA_HPG
    d = NSA_D
    r = hg * tq

    @pl.when(i == 0)
    def _():
        kaug_ref[:, :d] = ks_ref[...]
        blk = lax.broadcasted_iota(I32, (seq, LANES), 0) // SEL_BLOCK
        lane = lax.broadcasted_iota(I32, (seq, LANES), 1)
        kaug_ref[:, d:] = (blk == lane).astype(BF16)

    q = q_ref[...]
    qh = [(q[:, h * d:(h + 1) * d].astype(F32) * (d ** -0.5)).astype(BF16) for h in range(hg)]
    qa = jnp.concatenate(qh, axis=0)
    t1 = i * tq + lax.broadcasted_iota(I32, (tq, 1), 0)
    tpos = jnp.concatenate([t1] * hg, axis=0)
    lane = lax.broadcasted_iota(I32, (1, LANES), 1)

    sc = _dot_nt(qa, kc_ref[...])
    clane = lax.broadcasted_iota(I32, (1, crow), 1)
    cmask = (clane * CMP_STRIDE + (CMP_BLOCK - 1) <= tpos) & (clane < ncmp)
    p_cmp = _masked_softmax(sc, cmask)
    o_cmp = _dot(p_cmp.astype(BF16), vc_ref[...])

    psum = p_cmp[0:tq]
    for h in range(1, hg):
        psum = psum + p_cmp[h * tq:(h + 1) * tq]
    imp = jnp.dot(psum, ov_ref[...], preferred_element_type=F32, precision=lax.Precision.HIGHEST)
    cur = t1 // SEL_BLOCK
    valid = lane <= cur
    forced = (lane == 0) | (lane == cur) | (lane == cur - 1)
    score = jnp.where(valid, imp + jnp.where(forced, FORCE_BONUS, 0.0), -1.0)
    nsel_blocks = seq // SEL_BLOCK
    rank = jnp.zeros((tq, LANES), F32)
    for s2 in range(nsel_blocks):
        col = score[:, s2:s2 + 1]
        beats = (col > score) | ((col == score) & (lane > s2))
        rank = rank + beats.astype(F32)
    sel = valid & (rank < float(min(SEL_COUNT, nsel_blocks)))
    selb = jnp.where(sel, 0.0, NEG).astype(BF16)
    q_aug = jnp.concatenate([jnp.concatenate([qh[h], selb], axis=1) for h in range(hg)], axis=0)

    n_tiles = (i * tq + tq + tk - 1) // tk
    kl = lax.broadcasted_iota(I32, (1, tk), 1)

    def body(j, carry):
        m, l, acc = carry
        k0 = pl.multiple_of(j * tk, tk)
        s = _dot_nt(q_aug, kaug_ref[pl.ds(k0, tk), :])
        s = jnp.where(k0 + kl <= tpos, s, NEG)
        m_new = jnp.maximum(m, jnp.max(s, axis=-1, keepdims=True))
        alpha = jnp.exp(m - m_new)
        p = jnp.exp(s - m_new)
        l = alpha * l + jnp.sum(p, axis=-1, keepdims=True)
        acc = alpha * acc + _dot(p.astype(BF16), vs_ref[pl.ds(k0, tk), :])
        return m_new, l, acc

    m0 = jnp.full((r, 1), NEG, F32)
    l0 = jnp.zeros((r, 1), F32)
    a0 = jnp.zeros((r, d), F32)
    _, l_s, acc_s = lax.fori_loop(0, n_tiles, body, (m0, l0, a0))
    o_slc = acc_s / l_s

    w0 = pl.multiple_of(jnp.maximum(i * tq + tq - wlen, 0), tq)
    sw = _dot_nt(qa, kw_ref[pl.ds(w0, wlen), :])
    delta = tpos - (w0 + lax.broadcasted_iota(I32, (1, wlen), 1))
    p_win = _masked_softmax(sw, (delta >= 0) & (delta < WIN))
    o_win = _dot(p_win.astype(BF16), vw_ref[pl.ds(w0, wlen), :])

    gt = jax.nn.sigmoid(gate_ref[0])
    for h in range(hg):
        rows = slice(h * tq, (h + 1) * tq)
        o = (gt[:, 3 * h:3 * h + 1] * o_cmp[rows] + gt[:, 3 * h + 1:3 * h + 2] * o_slc[rows]
             + gt[:, 3 * h + 2:3 * h + 3] * o_win[rows])
        o_ref[:, h * d:(h + 1) * d] = o.astype(BF16)


def _nsa(proj, kc, vc, gates_g, bsz, seq):
    n = proj.shape[0]
    g_ = NSA_KV_GROUPS
    tq = 128
    tk = 256
    nq = seq // tq
    ncmp = (seq - CMP_BLOCK) // CMP_STRIDE + 1
    wlen = min(WIN + tq, seq)
    nsel = seq // SEL_BLOCK
    crow = seq // CMP_STRIDE
    assert nsel <= LANES and crow <= LANES
    cs = np.arange(crow)[:, None] * CMP_STRIDE
    ss = np.arange(LANES)[None, :] * SEL_BLOCK
    ov = ((cs < ss + SEL_BLOCK) & (cs + CMP_BLOCK > ss)
          & (np.arange(crow)[:, None] < ncmp) & (np.arange(LANES)[None, :] < nsel))
    ov = jnp.asarray(ov.astype(np.float32))
    kvspec = lambda c0: pl.BlockSpec((seq, NSA_D), lambda b, g, i: (b, c0 + g))
    kern = functools.partial(_nsa_kernel, tq=tq, tk=tk, seq=seq, ncmp=ncmp, wlen=wlen)
    return pl.pallas_call(
        kern,
        grid=(bsz, g_, nq),
        in_specs=[pl.BlockSpec((tq, NSA_HPG * NSA_D), lambda b, g, i: (b * nq + i, _C_NQ // NSA_HPG + g)),
                  pl.BlockSpec((crow, NSA_D), lambda b, g, i: (b * g_ + g, 0)),
                  pl.BlockSpec((crow, NSA_D), lambda b, g, i: (b * g_ + g, 0)),
                  kvspec(_C_KS), kvspec(_C_VS), kvspec(_C_KW), kvspec(_C_VW),
                  pl.BlockSpec((1, tq, LANES), lambda b, g, i: (g, b * nq + i, 0)),
                  pl.BlockSpec((crow, LANES), lambda b, g, i: (0, 0))],
        out_specs=pl.BlockSpec((tq, NSA_HPG * NSA_D), lambda b, g, i: (b * nq + i, g)),
        out_shape=jax.ShapeDtypeStruct((n, NSA_HEADS * NSA_D), BF16),
        scratch_shapes=[pltpu.VMEM((seq, 2 * NSA_D), BF16)],
        compiler_params=_cp(("arbitrary", "arbitrary", "arbitrary")),
        name="nsa",
    )(proj, kc, vc, proj, proj, proj, proj, gates_g, ov)


def _outproj_kernel(oret_ref, onsa_ref, x_ref, mod_ref, g_ref, w_ref, wr_ref,
                    x1_ref, h2_ref, lt_ref):
    hw = oret_ref.shape[1]
    mix = _dot(oret_ref[...], w_ref[:hw, :]) + _dot(onsa_ref[...], w_ref[hw:, :])
    x1 = x_ref[...] + mod_ref[0, 2:3, :] * mix
    x1_ref[...] = x1
    y = x1 * lax.rsqrt(jnp.mean(x1 * x1, axis=-1, keepdims=True) + EPS) * g_ref[...]
    h2 = y * (1.0 + mod_ref[0, 4:5, :]) + mod_ref[0, 3:4, :]
    h2_ref[...] = h2
    lt_ref[...] = _dot_nt(wr_ref[...], h2.astype(BF16))


def _outproj(o_ret, o_nsa, x2, mod3, g, w_bf, wr_bf, seq):
    n, d = x2.shape
    tm = min(512, seq)
    hw = o_ret.shape[1]
    return pl.pallas_call(
        _outproj_kernel,
        grid=(n // tm,),
        in_specs=[pl.BlockSpec((tm, hw), lambda i: (i, 0)),
                  pl.BlockSpec((tm, o_nsa.shape[1]), lambda i: (i, 0)),
                  pl.BlockSpec((tm, d), lambda i: (i, 0)),
                  pl.BlockSpec((1, 6, d), lambda i: ((i * tm) // seq, 0, 0)),
                  pl.BlockSpec((1, d), lambda i: (0, 0)),
                  pl.BlockSpec(w_bf.shape, lambda i: (0, 0)),
                  pl.BlockSpec(wr_bf.shape, lambda i: (0, 0))],
        out_specs=[pl.BlockSpec((tm, d), lambda i: (i, 0)),
                   pl.BlockSpec((tm, d), lambda i: (i, 0)),
                   pl.BlockSpec((LANES, tm), lambda i: (0, i))],
        out_shape=[jax.ShapeDtypeStruct((n, d), F32),
                   jax.ShapeDtypeStruct((n, d), F32),
                   jax.ShapeDtypeStruct((LANES, n), F32)],
        compiler_params=_cp(("arbitrary",)),
        name="outproj",
    )(o_ret, o_nsa, x2, mod3, g.reshape(1, d), w_bf, wr_bf)


def _route_kernel(lt_ref, b_ref, tri_ref, ids_ref, wts_ref, cnt_ref, carry_ref, *, sub):
    @pl.when(pl.program_id(0) == 0)
    def _():
        carry_ref[...] = jnp.zeros_like(carry_ref)

    ng, ne = N_GROUPS, EXP_PER_GROUP
    l = lt_ref[...] + b_ref[:, 0:1]
    tc = l.shape[1]
    ridx = lax.broadcasted_iota(I32, (ng, tc), 0).astype(F32)

    def softmax0(v):
        e = jnp.exp(v - jnp.max(v, axis=0, keepdims=True))
        return e / jnp.sum(e, axis=0, keepdims=True)

    def top1(p):
        top = jnp.max(p, axis=0, keepdims=True)
        idx = jnp.min(jnp.where(p == top, ridx, float(ng)), axis=0, keepdims=True)
        return top, idx

    pg_top, grp = top1(softmax0(l[0:ng]))
    leg = jnp.zeros((ne, tc), F32)
    for g in range(ng):
        leg = jnp.where(grp == float(g), l[ng + g * ne:ng + (g + 1) * ne], leg)
    pe = softmax0(leg)
    p1, i1 = top1(pe)
    p2, i2 = top1(jnp.where(ridx == i1, -1.0, pe))
    den = p1 + p2
    w1 = pg_top * p1 / den
    w2 = pg_top * p2 / den
    e1 = grp * float(ne) + i1
    e2 = grp * float(ne) + i2

    eio = lax.broadcasted_iota(I32, (N_EXPERTS, sub), 0).astype(F32)
    r1 = []
    r2 = []
    carry = carry_ref[:, 0:1]
    for c in range(tc // sub):
        cs = slice(c * sub, (c + 1) * sub)
        oh1 = (eio == e1[:, cs]).astype(F32)
        oh2 = (eio == e2[:, cs]).astype(F32)
        oh = oh1 + oh2
        before = carry + _dot(oh.astype(BF16), tri_ref[...])
        r1.append(jnp.sum(oh1 * before, axis=0, keepdims=True))
        r2.append(jnp.sum(oh2 * before, axis=0, keepdims=True))
        carry = carry + jnp.sum(oh, axis=1, keepdims=True)
    carry_ref[...] = jnp.broadcast_to(carry, carry_ref.shape)
    cnt_ref[...] = jnp.broadcast_to(carry, cnt_ref.shape).astype(I32)
    r1 = jnp.concatenate(r1, axis=1)
    r2 = jnp.concatenate(r2, axis=1)
    zf = jnp.zeros((4, tc), F32)
    ids_ref[...] = jnp.concatenate([e1, e2, r1, r2, zf], axis=0).astype(I32)
    wts_ref[...] = jnp.concatenate([w1, w2, jnp.zeros((6, tc), F32)], axis=0)


def _route(lt, bias_col):
    n = lt.shape[1]
    tc = min(2048, n)
    sub = min(512, tc)
    tri = jnp.asarray(np.triu(np.ones((sub, sub), np.float32), 1), BF16)
    return pl.pallas_call(
        functools.partial(_route_kernel, sub=sub),
        grid=(n // tc,),
        in_specs=[pl.BlockSpec((LANES, tc), lambda i: (0, i)),
                  pl.BlockSpec((LANES, LANES), lambda i: (0, 0)),
                  pl.BlockSpec((sub, sub), lambda i: (0, 0))],
        out_specs=[pl.BlockSpec((8, tc), lambda i: (0, i)),
                   pl.BlockSpec((8, tc), lambda i: (0, i)),
                   pl.BlockSpec((N_EXPERTS, LANES), lambda i: (0, 0))],
        out_shape=[jax.ShapeDtypeStruct((8, n), I32),
                   jax.ShapeDtypeStruct((8, n), F32),
                   jax.ShapeDtypeStruct((N_EXPERTS, LANES), I32)],
        scratch_shapes=[pltpu.VMEM((N_EXPERTS, LANES), F32)],
        compiler_params=_cp(("arbitrary",)),
        name="route",
    )(lt, bias_col, tri)


def _dispatch_kernel(ids_ref, ps_ref, h_ref, xs_in_ref, xs_ref, sem, *, tcd):
    del xs_in_ref

    def issue(t, _):
        for k in range(2):
            dst = ps_ref[ids_ref[0, k, t]] + ids_ref[0, 2 + k, t]
            pltpu.make_async_copy(h_ref.at[pl.ds(t, 1)], xs_ref.at[pl.ds(dst, 1)], sem).start()
        return 0

    lax.fori_loop(0, tcd, issue, 0, unroll=8)
    for k in range(2):
        pltpu.make_async_copy(h_ref, xs_ref.at[pl.ds(0, tcd)], sem).wait()


def _dispatch(ids_blk, pad_start, h2, xs_zero, tcd):
    n, d = h2.shape
    return pl.pallas_call(
        functools.partial(_dispatch_kernel, tcd=tcd),
        grid=(n // tcd,),
        in_specs=[pl.BlockSpec((1, 4, tcd), lambda i: (i, 0, 0), memory_space=pltpu.SMEM),
                  pl.BlockSpec(memory_space=pltpu.SMEM),
                  pl.BlockSpec((tcd, d), lambda i: (i, 0)),
                  pl.BlockSpec(memory_space=pl.ANY)],
        out_specs=pl.BlockSpec(memory_space=pl.ANY),
        out_shape=jax.ShapeDtypeStruct(xs_zero.shape, xs_zero.dtype),
        scratch_shapes=[pltpu.SemaphoreType.DMA(())],
        input_output_aliases={3: 0},
        compiler_params=_cp(("arbitrary",)),
        name="dispatch",
    )(ids_blk, pad_start, h2, xs_zero)


def _experts_kernel(be_ref, nu_ref, xs_ref, wg_ref, wu_ref, wd_ref, ys_ref, g_s, u_s, d_s):
    i = pl.program_id(0)
    used = i < nu_ref[0]
    prev = be_ref[jnp.maximum(i - 1, 0)]
    fresh = (i == 0) | (be_ref[i] != prev)

    @pl.when(used & fresh)
    def _():
        g_s[...] = wg_ref[0].astype(BF16)
        u_s[...] = wu_ref[0].astype(BF16)
        d_s[...] = wd_ref[0].astype(BF16)

    @pl.when(used)
    def _():
        x = xs_ref[...].astype(BF16)
        a = _silu(_dot(x, g_s[...])) * _dot(x, u_s[...])
        ys_ref[...] = _dot(a.astype(BF16), d_s[...])

    @pl.when(jnp.logical_not(used))
    def _():
        ys_ref[...] = jnp.zeros_like(ys_ref)


def _experts(blk_exp, n_used, xs, w_gate, w_up, w_down, tm):
    p, d = xs.shape
    de = w_gate.shape[2]
    nb = p // tm
    row = lambda i, be, nu: (jnp.minimum(i, nu[0] - 1), 0)
    wsel = lambda i, be, nu: (be[i], 0, 0)
    grid_spec = pltpu.PrefetchScalarGridSpec(
        num_scalar_prefetch=2,
        grid=(nb,),
        in_specs=[pl.BlockSpec((tm, d), row),
                  pl.BlockSpec((1, d, de), wsel),
                  pl.BlockSpec((1, d, de), wsel),
                  pl.BlockSpec((1, de, d), wsel)],
        out_specs=pl.BlockSpec((tm, d), lambda i, be, nu: (i, 0)),
        scratch_shapes=[pltpu.VMEM((d, de), BF16), pltpu.VMEM((d, de), BF16),
                        pltpu.VMEM((de, d), BF16)],
    )
    return pl.pallas_call(
        _experts_kernel,
        grid_spec=grid_spec,
        out_shape=jax.ShapeDtypeStruct((p, d), F32),
        compiler_params=_cp(("arbitrary",)),
        name="experts",
    )(blk_exp, n_used, xs, w_gate, w_up, w_down)


def _combine_kernel(ids_ref, ps_ref, ys_ref, x1_ref, wt_ref, mod_ref, g_ref, o_ref, buf, sem, *, tc):
    def copy(src, k, t):
        return pltpu.make_async_copy(ys_ref.at[pl.ds(src, 1)], buf.at[k, pl.ds(t, 1)], sem)

    def issue(t, _):
        for k in range(2):
            src = ps_ref[ids_ref[0, k, t]] + ids_ref[0, 2 + k, t]
            copy(src, k, t).start()
        return 0

    lax.fori_loop(0, tc, issue, 0, unroll=8)
    for k in range(2):
        pltpu.make_async_copy(ys_ref.at[pl.ds(0, tc)], buf.at[k], sem).wait()

    moe = buf[0] * wt_ref[:, 0:1] + buf[1] * wt_ref[:, 1:2]
    x2 = x1_ref[...] + mod_ref[0, 5:6, :] * moe
    o_ref[...] = x2 * lax.rsqrt(jnp.mean(x2 * x2, axis=-1, keepdims=True) + EPS) * g_ref[...]


def _combine(ids_blk, pad_start, ys, x1, wts_t, mod3, final_g, seq, tc):
    n, d = x1.shape
    return pl.pallas_call(
        functools.partial(_combine_kernel, tc=tc),
        grid=(n // tc,),
        in_specs=[pl.BlockSpec((1, 4, tc), lambda i: (i, 0, 0), memory_space=pltpu.SMEM),
                  pl.BlockSpec(memory_space=pltpu.SMEM),
                  pl.BlockSpec(memory_space=pl.ANY),
                  pl.BlockSpec((tc, d), lambda i: (i, 0)),
                  pl.BlockSpec((tc, LANES), lambda i: (i, 0)),
                  pl.BlockSpec((1, 6, d), lambda i: ((i * tc) // seq, 0, 0)),
                  pl.BlockSpec((1, d), lambda i: (0, 0))],
        out_specs=pl.BlockSpec((tc, d), lambda i: (i, 0)),
        out_shape=jax.ShapeDtypeStruct((n, d), F32),
        scratch_shapes=[pltpu.VMEM((2, tc, d), F32), pltpu.SemaphoreType.DMA(())],
        compiler_params=_cp(("arbitrary",)),
        name="combine",
    )(ids_blk, pad_start, ys, x1, wts_t, mod3, final_g.reshape(1, d))


def _token_mixer(x2, mod3, norm1_g, w_in, ret_gn_g, cmp_pos_k, cmp_w1_k, cmp_w2_k,
                 cmp_pos_v, cmp_w1_v, cmp_w2_v, bsz, seq):
    n, d = x2.shape
    w_main = w_in[:, :PROJ_MAIN].astype(BF16)
    w_gate_cols = jnp.pad(w_in[:, PROJ_MAIN:], ((0, 0), (0, LANES - N_GATE_COLS))).astype(BF16)
    proj, gate_logits = _inproj(x2, mod3, norm1_g, w_main, w_gate_cols, seq)

    o_ret = _retention(proj, ret_gn_g, bsz, seq)

    g_ = NSA_KV_GROUPS
    crow = seq // CMP_STRIDE

    def cmp_src(c0):
        a = proj[:, c0 * LANES:(c0 + g_) * LANES].reshape(bsz, seq, g_, NSA_D)
        return a.transpose(0, 2, 1, 3).reshape(bsz * g_ * crow, CMP_STRIDE * NSA_D)

    pe2 = lambda pe: pe.reshape(2, CMP_STRIDE * NSA_D)
    kc, vc = _compress(cmp_src(_C_KC), cmp_src(_C_VC), pe2(cmp_pos_k), pe2(cmp_pos_v),
                       cmp_w1_k, cmp_w1_v, cmp_w2_k, cmp_w2_v, crow)

    gl = gate_logits[:, :N_GATE_COLS].reshape(n, g_, NSA_HPG * 3).transpose(1, 0, 2)
    gates_g = jnp.pad(gl, ((0, 0), (0, 0), (0, LANES - NSA_HPG * 3)))
    o_nsa = _nsa(proj, kc, vc, gates_g, bsz, seq)
    return o_ret, o_nsa


def _moe(h2, lt, x1, mod3, final_g, b_grp, b_exp, w_gate, w_up, w_down, seq, tm):
    n, d = h2.shape
    bias_col = jnp.zeros((LANES,), F32).at[:N_GROUPS].set(b_grp).at[N_GROUPS:N_GROUPS + N_EXPERTS].set(b_exp)
    bias_col = jnp.broadcast_to(bias_col[:, None], (LANES, LANES))
    ids, wts, cnt = _route(lt, bias_col)

    counts = cnt[:, 0]
    padded = (counts + tm - 1) // tm * tm
    pad_end = jnp.cumsum(padded)
    pad_start = (pad_end - padded).astype(I32)
    nb = (2 * n) // tm + N_EXPERTS
    n_used = (pad_end[-1] // tm).astype(I32).reshape(1)
    blk_start = jnp.arange(nb, dtype=I32) * tm
    blk_exp = jnp.minimum(jnp.sum((pad_end[None, :] <= blk_start[:, None]).astype(I32), axis=1),
                          N_EXPERTS - 1).astype(I32)
    last_exp = blk_exp[jnp.maximum(n_used[0] - 1, 0)]
    blk_exp = jnp.where(jnp.arange(nb) < n_used[0], blk_exp, last_exp)

    tcd = min(512, n)
    ids_d =ids[:4].reshape(4, n // tcd, tcd).transpose(1, 0, 2)
    xs = _dispatch(ids_d, pad_start, h2, jnp.zeros((nb * tm, d), F32), tcd)
    ys = _experts(blk_exp, n_used, xs, w_gate, w_up, w_down, tm)

    tc = min(256, n)
    ids_c = ids[:4].reshape(4, n // tc, tc).transpose(1, 0, 2)
    wts_t = jnp.pad(wts[:2].T, ((0, 0), (0, LANES - 2)))
    return _combine(ids_c, pad_start, ys, x1, wts_t, mod3, final_g, seq, tc)


def kernel(x, c, w_ada, b_ada, norm1_g, norm2_g, final_g, w_in, ret_gn_g, cmp_pos_k, cmp_w1_k,
           cmp_w2_k, cmp_pos_v, cmp_w1_v, cmp_w2_v, w_out, w_grp, b_grp, w_exp, b_exp, w_gate,
           w_up, w_down):
    bsz, seq, d = x.shape
    assert w_ada.shape[0] == 1, "single-layer block"
    n = bsz * seq
    x2 = x.reshape(n, d)
    mod3 = _ada(c, w_ada[0], b_ada[0]).reshape(bsz, 6, d)

    o_ret, o_nsa = _token_mixer(x2, mod3, norm1_g[0], w_in[0], ret_gn_g[0], cmp_pos_k[0],
                                cmp_w1_k[0], cmp_w2_k[0], cmp_pos_v[0], cmp_w1_v[0], cmp_w2_v[0],
                                bsz, seq)

    w_route = jnp.concatenate([w_grp[0], w_exp[0]], axis=1)
    w_route = jnp.pad(w_route, ((0, 0), (0, LANES - w_route.shape[1]))).T.astype(BF16)
    x1, h2, lt = _outproj(o_ret, o_nsa, x2, mod3, norm2_g[0], w_out[0].astype(BF16), w_route, seq)

    out = _moe(h2, lt, x1, mod3, final_g, b_grp[0], b_exp[0], w_gate[0], w_up[0], w_down[0],
               seq, 256)
    return out.reshape(bsz, seq, d)
```

```python
import functools
import math

import numpy as np
import jax
import jax.numpy as jnp
from jax import lax
from jax.experimental import pallas as pl
from jax.experimental.pallas import tpu as pltpu

F32 = jnp.float32
BF16 = jnp.bfloat16
I32 = jnp.int32

RET_HEADS = 4
RET_DK = 256
RET_DV = 256
RET_CHUNK = 128
NSA_HEADS = 8
NSA_KV_GROUPS = 2
NSA_HPG = NSA_HEADS // NSA_KV_GROUPS
NSA_D = 128
CMP_BLOCK = 32
CMP_STRIDE = 16
CMP_HIDDEN = 256
SEL_BLOCK = 64
SEL_COUNT = 16
WIN = 512
N_GROUPS = 8
EXP_PER_GROUP = 8
N_EXPERTS = N_GROUPS * EXP_PER_GROUP
D_EXPERT = 512
ROPE_BASE = 10000.0
EPS = 1e-6
NEG = -1e30
FORCE_BONUS = 1e4

LANES = 128
SUBLANES = 8
VMEM_LIMIT = 52 * 1024 * 1024

_C_RQ, _C_RK, _C_RV, _C_RG = 0, 8, 16, 24
_C_NQ = 32
_C_KC, _C_VC, _C_KS, _C_VS, _C_KW, _C_VW = 40, 42, 44, 46, 48, 50
_C_GATE = 52
PROJ_MAIN = _C_GATE * LANES
N_GATE_COLS = NSA_HEADS * 3


def _cp(sem, vmem=VMEM_LIMIT):
    return pltpu.CompilerParams(dimension_semantics=sem, vmem_limit_bytes=vmem)


def _silu(v):
    return v * jax.nn.sigmoid(v)


def _dot(a, b):
    return jnp.dot(a, b, preferred_element_type=F32)


def _dot_nt(a, b):
    return lax.dot_general(a, b, (((1,), (1,)), ((), ())), preferred_element_type=F32)


def _dot_tn(a, b):
    return lax.dot_general(a, b, (((0,), (0,)), ((), ())), preferred_element_type=F32)


def _ada_kernel(c_ref, w_ref, b_ref, o_ref):
    ca = _silu(c_ref[...])
    o_ref[...] = jnp.dot(ca, w_ref[...], preferred_element_type=F32,
                         precision=lax.Precision.HIGHEST) + b_ref[...]


def _ada(c, w, b):
    bsz, d = c.shape
    n = w.shape[1]
    tn = min(1024, n)
    return pl.pallas_call(
        _ada_kernel,
        grid=(n // tn,),
        in_specs=[pl.BlockSpec((bsz, d), lambda j: (0, 0)),
                  pl.BlockSpec((d, tn), lambda j: (0, j)),
                  pl.BlockSpec((1, tn), lambda j: (0, j))],
        out_specs=pl.BlockSpec((bsz, tn), lambda j: (0, j)),
        out_shape=jax.ShapeDtypeStruct((bsz, n), F32),
        compiler_params=_cp(("arbitrary",)),
        name="ada",
    )(c, w, b.reshape(1, n))


def _inproj_kernel(x_ref, mod_ref, g_ref, w_ref, wg_ref, proj_ref, gate_ref, h_ref):
    @pl.when(pl.program_id(1) == 0)
    def _():
        x = x_ref[...]
        y = x * lax.rsqrt(jnp.mean(x * x, axis=-1, keepdims=True) + EPS) * g_ref[...]
        h = (y * (1.0 + mod_ref[0, 1:2, :]) + mod_ref[0, 0:1, :]).astype(BF16)
        h_ref[...] = h
        gate_ref[...] = _dot(h, wg_ref[...])

    proj_ref[...] = _dot(h_ref[...], w_ref[...]).astype(BF16)


def _inproj(x2, mod3, g, w_bf, wg_bf, seq):
    n, d = x2.shape
    tm = min(1024, seq)
    tn = 512
    nj = PROJ_MAIN // tn
    return pl.pallas_call(
        _inproj_kernel,
        grid=(n // tm, nj),
        in_specs=[pl.BlockSpec((tm, d), lambda i, j: (i, 0)),
                  pl.BlockSpec((1, 6, d), lambda i, j: ((i * tm) // seq, 0, 0)),
                  pl.BlockSpec((1, d), lambda i, j: (0, 0)),
                  pl.BlockSpec((d, tn), lambda i, j: (0, j)),
                  pl.BlockSpec((d, LANES), lambda i, j: (0, 0))],
        out_specs=[pl.BlockSpec((tm, tn), lambda i, j: (i, j)),
                   pl.BlockSpec((tm, LANES), lambda i, j: (i, 0))],
        out_shape=[jax.ShapeDtypeStruct((n, PROJ_MAIN), BF16),
                   jax.ShapeDtypeStruct((n, LANES), F32)],
        scratch_shapes=[pltpu.VMEM((tm, d), BF16)],
        compiler_params=_cp(("arbitrary", "arbitrary")),
        name="inproj",
    )(x2, mod3, g.reshape(1, d), w_bf, wg_bf)


def _retention_kernel(q_ref, k_ref, v_ref, g_ref, cos_ref, sin_ref, din_ref, zeta_ref,
                      qdec_ref, cdec_ref, gn_ref, o_ref, s_ref):
    @pl.when(pl.program_id(1) == 0)
    def _():
        s_ref[...] = jnp.zeros_like(s_ref)

    cos = cos_ref[...]
    sin = sin_ref[...]
    half = RET_DK // 2

    def rot(a):
        a1, a2 = a[:, :half], a[:, half:]
        return jnp.concatenate([a1 * cos - a2 * sin, a1 * sin + a2 * cos], axis=1)

    for h in range(RET_HEADS):
        qs = slice(h * RET_DK, (h + 1) * RET_DK)
        vs = slice(h * RET_DV, (h + 1) * RET_DV)
        q = rot(q_ref[:, qs].astype(F32))
        k = rot(k_ref[:, qs].astype(F32)) * (RET_DK ** -0.5)
        v = v_ref[:, vs]
        qb = q.astype(BF16)
        kb = k.astype(BF16)
        s = _dot_nt(qb, kb) * din_ref[h]
        inner = _dot(s.astype(BF16), v)
        s_prev = s_ref[h]
        cross = _dot(qb, s_prev.astype(BF16)) * qdec_ref[h]
        kv = _dot_tn((k * zeta_ref[h]).astype(BF16), v)
        s_ref[h] = cdec_ref[h] * s_prev + kv
        o = inner + cross
        mu = jnp.mean(o, axis=-1, keepdims=True)
        oc = o - mu
        var = jnp.mean(oc * oc, axis=-1, keepdims=True)
        o = oc * lax.rsqrt(var + EPS) * gn_ref[:, vs]
        o_ref[:, vs] = (o * _silu(g_ref[:, vs].astype(F32))).astype(BF16)


def _retention(proj, gn_g, bsz, seq):
    n = proj.shape[0]
    c = RET_CHUNK
    nc = seq // c
    hw = RET_HEADS * RET_DK
    half = RET_DK // 2
    pos = jnp.arange(seq, dtype=F32)
    inv = ROPE_BASE ** (-jnp.arange(half, dtype=F32) / half)
    ang = pos[:, None] * inv[None, :]
    cos, sin = jnp.cos(ang), jnp.sin(ang)
    log_gamma = jnp.log1p(-jnp.exp2(-5.0 - jnp.arange(RET_HEADS, dtype=F32)))
    idx = jnp.arange(c, dtype=F32)
    rel = idx[:, None] - idx[None, :]
    decay_in = jnp.where(rel >= 0, jnp.exp(log_gamma[:, None, None] * jnp.maximum(rel, 0.0)), 0.0)
    zeta = jnp.exp(log_gamma[:, None] * (c - 1 - idx)[None, :])[:, :, None]
    q_decay = jnp.exp(log_gamma[:, None] * (idx + 1)[None, :])[:, :, None]
    chunk_decay = jnp.exp(log_gamma * c)[:, None, None]
    row = lambda b, t: (b * nc + t)
    return pl.pallas_call(
        _retention_kernel,
        grid=(bsz, nc),
        in_specs=[pl.BlockSpec((c, hw), lambda b, t: (row(b, t), 0)),
                  pl.BlockSpec((c, hw), lambda b, t: (row(b, t), 1)),
                  pl.BlockSpec((c, hw), lambda b, t: (row(b, t), 2)),
                  pl.BlockSpec((c, hw), lambda b, t: (row(b, t), 3)),
                  pl.BlockSpec((c, half), lambda b, t: (t, 0)),
                  pl.BlockSpec((c, half), lambda b, t: (t, 0)),
                  pl.BlockSpec((RET_HEADS, c, c), lambda b, t: (0, 0, 0)),
                  pl.BlockSpec((RET_HEADS, c, 1), lambda b, t: (0, 0, 0)),
                  pl.BlockSpec((RET_HEADS, c, 1), lambda b, t: (0, 0, 0)),
                  pl.BlockSpec((RET_HEADS, 1, 1), lambda b, t: (0, 0, 0)),
                  pl.BlockSpec((1, hw), lambda b, t: (0, 0))],
        out_specs=pl.BlockSpec((c, hw), lambda b, t: (row(b, t), 0)),
        out_shape=jax.ShapeDtypeStruct((n, hw), BF16),
        scratch_shapes=[pltpu.VMEM((RET_HEADS, RET_DK, RET_DV), F32)],
        compiler_params=_cp(("arbitrary", "arbitrary")),
        name="retention",
    )(proj, proj, proj, proj, cos, sin, decay_in, zeta, q_decay, chunk_decay,
      gn_g.reshape(1, hw))


def _compress_kernel(ak_ref, av_ref, pek_ref, pev_ref, w1k_ref, w1v_ref, w2k_ref, w2v_ref,
                     kc_ref, vc_ref):
    half = CMP_STRIDE * NSA_D

    def one(a_ref, pe_ref, w1_ref, w2_ref, o_ref):
        a = a_ref[...].astype(F32)
        lo = _dot((a + pe_ref[0:1, :]).astype(BF16), w1_ref[:half, :].astype(BF16))
        hi = _dot((a + pe_ref[1:2, :]).astype(BF16), w1_ref[half:, :].astype(BF16))
        rows = hi.shape[0]
        pre = lo + pltpu.roll(hi, rows - 1, 0)
        o_ref[...] = _dot(_silu(pre).astype(BF16), w2_ref[...].astype(BF16)).astype(BF16)

    one(ak_ref, pek_ref, w1k_ref, w2k_ref, kc_ref)
    one(av_ref, pev_ref, w1v_ref, w2v_ref, vc_ref)


def _compress(ak, av, pek, pev, w1k, w1v, w2k, w2v, rows_per):
    r, w = ak.shape
    full = lambda a: pl.BlockSpec(a.shape, lambda i: (0,) * a.ndim)
    blk = pl.BlockSpec((rows_per, w), lambda i: (i, 0))
    oblk = pl.BlockSpec((rows_per, NSA_D), lambda i: (i, 0))
    return pl.pallas_call(
        _compress_kernel,
        grid=(r // rows_per,),
        in_specs=[blk, blk, full(pek), full(pev), full(w1k), full(w1v), full(w2k), full(w2v)],
        out_specs=[oblk, oblk],
        out_shape=[jax.ShapeDtypeStruct((r, NSA_D), BF16)] * 2,
        compiler_params=_cp(("arbitrary",)),
        name="compress",
    )(ak, av, pek, pev, w1k, w1v, w2k, w2v)


def _nsa_kernel(q_ref, kc_ref, vc_ref, ks_ref, vs_ref, kw_ref, vw_ref, gate_ref, ovt_ref,
                o_ref, kaug_ref, *, tq, tk, seq, ncmp, wlen):
    crow = kc_ref.shape[0]
    i = pl.program_id(2)
    hg = NSA_HPG
    d = NSA_D
    r = hg * tq

    @pl.when(i == 0)
    def _():
        kaug_ref[:, :d] = ks_ref[...]
        blk = lax.broadcasted_iota(I32, (seq, LANES), 0) // SEL_BLOCK
        lane = lax.broadcasted_iota(I32, (seq, LANES), 1)
        kaug_ref[:, d:] = (blk == lane).astype(BF16)

    q = q_ref[...]
    qh = [(q[:, h * d:(h + 1) * d].astype(F32) * (d ** -0.5)).astype(BF16) for h in range(hg)]
    qa = jnp.concatenate(qh, axis=0)
    t1 = i * tq + lax.broadcasted_iota(I32, (tq, 1), 0)
    tpos = jnp.concatenate([t1] * hg, axis=0)
    t_row = i * tq + lax.broadcasted_iota(I32, (1, tq), 1)
    tpos_row = jnp.concatenate([t_row] * hg, axis=1)

    w0 = pl.multiple_of(jnp.maximum(i * tq + tq - wlen, 0), tq)
    sw = _dot_nt(qa, kw_ref[pl.ds(w0, wlen), :])
    delta = tpos - (w0 + lax.broadcasted_iota(I32, (1, wlen), 1))
    sw = jnp.where((delta >= 0) & (delta < WIN), sw, NEG)
    e_w = jnp.exp(sw - jnp.max(sw, axis=-1, keepdims=True))
    o_win = _dot(e_w.astype(BF16), vw_ref[pl.ds(w0, wlen), :]) / jnp.sum(e_w, axis=-1, keepdims=True)

    sct = _dot_nt(kc_ref[...], qa)
    cidx = lax.broadcasted_iota(I32, (crow, 1), 0)
    cmask = (cidx * CMP_STRIDE + (CMP_BLOCK - 1) <= tpos_row) & (cidx < ncmp)
    sct = jnp.where(cmask, sct, NEG)
    e_c = jnp.where(cmask, jnp.exp(sct - jnp.max(sct, axis=0, keepdims=True)), 0.0)
    den_c = jnp.sum(e_c, axis=0, keepdims=True)
    p_t = jnp.where(den_c > 0.0, e_c / jnp.where(den_c > 0.0, den_c, 1.0), 0.0)
    o_cmp = _dot_tn(p_t.astype(BF16), vc_ref[...])

    psum_t = p_t[:, 0:tq]
    for h in range(1, hg):
        psum_t = psum_t + p_t[:, h * tq:(h + 1) * tq]
    nsel_blocks = seq // SEL_BLOCK
    nsb = ovt_ref.shape[0]
    imp_t = jnp.dot(ovt_ref[...], psum_t, preferred_element_type=F32,
                    precision=lax.Precision.HIGHEST)
    sidx = lax.broadcasted_iota(I32, (nsb, 1), 0)
    cur = t_row // SEL_BLOCK
    valid = sidx <= cur
    forced = (sidx == 0) | (sidx == cur) | (sidx == cur - 1)
    score = jnp.where(valid, imp_t + jnp.where(forced, FORCE_BONUS, 0.0), -1.0)
    rank = jnp.zeros((nsb, tq), F32)
    for s2 in range(nsel_blocks):
        row = score[s2:s2 + 1, :]
        beats = (row > score) | ((row == score) & (sidx > s2))
        rank = rank + beats.astype(F32)
    sel = valid & (rank < float(min(SEL_COUNT, nsel_blocks)))
    selb_t = jnp.where(sel, 0.0, NEG)
    selb_t = jnp.concatenate([selb_t, jnp.zeros((LANES - nsb, tq), F32)], axis=0)
    selb = selb_t.T.astype(BF16)
    q_aug = jnp.concatenate([jnp.concatenate([qh[h], selb], axis=1) for h in range(hg)], axis=0)

    n_full = (i * tq + 1) // tk
    kl = lax.broadcasted_iota(I32, (1, tk), 1)

    def tile(j, carry, masked):
        m, l, acc = carry
        k0 = pl.multiple_of(j * tk, tk)
        s = _dot_nt(q_aug, kaug_ref[pl.ds(k0, tk), :])
        if masked:
            s = jnp.where(k0 + kl <= tpos, s, NEG)
        m_new = jnp.maximum(m, jnp.max(s, axis=-1, keepdims=True))
        alpha = jnp.exp(m - m_new)
        p = jnp.exp(s - m_new)
        l = alpha * l + jnp.sum(p, axis=-1, keepdims=True)
        acc = alpha * acc + _dot(p.astype(BF16), vs_ref[pl.ds(k0, tk), :])
        return m_new, l, acc

    m0 = jnp.full((r, 1), NEG, F32)
    l0 = jnp.zeros((r, 1), F32)
    a0 = jnp.zeros((r, d), F32)
    carry = lax.fori_loop(0, n_full, functools.partial(tile, masked=False), (m0, l0, a0))
    _, l_s, acc_s = tile(n_full, carry, True)
    o_slc = acc_s / l_s

    gt = jax.nn.sigmoid(gate_ref[0])
    for h in range(hg):
        rows = slice(h * tq, (h + 1) * tq)
        o = (gt[:, 3 * h:3 * h + 1] * o_cmp[rows] + gt[:, 3 * h + 1:3 * h + 2] * o_slc[rows]
             + gt[:, 3 * h + 2:3 * h + 3] * o_win[rows])
        o_ref[:, h * d:(h + 1) * d] = o.astype(BF16)


def _nsa(proj, kc, vc, gates_g, bsz, seq):
    n = proj.shape[0]
    g_ = NSA_KV_GROUPS
    tq = 128
    tk = min(512, seq)
    nq = seq // tq
    ncmp = (seq - CMP_BLOCK) // CMP_STRIDE + 1
    wlen = min(WIN + tq, seq)
    nsel = seq // SEL_BLOCK
    crow = seq // CMP_STRIDE
    assert nsel <= LANES and crow <= LANES and tq <= tk and tk % tq == 0
    nsb = -(-nsel // 8) * 8
    ss = np.arange(nsb)[:, None] * SEL_BLOCK
    cs = np.arange(crow)[None, :] * CMP_STRIDE
    ov = ((cs < ss + SEL_BLOCK) & (cs + CMP_BLOCK > ss)
          & (np.arange(crow)[None, :] < ncmp) & (np.arange(nsb)[:, None] < nsel))
    ov = jnp.asarray(ov.astype(np.float32))
    kvspec = lambda c0: pl.BlockSpec((seq, NSA_D), lambda b, g, i: (b, c0 + g))
    kern = functools.partial(_nsa_kernel, tq=tq, tk=tk, seq=seq, ncmp=ncmp, wlen=wlen)
    return pl.pallas_call(
        kern,
        grid=(bsz, g_, nq),
        in_specs=[pl.BlockSpec((tq, NSA_HPG * NSA_D), lambda b, g, i: (b * nq + i, _C_NQ // NSA_HPG + g)),
                  pl.BlockSpec((crow, NSA_D), lambda b, g, i: (b * g_ + g, 0)),
                  pl.BlockSpec((crow, NSA_D), lambda b, g, i: (b * g_ + g, 0)),
                  kvspec(_C_KS), kvspec(_C_VS), kvspec(_C_KW), kvspec(_C_VW),
                  pl.BlockSpec((1, tq, LANES), lambda b, g, i: (g, b * nq + i, 0)),
                  pl.BlockSpec((nsb, crow), lambda b, g, i: (0, 0))],
        out_specs=pl.BlockSpec((tq, NSA_HPG * NSA_D), lambda b, g, i: (b * nq + i, g)),
        out_shape=jax.ShapeDtypeStruct((n, NSA_HEADS * NSA_D), BF16),
        scratch_shapes=[pltpu.VMEM((seq, 2 * NSA_D), BF16)],
        compiler_params=_cp(("arbitrary", "arbitrary", "arbitrary")),
        name="nsa",
    )(proj, kc, vc, proj, proj, proj, proj, gates_g, ov)


def _outproj_kernel(oret_ref, onsa_ref, x_ref, mod_ref, g_ref, w_ref, wr_ref,
                    x1_ref, h2_ref, lt_ref):
    hw = oret_ref.shape[1]
    mix = _dot(oret_ref[...], w_ref[:hw, :]) + _dot(onsa_ref[...], w_ref[hw:, :])
    x1 = x_ref[...] + mod_ref[0, 2:3, :] * mix
    x1_ref[...] = x1
    y = x1 * lax.rsqrt(jnp.mean(x1 * x1, axis=-1, keepdims=True) + EPS) * g_ref[...]
    h2 = y * (1.0 + mod_ref[0, 4:5, :]) + mod_ref[0, 3:4, :]
    h2_ref[...] = h2
    lt_ref[...] = _dot_nt(wr_ref[...], h2.astype(BF16))


def _outproj(o_ret, o_nsa, x2, mod3, g, w_bf, wr_bf, seq):
    n, d = x2.shape
    tm = min(512, seq)
    hw = o_ret.shape[1]
    return pl.pallas_call(
        _outproj_kernel,
        grid=(n // tm,),
        in_specs=[pl.BlockSpec((tm, hw), lambda i: (i, 0)),
                  pl.BlockSpec((tm, o_nsa.shape[1]), lambda i: (i, 0)),
                  pl.BlockSpec((tm, d), lambda i: (i, 0)),
                  pl.BlockSpec((1, 6, d), lambda i: ((i * tm) // seq, 0, 0)),
                  pl.BlockSpec((1, d), lambda i: (0, 0)),
                  pl.BlockSpec(w_bf.shape, lambda i: (0, 0)),
                  pl.BlockSpec(wr_bf.shape, lambda i: (0, 0))],
        out_specs=[pl.BlockSpec((tm, d), lambda i: (i, 0)),
                   pl.BlockSpec((tm, d), lambda i: (i, 0)),
                   pl.BlockSpec((LANES, tm), lambda i: (0, i))],
        out_shape=[jax.ShapeDtypeStruct((n, d), F32),
                   jax.ShapeDtypeStruct((n, d), F32),
                   jax.ShapeDtypeStruct((LANES, n), F32)],
        compiler_params=_cp(("arbitrary",)),
        name="outproj",
    )(o_ret, o_nsa, x2, mod3, g.reshape(1, d), w_bf, wr_bf)


def _route_kernel(lt_ref, b_ref, tri_ref, ids_ref, wts_ref, cnt_ref, carry_ref, *, sub):
    @pl.when(pl.program_id(0) == 0)
    def _():
        carry_ref[...] = jnp.zeros_like(carry_ref)

    ng, ne = N_GROUPS, EXP_PER_GROUP
    l = lt_ref[...] + b_ref[:, 0:1]
    tc = l.shape[1]
    ridx = lax.broadcasted_iota(I32, (ng, tc), 0).astype(F32)

    def softmax0(v):
        e = jnp.exp(v - jnp.max(v, axis=0, keepdims=True))
        return e / jnp.sum(e, axis=0, keepdims=True)

    def top1(p):
        top = jnp.max(p, axis=0, keepdims=True)
        idx = jnp.min(jnp.where(p == top, ridx, float(ng)), axis=0, keepdims=True)
        return top, idx

    pg_top, grp = top1(softmax0(l[0:ng]))
    leg = jnp.zeros((ne, tc), F32)
    for g in range(ng):
        leg = jnp.where(grp == float(g), l[ng + g * ne:ng + (g + 1) * ne], leg)
    pe = softmax0(leg)
    p1, i1 = top1(pe)
    p2, i2 = top1(jnp.where(ridx == i1, -1.0, pe))
    den = p1 + p2
    w1 = pg_top * p1 / den
    w2 = pg_top * p2 / den
    e1 = grp * float(ne) + i1
    e2 = grp * float(ne) + i2

    eio = lax.broadcasted_iota(I32, (N_EXPERTS, sub), 0).astype(F32)
    r1 = []
    r2 = []
    carry = carry_ref[:, 0:1]
    for c in range(tc // sub):
        cs = slice(c * sub, (c + 1) * sub)
        oh1 = (eio == e1[:, cs]).astype(F32)
        oh2 = (eio == e2[:, cs]).astype(F32)
        oh = oh1 + oh2
        before = carry + _dot(oh.astype(BF16), tri_ref[...])
        r1.append(jnp.sum(oh1 * before, axis=0, keepdims=True))
        r2.append(jnp.sum(oh2 * before, axis=0, keepdims=True))
        carry = carry + jnp.sum(oh, axis=1, keepdims=True)
    carry_ref[...] = jnp.broadcast_to(carry, carry_ref.shape)
    cnt_ref[...] = jnp.broadcast_to(carry, cnt_ref.shape).astype(I32)
    r1 = jnp.concatenate(r1, axis=1)
    r2 = jnp.concatenate(r2, axis=1)
    zf = jnp.zeros((4, tc), F32)
    ids_ref[...] = jnp.concatenate([e1, e2, r1, r2, zf], axis=0).astype(I32)
    wts_ref[...] = jnp.concatenate([w1, w2, jnp.zeros((6, tc), F32)], axis=0)


def _route(lt, bias_col):
    n = lt.shape[1]
    tc = min(2048, n)
    sub = min(512, tc)
    tri = jnp.asarray(np.triu(np.ones((sub, sub), np.float32), 1), BF16)
    return pl.pallas_call(
        functools.partial(_route_kernel, sub=sub),
        grid=(n // tc,),
        in_specs=[pl.BlockSpec((LANES, tc), lambda i: (0, i)),
                  pl.BlockSpec((LANES, LANES), lambda i: (0, 0)),
                  pl.BlockSpec((sub, sub), lambda i: (0, 0))],
        out_specs=[pl.BlockSpec((8, tc), lambda i: (0, i)),
                   pl.BlockSpec((8, tc), lambda i: (0, i)),
                   pl.BlockSpec((N_EXPERTS, LANES), lambda i: (0, 0))],
        out_shape=[jax.ShapeDtypeStruct((8, n), I32),
                   jax.ShapeDtypeStruct((8, n), F32),
                   jax.ShapeDtypeStruct((N_EXPERTS, LANES), I32)],
        scratch_shapes=[pltpu.VMEM((N_EXPERTS, LANES), F32)],
        compiler_params=_cp(("arbitrary",)),
        name="route",
    )(lt, bias_col, tri)


def _dispatch_kernel(ids_ref, ps_ref, cnt_ref, h_ref, xs_ref, zbuf, sem, zsem, *, tcd, tm, zr):
    @pl.when(pl.program_id(0) == 0)
    def _():
        zbuf[...] = jnp.zeros_like(zbuf)
        sizes = [zr >> b for b in range(zr.bit_length()) if (zr >> b) >= SUBLANES]

        def zero_rows(row, size):
            return pltpu.make_async_copy(zbuf.at[pl.ds(0, size)],
                                         xs_ref.at[pl.ds(pl.multiple_of(row, SUBLANES), size)], zsem)

        def zero_row(row):
            return pltpu.make_async_copy(zbuf.at[pl.ds(0, 1)], xs_ref.at[pl.ds(row, 1)], zsem)

        def fill(e, wait):
            cnt = cnt_ref[e]
            cnt8 = lax.div(cnt + (SUBLANES - 1), SUBLANES) * SUBLANES
            gap = lax.rem(tm - lax.rem(cnt8, tm), tm)
            base = ps_ref[e]

            def single(c, _):
                cp = zero_row(base + cnt + c)
                cp.wait() if wait else cp.start()
                return 0

            lax.fori_loop(0, cnt8 - cnt, single, 0)
            row = base + cnt8
            for size in sizes:
                has = lax.rem(lax.div(gap, size), 2) == 1

                @pl.when(has)
                def _():
                    cp = zero_rows(row, size)
                    cp.wait() if wait else cp.start()

                row = row + jnp.where(has, size, 0)

        last = N_EXPERTS - 1
        used_rows = ps_ref[last] + lax.div(cnt_ref[last] + (tm - 1), tm) * tm
        n_tail = lax.div(xs_ref.shape[0] - used_rows, zr)

        def tail(c, wait):
            cp = zero_rows(used_rows + c * zr, zr)
            cp.wait() if wait else cp.start()

        for wait in (False, True):
            lax.fori_loop(0, N_EXPERTS, lambda e, _: (fill(e, wait), 0)[1], 0)
            lax.fori_loop(0, n_tail, lambda c, _: (tail(c, wait), 0)[1], 0)

    def issue(t, _):
        for k in range(2):
            dst = ps_ref[ids_ref[0, k, t]] + ids_ref[0, 2 + k, t]
            pltpu.make_async_copy(h_ref.at[pl.ds(t, 1)], xs_ref.at[pl.ds(dst, 1)], sem).start()
        return 0

    lax.fori_loop(0, tcd, issue, 0, unroll=8)
    for k in range(2):
        pltpu.make_async_copy(h_ref, xs_ref.at[pl.ds(0, tcd)], sem).wait()


def _dispatch(ids_blk, pad_start, counts, h2, rows, tcd, tm):
    n, d = h2.shape
    assert tm & (tm - 1) == 0 and tm >= 2 * SUBLANES
    zr = tm // 2
    return pl.pallas_call(
        functools.partial(_dispatch_kernel, tcd=tcd, tm=tm, zr=zr),
        grid=(n // tcd,),
        in_specs=[pl.BlockSpec((1, 4, tcd), lambda i: (i, 0, 0), memory_space=pltpu.SMEM),
                  pl.BlockSpec(memory_space=pltpu.SMEM),
                  pl.BlockSpec(memory_space=pltpu.SMEM),
                  pl.BlockSpec((tcd, d), lambda i: (i, 0))],
        out_specs=pl.BlockSpec(memory_space=pl.ANY),
        out_shape=jax.ShapeDtypeStruct((rows, d), h2.dtype),
        scratch_shapes=[pltpu.VMEM((zr, d), h2.dtype), pltpu.SemaphoreType.DMA(()),
                        pltpu.SemaphoreType.DMA(())],
        compiler_params=_cp(("arbitrary",)),
        name="dispatch",
    )(ids_blk, pad_start, counts, h2)


def _experts_kernel(be_ref, nu_ref, xs_ref, wg_ref, wu_ref, wd_ref, ys_ref, g_s, u_s, d_s):
    i = pl.program_id(0)
    used = i < nu_ref[0]
    prev = be_ref[jnp.maximum(i - 1, 0)]
    fresh = (i == 0) | (be_ref[i] != prev)

    @pl.when(used & fresh)
    def _():
        g_s[...] = wg_ref[0].astype(BF16)
        u_s[...] = wu_ref[0].astype(BF16)
        d_s[...] = wd_ref[0].astype(BF16)

    @pl.when(used)
    def _():
        x = xs_ref[...].astype(BF16)
        a = _silu(_dot(x, g_s[...])) * _dot(x, u_s[...])
        ys_ref[...] = _dot(a.astype(BF16), d_s[...])

    @pl.when(jnp.logical_not(used))
    def _():
        ys_ref[...] = jnp.zeros_like(ys_ref)


def _experts(blk_exp, n_used, xs, w_gate, w_up, w_down, tm):
    d = xs.shape[1]
    de = w_gate.shape[2]
    nb = blk_exp.shape[0]
    p = nb * tm
    row = lambda i, be, nu: (jnp.minimum(i, nu[0] - 1), 0)
    wsel = lambda i, be, nu: (be[i], 0, 0)
    grid_spec = pltpu.PrefetchScalarGridSpec(
        num_scalar_prefetch=2,
        grid=(nb,),
        in_specs=[pl.BlockSpec((tm, d), row),
                  pl.BlockSpec((1, d, de), wsel),
                  pl.BlockSpec((1, d, de), wsel),
                  pl.BlockSpec((1, de, d), wsel)],
        out_specs=pl.BlockSpec((tm, d), lambda i, be, nu: (i, 0)),
        scratch_shapes=[pltpu.VMEM((d, de), BF16), pltpu.VMEM((d, de), BF16),
                        pltpu.VMEM((de, d), BF16)],
    )
    return pl.pallas_call(
        _experts_kernel,
        grid_spec=grid_spec,
        out_shape=jax.ShapeDtypeStruct((p, d), F32),
        compiler_params=_cp(("arbitrary",)),
        name="experts",
    )(blk_exp, n_used, xs, w_gate, w_up, w_down)


def _combine_kernel(ids_ref, idn_ref, ps_ref, ys_ref, x1_ref, wt_ref, mod_ref, g_ref, o_ref,
                    buf, sem, *, tc):
    i = pl.program_id(0)
    slot = lax.rem(i, 2)

    def gather(id_ref, s):
        def issue(t, _):
            for k in range(2):
                src = ps_ref[id_ref[0, k, t]] + id_ref[0, 2 + k, t]
                pltpu.make_async_copy(ys_ref.at[pl.ds(src, 1)], buf.at[s, k, pl.ds(t, 1)],
                                      sem.at[s]).start()
            return 0

        lax.fori_loop(0, tc, issue, 0, unroll=8)

    @pl.when(i == 0)
    def _():
        gather(ids_ref, 0)

    @pl.when(i + 1 < pl.num_programs(0))
    def _():
        gather(idn_ref, 1 - slot)

    for k in range(2):
        pltpu.make_async_copy(ys_ref.at[pl.ds(0, tc)], buf.at[slot, k], sem.at[slot]).wait()

    moe = buf[slot, 0] * wt_ref[:, 0:1] + buf[slot, 1] * wt_ref[:, 1:2]
    x2 = x1_ref[...] + mod_ref[0, 5:6, :] * moe
    o_ref[...] = x2 * lax.rsqrt(jnp.mean(x2 * x2, axis=-1, keepdims=True) + EPS) * g_ref[...]


def _combine(ids_blk, pad_start, ys, x1, wts_t, mod3, final_g, seq, tc):
    n, d = x1.shape
    last = n // tc - 1
    return pl.pallas_call(
        functools.partial(_combine_kernel, tc=tc),
        grid=(n // tc,),
        in_specs=[pl.BlockSpec((1, 4, tc), lambda i: (i, 0, 0), memory_space=pltpu.SMEM),
                  pl.BlockSpec((1, 4, tc), lambda i: (jnp.minimum(i + 1, last), 0, 0),
                               memory_space=pltpu.SMEM),
                  pl.BlockSpec(memory_space=pltpu.SMEM),
                  pl.BlockSpec(memory_space=pl.ANY),
                  pl.BlockSpec((tc, d), lambda i: (i, 0)),
                  pl.BlockSpec((tc, LANES), lambda i: (i, 0)),
                  pl.BlockSpec((1, 6, d), lambda i: ((i * tc) // seq, 0, 0)),
                  pl.BlockSpec((1, d), lambda i: (0, 0))],
        out_specs=pl.BlockSpec((tc, d), lambda i: (i, 0)),
        out_shape=jax.ShapeDtypeStruct((n, d), F32),
        scratch_shapes=[pltpu.VMEM((2, 2, tc, d), F32), pltpu.SemaphoreType.DMA((2,))],
        compiler_params=_cp(("arbitrary",)),
        name="combine",
    )(ids_blk, ids_blk, pad_start, ys, x1, wts_t, mod3, final_g.reshape(1, d))


def _token_mixer(x2, mod3, norm1_g, w_in, ret_gn_g, cmp_pos_k, cmp_w1_k, cmp_w2_k,
                 cmp_pos_v, cmp_w1_v, cmp_w2_v, bsz, seq):
    n, d = x2.shape
    w_main = w_in[:, :PROJ_MAIN].astype(BF16)
    w_gate_cols = jnp.pad(w_in[:, PROJ_MAIN:], ((0, 0), (0, LANES - N_GATE_COLS))).astype(BF16)
    proj, gate_logits = _inproj(x2, mod3, norm1_g, w_main, w_gate_cols, seq)

    o_ret = _retention(proj, ret_gn_g, bsz, seq)

    g_ = NSA_KV_GROUPS
    crow = seq // CMP_STRIDE

    def cmp_src(c0):
        a = proj[:, c0 * LANES:(c0 + g_) * LANES].reshape(bsz, seq, g_, NSA_D)
        return a.transpose(0, 2, 1, 3).reshape(bsz * g_ * crow, CMP_STRIDE * NSA_D)

    pe2 = lambda pe: pe.reshape(2, CMP_STRIDE * NSA_D)
    kc, vc = _compress(cmp_src(_C_KC), cmp_src(_C_VC), pe2(cmp_pos_k), pe2(cmp_pos_v),
                       cmp_w1_k, cmp_w1_v, cmp_w2_k, cmp_w2_v, crow)

    gl = gate_logits[:, :N_GATE_COLS].reshape(n, g_, NSA_HPG * 3).transpose(1, 0, 2)
    gates_g = jnp.pad(gl, ((0, 0), (0, 0), (0, LANES - NSA_HPG * 3)))
    o_nsa = _nsa(proj, kc, vc, gates_g, bsz, seq)
    return o_ret, o_nsa


def _moe(h2, lt, x1, mod3, final_g, b_grp, b_exp, w_gate, w_up, w_down, seq, tm):
    n, d = h2.shape
    bias_col = jnp.zeros((LANES,), F32).at[:N_GROUPS].set(b_grp).at[N_GROUPS:N_GROUPS + N_EXPERTS].set(b_exp)
    bias_col = jnp.broadcast_to(bias_col[:, None], (LANES, LANES))
    ids, wts, cnt = _route(lt, bias_col)

    counts = cnt[:, 0]
    padded = (counts + tm - 1) // tm * tm
    pad_end = jnp.cumsum(padded)
    pad_start = (pad_end - padded).astype(I32)
    nb = (2 * n) // tm + N_EXPERTS
    n_used = (pad_end[-1] // tm).astype(I32).reshape(1)
    blk_start = jnp.arange(nb, dtype=I32) * tm
    blk_exp = jnp.minimum(jnp.sum((pad_end[None, :] <= blk_start[:, None]).astype(I32), axis=1),
                          N_EXPERTS - 1).astype(I32)
    last_exp = blk_exp[jnp.maximum(n_used[0] - 1, 0)]
    blk_exp = jnp.where(jnp.arange(nb) < n_used[0], blk_exp, last_exp)

    tcd = min(512, n)
    ids_d =ids[:4].reshape(4, n // tcd, tcd).transpose(1, 0, 2)
    xs = _dispatch(ids_d, pad_start, counts.astype(I32), h2, nb * tm, tcd, tm)
    ys = _experts(blk_exp, n_used, xs, w_gate, w_up, w_down, tm)

    tc = min(256, n)
    ids_c = ids[:4].reshape(4, n // tc, tc).transpose(1, 0, 2)
    wts_t = jnp.pad(wts[:2].T, ((0, 0), (0, LANES - 2)))
    return _combine(ids_c, pad_start, ys, x1, wts_t, mod3, final_g, seq, tc)


def kernel(x, c, w_ada, b_ada, norm1_g, norm2_g, final_g, w_in, ret_gn_g, cmp_pos_k, cmp_w1_k,
           cmp_w2_k, cmp_pos_v, cmp_w1_v, cmp_w2_v, w_out, w_grp, b_grp, w_exp, b_exp, w_gate,
           w_up, w_down):
    bsz, seq, d = x.shape
    assert w_ada.shape[0] == 1, "single-layer block"
    n = bsz * seq
    x2 = x.reshape(n, d)
    mod3 = _ada(c, w_ada[0], b_ada[0]).reshape(bsz, 6, d)

    o_ret, o_nsa = _token_mixer(x2, mod3, norm1_g[0], w_in[0], ret_gn_g[0], cmp_pos_k[0],
                                cmp_w1_k[0], cmp_w2_k[0], cmp_pos_v[0], cmp_w1_v[0], cmp_w2_v[0],
                                bsz, seq)

    w_route = jnp.concatenate([w_grp[0], w_exp[0]], axis=1)
    w_route = jnp.pad(w_route, ((0, 0), (0, LANES - w_route.shape[1]))).T.astype(BF16)
    x1, h2, lt = _outproj(o_ret, o_nsa, x2, mod3, norm2_g[0], w_out[0].astype(BF16), w_route, seq)

    out = _moe(h2, lt, x1, mod3, final_g, b_grp[0], b_exp[0], w_gate[0], w_up[0], w_down[0],
               seq, 256)
    return out.reshape(bsz, seq, d)
```

```python
import functools
import math

import numpy as np
import jax
import jax.numpy as jnp
from jax import lax
from jax.experimental import pallas as pl
from jax.experimental.pallas import tpu as pltpu

F32 = jnp.float32
BF16 = jnp.bfloat16
I32 = jnp.int32

RET_HEADS = 4
RET_DK = 256
RET_DV = 256
RET_CHUNK = 128
NSA_HEADS = 8
NSA_KV_GROUPS = 2
NSA_HPG = NSA_HEADS // NSA_KV_GROUPS
NSA_D = 128
CMP_BLOCK = 32
CMP_STRIDE = 16
CMP_HIDDEN = 256
SEL_BLOCK = 64
SEL_COUNT = 16
WIN = 512
N_GROUPS = 8
EXP_PER_GROUP = 8
N_EXPERTS = N_GROUPS * EXP_PER_GROUP
D_EXPERT = 512
ROPE_BASE = 10000.0
EPS = 1e-6
NEG = -1e30
FORCE_BONUS = 1e4

LANES = 128
SUBLANES = 8
VMEM_LIMIT = 52 * 1024 * 1024

_C_RQ, _C_RK, _C_RV, _C_RG = 0, 8, 16, 24
_C_NQ = 32
_C_KC, _C_VC, _C_KS, _C_VS, _C_KW, _C_VW = 40, 42, 44, 46, 48, 50
_C_GATE = 52
PROJ_MAIN = _C_GATE * LANES
N_GATE_COLS = NSA_HEADS * 3


def _cp(sem, vmem=VMEM_LIMIT):
    return pltpu.CompilerParams(dimension_semantics=sem, vmem_limit_bytes=vmem)


def _silu(v):
    return v * jax.nn.sigmoid(v)


def _dot(a, b):
    return jnp.dot(a, b, preferred_element_type=F32)


def _dot_nt(a, b):
    return lax.dot_general(a, b, (((1,), (1,)), ((), ())), preferred_element_type=F32)


def _dot_tn(a, b):
    return lax.dot_general(a, b, (((0,), (0,)), ((), ())), preferred_element_type=F32)


def _ada_kernel(c_ref, w_ref, b_ref, o_ref):
    ca = _silu(c_ref[...])
    o_ref[...] = jnp.dot(ca, w_ref[...], preferred_element_type=F32,
                         precision=lax.Precision.HIGHEST) + b_ref[...]


def _ada(c, w, b):
    bsz, d = c.shape
    n = w.shape[1]
    tn = min(1024, n)
    return pl.pallas_call(
        _ada_kernel,
        grid=(n // tn,),
        in_specs=[pl.BlockSpec((bsz, d), lambda j: (0, 0)),
                  pl.BlockSpec((d, tn), lambda j: (0, j)),
                  pl.BlockSpec((1, tn), lambda j: (0, j))],
        out_specs=pl.BlockSpec((bsz, tn), lambda j: (0, j)),
        out_shape=jax.ShapeDtypeStruct((bsz, n), F32),
        compiler_params=_cp(("arbitrary",)),
        name="ada",
    )(c, w, b.reshape(1, n))


def _inproj_kernel(x_ref, mod_ref, g_ref, w_ref, wg_ref, proj_ref, gate_ref, h_ref):
    @pl.when(pl.program_id(1) == 0)
    def _():
        x = x_ref[...]
        y = x * lax.rsqrt(jnp.mean(x * x, axis=-1, keepdims=True) + EPS) * g_ref[...]
        h = (y * (1.0 + mod_ref[0, 1:2, :]) + mod_ref[0, 0:1, :]).astype(BF16)
        h_ref[...] = h
        gate_ref[...] = _dot(h, wg_ref[...])

    proj_ref[...] = _dot(h_ref[...], w_ref[...]).astype(BF16)


def _inproj(x2, mod3, g, w_bf, wg_bf, seq):
    n, d = x2.shape
    tm = min(1024, seq)
    tn = 512
    nj = PROJ_MAIN // tn
    return pl.pallas_call(
        _inproj_kernel,
        grid=(n // tm, nj),
        in_specs=[pl.BlockSpec((tm, d), lambda i, j: (i, 0)),
                  pl.BlockSpec((1, 6, d), lambda i, j: ((i * tm) // seq, 0, 0)),
                  pl.BlockSpec((1, d), lambda i, j: (0, 0)),
                  pl.BlockSpec((d, tn), lambda i, j: (0, j)),
                  pl.BlockSpec((d, LANES), lambda i, j: (0, 0))],
        out_specs=[pl.BlockSpec((tm, tn), lambda i, j: (i, j)),
                   pl.BlockSpec((tm, LANES), lambda i, j: (i, 0))],
        out_shape=[jax.ShapeDtypeStruct((n, PROJ_MAIN), BF16),
                   jax.ShapeDtypeStruct((n, LANES), F32)],
        scratch_shapes=[pltpu.VMEM((tm, d), BF16)],
        compiler_params=_cp(("arbitrary", "arbitrary")),
        name="inproj",
    )(x2, mod3, g.reshape(1, d), w_bf, wg_bf)


def _retention_kernel(q_ref, k_ref, v_ref, g_ref, cos_ref, sin_ref, din_ref, zeta_ref,
                      qdec_ref, cdec_ref, gn_ref, o_ref, s_ref):
    @pl.when(pl.program_id(1) == 0)
    def _():
        s_ref[...] = jnp.zeros_like(s_ref)

    cos = cos_ref[...]
    sin = sin_ref[...]
    half = RET_DK // 2

    def rot(a):
        a1, a2 = a[:, :half], a[:, half:]
        return jnp.concatenate([a1 * cos - a2 * sin, a1 * sin + a2 * cos], axis=1)

    for h in range(RET_HEADS):
        qs = slice(h * RET_DK, (h + 1) * RET_DK)
        vs = slice(h * RET_DV, (h + 1) * RET_DV)
        q = rot(q_ref[:, qs].astype(F32))
        k = rot(k_ref[:, qs].astype(F32)) * (RET_DK ** -0.5)
        v = v_ref[:, vs]
        qb = q.astype(BF16)
        kb = k.astype(BF16)
        s = _dot_nt(qb, kb) * din_ref[h]
        inner = _dot(s.astype(BF16), v)
        s_prev = s_ref[h]
        cross = _dot(qb, s_prev.astype(BF16)) * qdec_ref[h]
        kv = _dot_tn((k * zeta_ref[h]).astype(BF16), v)
        s_ref[h] = cdec_ref[h] * s_prev + kv
        o = inner + cross
        mu = jnp.mean(o, axis=-1, keepdims=True)
        oc = o - mu
        var = jnp.mean(oc * oc, axis=-1, keepdims=True)
        o = oc * lax.rsqrt(var + EPS) * gn_ref[:, vs]
        o_ref[:, vs] = (o * _silu(g_ref[:, vs].astype(F32))).astype(BF16)


def _retention(proj, gn_g, bsz, seq):
    n = proj.shape[0]
    c = RET_CHUNK
    nc = seq // c
    hw = RET_HEADS * RET_DK
    half = RET_DK // 2
    pos = jnp.arange(seq, dtype=F32)
    inv = ROPE_BASE ** (-jnp.arange(half, dtype=F32) / half)
    ang = pos[:, None] * inv[None, :]
    cos, sin = jnp.cos(ang), jnp.sin(ang)
    log_gamma = jnp.log1p(-jnp.exp2(-5.0 - jnp.arange(RET_HEADS, dtype=F32)))
    idx = jnp.arange(c, dtype=F32)
    rel = idx[:, None] - idx[None, :]
    decay_in = jnp.where(rel >= 0, jnp.exp(log_gamma[:, None, None] * jnp.maximum(rel, 0.0)), 0.0)
    zeta = jnp.exp(log_gamma[:, None] * (c - 1 - idx)[None, :])[:, :, None]
    q_decay = jnp.exp(log_gamma[:, None] * (idx + 1)[None, :])[:, :, None]
    chunk_decay = jnp.exp(log_gamma * c)[:, None, None]
    row = lambda b, t: (b * nc + t)
    return pl.pallas_call(
        _retention_kernel,
        grid=(bsz, nc),
        in_specs=[pl.BlockSpec((c, hw), lambda b, t: (row(b, t), 0)),
                  pl.BlockSpec((c, hw), lambda b, t: (row(b, t), 1)),
                  pl.BlockSpec((c, hw), lambda b, t: (row(b, t), 2)),
                  pl.BlockSpec((c, hw), lambda b, t: (row(b, t), 3)),
                  pl.BlockSpec((c, half), lambda b, t: (t, 0)),
                  pl.BlockSpec((c, half), lambda b, t: (t, 0)),
                  pl.BlockSpec((RET_HEADS, c, c), lambda b, t: (0, 0, 0)),
                  pl.BlockSpec((RET_HEADS, c, 1), lambda b, t: (0, 0, 0)),
                  pl.BlockSpec((RET_HEADS, c, 1), lambda b, t: (0, 0, 0)),
                  pl.BlockSpec((RET_HEADS, 1, 1), lambda b, t: (0, 0, 0)),
                  pl.BlockSpec((1, hw), lambda b, t: (0, 0))],
        out_specs=pl.BlockSpec((c, hw), lambda b, t: (row(b, t), 0)),
        out_shape=jax.ShapeDtypeStruct((n, hw), BF16),
        scratch_shapes=[pltpu.VMEM((RET_HEADS, RET_DK, RET_DV), F32)],
        compiler_params=_cp(("arbitrary", "arbitrary")),
        name="retention",
    )(proj, proj, proj, proj, cos, sin, decay_in, zeta, q_decay, chunk_decay,
      gn_g.reshape(1, hw))


def _compress_kernel(ak_ref, av_ref, pek_ref, pev_ref, w1k_ref, w1v_ref, w2k_ref, w2v_ref,
                     kc_ref, vc_ref):
    half = CMP_STRIDE * NSA_D

    def one(a_ref, pe_ref, w1_ref, w2_ref, o_ref):
        a = a_ref[...].astype(F32)
        lo = _dot((a + pe_ref[0:1, :]).astype(BF16), w1_ref[:half, :].astype(BF16))
        hi = _dot((a + pe_ref[1:2, :]).astype(BF16), w1_ref[half:, :].astype(BF16))
        rows = hi.shape[0]
        pre = lo + pltpu.roll(hi, rows - 1, 0)
        o_ref[...] = _dot(_silu(pre).astype(BF16), w2_ref[...].astype(BF16)).astype(BF16)

    one(ak_ref, pek_ref, w1k_ref, w2k_ref, kc_ref)
    one(av_ref, pev_ref, w1v_ref, w2v_ref, vc_ref)


def _compress(ak, av, pek, pev, w1k, w1v, w2k, w2v, rows_per):
    r, w = ak.shape
    full = lambda a: pl.BlockSpec(a.shape, lambda i: (0,) * a.ndim)
    blk = pl.BlockSpec((rows_per, w), lambda i: (i, 0))
    oblk = pl.BlockSpec((rows_per, NSA_D), lambda i: (i, 0))
    return pl.pallas_call(
        _compress_kernel,
        grid=(r // rows_per,),
        in_specs=[blk, blk, full(pek), full(pev), full(w1k), full(w1v), full(w2k), full(w2v)],
        out_specs=[oblk, oblk],
        out_shape=[jax.ShapeDtypeStruct((r, NSA_D), BF16)] * 2,
        compiler_params=_cp(("arbitrary",)),
        name="compress",
    )(ak, av, pek, pev, w1k, w1v, w2k, w2v)


def _nsa_kernel(q_ref, kc_ref, vc_ref, ks_ref, vs_ref, kw_ref, vw_ref, gate_ref, ovt_ref,
                o_ref, kaug_ref, *, tq, tk, seq, ncmp, wlen):
    crow = kc_ref.shape[0]
    i = pl.program_id(2)
    hg = NSA_HPG
    d = NSA_D
    r = hg * tq

    @pl.when(i == 0)
    def _():
        kaug_ref[:, :d] = ks_ref[...]
        blk = lax.broadcasted_iota(I32, (seq, LANES), 0) // SEL_BLOCK
        lane = lax.broadcasted_iota(I32, (seq, LANES), 1)
        kaug_ref[:, d:] = (blk == lane).astype(BF16)

    q = q_ref[...]
    qh = [(q[:, h * d:(h + 1) * d].astype(F32) * (d ** -0.5)).astype(BF16) for h in range(hg)]
    qa = jnp.concatenate(qh, axis=0)
    t1 = i * tq + lax.broadcasted_iota(I32, (tq, 1), 0)
    tpos = jnp.concatenate([t1] * hg, axis=0)
    t_row = i * tq + lax.broadcasted_iota(I32, (1, tq), 1)
    tpos_row = jnp.concatenate([t_row] * hg, axis=1)

    w0 = pl.multiple_of(jnp.maximum(i * tq + tq - wlen, 0), tq)
    sw = _dot_nt(qa, kw_ref[pl.ds(w0, wlen), :])
    delta = tpos - (w0 + lax.broadcasted_iota(I32, (1, wlen), 1))
    sw = jnp.where((delta >= 0) & (delta < WIN), sw, NEG)
    e_w = jnp.exp(sw - jnp.max(sw, axis=-1, keepdims=True))
    o_win = _dot(e_w.astype(BF16), vw_ref[pl.ds(w0, wlen), :]) / jnp.sum(e_w, axis=-1, keepdims=True)

    sct = _dot_nt(kc_ref[...], qa)
    cidx = lax.broadcasted_iota(I32, (crow, 1), 0)
    cmask = (cidx * CMP_STRIDE + (CMP_BLOCK - 1) <= tpos_row) & (cidx < ncmp)
    sct = jnp.where(cmask, sct, NEG)
    e_c = jnp.where(cmask, jnp.exp(sct - jnp.max(sct, axis=0, keepdims=True)), 0.0)
    den_c = jnp.sum(e_c, axis=0, keepdims=True)
    p_t = jnp.where(den_c > 0.0, e_c / jnp.where(den_c > 0.0, den_c, 1.0), 0.0)
    o_cmp = _dot_tn(p_t.astype(BF16), vc_ref[...])

    psum_t = p_t[:, 0:tq]
    for h in range(1, hg):
        psum_t = psum_t + p_t[:, h * tq:(h + 1) * tq]
    nsel_blocks = seq // SEL_BLOCK
    nsb = ovt_ref.shape[0]
    imp_t = jnp.dot(ovt_ref[...], psum_t, preferred_element_type=F32,
                    precision=lax.Precision.HIGHEST)
    sidx = lax.broadcasted_iota(I32, (nsb, 1), 0)
    cur = t_row // SEL_BLOCK
    valid = sidx <= cur
    forced = (sidx == 0) | (sidx == cur) | (sidx == cur - 1)
    score = jnp.where(valid, imp_t + jnp.where(forced, FORCE_BONUS, 0.0), -1.0)
    rank = jnp.zeros((nsb, tq), F32)
    for s2 in range(nsel_blocks):
        row = score[s2:s2 + 1, :]
        beats = (row > score) | ((row == score) & (sidx > s2))
        rank = rank + beats.astype(F32)
    sel = valid & (rank < float(min(SEL_COUNT, nsel_blocks)))
    selb_t = jnp.where(sel, 0.0, NEG)
    selb_t = jnp.concatenate([selb_t, jnp.zeros((LANES - nsb, tq), F32)], axis=0)
    selb = selb_t.T.astype(BF16)
    q_aug = jnp.concatenate([jnp.concatenate([qh[h], selb], axis=1) for h in range(hg)], axis=0)

    n_full = (i * tq + 1) // tk
    kl = lax.broadcasted_iota(I32, (1, tk), 1)

    def tile(j, carry, masked):
        m, l, acc = carry
        k0 = pl.multiple_of(j * tk, tk)
        s = _dot_nt(q_aug, kaug_ref[pl.ds(k0, tk), :])
        if masked:
            s = jnp.where(k0 + kl <= tpos, s, NEG)
        m_new = jnp.maximum(m, jnp.max(s, axis=-1, keepdims=True))
        alpha = jnp.exp(m - m_new)
        p = jnp.exp(s - m_new)
        l = alpha * l + jnp.sum(p, axis=-1, keepdims=True)
        acc = alpha * acc + _dot(p.astype(BF16), vs_ref[pl.ds(k0, tk), :])
        return m_new, l, acc

    m0 = jnp.full((r, 1), NEG, F32)
    l0 = jnp.zeros((r, 1), F32)
    a0 = jnp.zeros((r, d), F32)
    carry = lax.fori_loop(0, n_full, functools.partial(tile, masked=False), (m0, l0, a0))
    _, l_s, acc_s = tile(n_full, carry, True)
    o_slc = acc_s / l_s

    gt = jax.nn.sigmoid(gate_ref[0])
    for h in range(hg):
        rows = slice(h * tq, (h + 1) * tq)
        o = (gt[:, 3 * h:3 * h + 1] * o_cmp[rows] + gt[:, 3 * h + 1:3 * h + 2] * o_slc[rows]
             + gt[:, 3 * h + 2:3 * h + 3] * o_win[rows])
        o_ref[:, h * d:(h + 1) * d] = o.astype(BF16)


def _nsa(proj, kc, vc, gates_g, bsz, seq):
    n = proj.shape[0]
    g_ = NSA_KV_GROUPS
    tq = 128
    tk = min(512, seq)
    nq = seq // tq
    ncmp = (seq - CMP_BLOCK) // CMP_STRIDE + 1
    wlen = min(WIN + tq, seq)
    nsel = seq // SEL_BLOCK
    crow = seq // CMP_STRIDE
    assert nsel <= LANES and crow <= LANES and tq <= tk and tk % tq == 0
    nsb = -(-nsel // 8) * 8
    ss = np.arange(nsb)[:, None] * SEL_BLOCK
    cs = np.arange(crow)[None, :] * CMP_STRIDE
    ov = ((cs < ss + SEL_BLOCK) & (cs + CMP_BLOCK > ss)
          & (np.arange(crow)[None, :] < ncmp) & (np.arange(nsb)[:, None] < nsel))
    ov = jnp.asarray(ov.astype(np.float32))
    kvspec = lambda c0: pl.BlockSpec((seq, NSA_D), lambda b, g, i: (b, c0 + g))
    kern = functools.partial(_nsa_kernel, tq=tq, tk=tk, seq=seq, ncmp=ncmp, wlen=wlen)
    return pl.pallas_call(
        kern,
        grid=(bsz, g_, nq),
        in_specs=[pl.BlockSpec((tq, NSA_HPG * NSA_D), lambda b, g, i: (b * nq + i, _C_NQ // NSA_HPG + g)),
                  pl.BlockSpec((crow, NSA_D), lambda b, g, i: (b * g_ + g, 0)),
                  pl.BlockSpec((crow, NSA_D), lambda b, g, i: (b * g_ + g, 0)),
                  kvspec(_C_KS), kvspec(_C_VS), kvspec(_C_KW), kvspec(_C_VW),
                  pl.BlockSpec((1, tq, LANES), lambda b, g, i: (g, b * nq + i, 0)),
                  pl.BlockSpec((nsb, crow), lambda b, g, i: (0, 0))],
        out_specs=pl.BlockSpec((tq, NSA_HPG * NSA_D), lambda b, g, i: (b * nq + i, g)),
        out_shape=jax.ShapeDtypeStruct((n, NSA_HEADS * NSA_D), BF16),
        scratch_shapes=[pltpu.VMEM((seq, 2 * NSA_D), BF16)],
        compiler_params=_cp(("arbitrary", "arbitrary", "arbitrary")),
        name="nsa",
    )(proj, kc, vc, proj, proj, proj, proj, gates_g, ov)


def _outproj_kernel(oret_ref, onsa_ref, x_ref, mod_ref, g_ref, w_ref, wr_ref,
                    x1_ref, h2_ref, lt_ref):
    hw = oret_ref.shape[1]
    mix = _dot(oret_ref[...], w_ref[:hw, :]) + _dot(onsa_ref[...], w_ref[hw:, :])
    x1 = x_ref[...] + mod_ref[0, 2:3, :] * mix
    x1_ref[...] = x1
    y = x1 * lax.rsqrt(jnp.mean(x1 * x1, axis=-1, keepdims=True) + EPS) * g_ref[...]
    h2 = y * (1.0 + mod_ref[0, 4:5, :]) + mod_ref[0, 3:4, :]
    h2_ref[...] = h2
    lt_ref[...] = _dot_nt(wr_ref[...], h2.astype(BF16))


def _outproj(o_ret, o_nsa, x2, mod3, g, w_bf, wr_bf, seq):
    n, d = x2.shape
    tm = min(512, seq)
    hw = o_ret.shape[1]
    return pl.pallas_call(
        _outproj_kernel,
        grid=(n // tm,),
        in_specs=[pl.BlockSpec((tm, hw), lambda i: (i, 0)),
                  pl.BlockSpec((tm, o_nsa.shape[1]), lambda i: (i, 0)),
                  pl.BlockSpec((tm, d), lambda i: (i, 0)),
                  pl.BlockSpec((1, 6, d), lambda i: ((i * tm) // seq, 0, 0)),
                  pl.BlockSpec((1, d), lambda i: (0, 0)),
                  pl.BlockSpec(w_bf.shape, lambda i: (0, 0)),
                  pl.BlockSpec(wr_bf.shape, lambda i: (0, 0))],
        out_specs=[pl.BlockSpec((tm, d), lambda i: (i, 0)),
                   pl.BlockSpec((tm, d), lambda i: (i, 0)),
                   pl.BlockSpec((LANES, tm), lambda i: (0, i))],
        out_shape=[jax.ShapeDtypeStruct((n, d), F32),
                   jax.ShapeDtypeStruct((n, d), F32),
                   jax.ShapeDtypeStruct((LANES, n), F32)],
        compiler_params=_cp(("arbitrary",)),
        name="outproj",
    )(o_ret, o_nsa, x2, mod3, g.reshape(1, d), w_bf, wr_bf)


def _route_kernel(lt_ref, b_ref, tri_ref, ids_ref, wts_ref, cnt_ref, carry_ref, *, sub):
    @pl.when(pl.program_id(0) == 0)
    def _():
        carry_ref[...] = jnp.zeros_like(carry_ref)

    ng, ne = N_GROUPS, EXP_PER_GROUP
    l = lt_ref[...] + b_ref[:, 0:1]
    tc = l.shape[1]
    ridx = lax.broadcasted_iota(I32, (ng, tc), 0).astype(F32)

    def softmax0(v):
        e = jnp.exp(v - jnp.max(v, axis=0, keepdims=True))
        return e / jnp.sum(e, axis=0, keepdims=True)

    def top1(p):
        top = jnp.max(p, axis=0, keepdims=True)
        idx = jnp.min(jnp.where(p == top, ridx, float(ng)), axis=0, keepdims=True)
        return top, idx

    pg_top, grp = top1(softmax0(l[0:ng]))
    leg = jnp.zeros((ne, tc), F32)
    for g in range(ng):
        leg = jnp.where(grp == float(g), l[ng + g * ne:ng + (g + 1) * ne], leg)
    pe = softmax0(leg)
    p1, i1 = top1(pe)
    p2, i2 = top1(jnp.where(ridx == i1, -1.0, pe))
    den = p1 + p2
    w1 = pg_top * p1 / den
    w2 = pg_top * p2 / den
    e1 = grp * float(ne) + i1
    e2 = grp * float(ne) + i2

    eio = lax.broadcasted_iota(I32, (N_EXPERTS, sub), 0).astype(F32)
    r1 = []
    r2 = []
    carry = carry_ref[:, 0:1]
    for c in range(tc // sub):
        cs = slice(c * sub, (c + 1) * sub)
        oh1 = (eio == e1[:, cs]).astype(F32)
        oh2 = (eio == e2[:, cs]).astype(F32)
        oh = oh1 + oh2
        before = carry + _dot(oh.astype(BF16), tri_ref[...])
        r1.append(jnp.sum(oh1 * before, axis=0, keepdims=True))
        r2.append(jnp.sum(oh2 * before, axis=0, keepdims=True))
        carry = carry + jnp.sum(oh, axis=1, keepdims=True)
    carry_ref[...] = jnp.broadcast_to(carry, carry_ref.shape)
    cnt_ref[...] = jnp.broadcast_to(carry, cnt_ref.shape).astype(I32)
    r1 = jnp.concatenate(r1, axis=1)
    r2 = jnp.concatenate(r2, axis=1)
    zf = jnp.zeros((4, tc), F32)
    ids_ref[...] = jnp.concatenate([e1, e2, r1, r2, zf], axis=0).astype(I32)
    wts_ref[...] = jnp.concatenate([w1, w2, jnp.zeros((6, tc), F32)], axis=0)


def _route(lt, bias_col):
    n = lt.shape[1]
    tc = min(2048, n)
    sub = min(512, tc)
    tri = jnp.asarray(np.triu(np.ones((sub, sub), np.float32), 1), BF16)
    return pl.pallas_call(
        functools.partial(_route_kernel, sub=sub),
        grid=(n // tc,),
        in_specs=[pl.BlockSpec((LANES, tc), lambda i: (0, i)),
                  pl.BlockSpec((LANES, LANES), lambda i: (0, 0)),
                  pl.BlockSpec((sub, sub), lambda i: (0, 0))],
        out_specs=[pl.BlockSpec((8, tc), lambda i: (0, i)),
                   pl.BlockSpec((8, tc), lambda i: (0, i)),
                   pl.BlockSpec((N_EXPERTS, LANES), lambda i: (0, 0))],
        out_shape=[jax.ShapeDtypeStruct((8, n), I32),
                   jax.ShapeDtypeStruct((8, n), F32),
                   jax.ShapeDtypeStruct((N_EXPERTS, LANES), I32)],
        scratch_shapes=[pltpu.VMEM((N_EXPERTS, LANES), F32)],
        compiler_params=_cp(("arbitrary",)),
        name="route",
    )(lt, bias_col, tri)


def _dispatch_kernel(ids_ref, ps_ref, cnt_ref, h_ref, xs_ref, zbuf, sem, zsem, *, tcd, tm, zr):
    @pl.when(pl.program_id(0) == 0)
    def _():
        zbuf[...] = jnp.zeros_like(zbuf)
        sizes = [zr >> b for b in range(zr.bit_length()) if (zr >> b) >= SUBLANES]

        def zero_rows(row, size):
            return pltpu.make_async_copy(zbuf.at[pl.ds(0, size)],
                                         xs_ref.at[pl.ds(pl.multiple_of(row, SUBLANES), size)], zsem)

        def zero_row(row):
            return pltpu.make_async_copy(zbuf.at[pl.ds(0, 1)], xs_ref.at[pl.ds(row, 1)], zsem)

        def fill(e, wait):
            cnt = cnt_ref[e]
            cnt8 = lax.div(cnt + (SUBLANES - 1), SUBLANES) * SUBLANES
            gap = lax.rem(tm - lax.rem(cnt8, tm), tm)
            base = ps_ref[e]

            def single(c, _):
                cp = zero_row(base + cnt + c)
                cp.wait() if wait else cp.start()
                return 0

            lax.fori_loop(0, cnt8 - cnt, single, 0)
            row = base + cnt8
            for size in sizes:
                has = lax.rem(lax.div(gap, size), 2) == 1

                @pl.when(has)
                def _():
                    cp = zero_rows(row, size)
                    cp.wait() if wait else cp.start()

                row = row + jnp.where(has, size, 0)

        last = N_EXPERTS - 1
        used_rows = ps_ref[last] + lax.div(cnt_ref[last] + (tm - 1), tm) * tm
        n_tail = lax.div(xs_ref.shape[0] - used_rows, zr)

        def tail(c, wait):
            cp = zero_rows(used_rows + c * zr, zr)
            cp.wait() if wait else cp.start()

        for wait in (False, True):
            lax.fori_loop(0, N_EXPERTS, lambda e, _: (fill(e, wait), 0)[1], 0)
            lax.fori_loop(0, n_tail, lambda c, _: (tail(c, wait), 0)[1], 0)

    def issue(t, _):
        for k in range(2):
            dst = ps_ref[ids_ref[0, k, t]] + ids_ref[0, 2 + k, t]
            pltpu.make_async_copy(h_ref.at[pl.ds(t, 1)], xs_ref.at[pl.ds(dst, 1)], sem).start()
        return 0

    lax.fori_loop(0, tcd, issue, 0, unroll=8)
    for k in range(2):
        pltpu.make_async_copy(h_ref, xs_ref.at[pl.ds(0, tcd)], sem).wait()


def _dispatch(ids_blk, pad_start, counts, h2, rows, tcd, tm):
    n, d = h2.shape
    assert tm & (tm - 1) == 0 and tm >= 2 * SUBLANES
    zr = tm // 2
    return pl.pallas_call(
        functools.partial(_dispatch_kernel, tcd=tcd, tm=tm, zr=zr),
        grid=(n // tcd,),
        in_specs=[pl.BlockSpec((1, 4, tcd), lambda i: (i, 0, 0), memory_space=pltpu.SMEM),
                  pl.BlockSpec(memory_space=pltpu.SMEM),
                  pl.BlockSpec(memory_space=pltpu.SMEM),
                  pl.BlockSpec((tcd, d), lambda i: (i, 0))],
        out_specs=pl.BlockSpec(memory_space=pl.ANY),
        out_shape=jax.ShapeDtypeStruct((rows, d), h2.dtype),
        scratch_shapes=[pltpu.VMEM((zr, d), h2.dtype), pltpu.SemaphoreType.DMA(()),
                        pltpu.SemaphoreType.DMA(())],
        compiler_params=_cp(("arbitrary",)),
        name="dispatch",
    )(ids_blk, pad_start, counts, h2)


def _experts_kernel(be_ref, ord_ref, nxt_ref, nu_ref, xs_ref, wg_hbm, wu_hbm, wd_hbm, ys_ref,
                    g_f, u_f, d_f, g_s, u_s, d_s, sem):
    i = pl.program_id(0)
    used = i < nu_ref[0]
    e = be_ref[i]
    fresh = (i == 0) | (e != be_ref[jnp.maximum(i - 1, 0)])
    slot = lax.rem(ord_ref[i], 2)

    def fetch(expert, s):
        return [pltpu.make_async_copy(w.at[expert], f.at[s], sem.at[s, k])
                for k, (w, f) in enumerate(((wg_hbm, g_f), (wu_hbm, u_f), (wd_hbm, d_f)))]

    @pl.when(used & (i == 0))
    def _():
        for c in fetch(e, slot):
            c.start()

    @pl.when(used & fresh)
    def _():
        for c in fetch(e, slot):
            c.wait()
        nxt = nxt_ref[i]

        @pl.when(nxt >= 0)
        def _():
            for c in fetch(nxt, 1 - slot):
                c.start()

        g_s[...] = g_f[slot].astype(BF16)
        u_s[...] = u_f[slot].astype(BF16)
        d_s[...] = d_f[slot].astype(BF16)

    @pl.when(used)
    def _():
        x = xs_ref[...].astype(BF16)
        a = _silu(_dot(x, g_s[...])) * _dot(x, u_s[...])
        ys_ref[...] = _dot(a.astype(BF16), d_s[...])

    @pl.when(jnp.logical_not(used))
    def _():
        ys_ref[...] = jnp.zeros_like(ys_ref)


def _experts(blk_exp, blk_ord, blk_nxt, n_used, xs, w_gate, w_up, w_down, tm):
    d = xs.shape[1]
    de = w_gate.shape[2]
    nb = blk_exp.shape[0]
    row = lambda i, be, od, nx, nu: (jnp.minimum(i, nu[0] - 1), 0)
    anyspec = pl.BlockSpec(memory_space=pl.ANY)
    grid_spec = pltpu.PrefetchScalarGridSpec(
        num_scalar_prefetch=4,
        grid=(nb,),
        in_specs=[pl.BlockSpec((tm, d), row), anyspec, anyspec, anyspec],
        out_specs=pl.BlockSpec((tm, d), lambda i, be, od, nx, nu: (i, 0)),
        scratch_shapes=[pltpu.VMEM((2, d, de), F32), pltpu.VMEM((2, d, de), F32),
                        pltpu.VMEM((2, de, d), F32),
                        pltpu.VMEM((d, de), BF16), pltpu.VMEM((d, de), BF16),
                        pltpu.VMEM((de, d), BF16),
                        pltpu.SemaphoreType.DMA((2, 3))],
    )
    return pl.pallas_call(
        _experts_kernel,
        grid_spec=grid_spec,
        out_shape=jax.ShapeDtypeStruct((nb * tm, d), F32),
        compiler_params=_cp(("arbitrary",)),
        name="experts",
    )(blk_exp, blk_ord, blk_nxt, n_used, xs, w_gate, w_up, w_down)


def _combine_kernel(ids_ref, idn_ref, ps_ref, ys_ref, x1_ref, wt_ref, mod_ref, g_ref, o_ref,
                    buf, sem, *, tc):
    i = pl.program_id(0)
    slot = lax.rem(i, 2)

    def gather(id_ref, s):
        def issue(t, _):
            for k in range(2):
                src = ps_ref[id_ref[0, k, t]] + id_ref[0, 2 + k, t]
                pltpu.make_async_copy(ys_ref.at[pl.ds(src, 1)], buf.at[s, k, pl.ds(t, 1)],
                                      sem.at[s]).start()
            return 0

        lax.fori_loop(0, tc, issue, 0, unroll=8)

    @pl.when(i == 0)
    def _():
        gather(ids_ref, 0)

    @pl.when(i + 1 < pl.num_programs(0))
    def _():
        gather(idn_ref, 1 - slot)

    for k in range(2):
        pltpu.make_async_copy(ys_ref.at[pl.ds(0, tc)], buf.at[slot, k], sem.at[slot]).wait()

    moe = buf[slot, 0] * wt_ref[:, 0:1] + buf[slot, 1] * wt_ref[:, 1:2]
    x2 = x1_ref[...] + mod_ref[0, 5:6, :] * moe
    o_ref[...] = x2 * lax.rsqrt(jnp.mean(x2 * x2, axis=-1, keepdims=True) + EPS) * g_ref[...]


def _combine(ids_blk, pad_start, ys, x1, wts_t, mod3, final_g, seq, tc):
    n, d = x1.shape
    last = n // tc - 1
    return pl.pallas_call(
        functools.partial(_combine_kernel, tc=tc),
        grid=(n // tc,),
        in_specs=[pl.BlockSpec((1, 4, tc), lambda i: (i, 0, 0), memory_space=pltpu.SMEM),
                  pl.BlockSpec((1, 4, tc), lambda i: (jnp.minimum(i + 1, last), 0, 0),
                               memory_space=pltpu.SMEM),
                  pl.BlockSpec(memory_space=pltpu.SMEM),
                  pl.BlockSpec(memory_space=pl.ANY),
                  pl.BlockSpec((tc, d), lambda i: (i, 0)),
                  pl.BlockSpec((tc, LANES), lambda i: (i, 0)),
                  pl.BlockSpec((1, 6, d), lambda i: ((i * tc) // seq, 0, 0)),
                  pl.BlockSpec((1, d), lambda i: (0, 0))],
        out_specs=pl.BlockSpec((tc, d), lambda i: (i, 0)),
        out_shape=jax.ShapeDtypeStruct((n, d), F32),
        scratch_shapes=[pltpu.VMEM((2, 2, tc, d), F32), pltpu.SemaphoreType.DMA((2,))],
        compiler_params=_cp(("arbitrary",)),
        name="combine",
    )(ids_blk, ids_blk, pad_start, ys, x1, wts_t, mod3, final_g.reshape(1, d))


def _token_mixer(x2, mod3, norm1_g, w_in, ret_gn_g, cmp_pos_k, cmp_w1_k, cmp_w2_k,
                 cmp_pos_v, cmp_w1_v, cmp_w2_v, bsz, seq):
    n, d = x2.shape
    w_main = w_in[:, :PROJ_MAIN].astype(BF16)
    w_gate_cols = jnp.pad(w_in[:, PROJ_MAIN:], ((0, 0), (0, LANES - N_GATE_COLS))).astype(BF16)
    proj, gate_logits = _inproj(x2, mod3, norm1_g, w_main, w_gate_cols, seq)

    o_ret = _retention(proj, ret_gn_g, bsz, seq)

    g_ = NSA_KV_GROUPS
    crow = seq // CMP_STRIDE

    def cmp_src(c0):
        a = proj[:, c0 * LANES:(c0 + g_) * LANES].reshape(bsz, seq, g_, NSA_D)
        return a.transpose(0, 2, 1, 3).reshape(bsz * g_ * crow, CMP_STRIDE * NSA_D)

    pe2 = lambda pe: pe.reshape(2, CMP_STRIDE * NSA_D)
    kc, vc = _compress(cmp_src(_C_KC), cmp_src(_C_VC), pe2(cmp_pos_k), pe2(cmp_pos_v),
                       cmp_w1_k, cmp_w1_v, cmp_w2_k, cmp_w2_v, crow)

    gl = gate_logits[:, :N_GATE_COLS].reshape(n, g_, NSA_HPG * 3).transpose(1, 0, 2)
    gates_g = jnp.pad(gl, ((0, 0), (0, 0), (0, LANES - NSA_HPG * 3)))
    o_nsa = _nsa(proj, kc, vc, gates_g, bsz, seq)
    return o_ret, o_nsa


def _moe(h2, lt, x1, mod3, final_g, b_grp, b_exp, w_gate, w_up, w_down, seq, tm):
    n, d = h2.shape
    bias_col = jnp.zeros((LANES,), F32).at[:N_GROUPS].set(b_grp).at[N_GROUPS:N_GROUPS + N_EXPERTS].set(b_exp)
    bias_col = jnp.broadcast_to(bias_col[:, None], (LANES, LANES))
    ids, wts, cnt = _route(lt, bias_col)

    counts = cnt[:, 0]
    padded = (counts + tm - 1) // tm * tm
    pad_end = jnp.cumsum(padded)
    pad_start = (pad_end - padded).astype(I32)
    nb = (2 * n) // tm + N_EXPERTS
    n_used = (pad_end[-1] // tm).astype(I32).reshape(1)
    blk_start = jnp.arange(nb, dtype=I32) * tm
    blk_exp = jnp.minimum(jnp.sum((pad_end[None, :] <= blk_start[:, None]).astype(I32), axis=1),
                          N_EXPERTS - 1).astype(I32)
    last_exp = blk_exp[jnp.maximum(n_used[0] - 1, 0)]
    blk_exp = jnp.where(jnp.arange(nb) < n_used[0], blk_exp, last_exp)
    eid = jnp.arange(N_EXPERTS, dtype=I32)
    has = counts > 0
    exp_ord = jnp.sum((has[None, :] & (eid[None, :] < eid[:, None])).astype(I32), axis=1)
    exp_nxt = jnp.min(jnp.where(has[None, :] & (eid[None, :] > eid[:, None]), eid[None, :], N_EXPERTS), axis=1)
    exp_nxt = jnp.where(exp_nxt < N_EXPERTS, exp_nxt, -1).astype(I32)
    blk_ord = exp_ord[blk_exp]
    blk_nxt = exp_nxt[blk_exp]

    tcd = min(512, n)
    ids_d =ids[:4].reshape(4, n // tcd, tcd).transpose(1, 0, 2)
    xs = _dispatch(ids_d, pad_start, counts.astype(I32), h2, nb * tm, tcd, tm)
    ys = _experts(blk_exp, blk_ord, blk_nxt, n_used, xs, w_gate, w_up, w_down, tm)

    tc = min(256, n)
    ids_c = ids[:4].reshape(4, n // tc, tc).transpose(1, 0, 2)
    wts_t = jnp.pad(wts[:2].T, ((0, 0), (0, LANES - 2)))
    return _combine(ids_c, pad_start, ys, x1, wts_t, mod3, final_g, seq, tc)


def kernel(x, c, w_ada, b_ada, norm1_g, norm2_g, final_g, w_in, ret_gn_g, cmp_pos_k, cmp_w1_k,
           cmp_w2_k, cmp_pos_v, cmp_w1_v, cmp_w2_v, w_out, w_grp, b_grp, w_exp, b_exp, w_gate,
           w_up, w_down):
    bsz, seq, d = x.shape
    assert w_ada.shape[0] == 1, "single-layer block"
    n = bsz * seq
    x2 = x.reshape(n, d)
    mod3 = _ada(c, w_ada[0], b_ada[0]).reshape(bsz, 6, d)

    o_ret, o_nsa = _token_mixer(x2, mod3, norm1_g[0], w_in[0], ret_gn_g[0], cmp_pos_k[0],
                                cmp_w1_k[0], cmp_w2_k[0], cmp_pos_v[0], cmp_w1_v[0], cmp_w2_v[0],
                                bsz, seq)

    w_route = jnp.concatenate([w_grp[0], w_exp[0]], axis=1)
    w_route = jnp.pad(w_route, ((0, 0), (0, LANES - w_route.shape[1]))).T.astype(BF16)
    x1, h2, lt = _outproj(o_ret, o_nsa, x2, mod3, norm2_g[0], w_out[0].astype(BF16), w_route, seq)

    out = _moe(h2, lt, x1, mod3, final_g, b_grp[0], b_exp[0], w_gate[0], w_up[0], w_down[0],
               seq, 256)
    return out.reshape(bsz, seq, d)
```

```python
import functools
import math

import numpy as np
import jax
import jax.numpy as jnp
from jax import lax
from jax.experimental import pallas as pl
from jax.experimental.pallas import tpu as pltpu

F32 = jnp.float32
BF16 = jnp.bfloat16
I32 = jnp.int32

RET_HEADS = 4
RET_DK = 256
RET_DV = 256
RET_CHUNK = 128
NSA_HEADS = 8
NSA_KV_GROUPS = 2
NSA_HPG = NSA_HEADS // NSA_KV_GROUPS
NSA_D = 128
CMP_BLOCK = 32
CMP_STRIDE = 16
CMP_HIDDEN = 256
SEL_BLOCK = 64
SEL_COUNT = 16
WIN = 512
N_GROUPS = 8
EXP_PER_GROUP = 8
N_EXPERTS = N_GROUPS * EXP_PER_GROUP
D_EXPERT = 512
ROPE_BASE = 10000.0
EPS = 1e-6
NEG = -1e30
FORCE_BONUS = 1e4

LANES = 128
SUBLANES = 8
VMEM_LIMIT = 52 * 1024 * 1024

_C_RQ, _C_RK, _C_RV, _C_RG = 0, 8, 16, 24
_C_NQ = 32
_C_KC, _C_VC, _C_KS, _C_VS, _C_KW, _C_VW = 40, 42, 44, 46, 48, 50
_C_GATE = 52
PROJ_MAIN = _C_GATE * LANES
N_GATE_COLS = NSA_HEADS * 3


def _cp(sem, vmem=VMEM_LIMIT):
    return pltpu.CompilerParams(dimension_semantics=sem, vmem_limit_bytes=vmem)


def _silu(v):
    return v * jax.nn.sigmoid(v)


def _dot(a, b):
    return jnp.dot(a, b, preferred_element_type=F32)


def _dot_nt(a, b):
    return lax.dot_general(a, b, (((1,), (1,)), ((), ())), preferred_element_type=F32)


def _dot_tn(a, b):
    return lax.dot_general(a, b, (((0,), (0,)), ((), ())), preferred_element_type=F32)


def _ada_kernel(c_ref, w_ref, b_ref, o_ref):
    ca = _silu(c_ref[...])
    o_ref[...] = jnp.dot(ca, w_ref[...], preferred_element_type=F32,
                         precision=lax.Precision.HIGHEST) + b_ref[...]


def _ada(c, w, b):
    bsz, d = c.shape
    n = w.shape[1]
    tn = min(1024, n)
    return pl.pallas_call(
        _ada_kernel,
        grid=(n // tn,),
        in_specs=[pl.BlockSpec((bsz, d), lambda j: (0, 0)),
                  pl.BlockSpec((d, tn), lambda j: (0, j)),
                  pl.BlockSpec((1, tn), lambda j: (0, j))],
        out_specs=pl.BlockSpec((bsz, tn), lambda j: (0, j)),
        out_shape=jax.ShapeDtypeStruct((bsz, n), F32),
        compiler_params=_cp(("arbitrary",)),
        name="ada",
    )(c, w, b.reshape(1, n))


def _inproj_kernel(x_ref, mod_ref, g_ref, w_ref, wg_ref, proj_ref, gate_ref, h_ref):
    @pl.when(pl.program_id(1) == 0)
    def _():
        x = x_ref[...]
        y = x * lax.rsqrt(jnp.mean(x * x, axis=-1, keepdims=True) + EPS) * g_ref[...]
        h = (y * (1.0 + mod_ref[0, 1:2, :]) + mod_ref[0, 0:1, :]).astype(BF16)
        h_ref[...] = h
        gate_ref[...] = _dot(h, wg_ref[...])

    proj_ref[...] = _dot(h_ref[...], w_ref[...]).astype(BF16)


def _inproj(x2, mod3, g, w_bf, wg_bf, seq):
    n, d = x2.shape
    tm = min(1024, seq)
    tn = 512
    nj = PROJ_MAIN // tn
    return pl.pallas_call(
        _inproj_kernel,
        grid=(n // tm, nj),
        in_specs=[pl.BlockSpec((tm, d), lambda i, j: (i, 0)),
                  pl.BlockSpec((1, 6, d), lambda i, j: ((i * tm) // seq, 0, 0)),
                  pl.BlockSpec((1, d), lambda i, j: (0, 0)),
                  pl.BlockSpec((d, tn), lambda i, j: (0, j)),
                  pl.BlockSpec((d, LANES), lambda i, j: (0, 0))],
        out_specs=[pl.BlockSpec((tm, tn), lambda i, j: (i, j)),
                   pl.BlockSpec((tm, LANES), lambda i, j: (i, 0))],
        out_shape=[jax.ShapeDtypeStruct((n, PROJ_MAIN), BF16),
                   jax.ShapeDtypeStruct((n, LANES), F32)],
        scratch_shapes=[pltpu.VMEM((tm, d), BF16)],
        compiler_params=_cp(("arbitrary", "arbitrary")),
        name="inproj",
    )(x2, mod3, g.reshape(1, d), w_bf, wg_bf)


def _retention_kernel(q_ref, k_ref, v_ref, g_ref, cos_ref, sin_ref, din_ref, zeta_ref,
                      qdec_ref, cdec_ref, gn_ref, o_ref, s_ref):
    @pl.when(pl.program_id(1) == 0)
    def _():
        s_ref[...] = jnp.zeros_like(s_ref)

    half = RET_DK // 2
    c = RET_CHUNK

    for sub in range(q_ref.shape[0] // c):
        rows = slice(sub * c, (sub + 1) * c)
        cos = cos_ref[rows, :]
        sin = sin_ref[rows, :]

        def rot(a):
            a1, a2 = a[:, :half], a[:, half:]
            return jnp.concatenate([a1 * cos - a2 * sin, a1 * sin + a2 * cos], axis=1)

        for h in range(RET_HEADS):
            qs = slice(h * RET_DK, (h + 1) * RET_DK)
            vs = slice(h * RET_DV, (h + 1) * RET_DV)
            q = rot(q_ref[rows, qs].astype(F32))
            k = rot(k_ref[rows, qs].astype(F32)) * (RET_DK ** -0.5)
            v = v_ref[rows, vs]
            qb = q.astype(BF16)
            kb = k.astype(BF16)
            s = _dot_nt(qb, kb) * din_ref[h]
            inner = _dot(s.astype(BF16), v)
            s_prev = s_ref[h]
            cross = _dot(qb, s_prev.astype(BF16)) * qdec_ref[h]
            kv = _dot_tn((k * zeta_ref[h]).astype(BF16), v)
            s_ref[h] = cdec_ref[h] * s_prev + kv
            o = inner + cross
            mu = jnp.mean(o, axis=-1, keepdims=True)
            oc = o - mu
            var = jnp.mean(oc * oc, axis=-1, keepdims=True)
            o = oc * lax.rsqrt(var + EPS) * gn_ref[:, vs]
            o_ref[rows, vs] = (o * _silu(g_ref[rows, vs].astype(F32))).astype(BF16)


def _retention(proj, gn_g, bsz, seq):
    n = proj.shape[0]
    c = RET_CHUNK
    nc = seq // c
    hw = RET_HEADS * RET_DK
    half = RET_DK // 2
    pos = jnp.arange(seq, dtype=F32)
    inv = ROPE_BASE ** (-jnp.arange(half, dtype=F32) / half)
    ang = pos[:, None] * inv[None, :]
    cos, sin = jnp.cos(ang), jnp.sin(ang)
    log_gamma = jnp.log1p(-jnp.exp2(-5.0 - jnp.arange(RET_HEADS, dtype=F32)))
    idx = jnp.arange(c, dtype=F32)
    rel = idx[:, None] - idx[None, :]
    decay_in = jnp.where(rel >= 0, jnp.exp(log_gamma[:, None, None] * jnp.maximum(rel, 0.0)), 0.0)
    zeta = jnp.exp(log_gamma[:, None] * (c - 1 - idx)[None, :])[:, :, None]
    q_decay = jnp.exp(log_gamma[:, None] * (idx + 1)[None, :])[:, :, None]
    chunk_decay = jnp.exp(log_gamma * c)[:, None, None]
    rb = min(4 * c, seq)
    ns = seq // rb
    row = lambda b, t: (b * ns + t)
    return pl.pallas_call(
        _retention_kernel,
        grid=(bsz, ns),
        in_specs=[pl.BlockSpec((rb, hw), lambda b, t: (row(b, t), 0)),
                  pl.BlockSpec((rb, hw), lambda b, t: (row(b, t), 1)),
                  pl.BlockSpec((rb, hw), lambda b, t: (row(b, t), 2)),
                  pl.BlockSpec((rb, hw), lambda b, t: (row(b, t), 3)),
                  pl.BlockSpec((rb, half), lambda b, t: (t, 0)),
                  pl.BlockSpec((rb, half), lambda b, t: (t, 0)),
                  pl.BlockSpec((RET_HEADS, c, c), lambda b, t: (0, 0, 0)),
                  pl.BlockSpec((RET_HEADS, c, 1), lambda b, t: (0, 0, 0)),
                  pl.BlockSpec((RET_HEADS, c, 1), lambda b, t: (0, 0, 0)),
                  pl.BlockSpec((RET_HEADS, 1, 1), lambda b, t: (0, 0, 0)),
                  pl.BlockSpec((1, hw), lambda b, t: (0, 0))],
        out_specs=pl.BlockSpec((rb, hw), lambda b, t: (row(b, t), 0)),
        out_shape=jax.ShapeDtypeStruct((n, hw), BF16),
        scratch_shapes=[pltpu.VMEM((RET_HEADS, RET_DK, RET_DV), F32)],
        compiler_params=_cp(("arbitrary", "arbitrary")),
        name="retention",
    )(proj, proj, proj, proj, cos, sin, decay_in, zeta, q_decay, chunk_decay,
      gn_g.reshape(1, hw))


def _compress_kernel(ak_ref, av_ref, pek_ref, pev_ref, w1k_ref, w1v_ref, w2k_ref, w2v_ref,
                     kc_ref, vc_ref):
    half = CMP_STRIDE * NSA_D

    def one(a_ref, pe_ref, w1_ref, w2_ref, o_ref):
        a = a_ref[...].astype(F32)
        lo = _dot((a + pe_ref[0:1, :]).astype(BF16), w1_ref[:half, :].astype(BF16))
        hi = _dot((a + pe_ref[1:2, :]).astype(BF16), w1_ref[half:, :].astype(BF16))
        rows = hi.shape[0]
        pre = lo + pltpu.roll(hi, rows - 1, 0)
        o_ref[...] = _dot(_silu(pre).astype(BF16), w2_ref[...].astype(BF16)).astype(BF16)

    one(ak_ref, pek_ref, w1k_ref, w2k_ref, kc_ref)
    one(av_ref, pev_ref, w1v_ref, w2v_ref, vc_ref)


def _compress(ak, av, pek, pev, w1k, w1v, w2k, w2v, rows_per):
    r, w = ak.shape
    full = lambda a: pl.BlockSpec(a.shape, lambda i: (0,) * a.ndim)
    blk = pl.BlockSpec((rows_per, w), lambda i: (i, 0))
    oblk = pl.BlockSpec((rows_per, NSA_D), lambda i: (i, 0))
    return pl.pallas_call(
        _compress_kernel,
        grid=(r // rows_per,),
        in_specs=[blk, blk, full(pek), full(pev), full(w1k), full(w1v), full(w2k), full(w2v)],
        out_specs=[oblk, oblk],
        out_shape=[jax.ShapeDtypeStruct((r, NSA_D), BF16)] * 2,
        compiler_params=_cp(("arbitrary",)),
        name="compress",
    )(ak, av, pek, pev, w1k, w1v, w2k, w2v)


def _nsa_kernel(q_ref, kc_ref, vc_ref, ks_ref, vs_ref, kw_ref, vw_ref, gate_ref, ovt_ref,
                o_ref, kaug_ref, *, tq, tk, seq, ncmp, wlen):
    crow = kc_ref.shape[0]
    i = pl.program_id(2)
    hg = NSA_HPG
    d = NSA_D
    r = hg * tq

    @pl.when(i == 0)
    def _():
        kaug_ref[:, :d] = ks_ref[...]
        blk = lax.broadcasted_iota(I32, (seq, LANES), 0) // SEL_BLOCK
        lane = lax.broadcasted_iota(I32, (seq, LANES), 1)
        kaug_ref[:, d:] = (blk == lane).astype(BF16)

    q = q_ref[...]
    qh = [(q[:, h * d:(h + 1) * d].astype(F32) * (d ** -0.5)).astype(BF16) for h in range(hg)]
    qa = jnp.concatenate(qh, axis=0)
    t1 = i * tq + lax.broadcasted_iota(I32, (tq, 1), 0)
    tpos = jnp.concatenate([t1] * hg, axis=0)
    t_row = i * tq + lax.broadcasted_iota(I32, (1, tq), 1)
    tpos_row = jnp.concatenate([t_row] * hg, axis=1)

    w0 = pl.multiple_of(jnp.maximum(i * tq + tq - wlen, 0), tq)
    sw = _dot_nt(qa, kw_ref[pl.ds(w0, wlen), :])
    delta = tpos - (w0 + lax.broadcasted_iota(I32, (1, wlen), 1))
    sw = jnp.where((delta >= 0) & (delta < WIN), sw, NEG)
    e_w = jnp.exp(sw - jnp.max(sw, axis=-1, keepdims=True))
    o_win = _dot(e_w.astype(BF16), vw_ref[pl.ds(w0, wlen), :]) / jnp.sum(e_w, axis=-1, keepdims=True)

    sct = _dot_nt(kc_ref[...], qa)
    cidx = lax.broadcasted_iota(I32, (crow, 1), 0)
    cmask = (cidx * CMP_STRIDE + (CMP_BLOCK - 1) <= tpos_row) & (cidx < ncmp)
    sct = jnp.where(cmask, sct, NEG)
    e_c = jnp.where(cmask, jnp.exp(sct - jnp.max(sct, axis=0, keepdims=True)), 0.0)
    den_c = jnp.sum(e_c, axis=0, keepdims=True)
    p_t = jnp.where(den_c > 0.0, e_c / jnp.where(den_c > 0.0, den_c, 1.0), 0.0)
    o_cmp = _dot_tn(p_t.astype(BF16), vc_ref[...])

    psum_t = p_t[:, 0:tq]
    for h in range(1, hg):
        psum_t = psum_t + p_t[:, h * tq:(h + 1) * tq]
    nsel_blocks = seq // SEL_BLOCK
    nsb = ovt_ref.shape[0]
    imp_t = jnp.dot(ovt_ref[...], psum_t, preferred_element_type=F32,
                    precision=lax.Precision.HIGHEST)
    sidx = lax.broadcasted_iota(I32, (nsb, 1), 0)
    cur = t_row // SEL_BLOCK
    valid = sidx <= cur
    forced = (sidx == 0) | (sidx == cur) | (sidx == cur - 1)
    score = jnp.where(valid, imp_t + jnp.where(forced, FORCE_BONUS, 0.0), -1.0)
    rank = jnp.zeros((nsb, tq), F32)
    for s2 in range(nsel_blocks):
        row = score[s2:s2 + 1, :]
        beats = (row > score) | ((row == score) & (sidx > s2))
        rank = rank + beats.astype(F32)
    sel = valid & (rank < float(min(SEL_COUNT, nsel_blocks)))
    selb_t = jnp.where(sel, 0.0, NEG)
    selb_t = jnp.concatenate([selb_t, jnp.zeros((LANES - nsb, tq), F32)], axis=0)
    selb = selb_t.T.astype(BF16)
    q_aug = jnp.concatenate([jnp.concatenate([qh[h], selb], axis=1) for h in range(hg)], axis=0)

    n_full = (i * tq + 1) // tk
    kl = lax.broadcasted_iota(I32, (1, tk), 1)

    def tile(j, carry, masked):
        m, l, acc = carry
        k0 = pl.multiple_of(j * tk, tk)
        s = _dot_nt(q_aug, kaug_ref[pl.ds(k0, tk), :])
        if masked:
            s = jnp.where(k0 + kl <= tpos, s, NEG)
        m_new = jnp.maximum(m, jnp.max(s, axis=-1, keepdims=True))
        alpha = jnp.exp(m - m_new)
        p = jnp.exp(s - m_new)
        l = alpha * l + jnp.sum(p, axis=-1, keepdims=True)
        acc = alpha * acc + _dot(p.astype(BF16), vs_ref[pl.ds(k0, tk), :])
        return m_new, l, acc

    m0 = jnp.full((r, 1), NEG, F32)
    l0 = jnp.zeros((r, 1), F32)
    a0 = jnp.zeros((r, d), F32)
    carry = lax.fori_loop(0, n_full, functools.partial(tile, masked=False), (m0, l0, a0))
    _, l_s, acc_s = tile(n_full, carry, True)
    o_slc = acc_s / l_s

    gt = jax.nn.sigmoid(gate_ref[0])
    for h in range(hg):
        rows = slice(h * tq, (h + 1) * tq)
        o = (gt[:, 3 * h:3 * h + 1] * o_cmp[rows] + gt[:, 3 * h + 1:3 * h + 2] * o_slc[rows]
             + gt[:, 3 * h + 2:3 * h + 3] * o_win[rows])
        o_ref[:, h * d:(h + 1) * d] = o.astype(BF16)


def _nsa(proj, kc, vc, gates_g, bsz, seq):
    n = proj.shape[0]
    g_ = NSA_KV_GROUPS
    tq = 256
    tk = min(512, seq)
    nq = seq // tq
    ncmp = (seq - CMP_BLOCK) // CMP_STRIDE + 1
    wlen = min(WIN + tq, seq)
    nsel = seq // SEL_BLOCK
    crow = seq // CMP_STRIDE
    assert nsel <= LANES and crow <= LANES and tq <= tk and tk % tq == 0
    nsb = -(-nsel // 8) * 8
    ss = np.arange(nsb)[:, None] * SEL_BLOCK
    cs = np.arange(crow)[None, :] * CMP_STRIDE
    ov = ((cs < ss + SEL_BLOCK) & (cs + CMP_BLOCK > ss)
          & (np.arange(crow)[None, :] < ncmp) & (np.arange(nsb)[:, None] < nsel))
    ov = jnp.asarray(ov.astype(np.float32))
    kvspec = lambda c0: pl.BlockSpec((seq, NSA_D), lambda b, g, i: (b, c0 + g))
    kern = functools.partial(_nsa_kernel, tq=tq, tk=tk, seq=seq, ncmp=ncmp, wlen=wlen)
    return pl.pallas_call(
        kern,
        grid=(bsz, g_, nq),
        in_specs=[pl.BlockSpec((tq, NSA_HPG * NSA_D), lambda b, g, i: (b * nq + i, _C_NQ // NSA_HPG + g)),
                  pl.BlockSpec((crow, NSA_D), lambda b, g, i: (b * g_ + g, 0)),
                  pl.BlockSpec((crow, NSA_D), lambda b, g, i: (b * g_ + g, 0)),
                  kvspec(_C_KS), kvspec(_C_VS), kvspec(_C_KW), kvspec(_C_VW),
                  pl.BlockSpec((1, tq, LANES), lambda b, g, i: (g, b * nq + i, 0)),
                  pl.BlockSpec((nsb, crow), lambda b, g, i: (0, 0))],
        out_specs=pl.BlockSpec((tq, NSA_HPG * NSA_D), lambda b, g, i: (b * nq + i, g)),
        out_shape=jax.ShapeDtypeStruct((n, NSA_HEADS * NSA_D), BF16),
        scratch_shapes=[pltpu.VMEM((seq, 2 * NSA_D), BF16)],
        compiler_params=_cp(("arbitrary", "arbitrary", "arbitrary")),
        name="nsa",
    )(proj, kc, vc, proj, proj, proj, proj, gates_g, ov)


def _outproj_kernel(oret_ref, onsa_ref, x_ref, mod_ref, g_ref, w_ref, wr_ref,
                    x1_ref, h2_ref, lt_ref):
    hw = oret_ref.shape[1]
    mix = _dot(oret_ref[...], w_ref[:hw, :]) + _dot(onsa_ref[...], w_ref[hw:, :])
    x1 = x_ref[...] + mod_ref[0, 2:3, :] * mix
    x1_ref[...] = x1
    y = x1 * lax.rsqrt(jnp.mean(x1 * x1, axis=-1, keepdims=True) + EPS) * g_ref[...]
    h2 = y * (1.0 + mod_ref[0, 4:5, :]) + mod_ref[0, 3:4, :]
    h2_ref[...] = h2
    lt_ref[...] = _dot_nt(wr_ref[...], h2.astype(BF16))


def _outproj(o_ret, o_nsa, x2, mod3, g, w_bf, wr_bf, seq):
    n, d = x2.shape
    tm = min(512, seq)
    hw = o_ret.shape[1]
    return pl.pallas_call(
        _outproj_kernel,
        grid=(n // tm,),
        in_specs=[pl.BlockSpec((tm, hw), lambda i: (i, 0)),
                  pl.BlockSpec((tm, o_nsa.shape[1]), lambda i: (i, 0)),
                  pl.BlockSpec((tm, d), lambda i: (i, 0)),
                  pl.BlockSpec((1, 6, d), lambda i: ((i * tm) // seq, 0, 0)),
                  pl.BlockSpec((1, d), lambda i: (0, 0)),
                  pl.BlockSpec(w_bf.shape, lambda i: (0, 0)),
                  pl.BlockSpec(wr_bf.shape, lambda i: (0, 0))],
        out_specs=[pl.BlockSpec((tm, d), lambda i: (i, 0)),
                   pl.BlockSpec((tm, d), lambda i: (i, 0)),
                   pl.BlockSpec((LANES, tm), lambda i: (0, i))],
        out_shape=[jax.ShapeDtypeStruct((n, d), F32),
                   jax.ShapeDtypeStruct((n, d), F32),
                   jax.ShapeDtypeStruct((LANES, n), F32)],
        compiler_params=_cp(("arbitrary",)),
        name="outproj",
    )(o_ret, o_nsa, x2, mod3, g.reshape(1, d), w_bf, wr_bf)


def _route_kernel(lt_ref, b_ref, tri_ref, ids_ref, wts_ref, cnt_ref, carry_ref, *, sub):
    @pl.when(pl.program_id(0) == 0)
    def _():
        carry_ref[...] = jnp.zeros_like(carry_ref)

    ng, ne = N_GROUPS, EXP_PER_GROUP
    l = lt_ref[...] + b_ref[:, 0:1]
    tc = l.shape[1]
    ridx = lax.broadcasted_iota(I32, (ng, tc), 0).astype(F32)

    def softmax0(v):
        e = jnp.exp(v - jnp.max(v, axis=0, keepdims=True))
        return e / jnp.sum(e, axis=0, keepdims=True)

    def top1(p):
        top = jnp.max(p, axis=0, keepdims=True)
        idx = jnp.min(jnp.where(p == top, ridx, float(ng)), axis=0, keepdims=True)
        return top, idx

    pg_top, grp = top1(softmax0(l[0:ng]))
    leg = jnp.zeros((ne, tc), F32)
    for g in range(ng):
        leg = jnp.where(grp == float(g), l[ng + g * ne:ng + (g + 1) * ne], leg)
    pe = softmax0(leg)
    p1, i1 = top1(pe)
    p2, i2 = top1(jnp.where(ridx == i1, -1.0, pe))
    den = p1 + p2
    w1 = pg_top * p1 / den
    w2 = pg_top * p2 / den
    e1 = grp * float(ne) + i1
    e2 = grp * float(ne) + i2

    eio = lax.broadcasted_iota(I32, (N_EXPERTS, sub), 0).astype(F32)
    r1 = []
    r2 = []
    carry = carry_ref[:, 0:1]
    for c in range(tc // sub):
        cs = slice(c * sub, (c + 1) * sub)
        oh1 = (eio == e1[:, cs]).astype(F32)
        oh2 = (eio == e2[:, cs]).astype(F32)
        oh = oh1 + oh2
        before = carry + _dot(oh.astype(BF16), tri_ref[...])
        r1.append(jnp.sum(oh1 * before, axis=0, keepdims=True))
        r2.append(jnp.sum(oh2 * before, axis=0, keepdims=True))
        carry = carry + jnp.sum(oh, axis=1, keepdims=True)
    carry_ref[...] = jnp.broadcast_to(carry, carry_ref.shape)
    cnt_ref[...] = jnp.broadcast_to(carry, cnt_ref.shape).astype(I32)
    r1 = jnp.concatenate(r1, axis=1)
    r2 = jnp.concatenate(r2, axis=1)
    zf = jnp.zeros((4, tc), F32)
    ids_ref[...] = jnp.concatenate([e1, e2, r1, r2, zf], axis=0).astype(I32)
    wts_ref[...] = jnp.concatenate([w1, w2, jnp.zeros((6, tc), F32)], axis=0)


def _route(lt, bias_col):
    n = lt.shape[1]
    tc = min(2048, n)
    sub = min(512, tc)
    tri = jnp.asarray(np.triu(np.ones((sub, sub), np.float32), 1), BF16)
    return pl.pallas_call(
        functools.partial(_route_kernel, sub=sub),
        grid=(n // tc,),
        in_specs=[pl.BlockSpec((LANES, tc), lambda i: (0, i)),
                  pl.BlockSpec((LANES, LANES), lambda i: (0, 0)),
                  pl.BlockSpec((sub, sub), lambda i: (0, 0))],
        out_specs=[pl.BlockSpec((8, tc), lambda i: (0, i)),
                   pl.BlockSpec((8, tc), lambda i: (0, i)),
                   pl.BlockSpec((N_EXPERTS, LANES), lambda i: (0, 0))],
        out_shape=[jax.ShapeDtypeStruct((8, n), I32),
                   jax.ShapeDtypeStruct((8, n), F32),
                   jax.ShapeDtypeStruct((N_EXPERTS, LANES), I32)],
        scratch_shapes=[pltpu.VMEM((N_EXPERTS, LANES), F32)],
        compiler_params=_cp(("arbitrary",)),
        name="route",
    )(lt, bias_col, tri)


def _dest_kernel(ids_ref, ps_ref, o_ref):
    ids = ids_ref[...].astype(F32)
    tc = ids.shape[1]
    eio = lax.broadcasted_iota(I32, (N_EXPERTS, tc), 0).astype(F32)
    ps = ps_ref[:, 0:1]
    rows = [jnp.sum(jnp.where(eio == ids[k:k + 1], ps, 0.0), axis=0, keepdims=True) + ids[2 + k:3 + k]
            for k in range(2)]
    o_ref[...] = jnp.concatenate(rows + [jnp.zeros((6, tc), F32)], axis=0).astype(I32)


def _dest(ids, pad_start):
    n = ids.shape[1]
    tc = min(2048, n)
    ps = jnp.broadcast_to(pad_start.astype(F32)[:, None], (N_EXPERTS, LANES))
    dest = pl.pallas_call(
        _dest_kernel,
        grid=(n // tc,),
        in_specs=[pl.BlockSpec((8, tc), lambda i: (0, i)),
                  pl.BlockSpec((N_EXPERTS, LANES), lambda i: (0, 0))],
        out_specs=pl.BlockSpec((8, tc), lambda i: (0, i)),
        out_shape=jax.ShapeDtypeStruct((8, n), I32),
        compiler_params=_cp(("arbitrary",)),
        name="dest",
    )(ids, ps)
    return dest[:2].T.reshape(2 * n)


def _dispatch_kernel(dst_ref, ps_ref, cnt_ref, h_ref, xs_ref, zbuf, sem, zsem, *, tcd, tm, zr):
    @pl.when(pl.program_id(0) == 0)
    def _():
        zbuf[...] = jnp.zeros_like(zbuf)
        sizes = [zr >> b for b in range(zr.bit_length()) if (zr >> b) >= SUBLANES]

        def zero_rows(row, size):
            return pltpu.make_async_copy(zbuf.at[pl.ds(0, size)],
                                         xs_ref.at[pl.ds(pl.multiple_of(row, SUBLANES), size)], zsem)

        def zero_row(row):
            return pltpu.make_async_copy(zbuf.at[pl.ds(0, 1)], xs_ref.at[pl.ds(row, 1)], zsem)

        def fill(e, wait):
            cnt = cnt_ref[e]
            cnt8 = lax.div(cnt + (SUBLANES - 1), SUBLANES) * SUBLANES
            gap = lax.rem(tm - lax.rem(cnt8, tm), tm)
            base = ps_ref[e]

            def single(c, _):
                cp = zero_row(base + cnt + c)
                cp.wait() if wait else cp.start()
                return 0

            lax.fori_loop(0, cnt8 - cnt, single, 0)
            row = base + cnt8
            for size in sizes:
                has = lax.rem(lax.div(gap, size), 2) == 1

                @pl.when(has)
                def _():
                    cp = zero_rows(row, size)
                    cp.wait() if wait else cp.start()

                row = row + jnp.where(has, size, 0)

        last = N_EXPERTS - 1
        used_rows = ps_ref[last] + lax.div(cnt_ref[last] + (tm - 1), tm) * tm
        n_tail = lax.div(xs_ref.shape[0] - used_rows, zr)

        def tail(c, wait):
            cp = zero_rows(used_rows + c * zr, zr)
            cp.wait() if wait else cp.start()

        for wait in (False, True):
            lax.fori_loop(0, N_EXPERTS, lambda e, _: (fill(e, wait), 0)[1], 0)
            lax.fori_loop(0, n_tail, lambda c, _: (tail(c, wait), 0)[1], 0)

    def issue(t, _):
        for k in range(2):
            dst = dst_ref[2 * t + k]
            pltpu.make_async_copy(h_ref.at[pl.ds(t, 1)], xs_ref.at[pl.ds(dst, 1)], sem).start()
        return 0

    lax.fori_loop(0, tcd, issue, 0, unroll=8)
    for k in range(2):
        pltpu.make_async_copy(h_ref, xs_ref.at[pl.ds(0, tcd)], sem).wait()


def _dispatch(dest, pad_start, counts, h2, rows, tcd, tm):
    n, d = h2.shape
    assert tm & (tm - 1) == 0 and tm >= 2 * SUBLANES
    zr = tm // 2
    return pl.pallas_call(
        functools.partial(_dispatch_kernel, tcd=tcd, tm=tm, zr=zr),
        grid=(n // tcd,),
        in_specs=[pl.BlockSpec((2 * tcd,), lambda i: (i,), memory_space=pltpu.SMEM),
                  pl.BlockSpec(memory_space=pltpu.SMEM),
                  pl.BlockSpec(memory_space=pltpu.SMEM),
                  pl.BlockSpec((tcd, d), lambda i: (i, 0))],
        out_specs=pl.BlockSpec(memory_space=pl.ANY),
        out_shape=jax.ShapeDtypeStruct((rows, d), h2.dtype),
        scratch_shapes=[pltpu.VMEM((zr, d), h2.dtype), pltpu.SemaphoreType.DMA(()),
                        pltpu.SemaphoreType.DMA(())],
        compiler_params=_cp(("arbitrary",)),
        name="dispatch",
    )(dest, pad_start, counts, h2)


def _experts_kernel(be_ref, ord_ref, nxt_ref, nu_ref, xs_ref, wg_hbm, wu_hbm, wd_hbm, ys_ref,
                    g_f, u_f, d_f, g_s, u_s, d_s, sem):
    i = pl.program_id(0)
    used = i < nu_ref[0]
    e = be_ref[i]
    fresh = (i == 0) | (e != be_ref[jnp.maximum(i - 1, 0)])
    slot = lax.rem(ord_ref[i], 2)

    def fetch(expert, s):
        return [pltpu.make_async_copy(w.at[expert], f.at[s], sem.at[s, k])
                for k, (w, f) in enumerate(((wg_hbm, g_f), (wu_hbm, u_f), (wd_hbm, d_f)))]

    @pl.when(used & (i == 0))
    def _():
        for c in fetch(e, slot):
            c.start()

    @pl.when(used & fresh)
    def _():
        for c in fetch(e, slot):
            c.wait()
        nxt = nxt_ref[i]

        @pl.when(nxt >= 0)
        def _():
            for c in fetch(nxt, 1 - slot):
                c.start()

        g_s[...] = g_f[slot].astype(BF16)
        u_s[...] = u_f[slot].astype(BF16)
        d_s[...] = d_f[slot].astype(BF16)

    @pl.when(used)
    def _():
        x = xs_ref[...].astype(BF16)
        a = _silu(_dot(x, g_s[...])) * _dot(x, u_s[...])
        ys_ref[...] = _dot(a.astype(BF16), d_s[...])

    @pl.when(jnp.logical_not(used))
    def _():
        ys_ref[...] = jnp.zeros_like(ys_ref)


def _experts(blk_exp, blk_ord, blk_nxt, n_used, xs, w_gate, w_up, w_down, tm):
    d = xs.shape[1]
    de = w_gate.shape[2]
    nb = blk_exp.shape[0]
    row = lambda i, be, od, nx, nu: (jnp.minimum(i, nu[0] - 1), 0)
    anyspec = pl.BlockSpec(memory_space=pl.ANY)
    grid_spec = pltpu.PrefetchScalarGridSpec(
        num_scalar_prefetch=4,
        grid=(nb,),
        in_specs=[pl.BlockSpec((tm, d), row), anyspec, anyspec, anyspec],
        out_specs=pl.BlockSpec((tm, d), lambda i, be, od, nx, nu: (i, 0)),
        scratch_shapes=[pltpu.VMEM((2, d, de), F32), pltpu.VMEM((2, d, de), F32),
                        pltpu.VMEM((2, de, d), F32),
                        pltpu.VMEM((d, de), BF16), pltpu.VMEM((d, de), BF16),
                        pltpu.VMEM((de, d), BF16),
                        pltpu.SemaphoreType.DMA((2, 3))],
    )
    return pl.pallas_call(
        _experts_kernel,
        grid_spec=grid_spec,
        out_shape=jax.ShapeDtypeStruct((nb * tm, d), F32),
        compiler_params=_cp(("arbitrary",)),
        name="experts",
    )(blk_exp, blk_ord, blk_nxt, n_used, xs, w_gate, w_up, w_down)


def _combine_kernel(ids_ref, idn_ref, ys_ref, x1_ref, wt_ref, mod_ref, g_ref, o_ref,
                    buf, sem, *, tc):
    i = pl.program_id(0)
    slot = lax.rem(i, 2)

    def gather(id_ref, s):
        def issue(t, _):
            for k in range(2):
                src = id_ref[2 * t + k]
                pltpu.make_async_copy(ys_ref.at[pl.ds(src, 1)], buf.at[s, k, pl.ds(t, 1)],
                                      sem.at[s]).start()
            return 0

        lax.fori_loop(0, tc, issue, 0, unroll=8)

    @pl.when(i == 0)
    def _():
        gather(ids_ref, 0)

    @pl.when(i + 1 < pl.num_programs(0))
    def _():
        gather(idn_ref, 1 - slot)

    for k in range(2):
        pltpu.make_async_copy(ys_ref.at[pl.ds(0, tc)], buf.at[slot, k], sem.at[slot]).wait()

    moe = buf[slot, 0] * wt_ref[:, 0:1] + buf[slot, 1] * wt_ref[:, 1:2]
    x2 = x1_ref[...] + mod_ref[0, 5:6, :] * moe
    o_ref[...] = x2 * lax.rsqrt(jnp.mean(x2 * x2, axis=-1, keepdims=True) + EPS) * g_ref[...]


def _combine(dest, ys, x1, wts_t, mod3, final_g, seq, tc):
    n, d = x1.shape
    last = n // tc - 1
    return pl.pallas_call(
        functools.partial(_combine_kernel, tc=tc),
        grid=(n // tc,),
        in_specs=[pl.BlockSpec((2 * tc,), lambda i: (i,), memory_space=pltpu.SMEM),
                  pl.BlockSpec((2 * tc,), lambda i: (jnp.minimum(i + 1, last),),
                               memory_space=pltpu.SMEM),
                  pl.BlockSpec(memory_space=pl.ANY),
                  pl.BlockSpec((tc, d), lambda i: (i, 0)),
                  pl.BlockSpec((tc, LANES), lambda i: (i, 0)),
                  pl.BlockSpec((1, 6, d), lambda i: ((i * tc) // seq, 0, 0)),
                  pl.BlockSpec((1, d), lambda i: (0, 0))],
        out_specs=pl.BlockSpec((tc, d), lambda i: (i, 0)),
        out_shape=jax.ShapeDtypeStruct((n, d), F32),
        scratch_shapes=[pltpu.VMEM((2, 2, tc, d), F32), pltpu.SemaphoreType.DMA((2,))],
        compiler_params=_cp(("arbitrary",)),
        name="combine",
    )(dest, dest, ys, x1, wts_t, mod3, final_g.reshape(1, d))


def _token_mixer(x2, mod3, norm1_g, w_in, ret_gn_g, cmp_pos_k, cmp_w1_k, cmp_w2_k,
                 cmp_pos_v, cmp_w1_v, cmp_w2_v, bsz, seq):
    n, d = x2.shape
    w_main = w_in[:, :PROJ_MAIN].astype(BF16)
    w_gate_cols = jnp.pad(w_in[:, PROJ_MAIN:], ((0, 0), (0, LANES - N_GATE_COLS))).astype(BF16)
    proj, gate_logits = _inproj(x2, mod3, norm1_g, w_main, w_gate_cols, seq)

    o_ret = _retention(proj, ret_gn_g, bsz, seq)

    g_ = NSA_KV_GROUPS
    crow = seq // CMP_STRIDE

    def cmp_src(c0):
        a = proj[:, c0 * LANES:(c0 + g_) * LANES].reshape(bsz, seq, g_, NSA_D)
        return a.transpose(0, 2, 1, 3).reshape(bsz * g_ * crow, CMP_STRIDE * NSA_D)

    pe2 = lambda pe: pe.reshape(2, CMP_STRIDE * NSA_D)
    kc, vc = _compress(cmp_src(_C_KC), cmp_src(_C_VC), pe2(cmp_pos_k), pe2(cmp_pos_v),
                       cmp_w1_k, cmp_w1_v, cmp_w2_k, cmp_w2_v, crow)

    gl = gate_logits[:, :N_GATE_COLS].reshape(n, g_, NSA_HPG * 3).transpose(1, 0, 2)
    gates_g = jnp.pad(gl, ((0, 0), (0, 0), (0, LANES - NSA_HPG * 3)))
    o_nsa = _nsa(proj, kc, vc, gates_g, bsz, seq)
    return o_ret, o_nsa


def _moe(h2, lt, x1, mod3, final_g, b_grp, b_exp, w_gate, w_up, w_down, seq, tm):
    n, d = h2.shape
    bias_col = jnp.zeros((LANES,), F32).at[:N_GROUPS].set(b_grp).at[N_GROUPS:N_GROUPS + N_EXPERTS].set(b_exp)
    bias_col = jnp.broadcast_to(bias_col[:, None], (LANES, LANES))
    ids, wts, cnt = _route(lt, bias_col)

    counts = cnt[:, 0]
    padded = (counts + tm - 1) // tm * tm
    pad_end = jnp.cumsum(padded)
    pad_start = (pad_end - padded).astype(I32)
    nb = (2 * n) // tm + N_EXPERTS
    n_used = (pad_end[-1] // tm).astype(I32).reshape(1)
    blk_start = jnp.arange(nb, dtype=I32) * tm
    blk_exp = jnp.minimum(jnp.sum((pad_end[None, :] <= blk_start[:, None]).astype(I32), axis=1),
                          N_EXPERTS - 1).astype(I32)
    last_exp = blk_exp[jnp.maximum(n_used[0] - 1, 0)]
    blk_exp = jnp.where(jnp.arange(nb) < n_used[0], blk_exp, last_exp)
    eid = jnp.arange(N_EXPERTS, dtype=I32)
    has = counts > 0
    exp_ord = jnp.sum((has[None, :] & (eid[None, :] < eid[:, None])).astype(I32), axis=1)
    exp_nxt = jnp.min(jnp.where(has[None, :] & (eid[None, :] > eid[:, None]), eid[None, :], N_EXPERTS), axis=1)
    exp_nxt = jnp.where(exp_nxt < N_EXPERTS, exp_nxt, -1).astype(I32)
    blk_ord = exp_ord[blk_exp]
    blk_nxt = exp_nxt[blk_exp]

    dest = _dest(ids, pad_start)
    xs = _dispatch(dest, pad_start, counts.astype(I32), h2, nb * tm, min(512, n), tm)
    ys = _experts(blk_exp, blk_ord, blk_nxt, n_used, xs, w_gate, w_up, w_down, tm)
    wts_t = jnp.pad(wts[:2].T, ((0, 0), (0, LANES - 2)))
    return _combine(dest, ys, x1, wts_t, mod3, final_g, seq, min(256, n))


def kernel(x, c, w_ada, b_ada, norm1_g, norm2_g, final_g, w_in, ret_gn_g, cmp_pos_k, cmp_w1_k,
           cmp_w2_k, cmp_pos_v, cmp_w1_v, cmp_w2_v, w_out, w_grp, b_grp, w_exp, b_exp, w_gate,
           w_up, w_down):
    bsz, seq, d = x.shape
    assert w_ada.shape[0] == 1, "single-layer block"
    n = bsz * seq
    x2 = x.reshape(n, d)
    mod3 = _ada(c, w_ada[0], b_ada[0]).reshape(bsz, 6, d)

    o_ret, o_nsa = _token_mixer(x2, mod3, norm1_g[0], w_in[0], ret_gn_g[0], cmp_pos_k[0],
                                cmp_w1_k[0], cmp_w2_k[0], cmp_pos_v[0], cmp_w1_v[0], cmp_w2_v[0],
                                bsz, seq)

    w_route = jnp.concatenate([w_grp[0], w_exp[0]], axis=1)
    w_route = jnp.pad(w_route, ((0, 0), (0, LANES - w_route.shape[1]))).T.astype(BF16)
    x1, h2, lt = _outproj(o_ret, o_nsa, x2, mod3, norm2_g[0], w_out[0].astype(BF16), w_route, seq)

    out = _moe(h2, lt, x1, mod3, final_g, b_grp[0], b_exp[0], w_gate[0], w_up[0], w_down[0],
               seq, 256)
    return out.reshape(bsz, seq, d)
```

```python
import functools
import math

import numpy as np
import jax
import jax.numpy as jnp
from jax import lax
from jax.experimental import pallas as pl
from jax.experimental.pallas import tpu as pltpu

F32 = jnp.float32
BF16 = jnp.bfloat16
I32 = jnp.int32

RET_HEADS = 4
RET_DK = 256
RET_DV = 256
RET_CHUNK = 128
NSA_HEADS = 8
NSA_KV_GROUPS = 2
NSA_HPG = NSA_HEADS // NSA_KV_GROUPS
NSA_D = 128
CMP_BLOCK = 32
CMP_STRIDE = 16
CMP_HIDDEN = 256
SEL_BLOCK = 64
SEL_COUNT = 16
WIN = 512
N_GROUPS = 8
EXP_PER_GROUP = 8
N_EXPERTS = N_GROUPS * EXP_PER_GROUP
D_EXPERT = 512
ROPE_BASE = 10000.0
EPS = 1e-6
NEG = -1e30
FORCE_BONUS = 1e4

LANES = 128
SUBLANES = 8
VMEM_LIMIT = 52 * 1024 * 1024

_C_RQ, _C_RK, _C_RV, _C_RG = 0, 8, 16, 24
_C_NQ = 32
_C_KC, _C_VC, _C_KS, _C_VS, _C_KW, _C_VW = 40, 42, 44, 46, 48, 50
_C_GATE = 52
PROJ_MAIN = _C_GATE * LANES
N_GATE_COLS = NSA_HEADS * 3


def _cp(sem, vmem=VMEM_LIMIT):
    return pltpu.CompilerParams(dimension_semantics=sem, vmem_limit_bytes=vmem)


def _silu(v):
    return v * jax.nn.sigmoid(v)


def _dot(a, b):
    return jnp.dot(a, b, preferred_element_type=F32)


def _dot_nt(a, b):
    return lax.dot_general(a, b, (((1,), (1,)), ((), ())), preferred_element_type=F32)


def _dot_tn(a, b):
    return lax.dot_general(a, b, (((0,), (0,)), ((), ())), preferred_element_type=F32)


def _ada_kernel(c_ref, w_ref, b_ref, o_ref):
    ca = _silu(c_ref[...])
    o_ref[...] = jnp.dot(ca, w_ref[...], preferred_element_type=F32,
                         precision=lax.Precision.HIGHEST) + b_ref[...]


def _ada(c, w, b):
    bsz, d = c.shape
    n = w.shape[1]
    tn = min(1024, n)
    return pl.pallas_call(
        _ada_kernel,
        grid=(n // tn,),
        in_specs=[pl.BlockSpec((bsz, d), lambda j: (0, 0)),
                  pl.BlockSpec((d, tn), lambda j: (0, j)),
                  pl.BlockSpec((1, tn), lambda j: (0, j))],
        out_specs=pl.BlockSpec((bsz, tn), lambda j: (0, j)),
        out_shape=jax.ShapeDtypeStruct((bsz, n), F32),
        compiler_params=_cp(("arbitrary",)),
        name="ada",
    )(c, w, b.reshape(1, n))


def _inproj_kernel(x_ref, mod_ref, g_ref, w_ref, wg_ref, proj_ref, gate_ref, h_ref):
    @pl.when(pl.program_id(1) == 0)
    def _():
        x = x_ref[...]
        y = x * lax.rsqrt(jnp.mean(x * x, axis=-1, keepdims=True) + EPS) * g_ref[...]
        h = (y * (1.0 + mod_ref[0, 1:2, :]) + mod_ref[0, 0:1, :]).astype(BF16)
        h_ref[...] = h
        gate_ref[...] = _dot(h, wg_ref[...])

    proj_ref[...] = _dot(h_ref[...], w_ref[...].astype(BF16)).astype(BF16)


def _inproj(x2, mod3, g, w_bf, wg_bf, seq):
    n, d = x2.shape
    tm = min(1024, seq)
    tn = 512
    nj = PROJ_MAIN // tn
    return pl.pallas_call(
        _inproj_kernel,
        grid=(n // tm, nj),
        in_specs=[pl.BlockSpec((tm, d), lambda i, j: (i, 0)),
                  pl.BlockSpec((1, 6, d), lambda i, j: ((i * tm) // seq, 0, 0)),
                  pl.BlockSpec((1, d), lambda i, j: (0, 0)),
                  pl.BlockSpec((d, tn), lambda i, j: (0, j)),
                  pl.BlockSpec((d, LANES), lambda i, j: (0, 0))],
        out_specs=[pl.BlockSpec((tm, tn), lambda i, j: (i, j)),
                   pl.BlockSpec((tm, LANES), lambda i, j: (i, 0))],
        out_shape=[jax.ShapeDtypeStruct((n, PROJ_MAIN), BF16),
                   jax.ShapeDtypeStruct((n, LANES), F32)],
        scratch_shapes=[pltpu.VMEM((tm, d), BF16)],
        compiler_params=_cp(("arbitrary", "arbitrary")),
        name="inproj",
    )(x2, mod3, g.reshape(1, d), w_bf, wg_bf)


def _retention_kernel(q_ref, k_ref, v_ref, g_ref, cos_ref, sin_ref, din_ref, zeta_ref,
                      qdec_ref, cdec_ref, gn_ref, o_ref, s_ref):
    @pl.when(pl.program_id(1) == 0)
    def _():
        s_ref[...] = jnp.zeros_like(s_ref)

    half = RET_DK // 2
    c = RET_CHUNK

    for sub in range(q_ref.shape[0] // c):
        rows = slice(sub * c, (sub + 1) * c)
        cos = cos_ref[rows, :]
        sin = sin_ref[rows, :]

        def rot(a):
            a1, a2 = a[:, :half], a[:, half:]
            return jnp.concatenate([a1 * cos - a2 * sin, a1 * sin + a2 * cos], axis=1)

        for h in range(RET_HEADS):
            qs = slice(h * RET_DK, (h + 1) * RET_DK)
            vs = slice(h * RET_DV, (h + 1) * RET_DV)
            q = rot(q_ref[rows, qs].astype(F32))
            k = rot(k_ref[rows, qs].astype(F32)) * (RET_DK ** -0.5)
            v = v_ref[rows, vs]
            qb = q.astype(BF16)
            kb = k.astype(BF16)
            s = _dot_nt(qb, kb) * din_ref[h]
            inner = _dot(s.astype(BF16), v)
            s_prev = s_ref[h]
            cross = _dot(qb, s_prev.astype(BF16)) * qdec_ref[h]
            kv = _dot_tn((k * zeta_ref[h]).astype(BF16), v)
            s_ref[h] = cdec_ref[h] * s_prev + kv
            o = inner + cross
            mu = jnp.mean(o, axis=-1, keepdims=True)
            oc = o - mu
            var = jnp.mean(oc * oc, axis=-1, keepdims=True)
            o = oc * lax.rsqrt(var + EPS) * gn_ref[:, vs]
            o_ref[rows, vs] = (o * _silu(g_ref[rows, vs].astype(F32))).astype(BF16)


def _retention(proj, gn_g, bsz, seq):
    n = proj.shape[0]
    c = RET_CHUNK
    nc = seq // c
    hw = RET_HEADS * RET_DK
    half = RET_DK // 2
    pos = jnp.arange(seq, dtype=F32)
    inv = ROPE_BASE ** (-jnp.arange(half, dtype=F32) / half)
    ang = pos[:, None] * inv[None, :]
    cos, sin = jnp.cos(ang), jnp.sin(ang)
    log_gamma = jnp.log1p(-jnp.exp2(-5.0 - jnp.arange(RET_HEADS, dtype=F32)))
    idx = jnp.arange(c, dtype=F32)
    rel = idx[:, None] - idx[None, :]
    decay_in = jnp.where(rel >= 0, jnp.exp(log_gamma[:, None, None] * jnp.maximum(rel, 0.0)), 0.0)
    zeta = jnp.exp(log_gamma[:, None] * (c - 1 - idx)[None, :])[:, :, None]
    q_decay = jnp.exp(log_gamma[:, None] * (idx + 1)[None, :])[:, :, None]
    chunk_decay = jnp.exp(log_gamma * c)[:, None, None]
    rb = min(4 * c, seq)
    ns = seq // rb
    row = lambda b, t: (b * ns + t)
    return pl.pallas_call(
        _retention_kernel,
        grid=(bsz, ns),
        in_specs=[pl.BlockSpec((rb, hw), lambda b, t: (row(b, t), 0)),
                  pl.BlockSpec((rb, hw), lambda b, t: (row(b, t), 1)),
                  pl.BlockSpec((rb, hw), lambda b, t: (row(b, t), 2)),
                  pl.BlockSpec((rb, hw), lambda b, t: (row(b, t), 3)),
                  pl.BlockSpec((rb, half), lambda b, t: (t, 0)),
                  pl.BlockSpec((rb, half), lambda b, t: (t, 0)),
                  pl.BlockSpec((RET_HEADS, c, c), lambda b, t: (0, 0, 0)),
                  pl.BlockSpec((RET_HEADS, c, 1), lambda b, t: (0, 0, 0)),
                  pl.BlockSpec((RET_HEADS, c, 1), lambda b, t: (0, 0, 0)),
                  pl.BlockSpec((RET_HEADS, 1, 1), lambda b, t: (0, 0, 0)),
                  pl.BlockSpec((1, hw), lambda b, t: (0, 0))],
        out_specs=pl.BlockSpec((rb, hw), lambda b, t: (row(b, t), 0)),
        out_shape=jax.ShapeDtypeStruct((n, hw), BF16),
        scratch_shapes=[pltpu.VMEM((RET_HEADS, RET_DK, RET_DV), F32)],
        compiler_params=_cp(("arbitrary", "arbitrary")),
        name="retention",
    )(proj, proj, proj, proj, cos, sin, decay_in, zeta, q_decay, chunk_decay,
      gn_g.reshape(1, hw))


def _compress_kernel(ak_ref, av_ref, pek_ref, pev_ref, w1k_ref, w1v_ref, w2k_ref, w2v_ref,
                     kc_ref, vc_ref):
    half = CMP_STRIDE * NSA_D

    def one(a_ref, pe_ref, w1_ref, w2_ref, o_ref):
        a = a_ref[0].astype(F32)
        lo = _dot((a + pe_ref[0:1, :]).astype(BF16), w1_ref[:half, :].astype(BF16))
        hi = _dot((a + pe_ref[1:2, :]).astype(BF16), w1_ref[half:, :].astype(BF16))
        rows = hi.shape[0]
        pre = lo + pltpu.roll(hi, rows - 1, 0)
        o_ref[...] = _dot(_silu(pre).astype(BF16), w2_ref[...].astype(BF16)).astype(BF16)

    one(ak_ref, pek_ref, w1k_ref, w2k_ref, kc_ref)
    one(av_ref, pev_ref, w1v_ref, w2v_ref, vc_ref)


def _compress(src, pek, pev, w1k, w1v, w2k, w2v, bsz, rows_per):
    g_ = NSA_KV_GROUPS
    w = src.shape[2]
    full = lambda a: pl.BlockSpec(a.shape, lambda i: (0,) * a.ndim)
    kblk = pl.BlockSpec((1, rows_per, w), lambda i: (i % g_, i // g_, 0))
    vblk = pl.BlockSpec((1, rows_per, w), lambda i: (g_ + i % g_, i // g_, 0))
    oblk = pl.BlockSpec((rows_per, NSA_D), lambda i: (i, 0))
    r = bsz * g_ * rows_per
    return pl.pallas_call(
        _compress_kernel,
        grid=(bsz * g_,),
        in_specs=[kblk, vblk, full(pek), full(pev), full(w1k), full(w1v), full(w2k), full(w2v)],
        out_specs=[oblk, oblk],
        out_shape=[jax.ShapeDtypeStruct((r, NSA_D), BF16)] * 2,
        compiler_params=_cp(("arbitrary",)),
        name="compress",
    )(src, src, pek, pev, w1k, w1v, w2k, w2v)


def _nsa_kernel(q_ref, kc_ref, vc_ref, ks_ref, vs_ref, kw_ref, vw_ref, gate_ref, ovt_ref,
                o_ref, kaug_ref, vst_ref, vwt_ref, *, tq, tk, vt, seq, ncmp, wlen):
    crow = kc_ref.shape[0]
    i = pl.program_id(2)
    hg = NSA_HPG
    d = NSA_D
    r = hg * tq

    @pl.when(i == 0)
    def _():
        kaug_ref[:, :d] = ks_ref[...]
        blk = lax.broadcasted_iota(I32, (seq, LANES), 0) // SEL_BLOCK
        lane = lax.broadcasted_iota(I32, (seq, LANES), 1)
        kaug_ref[:, d:] = (blk == lane).astype(BF16)
        for c in range(seq // vt):
            vst_ref[c] = vs_ref[c * vt:(c + 1) * vt, :].astype(F32).T.astype(BF16)
            vwt_ref[c] = vw_ref[c * vt:(c + 1) * vt, :].astype(F32).T.astype(BF16)

    def pv_t(vt_ref, first_tile, p):
        out = None
        for c in range(p.shape[0] // vt):
            part = _dot(vt_ref[first_tile + c], p[c * vt:(c + 1) * vt, :])
            out = part if out is None else out + part
        return out

    q = q_ref[...]
    qh = [(q[:, h * d:(h + 1) * d].astype(F32) * (d ** -0.5)).astype(BF16) for h in range(hg)]
    qa = jnp.concatenate(qh, axis=0)
    t1 = i * tq + lax.broadcasted_iota(I32, (tq, 1), 0)
    tpos = jnp.concatenate([t1] * hg, axis=0)
    t_row = i * tq + lax.broadcasted_iota(I32, (1, tq), 1)
    tpos_row = jnp.concatenate([t_row] * hg, axis=1)

    w0 = pl.multiple_of(jnp.maximum(i * tq + tq - wlen, 0), vt)
    sw = _dot_nt(kw_ref[pl.ds(w0, wlen), :], qa)
    delta = tpos_row - (w0 + lax.broadcasted_iota(I32, (wlen, 1), 0))
    sw = jnp.where((delta >= 0) & (delta < WIN), sw, NEG)
    e_w = jnp.exp(sw - jnp.max(sw, axis=0, keepdims=True))
    o_win = (pv_t(vwt_ref, w0 // vt, e_w.astype(BF16)) / jnp.sum(e_w, axis=0, keepdims=True)).T

    sct = _dot_nt(kc_ref[...], qa)
    cidx = lax.broadcasted_iota(I32, (crow, 1), 0)
    cmask = (cidx * CMP_STRIDE + (CMP_BLOCK - 1) <= tpos_row) & (cidx < ncmp)
    sct = jnp.where(cmask, sct, NEG)
    e_c = jnp.where(cmask, jnp.exp(sct - jnp.max(sct, axis=0, keepdims=True)), 0.0)
    den_c = jnp.sum(e_c, axis=0, keepdims=True)
    p_t = jnp.where(den_c > 0.0, e_c / jnp.where(den_c > 0.0, den_c, 1.0), 0.0)
    o_cmp = _dot_tn(p_t.astype(BF16), vc_ref[...])

    psum_t = p_t[:, 0:tq]
    for h in range(1, hg):
        psum_t = psum_t + p_t[:, h * tq:(h + 1) * tq]
    nsel_blocks = seq // SEL_BLOCK
    nsb = ovt_ref.shape[0]
    imp_t = jnp.dot(ovt_ref[...], psum_t, preferred_element_type=F32,
                    precision=lax.Precision.HIGHEST)
    sidx = lax.broadcasted_iota(I32, (nsb, 1), 0)
    cur = t_row // SEL_BLOCK
    valid = sidx <= cur
    forced = (sidx == 0) | (sidx == cur) | (sidx == cur - 1)
    score = jnp.where(valid, imp_t + jnp.where(forced, FORCE_BONUS, 0.0), -1.0)
    rank = jnp.zeros((nsb, tq), F32)
    for s2 in range(nsel_blocks):
        row = score[s2:s2 + 1, :]
        beats = (row > score) | ((row == score) & (sidx > s2))
        rank = rank + beats.astype(F32)
    sel = valid & (rank < float(min(SEL_COUNT, nsel_blocks)))
    selb_t = jnp.where(sel, 0.0, NEG)
    selb_t = jnp.concatenate([selb_t, jnp.zeros((LANES - nsb, tq), F32)], axis=0)
    selb = selb_t.T.astype(BF16)
    q_aug = jnp.concatenate([jnp.concatenate([qh[h], selb], axis=1) for h in range(hg)], axis=0)

    n_full = (i * tq + 1) // tk
    kl = lax.broadcasted_iota(I32, (tk, 1), 0)

    def tile(j, carry, masked):
        m, l, acc = carry
        k0 = pl.multiple_of(j * tk, tk)
        s = _dot_nt(kaug_ref[pl.ds(k0, tk), :], q_aug)
        if masked:
            s = jnp.where(k0 + kl <= tpos_row, s, NEG)
        m_new = jnp.maximum(m, jnp.max(s, axis=0, keepdims=True))
        alpha = jnp.exp(m - m_new)
        p = jnp.exp(s - m_new)
        l = alpha * l + jnp.sum(p, axis=0, keepdims=True)
        acc = alpha * acc + pv_t(vst_ref, j * (tk // vt), p.astype(BF16))
        return m_new, l, acc

    m0 = jnp.full((1, r), NEG, F32)
    l0 = jnp.zeros((1, r), F32)
    a0 = jnp.zeros((d, r), F32)
    carry = lax.fori_loop(0, n_full, functools.partial(tile, masked=False), (m0, l0, a0))
    _, l_s, acc_s = tile(n_full, carry, True)
    o_slc = (acc_s / l_s).T

    gt = jax.nn.sigmoid(gate_ref[0])
    for h in range(hg):
        rows = slice(h * tq, (h + 1) * tq)
        o = (gt[:, 3 * h:3 * h + 1] * o_cmp[rows] + gt[:, 3 * h + 1:3 * h + 2] * o_slc[rows]
             + gt[:, 3 * h + 2:3 * h + 3] * o_win[rows])
        o_ref[:, h * d:(h + 1) * d] = o.astype(BF16)


def _nsa(proj, kc, vc, gates_g, bsz, seq):
    n = proj.shape[0]
    g_ = NSA_KV_GROUPS
    tq = 256
    tk = min(512, seq)
    nq = seq // tq
    ncmp = (seq - CMP_BLOCK) // CMP_STRIDE + 1
    wlen = min(WIN + tq, seq)
    nsel = seq // SEL_BLOCK
    crow = seq // CMP_STRIDE
    assert nsel <= LANES and crow <= LANES and tq <= tk and tk % tq == 0
    nsb = -(-nsel // 8) * 8
    ss = np.arange(nsb)[:, None] * SEL_BLOCK
    cs = np.arange(crow)[None, :] * CMP_STRIDE
    ov = ((cs < ss + SEL_BLOCK) & (cs + CMP_BLOCK > ss)
          & (np.arange(crow)[None, :] < ncmp) & (np.arange(nsb)[:, None] < nsel))
    ov = jnp.asarray(ov.astype(np.float32))
    kvspec = lambda c0: pl.BlockSpec((seq, NSA_D), lambda b, g, i: (b, c0 + g))
    vt = 256
    assert tk % vt == 0 and wlen % vt == 0 and tq % vt == 0 and seq % vt == 0
    kern = functools.partial(_nsa_kernel, tq=tq, tk=tk, vt=vt, seq=seq, ncmp=ncmp, wlen=wlen)
    return pl.pallas_call(
        kern,
        grid=(bsz, g_, nq),
        in_specs=[pl.BlockSpec((tq, NSA_HPG * NSA_D), lambda b, g, i: (b * nq + i, _C_NQ // NSA_HPG + g)),
                  pl.BlockSpec((crow, NSA_D), lambda b, g, i: (b * g_ + g, 0)),
                  pl.BlockSpec((crow, NSA_D), lambda b, g, i: (b * g_ + g, 0)),
                  kvspec(_C_KS), kvspec(_C_VS), kvspec(_C_KW), kvspec(_C_VW),
                  pl.BlockSpec((1, tq, LANES), lambda b, g, i: (g, b * nq + i, 0)),
                  pl.BlockSpec((nsb, crow), lambda b, g, i: (0, 0))],
        out_specs=pl.BlockSpec((tq, NSA_HPG * NSA_D), lambda b, g, i: (b * nq + i, g)),
        out_shape=jax.ShapeDtypeStruct((n, NSA_HEADS * NSA_D), BF16),
        scratch_shapes=[pltpu.VMEM((seq, 2 * NSA_D), BF16),
                        pltpu.VMEM((seq // vt, NSA_D, vt), BF16),
                        pltpu.VMEM((seq // vt, NSA_D, vt), BF16)],
        compiler_params=_cp(("arbitrary", "arbitrary", "arbitrary")),
        name="nsa",
    )(proj, kc, vc, proj, proj, proj, proj, gates_g, ov)


def _outproj_kernel(oret_ref, onsa_ref, x_ref, mod_ref, g_ref, w_ref, wr_ref,
                    x1_ref, h2_ref, lt_ref):
    hw = oret_ref.shape[1]
    mix = _dot(oret_ref[...], w_ref[:hw, :]) + _dot(onsa_ref[...], w_ref[hw:, :])
    x1 = x_ref[...] + mod_ref[0, 2:3, :] * mix
    x1_ref[...] = x1
    y = x1 * lax.rsqrt(jnp.mean(x1 * x1, axis=-1, keepdims=True) + EPS) * g_ref[...]
    h2 = y * (1.0 + mod_ref[0, 4:5, :]) + mod_ref[0, 3:4, :]
    h2_ref[...] = h2
    lt_ref[...] = _dot_nt(wr_ref[...], h2.astype(BF16))


def _outproj(o_ret, o_nsa, x2, mod3, g, w_bf, wr_bf, seq):
    n, d = x2.shape
    tm = min(512, seq)
    hw = o_ret.shape[1]
    return pl.pallas_call(
        _outproj_kernel,
        grid=(n // tm,),
        in_specs=[pl.BlockSpec((tm, hw), lambda i: (i, 0)),
                  pl.BlockSpec((tm, o_nsa.shape[1]), lambda i: (i, 0)),
                  pl.BlockSpec((tm, d), lambda i: (i, 0)),
                  pl.BlockSpec((1, 6, d), lambda i: ((i * tm) // seq, 0, 0)),
                  pl.BlockSpec((1, d), lambda i: (0, 0)),
                  pl.BlockSpec(w_bf.shape, lambda i: (0, 0)),
                  pl.BlockSpec(wr_bf.shape, lambda i: (0, 0))],
        out_specs=[pl.BlockSpec((tm, d), lambda i: (i, 0)),
                   pl.BlockSpec((tm, d), lambda i: (i, 0)),
                   pl.BlockSpec((LANES, tm), lambda i: (0, i))],
        out_shape=[jax.ShapeDtypeStruct((n, d), F32),
                   jax.ShapeDtypeStruct((n, d), F32),
                   jax.ShapeDtypeStruct((LANES, n), F32)],
        compiler_params=_cp(("arbitrary",)),
        name="outproj",
    )(o_ret, o_nsa, x2, mod3, g.reshape(1, d), w_bf, wr_bf)


def _route_kernel(lt_ref, b_ref, tri_ref, ids_ref, wts_ref, cnt_ref, carry_ref, *, sub):
    @pl.when(pl.program_id(0) == 0)
    def _():
        carry_ref[...] = jnp.zeros_like(carry_ref)

    ng, ne = N_GROUPS, EXP_PER_GROUP
    l = lt_ref[...] + b_ref[:, 0:1]
    tc = l.shape[1]
    ridx = lax.broadcasted_iota(I32, (ng, tc), 0).astype(F32)

    def softmax0(v):
        e = jnp.exp(v - jnp.max(v, axis=0, keepdims=True))
        return e / jnp.sum(e, axis=0, keepdims=True)

    def top1(p):
        top = jnp.max(p, axis=0, keepdims=True)
        idx = jnp.min(jnp.where(p == top, ridx, float(ng)), axis=0, keepdims=True)
        return top, idx

    pg_top, grp = top1(softmax0(l[0:ng]))
    leg = jnp.zeros((ne, tc), F32)
    for g in range(ng):
        leg = jnp.where(grp == float(g), l[ng + g * ne:ng + (g + 1) * ne], leg)
    pe = softmax0(leg)
    p1, i1 = top1(pe)
    p2, i2 = top1(jnp.where(ridx == i1, -1.0, pe))
    den = p1 + p2
    w1 = pg_top * p1 / den
    w2 = pg_top * p2 / den
    e1 = grp * float(ne) + i1
    e2 = grp * float(ne) + i2

    eio = lax.broadcasted_iota(I32, (N_EXPERTS, sub), 0).astype(F32)
    r1 = []
    r2 = []
    carry = carry_ref[:, 0:1]
    for c in range(tc // sub):
        cs = slice(c * sub, (c + 1) * sub)
        oh1 = (eio == e1[:, cs]).astype(F32)
        oh2 = (eio == e2[:, cs]).astype(F32)
        oh = oh1 + oh2
        before = carry + _dot(oh.astype(BF16), tri_ref[...])
        r1.append(jnp.sum(oh1 * before, axis=0, keepdims=True))
        r2.append(jnp.sum(oh2 * before, axis=0, keepdims=True))
        carry = carry + jnp.sum(oh, axis=1, keepdims=True)
    carry_ref[...] = jnp.broadcast_to(carry, carry_ref.shape)
    cnt_ref[...] = jnp.broadcast_to(carry, cnt_ref.shape).astype(I32)
    r1 = jnp.concatenate(r1, axis=1)
    r2 = jnp.concatenate(r2, axis=1)
    zf = jnp.zeros((4, tc), F32)
    ids_ref[...] = jnp.concatenate([e1, e2, r1, r2, zf], axis=0).astype(I32)
    wts_ref[...] = jnp.concatenate([w1, w2, jnp.zeros((6, tc), F32)], axis=0)


def _route(lt, bias_col):
    n = lt.shape[1]
    tc = min(2048, n)
    sub = min(512, tc)
    tri = jnp.asarray(np.triu(np.ones((sub, sub), np.float32), 1), BF16)
    return pl.pallas_call(
        functools.partial(_route_kernel, sub=sub),
        grid=(n // tc,),
        in_specs=[pl.BlockSpec((LANES, tc), lambda i: (0, i)),
                  pl.BlockSpec((LANES, LANES), lambda i: (0, 0)),
                  pl.BlockSpec((sub, sub), lambda i: (0, 0))],
        out_specs=[pl.BlockSpec((8, tc), lambda i: (0, i)),
                   pl.BlockSpec((8, tc), lambda i: (0, i)),
                   pl.BlockSpec((N_EXPERTS, LANES), lambda i: (0, 0))],
        out_shape=[jax.ShapeDtypeStruct((8, n), I32),
                   jax.ShapeDtypeStruct((8, n), F32),
                   jax.ShapeDtypeStruct((N_EXPERTS, LANES), I32)],
        scratch_shapes=[pltpu.VMEM((N_EXPERTS, LANES), F32)],
        compiler_params=_cp(("arbitrary",)),
        name="route",
    )(lt, bias_col, tri)


def _dest_kernel(ids_ref, ps_ref, o_ref):
    ids = ids_ref[...].astype(F32)
    tc = ids.shape[1]
    eio = lax.broadcasted_iota(I32, (N_EXPERTS, tc), 0).astype(F32)
    ps = ps_ref[:, 0:1]
    rows = [jnp.sum(jnp.where(eio == ids[k:k + 1], ps, 0.0), axis=0, keepdims=True) + ids[2 + k:3 + k]
            for k in range(2)]
    o_ref[...] = jnp.concatenate(rows + [jnp.zeros((6, tc), F32)], axis=0).astype(I32)


def _dest(ids, pad_start):
    n = ids.shape[1]
    tc = min(2048, n)
    ps = jnp.broadcast_to(pad_start.astype(F32)[:, None], (N_EXPERTS, LANES))
    dest = pl.pallas_call(
        _dest_kernel,
        grid=(n // tc,),
        in_specs=[pl.BlockSpec((8, tc), lambda i: (0, i)),
                  pl.BlockSpec((N_EXPERTS, LANES), lambda i: (0, 0))],
        out_specs=pl.BlockSpec((8, tc), lambda i: (0, i)),
        out_shape=jax.ShapeDtypeStruct((8, n), I32),
        compiler_params=_cp(("arbitrary",)),
        name="dest",
    )(ids, ps)
    return dest[:2].T.reshape(2 * n)


def _dispatch_kernel(dst_ref, ps_ref, cnt_ref, h_ref, xs_ref, zbuf, sem, zsem, *, tcd, tm, zr):
    @pl.when(pl.program_id(0) == 0)
    def _():
        zbuf[...] = jnp.zeros_like(zbuf)
        sizes = [zr >> b for b in range(zr.bit_length()) if (zr >> b) >= SUBLANES]

        def zero_rows(row, size):
            return pltpu.make_async_copy(zbuf.at[pl.ds(0, size)],
                                         xs_ref.at[pl.ds(pl.multiple_of(row, SUBLANES), size)], zsem)

        def zero_row(row):
            return pltpu.make_async_copy(zbuf.at[pl.ds(0, 1)], xs_ref.at[pl.ds(row, 1)], zsem)

        def fill(e, wait):
            cnt = cnt_ref[e]
            cnt8 = lax.div(cnt + (SUBLANES - 1), SUBLANES) * SUBLANES
            gap = lax.rem(tm - lax.rem(cnt8, tm), tm)
            base = ps_ref[e]

            def single(c, _):
                cp = zero_row(base + cnt + c)
                cp.wait() if wait else cp.start()
                return 0

            lax.fori_loop(0, cnt8 - cnt, single, 0)
            row = base + cnt8
            for size in sizes:
                has = lax.rem(lax.div(gap, size), 2) == 1

                @pl.when(has)
                def _():
                    cp = zero_rows(row, size)
                    cp.wait() if wait else cp.start()

                row = row + jnp.where(has, size, 0)

        last = N_EXPERTS - 1
        used_rows = ps_ref[last] + lax.div(cnt_ref[last] + (tm - 1), tm) * tm
        n_tail = lax.div(xs_ref.shape[0] - used_rows, zr)

        def tail(c, wait):
            cp = zero_rows(used_rows + c * zr, zr)
            cp.wait() if wait else cp.start()

        for wait in (False, True):
            lax.fori_loop(0, N_EXPERTS, lambda e, _: (fill(e, wait), 0)[1], 0)
            lax.fori_loop(0, n_tail, lambda c, _: (tail(c, wait), 0)[1], 0)

    def issue(t, _):
        for k in range(2):
            dst = dst_ref[2 * t + k]
            pltpu.make_async_copy(h_ref.at[pl.ds(t, 1)], xs_ref.at[pl.ds(dst, 1)], sem).start()
        return 0

    lax.fori_loop(0, tcd, issue, 0, unroll=8)
    for k in range(2):
        pltpu.make_async_copy(h_ref, xs_ref.at[pl.ds(0, tcd)], sem).wait()


def _dispatch(dest, pad_start, counts, h2, rows, tcd, tm):
    n, d = h2.shape
    assert tm & (tm - 1) == 0 and tm >= 2 * SUBLANES
    zr = tm // 2
    return pl.pallas_call(
        functools.partial(_dispatch_kernel, tcd=tcd, tm=tm, zr=zr),
        grid=(n // tcd,),
        in_specs=[pl.BlockSpec((2 * tcd,), lambda i: (i,), memory_space=pltpu.SMEM),
                  pl.BlockSpec(memory_space=pltpu.SMEM),
                  pl.BlockSpec(memory_space=pltpu.SMEM),
                  pl.BlockSpec((tcd, d), lambda i: (i, 0))],
        out_specs=pl.BlockSpec(memory_space=pl.ANY),
        out_shape=jax.ShapeDtypeStruct((rows, d), h2.dtype),
        scratch_shapes=[pltpu.VMEM((zr, d), h2.dtype), pltpu.SemaphoreType.DMA(()),
                        pltpu.SemaphoreType.DMA(())],
        compiler_params=_cp(("arbitrary",)),
        name="dispatch",
    )(dest, pad_start, counts, h2)


def _experts_kernel(be_ref, ord_ref, nxt_ref, nu_ref, xs_ref, wg_hbm, wu_hbm, wd_hbm, ys_ref,
                    g_f, u_f, d_f, g_s, u_s, d_s, sem):
    i = pl.program_id(0)
    used = i < nu_ref[0]
    e = be_ref[i]
    fresh = (i == 0) | (e != be_ref[jnp.maximum(i - 1, 0)])
    slot = lax.rem(ord_ref[i], 2)

    def fetch(expert, s):
        return [pltpu.make_async_copy(w.at[expert], f.at[s], sem.at[s, k])
                for k, (w, f) in enumerate(((wg_hbm, g_f), (wu_hbm, u_f), (wd_hbm, d_f)))]

    @pl.when(used & (i == 0))
    def _():
        for c in fetch(e, slot):
            c.start()

    @pl.when(used & fresh)
    def _():
        for c in fetch(e, slot):
            c.wait()
        nxt = nxt_ref[i]

        @pl.when(nxt >= 0)
        def _():
            for c in fetch(nxt, 1 - slot):
                c.start()

        g_s[...] = g_f[slot].astype(BF16)
        u_s[...] = u_f[slot].astype(BF16)
        d_s[...] = d_f[slot].astype(BF16)

    @pl.when(used)
    def _():
        x = xs_ref[...].astype(BF16)
        a = _silu(_dot(x, g_s[...])) * _dot(x, u_s[...])
        ys_ref[...] = _dot(a.astype(BF16), d_s[...])

    @pl.when(jnp.logical_not(used))
    def _():
        ys_ref[...] = jnp.zeros_like(ys_ref)


def _experts(blk_exp, blk_ord, blk_nxt, n_used, xs, w_gate, w_up, w_down, tm):
    d = xs.shape[1]
    de = w_gate.shape[2]
    nb = blk_exp.shape[0]
    row = lambda i, be, od, nx, nu: (jnp.minimum(i, nu[0] - 1), 0)
    anyspec = pl.BlockSpec(memory_space=pl.ANY)
    grid_spec = pltpu.PrefetchScalarGridSpec(
        num_scalar_prefetch=4,
        grid=(nb,),
        in_specs=[pl.BlockSpec((tm, d), row), anyspec, anyspec, anyspec],
        out_specs=pl.BlockSpec((tm, d), lambda i, be, od, nx, nu: (i, 0)),
        scratch_shapes=[pltpu.VMEM((2, d, de), F32), pltpu.VMEM((2, d, de), F32),
                        pltpu.VMEM((2, de, d), F32),
                        pltpu.VMEM((d, de), BF16), pltpu.VMEM((d, de), BF16),
                        pltpu.VMEM((de, d), BF16),
                        pltpu.SemaphoreType.DMA((2, 3))],
    )
    return pl.pallas_call(
        _experts_kernel,
        grid_spec=grid_spec,
        out_shape=jax.ShapeDtypeStruct((nb * tm, d), F32),
        compiler_params=_cp(("arbitrary",)),
        name="experts",
    )(blk_exp, blk_ord, blk_nxt, n_used, xs, w_gate, w_up, w_down)


def _combine_kernel(ids_ref, idn_ref, ys_ref, x1_ref, wt_ref, mod_ref, g_ref, o_ref,
                    buf, sem, *, tc):
    i = pl.program_id(0)
    slot = lax.rem(i, 2)

    def gather(id_ref, s):
        def issue(t, _):
            for k in range(2):
                src = id_ref[2 * t + k]
                pltpu.make_async_copy(ys_ref.at[pl.ds(src, 1)], buf.at[s, k, pl.ds(t, 1)],
                                      sem.at[s]).start()
            return 0

        lax.fori_loop(0, tc, issue, 0, unroll=8)

    @pl.when(i == 0)
    def _():
        gather(ids_ref, 0)

    @pl.when(i + 1 < pl.num_programs(0))
    def _():
        gather(idn_ref, 1 - slot)

    for k in range(2):
        pltpu.make_async_copy(ys_ref.at[pl.ds(0, tc)], buf.at[slot, k], sem.at[slot]).wait()

    moe = buf[slot, 0] * wt_ref[:, 0:1] + buf[slot, 1] * wt_ref[:, 1:2]
    x2 = x1_ref[...] + mod_ref[0, 5:6, :] * moe
    o_ref[...] = x2 * lax.rsqrt(jnp.mean(x2 * x2, axis=-1, keepdims=True) + EPS) * g_ref[...]


def _combine(dest, ys, x1, wts_t, mod3, final_g, seq, tc):
    n, d = x1.shape
    last = n // tc - 1
    return pl.pallas_call(
        functools.partial(_combine_kernel, tc=tc),
        grid=(n // tc,),
        in_specs=[pl.BlockSpec((2 * tc,), lambda i: (i,), memory_space=pltpu.SMEM),
                  pl.BlockSpec((2 * tc,), lambda i: (jnp.minimum(i + 1, last),),
                               memory_space=pltpu.SMEM),
                  pl.BlockSpec(memory_space=pl.ANY),
                  pl.BlockSpec((tc, d), lambda i: (i, 0)),
                  pl.BlockSpec((tc, LANES), lambda i: (i, 0)),
                  pl.BlockSpec((1, 6, d), lambda i: ((i * tc) // seq, 0, 0)),
                  pl.BlockSpec((1, d), lambda i: (0, 0))],
        out_specs=pl.BlockSpec((tc, d), lambda i: (i, 0)),
        out_shape=jax.ShapeDtypeStruct((n, d), F32),
        scratch_shapes=[pltpu.VMEM((2, 2, tc, d), F32), pltpu.SemaphoreType.DMA((2,))],
        compiler_params=_cp(("arbitrary",)),
        name="combine",
    )(dest, dest, ys, x1, wts_t, mod3, final_g.reshape(1, d))


def _token_mixer(x2, mod3, norm1_g, w_in, ret_gn_g, cmp_pos_k, cmp_w1_k, cmp_w2_k,
                 cmp_pos_v, cmp_w1_v, cmp_w2_v, bsz, seq):
    n, d = x2.shape
    w_gate_cols = jnp.pad(w_in[:, PROJ_MAIN:], ((0, 0), (0, LANES - N_GATE_COLS))).astype(BF16)
    proj, gate_logits = _inproj(x2, mod3, norm1_g, w_in, w_gate_cols, seq)

    o_ret = _retention(proj, ret_gn_g, bsz, seq)

    g_ = NSA_KV_GROUPS
    crow = seq // CMP_STRIDE

    src = proj[:, _C_KC * LANES:(_C_KC + 2 * g_) * LANES].reshape(bsz, crow, CMP_STRIDE, 2 * g_, NSA_D)
    src = src.transpose(3, 0, 1, 2, 4).reshape(2 * g_, bsz * crow, CMP_STRIDE * NSA_D)
    pe2 = lambda pe: pe.reshape(2, CMP_STRIDE * NSA_D)
    kc, vc = _compress(src, pe2(cmp_pos_k), pe2(cmp_pos_v),
                       cmp_w1_k, cmp_w1_v, cmp_w2_k, cmp_w2_v, bsz, crow)

    gl = gate_logits[:, :N_GATE_COLS].reshape(n, g_, NSA_HPG * 3).transpose(1, 0, 2)
    gates_g = jnp.pad(gl, ((0, 0), (0, 0), (0, LANES - NSA_HPG * 3)))
    o_nsa = _nsa(proj, kc, vc, gates_g, bsz, seq)
    return o_ret, o_nsa


def _moe(h2, lt, x1, mod3, final_g, b_grp, b_exp, w_gate, w_up, w_down, seq, tm):
    n, d = h2.shape
    bias_col = jnp.zeros((LANES,), F32).at[:N_GROUPS].set(b_grp).at[N_GROUPS:N_GROUPS + N_EXPERTS].set(b_exp)
    bias_col = jnp.broadcast_to(bias_col[:, None], (LANES, LANES))
    ids, wts, cnt = _route(lt, bias_col)

    counts = cnt[:, 0]
    padded = (counts + tm - 1) // tm * tm
    pad_end = jnp.cumsum(padded)
    pad_start = (pad_end - padded).astype(I32)
    nb = (2 * n) // tm + N_EXPERTS
    n_used = (pad_end[-1] // tm).astype(I32).reshape(1)
    blk_start = jnp.arange(nb, dtype=I32) * tm
    blk_exp = jnp.minimum(jnp.sum((pad_end[None, :] <= blk_start[:, None]).astype(I32), axis=1),
                          N_EXPERTS - 1).astype(I32)
    last_exp = blk_exp[jnp.maximum(n_used[0] - 1, 0)]
    blk_exp = jnp.where(jnp.arange(nb) < n_used[0], blk_exp, last_exp)
    eid = jnp.arange(N_EXPERTS, dtype=I32)
    has = counts > 0
    exp_ord = jnp.sum((has[None, :] & (eid[None, :] < eid[:, None])).astype(I32), axis=1)
    exp_nxt = jnp.min(jnp.where(has[None, :] & (eid[None, :] > eid[:, None]), eid[None, :], N_EXPERTS), axis=1)
    exp_nxt = jnp.where(exp_nxt < N_EXPERTS, exp_nxt, -1).astype(I32)
    blk_ord = exp_ord[blk_exp]
    blk_nxt = exp_nxt[blk_exp]

    dest = _dest(ids, pad_start)
    xs = _dispatch(dest, pad_start, counts.astype(I32), h2, nb * tm, min(512, n), tm)
    ys = _experts(blk_exp, blk_ord, blk_nxt, n_used, xs, w_gate, w_up, w_down, tm)
    wts_t = jnp.pad(wts[:2].T, ((0, 0), (0, LANES - 2)))
    return _combine(dest, ys, x1, wts_t, mod3, final_g, seq, min(256, n))


def kernel(x, c, w_ada, b_ada, norm1_g, norm2_g, final_g, w_in, ret_gn_g, cmp_pos_k, cmp_w1_k,
           cmp_w2_k, cmp_pos_v, cmp_w1_v, cmp_w2_v, w_out, w_grp, b_grp, w_exp, b_exp, w_gate,
           w_up, w_down):
    bsz, seq, d = x.shape
    assert w_ada.shape[0] == 1, "single-layer block"
    n = bsz * seq
    x2 = x.reshape(n, d)
    mod3 = _ada(c, w_ada[0], b_ada[0]).reshape(bsz, 6, d)

    o_ret, o_nsa = _token_mixer(x2, mod3, norm1_g[0], w_in[0], ret_gn_g[0], cmp_pos_k[0],
                                cmp_w1_k[0], cmp_w2_k[0], cmp_pos_v[0], cmp_w1_v[0], cmp_w2_v[0],
                                bsz, seq)

    w_route = jnp.concatenate([w_grp[0], w_exp[0]], axis=1)
    w_route = jnp.pad(w_route, ((0, 0), (0, LANES - w_route.shape[1]))).T.astype(BF16)
    x1, h2, lt = _outproj(o_ret, o_nsa, x2, mod3, norm2_g[0], w_out[0].astype(BF16), w_route, seq)

    out = _moe(h2, lt, x1, mod3, final_g, b_grp[0], b_exp[0], w_gate[0], w_up[0], w_down[0],
               seq, 256)
    return out.reshape(bsz, seq, d)
```

```python
import functools
import math

import numpy as np
import jax
import jax.numpy as jnp
from jax import lax
from jax.experimental import pallas as pl
from jax.experimental.pallas import tpu as pltpu

F32 = jnp.float32
BF16 = jnp.bfloat16
I32 = jnp.int32

RET_HEADS = 4
RET_DK = 256
RET_DV = 256
RET_CHUNK = 128
NSA_HEADS = 8
NSA_KV_GROUPS = 2
NSA_HPG = NSA_HEADS // NSA_KV_GROUPS
NSA_D = 128
CMP_BLOCK = 32
CMP_STRIDE = 16
CMP_HIDDEN = 256
SEL_BLOCK = 64
SEL_COUNT = 16
WIN = 512
N_GROUPS = 8
EXP_PER_GROUP = 8
N_EXPERTS = N_GROUPS * EXP_PER_GROUP
D_EXPERT = 512
ROPE_BASE = 10000.0
EPS = 1e-6
NEG = -1e30
FORCE_BONUS = 1e4

LANES = 128
SUBLANES = 8
VMEM_LIMIT = 52 * 1024 * 1024

_C_RQ, _C_RK, _C_RV, _C_RG = 0, 8, 16, 24
_C_NQ = 32
_C_KC, _C_VC, _C_KS, _C_VS, _C_KW, _C_VW = 40, 42, 44, 46, 48, 50
_C_GATE = 52
PROJ_MAIN = _C_GATE * LANES
N_GATE_COLS = NSA_HEADS * 3


def _cp(sem, vmem=VMEM_LIMIT):
    return pltpu.CompilerParams(dimension_semantics=sem, vmem_limit_bytes=vmem)


def _silu(v):
    return v * jax.nn.sigmoid(v)


def _dot(a, b):
    return jnp.dot(a, b, preferred_element_type=F32)


def _dot_nt(a, b):
    return lax.dot_general(a, b, (((1,), (1,)), ((), ())), preferred_element_type=F32)


def _dot_tn(a, b):
    return lax.dot_general(a, b, (((0,), (0,)), ((), ())), preferred_element_type=F32)


def _ada_kernel(c_ref, w_ref, b_ref, o_ref):
    ca = _silu(c_ref[...])
    o_ref[...] = jnp.dot(ca, w_ref[...], preferred_element_type=F32,
                         precision=lax.Precision.HIGHEST) + b_ref[...]


def _ada(c, w, b):
    bsz, d = c.shape
    n = w.shape[1]
    tn = min(1024, n)
    return pl.pallas_call(
        _ada_kernel,
        grid=(n // tn,),
        in_specs=[pl.BlockSpec((bsz, d), lambda j: (0, 0)),
                  pl.BlockSpec((d, tn), lambda j: (0, j)),
                  pl.BlockSpec((1, tn), lambda j: (0, j))],
        out_specs=pl.BlockSpec((bsz, tn), lambda j: (0, j)),
        out_shape=jax.ShapeDtypeStruct((bsz, n), F32),
        compiler_params=_cp(("arbitrary",)),
        name="ada",
    )(c, w, b.reshape(1, n))


def _inproj_kernel(x_ref, mod_ref, g_ref, wt_ref, wgt_ref, proj_ref, gate_ref, h_ref):
    @pl.when(pl.program_id(1) == 0)
    def _():
        x = x_ref[...]
        y = x * lax.rsqrt(jnp.mean(x * x, axis=-1, keepdims=True) + EPS) * g_ref[...]
        h = (y * (1.0 + mod_ref[0, 1:2, :]) + mod_ref[0, 0:1, :]).astype(BF16)
        h_ref[...] = h
        gate_ref[...] = _dot_nt(h, wgt_ref[...])

    proj_ref[...] = _dot_nt(h_ref[...], wt_ref[...].astype(BF16)).astype(BF16)


def _inproj(x2, mod3, g, w_t, wg_t, seq):
    n, d = x2.shape
    tm = min(1024, seq)
    tn = 512
    nj = PROJ_MAIN // tn
    return pl.pallas_call(
        _inproj_kernel,
        grid=(n // tm, nj),
        in_specs=[pl.BlockSpec((tm, d), lambda i, j: (i, 0)),
                  pl.BlockSpec((1, 6, d), lambda i, j: ((i * tm) // seq, 0, 0)),
                  pl.BlockSpec((1, d), lambda i, j: (0, 0)),
                  pl.BlockSpec((tn, d), lambda i, j: (j, 0)),
                  pl.BlockSpec((LANES, d), lambda i, j: (0, 0))],
        out_specs=[pl.BlockSpec((tm, tn), lambda i, j: (i, j)),
                   pl.BlockSpec((tm, LANES), lambda i, j: (i, 0))],
        out_shape=[jax.ShapeDtypeStruct((n, PROJ_MAIN), BF16),
                   jax.ShapeDtypeStruct((n, LANES), F32)],
        scratch_shapes=[pltpu.VMEM((tm, d), BF16)],
        compiler_params=_cp(("arbitrary", "arbitrary")),
        name="inproj",
    )(x2, mod3, g.reshape(1, d), w_t, wg_t)


def _retention_kernel(q_ref, k_ref, v_ref, g_ref, cos_ref, sin_ref, din_ref, zeta_ref,
                      qdec_ref, cdec_ref, gn_ref, o_ref, s_ref):
    @pl.when(pl.program_id(1) == 0)
    def _():
        s_ref[...] = jnp.zeros_like(s_ref)

    half = RET_DK // 2
    c = RET_CHUNK

    for sub in range(q_ref.shape[0] // c):
        rows = slice(sub * c, (sub + 1) * c)
        cos = cos_ref[rows, :]
        sin = sin_ref[rows, :]

        def rot(a):
            a1, a2 = a[:, :half], a[:, half:]
            return jnp.concatenate([a1 * cos - a2 * sin, a1 * sin + a2 * cos], axis=1)

        for h in range(RET_HEADS):
            qs = slice(h * RET_DK, (h + 1) * RET_DK)
            vs = slice(h * RET_DV, (h + 1) * RET_DV)
            q = rot(q_ref[rows, qs].astype(F32))
            k = rot(k_ref[rows, qs].astype(F32)) * (RET_DK ** -0.5)
            v = v_ref[rows, vs]
            qb = q.astype(BF16)
            kb = k.astype(BF16)
            s = _dot_nt(qb, kb) * din_ref[h]
            inner = _dot(s.astype(BF16), v)
            s_prev = s_ref[h]
            cross = _dot(qb, s_prev.astype(BF16)) * qdec_ref[h]
            kv = _dot_tn((k * zeta_ref[h]).astype(BF16), v)
            s_ref[h] = cdec_ref[h] * s_prev + kv
            o = inner + cross
            mu = jnp.mean(o, axis=-1, keepdims=True)
            oc = o - mu
            var = jnp.mean(oc * oc, axis=-1, keepdims=True)
            o = oc * lax.rsqrt(var + EPS) * gn_ref[:, vs]
            o_ref[rows, vs] = (o * _silu(g_ref[rows, vs].astype(F32))).astype(BF16)


def _retention(proj, gn_g, bsz, seq):
    n = proj.shape[0]
    c = RET_CHUNK
    nc = seq // c
    hw = RET_HEADS * RET_DK
    half = RET_DK // 2
    pos = jnp.arange(seq, dtype=F32)
    inv = ROPE_BASE ** (-jnp.arange(half, dtype=F32) / half)
    ang = pos[:, None] * inv[None, :]
    cos, sin = jnp.cos(ang), jnp.sin(ang)
    log_gamma = jnp.log1p(-jnp.exp2(-5.0 - jnp.arange(RET_HEADS, dtype=F32)))
    idx = jnp.arange(c, dtype=F32)
    rel = idx[:, None] - idx[None, :]
    decay_in = jnp.where(rel >= 0, jnp.exp(log_gamma[:, None, None] * jnp.maximum(rel, 0.0)), 0.0)
    zeta = jnp.exp(log_gamma[:, None] * (c - 1 - idx)[None, :])[:, :, None]
    q_decay = jnp.exp(log_gamma[:, None] * (idx + 1)[None, :])[:, :, None]
    chunk_decay = jnp.exp(log_gamma * c)[:, None, None]
    rb = min(4 * c, seq)
    ns = seq // rb
    row = lambda b, t: (b * ns + t)
    return pl.pallas_call(
        _retention_kernel,
        grid=(bsz, ns),
        in_specs=[pl.BlockSpec((rb, hw), lambda b, t: (row(b, t), 0)),
                  pl.BlockSpec((rb, hw), lambda b, t: (row(b, t), 1)),
                  pl.BlockSpec((rb, hw), lambda b, t: (row(b, t), 2)),
                  pl.BlockSpec((rb, hw), lambda b, t: (row(b, t), 3)),
                  pl.BlockSpec((rb, half), lambda b, t: (t, 0)),
                  pl.BlockSpec((rb, half), lambda b, t: (t, 0)),
                  pl.BlockSpec((RET_HEADS, c, c), lambda b, t: (0, 0, 0)),
                  pl.BlockSpec((RET_HEADS, c, 1), lambda b, t: (0, 0, 0)),
                  pl.BlockSpec((RET_HEADS, c, 1), lambda b, t: (0, 0, 0)),
                  pl.BlockSpec((RET_HEADS, 1, 1), lambda b, t: (0, 0, 0)),
                  pl.BlockSpec((1, hw), lambda b, t: (0, 0))],
        out_specs=pl.BlockSpec((rb, hw), lambda b, t: (row(b, t), 0)),
        out_shape=jax.ShapeDtypeStruct((n, hw), BF16),
        scratch_shapes=[pltpu.VMEM((RET_HEADS, RET_DK, RET_DV), F32)],
        compiler_params=_cp(("arbitrary", "arbitrary")),
        name="retention",
    )(proj, proj, proj, proj, cos, sin, decay_in, zeta, q_decay, chunk_decay,
      gn_g.reshape(1, hw))


def _compress_kernel(ak_ref, av_ref, pek_ref, pev_ref, w1k_ref, w1v_ref, w2k_ref, w2v_ref,
                     kc_ref, vc_ref, a_scr):
    seq = ak_ref.shape[0]
    nblk = seq // CMP_STRIDE
    a_scr[seq:, :] = jnp.zeros((a_scr.shape[0] - seq, NSA_D), F32)

    def one(a_ref, pe_ref, w1_ref, w2_ref, o_ref):
        a_scr[:seq, :] = a_ref[...].astype(F32)
        flat = jnp.concatenate([a_scr[pl.ds(l, nblk, stride=CMP_STRIDE), :] for l in range(CMP_BLOCK)],
                               axis=1)
        pre = _dot((flat + pe_ref[...]).astype(BF16), w1_ref[...].astype(BF16))
        o_ref[...] = _dot(_silu(pre).astype(BF16), w2_ref[...].astype(BF16)).astype(BF16)

    one(ak_ref, pek_ref, w1k_ref, w2k_ref, kc_ref)
    one(av_ref, pev_ref, w1v_ref, w2v_ref, vc_ref)


def _compress(proj, pek, pev, w1k, w1v, w2k, w2v, bsz, seq):
    g_ = NSA_KV_GROUPS
    nblk = seq // CMP_STRIDE
    full = lambda a: pl.BlockSpec(a.shape, lambda i: (0,) * a.ndim)
    kblk = pl.BlockSpec((seq, NSA_D), lambda i: (i // g_, _C_KC + i % g_))
    vblk = pl.BlockSpec((seq, NSA_D), lambda i: (i // g_, _C_VC + i % g_))
    oblk = pl.BlockSpec((nblk, NSA_D), lambda i: (i, 0))
    return pl.pallas_call(
        _compress_kernel,
        grid=(bsz * g_,),
        in_specs=[kblk, vblk, full(pek), full(pev), full(w1k), full(w1v), full(w2k), full(w2v)],
        out_specs=[oblk, oblk],
        out_shape=[jax.ShapeDtypeStruct((bsz * g_ * nblk, NSA_D), BF16)] * 2,
        scratch_shapes=[pltpu.VMEM((seq + CMP_BLOCK, NSA_D), F32)],
        compiler_params=_cp(("arbitrary",)),
        name="compress",
    )(proj, proj, pek, pev, w1k, w1v, w2k, w2v)


def _nsa_kernel(q_ref, kc_ref, vc_ref, ks_ref, vs_ref, kw_ref, vw_ref, gate_ref, ovt_ref,
                o_ref, kaug_ref, vst_ref, vwt_ref, *, tq, tk, vt, seq, ncmp, wlen):
    crow = kc_ref.shape[0]
    i = pl.program_id(2)
    hg = NSA_HPG
    d = NSA_D
    r = hg * tq

    @pl.when(i == 0)
    def _():
        kaug_ref[:, :d] = ks_ref[...]
        blk = lax.broadcasted_iota(I32, (seq, LANES), 0) // SEL_BLOCK
        lane = lax.broadcasted_iota(I32, (seq, LANES), 1)
        kaug_ref[:, d:] = (blk == lane).astype(BF16)
        for c in range(seq // vt):
            vst_ref[c] = vs_ref[c * vt:(c + 1) * vt, :].astype(F32).T.astype(BF16)
            vwt_ref[c] = vw_ref[c * vt:(c + 1) * vt, :].astype(F32).T.astype(BF16)

    def pv_t(vt_ref, first_tile, p):
        out = None
        for c in range(p.shape[0] // vt):
            part = _dot(vt_ref[first_tile + c], p[c * vt:(c + 1) * vt, :])
            out = part if out is None else out + part
        return out

    q = q_ref[...]
    qh = [(q[:, h * d:(h + 1) * d].astype(F32) * (d ** -0.5)).astype(BF16) for h in range(hg)]
    qa = jnp.concatenate(qh, axis=0)
    t1 = i * tq + lax.broadcasted_iota(I32, (tq, 1), 0)
    tpos = jnp.concatenate([t1] * hg, axis=0)
    t_row = i * tq + lax.broadcasted_iota(I32, (1, tq), 1)
    tpos_row = jnp.concatenate([t_row] * hg, axis=1)

    w0 = pl.multiple_of(jnp.maximum(i * tq + tq - wlen, 0), vt)
    sw = _dot_nt(kw_ref[pl.ds(w0, wlen), :], qa)
    delta = tpos_row - (w0 + lax.broadcasted_iota(I32, (wlen, 1), 0))
    sw = jnp.where((delta >= 0) & (delta < WIN), sw, NEG)
    e_w = jnp.exp(sw - jnp.max(sw, axis=0, keepdims=True))
    o_win = (pv_t(vwt_ref, w0 // vt, e_w.astype(BF16)) / jnp.sum(e_w, axis=0, keepdims=True)).T

    sct = _dot_nt(kc_ref[...], qa)
    cidx = lax.broadcasted_iota(I32, (crow, 1), 0)
    cmask = (cidx * CMP_STRIDE + (CMP_BLOCK - 1) <= tpos_row) & (cidx < ncmp)
    sct = jnp.where(cmask, sct, NEG)
    e_c = jnp.where(cmask, jnp.exp(sct - jnp.max(sct, axis=0, keepdims=True)), 0.0)
    den_c = jnp.sum(e_c, axis=0, keepdims=True)
    p_t = jnp.where(den_c > 0.0, e_c / jnp.where(den_c > 0.0, den_c, 1.0), 0.0)
    o_cmp = _dot_tn(p_t.astype(BF16), vc_ref[...])

    psum_t = p_t[:, 0:tq]
    for h in range(1, hg):
        psum_t = psum_t + p_t[:, h * tq:(h + 1) * tq]
    nsel_blocks = seq // SEL_BLOCK
    nsb = ovt_ref.shape[0]
    imp_t = jnp.dot(ovt_ref[...], psum_t, preferred_element_type=F32,
                    precision=lax.Precision.HIGHEST)
    sidx = lax.broadcasted_iota(I32, (nsb, 1), 0)
    cur = t_row // SEL_BLOCK
    valid = sidx <= cur
    forced = (sidx == 0) | (sidx == cur) | (sidx == cur - 1)
    score = jnp.where(valid, imp_t + jnp.where(forced, FORCE_BONUS, 0.0), -1.0)
    rank = jnp.zeros((nsb, tq), F32)
    for s2 in range(nsel_blocks):
        row = score[s2:s2 + 1, :]
        beats = (row > score) | ((row == score) & (sidx > s2))
        rank = rank + beats.astype(F32)
    sel = valid & (rank < float(min(SEL_COUNT, nsel_blocks)))
    selb_t = jnp.where(sel, 0.0, NEG)
    selb_t = jnp.concatenate([selb_t, jnp.zeros((LANES - nsb, tq), F32)], axis=0)
    selb = selb_t.T.astype(BF16)
    q_aug = jnp.concatenate([jnp.concatenate([qh[h], selb], axis=1) for h in range(hg)], axis=0)

    n_full = (i * tq + 1) // tk
    kl = lax.broadcasted_iota(I32, (tk, 1), 0)

    def tile(j, carry, masked):
        m, l, acc = carry
        k0 = pl.multiple_of(j * tk, tk)
        s = _dot_nt(kaug_ref[pl.ds(k0, tk), :], q_aug)
        if masked:
            s = jnp.where(k0 + kl <= tpos_row, s, NEG)
        m_new = jnp.maximum(m, jnp.max(s, axis=0, keepdims=True))
        alpha = jnp.exp(m - m_new)
        p = jnp.exp(s - m_new)
        l = alpha * l + jnp.sum(p, axis=0, keepdims=True)
        acc = alpha * acc + pv_t(vst_ref, j * (tk // vt), p.astype(BF16))
        return m_new, l, acc

    m0 = jnp.full((1, r), NEG, F32)
    l0 = jnp.zeros((1, r), F32)
    a0 = jnp.zeros((d, r), F32)
    carry = lax.fori_loop(0, n_full, functools.partial(tile, masked=False), (m0, l0, a0))
    _, l_s, acc_s = tile(n_full, carry, True)
    o_slc = (acc_s / l_s).T

    gt = jax.nn.sigmoid(gate_ref[...])
    grp = pl.program_id(1)
    for g in range(1, NSA_KV_GROUPS):
        gt = jnp.where(grp == g, pltpu.roll(gt, LANES - g * 3 * hg, 1), gt)
    for h in range(hg):
        rows = slice(h * tq, (h + 1) * tq)
        o = (gt[:, 3 * h:3 * h + 1] * o_cmp[rows] + gt[:, 3 * h + 1:3 * h + 2] * o_slc[rows]
             + gt[:, 3 * h + 2:3 * h + 3] * o_win[rows])
        o_ref[:, h * d:(h + 1) * d] = o.astype(BF16)


def _nsa(proj, kc, vc, gates_g, bsz, seq):
    n = proj.shape[0]
    g_ = NSA_KV_GROUPS
    tq = 256
    tk = min(512, seq)
    nq = seq // tq
    ncmp = (seq - CMP_BLOCK) // CMP_STRIDE + 1
    wlen = min(WIN + tq, seq)
    nsel = seq // SEL_BLOCK
    crow = seq // CMP_STRIDE
    assert nsel <= LANES and crow <= LANES and tq <= tk and tk % tq == 0
    nsb = -(-nsel // 8) * 8
    ss = np.arange(nsb)[:, None] * SEL_BLOCK
    cs = np.arange(crow)[None, :] * CMP_STRIDE
    ov = ((cs < ss + SEL_BLOCK) & (cs + CMP_BLOCK > ss)
          & (np.arange(crow)[None, :] < ncmp) & (np.arange(nsb)[:, None] < nsel))
    ov = jnp.asarray(ov.astype(np.float32))
    kvspec = lambda c0: pl.BlockSpec((seq, NSA_D), lambda b, g, i: (b, c0 + g))
    vt = 256
    assert tk % vt == 0 and wlen % vt == 0 and tq % vt == 0 and seq % vt == 0
    kern = functools.partial(_nsa_kernel, tq=tq, tk=tk, vt=vt, seq=seq, ncmp=ncmp, wlen=wlen)
    return pl.pallas_call(
        kern,
        grid=(bsz, g_, nq),
        in_specs=[pl.BlockSpec((tq, NSA_HPG * NSA_D), lambda b, g, i: (b * nq + i, _C_NQ // NSA_HPG + g)),
                  pl.BlockSpec((crow, NSA_D), lambda b, g, i: (b * g_ + g, 0)),
                  pl.BlockSpec((crow, NSA_D), lambda b, g, i: (b * g_ + g, 0)),
                  kvspec(_C_KS), kvspec(_C_VS), kvspec(_C_KW), kvspec(_C_VW),
                  pl.BlockSpec((tq, LANES), lambda b, g, i: (b * nq + i, 0)),
                  pl.BlockSpec((nsb, crow), lambda b, g, i: (0, 0))],
        out_specs=pl.BlockSpec((tq, NSA_HPG * NSA_D), lambda b, g, i: (b * nq + i, g)),
        out_shape=jax.ShapeDtypeStruct((n, NSA_HEADS * NSA_D), BF16),
        scratch_shapes=[pltpu.VMEM((seq, 2 * NSA_D), BF16),
                        pltpu.VMEM((seq // vt, NSA_D, vt), BF16),
                        pltpu.VMEM((seq // vt, NSA_D, vt), BF16)],
        compiler_params=_cp(("arbitrary", "arbitrary", "arbitrary")),
        name="nsa",
    )(proj, kc, vc, proj, proj, proj, proj, gates_g, ov)


def _outproj_kernel(oret_ref, onsa_ref, x_ref, mod_ref, g_ref, w_ref, wr_ref,
                    x1_ref, h2_ref, lt_ref):
    hw = oret_ref.shape[1]
    mix = _dot(oret_ref[...], w_ref[:hw, :]) + _dot(onsa_ref[...], w_ref[hw:, :])
    x1 = x_ref[...] + mod_ref[0, 2:3, :] * mix
    x1_ref[...] = x1
    y = x1 * lax.rsqrt(jnp.mean(x1 * x1, axis=-1, keepdims=True) + EPS) * g_ref[...]
    h2 = y * (1.0 + mod_ref[0, 4:5, :]) + mod_ref[0, 3:4, :]
    h2_ref[...] = h2
    lt_ref[...] = _dot_nt(wr_ref[...], h2.astype(BF16))


def _outproj(o_ret, o_nsa, x2, mod3, g, w_bf, wr_bf, seq):
    n, d = x2.shape
    tm = min(512, seq)
    hw = o_ret.shape[1]
    return pl.pallas_call(
        _outproj_kernel,
        grid=(n // tm,),
        in_specs=[pl.BlockSpec((tm, hw), lambda i: (i, 0)),
                  pl.BlockSpec((tm, o_nsa.shape[1]), lambda i: (i, 0)),
                  pl.BlockSpec((tm, d), lambda i: (i, 0)),
                  pl.BlockSpec((1, 6, d), lambda i: ((i * tm) // seq, 0, 0)),
                  pl.BlockSpec((1, d), lambda i: (0, 0)),
                  pl.BlockSpec(w_bf.shape, lambda i: (0, 0)),
                  pl.BlockSpec(wr_bf.shape, lambda i: (0, 0))],
        out_specs=[pl.BlockSpec((tm, d), lambda i: (i, 0)),
                   pl.BlockSpec((tm, d), lambda i: (i, 0)),
                   pl.BlockSpec((LANES, tm), lambda i: (0, i))],
        out_shape=[jax.ShapeDtypeStruct((n, d), F32),
                   jax.ShapeDtypeStruct((n, d), F32),
                   jax.ShapeDtypeStruct((LANES, n), F32)],
        compiler_params=_cp(("arbitrary",)),
        name="outproj",
    )(o_ret, o_nsa, x2, mod3, g.reshape(1, d), w_bf, wr_bf)


def _route_kernel(lt_ref, b_ref, tri_ref, ids_ref, wts_ref, cnt_ref, carry_ref, *, sub):
    @pl.when(pl.program_id(0) == 0)
    def _():
        carry_ref[...] = jnp.zeros_like(carry_ref)

    ng, ne = N_GROUPS, EXP_PER_GROUP
    l = lt_ref[...] + b_ref[:, 0:1]
    tc = l.shape[1]
    ridx = lax.broadcasted_iota(I32, (ng, tc), 0).astype(F32)

    def softmax0(v):
        e = jnp.exp(v - jnp.max(v, axis=0, keepdims=True))
        return e / jnp.sum(e, axis=0, keepdims=True)

    def top1(p):
        top = jnp.max(p, axis=0, keepdims=True)
        idx = jnp.min(jnp.where(p == top, ridx, float(ng)), axis=0, keepdims=True)
        return top, idx

    pg_top, grp = top1(softmax0(l[0:ng]))
    leg = jnp.zeros((ne, tc), F32)
    for g in range(ng):
        leg = jnp.where(grp == float(g), l[ng + g * ne:ng + (g + 1) * ne], leg)
    pe = softmax0(leg)
    p1, i1 = top1(pe)
    p2, i2 = top1(jnp.where(ridx == i1, -1.0, pe))
    den = p1 + p2
    w1 = pg_top * p1 / den
    w2 = pg_top * p2 / den
    e1 = grp * float(ne) + i1
    e2 = grp * float(ne) + i2

    eio = lax.broadcasted_iota(I32, (N_EXPERTS, sub), 0).astype(F32)
    r1 = []
    r2 = []
    carry = carry_ref[:, 0:1]
    for c in range(tc // sub):
        cs = slice(c * sub, (c + 1) * sub)
        oh1 = (eio == e1[:, cs]).astype(F32)
        oh2 = (eio == e2[:, cs]).astype(F32)
        oh = oh1 + oh2
        before = carry + _dot(oh.astype(BF16), tri_ref[...])
        r1.append(jnp.sum(oh1 * before, axis=0, keepdims=True))
        r2.append(jnp.sum(oh2 * before, axis=0, keepdims=True))
        carry = carry + jnp.sum(oh, axis=1, keepdims=True)
    carry_ref[...] = jnp.broadcast_to(carry, carry_ref.shape)
    cnt_ref[...] = jnp.broadcast_to(carry, cnt_ref.shape).astype(I32)
    r1 = jnp.concatenate(r1, axis=1)
    r2 = jnp.concatenate(r2, axis=1)
    zf = jnp.zeros((4, tc), F32)
    ids_ref[...] = jnp.concatenate([e1, e2, r1, r2, zf], axis=0).astype(I32)
    wts_ref[...] = jnp.concatenate([w1, w2, jnp.zeros((6, tc), F32)], axis=0)


def _route(lt, bias_col):
    n = lt.shape[1]
    tc = min(2048, n)
    sub = min(512, tc)
    tri = jnp.asarray(np.triu(np.ones((sub, sub), np.float32), 1), BF16)
    return pl.pallas_call(
        functools.partial(_route_kernel, sub=sub),
        grid=(n // tc,),
        in_specs=[pl.BlockSpec((LANES, tc), lambda i: (0, i)),
                  pl.BlockSpec((LANES, LANES), lambda i: (0, 0)),
                  pl.BlockSpec((sub, sub), lambda i: (0, 0))],
        out_specs=[pl.BlockSpec((8, tc), lambda i: (0, i)),
                   pl.BlockSpec((8, tc), lambda i: (0, i)),
                   pl.BlockSpec((N_EXPERTS, LANES), lambda i: (0, 0))],
        out_shape=[jax.ShapeDtypeStruct((8, n), I32),
                   jax.ShapeDtypeStruct((8, n), F32),
                   jax.ShapeDtypeStruct((N_EXPERTS, LANES), I32)],
        scratch_shapes=[pltpu.VMEM((N_EXPERTS, LANES), F32)],
        compiler_params=_cp(("arbitrary",)),
        name="route",
    )(lt, bias_col, tri)


def _dest_kernel(ids_ref, ps_ref, o_ref):
    ids = ids_ref[...].astype(F32)
    tc = ids.shape[1]
    eio = lax.broadcasted_iota(I32, (N_EXPERTS, tc), 0).astype(F32)
    ps = ps_ref[:, 0:1]
    rows = [jnp.sum(jnp.where(eio == ids[k:k + 1], ps, 0.0), axis=0, keepdims=True) + ids[2 + k:3 + k]
            for k in range(2)]
    o_ref[...] = jnp.concatenate(rows + [jnp.zeros((6, tc), F32)], axis=0).astype(I32)


def _dest(ids, pad_start):
    n = ids.shape[1]
    tc = min(2048, n)
    ps = jnp.broadcast_to(pad_start.astype(F32)[:, None], (N_EXPERTS, LANES))
    dest = pl.pallas_call(
        _dest_kernel,
        grid=(n // tc,),
        in_specs=[pl.BlockSpec((8, tc), lambda i: (0, i)),
                  pl.BlockSpec((N_EXPERTS, LANES), lambda i: (0, 0))],
        out_specs=pl.BlockSpec((8, tc), lambda i: (0, i)),
        out_shape=jax.ShapeDtypeStruct((8, n), I32),
        compiler_params=_cp(("arbitrary",)),
        name="dest",
    )(ids, ps)
    return dest[:2].T.reshape(2 * n)


def _dispatch_kernel(dst_ref, ps_ref, cnt_ref, h_ref, xs_ref, zbuf, sem, zsem, *, tcd, tm, zr):
    @pl.when(pl.program_id(0) == 0)
    def _():
        zbuf[...] = jnp.zeros_like(zbuf)
        sizes = [zr >> b for b in range(zr.bit_length()) if (zr >> b) >= SUBLANES]

        def zero_rows(row, size):
            return pltpu.make_async_copy(zbuf.at[pl.ds(0, size)],
                                         xs_ref.at[pl.ds(pl.multiple_of(row, SUBLANES), size)], zsem)

        def zero_row(row):
            return pltpu.make_async_copy(zbuf.at[pl.ds(0, 1)], xs_ref.at[pl.ds(row, 1)], zsem)

        def fill(e, wait):
            cnt = cnt_ref[e]
            cnt8 = lax.div(cnt + (SUBLANES - 1), SUBLANES) * SUBLANES
            gap = lax.rem(tm - lax.rem(cnt8, tm), tm)
            base = ps_ref[e]

            def single(c, _):
                cp = zero_row(base + cnt + c)
                cp.wait() if wait else cp.start()
                return 0

            lax.fori_loop(0, cnt8 - cnt, single, 0)
            row = base + cnt8
            for size in sizes:
                has = lax.rem(lax.div(gap, size), 2) == 1

                @pl.when(has)
                def _():
                    cp = zero_rows(row, size)
                    cp.wait() if wait else cp.start()

                row = row + jnp.where(has, size, 0)

        last = N_EXPERTS - 1
        used_rows = ps_ref[last] + lax.div(cnt_ref[last] + (tm - 1), tm) * tm
        n_tail = lax.div(xs_ref.shape[0] - used_rows, zr)

        def tail(c, wait):
            cp = zero_rows(used_rows + c * zr, zr)
            cp.wait() if wait else cp.start()

        for wait in (False, True):
            lax.fori_loop(0, N_EXPERTS, lambda e, _: (fill(e, wait), 0)[1], 0)
            lax.fori_loop(0, n_tail, lambda c, _: (tail(c, wait), 0)[1], 0)

    def issue(t, _):
        for k in range(2):
            dst = dst_ref[2 * t + k]
            pltpu.make_async_copy(h_ref.at[pl.ds(t, 1)], xs_ref.at[pl.ds(dst, 1)], sem).start()
        return 0

    lax.fori_loop(0, tcd, issue, 0, unroll=8)
    for k in range(2):
        pltpu.make_async_copy(h_ref, xs_ref.at[pl.ds(0, tcd)], sem).wait()


def _dispatch(dest, pad_start, counts, h2, rows, tcd, tm):
    n, d = h2.shape
    assert tm & (tm - 1) == 0 and tm >= 2 * SUBLANES
    zr = tm // 2
    return pl.pallas_call(
        functools.partial(_dispatch_kernel, tcd=tcd, tm=tm, zr=zr),
        grid=(n // tcd,),
        in_specs=[pl.BlockSpec((2 * tcd,), lambda i: (i,), memory_space=pltpu.SMEM),
                  pl.BlockSpec(memory_space=pltpu.SMEM),
                  pl.BlockSpec(memory_space=pltpu.SMEM),
                  pl.BlockSpec((tcd, d), lambda i: (i, 0))],
        out_specs=pl.BlockSpec(memory_space=pl.ANY),
        out_shape=jax.ShapeDtypeStruct((rows, d), h2.dtype),
        scratch_shapes=[pltpu.VMEM((zr, d), h2.dtype), pltpu.SemaphoreType.DMA(()),
                        pltpu.SemaphoreType.DMA(())],
        compiler_params=_cp(("arbitrary",)),
        name="dispatch",
    )(dest, pad_start, counts, h2)


def _experts_kernel(be_ref, ord_ref, nxt_ref, nu_ref, xs_ref, wg_hbm, wu_hbm, wd_hbm, ys_ref,
                    g_f, u_f, d_f, g_s, u_s, d_s, sem):
    i = pl.program_id(0)
    used = i < nu_ref[0]
    e = be_ref[i]
    fresh = (i == 0) | (e != be_ref[jnp.maximum(i - 1, 0)])
    slot = lax.rem(ord_ref[i], 2)

    def fetch(expert, s):
        return [pltpu.make_async_copy(w.at[expert], f.at[s], sem.at[s, k])
                for k, (w, f) in enumerate(((wg_hbm, g_f), (wu_hbm, u_f), (wd_hbm, d_f)))]

    @pl.when(used & (i == 0))
    def _():
        for c in fetch(e, slot):
            c.start()

    @pl.when(used & fresh)
    def _():
        for c in fetch(e, slot):
            c.wait()
        nxt = nxt_ref[i]

        @pl.when(nxt >= 0)
        def _():
            for c in fetch(nxt, 1 - slot):
                c.start()

        g_s[...] = g_f[slot].astype(BF16)
        u_s[...] = u_f[slot].astype(BF16)
        d_s[...] = d_f[slot].astype(BF16)

    @pl.when(used)
    def _():
        x = xs_ref[...].astype(BF16)
        a = _silu(_dot(x, g_s[...])) * _dot(x, u_s[...])
        ys_ref[...] = _dot(a.astype(BF16), d_s[...])

    @pl.when(jnp.logical_not(used))
    def _():
        ys_ref[...] = jnp.zeros_like(ys_ref)


def _experts(blk_exp, blk_ord, blk_nxt, n_used, xs, w_gate, w_up, w_down, tm):
    d = xs.shape[1]
    de = w_gate.shape[2]
    nb = blk_exp.shape[0]
    row = lambda i, be, od, nx, nu: (jnp.minimum(i, nu[0] - 1), 0)
    anyspec = pl.BlockSpec(memory_space=pl.ANY)
    grid_spec = pltpu.PrefetchScalarGridSpec(
        num_scalar_prefetch=4,
        grid=(nb,),
        in_specs=[pl.BlockSpec((tm, d), row), anyspec, anyspec, anyspec],
        out_specs=pl.BlockSpec((tm, d), lambda i, be, od, nx, nu: (i, 0)),
        scratch_shapes=[pltpu.VMEM((2, d, de), F32), pltpu.VMEM((2, d, de), F32),
                        pltpu.VMEM((2, de, d), F32),
                        pltpu.VMEM((d, de), BF16), pltpu.VMEM((d, de), BF16),
                        pltpu.VMEM((de, d), BF16),
                        pltpu.SemaphoreType.DMA((2, 3))],
    )
    return pl.pallas_call(
        _experts_kernel,
        grid_spec=grid_spec,
        out_shape=jax.ShapeDtypeStruct((nb * tm, d), F32),
        compiler_params=_cp(("arbitrary",)),
        name="experts",
    )(blk_exp, blk_ord, blk_nxt, n_used, xs, w_gate, w_up, w_down)


def _combine_kernel(ids_ref, idn_ref, ys_ref, x1_ref, wt_ref, mod_ref, g_ref, o_ref,
                    buf, sem, *, tc):
    i = pl.program_id(0)
    slot = lax.rem(i, 2)

    def gather(id_ref, s):
        def issue(t, _):
            for k in range(2):
                src = id_ref[2 * t + k]
                pltpu.make_async_copy(ys_ref.at[pl.ds(src, 1)], buf.at[s, k, pl.ds(t, 1)],
                                      sem.at[s]).start()
            return 0

        lax.fori_loop(0, tc, issue, 0, unroll=8)

    @pl.when(i == 0)
    def _():
        gather(ids_ref, 0)

    @pl.when(i + 1 < pl.num_programs(0))
    def _():
        gather(idn_ref, 1 - slot)

    for k in range(2):
        pltpu.make_async_copy(ys_ref.at[pl.ds(0, tc)], buf.at[slot, k], sem.at[slot]).wait()

    moe = buf[slot, 0] * wt_ref[:, 0:1] + buf[slot, 1] * wt_ref[:, 1:2]
    x2 = x1_ref[...] + mod_ref[0, 5:6, :] * moe
    o_ref[...] = x2 * lax.rsqrt(jnp.mean(x2 * x2, axis=-1, keepdims=True) + EPS) * g_ref[...]


def _combine(dest, ys, x1, wts_t, mod3, final_g, seq, tc):
    n, d = x1.shape
    last = n // tc - 1
    return pl.pallas_call(
        functools.partial(_combine_kernel, tc=tc),
        grid=(n // tc,),
        in_specs=[pl.BlockSpec((2 * tc,), lambda i: (i,), memory_space=pltpu.SMEM),
                  pl.BlockSpec((2 * tc,), lambda i: (jnp.minimum(i + 1, last),),
                               memory_space=pltpu.SMEM),
                  pl.BlockSpec(memory_space=pl.ANY),
                  pl.BlockSpec((tc, d), lambda i: (i, 0)),
                  pl.BlockSpec((tc, LANES), lambda i: (i, 0)),
                  pl.BlockSpec((1, 6, d), lambda i: ((i * tc) // seq, 0, 0)),
                  pl.BlockSpec((1, d), lambda i: (0, 0))],
        out_specs=pl.BlockSpec((tc, d), lambda i: (i, 0)),
        out_shape=jax.ShapeDtypeStruct((n, d), F32),
        scratch_shapes=[pltpu.VMEM((2, 2, tc, d), F32), pltpu.SemaphoreType.DMA((2,))],
        compiler_params=_cp(("arbitrary",)),
        name="combine",
    )(dest, dest, ys, x1, wts_t, mod3, final_g.reshape(1, d))


def _token_mixer(x2, mod3, norm1_g, w_in, ret_gn_g, cmp_pos_k, cmp_w1_k, cmp_w2_k,
                 cmp_pos_v, cmp_w1_v, cmp_w2_v, bsz, seq):
    n, d = x2.shape
    w_t = w_in.T
    w_gate_t = jnp.pad(w_t[PROJ_MAIN:], ((0, LANES - N_GATE_COLS), (0, 0))).astype(BF16)
    proj, gate_logits = _inproj(x2, mod3, norm1_g, w_t, w_gate_t, seq)

    o_ret = _retention(proj, ret_gn_g, bsz, seq)

    g_ = NSA_KV_GROUPS
    crow = seq // CMP_STRIDE

    pe_flat = lambda pe: pe.reshape(1, CMP_BLOCK * NSA_D)
    kc, vc = _compress(proj, pe_flat(cmp_pos_k), pe_flat(cmp_pos_v),
                       cmp_w1_k, cmp_w1_v, cmp_w2_k, cmp_w2_v, bsz, seq)

    o_nsa = _nsa(proj, kc, vc, gate_logits, bsz, seq)
    return o_ret, o_nsa


def _moe(h2, lt, x1, mod3, final_g, b_grp, b_exp, w_gate, w_up, w_down, seq, tm):
    n, d = h2.shape
    bias_col = jnp.zeros((LANES,), F32).at[:N_GROUPS].set(b_grp).at[N_GROUPS:N_GROUPS + N_EXPERTS].set(b_exp)
    bias_col = jnp.broadcast_to(bias_col[:, None], (LANES, LANES))
    ids, wts, cnt = _route(lt, bias_col)

    counts = cnt[:, 0]
    padded = (counts + tm - 1) // tm * tm
    pad_end = jnp.cumsum(padded)
    pad_start = (pad_end - padded).astype(I32)
    nb = (2 * n) // tm + N_EXPERTS
    n_used = (pad_end[-1] // tm).astype(I32).reshape(1)
    blk_start = jnp.arange(nb, dtype=I32) * tm
    blk_exp = jnp.minimum(jnp.sum((pad_end[None, :] <= blk_start[:, None]).astype(I32), axis=1),
                          N_EXPERTS - 1).astype(I32)
    last_exp = blk_exp[jnp.maximum(n_used[0] - 1, 0)]
    blk_exp = jnp.where(jnp.arange(nb) < n_used[0], blk_exp, last_exp)
    eid = jnp.arange(N_EXPERTS, dtype=I32)
    has = counts > 0
    exp_ord = jnp.sum((has[None, :] & (eid[None, :] < eid[:, None])).astype(I32), axis=1)
    exp_nxt = jnp.min(jnp.where(has[None, :] & (eid[None, :] > eid[:, None]), eid[None, :], N_EXPERTS), axis=1)
    exp_nxt = jnp.where(exp_nxt < N_EXPERTS, exp_nxt, -1).astype(I32)
    blk_ord = exp_ord[blk_exp]
    blk_nxt = exp_nxt[blk_exp]

    dest = _dest(ids, pad_start)
    xs = _dispatch(dest, pad_start, counts.astype(I32), h2, nb * tm, min(512, n), tm)
    ys = _experts(blk_exp, blk_ord, blk_nxt, n_used, xs, w_gate, w_up, w_down, tm)
    wts_t = jnp.pad(wts[:2].T, ((0, 0), (0, LANES - 2)))
    return _combine(dest, ys, x1, wts_t, mod3, final_g, seq, min(256, n))


def kernel(x, c, w_ada, b_ada, norm1_g, norm2_g, final_g, w_in, ret_gn_g, cmp_pos_k, cmp_w1_k,
           cmp_w2_k, cmp_pos_v, cmp_w1_v, cmp_w2_v, w_out, w_grp, b_grp, w_exp, b_exp, w_gate,
           w_up, w_down):
    bsz, seq, d = x.shape
    assert w_ada.shape[0] == 1, "single-layer block"
    n = bsz * seq
    x2 = x.reshape(n, d)
    mod3 = _ada(c, w_ada[0], b_ada[0]).reshape(bsz, 6, d)

    o_ret, o_nsa = _token_mixer(x2, mod3, norm1_g[0], w_in[0], ret_gn_g[0], cmp_pos_k[0],
                                cmp_w1_k[0], cmp_w2_k[0], cmp_pos_v[0], cmp_w1_v[0], cmp_w2_v[0],
                                bsz, seq)

    w_route = jnp.concatenate([w_grp[0], w_exp[0]], axis=1)
    w_route = jnp.pad(w_route, ((0, 0), (0, LANES - w_route.shape[1]))).T.astype(BF16)
    x1, h2, lt = _outproj(o_ret, o_nsa, x2, mod3, norm2_g[0], w_out[0].astype(BF16), w_route, seq)

    out = _moe(h2, lt, x1, mod3, final_g, b_grp[0], b_exp[0], w_gate[0], w_up[0], w_down[0],
               seq, 256)
    return out.reshape(bsz, seq, d)
```

```python
import functools
import math

import numpy as np
import jax
import jax.numpy as jnp
from jax import lax
from jax.experimental import pallas as pl
from jax.experimental.pallas import tpu as pltpu

F32 = jnp.float32
BF16 = jnp.bfloat16
I32 = jnp.int32

RET_HEADS = 4
RET_DK = 256
RET_DV = 256
RET_CHUNK = 128
NSA_HEADS = 8
NSA_KV_GROUPS = 2
NSA_HPG = NSA_HEADS // NSA_KV_GROUPS
NSA_D = 128
CMP_BLOCK = 32
CMP_STRIDE = 16
CMP_HIDDEN = 256
SEL_BLOCK = 64
SEL_COUNT = 16
WIN = 512
N_GROUPS = 8
EXP_PER_GROUP = 8
N_EXPERTS = N_GROUPS * EXP_PER_GROUP
D_EXPERT = 512
ROPE_BASE = 10000.0
EPS = 1e-6
NEG = -1e30
FORCE_BONUS = 1e4

LANES = 128
SUBLANES = 8
VMEM_LIMIT = 52 * 1024 * 1024

_C_RQ, _C_RK, _C_RV, _C_RG = 0, 8, 16, 24
_C_NQ = 32
_C_KC, _C_VC, _C_KS, _C_VS, _C_KW, _C_VW = 40, 42, 44, 46, 48, 50
_C_GATE = 52
PROJ_MAIN = _C_GATE * LANES
N_GATE_COLS = NSA_HEADS * 3


def _cp(sem, vmem=VMEM_LIMIT):
    return pltpu.CompilerParams(dimension_semantics=sem, vmem_limit_bytes=vmem)


def _silu(v):
    return v * jax.nn.sigmoid(v)


def _dot(a, b):
    return jnp.dot(a, b, preferred_element_type=F32)


def _dot_nt(a, b):
    return lax.dot_general(a, b, (((1,), (1,)), ((), ())), preferred_element_type=F32)


def _dot_tn(a, b):
    return lax.dot_general(a, b, (((0,), (0,)), ((), ())), preferred_element_type=F32)


def _ada_kernel(c_ref, w_ref, b_ref, o_ref):
    ca = _silu(c_ref[...]).astype(BF16)
    o_ref[...] = _dot(ca, w_ref[...].astype(BF16)) + b_ref[...]


def _ada(c, w, b):
    bsz, d = c.shape
    n = w.shape[1]
    tn = min(1024, n)
    return pl.pallas_call(
        _ada_kernel,
        grid=(n // tn,),
        in_specs=[pl.BlockSpec((bsz, d), lambda j: (0, 0)),
                  pl.BlockSpec((d, tn), lambda j: (0, j)),
                  pl.BlockSpec((1, tn), lambda j: (0, j))],
        out_specs=pl.BlockSpec((bsz, tn), lambda j: (0, j)),
        out_shape=jax.ShapeDtypeStruct((bsz, n), F32),
        compiler_params=_cp(("arbitrary",)),
        name="ada",
    )(c, w, b.reshape(1, n))


def _inproj_kernel(x_ref, mod_ref, g_ref, wt_ref, wgt_ref, proj_ref, gate_ref, h_ref):
    @pl.when(pl.program_id(1) == 0)
    def _():
        x = x_ref[...]
        y = x * lax.rsqrt(jnp.mean(x * x, axis=-1, keepdims=True) + EPS) * g_ref[...]
        h = (y * (1.0 + mod_ref[0, 1:2, :]) + mod_ref[0, 0:1, :]).astype(BF16)
        h_ref[...] = h
        gate_ref[...] = _dot_nt(h, wgt_ref[...])

    proj_ref[...] = _dot_nt(h_ref[...], wt_ref[...].astype(BF16)).astype(BF16)


def _inproj(x2, mod3, g, w_t, wg_t, seq):
    n, d = x2.shape
    tm = min(1024, seq)
    tn = 512
    nj = PROJ_MAIN // tn
    return pl.pallas_call(
        _inproj_kernel,
        grid=(n // tm, nj),
        in_specs=[pl.BlockSpec((tm, d), lambda i, j: (i, 0)),
                  pl.BlockSpec((1, 6, d), lambda i, j: ((i * tm) // seq, 0, 0)),
                  pl.BlockSpec((1, d), lambda i, j: (0, 0)),
                  pl.BlockSpec((tn, d), lambda i, j: (j, 0)),
                  pl.BlockSpec((LANES, d), lambda i, j: (0, 0))],
        out_specs=[pl.BlockSpec((tm, tn), lambda i, j: (i, j)),
                   pl.BlockSpec((tm, LANES), lambda i, j: (i, 0))],
        out_shape=[jax.ShapeDtypeStruct((n, PROJ_MAIN), BF16),
                   jax.ShapeDtypeStruct((n, LANES), F32)],
        scratch_shapes=[pltpu.VMEM((tm, d), BF16)],
        compiler_params=_cp(("arbitrary", "arbitrary")),
        name="inproj",
    )(x2, mod3, g.reshape(1, d), w_t, wg_t)


def _retention_kernel(q_ref, k_ref, v_ref, g_ref, cos_ref, sin_ref, din_ref, zeta_ref,
                      qdec_ref, cdec_ref, gn_ref, o_ref, s_ref):
    @pl.when(pl.program_id(1) == 0)
    def _():
        s_ref[...] = jnp.zeros_like(s_ref)

    half = RET_DK // 2
    c = RET_CHUNK

    for sub in range(q_ref.shape[0] // c):
        rows = slice(sub * c, (sub + 1) * c)
        cos = cos_ref[rows, :]
        sin = sin_ref[rows, :]

        def rot(a):
            a1, a2 = a[:, :half], a[:, half:]
            return jnp.concatenate([a1 * cos - a2 * sin, a1 * sin + a2 * cos], axis=1)

        for h in range(RET_HEADS):
            qs = slice(h * RET_DK, (h + 1) * RET_DK)
            vs = slice(h * RET_DV, (h + 1) * RET_DV)
            q = rot(q_ref[rows, qs].astype(F32))
            k = rot(k_ref[rows, qs].astype(F32)) * (RET_DK ** -0.5)
            v = v_ref[rows, vs]
            qb = q.astype(BF16)
            kb = k.astype(BF16)
            s = _dot_nt(qb, kb) * din_ref[h]
            inner = _dot(s.astype(BF16), v)
            s_prev = s_ref[h]
            cross = _dot(qb, s_prev.astype(BF16)) * qdec_ref[h]
            kv = _dot_tn((k * zeta_ref[h]).astype(BF16), v)
            s_ref[h] = cdec_ref[h] * s_prev + kv
            o = inner + cross
            mu = jnp.mean(o, axis=-1, keepdims=True)
            oc = o - mu
            var = jnp.mean(oc * oc, axis=-1, keepdims=True)
            o = oc * lax.rsqrt(var + EPS) * gn_ref[:, vs]
            o_ref[rows, vs] = (o * _silu(g_ref[rows, vs].astype(F32))).astype(BF16)


def _retention(proj, gn_g, bsz, seq):
    n = proj.shape[0]
    c = RET_CHUNK
    nc = seq // c
    hw = RET_HEADS * RET_DK
    half = RET_DK // 2
    pos = jnp.arange(seq, dtype=F32)
    inv = ROPE_BASE ** (-jnp.arange(half, dtype=F32) / half)
    ang = pos[:, None] * inv[None, :]
    cos, sin = jnp.cos(ang), jnp.sin(ang)
    log_gamma = jnp.log1p(-jnp.exp2(-5.0 - jnp.arange(RET_HEADS, dtype=F32)))
    idx = jnp.arange(c, dtype=F32)
    rel = idx[:, None] - idx[None, :]
    decay_in = jnp.where(rel >= 0, jnp.exp(log_gamma[:, None, None] * jnp.maximum(rel, 0.0)), 0.0)
    zeta = jnp.exp(log_gamma[:, None] * (c - 1 - idx)[None, :])[:, :, None]
    q_decay = jnp.exp(log_gamma[:, None] * (idx + 1)[None, :])[:, :, None]
    chunk_decay = jnp.exp(log_gamma * c)[:, None, None]
    rb = min(4 * c, seq)
    ns = seq // rb
    row = lambda b, t: (b * ns + t)
    return pl.pallas_call(
        _retention_kernel,
        grid=(bsz, ns),
        in_specs=[pl.BlockSpec((rb, hw), lambda b, t: (row(b, t), 0)),
                  pl.BlockSpec((rb, hw), lambda b, t: (row(b, t), 1)),
                  pl.BlockSpec((rb, hw), lambda b, t: (row(b, t), 2)),
                  pl.BlockSpec((rb, hw), lambda b, t: (row(b, t), 3)),
                  pl.BlockSpec((rb, half), lambda b, t: (t, 0)),
                  pl.BlockSpec((rb, half), lambda b, t: (t, 0)),
                  pl.BlockSpec((RET_HEADS, c, c), lambda b, t: (0, 0, 0)),
                  pl.BlockSpec((RET_HEADS, c, 1), lambda b, t: (0, 0, 0)),
                  pl.BlockSpec((RET_HEADS, c, 1), lambda b, t: (0, 0, 0)),
                  pl.BlockSpec((RET_HEADS, 1, 1), lambda b, t: (0, 0, 0)),
                  pl.BlockSpec((1, hw), lambda b, t: (0, 0))],
        out_specs=pl.BlockSpec((rb, hw), lambda b, t: (row(b, t), 0)),
        out_shape=jax.ShapeDtypeStruct((n, hw), BF16),
        scratch_shapes=[pltpu.VMEM((RET_HEADS, RET_DK, RET_DV), F32)],
        compiler_params=_cp(("arbitrary", "arbitrary")),
        name="retention",
    )(proj, proj, proj, proj, cos, sin, decay_in, zeta, q_decay, chunk_decay,
      gn_g.reshape(1, hw))


def _compress_kernel(ak_ref, av_ref, pek_ref, pev_ref, w1k_ref, w1v_ref, w2k_ref, w2v_ref,
                     kc_ref, vc_ref, a_scr):
    seq = ak_ref.shape[0]
    nblk = seq // CMP_STRIDE
    a_scr[seq:, :] = jnp.zeros((a_scr.shape[0] - seq, NSA_D), F32)

    def one(a_ref, pe_ref, w1_ref, w2_ref, o_ref):
        a_scr[:seq, :] = a_ref[...].astype(F32)
        flat = jnp.concatenate([a_scr[pl.ds(l, nblk, stride=CMP_STRIDE), :] for l in range(CMP_BLOCK)],
                               axis=1)
        pre = _dot((flat + pe_ref[...]).astype(BF16), w1_ref[...].astype(BF16))
        o_ref[...] = _dot(_silu(pre).astype(BF16), w2_ref[...].astype(BF16)).astype(BF16)

    one(ak_ref, pek_ref, w1k_ref, w2k_ref, kc_ref)
    one(av_ref, pev_ref, w1v_ref, w2v_ref, vc_ref)


def _compress(proj, pek, pev, w1k, w1v, w2k, w2v, bsz, seq):
    g_ = NSA_KV_GROUPS
    nblk = seq // CMP_STRIDE
    full = lambda a: pl.BlockSpec(a.shape, lambda i: (0,) * a.ndim)
    kblk = pl.BlockSpec((seq, NSA_D), lambda i: (i // g_, _C_KC + i % g_))
    vblk = pl.BlockSpec((seq, NSA_D), lambda i: (i // g_, _C_VC + i % g_))
    oblk = pl.BlockSpec((nblk, NSA_D), lambda i: (i, 0))
    return pl.pallas_call(
        _compress_kernel,
        grid=(bsz * g_,),
        in_specs=[kblk, vblk, full(pek), full(pev), full(w1k), full(w1v), full(w2k), full(w2v)],
        out_specs=[oblk, oblk],
        out_shape=[jax.ShapeDtypeStruct((bsz * g_ * nblk, NSA_D), BF16)] * 2,
        scratch_shapes=[pltpu.VMEM((seq + CMP_BLOCK, NSA_D), F32)],
        compiler_params=_cp(("arbitrary",)),
        name="compress",
    )(proj, proj, pek, pev, w1k, w1v, w2k, w2v)


def _nsa_kernel(q_ref, kc_ref, vc_ref, ks_ref, vs_ref, kw_ref, vw_ref, gate_ref, ovt_ref,
                o_ref, kaug_ref, vst_ref, vwt_ref, *, tq, tk, vt, seq, ncmp, wlen):
    crow = kc_ref.shape[0]
    i = pl.program_id(2)
    hg = NSA_HPG
    d = NSA_D
    r = hg * tq

    @pl.when(i == 0)
    def _():
        kaug_ref[:, :d] = ks_ref[...]
        blk = lax.broadcasted_iota(I32, (seq, LANES), 0) // SEL_BLOCK
        lane = lax.broadcasted_iota(I32, (seq, LANES), 1)
        kaug_ref[:, d:] = (blk == lane).astype(BF16)
        for c in range(seq // vt):
            vst_ref[c] = vs_ref[c * vt:(c + 1) * vt, :].astype(F32).T.astype(BF16)
            vwt_ref[c] = vw_ref[c * vt:(c + 1) * vt, :].astype(F32).T.astype(BF16)

    def pv_t(vt_ref, first_tile, p):
        out = None
        for c in range(p.shape[0] // vt):
            part = _dot(vt_ref[first_tile + c], p[c * vt:(c + 1) * vt, :])
            out = part if out is None else out + part
        return out

    q = q_ref[...]
    qh = [(q[:, h * d:(h + 1) * d].astype(F32) * (d ** -0.5)).astype(BF16) for h in range(hg)]
    qa = jnp.concatenate(qh, axis=0)
    t1 = i * tq + lax.broadcasted_iota(I32, (tq, 1), 0)
    tpos = jnp.concatenate([t1] * hg, axis=0)
    t_row = i * tq + lax.broadcasted_iota(I32, (1, tq), 1)
    tpos_row = jnp.concatenate([t_row] * hg, axis=1)

    w0 = pl.multiple_of(jnp.maximum(i * tq + tq - wlen, 0), vt)
    sw = _dot_nt(kw_ref[pl.ds(w0, wlen), :], qa)
    delta = tpos_row - (w0 + lax.broadcasted_iota(I32, (wlen, 1), 0))
    sw = jnp.where((delta >= 0) & (delta < WIN), sw, NEG)
    e_w = jnp.exp(sw - jnp.max(sw, axis=0, keepdims=True))
    o_win = (pv_t(vwt_ref, w0 // vt, e_w.astype(BF16)) / jnp.sum(e_w, axis=0, keepdims=True)).T

    sct = _dot_nt(kc_ref[...], qa)
    cidx = lax.broadcasted_iota(I32, (crow, 1), 0)
    cmask = (cidx * CMP_STRIDE + (CMP_BLOCK - 1) <= tpos_row) & (cidx < ncmp)
    sct = jnp.where(cmask, sct, NEG)
    e_c = jnp.where(cmask, jnp.exp(sct - jnp.max(sct, axis=0, keepdims=True)), 0.0)
    den_c = jnp.sum(e_c, axis=0, keepdims=True)
    p_t = jnp.where(den_c > 0.0, e_c / jnp.where(den_c > 0.0, den_c, 1.0), 0.0)
    o_cmp = _dot_tn(p_t.astype(BF16), vc_ref[...])

    psum_t = p_t[:, 0:tq]
    for h in range(1, hg):
        psum_t = psum_t + p_t[:, h * tq:(h + 1) * tq]
    nsel_blocks = seq // SEL_BLOCK
    nsb = ovt_ref.shape[0]
    imp_t = jnp.dot(ovt_ref[...], psum_t, preferred_element_type=F32,
                    precision=lax.Precision.HIGHEST)
    sidx = lax.broadcasted_iota(I32, (nsb, 1), 0)
    cur = t_row // SEL_BLOCK
    valid = sidx <= cur
    forced = (sidx == 0) | (sidx == cur) | (sidx == cur - 1)
    score = jnp.where(valid, imp_t + jnp.where(forced, FORCE_BONUS, 0.0), -1.0)
    rank = jnp.zeros((nsb, tq), F32)
    for s2 in range(nsel_blocks):
        row = score[s2:s2 + 1, :]
        beats = (row > score) | ((row == score) & (sidx > s2))
        rank = rank + beats.astype(F32)
    sel = valid & (rank < float(min(SEL_COUNT, nsel_blocks)))
    selb_t = jnp.where(sel, 0.0, NEG)
    selb_t = jnp.concatenate([selb_t, jnp.zeros((LANES - nsb, tq), F32)], axis=0)
    selb = selb_t.T.astype(BF16)
    q_aug = jnp.concatenate([jnp.concatenate([qh[h], selb], axis=1) for h in range(hg)], axis=0)

    n_full = (i * tq + 1) // tk
    kl = lax.broadcasted_iota(I32, (tk, 1), 0)

    def tile(j, carry, masked):
        m, l, acc = carry
        k0 = pl.multiple_of(j * tk, tk)
        s = _dot_nt(kaug_ref[pl.ds(k0, tk), :], q_aug)
        if masked:
            s = jnp.where(k0 + kl <= tpos_row, s, NEG)
        m_new = jnp.maximum(m, jnp.max(s, axis=0, keepdims=True))
        alpha = jnp.exp(m - m_new)
        p = jnp.exp(s - m_new)
        l = alpha * l + jnp.sum(p, axis=0, keepdims=True)
        acc = alpha * acc + pv_t(vst_ref, j * (tk // vt), p.astype(BF16))
        return m_new, l, acc

    m0 = jnp.full((1, r), NEG, F32)
    l0 = jnp.zeros((1, r), F32)
    a0 = jnp.zeros((d, r), F32)
    carry = lax.fori_loop(0, n_full, functools.partial(tile, masked=False), (m0, l0, a0))
    _, l_s, acc_s = tile(n_full, carry, True)
    o_slc = (acc_s / l_s).T

    gt = jax.nn.sigmoid(gate_ref[...])
    grp = pl.program_id(1)
    for g in range(1, NSA_KV_GROUPS):
        gt = jnp.where(grp == g, pltpu.roll(gt, LANES - g * 3 * hg, 1), gt)
    for h in range(hg):
        rows = slice(h * tq, (h + 1) * tq)
        o = (gt[:, 3 * h:3 * h + 1] * o_cmp[rows] + gt[:, 3 * h + 1:3 * h + 2] * o_slc[rows]
             + gt[:, 3 * h + 2:3 * h + 3] * o_win[rows])
        o_ref[:, h * d:(h + 1) * d] = o.astype(BF16)


def _nsa(proj, kc, vc, gates_g, bsz, seq):
    n = proj.shape[0]
    g_ = NSA_KV_GROUPS
    tq = 256
    tk = min(512, seq)
    nq = seq // tq
    ncmp = (seq - CMP_BLOCK) // CMP_STRIDE + 1
    wlen = min(WIN + tq, seq)
    nsel = seq // SEL_BLOCK
    crow = seq // CMP_STRIDE
    assert nsel <= LANES and crow <= LANES and tq <= tk and tk % tq == 0
    nsb = -(-nsel // 8) * 8
    ss = np.arange(nsb)[:, None] * SEL_BLOCK
    cs = np.arange(crow)[None, :] * CMP_STRIDE
    ov = ((cs < ss + SEL_BLOCK) & (cs + CMP_BLOCK > ss)
          & (np.arange(crow)[None, :] < ncmp) & (np.arange(nsb)[:, None] < nsel))
    ov = jnp.asarray(ov.astype(np.float32))
    kvspec = lambda c0: pl.BlockSpec((seq, NSA_D), lambda b, g, i: (b, c0 + g))
    vt = 256
    assert tk % vt == 0 and wlen % vt == 0 and tq % vt == 0 and seq % vt == 0
    kern = functools.partial(_nsa_kernel, tq=tq, tk=tk, vt=vt, seq=seq, ncmp=ncmp, wlen=wlen)
    return pl.pallas_call(
        kern,
        grid=(bsz, g_, nq),
        in_specs=[pl.BlockSpec((tq, NSA_HPG * NSA_D), lambda b, g, i: (b * nq + i, _C_NQ // NSA_HPG + g)),
                  pl.BlockSpec((crow, NSA_D), lambda b, g, i: (b * g_ + g, 0)),
                  pl.BlockSpec((crow, NSA_D), lambda b, g, i: (b * g_ + g, 0)),
                  kvspec(_C_KS), kvspec(_C_VS), kvspec(_C_KW), kvspec(_C_VW),
                  pl.BlockSpec((tq, LANES), lambda b, g, i: (b * nq + i, 0)),
                  pl.BlockSpec((nsb, crow), lambda b, g, i: (0, 0))],
        out_specs=pl.BlockSpec((tq, NSA_HPG * NSA_D), lambda b, g, i: (b * nq + i, g)),
        out_shape=jax.ShapeDtypeStruct((n, NSA_HEADS * NSA_D), BF16),
        scratch_shapes=[pltpu.VMEM((seq, 2 * NSA_D), BF16),
                        pltpu.VMEM((seq // vt, NSA_D, vt), BF16),
                        pltpu.VMEM((seq // vt, NSA_D, vt), BF16)],
        compiler_params=_cp(("arbitrary", "arbitrary", "arbitrary")),
        name="nsa",
    )(proj, kc, vc, proj, proj, proj, proj, gates_g, ov)


def _outproj_kernel(oret_ref, onsa_ref, x_ref, mod_ref, g_ref, w_ref, wr_ref,
                    x1_ref, h2_ref, lt_ref):
    hw = oret_ref.shape[1]
    mix = _dot(oret_ref[...], w_ref[:hw, :]) + _dot(onsa_ref[...], w_ref[hw:, :])
    x1 = x_ref[...] + mod_ref[0, 2:3, :] * mix
    x1_ref[...] = x1
    y = x1 * lax.rsqrt(jnp.mean(x1 * x1, axis=-1, keepdims=True) + EPS) * g_ref[...]
    h2 = y * (1.0 + mod_ref[0, 4:5, :]) + mod_ref[0, 3:4, :]
    h2_ref[...] = h2
    lt_ref[...] = _dot_nt(wr_ref[...], h2.astype(BF16))


def _outproj(o_ret, o_nsa, x2, mod3, g, w_bf, wr_bf, seq):
    n, d = x2.shape
    tm = min(512, seq)
    hw = o_ret.shape[1]
    return pl.pallas_call(
        _outproj_kernel,
        grid=(n // tm,),
        in_specs=[pl.BlockSpec((tm, hw), lambda i: (i, 0)),
                  pl.BlockSpec((tm, o_nsa.shape[1]), lambda i: (i, 0)),
                  pl.BlockSpec((tm, d), lambda i: (i, 0)),
                  pl.BlockSpec((1, 6, d), lambda i: ((i * tm) // seq, 0, 0)),
                  pl.BlockSpec((1, d), lambda i: (0, 0)),
                  pl.BlockSpec(w_bf.shape, lambda i: (0, 0)),
                  pl.BlockSpec(wr_bf.shape, lambda i: (0, 0))],
        out_specs=[pl.BlockSpec((tm, d), lambda i: (i, 0)),
                   pl.BlockSpec((tm, d), lambda i: (i, 0)),
                   pl.BlockSpec((LANES, tm), lambda i: (0, i))],
        out_shape=[jax.ShapeDtypeStruct((n, d), F32),
                   jax.ShapeDtypeStruct((n, d), F32),
                   jax.ShapeDtypeStruct((LANES, n), F32)],
        compiler_params=_cp(("arbitrary",)),
        name="outproj",
    )(o_ret, o_nsa, x2, mod3, g.reshape(1, d), w_bf, wr_bf)


def _route_kernel(lt_ref, b_ref, tri_ref, ids_ref, wts_ref, cnt_ref, carry_ref, *, sub):
    @pl.when(pl.program_id(0) == 0)
    def _():
        carry_ref[...] = jnp.zeros_like(carry_ref)

    ng, ne = N_GROUPS, EXP_PER_GROUP
    l = lt_ref[...] + b_ref[:, 0:1]
    tc = l.shape[1]
    ridx = lax.broadcasted_iota(I32, (ng, tc), 0).astype(F32)

    def softmax0(v):
        e = jnp.exp(v - jnp.max(v, axis=0, keepdims=True))
        return e / jnp.sum(e, axis=0, keepdims=True)

    def top1(p):
        top = jnp.max(p, axis=0, keepdims=True)
        idx = jnp.min(jnp.where(p == top, ridx, float(ng)), axis=0, keepdims=True)
        return top, idx

    pg_top, grp = top1(softmax0(l[0:ng]))
    leg = jnp.zeros((ne, tc), F32)
    for g in range(ng):
        leg = jnp.where(grp == float(g), l[ng + g * ne:ng + (g + 1) * ne], leg)
    pe = softmax0(leg)
    p1, i1 = top1(pe)
    p2, i2 = top1(jnp.where(ridx == i1, -1.0, pe))
    den = p1 + p2
    w1 = pg_top * p1 / den
    w2 = pg_top * p2 / den
    e1 = grp * float(ne) + i1
    e2 = grp * float(ne) + i2

    eio = lax.broadcasted_iota(I32, (N_EXPERTS, sub), 0).astype(F32)
    r1 = []
    r2 = []
    carry = carry_ref[:, 0:1]
    for c in range(tc // sub):
        cs = slice(c * sub, (c + 1) * sub)
        oh1 = (eio == e1[:, cs]).astype(F32)
        oh2 = (eio == e2[:, cs]).astype(F32)
        oh = oh1 + oh2
        before = carry + _dot(oh.astype(BF16), tri_ref[...])
        r1.append(jnp.sum(oh1 * before, axis=0, keepdims=True))
        r2.append(jnp.sum(oh2 * before, axis=0, keepdims=True))
        carry = carry + jnp.sum(oh, axis=1, keepdims=True)
    carry_ref[...] = jnp.broadcast_to(carry, carry_ref.shape)
    cnt_ref[...] = jnp.broadcast_to(carry, cnt_ref.shape).astype(I32)
    r1 = jnp.concatenate(r1, axis=1)
    r2 = jnp.concatenate(r2, axis=1)
    zf = jnp.zeros((4, tc), F32)
    ids_ref[...] = jnp.concatenate([e1, e2, r1, r2, zf], axis=0).astype(I32)
    wts_ref[...] = jnp.concatenate([w1, w2, jnp.zeros((6, tc), F32)], axis=0)


def _route(lt, bias_col):
    n = lt.shape[1]
    tc = min(2048, n)
    sub = min(512, tc)
    tri = jnp.asarray(np.triu(np.ones((sub, sub), np.float32), 1), BF16)
    return pl.pallas_call(
        functools.partial(_route_kernel, sub=sub),
        grid=(n // tc,),
        in_specs=[pl.BlockSpec((LANES, tc), lambda i: (0, i)),
                  pl.BlockSpec((LANES, LANES), lambda i: (0, 0)),
                  pl.BlockSpec((sub, sub), lambda i: (0, 0))],
        out_specs=[pl.BlockSpec((8, tc), lambda i: (0, i)),
                   pl.BlockSpec((8, tc), lambda i: (0, i)),
                   pl.BlockSpec((N_EXPERTS, LANES), lambda i: (0, 0))],
        out_shape=[jax.ShapeDtypeStruct((8, n), I32),
                   jax.ShapeDtypeStruct((8, n), F32),
                   jax.ShapeDtypeStruct((N_EXPERTS, LANES), I32)],
        scratch_shapes=[pltpu.VMEM((N_EXPERTS, LANES), F32)],
        compiler_params=_cp(("arbitrary",)),
        name="route",
    )(lt, bias_col, tri)


def _dest_kernel(ids_ref, ps_ref, o_ref):
    ids = ids_ref[...].astype(F32)
    tc = ids.shape[1]
    eio = lax.broadcasted_iota(I32, (N_EXPERTS, tc), 0).astype(F32)
    ps = ps_ref[:, 0:1]
    rows = [jnp.sum(jnp.where(eio == ids[k:k + 1], ps, 0.0), axis=0, keepdims=True) + ids[2 + k:3 + k]
            for k in range(2)]
    o_ref[...] = jnp.concatenate(rows + [jnp.zeros((6, tc), F32)], axis=0).astype(I32)


def _dest(ids, pad_start):
    n = ids.shape[1]
    tc = min(2048, n)
    ps = jnp.broadcast_to(pad_start.astype(F32)[:, None], (N_EXPERTS, LANES))
    dest = pl.pallas_call(
        _dest_kernel,
        grid=(n // tc,),
        in_specs=[pl.BlockSpec((8, tc), lambda i: (0, i)),
                  pl.BlockSpec((N_EXPERTS, LANES), lambda i: (0, 0))],
        out_specs=pl.BlockSpec((8, tc), lambda i: (0, i)),
        out_shape=jax.ShapeDtypeStruct((8, n), I32),
        compiler_params=_cp(("arbitrary",)),
        name="dest",
    )(ids, ps)
    return dest[:2].T.reshape(2 * n)


def _dispatch_kernel(dst_ref, ps_ref, cnt_ref, h_ref, xs_ref, zbuf, sem, zsem, *, tcd, tm, zr):
    @pl.when(pl.program_id(0) == 0)
    def _():
        zbuf[...] = jnp.zeros_like(zbuf)
        sizes = [zr >> b for b in range(zr.bit_length()) if (zr >> b) >= SUBLANES]

        def zero_rows(row, size):
            return pltpu.make_async_copy(zbuf.at[pl.ds(0, size)],
                                         xs_ref.at[pl.ds(pl.multiple_of(row, SUBLANES), size)], zsem)

        def zero_row(row):
            return pltpu.make_async_copy(zbuf.at[pl.ds(0, 1)], xs_ref.at[pl.ds(row, 1)], zsem)

        def fill(e, wait):
            cnt = cnt_ref[e]
            cnt8 = lax.div(cnt + (SUBLANES - 1), SUBLANES) * SUBLANES
            gap = lax.rem(tm - lax.rem(cnt8, tm), tm)
            base = ps_ref[e]

            def single(c, _):
                cp = zero_row(base + cnt + c)
                cp.wait() if wait else cp.start()
                return 0

            lax.fori_loop(0, cnt8 - cnt, single, 0)
            row = base + cnt8
            for size in sizes:
                has = lax.rem(lax.div(gap, size), 2) == 1

                @pl.when(has)
                def _():
                    cp = zero_rows(row, size)
                    cp.wait() if wait else cp.start()

                row = row + jnp.where(has, size, 0)

        last = N_EXPERTS - 1
        used_rows = ps_ref[last] + lax.div(cnt_ref[last] + (tm - 1), tm) * tm
        n_tail = lax.div(xs_ref.shape[0] - used_rows, zr)

        def tail(c, wait):
            cp = zero_rows(used_rows + c * zr, zr)
            cp.wait() if wait else cp.start()

        for wait in (False, True):
            lax.fori_loop(0, N_EXPERTS, lambda e, _: (fill(e, wait), 0)[1], 0)
            lax.fori_loop(0, n_tail, lambda c, _: (tail(c, wait), 0)[1], 0)

    def issue(t, _):
        for k in range(2):
            dst = dst_ref[2 * t + k]
            pltpu.make_async_copy(h_ref.at[pl.ds(t, 1)], xs_ref.at[pl.ds(dst, 1)], sem).start()
        return 0

    lax.fori_loop(0, tcd, issue, 0, unroll=8)
    for k in range(2):
        pltpu.make_async_copy(h_ref, xs_ref.at[pl.ds(0, tcd)], sem).wait()


def _dispatch(dest, pad_start, counts, h2, rows, tcd, tm):
    n, d = h2.shape
    assert tm & (tm - 1) == 0 and tm >= 2 * SUBLANES
    zr = tm // 2
    return pl.pallas_call(
        functools.partial(_dispatch_kernel, tcd=tcd, tm=tm, zr=zr),
        grid=(n // tcd,),
        in_specs=[pl.BlockSpec((2 * tcd,), lambda i: (i,), memory_space=pltpu.SMEM),
                  pl.BlockSpec(memory_space=pltpu.SMEM),
                  pl.BlockSpec(memory_space=pltpu.SMEM),
                  pl.BlockSpec((tcd, d), lambda i: (i, 0))],
        out_specs=pl.BlockSpec(memory_space=pl.ANY),
        out_shape=jax.ShapeDtypeStruct((rows, d), h2.dtype),
        scratch_shapes=[pltpu.VMEM((zr, d), h2.dtype), pltpu.SemaphoreType.DMA(()),
                        pltpu.SemaphoreType.DMA(())],
        compiler_params=_cp(("arbitrary",)),
        name="dispatch",
    )(dest, pad_start, counts, h2)


def _rowtok_kernel(dst_ref, o_ref, *, chunk):
    step = pl.program_id(0)

    @pl.when(step == 0)
    def _():
        def clear(row, _):
            o_ref[row] = 0
            return 0

        lax.fori_loop(0, o_ref.shape[0], clear, 0, unroll=8)

    base = step * chunk

    def place(a, _):
        o_ref[dst_ref[a]] = lax.shift_right_logical(base + a, 1)
        return 0

    lax.fori_loop(0, chunk, place, 0, unroll=8)


def _rowtok(dest, rows):
    n2 = dest.shape[0]
    chunk = min(4096, n2)
    return pl.pallas_call(
        functools.partial(_rowtok_kernel, chunk=chunk),
        grid=(n2 // chunk,),
        in_specs=[pl.BlockSpec((chunk,), lambda i: (i,), memory_space=pltpu.SMEM)],
        out_specs=pl.BlockSpec(memory_space=pltpu.SMEM),
        out_shape=jax.ShapeDtypeStruct((rows,), I32),
        compiler_params=_cp(("arbitrary",)),
        name="rowtok",
    )(dest)


def _experts_kernel(be_ref, ord_ref, nxt_ref, nu_ref, tok_ref, tokn_ref, h_hbm, wg_hbm, wu_hbm, wd_hbm,
                    ys_ref, xbuf, g_f, u_f, d_f, g_s, u_s, d_s, sem, gsem, *, tm):
    i = pl.program_id(0)
    used = i < nu_ref[0]
    e = be_ref[i]
    fresh = (i == 0) | (e != be_ref[jnp.maximum(i - 1, 0)])
    slot = lax.rem(ord_ref[i], 2)
    xslot = lax.rem(i, 2)

    def gather(t_ref, s):
        def issue(t, _):
            pltpu.make_async_copy(h_hbm.at[pl.ds(t_ref[t], 1)], xbuf.at[s, pl.ds(t, 1)],
                                  gsem.at[s]).start()
            return 0

        lax.fori_loop(0, tm, issue, 0, unroll=8)

    @pl.when(used & (i == 0))
    def _():
        gather(tok_ref, 0)

    @pl.when(i + 1 < nu_ref[0])
    def _():
        gather(tokn_ref, 1 - xslot)

    def fetch(expert, s):
        return [pltpu.make_async_copy(w.at[expert], f.at[s], sem.at[s, k])
                for k, (w, f) in enumerate(((wg_hbm, g_f), (wu_hbm, u_f), (wd_hbm, d_f)))]

    @pl.when(used & (i == 0))
    def _():
        for c in fetch(e, slot):
            c.start()

    @pl.when(used & fresh)
    def _():
        for c in fetch(e, slot):
            c.wait()
        nxt = nxt_ref[i]

        @pl.when(nxt >= 0)
        def _():
            for c in fetch(nxt, 1 - slot):
                c.start()

        g_s[...] = g_f[slot].astype(BF16)
        u_s[...] = u_f[slot].astype(BF16)
        d_s[...] = d_f[slot].astype(BF16)

    @pl.when(used)
    def _():
        pltpu.make_async_copy(h_hbm.at[pl.ds(0, tm)], xbuf.at[xslot], gsem.at[xslot]).wait()
        x = xbuf[xslot].astype(BF16)
        a = _silu(_dot(x, g_s[...])) * _dot(x, u_s[...])
        ys_ref[...] = _dot(a.astype(BF16), d_s[...])

    @pl.when(jnp.logical_not(used))
    def _():
        ys_ref[...] = jnp.zeros_like(ys_ref)


def _experts(blk_exp, blk_ord, blk_nxt, n_used, row_tok, h2, w_gate, w_up, w_down, tm):
    d = h2.shape[1]
    de = w_gate.shape[2]
    nb = blk_exp.shape[0]
    anyspec = pl.BlockSpec(memory_space=pl.ANY)
    tok_blk = lambda off: pl.BlockSpec(
        (tm,), lambda i, be, od, nx, nu: (jnp.minimum(i + off, nu[0] - 1),), memory_space=pltpu.SMEM)
    grid_spec = pltpu.PrefetchScalarGridSpec(
        num_scalar_prefetch=4,
        grid=(nb,),
        in_specs=[tok_blk(0), tok_blk(1), anyspec, anyspec, anyspec, anyspec],
        out_specs=pl.BlockSpec((tm, d), lambda i, be, od, nx, nu: (i, 0)),
        scratch_shapes=[pltpu.VMEM((2, tm, d), F32),
                        pltpu.VMEM((2, d, de), F32), pltpu.VMEM((2, d, de), F32),
                        pltpu.VMEM((2, de, d), F32),
                        pltpu.VMEM((d, de), BF16), pltpu.VMEM((d, de), BF16),
                        pltpu.VMEM((de, d), BF16),
                        pltpu.SemaphoreType.DMA((2, 3)), pltpu.SemaphoreType.DMA((2,))],
    )
    return pl.pallas_call(
        functools.partial(_experts_kernel, tm=tm),
        grid_spec=grid_spec,
        out_shape=jax.ShapeDtypeStruct((nb * tm, d), F32),
        compiler_params=_cp(("arbitrary",)),
        name="experts",
    )(blk_exp, blk_ord, blk_nxt, n_used, row_tok, row_tok, h2, w_gate, w_up, w_down)


def _combine_kernel(ids_ref, idn_ref, ys_ref, x1_ref, wt_ref, mod_ref, g_ref, o_ref,
                    buf, sem, *, tc):
    i = pl.program_id(0)
    slot = lax.rem(i, 2)

    def gather(id_ref, s):
        def issue(t, _):
            for k in range(2):
                src = id_ref[2 * t + k]
                pltpu.make_async_copy(ys_ref.at[pl.ds(src, 1)], buf.at[s, k, pl.ds(t, 1)],
                                      sem.at[s]).start()
            return 0

        lax.fori_loop(0, tc, issue, 0, unroll=8)

    @pl.when(i == 0)
    def _():
        gather(ids_ref, 0)

    @pl.when(i + 1 < pl.num_programs(0))
    def _():
        gather(idn_ref, 1 - slot)

    for k in range(2):
        pltpu.make_async_copy(ys_ref.at[pl.ds(0, tc)], buf.at[slot, k], sem.at[slot]).wait()

    moe = buf[slot, 0] * wt_ref[:, 0:1] + buf[slot, 1] * wt_ref[:, 1:2]
    x2 = x1_ref[...] + mod_ref[0, 5:6, :] * moe
    o_ref[...] = x2 * lax.rsqrt(jnp.mean(x2 * x2, axis=-1, keepdims=True) + EPS) * g_ref[...]


def _combine(dest, ys, x1, wts_t, mod3, final_g, seq, tc):
    n, d = x1.shape
    last = n // tc - 1
    return pl.pallas_call(
        functools.partial(_combine_kernel, tc=tc),
        grid=(n // tc,),
        in_specs=[pl.BlockSpec((2 * tc,), lambda i: (i,), memory_space=pltpu.SMEM),
                  pl.BlockSpec((2 * tc,), lambda i: (jnp.minimum(i + 1, last),),
                               memory_space=pltpu.SMEM),
                  pl.BlockSpec(memory_space=pl.ANY),
                  pl.BlockSpec((tc, d), lambda i: (i, 0)),
                  pl.BlockSpec((tc, LANES), lambda i: (i, 0)),
                  pl.BlockSpec((1, 6, d), lambda i: ((i * tc) // seq, 0, 0)),
                  pl.BlockSpec((1, d), lambda i: (0, 0))],
        out_specs=pl.BlockSpec((tc, d), lambda i: (i, 0)),
        out_shape=jax.ShapeDtypeStruct((n, d), F32),
        scratch_shapes=[pltpu.VMEM((2, 2, tc, d), F32), pltpu.SemaphoreType.DMA((2,))],
        compiler_params=_cp(("arbitrary",)),
        name="combine",
    )(dest, dest, ys, x1, wts_t, mod3, final_g.reshape(1, d))


def _token_mixer(x2, mod3, norm1_g, w_in, ret_gn_g, cmp_pos_k, cmp_w1_k, cmp_w2_k,
                 cmp_pos_v, cmp_w1_v, cmp_w2_v, bsz, seq):
    n, d = x2.shape
    w_t = w_in.T
    w_gate_t = jnp.pad(w_t[PROJ_MAIN:], ((0, LANES - N_GATE_COLS), (0, 0))).astype(BF16)
    proj, gate_logits = _inproj(x2, mod3, norm1_g, w_t, w_gate_t, seq)

    o_ret = _retention(proj, ret_gn_g, bsz, seq)

    g_ = NSA_KV_GROUPS
    crow = seq // CMP_STRIDE

    pe_flat = lambda pe: pe.reshape(1, CMP_BLOCK * NSA_D)
    kc, vc = _compress(proj, pe_flat(cmp_pos_k), pe_flat(cmp_pos_v),
                       cmp_w1_k, cmp_w1_v, cmp_w2_k, cmp_w2_v, bsz, seq)

    o_nsa = _nsa(proj, kc, vc, gate_logits, bsz, seq)
    return o_ret, o_nsa


def _moe(h2, lt, x1, mod3, final_g, b_grp, b_exp, w_gate, w_up, w_down, seq, tm):
    n, d = h2.shape
    bias_col = jnp.zeros((LANES,), F32).at[:N_GROUPS].set(b_grp).at[N_GROUPS:N_GROUPS + N_EXPERTS].set(b_exp)
    bias_col = jnp.broadcast_to(bias_col[:, None], (LANES, LANES))
    ids, wts, cnt = _route(lt, bias_col)

    counts = cnt[:, 0]
    padded = (counts + tm - 1) // tm * tm
    pad_end = jnp.cumsum(padded)
    pad_start = (pad_end - padded).astype(I32)
    nb = (2 * n) // tm + N_EXPERTS
    n_used = (pad_end[-1] // tm).astype(I32).reshape(1)
    blk_start = jnp.arange(nb, dtype=I32) * tm
    blk_exp = jnp.minimum(jnp.sum((pad_end[None, :] <= blk_start[:, None]).astype(I32), axis=1),
                          N_EXPERTS - 1).astype(I32)
    last_exp = blk_exp[jnp.maximum(n_used[0] - 1, 0)]
    blk_exp = jnp.where(jnp.arange(nb) < n_used[0], blk_exp, last_exp)
    eid = jnp.arange(N_EXPERTS, dtype=I32)
    has = counts > 0
    exp_ord = jnp.sum((has[None, :] & (eid[None, :] < eid[:, None])).astype(I32), axis=1)
    exp_nxt = jnp.min(jnp.where(has[None, :] & (eid[None, :] > eid[:, None]), eid[None, :], N_EXPERTS), axis=1)
    exp_nxt = jnp.where(exp_nxt < N_EXPERTS, exp_nxt, -1).astype(I32)
    blk_ord = exp_ord[blk_exp]
    blk_nxt = exp_nxt[blk_exp]

    dest = _dest(ids, pad_start)
    row_tok = _rowtok(dest, nb * tm)
    ys = _experts(blk_exp, blk_ord, blk_nxt, n_used, row_tok, h2, w_gate, w_up, w_down, tm)
    wts_t = jnp.pad(wts[:2].T, ((0, 0), (0, LANES - 2)))
    return _combine(dest, ys, x1, wts_t, mod3, final_g, seq, min(256, n))


def kernel(x, c, w_ada, b_ada, norm1_g, norm2_g, final_g, w_in, ret_gn_g, cmp_pos_k, cmp_w1_k,
           cmp_w2_k, cmp_pos_v, cmp_w1_v, cmp_w2_v, w_out, w_grp, b_grp, w_exp, b_exp, w_gate,
           w_up, w_down):
    bsz, seq, d = x.shape
    assert w_ada.shape[0] == 1, "single-layer block"
    n = bsz * seq
    x2 = x.reshape(n, d)
    mod3 = _ada(c, w_ada[0], b_ada[0]).reshape(bsz, 6, d)

    o_ret, o_nsa = _token_mixer(x2, mod3, norm1_g[0], w_in[0], ret_gn_g[0], cmp_pos_k[0],
                                cmp_w1_k[0], cmp_w2_k[0], cmp_pos_v[0], cmp_w1_v[0], cmp_w2_v[0],
                                bsz, seq)

    w_route = jnp.concatenate([w_grp[0], w_exp[0]], axis=1)
    w_route = jnp.pad(w_route, ((0, 0), (0, LANES - w_route.shape[1]))).T.astype(BF16)
    x1, h2, lt = _outproj(o_ret, o_nsa, x2, mod3, norm2_g[0], w_out[0].astype(BF16), w_route, seq)

    out = _moe(h2, lt, x1, mod3, final_g, b_grp[0], b_exp[0], w_gate[0], w_up[0], w_down[0],
               seq, 256)
    return out.reshape(bsz, seq, d)
```

```python
import functools
import math

import numpy as np
import jax
import jax.numpy as jnp
from jax import lax
from jax.experimental import pallas as pl
from jax.experimental.pallas import tpu as pltpu

F32 = jnp.float32
BF16 = jnp.bfloat16
I32 = jnp.int32

RET_HEADS = 4
RET_DK = 256
RET_DV = 256
RET_CHUNK = 128
NSA_HEADS = 8
NSA_KV_GROUPS = 2
NSA_HPG = NSA_HEADS // NSA_KV_GROUPS
NSA_D = 128
CMP_BLOCK = 32
CMP_STRIDE = 16
CMP_HIDDEN = 256
SEL_BLOCK = 64
SEL_COUNT = 16
WIN = 512
N_GROUPS = 8
EXP_PER_GROUP = 8
N_EXPERTS = N_GROUPS * EXP_PER_GROUP
D_EXPERT = 512
ROPE_BASE = 10000.0
EPS = 1e-6
NEG = -1e30
FORCE_BONUS = 1e4

LANES = 128
SUBLANES = 8
VMEM_LIMIT = 52 * 1024 * 1024

_C_RQ, _C_RK, _C_RV, _C_RG = 0, 8, 16, 24
_C_NQ = 32
_C_KC, _C_VC, _C_KS, _C_VS, _C_KW, _C_VW = 40, 42, 44, 46, 48, 50
_C_GATE = 52
PROJ_MAIN = _C_GATE * LANES
N_GATE_COLS = NSA_HEADS * 3


def _cp(sem, vmem=VMEM_LIMIT):
    return pltpu.CompilerParams(dimension_semantics=sem, vmem_limit_bytes=vmem)


def _silu(v):
    return v * jax.nn.sigmoid(v)


def _dot(a, b):
    return jnp.dot(a, b, preferred_element_type=F32)


def _dot_nt(a, b):
    return lax.dot_general(a, b, (((1,), (1,)), ((), ())), preferred_element_type=F32)


def _dot_tn(a, b):
    return lax.dot_general(a, b, (((0,), (0,)), ((), ())), preferred_element_type=F32)


def _ada_kernel(c_ref, w_ref, b_ref, o_ref):
    ca = _silu(c_ref[...]).astype(BF16)
    o_ref[...] = _dot(ca, w_ref[...].astype(BF16)) + b_ref[...]


def _ada(c, w, b):
    bsz, d = c.shape
    n = w.shape[1]
    tn = min(1024, n)
    return pl.pallas_call(
        _ada_kernel,
        grid=(n // tn,),
        in_specs=[pl.BlockSpec((bsz, d), lambda j: (0, 0)),
                  pl.BlockSpec((d, tn), lambda j: (0, j)),
                  pl.BlockSpec((1, tn), lambda j: (0, j))],
        out_specs=pl.BlockSpec((bsz, tn), lambda j: (0, j)),
        out_shape=jax.ShapeDtypeStruct((bsz, n), F32),
        compiler_params=_cp(("arbitrary",)),
        name="ada",
    )(c, w, b.reshape(1, n))


def _inproj_kernel(x_ref, mod_ref, g_ref, wt_ref, wgt_ref, proj_ref, gate_ref, h_ref):
    @pl.when(pl.program_id(1) == 0)
    def _():
        x = x_ref[...]
        y = x * lax.rsqrt(jnp.mean(x * x, axis=-1, keepdims=True) + EPS) * g_ref[...]
        h = (y * (1.0 + mod_ref[0, 1:2, :]) + mod_ref[0, 0:1, :]).astype(BF16)
        h_ref[...] = h
        gate_ref[...] = _dot_nt(h, wgt_ref[...])

    proj_ref[...] = _dot_nt(h_ref[...], wt_ref[...].astype(BF16)).astype(BF16)


def _inproj(x2, mod3, g, w_t, wg_t, seq):
    n, d = x2.shape
    tm = min(1024, seq)
    tn = 512
    nj = PROJ_MAIN // tn
    return pl.pallas_call(
        _inproj_kernel,
        grid=(n // tm, nj),
        in_specs=[pl.BlockSpec((tm, d), lambda i, j: (i, 0)),
                  pl.BlockSpec((1, 6, d), lambda i, j: ((i * tm) // seq, 0, 0)),
                  pl.BlockSpec((1, d), lambda i, j: (0, 0)),
                  pl.BlockSpec((tn, d), lambda i, j: (j, 0)),
                  pl.BlockSpec((LANES, d), lambda i, j: (0, 0))],
        out_specs=[pl.BlockSpec((tm, tn), lambda i, j: (i, j)),
                   pl.BlockSpec((tm, LANES), lambda i, j: (i, 0))],
        out_shape=[jax.ShapeDtypeStruct((n, PROJ_MAIN), BF16),
                   jax.ShapeDtypeStruct((n, LANES), F32)],
        scratch_shapes=[pltpu.VMEM((tm, d), BF16)],
        compiler_params=_cp(("arbitrary", "arbitrary")),
        name="inproj",
    )(x2, mod3, g.reshape(1, d), w_t, wg_t)


def _retention_kernel(q_ref, k_ref, v_ref, g_ref, cos_ref, sin_ref, din_ref, zeta_ref,
                      qdec_ref, cdec_ref, gn_ref, o_ref, s_ref):
    @pl.when(pl.program_id(1) == 0)
    def _():
        s_ref[...] = jnp.zeros_like(s_ref)

    half = RET_DK // 2
    c = RET_CHUNK

    for sub in range(q_ref.shape[0] // c):
        rows = slice(sub * c, (sub + 1) * c)
        cos = cos_ref[rows, :]
        sin = sin_ref[rows, :]

        def rot(a):
            a1, a2 = a[:, :half], a[:, half:]
            return jnp.concatenate([a1 * cos - a2 * sin, a1 * sin + a2 * cos], axis=1)

        for h in range(RET_HEADS):
            qs = slice(h * RET_DK, (h + 1) * RET_DK)
            vs = slice(h * RET_DV, (h + 1) * RET_DV)
            q = rot(q_ref[rows, qs].astype(F32))
            k = rot(k_ref[rows, qs].astype(F32)) * (RET_DK ** -0.5)
            v = v_ref[rows, vs]
            qb = q.astype(BF16)
            kb = k.astype(BF16)
            s = _dot_nt(qb, kb) * din_ref[h]
            inner = _dot(s.astype(BF16), v)
            s_prev = s_ref[h]
            cross = _dot(qb, s_prev.astype(BF16)) * qdec_ref[h]
            kv = _dot_tn((k * zeta_ref[h]).astype(BF16), v)
            s_ref[h] = cdec_ref[h] * s_prev + kv
            o = inner + cross
            mu = jnp.mean(o, axis=-1, keepdims=True)
            oc = o - mu
            var = jnp.mean(oc * oc, axis=-1, keepdims=True)
            o = oc * lax.rsqrt(var + EPS) * gn_ref[:, vs]
            o_ref[rows, vs] = (o * _silu(g_ref[rows, vs].astype(F32))).astype(BF16)


def _retention(proj, gn_g, bsz, seq):
    n = proj.shape[0]
    c = RET_CHUNK
    nc = seq // c
    hw = RET_HEADS * RET_DK
    half = RET_DK // 2
    pos = jnp.arange(seq, dtype=F32)
    inv = ROPE_BASE ** (-jnp.arange(half, dtype=F32) / half)
    ang = pos[:, None] * inv[None, :]
    cos, sin = jnp.cos(ang), jnp.sin(ang)
    log_gamma = jnp.log1p(-jnp.exp2(-5.0 - jnp.arange(RET_HEADS, dtype=F32)))
    idx = jnp.arange(c, dtype=F32)
    rel = idx[:, None] - idx[None, :]
    decay_in = jnp.where(rel >= 0, jnp.exp(log_gamma[:, None, None] * jnp.maximum(rel, 0.0)), 0.0)
    zeta = jnp.exp(log_gamma[:, None] * (c - 1 - idx)[None, :])[:, :, None]
    q_decay = jnp.exp(log_gamma[:, None] * (idx + 1)[None, :])[:, :, None]
    chunk_decay = jnp.exp(log_gamma * c)[:, None, None]
    rb = min(4 * c, seq)
    ns = seq // rb
    row = lambda b, t: (b * ns + t)
    return pl.pallas_call(
        _retention_kernel,
        grid=(bsz, ns),
        in_specs=[pl.BlockSpec((rb, hw), lambda b, t: (row(b, t), 0)),
                  pl.BlockSpec((rb, hw), lambda b, t: (row(b, t), 1)),
                  pl.BlockSpec((rb, hw), lambda b, t: (row(b, t), 2)),
                  pl.BlockSpec((rb, hw), lambda b, t: (row(b, t), 3)),
                  pl.BlockSpec((rb, half), lambda b, t: (t, 0)),
                  pl.BlockSpec((rb, half), lambda b, t: (t, 0)),
                  pl.BlockSpec((RET_HEADS, c, c), lambda b, t: (0, 0, 0)),
                  pl.BlockSpec((RET_HEADS, c, 1), lambda b, t: (0, 0, 0)),
                  pl.BlockSpec((RET_HEADS, c, 1), lambda b, t: (0, 0, 0)),
                  pl.BlockSpec((RET_HEADS, 1, 1), lambda b, t: (0, 0, 0)),
                  pl.BlockSpec((1, hw), lambda b, t: (0, 0))],
        out_specs=pl.BlockSpec((rb, hw), lambda b, t: (row(b, t), 0)),
        out_shape=jax.ShapeDtypeStruct((n, hw), BF16),
        scratch_shapes=[pltpu.VMEM((RET_HEADS, RET_DK, RET_DV), F32)],
        compiler_params=_cp(("arbitrary", "arbitrary")),
        name="retention",
    )(proj, proj, proj, proj, cos, sin, decay_in, zeta, q_decay, chunk_decay,
      gn_g.reshape(1, hw))


def _compress_kernel(ak_ref, av_ref, pek_ref, pev_ref, w1k_ref, w1v_ref, w2k_ref, w2v_ref,
                     kc_ref, vc_ref, a_scr):
    seq = ak_ref.shape[0]
    nblk = seq // CMP_STRIDE
    a_scr[seq:, :] = jnp.zeros((a_scr.shape[0] - seq, NSA_D), F32)

    def one(a_ref, pe_ref, w1_ref, w2_ref, o_ref):
        a_scr[:seq, :] = a_ref[...].astype(F32)
        flat = jnp.concatenate([a_scr[pl.ds(l, nblk, stride=CMP_STRIDE), :] for l in range(CMP_BLOCK)],
                               axis=1)
        pre = _dot((flat + pe_ref[...]).astype(BF16), w1_ref[...].astype(BF16))
        o_ref[...] = _dot(_silu(pre).astype(BF16), w2_ref[...].astype(BF16)).astype(BF16)

    one(ak_ref, pek_ref, w1k_ref, w2k_ref, kc_ref)
    one(av_ref, pev_ref, w1v_ref, w2v_ref, vc_ref)


def _compress(proj, pek, pev, w1k, w1v, w2k, w2v, bsz, seq):
    g_ = NSA_KV_GROUPS
    nblk = seq // CMP_STRIDE
    full = lambda a: pl.BlockSpec(a.shape, lambda i: (0,) * a.ndim)
    kblk = pl.BlockSpec((seq, NSA_D), lambda i: (i // g_, _C_KC + i % g_))
    vblk = pl.BlockSpec((seq, NSA_D), lambda i: (i // g_, _C_VC + i % g_))
    oblk = pl.BlockSpec((nblk, NSA_D), lambda i: (i, 0))
    return pl.pallas_call(
        _compress_kernel,
        grid=(bsz * g_,),
        in_specs=[kblk, vblk, full(pek), full(pev), full(w1k), full(w1v), full(w2k), full(w2v)],
        out_specs=[oblk, oblk],
        out_shape=[jax.ShapeDtypeStruct((bsz * g_ * nblk, NSA_D), BF16)] * 2,
        scratch_shapes=[pltpu.VMEM((seq + CMP_BLOCK, NSA_D), F32)],
        compiler_params=_cp(("arbitrary",)),
        name="compress",
    )(proj, proj, pek, pev, w1k, w1v, w2k, w2v)


def _nsa_kernel(q_ref, kc_ref, vc_ref, ks_ref, vs_ref, kw_ref, vw_ref, gate_ref, ovt_ref,
                o_ref, kaug_ref, vst_ref, vwt_ref, *, tq, tk, vt, seq, ncmp, wlen):
    crow = kc_ref.shape[0]
    i = pl.program_id(2)
    hg = NSA_HPG
    d = NSA_D
    r = hg * tq

    @pl.when(i == 0)
    def _():
        kaug_ref[:, :d] = ks_ref[...]
        blk = lax.broadcasted_iota(I32, (seq, LANES), 0) // SEL_BLOCK
        lane = lax.broadcasted_iota(I32, (seq, LANES), 1)
        kaug_ref[:, d:] = (blk == lane).astype(BF16)
        for c in range(seq // vt):
            vst_ref[c] = vs_ref[c * vt:(c + 1) * vt, :].astype(F32).T.astype(BF16)
            vwt_ref[c] = vw_ref[c * vt:(c + 1) * vt, :].astype(F32).T.astype(BF16)

    def pv_t(vt_ref, first_tile, p):
        out = None
        for c in range(p.shape[0] // vt):
            part = _dot(vt_ref[first_tile + c], p[c * vt:(c + 1) * vt, :])
            out = part if out is None else out + part
        return out

    q = q_ref[...]
    qh = [(q[:, h * d:(h + 1) * d].astype(F32) * (d ** -0.5)).astype(BF16) for h in range(hg)]
    qa = jnp.concatenate(qh, axis=0)
    t1 = i * tq + lax.broadcasted_iota(I32, (tq, 1), 0)
    tpos = jnp.concatenate([t1] * hg, axis=0)
    t_row = i * tq + lax.broadcasted_iota(I32, (1, tq), 1)
    tpos_row = jnp.concatenate([t_row] * hg, axis=1)

    w0 = pl.multiple_of(jnp.maximum(i * tq + tq - wlen, 0), vt)
    sw = _dot_nt(kw_ref[pl.ds(w0, wlen), :], qa)
    delta = tpos_row - (w0 + lax.broadcasted_iota(I32, (wlen, 1), 0))
    sw = jnp.where((delta >= 0) & (delta < WIN), sw, NEG)
    e_w = jnp.exp(sw - jnp.max(sw, axis=0, keepdims=True))
    o_win = (pv_t(vwt_ref, w0 // vt, e_w.astype(BF16)) / jnp.sum(e_w, axis=0, keepdims=True)).T

    sct = _dot_nt(kc_ref[...], qa)
    cidx = lax.broadcasted_iota(I32, (crow, 1), 0)
    cmask = (cidx * CMP_STRIDE + (CMP_BLOCK - 1) <= tpos_row) & (cidx < ncmp)
    sct = jnp.where(cmask, sct, NEG)
    e_c = jnp.where(cmask, jnp.exp(sct - jnp.max(sct, axis=0, keepdims=True)), 0.0)
    den_c = jnp.sum(e_c, axis=0, keepdims=True)
    p_t = jnp.where(den_c > 0.0, e_c / jnp.where(den_c > 0.0, den_c, 1.0), 0.0)
    o_cmp = _dot_tn(p_t.astype(BF16), vc_ref[...])

    psum_t = p_t[:, 0:tq]
    for h in range(1, hg):
        psum_t = psum_t + p_t[:, h * tq:(h + 1) * tq]
    nsel_blocks = seq // SEL_BLOCK
    nsb = ovt_ref.shape[0]
    imp_t = jnp.dot(ovt_ref[...], psum_t, preferred_element_type=F32,
                    precision=lax.Precision.HIGHEST)
    sidx = lax.broadcasted_iota(I32, (nsb, 1), 0)
    cur = t_row // SEL_BLOCK
    valid = sidx <= cur
    forced = (sidx == 0) | (sidx == cur) | (sidx == cur - 1)
    score = jnp.where(valid, imp_t + jnp.where(forced, FORCE_BONUS, 0.0), -1.0)
    rank = jnp.zeros((nsb, tq), F32)
    for s2 in range(nsel_blocks):
        row = score[s2:s2 + 1, :]
        beats = (row > score) | ((row == score) & (sidx > s2))
        rank = rank + beats.astype(F32)
    sel = valid & (rank < float(min(SEL_COUNT, nsel_blocks)))
    selb_t = jnp.where(sel, 0.0, NEG)
    selb_t = jnp.concatenate([selb_t, jnp.zeros((LANES - nsb, tq), F32)], axis=0)
    selb = selb_t.T.astype(BF16)
    q_aug = jnp.concatenate([jnp.concatenate([qh[h], selb], axis=1) for h in range(hg)], axis=0)

    n_full = (i * tq + 1) // tk
    kl = lax.broadcasted_iota(I32, (tk, 1), 0)

    def tile(j, carry, masked):
        m, l, acc = carry
        k0 = pl.multiple_of(j * tk, tk)
        s = _dot_nt(kaug_ref[pl.ds(k0, tk), :], q_aug)
        if masked:
            s = jnp.where(k0 + kl <= tpos_row, s, NEG)
        m_new = jnp.maximum(m, jnp.max(s, axis=0, keepdims=True))
        alpha = jnp.exp(m - m_new)
        p = jnp.exp(s - m_new)
        l = alpha * l + jnp.sum(p, axis=0, keepdims=True)
        acc = alpha * acc + pv_t(vst_ref, j * (tk // vt), p.astype(BF16))
        return m_new, l, acc

    m0 = jnp.full((1, r), NEG, F32)
    l0 = jnp.zeros((1, r), F32)
    a0 = jnp.zeros((d, r), F32)
    carry = lax.fori_loop(0, n_full, functools.partial(tile, masked=False), (m0, l0, a0))
    _, l_s, acc_s = tile(n_full, carry, True)
    o_slc = (acc_s / l_s).T

    gt = jax.nn.sigmoid(gate_ref[...])
    grp = pl.program_id(1)
    for g in range(1, NSA_KV_GROUPS):
        gt = jnp.where(grp == g, pltpu.roll(gt, LANES - g * 3 * hg, 1), gt)
    for h in range(hg):
        rows = slice(h * tq, (h + 1) * tq)
        o = (gt[:, 3 * h:3 * h + 1] * o_cmp[rows] + gt[:, 3 * h + 1:3 * h + 2] * o_slc[rows]
             + gt[:, 3 * h + 2:3 * h + 3] * o_win[rows])
        o_ref[:, h * d:(h + 1) * d] = o.astype(BF16)


def _nsa(proj, kc, vc, gates_g, bsz, seq):
    n = proj.shape[0]
    g_ = NSA_KV_GROUPS
    tq = 256
    tk = min(512, seq)
    nq = seq // tq
    ncmp = (seq - CMP_BLOCK) // CMP_STRIDE + 1
    wlen = min(WIN + tq, seq)
    nsel = seq // SEL_BLOCK
    crow = seq // CMP_STRIDE
    assert nsel <= LANES and crow <= LANES and tq <= tk and tk % tq == 0
    nsb = -(-nsel // 8) * 8
    ss = np.arange(nsb)[:, None] * SEL_BLOCK
    cs = np.arange(crow)[None, :] * CMP_STRIDE
    ov = ((cs < ss + SEL_BLOCK) & (cs + CMP_BLOCK > ss)
          & (np.arange(crow)[None, :] < ncmp) & (np.arange(nsb)[:, None] < nsel))
    ov = jnp.asarray(ov.astype(np.float32))
    kvspec = lambda c0: pl.BlockSpec((seq, NSA_D), lambda b, g, i: (b, c0 + g))
    vt = 256
    assert tk % vt == 0 and wlen % vt == 0 and tq % vt == 0 and seq % vt == 0
    kern = functools.partial(_nsa_kernel, tq=tq, tk=tk, vt=vt, seq=seq, ncmp=ncmp, wlen=wlen)
    return pl.pallas_call(
        kern,
        grid=(bsz, g_, nq),
        in_specs=[pl.BlockSpec((tq, NSA_HPG * NSA_D), lambda b, g, i: (b * nq + i, _C_NQ // NSA_HPG + g)),
                  pl.BlockSpec((crow, NSA_D), lambda b, g, i: (b * g_ + g, 0)),
                  pl.BlockSpec((crow, NSA_D), lambda b, g, i: (b * g_ + g, 0)),
                  kvspec(_C_KS), kvspec(_C_VS), kvspec(_C_KW), kvspec(_C_VW),
                  pl.BlockSpec((tq, LANES), lambda b, g, i: (b * nq + i, 0)),
                  pl.BlockSpec((nsb, crow), lambda b, g, i: (0, 0))],
        out_specs=pl.BlockSpec((tq, NSA_HPG * NSA_D), lambda b, g, i: (b * nq + i, g)),
        out_shape=jax.ShapeDtypeStruct((n, NSA_HEADS * NSA_D), BF16),
        scratch_shapes=[pltpu.VMEM((seq, 2 * NSA_D), BF16),
                        pltpu.VMEM((seq // vt, NSA_D, vt), BF16),
                        pltpu.VMEM((seq // vt, NSA_D, vt), BF16)],
        compiler_params=_cp(("arbitrary", "arbitrary", "arbitrary")),
        name="nsa",
    )(proj, kc, vc, proj, proj, proj, proj, gates_g, ov)


def _outproj_kernel(oret_ref, onsa_ref, x_ref, mod_ref, g_ref, w_ref, wr_ref,
                    x1_ref, h2_ref, lt_ref):
    hw = oret_ref.shape[1]
    mix = _dot(oret_ref[...], w_ref[:hw, :]) + _dot(onsa_ref[...], w_ref[hw:, :])
    x1 = x_ref[...] + mod_ref[0, 2:3, :] * mix
    x1_ref[...] = x1
    y = x1 * lax.rsqrt(jnp.mean(x1 * x1, axis=-1, keepdims=True) + EPS) * g_ref[...]
    h2 = y * (1.0 + mod_ref[0, 4:5, :]) + mod_ref[0, 3:4, :]
    h2_ref[...] = h2
    lt_ref[...] = _dot_nt(wr_ref[...], h2.astype(BF16))


def _outproj(o_ret, o_nsa, x2, mod3, g, w_bf, wr_bf, seq):
    n, d = x2.shape
    tm = min(512, seq)
    hw = o_ret.shape[1]
    return pl.pallas_call(
        _outproj_kernel,
        grid=(n // tm,),
        in_specs=[pl.BlockSpec((tm, hw), lambda i: (i, 0)),
                  pl.BlockSpec((tm, o_nsa.shape[1]), lambda i: (i, 0)),
                  pl.BlockSpec((tm, d), lambda i: (i, 0)),
                  pl.BlockSpec((1, 6, d), lambda i: ((i * tm) // seq, 0, 0)),
                  pl.BlockSpec((1, d), lambda i: (0, 0)),
                  pl.BlockSpec(w_bf.shape, lambda i: (0, 0)),
                  pl.BlockSpec(wr_bf.shape, lambda i: (0, 0))],
        out_specs=[pl.BlockSpec((tm, d), lambda i: (i, 0)),
                   pl.BlockSpec((tm, d), lambda i: (i, 0)),
                   pl.BlockSpec((LANES, tm), lambda i: (0, i))],
        out_shape=[jax.ShapeDtypeStruct((n, d), F32),
                   jax.ShapeDtypeStruct((n, d), F32),
                   jax.ShapeDtypeStruct((LANES, n), F32)],
        compiler_params=_cp(("arbitrary",)),
        name="outproj",
    )(o_ret, o_nsa, x2, mod3, g.reshape(1, d), w_bf, wr_bf)


def _route_kernel(lt_ref, b_ref, tri_ref, ids_ref, wts_ref, cnt_ref, carry_ref, *, sub):
    @pl.when(pl.program_id(0) == 0)
    def _():
        carry_ref[...] = jnp.zeros_like(carry_ref)

    ng, ne = N_GROUPS, EXP_PER_GROUP
    l = lt_ref[...] + b_ref[:, 0:1]
    tc = l.shape[1]
    ridx = lax.broadcasted_iota(I32, (ng, tc), 0).astype(F32)

    def softmax0(v):
        e = jnp.exp(v - jnp.max(v, axis=0, keepdims=True))
        return e / jnp.sum(e, axis=0, keepdims=True)

    def top1(p):
        top = jnp.max(p, axis=0, keepdims=True)
        idx = jnp.min(jnp.where(p == top, ridx, float(ng)), axis=0, keepdims=True)
        return top, idx

    pg_top, grp = top1(softmax0(l[0:ng]))
    leg = jnp.zeros((ne, tc), F32)
    for g in range(ng):
        leg = jnp.where(grp == float(g), l[ng + g * ne:ng + (g + 1) * ne], leg)
    pe = softmax0(leg)
    p1, i1 = top1(pe)
    p2, i2 = top1(jnp.where(ridx == i1, -1.0, pe))
    den = p1 + p2
    w1 = pg_top * p1 / den
    w2 = pg_top * p2 / den
    e1 = grp * float(ne) + i1
    e2 = grp * float(ne) + i2

    eio = lax.broadcasted_iota(I32, (N_EXPERTS, sub), 0).astype(F32)
    r1 = []
    r2 = []
    carry = carry_ref[:, 0:1]
    for c in range(tc // sub):
        cs = slice(c * sub, (c + 1) * sub)
        oh1 = (eio == e1[:, cs]).astype(F32)
        oh2 = (eio == e2[:, cs]).astype(F32)
        oh = oh1 + oh2
        before = carry + _dot(oh.astype(BF16), tri_ref[...])
        r1.append(jnp.sum(oh1 * before, axis=0, keepdims=True))
        r2.append(jnp.sum(oh2 * before, axis=0, keepdims=True))
        carry = carry + jnp.sum(oh, axis=1, keepdims=True)
    carry_ref[...] = jnp.broadcast_to(carry, carry_ref.shape)
    cnt_ref[...] = jnp.broadcast_to(carry, cnt_ref.shape).astype(I32)
    r1 = jnp.concatenate(r1, axis=1)
    r2 = jnp.concatenate(r2, axis=1)
    zf = jnp.zeros((4, tc), F32)
    ids_ref[...] = jnp.concatenate([e1, e2, r1, r2, zf], axis=0).astype(I32)
    wts_ref[...] = jnp.concatenate([w1, w2, jnp.zeros((6, tc), F32)], axis=0)


def _route(lt, bias_col):
    n = lt.shape[1]
    tc = min(2048, n)
    sub = min(512, tc)
    tri = jnp.asarray(np.triu(np.ones((sub, sub), np.float32), 1), BF16)
    return pl.pallas_call(
        functools.partial(_route_kernel, sub=sub),
        grid=(n // tc,),
        in_specs=[pl.BlockSpec((LANES, tc), lambda i: (0, i)),
                  pl.BlockSpec((LANES, LANES), lambda i: (0, 0)),
                  pl.BlockSpec((sub, sub), lambda i: (0, 0))],
        out_specs=[pl.BlockSpec((8, tc), lambda i: (0, i)),
                   pl.BlockSpec((8, tc), lambda i: (0, i)),
                   pl.BlockSpec((N_EXPERTS, LANES), lambda i: (0, 0))],
        out_shape=[jax.ShapeDtypeStruct((8, n), I32),
                   jax.ShapeDtypeStruct((8, n), F32),
                   jax.ShapeDtypeStruct((N_EXPERTS, LANES), I32)],
        scratch_shapes=[pltpu.VMEM((N_EXPERTS, LANES), F32)],
        compiler_params=_cp(("arbitrary",)),
        name="route",
    )(lt, bias_col, tri)


def _dest_kernel(ids_ref, ps_ref, o_ref):
    ids = ids_ref[...].astype(F32)
    tc = ids.shape[1]
    eio = lax.broadcasted_iota(I32, (N_EXPERTS, tc), 0).astype(F32)
    ps = ps_ref[:, 0:1]
    rows = [jnp.sum(jnp.where(eio == ids[k:k + 1], ps, 0.0), axis=0, keepdims=True) + ids[2 + k:3 + k]
            for k in range(2)]
    o_ref[...] = jnp.concatenate(rows + [jnp.zeros((6, tc), F32)], axis=0).astype(I32)


def _dest(ids, pad_start):
    n = ids.shape[1]
    tc = min(2048, n)
    ps = jnp.broadcast_to(pad_start.astype(F32)[:, None], (N_EXPERTS, LANES))
    dest = pl.pallas_call(
        _dest_kernel,
        grid=(n // tc,),
        in_specs=[pl.BlockSpec((8, tc), lambda i: (0, i)),
                  pl.BlockSpec((N_EXPERTS, LANES), lambda i: (0, 0))],
        out_specs=pl.BlockSpec((8, tc), lambda i: (0, i)),
        out_shape=jax.ShapeDtypeStruct((8, n), I32),
        compiler_params=_cp(("arbitrary",)),
        name="dest",
    )(ids, ps)
    return dest[:2].T.reshape(2 * n)


def _dispatch_kernel(dst_ref, ps_ref, cnt_ref, h_ref, xs_ref, zbuf, sem, zsem, *, tcd, tm, zr):
    @pl.when(pl.program_id(0) == 0)
    def _():
        zbuf[...] = jnp.zeros_like(zbuf)
        sizes = [zr >> b for b in range(zr.bit_length()) if (zr >> b) >= SUBLANES]

        def zero_rows(row, size):
            return pltpu.make_async_copy(zbuf.at[pl.ds(0, size)],
                                         xs_ref.at[pl.ds(pl.multiple_of(row, SUBLANES), size)], zsem)

        def zero_row(row):
            return pltpu.make_async_copy(zbuf.at[pl.ds(0, 1)], xs_ref.at[pl.ds(row, 1)], zsem)

        def fill(e, wait):
            cnt = cnt_ref[e]
            cnt8 = lax.div(cnt + (SUBLANES - 1), SUBLANES) * SUBLANES
            gap = lax.rem(tm - lax.rem(cnt8, tm), tm)
            base = ps_ref[e]

            def single(c, _):
                cp = zero_row(base + cnt + c)
                cp.wait() if wait else cp.start()
                return 0

            lax.fori_loop(0, cnt8 - cnt, single, 0)
            row = base + cnt8
            for size in sizes:
                has = lax.rem(lax.div(gap, size), 2) == 1

                @pl.when(has)
                def _():
                    cp = zero_rows(row, size)
                    cp.wait() if wait else cp.start()

                row = row + jnp.where(has, size, 0)

        last = N_EXPERTS - 1
        used_rows = ps_ref[last] + lax.div(cnt_ref[last] + (tm - 1), tm) * tm
        n_tail = lax.div(xs_ref.shape[0] - used_rows, zr)

        def tail(c, wait):
            cp = zero_rows(used_rows + c * zr, zr)
            cp.wait() if wait else cp.start()

        for wait in (False, True):
            lax.fori_loop(0, N_EXPERTS, lambda e, _: (fill(e, wait), 0)[1], 0)
            lax.fori_loop(0, n_tail, lambda c, _: (tail(c, wait), 0)[1], 0)

    def issue(t, _):
        for k in range(2):
            dst = dst_ref[2 * t + k]
            pltpu.make_async_copy(h_ref.at[pl.ds(t, 1)], xs_ref.at[pl.ds(dst, 1)], sem).start()
        return 0

    lax.fori_loop(0, tcd, issue, 0, unroll=8)
    for k in range(2):
        pltpu.make_async_copy(h_ref, xs_ref.at[pl.ds(0, tcd)], sem).wait()


def _dispatch(dest, pad_start, counts, h2, rows, tcd, tm):
    n, d = h2.shape
    assert tm & (tm - 1) == 0 and tm >= 2 * SUBLANES
    zr = tm // 2
    return pl.pallas_call(
        functools.partial(_dispatch_kernel, tcd=tcd, tm=tm, zr=zr),
        grid=(n // tcd,),
        in_specs=[pl.BlockSpec((2 * tcd,), lambda i: (i,), memory_space=pltpu.SMEM),
                  pl.BlockSpec(memory_space=pltpu.SMEM),
                  pl.BlockSpec(memory_space=pltpu.SMEM),
                  pl.BlockSpec((tcd, d), lambda i: (i, 0))],
        out_specs=pl.BlockSpec(memory_space=pl.ANY),
        out_shape=jax.ShapeDtypeStruct((rows, d), h2.dtype),
        scratch_shapes=[pltpu.VMEM((zr, d), h2.dtype), pltpu.SemaphoreType.DMA(()),
                        pltpu.SemaphoreType.DMA(())],
        compiler_params=_cp(("arbitrary",)),
        name="dispatch",
    )(dest, pad_start, counts, h2)


def _rowtok_kernel(dst_ref, ps_ref, cnt_ref, o_ref, *, chunk, tm):
    step = pl.program_id(0)

    @pl.when(step == 0)
    def _():
        def clear(row, _):
            o_ref[row] = 0
            return 0

        def pad(e, _):
            first = ps_ref[e] + cnt_ref[e]
            lax.fori_loop(first, ps_ref[e] + lax.div(cnt_ref[e] + (tm - 1), tm) * tm, clear, 0)
            return 0

        lax.fori_loop(0, N_EXPERTS, pad, 0)
        last = N_EXPERTS - 1
        used_rows = ps_ref[last] + lax.div(cnt_ref[last] + (tm - 1), tm) * tm
        lax.fori_loop(used_rows, o_ref.shape[0], clear, 0)

    base = step * chunk

    def place(a, _):
        o_ref[dst_ref[a]] = lax.shift_right_logical(base + a, 1)
        return 0

    lax.fori_loop(0, chunk, place, 0, unroll=8)


def _rowtok(dest, pad_start, counts, rows, tm):
    n2 = dest.shape[0]
    chunk = min(4096, n2)
    smem = pl.BlockSpec(memory_space=pltpu.SMEM)
    return pl.pallas_call(
        functools.partial(_rowtok_kernel, chunk=chunk, tm=tm),
        grid=(n2 // chunk,),
        in_specs=[pl.BlockSpec((chunk,), lambda i: (i,), memory_space=pltpu.SMEM), smem, smem],
        out_specs=smem,
        out_shape=jax.ShapeDtypeStruct((rows,), I32),
        compiler_params=_cp(("arbitrary",)),
        name="rowtok",
    )(dest, pad_start, counts)


def _experts_kernel(be_ref, ord_ref, nxt_ref, nu_ref, tok_ref, tokn_ref, h_hbm, wg_hbm, wu_hbm, wd_hbm,
                    ys_ref, xbuf, g_f, u_f, d_f, g_s, u_s, d_s, sem, gsem, *, tm):
    i = pl.program_id(0)
    used = i < nu_ref[0]
    e = be_ref[i]
    fresh = (i == 0) | (e != be_ref[jnp.maximum(i - 1, 0)])
    slot = lax.rem(ord_ref[i], 2)
    xslot = lax.rem(i, 2)

    def gather(t_ref, s):
        def issue(t, _):
            pltpu.make_async_copy(h_hbm.at[pl.ds(t_ref[t], 1)], xbuf.at[s, pl.ds(t, 1)],
                                  gsem.at[s]).start()
            return 0

        lax.fori_loop(0, tm, issue, 0, unroll=8)

    @pl.when(used & (i == 0))
    def _():
        gather(tok_ref, 0)

    @pl.when(i + 1 < nu_ref[0])
    def _():
        gather(tokn_ref, 1 - xslot)

    def fetch(expert, s):
        return [pltpu.make_async_copy(w.at[expert], f.at[s], sem.at[s, k])
                for k, (w, f) in enumerate(((wg_hbm, g_f), (wu_hbm, u_f), (wd_hbm, d_f)))]

    @pl.when(used & (i == 0))
    def _():
        for c in fetch(e, slot):
            c.start(priority=1)

    @pl.when(used & fresh)
    def _():
        for c in fetch(e, slot):
            c.wait()
        nxt = nxt_ref[i]

        @pl.when(nxt >= 0)
        def _():
            for c in fetch(nxt, 1 - slot):
                c.start(priority=1)

        g_s[...] = g_f[slot].astype(BF16)
        u_s[...] = u_f[slot].astype(BF16)
        d_s[...] = d_f[slot].astype(BF16)

    @pl.when(used)
    def _():
        pltpu.make_async_copy(h_hbm.at[pl.ds(0, tm)], xbuf.at[xslot], gsem.at[xslot]).wait()
        x = xbuf[xslot].astype(BF16)
        a = _silu(_dot(x, g_s[...])) * _dot(x, u_s[...])
        ys_ref[...] = _dot(a.astype(BF16), d_s[...])

    @pl.when(jnp.logical_not(used))
    def _():
        ys_ref[...] = jnp.zeros_like(ys_ref)


def _experts(blk_exp, blk_ord, blk_nxt, n_used, row_tok, h2, w_gate, w_up, w_down, tm):
    d = h2.shape[1]
    de = w_gate.shape[2]
    nb = blk_exp.shape[0]
    anyspec = pl.BlockSpec(memory_space=pl.ANY)
    tok_blk = lambda off: pl.BlockSpec(
        (tm,), lambda i, be, od, nx, nu: (jnp.minimum(i + off, nu[0] - 1),), memory_space=pltpu.SMEM)
    grid_spec = pltpu.PrefetchScalarGridSpec(
        num_scalar_prefetch=4,
        grid=(nb,),
        in_specs=[tok_blk(0), tok_blk(1), anyspec, anyspec, anyspec, anyspec],
        out_specs=pl.BlockSpec((tm, d), lambda i, be, od, nx, nu: (i, 0)),
        scratch_shapes=[pltpu.VMEM((2, tm, d), F32),
                        pltpu.VMEM((2, d, de), F32), pltpu.VMEM((2, d, de), F32),
                        pltpu.VMEM((2, de, d), F32),
                        pltpu.VMEM((d, de), BF16), pltpu.VMEM((d, de), BF16),
                        pltpu.VMEM((de, d), BF16),
                        pltpu.SemaphoreType.DMA((2, 3)), pltpu.SemaphoreType.DMA((2,))],
    )
    return pl.pallas_call(
        functools.partial(_experts_kernel, tm=tm),
        grid_spec=grid_spec,
        out_shape=jax.ShapeDtypeStruct((nb * tm, d), F32),
        compiler_params=_cp(("arbitrary",)),
        name="experts",
    )(blk_exp, blk_ord, blk_nxt, n_used, row_tok, row_tok, h2, w_gate, w_up, w_down)


def _combine_kernel(ids_ref, idn_ref, ys_ref, x1_ref, wt_ref, mod_ref, g_ref, o_ref,
                    buf, sem, *, tc):
    i = pl.program_id(0)
    slot = lax.rem(i, 2)

    def gather(id_ref, s):
        def issue(t, _):
            for k in range(2):
                src = id_ref[2 * t + k]
                pltpu.make_async_copy(ys_ref.at[pl.ds(src, 1)], buf.at[s, k, pl.ds(t, 1)],
                                      sem.at[s]).start()
            return 0

        lax.fori_loop(0, tc, issue, 0, unroll=8)

    @pl.when(i == 0)
    def _():
        gather(ids_ref, 0)

    @pl.when(i + 1 < pl.num_programs(0))
    def _():
        gather(idn_ref, 1 - slot)

    for k in range(2):
        pltpu.make_async_copy(ys_ref.at[pl.ds(0, tc)], buf.at[slot, k], sem.at[slot]).wait()

    moe = buf[slot, 0] * wt_ref[:, 0:1] + buf[slot, 1] * wt_ref[:, 1:2]
    x2 = x1_ref[...] + mod_ref[0, 5:6, :] * moe
    o_ref[...] = x2 * lax.rsqrt(jnp.mean(x2 * x2, axis=-1, keepdims=True) + EPS) * g_ref[...]


def _combine(dest, ys, x1, wts_t, mod3, final_g, seq, tc):
    n, d = x1.shape
    last = n // tc - 1
    return pl.pallas_call(
        functools.partial(_combine_kernel, tc=tc),
        grid=(n // tc,),
        in_specs=[pl.BlockSpec((2 * tc,), lambda i: (i,), memory_space=pltpu.SMEM),
                  pl.BlockSpec((2 * tc,), lambda i: (jnp.minimum(i + 1, last),),
                               memory_space=pltpu.SMEM),
                  pl.BlockSpec(memory_space=pl.ANY),
                  pl.BlockSpec((tc, d), lambda i: (i, 0)),
                  pl.BlockSpec((tc, LANES), lambda i: (i, 0)),
                  pl.BlockSpec((1, 6, d), lambda i: ((i * tc) // seq, 0, 0)),
                  pl.BlockSpec((1, d), lambda i: (0, 0))],
        out_specs=pl.BlockSpec((tc, d), lambda i: (i, 0)),
        out_shape=jax.ShapeDtypeStruct((n, d), F32),
        scratch_shapes=[pltpu.VMEM((2, 2, tc, d), F32), pltpu.SemaphoreType.DMA((2,))],
        compiler_params=_cp(("arbitrary",)),
        name="combine",
    )(dest, dest, ys, x1, wts_t, mod3, final_g.reshape(1, d))


def _token_mixer(x2, mod3, norm1_g, w_in, ret_gn_g, cmp_pos_k, cmp_w1_k, cmp_w2_k,
                 cmp_pos_v, cmp_w1_v, cmp_w2_v, bsz, seq):
    n, d = x2.shape
    w_t = w_in.T
    w_gate_t = jnp.pad(w_t[PROJ_MAIN:], ((0, LANES - N_GATE_COLS), (0, 0))).astype(BF16)
    proj, gate_logits = _inproj(x2, mod3, norm1_g, w_t, w_gate_t, seq)

    o_ret = _retention(proj, ret_gn_g, bsz, seq)

    g_ = NSA_KV_GROUPS
    crow = seq // CMP_STRIDE

    pe_flat = lambda pe: pe.reshape(1, CMP_BLOCK * NSA_D)
    kc, vc = _compress(proj, pe_flat(cmp_pos_k), pe_flat(cmp_pos_v),
                       cmp_w1_k, cmp_w1_v, cmp_w2_k, cmp_w2_v, bsz, seq)

    o_nsa = _nsa(proj, kc, vc, gate_logits, bsz, seq)
    return o_ret, o_nsa


def _moe(h2, lt, x1, mod3, final_g, b_grp, b_exp, w_gate, w_up, w_down, seq, tm):
    n, d = h2.shape
    bias_col = jnp.zeros((LANES,), F32).at[:N_GROUPS].set(b_grp).at[N_GROUPS:N_GROUPS + N_EXPERTS].set(b_exp)
    bias_col = jnp.broadcast_to(bias_col[:, None], (LANES, LANES))
    ids, wts, cnt = _route(lt, bias_col)

    counts = cnt[:, 0]
    padded = (counts + tm - 1) // tm * tm
    pad_end = jnp.cumsum(padded)
    pad_start = (pad_end - padded).astype(I32)
    nb = (2 * n) // tm + N_EXPERTS
    n_used = (pad_end[-1] // tm).astype(I32).reshape(1)
    blk_start = jnp.arange(nb, dtype=I32) * tm
    blk_exp = jnp.minimum(jnp.sum((pad_end[None, :] <= blk_start[:, None]).astype(I32), axis=1),
                          N_EXPERTS - 1).astype(I32)
    last_exp = blk_exp[jnp.maximum(n_used[0] - 1, 0)]
    blk_exp = jnp.where(jnp.arange(nb) < n_used[0], blk_exp, last_exp)
    eid = jnp.arange(N_EXPERTS, dtype=I32)
    has = counts > 0
    exp_ord = jnp.sum((has[None, :] & (eid[None, :] < eid[:, None])).astype(I32), axis=1)
    exp_nxt = jnp.min(jnp.where(has[None, :] & (eid[None, :] > eid[:, None]), eid[None, :], N_EXPERTS), axis=1)
    exp_nxt = jnp.where(exp_nxt < N_EXPERTS, exp_nxt, -1).astype(I32)
    blk_ord = exp_ord[blk_exp]
    blk_nxt = exp_nxt[blk_exp]

    dest = _dest(ids, pad_start)
    row_tok = _rowtok(dest, pad_start, counts.astype(I32), nb * tm, tm)
    ys = _experts(blk_exp, blk_ord, blk_nxt, n_used, row_tok, h2, w_gate, w_up, w_down, tm)
    wts_t = jnp.pad(wts[:2].T, ((0, 0), (0, LANES - 2)))
    return _combine(dest, ys, x1, wts_t, mod3, final_g, seq, min(256, n))


def kernel(x, c, w_ada, b_ada, norm1_g, norm2_g, final_g, w_in, ret_gn_g, cmp_pos_k, cmp_w1_k,
           cmp_w2_k, cmp_pos_v, cmp_w1_v, cmp_w2_v, w_out, w_grp, b_grp, w_exp, b_exp, w_gate,
           w_up, w_down):
    bsz, seq, d = x.shape
    assert w_ada.shape[0] == 1, "single-layer block"
    n = bsz * seq
    x2 = x.reshape(n, d)
    mod3 = _ada(c, w_ada[0], b_ada[0]).reshape(bsz, 6, d)

    o_ret, o_nsa = _token_mixer(x2, mod3, norm1_g[0], w_in[0], ret_gn_g[0], cmp_pos_k[0],
                                cmp_w1_k[0], cmp_w2_k[0], cmp_pos_v[0], cmp_w1_v[0], cmp_w2_v[0],
                                bsz, seq)

    w_route = jnp.concatenate([w_grp[0], w_exp[0]], axis=1)
    w_route = jnp.pad(w_route, ((0, 0), (0, LANES - w_route.shape[1]))).T.astype(BF16)
    x1, h2, lt = _outproj(o_ret, o_nsa, x2, mod3, norm2_g[0], w_out[0].astype(BF16), w_route, seq)

    out = _moe(h2, lt, x1, mod3, final_g, b_grp[0], b_exp[0], w_gate[0], w_up[0], w_down[0],
               seq, 256)
    return out.reshape(bsz, seq, d)
```

```python
import functools
import math

import numpy as np
import jax
import jax.numpy as jnp
from jax import lax
from jax.experimental import pallas as pl
from jax.experimental.pallas import tpu as pltpu

F32 = jnp.float32
BF16 = jnp.bfloat16
I32 = jnp.int32

RET_HEADS = 4
RET_DK = 256
RET_DV = 256
RET_CHUNK = 128
NSA_HEADS = 8
NSA_KV_GROUPS = 2
NSA_HPG = NSA_HEADS // NSA_KV_GROUPS
NSA_D = 128
CMP_BLOCK = 32
CMP_STRIDE = 16
CMP_HIDDEN = 256
SEL_BLOCK = 64
SEL_COUNT = 16
WIN = 512
N_GROUPS = 8
EXP_PER_GROUP = 8
N_EXPERTS = N_GROUPS * EXP_PER_GROUP
D_EXPERT = 512
ROPE_BASE = 10000.0
EPS = 1e-6
NEG = -1e30
FORCE_BONUS = 1e4

LANES = 128
SUBLANES = 8
VMEM_LIMIT = 52 * 1024 * 1024

_C_RQ, _C_RK, _C_RV, _C_RG = 0, 8, 16, 24
_C_NQ = 32
_C_KC, _C_VC, _C_KS, _C_VS, _C_KW, _C_VW = 40, 42, 44, 46, 48, 50
_C_GATE = 52
PROJ_MAIN = _C_GATE * LANES
N_GATE_COLS = NSA_HEADS * 3


def _cp(sem, vmem=VMEM_LIMIT):
    return pltpu.CompilerParams(dimension_semantics=sem, vmem_limit_bytes=vmem)


def _silu(v):
    return v * jax.nn.sigmoid(v)


def _dot(a, b):
    return jnp.dot(a, b, preferred_element_type=F32)


def _dot_nt(a, b):
    return lax.dot_general(a, b, (((1,), (1,)), ((), ())), preferred_element_type=F32)


def _dot_tn(a, b):
    return lax.dot_general(a, b, (((0,), (0,)), ((), ())), preferred_element_type=F32)


def _ada_kernel(c_ref, w_ref, b_ref, o_ref):
    ca = _silu(c_ref[...]).astype(BF16)
    o_ref[...] = _dot(ca, w_ref[...].astype(BF16)) + b_ref[...]


def _ada(c, w, b):
    bsz, d = c.shape
    n = w.shape[1]
    tn = min(1024, n)
    return pl.pallas_call(
        _ada_kernel,
        grid=(n // tn,),
        in_specs=[pl.BlockSpec((bsz, d), lambda j: (0, 0)),
                  pl.BlockSpec((d, tn), lambda j: (0, j)),
                  pl.BlockSpec((1, tn), lambda j: (0, j))],
        out_specs=pl.BlockSpec((bsz, tn), lambda j: (0, j)),
        out_shape=jax.ShapeDtypeStruct((bsz, n), F32),
        compiler_params=_cp(("arbitrary",)),
        name="ada",
    )(c, w, b.reshape(1, n))


def _inproj_kernel(x_ref, mod_ref, g_ref, wt_ref, wgt_ref, proj_ref, gate_ref, h_ref):
    @pl.when(pl.program_id(1) == 0)
    def _():
        x = x_ref[...]
        y = x * lax.rsqrt(jnp.mean(x * x, axis=-1, keepdims=True) + EPS) * g_ref[...]
        h = (y * (1.0 + mod_ref[0, 1:2, :]) + mod_ref[0, 0:1, :]).astype(BF16)
        h_ref[...] = h
        gate_ref[...] = _dot_nt(h, wgt_ref[...])

    proj_ref[...] = _dot_nt(h_ref[...], wt_ref[...].astype(BF16)).astype(BF16)


def _inproj(x2, mod3, g, w_t, wg_t, seq):
    n, d = x2.shape
    tm = min(1024, seq)
    tn = 512
    nj = PROJ_MAIN // tn
    return pl.pallas_call(
        _inproj_kernel,
        grid=(n // tm, nj),
        in_specs=[pl.BlockSpec((tm, d), lambda i, j: (i, 0)),
                  pl.BlockSpec((1, 6, d), lambda i, j: ((i * tm) // seq, 0, 0)),
                  pl.BlockSpec((1, d), lambda i, j: (0, 0)),
                  pl.BlockSpec((tn, d), lambda i, j: (j, 0)),
                  pl.BlockSpec((LANES, d), lambda i, j: (0, 0))],
        out_specs=[pl.BlockSpec((tm, tn), lambda i, j: (i, j)),
                   pl.BlockSpec((tm, LANES), lambda i, j: (i, 0))],
        out_shape=[jax.ShapeDtypeStruct((n, PROJ_MAIN), BF16),
                   jax.ShapeDtypeStruct((n, LANES), F32)],
        scratch_shapes=[pltpu.VMEM((tm, d), BF16)],
        compiler_params=_cp(("arbitrary", "arbitrary")),
        name="inproj",
    )(x2, mod3, g.reshape(1, d), w_t, wg_t)


def _retention_kernel(q_ref, k_ref, v_ref, g_ref, cos_ref, sin_ref, din_ref, zeta_ref,
                      qdec_ref, cdec_ref, gn_ref, o_ref, s_ref):
    @pl.when(pl.program_id(1) == 0)
    def _():
        s_ref[...] = jnp.zeros_like(s_ref)

    half = RET_DK // 2
    c = RET_CHUNK

    for sub in range(q_ref.shape[0] // c):
        rows = slice(sub * c, (sub + 1) * c)
        cos = cos_ref[rows, :]
        sin = sin_ref[rows, :]

        def rot(a):
            a1, a2 = a[:, :half], a[:, half:]
            return jnp.concatenate([a1 * cos - a2 * sin, a1 * sin + a2 * cos], axis=1)

        for h in range(RET_HEADS):
            qs = slice(h * RET_DK, (h + 1) * RET_DK)
            vs = slice(h * RET_DV, (h + 1) * RET_DV)
            q = rot(q_ref[rows, qs].astype(F32))
            k = rot(k_ref[rows, qs].astype(F32)) * (RET_DK ** -0.5)
            v = v_ref[rows, vs]
            qb = q.astype(BF16)
            kb = k.astype(BF16)
            s = _dot_nt(qb, kb) * din_ref[h]
            inner = _dot(s.astype(BF16), v)
            s_prev = s_ref[h]
            cross = _dot(qb, s_prev.astype(BF16)) * qdec_ref[h]
            kv = _dot_tn((k * zeta_ref[h]).astype(BF16), v)
            s_ref[h] = cdec_ref[h] * s_prev + kv
            o = inner + cross
            mu = jnp.mean(o, axis=-1, keepdims=True)
            oc = o - mu
            var = jnp.mean(oc * oc, axis=-1, keepdims=True)
            o = oc * lax.rsqrt(var + EPS) * gn_ref[:, vs]
            o_ref[rows, vs] = (o * _silu(g_ref[rows, vs].astype(F32))).astype(BF16)


def _retention(proj, gn_g, bsz, seq):
    n = proj.shape[0]
    c = RET_CHUNK
    nc = seq // c
    hw = RET_HEADS * RET_DK
    half = RET_DK // 2
    pos = jnp.arange(seq, dtype=F32)
    inv = ROPE_BASE ** (-jnp.arange(half, dtype=F32) / half)
    ang = pos[:, None] * inv[None, :]
    cos, sin = jnp.cos(ang), jnp.sin(ang)
    log_gamma = jnp.log1p(-jnp.exp2(-5.0 - jnp.arange(RET_HEADS, dtype=F32)))
    idx = jnp.arange(c, dtype=F32)
    rel = idx[:, None] - idx[None, :]
    decay_in = jnp.where(rel >= 0, jnp.exp(log_gamma[:, None, None] * jnp.maximum(rel, 0.0)), 0.0)
    zeta = jnp.exp(log_gamma[:, None] * (c - 1 - idx)[None, :])[:, :, None]
    q_decay = jnp.exp(log_gamma[:, None] * (idx + 1)[None, :])[:, :, None]
    chunk_decay = jnp.exp(log_gamma * c)[:, None, None]
    rb = min(4 * c, seq)
    ns = seq // rb
    row = lambda b, t: (b * ns + t)
    return pl.pallas_call(
        _retention_kernel,
        grid=(bsz, ns),
        in_specs=[pl.BlockSpec((rb, hw), lambda b, t: (row(b, t), 0)),
                  pl.BlockSpec((rb, hw), lambda b, t: (row(b, t), 1)),
                  pl.BlockSpec((rb, hw), lambda b, t: (row(b, t), 2)),
                  pl.BlockSpec((rb, hw), lambda b, t: (row(b, t), 3)),
                  pl.BlockSpec((rb, half), lambda b, t: (t, 0)),
                  pl.BlockSpec((rb, half), lambda b, t: (t, 0)),
                  pl.BlockSpec((RET_HEADS, c, c), lambda b, t: (0, 0, 0)),
                  pl.BlockSpec((RET_HEADS, c, 1), lambda b, t: (0, 0, 0)),
                  pl.BlockSpec((RET_HEADS, c, 1), lambda b, t: (0, 0, 0)),
                  pl.BlockSpec((RET_HEADS, 1, 1), lambda b, t: (0, 0, 0)),
                  pl.BlockSpec((1, hw), lambda b, t: (0, 0))],
        out_specs=pl.BlockSpec((rb, hw), lambda b, t: (row(b, t), 0)),
        out_shape=jax.ShapeDtypeStruct((n, hw), BF16),
        scratch_shapes=[pltpu.VMEM((RET_HEADS, RET_DK, RET_DV), F32)],
        compiler_params=_cp(("arbitrary", "arbitrary")),
        name="retention",
    )(proj, proj, proj, proj, cos, sin, decay_in, zeta, q_decay, chunk_decay,
      gn_g.reshape(1, hw))


def _compress_kernel(ak_ref, av_ref, pek_ref, pev_ref, w1k_ref, w1v_ref, w2k_ref, w2v_ref,
                     kc_ref, vc_ref, a_scr):
    seq = ak_ref.shape[0]
    nblk = seq // CMP_STRIDE
    a_scr[seq:, :] = jnp.zeros((a_scr.shape[0] - seq, NSA_D), F32)

    def one(a_ref, pe_ref, w1_ref, w2_ref, o_ref):
        a_scr[:seq, :] = a_ref[...].astype(F32)
        flat = jnp.concatenate([a_scr[pl.ds(l, nblk, stride=CMP_STRIDE), :] for l in range(CMP_BLOCK)],
                               axis=1)
        pre = _dot((flat + pe_ref[...]).astype(BF16), w1_ref[...].astype(BF16))
        o_ref[...] = _dot(_silu(pre).astype(BF16), w2_ref[...].astype(BF16)).astype(BF16)

    one(ak_ref, pek_ref, w1k_ref, w2k_ref, kc_ref)
    one(av_ref, pev_ref, w1v_ref, w2v_ref, vc_ref)


def _compress(proj, pek, pev, w1k, w1v, w2k, w2v, bsz, seq):
    g_ = NSA_KV_GROUPS
    nblk = seq // CMP_STRIDE
    full = lambda a: pl.BlockSpec(a.shape, lambda i: (0,) * a.ndim)
    kblk = pl.BlockSpec((seq, NSA_D), lambda i: (i // g_, _C_KC + i % g_))
    vblk = pl.BlockSpec((seq, NSA_D), lambda i: (i // g_, _C_VC + i % g_))
    oblk = pl.BlockSpec((nblk, NSA_D), lambda i: (i, 0))
    return pl.pallas_call(
        _compress_kernel,
        grid=(bsz * g_,),
        in_specs=[kblk, vblk, full(pek), full(pev), full(w1k), full(w1v), full(w2k), full(w2v)],
        out_specs=[oblk, oblk],
        out_shape=[jax.ShapeDtypeStruct((bsz * g_ * nblk, NSA_D), BF16)] * 2,
        scratch_shapes=[pltpu.VMEM((seq + CMP_BLOCK, NSA_D), F32)],
        compiler_params=_cp(("arbitrary",)),
        name="compress",
    )(proj, proj, pek, pev, w1k, w1v, w2k, w2v)


def _nsa_kernel(q_ref, kc_ref, vc_ref, ks_ref, vs_ref, kw_ref, vw_ref, gate_ref, ovt_ref,
                o_ref, kaug_ref, vst_ref, vwt_ref, *, tq, tk, vt, seq, ncmp, wlen):
    crow = kc_ref.shape[0]
    i = pl.program_id(2)
    hg = NSA_HPG
    d = NSA_D
    r = hg * tq

    @pl.when(i == 0)
    def _():
        kaug_ref[:, :d] = ks_ref[...]
        blk = lax.broadcasted_iota(I32, (seq, LANES), 0) // SEL_BLOCK
        lane = lax.broadcasted_iota(I32, (seq, LANES), 1)
        kaug_ref[:, d:] = (blk == lane).astype(BF16)
        for c in range(seq // vt):
            vst_ref[c] = vs_ref[c * vt:(c + 1) * vt, :].astype(F32).T.astype(BF16)
            vwt_ref[c] = vw_ref[c * vt:(c + 1) * vt, :].astype(F32).T.astype(BF16)

    def pv_t(vt_ref, first_tile, p):
        out = None
        for c in range(p.shape[0] // vt):
            part = _dot(vt_ref[first_tile + c], p[c * vt:(c + 1) * vt, :])
            out = part if out is None else out + part
        return out

    q = q_ref[...]
    qh = [(q[:, h * d:(h + 1) * d].astype(F32) * (d ** -0.5)).astype(BF16) for h in range(hg)]
    qa = jnp.concatenate(qh, axis=0)
    t1 = i * tq + lax.broadcasted_iota(I32, (tq, 1), 0)
    tpos = jnp.concatenate([t1] * hg, axis=0)
    t_row = i * tq + lax.broadcasted_iota(I32, (1, tq), 1)
    tpos_row = jnp.concatenate([t_row] * hg, axis=1)

    w0 = pl.multiple_of(jnp.maximum(i * tq + tq - wlen, 0), vt)
    sw = _dot_nt(kw_ref[pl.ds(w0, wlen), :], qa)
    delta = tpos_row - (w0 + lax.broadcasted_iota(I32, (wlen, 1), 0))
    sw = jnp.where((delta >= 0) & (delta < WIN), sw, NEG)
    e_w = jnp.exp(sw - jnp.max(sw, axis=0, keepdims=True))
    o_win = (pv_t(vwt_ref, w0 // vt, e_w.astype(BF16)) / jnp.sum(e_w, axis=0, keepdims=True)).T

    sct = _dot_nt(kc_ref[...], qa)
    cidx = lax.broadcasted_iota(I32, (crow, 1), 0)
    cmask = (cidx * CMP_STRIDE + (CMP_BLOCK - 1) <= tpos_row) & (cidx < ncmp)
    sct = jnp.where(cmask, sct, NEG)
    e_c = jnp.where(cmask, jnp.exp(sct - jnp.max(sct, axis=0, keepdims=True)), 0.0)
    den_c = jnp.sum(e_c, axis=0, keepdims=True)
    p_t = jnp.where(den_c > 0.0, e_c / jnp.where(den_c > 0.0, den_c, 1.0), 0.0)
    o_cmp = _dot_tn(p_t.astype(BF16), vc_ref[...])

    psum_t = p_t[:, 0:tq]
    for h in range(1, hg):
        psum_t = psum_t + p_t[:, h * tq:(h + 1) * tq]
    nsel_blocks = seq // SEL_BLOCK
    nsb = ovt_ref.shape[0]
    imp_t = jnp.dot(ovt_ref[...], psum_t, preferred_element_type=F32,
                    precision=lax.Precision.HIGHEST)
    sidx = lax.broadcasted_iota(I32, (nsb, 1), 0)
    cur = t_row // SEL_BLOCK
    valid = sidx <= cur
    forced = (sidx == 0) | (sidx == cur) | (sidx == cur - 1)
    score = jnp.where(valid, imp_t + jnp.where(forced, FORCE_BONUS, 0.0), -1.0)
    rank = jnp.zeros((nsb, tq), F32)
    for s2 in range(nsel_blocks):
        row = score[s2:s2 + 1, :]
        beats = (row > score) | ((row == score) & (sidx > s2))
        rank = rank + beats.astype(F32)
    sel = valid & (rank < float(min(SEL_COUNT, nsel_blocks)))
    selb_t = jnp.where(sel, 0.0, NEG)
    selb_t = jnp.concatenate([selb_t, jnp.zeros((LANES - nsb, tq), F32)], axis=0)
    selb = selb_t.T.astype(BF16)
    q_aug = jnp.concatenate([jnp.concatenate([qh[h], selb], axis=1) for h in range(hg)], axis=0)

    n_full = (i * tq + 1) // tk
    kl = lax.broadcasted_iota(I32, (tk, 1), 0)

    def tile(j, carry, masked):
        m, l, acc = carry
        k0 = pl.multiple_of(j * tk, tk)
        s = _dot_nt(kaug_ref[pl.ds(k0, tk), :], q_aug)
        if masked:
            s = jnp.where(k0 + kl <= tpos_row, s, NEG)
        m_new = jnp.maximum(m, jnp.max(s, axis=0, keepdims=True))
        alpha = jnp.exp(m - m_new)
        p = jnp.exp(s - m_new)
        l = alpha * l + jnp.sum(p, axis=0, keepdims=True)
        acc = alpha * acc + pv_t(vst_ref, j * (tk // vt), p.astype(BF16))
        return m_new, l, acc

    m0 = jnp.full((1, r), NEG, F32)
    l0 = jnp.zeros((1, r), F32)
    a0 = jnp.zeros((d, r), F32)
    carry = lax.fori_loop(0, n_full, functools.partial(tile, masked=False), (m0, l0, a0))
    _, l_s, acc_s = tile(n_full, carry, True)
    o_slc = (acc_s / l_s).T

    gt = jax.nn.sigmoid(gate_ref[...])
    grp = pl.program_id(1)
    for g in range(1, NSA_KV_GROUPS):
        gt = jnp.where(grp == g, pltpu.roll(gt, LANES - g * 3 * hg, 1), gt)
    for h in range(hg):
        rows = slice(h * tq, (h + 1) * tq)
        o = (gt[:, 3 * h:3 * h + 1] * o_cmp[rows] + gt[:, 3 * h + 1:3 * h + 2] * o_slc[rows]
             + gt[:, 3 * h + 2:3 * h + 3] * o_win[rows])
        o_ref[:, h * d:(h + 1) * d] = o.astype(BF16)


def _nsa(proj, kc, vc, gates_g, bsz, seq):
    n = proj.shape[0]
    g_ = NSA_KV_GROUPS
    tq = 256
    tk = min(512, seq)
    nq = seq // tq
    ncmp = (seq - CMP_BLOCK) // CMP_STRIDE + 1
    wlen = min(WIN + tq, seq)
    nsel = seq // SEL_BLOCK
    crow = seq // CMP_STRIDE
    assert nsel <= LANES and crow <= LANES and tq <= tk and tk % tq == 0
    nsb = -(-nsel // 8) * 8
    ss = np.arange(nsb)[:, None] * SEL_BLOCK
    cs = np.arange(crow)[None, :] * CMP_STRIDE
    ov = ((cs < ss + SEL_BLOCK) & (cs + CMP_BLOCK > ss)
          & (np.arange(crow)[None, :] < ncmp) & (np.arange(nsb)[:, None] < nsel))
    ov = jnp.asarray(ov.astype(np.float32))
    kvspec = lambda c0: pl.BlockSpec((seq, NSA_D), lambda b, g, i: (b, c0 + g))
    vt = 256
    assert tk % vt == 0 and wlen % vt == 0 and tq % vt == 0 and seq % vt == 0
    kern = functools.partial(_nsa_kernel, tq=tq, tk=tk, vt=vt, seq=seq, ncmp=ncmp, wlen=wlen)
    return pl.pallas_call(
        kern,
        grid=(bsz, g_, nq),
        in_specs=[pl.BlockSpec((tq, NSA_HPG * NSA_D), lambda b, g, i: (b * nq + i, _C_NQ // NSA_HPG + g)),
                  pl.BlockSpec((crow, NSA_D), lambda b, g, i: (b * g_ + g, 0)),
                  pl.BlockSpec((crow, NSA_D), lambda b, g, i: (b * g_ + g, 0)),
                  kvspec(_C_KS), kvspec(_C_VS), kvspec(_C_KW), kvspec(_C_VW),
                  pl.BlockSpec((tq, LANES), lambda b, g, i: (b * nq + i, 0)),
                  pl.BlockSpec((nsb, crow), lambda b, g, i: (0, 0))],
        out_specs=pl.BlockSpec((tq, NSA_HPG * NSA_D), lambda b, g, i: (b * nq + i, g)),
        out_shape=jax.ShapeDtypeStruct((n, NSA_HEADS * NSA_D), BF16),
        scratch_shapes=[pltpu.VMEM((seq, 2 * NSA_D), BF16),
                        pltpu.VMEM((seq // vt, NSA_D, vt), BF16),
                        pltpu.VMEM((seq // vt, NSA_D, vt), BF16)],
        compiler_params=_cp(("arbitrary", "arbitrary", "arbitrary")),
        name="nsa",
    )(proj, kc, vc, proj, proj, proj, proj, gates_g, ov)


def _outproj_kernel(oret_ref, onsa_ref, x_ref, mod_ref, g_ref, w_ref, wr_ref,
                    x1_ref, h2_ref, lt_ref):
    hw = oret_ref.shape[1]
    mix = _dot(oret_ref[...], w_ref[:hw, :]) + _dot(onsa_ref[...], w_ref[hw:, :])
    x1 = x_ref[...] + mod_ref[0, 2:3, :] * mix
    x1_ref[...] = x1
    y = x1 * lax.rsqrt(jnp.mean(x1 * x1, axis=-1, keepdims=True) + EPS) * g_ref[...]
    h2 = y * (1.0 + mod_ref[0, 4:5, :]) + mod_ref[0, 3:4, :]
    h2_ref[...] = h2
    lt_ref[...] = _dot_nt(wr_ref[...], h2.astype(BF16))


def _outproj(o_ret, o_nsa, x2, mod3, g, w_bf, wr_bf, seq):
    n, d = x2.shape
    tm = min(512, seq)
    hw = o_ret.shape[1]
    return pl.pallas_call(
        _outproj_kernel,
        grid=(n // tm,),
        in_specs=[pl.BlockSpec((tm, hw), lambda i: (i, 0)),
                  pl.BlockSpec((tm, o_nsa.shape[1]), lambda i: (i, 0)),
                  pl.BlockSpec((tm, d), lambda i: (i, 0)),
                  pl.BlockSpec((1, 6, d), lambda i: ((i * tm) // seq, 0, 0)),
                  pl.BlockSpec((1, d), lambda i: (0, 0)),
                  pl.BlockSpec(w_bf.shape, lambda i: (0, 0)),
                  pl.BlockSpec(wr_bf.shape, lambda i: (0, 0))],
        out_specs=[pl.BlockSpec((tm, d), lambda i: (i, 0)),
                   pl.BlockSpec((tm, d), lambda i: (i, 0)),
                   pl.BlockSpec((LANES, tm), lambda i: (0, i))],
        out_shape=[jax.ShapeDtypeStruct((n, d), F32),
                   jax.ShapeDtypeStruct((n, d), F32),
                   jax.ShapeDtypeStruct((LANES, n), F32)],
        compiler_params=_cp(("arbitrary",)),
        name="outproj",
    )(o_ret, o_nsa, x2, mod3, g.reshape(1, d), w_bf, wr_bf)


def _route_kernel(lt_ref, b_ref, tri_ref, ids_ref, wts_ref, cnt_ref, carry_ref, *, sub):
    @pl.when(pl.program_id(0) == 0)
    def _():
        carry_ref[...] = jnp.zeros_like(carry_ref)

    ng, ne = N_GROUPS, EXP_PER_GROUP
    l = lt_ref[...] + b_ref[:, 0:1]
    tc = l.shape[1]
    ridx = lax.broadcasted_iota(I32, (ng, tc), 0).astype(F32)

    def softmax0(v):
        e = jnp.exp(v - jnp.max(v, axis=0, keepdims=True))
        return e / jnp.sum(e, axis=0, keepdims=True)

    def top1(p):
        top = jnp.max(p, axis=0, keepdims=True)
        idx = jnp.min(jnp.where(p == top, ridx, float(ng)), axis=0, keepdims=True)
        return top, idx

    pg_top, grp = top1(softmax0(l[0:ng]))
    leg = jnp.zeros((ne, tc), F32)
    for g in range(ng):
        leg = jnp.where(grp == float(g), l[ng + g * ne:ng + (g + 1) * ne], leg)
    pe = softmax0(leg)
    p1, i1 = top1(pe)
    p2, i2 = top1(jnp.where(ridx == i1, -1.0, pe))
    den = p1 + p2
    w1 = pg_top * p1 / den
    w2 = pg_top * p2 / den
    e1 = grp * float(ne) + i1
    e2 = grp * float(ne) + i2

    eio = lax.broadcasted_iota(I32, (N_EXPERTS, sub), 0).astype(F32)
    r1 = []
    r2 = []
    carry = carry_ref[:, 0:1]
    for c in range(tc // sub):
        cs = slice(c * sub, (c + 1) * sub)
        oh1 = (eio == e1[:, cs]).astype(F32)
        oh2 = (eio == e2[:, cs]).astype(F32)
        oh = oh1 + oh2
        before = carry + _dot(oh.astype(BF16), tri_ref[...])
        r1.append(jnp.sum(oh1 * before, axis=0, keepdims=True))
        r2.append(jnp.sum(oh2 * before, axis=0, keepdims=True))
        carry = carry + jnp.sum(oh, axis=1, keepdims=True)
    carry_ref[...] = jnp.broadcast_to(carry, carry_ref.shape)
    cnt_ref[...] = jnp.broadcast_to(carry, cnt_ref.shape).astype(I32)
    r1 = jnp.concatenate(r1, axis=1)
    r2 = jnp.concatenate(r2, axis=1)
    zf = jnp.zeros((4, tc), F32)
    ids_ref[...] = jnp.concatenate([e1, e2, r1, r2, zf], axis=0).astype(I32)
    wts_ref[...] = jnp.concatenate([w1, w2, jnp.zeros((6, tc), F32)], axis=0)


def _route(lt, bias_col):
    n = lt.shape[1]
    tc = min(2048, n)
    sub = min(512, tc)
    tri = jnp.asarray(np.triu(np.ones((sub, sub), np.float32), 1), BF16)
    return pl.pallas_call(
        functools.partial(_route_kernel, sub=sub),
        grid=(n // tc,),
        in_specs=[pl.BlockSpec((LANES, tc), lambda i: (0, i)),
                  pl.BlockSpec((LANES, LANES), lambda i: (0, 0)),
                  pl.BlockSpec((sub, sub), lambda i: (0, 0))],
        out_specs=[pl.BlockSpec((8, tc), lambda i: (0, i)),
                   pl.BlockSpec((8, tc), lambda i: (0, i)),
                   pl.BlockSpec((N_EXPERTS, LANES), lambda i: (0, 0))],
        out_shape=[jax.ShapeDtypeStruct((8, n), I32),
                   jax.ShapeDtypeStruct((8, n), F32),
                   jax.ShapeDtypeStruct((N_EXPERTS, LANES), I32)],
        scratch_shapes=[pltpu.VMEM((N_EXPERTS, LANES), F32)],
        compiler_params=_cp(("arbitrary",)),
        name="route",
    )(lt, bias_col, tri)


def _dest_kernel(ids_ref, ps_ref, o_ref):
    ids = ids_ref[...].astype(F32)
    tc = ids.shape[1]
    eio = lax.broadcasted_iota(I32, (N_EXPERTS, tc), 0).astype(F32)
    ps = ps_ref[:, 0:1]
    rows = [jnp.sum(jnp.where(eio == ids[k:k + 1], ps, 0.0), axis=0, keepdims=True) + ids[2 + k:3 + k]
            for k in range(2)]
    o_ref[...] = jnp.concatenate(rows + [jnp.zeros((6, tc), F32)], axis=0).astype(I32)


def _dest(ids, pad_start):
    n = ids.shape[1]
    tc = min(2048, n)
    ps = jnp.broadcast_to(pad_start.astype(F32)[:, None], (N_EXPERTS, LANES))
    dest = pl.pallas_call(
        _dest_kernel,
        grid=(n // tc,),
        in_specs=[pl.BlockSpec((8, tc), lambda i: (0, i)),
                  pl.BlockSpec((N_EXPERTS, LANES), lambda i: (0, 0))],
        out_specs=pl.BlockSpec((8, tc), lambda i: (0, i)),
        out_shape=jax.ShapeDtypeStruct((8, n), I32),
        compiler_params=_cp(("arbitrary",)),
        name="dest",
    )(ids, ps)
    return dest[:2].T.reshape(2 * n)


def _dispatch_kernel(dst_ref, ps_ref, cnt_ref, h_ref, xs_ref, zbuf, sem, zsem, *, tcd, tm, zr):
    @pl.when(pl.program_id(0) == 0)
    def _():
        zbuf[...] = jnp.zeros_like(zbuf)
        sizes = [zr >> b for b in range(zr.bit_length()) if (zr >> b) >= SUBLANES]

        def zero_rows(row, size):
            return pltpu.make_async_copy(zbuf.at[pl.ds(0, size)],
                                         xs_ref.at[pl.ds(pl.multiple_of(row, SUBLANES), size)], zsem)

        def zero_row(row):
            return pltpu.make_async_copy(zbuf.at[pl.ds(0, 1)], xs_ref.at[pl.ds(row, 1)], zsem)

        def fill(e, wait):
            cnt = cnt_ref[e]
            cnt8 = lax.div(cnt + (SUBLANES - 1), SUBLANES) * SUBLANES
            gap = lax.rem(tm - lax.rem(cnt8, tm), tm)
            base = ps_ref[e]

            def single(c, _):
                cp = zero_row(base + cnt + c)
                cp.wait() if wait else cp.start()
                return 0

            lax.fori_loop(0, cnt8 - cnt, single, 0)
            row = base + cnt8
            for size in sizes:
                has = lax.rem(lax.div(gap, size), 2) == 1

                @pl.when(has)
                def _():
                    cp = zero_rows(row, size)
                    cp.wait() if wait else cp.start()

                row = row + jnp.where(has, size, 0)

        last = N_EXPERTS - 1
        used_rows = ps_ref[last] + lax.div(cnt_ref[last] + (tm - 1), tm) * tm
        n_tail = lax.div(xs_ref.shape[0] - used_rows, zr)

        def tail(c, wait):
            cp = zero_rows(used_rows + c * zr, zr)
            cp.wait() if wait else cp.start()

        for wait in (False, True):
            lax.fori_loop(0, N_EXPERTS, lambda e, _: (fill(e, wait), 0)[1], 0)
            lax.fori_loop(0, n_tail, lambda c, _: (tail(c, wait), 0)[1], 0)

    def issue(t, _):
        for k in range(2):
            dst = dst_ref[2 * t + k]
            pltpu.make_async_copy(h_ref.at[pl.ds(t, 1)], xs_ref.at[pl.ds(dst, 1)], sem).start()
        return 0

    lax.fori_loop(0, tcd, issue, 0, unroll=8)
    for k in range(2):
        pltpu.make_async_copy(h_ref, xs_ref.at[pl.ds(0, tcd)], sem).wait()


def _dispatch(dest, pad_start, counts, h2, rows, tcd, tm):
    n, d = h2.shape
    assert tm & (tm - 1) == 0 and tm >= 2 * SUBLANES
    zr = tm // 2
    return pl.pallas_call(
        functools.partial(_dispatch_kernel, tcd=tcd, tm=tm, zr=zr),
        grid=(n // tcd,),
        in_specs=[pl.BlockSpec((2 * tcd,), lambda i: (i,), memory_space=pltpu.SMEM),
                  pl.BlockSpec(memory_space=pltpu.SMEM),
                  pl.BlockSpec(memory_space=pltpu.SMEM),
                  pl.BlockSpec((tcd, d), lambda i: (i, 0))],
        out_specs=pl.BlockSpec(memory_space=pl.ANY),
        out_shape=jax.ShapeDtypeStruct((rows, d), h2.dtype),
        scratch_shapes=[pltpu.VMEM((zr, d), h2.dtype), pltpu.SemaphoreType.DMA(()),
                        pltpu.SemaphoreType.DMA(())],
        compiler_params=_cp(("arbitrary",)),
        name="dispatch",
    )(dest, pad_start, counts, h2)


def _experts_kernel(be_ref, ord_ref, nxt_ref, nu_ref, xs_ref, wg_hbm, wu_hbm, wd_hbm, ys_ref,
                    g_f, u_f, d_f, g_s, u_s, d_s, sem):
    i = pl.program_id(0)
    used = i < nu_ref[0]
    e = be_ref[i]
    fresh = (i == 0) | (e != be_ref[jnp.maximum(i - 1, 0)])
    slot = lax.rem(ord_ref[i], 2)

    def fetch(expert, s):
        return [pltpu.make_async_copy(w.at[expert], f.at[s], sem.at[s, k])
                for k, (w, f) in enumerate(((wg_hbm, g_f), (wu_hbm, u_f), (wd_hbm, d_f)))]

    @pl.when(used & (i == 0))
    def _():
        for c in fetch(e, slot):
            c.start()

    @pl.when(used & fresh)
    def _():
        for c in fetch(e, slot):
            c.wait()
        nxt = nxt_ref[i]

        @pl.when(nxt >= 0)
        def _():
            for c in fetch(nxt, 1 - slot):
                c.start()

        g_s[...] = g_f[slot].astype(BF16)
        u_s[...] = u_f[slot].astype(BF16)
        d_s[...] = d_f[slot].astype(BF16)

    @pl.when(used)
    def _():
        x = xs_ref[...].astype(BF16)
        a = _silu(_dot(x, g_s[...])) * _dot(x, u_s[...])
        ys_ref[...] = _dot(a.astype(BF16), d_s[...])

    @pl.when(jnp.logical_not(used))
    def _():
        ys_ref[...] = jnp.zeros_like(ys_ref)


def _experts(blk_exp, blk_ord, blk_nxt, n_used, xs, w_gate, w_up, w_down, tm):
    d = xs.shape[1]
    de = w_gate.shape[2]
    nb = blk_exp.shape[0]
    row = lambda i, be, od, nx, nu: (jnp.minimum(i, nu[0] - 1), 0)
    anyspec = pl.BlockSpec(memory_space=pl.ANY)
    grid_spec = pltpu.PrefetchScalarGridSpec(
        num_scalar_prefetch=4,
        grid=(nb,),
        in_specs=[pl.BlockSpec((tm, d), row), anyspec, anyspec, anyspec],
        out_specs=pl.BlockSpec((tm, d), lambda i, be, od, nx, nu: (i, 0)),
        scratch_shapes=[pltpu.VMEM((2, d, de), F32), pltpu.VMEM((2, d, de), F32),
                        pltpu.VMEM((2, de, d), F32),
                        pltpu.VMEM((d, de), BF16), pltpu.VMEM((d, de), BF16),
                        pltpu.VMEM((de, d), BF16),
                        pltpu.SemaphoreType.DMA((2, 3))],
    )
    return pl.pallas_call(
        _experts_kernel,
        grid_spec=grid_spec,
        out_shape=jax.ShapeDtypeStruct((nb * tm, d), F32),
        compiler_params=_cp(("arbitrary",)),
        name="experts",
    )(blk_exp, blk_ord, blk_nxt, n_used, xs, w_gate, w_up, w_down)


def _combine_kernel(ids_ref, idn_ref, ys_ref, x1_ref, wt_ref, mod_ref, g_ref, o_ref,
                    buf, sem, *, tc):
    i = pl.program_id(0)
    slot = lax.rem(i, 2)

    def gather(id_ref, s):
        def issue(t, _):
            for k in range(2):
                src = id_ref[2 * t + k]
                pltpu.make_async_copy(ys_ref.at[pl.ds(src, 1)], buf.at[s, k, pl.ds(t, 1)],
                                      sem.at[s]).start()
            return 0

        lax.fori_loop(0, tc, issue, 0, unroll=8)

    @pl.when(i == 0)
    def _():
        gather(ids_ref, 0)

    @pl.when(i + 1 < pl.num_programs(0))
    def _():
        gather(idn_ref, 1 - slot)

    for k in range(2):
        pltpu.make_async_copy(ys_ref.at[pl.ds(0, tc)], buf.at[slot, k], sem.at[slot]).wait()

    moe = buf[slot, 0] * wt_ref[:, 0:1] + buf[slot, 1] * wt_ref[:, 1:2]
    x2 = x1_ref[...] + mod_ref[0, 5:6, :] * moe
    o_ref[...] = x2 * lax.rsqrt(jnp.mean(x2 * x2, axis=-1, keepdims=True) + EPS) * g_ref[...]


def _combine(dest, ys, x1, wts_t, mod3, final_g, seq, tc):
    n, d = x1.shape
    last = n // tc - 1
    return pl.pallas_call(
        functools.partial(_combine_kernel, tc=tc),
        grid=(n // tc,),
        in_specs=[pl.BlockSpec((2 * tc,), lambda i: (i,), memory_space=pltpu.SMEM),
                  pl.BlockSpec((2 * tc,), lambda i: (jnp.minimum(i + 1, last),),
                               memory_space=pltpu.SMEM),
                  pl.BlockSpec(memory_space=pl.ANY),
                  pl.BlockSpec((tc, d), lambda i: (i, 0)),
                  pl.BlockSpec((tc, LANES), lambda i: (i, 0)),
                  pl.BlockSpec((1, 6, d), lambda i: ((i * tc) // seq, 0, 0)),
                  pl.BlockSpec((1, d), lambda i: (0, 0))],
        out_specs=pl.BlockSpec((tc, d), lambda i: (i, 0)),
        out_shape=jax.ShapeDtypeStruct((n, d), F32),
        scratch_shapes=[pltpu.VMEM((2, 2, tc, d), F32), pltpu.SemaphoreType.DMA((2,))],
        compiler_params=_cp(("arbitrary",)),
        name="combine",
    )(dest, dest, ys, x1, wts_t, mod3, final_g.reshape(1, d))


def _token_mixer(x2, mod3, norm1_g, w_in, ret_gn_g, cmp_pos_k, cmp_w1_k, cmp_w2_k,
                 cmp_pos_v, cmp_w1_v, cmp_w2_v, bsz, seq):
    n, d = x2.shape
    w_t = w_in.T
    w_gate_t = jnp.pad(w_t[PROJ_MAIN:], ((0, LANES - N_GATE_COLS), (0, 0))).astype(BF16)
    proj, gate_logits = _inproj(x2, mod3, norm1_g, w_t, w_gate_t, seq)

    o_ret = _retention(proj, ret_gn_g, bsz, seq)

    g_ = NSA_KV_GROUPS
    crow = seq // CMP_STRIDE

    pe_flat = lambda pe: pe.reshape(1, CMP_BLOCK * NSA_D)
    kc, vc = _compress(proj, pe_flat(cmp_pos_k), pe_flat(cmp_pos_v),
                       cmp_w1_k, cmp_w1_v, cmp_w2_k, cmp_w2_v, bsz, seq)

    o_nsa = _nsa(proj, kc, vc, gate_logits, bsz, seq)
    return o_ret, o_nsa


def _moe(h2, lt, x1, mod3, final_g, b_grp, b_exp, w_gate, w_up, w_down, seq, tm):
    n, d = h2.shape
    bias_col = jnp.zeros((LANES,), F32).at[:N_GROUPS].set(b_grp).at[N_GROUPS:N_GROUPS + N_EXPERTS].set(b_exp)
    bias_col = jnp.broadcast_to(bias_col[:, None], (LANES, LANES))
    ids, wts, cnt = _route(lt, bias_col)

    counts = cnt[:, 0]
    padded = (counts + tm - 1) // tm * tm
    pad_end = jnp.cumsum(padded)
    pad_start = (pad_end - padded).astype(I32)
    nb = (2 * n) // tm + N_EXPERTS
    n_used = (pad_end[-1] // tm).astype(I32).reshape(1)
    blk_start = jnp.arange(nb, dtype=I32) * tm
    blk_exp = jnp.minimum(jnp.sum((pad_end[None, :] <= blk_start[:, None]).astype(I32), axis=1),
                          N_EXPERTS - 1).astype(I32)
    last_exp = blk_exp[jnp.maximum(n_used[0] - 1, 0)]
    blk_exp = jnp.where(jnp.arange(nb) < n_used[0], blk_exp, last_exp)
    eid = jnp.arange(N_EXPERTS, dtype=I32)
    has = counts > 0
    exp_ord = jnp.sum((has[None, :] & (eid[None, :] < eid[:, None])).astype(I32), axis=1)
    exp_nxt = jnp.min(jnp.where(has[None, :] & (eid[None, :] > eid[:, None]), eid[None, :], N_EXPERTS), axis=1)
    exp_nxt = jnp.where(exp_nxt < N_EXPERTS, exp_nxt, -1).astype(I32)
    blk_ord = exp_ord[blk_exp]
    blk_nxt = exp_nxt[blk_exp]

    dest = _dest(ids, pad_start)
    xs = _dispatch(dest, pad_start, counts.astype(I32), h2, nb * tm, min(512, n), tm)
    ys = _experts(blk_exp, blk_ord, blk_nxt, n_used, xs, w_gate, w_up, w_down, tm)
    wts_t = jnp.pad(wts[:2].T, ((0, 0), (0, LANES - 2)))
    return _combine(dest, ys, x1, wts_t, mod3, final_g, seq, min(256, n))


def kernel(x, c, w_ada, b_ada, norm1_g, norm2_g, final_g, w_in, ret_gn_g, cmp_pos_k, cmp_w1_k,
           cmp_w2_k, cmp_pos_v, cmp_w1_v, cmp_w2_v, w_out, w_grp, b_grp, w_exp, b_exp, w_gate,
           w_up, w_down):
    bsz, seq, d = x.shape
    assert w_ada.shape[0] == 1, "single-layer block"
    n = bsz * seq
    x2 = x.reshape(n, d)
    mod3 = _ada(c, w_ada[0], b_ada[0]).reshape(bsz, 6, d)

    o_ret, o_nsa = _token_mixer(x2, mod3, norm1_g[0], w_in[0], ret_gn_g[0], cmp_pos_k[0],
                                cmp_w1_k[0], cmp_w2_k[0], cmp_pos_v[0], cmp_w1_v[0], cmp_w2_v[0],
                                bsz, seq)

    w_route = jnp.concatenate([w_grp[0], w_exp[0]], axis=1)
    w_route = jnp.pad(w_route, ((0, 0), (0, LANES - w_route.shape[1]))).T.astype(BF16)
    x1, h2, lt = _outproj(o_ret, o_nsa, x2, mod3, norm2_g[0], w_out[0].astype(BF16), w_route, seq)

    out = _moe(h2, lt, x1, mod3, final_g, b_grp[0], b_exp[0], w_gate[0], w_up[0], w_down[0],
               seq, 256)
    return out.reshape(bsz, seq, d)
```

```python
import functools

import numpy as np
import jax
import jax.numpy as jnp
from jax import lax
from jax.experimental import pallas as pl
from jax.experimental.pallas import tpu as pltpu

F32 = jnp.float32
BF16 = jnp.bfloat16
I32 = jnp.int32

RET_HEADS = 4
RET_DK = 256
RET_DV = 256
RET_CHUNK = 128
NSA_HEADS = 8
NSA_KV_GROUPS = 2
NSA_HPG = NSA_HEADS // NSA_KV_GROUPS
NSA_D = 128
CMP_BLOCK = 32
CMP_STRIDE = 16
SEL_BLOCK = 64
SEL_COUNT = 16
WIN = 512
N_GROUPS = 8
EXP_PER_GROUP = 8
N_EXPERTS = N_GROUPS * EXP_PER_GROUP
ROPE_BASE = 10000.0
EPS = 1e-6
NEG = -1e30
FORCE_BONUS = 1e4

LANES = 128
SUBLANES = 8
MXU_DIM = 256
VMEM_LIMIT = 52 * 1024 * 1024

ADA_TN = 1024
INPROJ_TM = 2048
INPROJ_TN = 512
RET_CHUNKS = 4
NSA_TQ = 256
NSA_TK = 512
NSA_VT = MXU_DIM
OUTPROJ_TM = 512
ROUTE_TC = 2048
ROUTE_SUB = 512
MOE_TM = 256
DISPATCH_TC = 1024
COMBINE_TC = 512

_C_NQ = 32
_C_KC, _C_VC, _C_KS, _C_VS, _C_KW, _C_VW = 40, 42, 44, 46, 48, 50
_C_GATE = 52
PROJ_MAIN = _C_GATE * LANES
N_GATE_COLS = NSA_HEADS * 3


def _cp(sem, vmem=VMEM_LIMIT):
    return pltpu.CompilerParams(dimension_semantics=sem, vmem_limit_bytes=vmem)


def _silu(v):
    return v * jax.nn.sigmoid(v)


def _dot(a, b):
    return jnp.dot(a, b, preferred_element_type=F32)


def _dot_nt(a, b):
    return lax.dot_general(a, b, (((1,), (1,)), ((), ())), preferred_element_type=F32)


def _dot_tn(a, b):
    return lax.dot_general(a, b, (((0,), (0,)), ((), ())), preferred_element_type=F32)


def _ada_kernel(c_ref, w_ref, b_ref, o_ref):
    ca = _silu(c_ref[...]).astype(BF16)
    o_ref[...] = _dot(ca, w_ref[...].astype(BF16)) + b_ref[...]


def _ada(c, w, b):
    bsz, d = c.shape
    n = w.shape[1]
    tn = min(ADA_TN, n)
    return pl.pallas_call(
        _ada_kernel,
        grid=(n // tn,),
        in_specs=[pl.BlockSpec((bsz, d), lambda j: (0, 0)),
                  pl.BlockSpec((d, tn), lambda j: (0, j)),
                  pl.BlockSpec((1, tn), lambda j: (0, j))],
        out_specs=pl.BlockSpec((bsz, tn), lambda j: (0, j)),
        out_shape=jax.ShapeDtypeStruct((bsz, n), F32),
        compiler_params=_cp(("arbitrary",)),
        name="ada",
    )(c, w, b.reshape(1, n))


def _inproj_kernel(x_ref, mod_ref, g_ref, wt_ref, wgt_ref, proj_ref, gate_ref, h_ref):
    @pl.when(pl.program_id(1) == 0)
    def _():
        x = x_ref[...]
        y = x * lax.rsqrt(jnp.mean(x * x, axis=-1, keepdims=True) + EPS) * g_ref[...]
        h = (y * (1.0 + mod_ref[0, 1:2, :]) + mod_ref[0, 0:1, :]).astype(BF16)
        h_ref[...] = h
        gate_ref[...] = _dot_nt(h, wgt_ref[...])

    proj_ref[...] = _dot_nt(h_ref[...], wt_ref[...].astype(BF16)).astype(BF16)


def _inproj(x2, mod3, g, w_t, wg_t, seq):
    n, d = x2.shape
    tm = min(INPROJ_TM, seq)
    tn = INPROJ_TN
    nj = PROJ_MAIN // tn
    return pl.pallas_call(
        _inproj_kernel,
        grid=(n // tm, nj),
        in_specs=[pl.BlockSpec((tm, d), lambda i, j: (i, 0), pipeline_mode=pl.Buffered(1)),
                  pl.BlockSpec((1, 6, d), lambda i, j: ((i * tm) // seq, 0, 0)),
                  pl.BlockSpec((1, d), lambda i, j: (0, 0)),
                  pl.BlockSpec((tn, d), lambda i, j: (j, 0)),
                  pl.BlockSpec((LANES, d), lambda i, j: (0, 0))],
        out_specs=[pl.BlockSpec((tm, tn), lambda i, j: (i, j)),
                   pl.BlockSpec((tm, LANES), lambda i, j: (i, 0))],
        out_shape=[jax.ShapeDtypeStruct((n, PROJ_MAIN), BF16),
                   jax.ShapeDtypeStruct((n, LANES), F32)],
        scratch_shapes=[pltpu.VMEM((tm, d), BF16)],
        compiler_params=_cp(("arbitrary", "arbitrary")),
        name="inproj",
    )(x2, mod3, g.reshape(1, d), w_t, wg_t)


def _retention_kernel(q_ref, k_ref, v_ref, g_ref, cos_ref, sin_ref, din_ref, zeta_ref,
                      qdec_ref, cdec_ref, gn_ref, o_ref, s_ref):
    @pl.when(pl.program_id(1) == 0)
    def _():
        s_ref[...] = jnp.zeros_like(s_ref)

    half = RET_DK // 2
    c = RET_CHUNK

    for sub in range(q_ref.shape[0] // c):
        rows = slice(sub * c, (sub + 1) * c)
        cos = cos_ref[rows, :]
        sin = sin_ref[rows, :]

        def rot(a):
            a1, a2 = a[:, :half], a[:, half:]
            return jnp.concatenate([a1 * cos - a2 * sin, a1 * sin + a2 * cos], axis=1)

        for h in range(RET_HEADS):
            qs = slice(h * RET_DK, (h + 1) * RET_DK)
            vs = slice(h * RET_DV, (h + 1) * RET_DV)
            q = rot(q_ref[rows, qs].astype(F32))
            k = rot(k_ref[rows, qs].astype(F32)) * (RET_DK ** -0.5)
            v = v_ref[rows, vs]
            qb = q.astype(BF16)
            kb = k.astype(BF16)
            s = _dot_nt(qb, kb) * din_ref[h]
            inner = _dot(s.astype(BF16), v)
            s_prev = s_ref[h]
            cross = _dot(qb, s_prev.astype(BF16)) * qdec_ref[h]
            kv = _dot_tn((k * zeta_ref[h]).astype(BF16), v)
            s_ref[h] = cdec_ref[h] * s_prev + kv
            o = inner + cross
            mu = jnp.mean(o, axis=-1, keepdims=True)
            oc = o - mu
            var = jnp.mean(oc * oc, axis=-1, keepdims=True)
            o = oc * lax.rsqrt(var + EPS) * gn_ref[:, vs]
            o_ref[rows, vs] = (o * _silu(g_ref[rows, vs].astype(F32))).astype(BF16)


def _retention(proj, gn_g, bsz, seq):
    n = proj.shape[0]
    c = RET_CHUNK
    nc = seq // c
    hw = RET_HEADS * RET_DK
    half = RET_DK // 2
    pos = jnp.arange(seq, dtype=F32)
    inv = ROPE_BASE ** (-jnp.arange(half, dtype=F32) / half)
    ang = pos[:, None] * inv[None, :]
    cos, sin = jnp.cos(ang), jnp.sin(ang)
    log_gamma = jnp.log1p(-jnp.exp2(-5.0 - jnp.arange(RET_HEADS, dtype=F32)))
    idx = jnp.arange(c, dtype=F32)
    rel = idx[:, None] - idx[None, :]
    decay_in = jnp.where(rel >= 0, jnp.exp(log_gamma[:, None, None] * jnp.maximum(rel, 0.0)), 0.0)
    zeta = jnp.exp(log_gamma[:, None] * (c - 1 - idx)[None, :])[:, :, None]
    q_decay = jnp.exp(log_gamma[:, None] * (idx + 1)[None, :])[:, :, None]
    chunk_decay = jnp.exp(log_gamma * c)[:, None, None]
    rb = min(RET_CHUNKS * c, seq)
    ns = seq // rb
    row = lambda b, t: (b * ns + t)
    return pl.pallas_call(
        _retention_kernel,
        grid=(bsz, ns),
        in_specs=[pl.BlockSpec((rb, hw), lambda b, t: (row(b, t), 0)),
                  pl.BlockSpec((rb, hw), lambda b, t: (row(b, t), 1)),
                  pl.BlockSpec((rb, hw), lambda b, t: (row(b, t), 2)),
                  pl.BlockSpec((rb, hw), lambda b, t: (row(b, t), 3)),
                  pl.BlockSpec((rb, half), lambda b, t: (t, 0)),
                  pl.BlockSpec((rb, half), lambda b, t: (t, 0)),
                  pl.BlockSpec((RET_HEADS, c, c), lambda b, t: (0, 0, 0)),
                  pl.BlockSpec((RET_HEADS, c, 1), lambda b, t: (0, 0, 0)),
                  pl.BlockSpec((RET_HEADS, c, 1), lambda b, t: (0, 0, 0)),
                  pl.BlockSpec((RET_HEADS, 1, 1), lambda b, t: (0, 0, 0)),
                  pl.BlockSpec((1, hw), lambda b, t: (0, 0))],
        out_specs=pl.BlockSpec((rb, hw), lambda b, t: (row(b, t), 0)),
        out_shape=jax.ShapeDtypeStruct((n, hw), BF16),
        scratch_shapes=[pltpu.VMEM((RET_HEADS, RET_DK, RET_DV), F32)],
        compiler_params=_cp(("arbitrary", "arbitrary")),
        name="retention",
    )(proj, proj, proj, proj, cos, sin, decay_in, zeta, q_decay, chunk_decay,
      gn_g.reshape(1, hw))


def _compress_kernel(ak_ref, av_ref, pek_ref, pev_ref, w1k_ref, w1v_ref, w2k_ref, w2v_ref,
                     kc_ref, vc_ref, a_scr):
    seq = ak_ref.shape[0]
    nblk = seq // CMP_STRIDE
    a_scr[seq:, :] = jnp.zeros((a_scr.shape[0] - seq, NSA_D), F32)

    def one(a_ref, pe_ref, w1_ref, w2_ref, o_ref):
        a_scr[:seq, :] = a_ref[...].astype(F32)
        flat = jnp.concatenate([a_scr[pl.ds(l, nblk, stride=CMP_STRIDE), :] for l in range(CMP_BLOCK)],
                               axis=1)
        pre = _dot((flat + pe_ref[...]).astype(BF16), w1_ref[...].astype(BF16))
        o_ref[...] = _dot(_silu(pre).astype(BF16), w2_ref[...].astype(BF16)).astype(BF16)

    one(ak_ref, pek_ref, w1k_ref, w2k_ref, kc_ref)
    one(av_ref, pev_ref, w1v_ref, w2v_ref, vc_ref)


def _compress(proj, pek, pev, w1k, w1v, w2k, w2v, bsz, seq):
    g_ = NSA_KV_GROUPS
    nblk = seq // CMP_STRIDE
    full = lambda a: pl.BlockSpec(a.shape, lambda i: (0,) * a.ndim)
    kblk = pl.BlockSpec((seq, NSA_D), lambda i: (i // g_, _C_KC + i % g_))
    vblk = pl.BlockSpec((seq, NSA_D), lambda i: (i // g_, _C_VC + i % g_))
    oblk = pl.BlockSpec((nblk, NSA_D), lambda i: (i, 0))
    return pl.pallas_call(
        _compress_kernel,
        grid=(bsz * g_,),
        in_specs=[kblk, vblk, full(pek), full(pev), full(w1k), full(w1v), full(w2k), full(w2v)],
        out_specs=[oblk, oblk],
        out_shape=[jax.ShapeDtypeStruct((bsz * g_ * nblk, NSA_D), BF16)] * 2,
        scratch_shapes=[pltpu.VMEM((seq + CMP_BLOCK, NSA_D), F32)],
        compiler_params=_cp(("arbitrary",)),
        name="compress",
    )(proj, proj, pek, pev, w1k, w1v, w2k, w2v)


def _nsa_kernel(q_ref, kc_ref, vc_ref, ks_ref, vs_ref, kw_ref, vw_ref, gate_ref, ovt_ref,
                o_ref, kaug_ref, vst_ref, vwt_ref, *, tq, tk, vt, seq, ncmp, wlen):
    crow = kc_ref.shape[0]
    i = pl.program_id(2)
    hg = NSA_HPG
    d = NSA_D
    r = hg * tq

    @pl.when(i == 0)
    def _():
        kaug_ref[:, :d] = ks_ref[...]
        blk = lax.broadcasted_iota(I32, (seq, LANES), 0) // SEL_BLOCK
        lane = lax.broadcasted_iota(I32, (seq, LANES), 1)
        kaug_ref[:, d:] = (blk == lane).astype(BF16)
        for c in range(seq // vt):
            vst_ref[c] = vs_ref[c * vt:(c + 1) * vt, :].astype(F32).T.astype(BF16)
            vwt_ref[c] = vw_ref[c * vt:(c + 1) * vt, :].astype(F32).T.astype(BF16)

    def pv_t(vt_ref, first_tile, p):
        out = None
        for c in range(p.shape[0] // vt):
            part = _dot(vt_ref[first_tile + c], p[c * vt:(c + 1) * vt, :])
            out = part if out is None else out + part
        return out

    q = q_ref[...]
    qh = [(q[:, h * d:(h + 1) * d].astype(F32) * (d ** -0.5)).astype(BF16) for h in range(hg)]
    qa = jnp.concatenate(qh, axis=0)
    t1 = i * tq + lax.broadcasted_iota(I32, (tq, 1), 0)
    tpos = jnp.concatenate([t1] * hg, axis=0)
    t_row = i * tq + lax.broadcasted_iota(I32, (1, tq), 1)
    tpos_row = jnp.concatenate([t_row] * hg, axis=1)

    w0 = pl.multiple_of(jnp.maximum(i * tq + tq - wlen, 0), vt)
    sw = _dot_nt(kw_ref[pl.ds(w0, wlen), :], qa)
    delta = tpos_row - (w0 + lax.broadcasted_iota(I32, (wlen, 1), 0))
    sw = jnp.where((delta >= 0) & (delta < WIN), sw, NEG)
    e_w = jnp.exp(sw - jnp.max(sw, axis=0, keepdims=True))
    o_win = (pv_t(vwt_ref, w0 // vt, e_w.astype(BF16)) / jnp.sum(e_w, axis=0, keepdims=True)).T

    sct = _dot_nt(kc_ref[...], qa)
    cidx = lax.broadcasted_iota(I32, (crow, 1), 0)
    cmask = (cidx * CMP_STRIDE + (CMP_BLOCK - 1) <= tpos_row) & (cidx < ncmp)
    sct = jnp.where(cmask, sct, NEG)
    e_c = jnp.where(cmask, jnp.exp(sct - jnp.max(sct, axis=0, keepdims=True)), 0.0)
    den_c = jnp.sum(e_c, axis=0, keepdims=True)
    p_t = jnp.where(den_c > 0.0, e_c / jnp.where(den_c > 0.0, den_c, 1.0), 0.0)
    o_cmp = _dot_tn(p_t.astype(BF16), vc_ref[...])

    psum_t = p_t[:, 0:tq]
    for h in range(1, hg):
        psum_t = psum_t + p_t[:, h * tq:(h + 1) * tq]
    nsel_blocks = seq // SEL_BLOCK
    nsb = ovt_ref.shape[0]
    imp_t = jnp.dot(ovt_ref[...], psum_t, preferred_element_type=F32,
                    precision=lax.Precision.HIGHEST)
    sidx = lax.broadcasted_iota(I32, (nsb, 1), 0)
    cur = t_row // SEL_BLOCK
    valid = sidx <= cur
    forced = (sidx == 0) | (sidx == cur) | (sidx == cur - 1)
    score = jnp.where(valid, imp_t + jnp.where(forced, FORCE_BONUS, 0.0), -1.0)
    rank = jnp.zeros((nsb, tq), F32)
    for s2 in range(nsel_blocks):
        row = score[s2:s2 + 1, :]
        beats = (row > score) | ((row == score) & (sidx > s2))
        rank = rank + beats.astype(F32)
    sel = valid & (rank < float(min(SEL_COUNT, nsel_blocks)))
    selb_t = jnp.where(sel, 0.0, NEG)
    selb_t = jnp.concatenate([selb_t, jnp.zeros((LANES - nsb, tq), F32)], axis=0)
    selb = selb_t.T.astype(BF16)
    q_aug = jnp.concatenate([jnp.concatenate([qh[h], selb], axis=1) for h in range(hg)], axis=0)

    n_full = (i * tq + 1) // tk
    kl = lax.broadcasted_iota(I32, (tk, 1), 0)

    def tile(j, carry, masked):
        m, l, acc = carry
        k0 = pl.multiple_of(j * tk, tk)
        s = _dot_nt(kaug_ref[pl.ds(k0, tk), :], q_aug)
        if masked:
            s = jnp.where(k0 + kl <= tpos_row, s, NEG)
        m_new = jnp.maximum(m, jnp.max(s, axis=0, keepdims=True))
        alpha = jnp.exp(m - m_new)
        p = jnp.exp(s - m_new)
        l = alpha * l + jnp.sum(p, axis=0, keepdims=True)
        acc = alpha * acc + pv_t(vst_ref, j * (tk // vt), p.astype(BF16))
        return m_new, l, acc

    m0 = jnp.full((1, r), NEG, F32)
    l0 = jnp.zeros((1, r), F32)
    a0 = jnp.zeros((d, r), F32)
    carry = lax.fori_loop(0, n_full, functools.partial(tile, masked=False), (m0, l0, a0))
    _, l_s, acc_s = tile(n_full, carry, True)
    o_slc = (acc_s / l_s).T

    gt = jax.nn.sigmoid(gate_ref[...])
    grp = pl.program_id(1)
    for g in range(1, NSA_KV_GROUPS):
        gt = jnp.where(grp == g, pltpu.roll(gt, LANES - g * 3 * hg, 1), gt)
    for h in range(hg):
        rows = slice(h * tq, (h + 1) * tq)
        o = (gt[:, 3 * h:3 * h + 1] * o_cmp[rows] + gt[:, 3 * h + 1:3 * h + 2] * o_slc[rows]
             + gt[:, 3 * h + 2:3 * h + 3] * o_win[rows])
        o_ref[:, h * d:(h + 1) * d] = o.astype(BF16)


def _nsa(proj, kc, vc, gates_g, bsz, seq):
    n = proj.shape[0]
    g_ = NSA_KV_GROUPS
    tq = NSA_TQ
    tk = min(NSA_TK, seq)
    nq = seq // tq
    ncmp = (seq - CMP_BLOCK) // CMP_STRIDE + 1
    wlen = min(WIN + tq, seq)
    nsel = seq // SEL_BLOCK
    crow = seq // CMP_STRIDE
    assert nsel <= LANES and crow <= LANES and tq <= tk and tk % tq == 0
    nsb = -(-nsel // 8) * 8
    ss = np.arange(nsb)[:, None] * SEL_BLOCK
    cs = np.arange(crow)[None, :] * CMP_STRIDE
    ov = ((cs < ss + SEL_BLOCK) & (cs + CMP_BLOCK > ss)
          & (np.arange(crow)[None, :] < ncmp) & (np.arange(nsb)[:, None] < nsel))
    ov = jnp.asarray(ov.astype(np.float32))
    kvspec = lambda c0: pl.BlockSpec((seq, NSA_D), lambda b, g, i: (b, c0 + g))
    vt = NSA_VT
    assert tk % vt == 0 and wlen % vt == 0 and tq % vt == 0 and seq % vt == 0
    kern = functools.partial(_nsa_kernel, tq=tq, tk=tk, vt=vt, seq=seq, ncmp=ncmp, wlen=wlen)
    return pl.pallas_call(
        kern,
        grid=(bsz, g_, nq),
        in_specs=[pl.BlockSpec((tq, NSA_HPG * NSA_D), lambda b, g, i: (b * nq + i, _C_NQ // NSA_HPG + g)),
                  pl.BlockSpec((crow, NSA_D), lambda b, g, i: (b * g_ + g, 0)),
                  pl.BlockSpec((crow, NSA_D), lambda b, g, i: (b * g_ + g, 0)),
                  kvspec(_C_KS), kvspec(_C_VS), kvspec(_C_KW), kvspec(_C_VW),
                  pl.BlockSpec((tq, LANES), lambda b, g, i: (b * nq + i, 0)),
                  pl.BlockSpec((nsb, crow), lambda b, g, i: (0, 0))],
        out_specs=pl.BlockSpec((tq, NSA_HPG * NSA_D), lambda b, g, i: (b * nq + i, g)),
        out_shape=jax.ShapeDtypeStruct((n, NSA_HEADS * NSA_D), BF16),
        scratch_shapes=[pltpu.VMEM((seq, 2 * NSA_D), BF16),
                        pltpu.VMEM((seq // vt, NSA_D, vt), BF16),
                        pltpu.VMEM((seq // vt, NSA_D, vt), BF16)],
        compiler_params=_cp(("arbitrary", "arbitrary", "arbitrary")),
        name="nsa",
    )(proj, kc, vc, proj, proj, proj, proj, gates_g, ov)


def _outproj_kernel(oret_ref, onsa_ref, x_ref, mod_ref, g_ref, w_ref, wr_ref,
                    x1_ref, h2_ref, lt_ref):
    hw = oret_ref.shape[1]
    mix = _dot(oret_ref[...], w_ref[:hw, :]) + _dot(onsa_ref[...], w_ref[hw:, :])
    x1 = x_ref[...] + mod_ref[0, 2:3, :] * mix
    x1_ref[...] = x1
    y = x1 * lax.rsqrt(jnp.mean(x1 * x1, axis=-1, keepdims=True) + EPS) * g_ref[...]
    h2 = y * (1.0 + mod_ref[0, 4:5, :]) + mod_ref[0, 3:4, :]
    h2_ref[...] = h2
    lt_ref[...] = _dot_nt(wr_ref[...], h2.astype(BF16))


def _outproj(o_ret, o_nsa, x2, mod3, g, w_bf, wr_bf, seq):
    n, d = x2.shape
    tm = min(OUTPROJ_TM, seq)
    hw = o_ret.shape[1]
    return pl.pallas_call(
        _outproj_kernel,
        grid=(n // tm,),
        in_specs=[pl.BlockSpec((tm, hw), lambda i: (i, 0)),
                  pl.BlockSpec((tm, o_nsa.shape[1]), lambda i: (i, 0)),
                  pl.BlockSpec((tm, d), lambda i: (i, 0)),
                  pl.BlockSpec((1, 6, d), lambda i: ((i * tm) // seq, 0, 0)),
                  pl.BlockSpec((1, d), lambda i: (0, 0)),
                  pl.BlockSpec(w_bf.shape, lambda i: (0, 0)),
                  pl.BlockSpec(wr_bf.shape, lambda i: (0, 0))],
        out_specs=[pl.BlockSpec((tm, d), lambda i: (i, 0)),
                   pl.BlockSpec((tm, d), lambda i: (i, 0)),
                   pl.BlockSpec((LANES, tm), lambda i: (0, i))],
        out_shape=[jax.ShapeDtypeStruct((n, d), F32),
                   jax.ShapeDtypeStruct((n, d), F32),
                   jax.ShapeDtypeStruct((LANES, n), F32)],
        compiler_params=_cp(("arbitrary",)),
        name="outproj",
    )(o_ret, o_nsa, x2, mod3, g.reshape(1, d), w_bf, wr_bf)


def _route_kernel(lt_ref, b_ref, tri_ref, ids_ref, wts_ref, cnt_ref, carry_ref, *, sub):
    @pl.when(pl.program_id(0) == 0)
    def _():
        carry_ref[...] = jnp.zeros_like(carry_ref)

    ng, ne = N_GROUPS, EXP_PER_GROUP
    l = lt_ref[...] + b_ref[:, 0:1]
    tc = l.shape[1]
    ridx = lax.broadcasted_iota(I32, (ng, tc), 0).astype(F32)

    def softmax0(v):
        e = jnp.exp(v - jnp.max(v, axis=0, keepdims=True))
        return e / jnp.sum(e, axis=0, keepdims=True)

    def top1(p):
        top = jnp.max(p, axis=0, keepdims=True)
        idx = jnp.min(jnp.where(p == top, ridx, float(ng)), axis=0, keepdims=True)
        return top, idx

    pg_top, grp = top1(softmax0(l[0:ng]))
    leg = jnp.zeros((ne, tc), F32)
    for g in range(ng):
        leg = jnp.where(grp == float(g), l[ng + g * ne:ng + (g + 1) * ne], leg)
    pe = softmax0(leg)
    p1, i1 = top1(pe)
    p2, i2 = top1(jnp.where(ridx == i1, -1.0, pe))
    den = p1 + p2
    w1 = pg_top * p1 / den
    w2 = pg_top * p2 / den
    e1 = grp * float(ne) + i1
    e2 = grp * float(ne) + i2

    eio = lax.broadcasted_iota(I32, (N_EXPERTS, sub), 0).astype(F32)
    r1 = []
    r2 = []
    carry = carry_ref[:, 0:1]
    for c in range(tc // sub):
        cs = slice(c * sub, (c + 1) * sub)
        oh1 = (eio == e1[:, cs]).astype(F32)
        oh2 = (eio == e2[:, cs]).astype(F32)
        oh = oh1 + oh2
        before = carry + _dot(oh.astype(BF16), tri_ref[...])
        r1.append(jnp.sum(oh1 * before, axis=0, keepdims=True))
        r2.append(jnp.sum(oh2 * before, axis=0, keepdims=True))
        carry = carry + jnp.sum(oh, axis=1, keepdims=True)
    carry_ref[...] = jnp.broadcast_to(carry, carry_ref.shape)
    cnt_ref[...] = jnp.broadcast_to(carry, cnt_ref.shape).astype(I32)
    r1 = jnp.concatenate(r1, axis=1)
    r2 = jnp.concatenate(r2, axis=1)
    zf = jnp.zeros((4, tc), F32)
    ids_ref[...] = jnp.concatenate([e1, e2, r1, r2, zf], axis=0).astype(I32)
    wts_ref[...] = jnp.concatenate([w1, w2, jnp.zeros((6, tc), F32)], axis=0)


def _route(lt, bias_col):
    n = lt.shape[1]
    tc = min(ROUTE_TC, n)
    sub = min(ROUTE_SUB, tc)
    tri =jnp.asarray(np.triu(np.ones((sub, sub), np.float32), 1), BF16)
    return pl.pallas_call(
        functools.partial(_route_kernel, sub=sub),
        grid=(n // tc,),
        in_specs=[pl.BlockSpec((LANES, tc), lambda i: (0, i)),
                  pl.BlockSpec((LANES, LANES), lambda i: (0, 0)),
                  pl.BlockSpec((sub, sub), lambda i: (0, 0))],
        out_specs=[pl.BlockSpec((8, tc), lambda i: (0, i)),
                   pl.BlockSpec((8, tc), lambda i: (0, i)),
                   pl.BlockSpec((N_EXPERTS, LANES), lambda i: (0, 0))],
        out_shape=[jax.ShapeDtypeStruct((8, n), I32),
                   jax.ShapeDtypeStruct((8, n), F32),
                   jax.ShapeDtypeStruct((N_EXPERTS, LANES), I32)],
        scratch_shapes=[pltpu.VMEM((N_EXPERTS, LANES), F32)],
        compiler_params=_cp(("arbitrary",)),
        name="route",
    )(lt, bias_col, tri)


def _dest_kernel(ids_ref, ps_ref, o_ref):
    ids = ids_ref[...].astype(F32)
    tc = ids.shape[1]
    eio = lax.broadcasted_iota(I32, (N_EXPERTS, tc), 0).astype(F32)
    ps = ps_ref[:, 0:1]
    rows = [jnp.sum(jnp.where(eio == ids[k:k + 1], ps, 0.0), axis=0, keepdims=True) + ids[2 + k:3 + k]
            for k in range(2)]
    o_ref[...] = jnp.concatenate(rows + [jnp.zeros((6, tc), F32)], axis=0).astype(I32)


def _dest(ids, pad_start):
    n = ids.shape[1]
    tc = min(ROUTE_TC, n)
    ps = jnp.broadcast_to(pad_start.astype(F32)[:, None], (N_EXPERTS, LANES))
    dest = pl.pallas_call(
        _dest_kernel,
        grid=(n // tc,),
        in_specs=[pl.BlockSpec((8, tc), lambda i: (0, i)),
                  pl.BlockSpec((N_EXPERTS, LANES), lambda i: (0, 0))],
        out_specs=pl.BlockSpec((8, tc), lambda i: (0, i)),
        out_shape=jax.ShapeDtypeStruct((8, n), I32),
        compiler_params=_cp(("arbitrary",)),
        name="dest",
    )(ids, ps)
    return dest[:2].T.reshape(2 * n)


def _dispatch_kernel(dst_ref, ps_ref, cnt_ref, h_ref, xs_ref, zbuf, sem, zsem, *, tcd, tm, zr):
    @pl.when(pl.program_id(0) == 0)
    def _():
        zbuf[...] = jnp.zeros_like(zbuf)
        sizes = [zr >> b for b in range(zr.bit_length()) if (zr >> b) >= SUBLANES]

        def zero_rows(row, size):
            return pltpu.make_async_copy(zbuf.at[pl.ds(0, size)],
                                         xs_ref.at[pl.ds(pl.multiple_of(row, SUBLANES), size)], zsem)

        def zero_row(row):
            return pltpu.make_async_copy(zbuf.at[pl.ds(0, 1)], xs_ref.at[pl.ds(row, 1)], zsem)

        def fill(e, wait):
            cnt = cnt_ref[e]
            cnt8 = lax.div(cnt + (SUBLANES - 1), SUBLANES) * SUBLANES
            gap = lax.rem(tm - lax.rem(cnt8, tm), tm)
            base = ps_ref[e]

            def single(c, _):
                cp = zero_row(base + cnt + c)
                cp.wait() if wait else cp.start()
                return 0

            lax.fori_loop(0, cnt8 - cnt, single, 0)
            row = base + cnt8
            for size in sizes:
                has = lax.rem(lax.div(gap, size), 2) == 1

                @pl.when(has)
                def _():
                    cp = zero_rows(row, size)
                    cp.wait() if wait else cp.start()

                row = row + jnp.where(has, size, 0)

        last = N_EXPERTS - 1
        used_rows = ps_ref[last] + lax.div(cnt_ref[last] + (tm - 1), tm) * tm
        n_tail = lax.div(xs_ref.shape[0] - used_rows, zr)

        def tail(c, wait):
            cp = zero_rows(used_rows + c * zr, zr)
            cp.wait() if wait else cp.start()

        for wait in (False, True):
            lax.fori_loop(0, N_EXPERTS, lambda e, _: (fill(e, wait), 0)[1], 0)
            lax.fori_loop(0, n_tail, lambda c, _: (tail(c, wait), 0)[1], 0)

    def issue(t, _):
        for k in range(2):
            dst = dst_ref[2 * t + k]
            pltpu.make_async_copy(h_ref.at[pl.ds(t, 1)], xs_ref.at[pl.ds(dst, 1)], sem).start()
        return 0

    lax.fori_loop(0, tcd, issue, 0, unroll=8)
    for k in range(2):
        pltpu.make_async_copy(h_ref, xs_ref.at[pl.ds(0, tcd)], sem).wait()


def _dispatch(dest, pad_start, counts, h2, rows, tcd, tm):
    n, d = h2.shape
    assert tm & (tm - 1) == 0 and tm >= 2 * SUBLANES
    zr = tm // 2
    return pl.pallas_call(
        functools.partial(_dispatch_kernel, tcd=tcd, tm=tm, zr=zr),
        grid=(n // tcd,),
        in_specs=[pl.BlockSpec((2 * tcd,), lambda i: (i,), memory_space=pltpu.SMEM),
                  pl.BlockSpec(memory_space=pltpu.SMEM),
                  pl.BlockSpec(memory_space=pltpu.SMEM),
                  pl.BlockSpec((tcd, d), lambda i: (i, 0))],
        out_specs=pl.BlockSpec(memory_space=pl.ANY),
        out_shape=jax.ShapeDtypeStruct((rows, d), h2.dtype),
        scratch_shapes=[pltpu.VMEM((zr, d), h2.dtype), pltpu.SemaphoreType.DMA(()),
                        pltpu.SemaphoreType.DMA(())],
        compiler_params=_cp(("arbitrary",)),
        name="dispatch",
    )(dest, pad_start, counts, h2)


def _experts_kernel(be_ref, ord_ref, nxt_ref, nu_ref, xs_ref, wg_hbm, wu_hbm, wd_hbm, ys_ref,
                    g_f, u_f, d_f, g_s, u_s, d_s, sem):
    i = pl.program_id(0)
    used = i < nu_ref[0]
    e = be_ref[i]
    fresh = (i == 0) | (e != be_ref[jnp.maximum(i - 1, 0)])
    slot = lax.rem(ord_ref[i], 2)

    def fetch(expert, s):
        return [pltpu.make_async_copy(w.at[expert], f.at[s], sem.at[s, k])
                for k, (w, f) in enumerate(((wg_hbm, g_f), (wu_hbm, u_f), (wd_hbm, d_f)))]

    @pl.when(used & (i == 0))
    def _():
        for c in fetch(e, slot):
            c.start()

    @pl.when(used & fresh)
    def _():
        for c in fetch(e, slot):
            c.wait()
        nxt = nxt_ref[i]

        @pl.when(nxt >= 0)
        def _():
            for c in fetch(nxt, 1 - slot):
                c.start()

        g_s[...] = g_f[slot].astype(BF16)
        u_s[...] = u_f[slot].astype(BF16)
        d_s[...] = d_f[slot].astype(BF16)

    @pl.when(used)
    def _():
        x = xs_ref[...].astype(BF16)
        a = _silu(_dot(x, g_s[...])) * _dot(x, u_s[...])
        ys_ref[...] = _dot(a.astype(BF16), d_s[...])

    @pl.when(jnp.logical_not(used))
    def _():
        ys_ref[...] = jnp.zeros_like(ys_ref)


def _experts(blk_exp, blk_ord, blk_nxt, n_used, xs, w_gate, w_up, w_down, tm):
    d = xs.shape[1]
    de = w_gate.shape[2]
    nb = blk_exp.shape[0]
    row = lambda i, be, od, nx, nu: (jnp.minimum(i, nu[0] - 1), 0)
    anyspec = pl.BlockSpec(memory_space=pl.ANY)
    grid_spec = pltpu.PrefetchScalarGridSpec(
        num_scalar_prefetch=4,
        grid=(nb,),
        in_specs=[pl.BlockSpec((tm, d), row), anyspec, anyspec, anyspec],
        out_specs=pl.BlockSpec((tm, d), lambda i, be, od, nx, nu: (i, 0)),
        scratch_shapes=[pltpu.VMEM((2, d, de), F32), pltpu.VMEM((2, d, de), F32),
                        pltpu.VMEM((2, de, d), F32),
                        pltpu.VMEM((d, de), BF16), pltpu.VMEM((d, de), BF16),
                        pltpu.VMEM((de, d), BF16),
                        pltpu.SemaphoreType.DMA((2, 3))],
    )
    return pl.pallas_call(
        _experts_kernel,
        grid_spec=grid_spec,
        out_shape=jax.ShapeDtypeStruct((nb * tm, d), F32),
        compiler_params=_cp(("arbitrary",)),
        name="experts",
    )(blk_exp, blk_ord, blk_nxt, n_used, xs, w_gate, w_up, w_down)


def _combine_kernel(ids_ref, idn_ref, ys_ref, x1_ref, wt_ref, mod_ref, g_ref, o_ref,
                    buf, sem, *, tc):
    i = pl.program_id(0)
    slot = lax.rem(i, 2)

    def gather(id_ref, s):
        def issue(t, _):
            for k in range(2):
                src = id_ref[2 * t + k]
                pltpu.make_async_copy(ys_ref.at[pl.ds(src, 1)], buf.at[s, k, pl.ds(t, 1)],
                                      sem.at[s]).start()
            return 0

        lax.fori_loop(0, tc, issue, 0, unroll=8)

    @pl.when(i == 0)
    def _():
        gather(ids_ref, 0)

    @pl.when(i + 1 < pl.num_programs(0))
    def _():
        gather(idn_ref, 1 - slot)

    for k in range(2):
        pltpu.make_async_copy(ys_ref.at[pl.ds(0, tc)], buf.at[slot, k], sem.at[slot]).wait()

    moe = buf[slot, 0] * wt_ref[:, 0:1] + buf[slot, 1] * wt_ref[:, 1:2]
    x2 = x1_ref[...] + mod_ref[0, 5:6, :] * moe
    o_ref[...] = x2 * lax.rsqrt(jnp.mean(x2 * x2, axis=-1, keepdims=True) + EPS) * g_ref[...]


def _combine(dest, ys, x1, wts_t, mod3, final_g, seq, tc):
    n, d = x1.shape
    last = n // tc - 1
    return pl.pallas_call(
        functools.partial(_combine_kernel, tc=tc),
        grid=(n // tc,),
        in_specs=[pl.BlockSpec((2 * tc,), lambda i: (i,), memory_space=pltpu.SMEM),
                  pl.BlockSpec((2 * tc,), lambda i: (jnp.minimum(i + 1, last),),
                               memory_space=pltpu.SMEM),
                  pl.BlockSpec(memory_space=pl.ANY),
                  pl.BlockSpec((tc, d), lambda i: (i, 0)),
                  pl.BlockSpec((tc, LANES), lambda i: (i, 0)),
                  pl.BlockSpec((1, 6, d), lambda i: ((i * tc) // seq, 0, 0)),
                  pl.BlockSpec((1, d), lambda i: (0, 0))],
        out_specs=pl.BlockSpec((tc, d), lambda i: (i, 0)),
        out_shape=jax.ShapeDtypeStruct((n, d), F32),
        scratch_shapes=[pltpu.VMEM((2, 2, tc, d), F32), pltpu.SemaphoreType.DMA((2,))],
        compiler_params=_cp(("arbitrary",)),
        name="combine",
    )(dest, dest, ys, x1, wts_t, mod3, final_g.reshape(1, d))


def _token_mixer(x2, mod3, norm1_g, w_in, ret_gn_g, cmp_pos_k, cmp_w1_k, cmp_w2_k,
                 cmp_pos_v, cmp_w1_v, cmp_w2_v, bsz, seq):
    n, d = x2.shape
    w_t = w_in.T
    w_gate_t = jnp.pad(w_t[PROJ_MAIN:], ((0, LANES - N_GATE_COLS), (0, 0))).astype(BF16)
    proj, gate_logits = _inproj(x2, mod3, norm1_g, w_t, w_gate_t, seq)

    o_ret = _retention(proj, ret_gn_g, bsz, seq)

    g_ = NSA_KV_GROUPS
    crow = seq // CMP_STRIDE

    pe_flat = lambda pe: pe.reshape(1, CMP_BLOCK * NSA_D)
    kc, vc = _compress(proj, pe_flat(cmp_pos_k), pe_flat(cmp_pos_v),
                       cmp_w1_k, cmp_w1_v, cmp_w2_k, cmp_w2_v, bsz, seq)

    o_nsa = _nsa(proj, kc, vc, gate_logits, bsz, seq)
    return o_ret, o_nsa


def _moe(h2, lt, x1, mod3, final_g, b_grp, b_exp, w_gate, w_up, w_down, seq, tm):
    n, d = h2.shape
    bias_col = jnp.zeros((LANES,), F32).at[:N_GROUPS].set(b_grp).at[N_GROUPS:N_GROUPS + N_EXPERTS].set(b_exp)
    bias_col = jnp.broadcast_to(bias_col[:, None], (LANES, LANES))
    ids, wts, cnt = _route(lt, bias_col)

    counts = cnt[:, 0]
    padded = (counts + tm - 1) // tm * tm
    pad_end = jnp.cumsum(padded)
    pad_start = (pad_end - padded).astype(I32)
    nb = (2 * n) // tm + N_EXPERTS
    n_used = (pad_end[-1] // tm).astype(I32).reshape(1)
    blk_start = jnp.arange(nb, dtype=I32) * tm
    blk_exp = jnp.minimum(jnp.sum((pad_end[None, :] <= blk_start[:, None]).astype(I32), axis=1),
                          N_EXPERTS - 1).astype(I32)
    last_exp = blk_exp[jnp.maximum(n_used[0] - 1, 0)]
    blk_exp = jnp.where(jnp.arange(nb) < n_used[0], blk_exp, last_exp)
    eid = jnp.arange(N_EXPERTS, dtype=I32)
    has = counts > 0
    exp_ord = jnp.sum((has[None, :] & (eid[None, :] < eid[:, None])).astype(I32), axis=1)
    exp_nxt = jnp.min(jnp.where(has[None, :] & (eid[None, :] > eid[:, None]), eid[None, :], N_EXPERTS), axis=1)
    exp_nxt = jnp.where(exp_nxt < N_EXPERTS, exp_nxt, -1).astype(I32)
    blk_ord = exp_ord[blk_exp]
    blk_nxt = exp_nxt[blk_exp]

    dest = _dest(ids, pad_start)
    xs = _dispatch(dest, pad_start, counts.astype(I32), h2, nb * tm, min(DISPATCH_TC, n), tm)
    ys = _experts(blk_exp, blk_ord, blk_nxt, n_used, xs, w_gate, w_up, w_down, tm)
    wts_t = jnp.pad(wts[:2].T, ((0, 0), (0, LANES - 2)))
    return _combine(dest, ys, x1, wts_t, mod3, final_g, seq, min(COMBINE_TC, n))


def kernel(x, c, w_ada, b_ada, norm1_g, norm2_g, final_g, w_in, ret_gn_g, cmp_pos_k, cmp_w1_k,
           cmp_w2_k, cmp_pos_v, cmp_w1_v, cmp_w2_v, w_out, w_grp, b_grp, w_exp, b_exp, w_gate,
           w_up, w_down):
    bsz, seq, d = x.shape
    assert w_ada.shape[0] == 1, "single-layer block"
    n = bsz * seq
    x2 = x.reshape(n, d)
    mod3 = _ada(c, w_ada[0], b_ada[0]).reshape(bsz, 6, d)

    o_ret, o_nsa = _token_mixer(x2, mod3, norm1_g[0], w_in[0], ret_gn_g[0], cmp_pos_k[0],
                                cmp_w1_k[0], cmp_w2_k[0], cmp_pos_v[0], cmp_w1_v[0], cmp_w2_v[0],
                                bsz, seq)

    w_route = jnp.concatenate([w_grp[0], w_exp[0]], axis=1)
    w_route = jnp.pad(w_route, ((0, 0), (0, LANES - w_route.shape[1]))).T.astype(BF16)
    x1, h2, lt = _outproj(o_ret, o_nsa, x2, mod3, norm2_g[0], w_out[0].astype(BF16), w_route, seq)

    out = _moe(h2, lt, x1, mod3, final_g, b_grp[0], b_exp[0], w_gate[0], w_up[0], w_down[0],
               seq, MOE_TM)
    return out.reshape(bsz, seq, d)
```

```python
import functools

import numpy as np
import jax
import jax.numpy as jnp
from jax import lax
from jax.experimental import pallas as pl
from jax.experimental.pallas import tpu as pltpu

F32 = jnp.float32
BF16 = jnp.bfloat16
I32 = jnp.int32

RET_HEADS = 4
RET_DK = 256
RET_DV = 256
RET_CHUNK = 128
NSA_HEADS = 8
NSA_KV_GROUPS = 2
NSA_HPG = NSA_HEADS // NSA_KV_GROUPS
NSA_D = 128
CMP_BLOCK = 32
CMP_STRIDE = 16
SEL_BLOCK = 64
SEL_COUNT = 16
WIN = 512
N_GROUPS = 8
EXP_PER_GROUP = 8
N_EXPERTS = N_GROUPS * EXP_PER_GROUP
ROPE_BASE = 10000.0
EPS = 1e-6
NEG = -1e30
FORCE_BONUS = 1e4

LANES = 128
SUBLANES = 8
MXU_DIM = 256
VMEM_LIMIT = 52 * 1024 * 1024

ADA_TN = 1024
INPROJ_TM = 2048
INPROJ_TN = 512
RET_CHUNKS = 4
NSA_TQ = 256
NSA_TK = 512
NSA_VT = MXU_DIM
OUTPROJ_TM = 512
ROUTE_TC = 2048
ROUTE_SUB = 512
MOE_TM = 256
DISPATCH_TC = 1024
COMBINE_TC = 256

_C_NQ = 32
_C_KC, _C_VC, _C_KS, _C_VS, _C_KW, _C_VW = 40, 42, 44, 46, 48, 50
_C_GATE = 52
PROJ_MAIN = _C_GATE * LANES
N_GATE_COLS = NSA_HEADS * 3


def _cp(sem, vmem=VMEM_LIMIT):
    return pltpu.CompilerParams(dimension_semantics=sem, vmem_limit_bytes=vmem)


def _silu(v):
    return v * jax.nn.sigmoid(v)


def _dot(a, b):
    return jnp.dot(a, b, preferred_element_type=F32)


def _dot_nt(a, b):
    return lax.dot_general(a, b, (((1,), (1,)), ((), ())), preferred_element_type=F32)


def _dot_tn(a, b):
    return lax.dot_general(a, b, (((0,), (0,)), ((), ())), preferred_element_type=F32)


def _ada_kernel(c_ref, w_ref, b_ref, o_ref):
    ca = _silu(c_ref[...]).astype(BF16)
    o_ref[...] = _dot(ca, w_ref[...].astype(BF16)) + b_ref[...]


def _ada(c, w, b):
    bsz, d = c.shape
    n = w.shape[1]
    tn = min(ADA_TN, n)
    return pl.pallas_call(
        _ada_kernel,
        grid=(n // tn,),
        in_specs=[pl.BlockSpec((bsz, d), lambda j: (0, 0)),
                  pl.BlockSpec((d, tn), lambda j: (0, j)),
                  pl.BlockSpec((1, tn), lambda j: (0, j))],
        out_specs=pl.BlockSpec((bsz, tn), lambda j: (0, j)),
        out_shape=jax.ShapeDtypeStruct((bsz, n), F32),
        compiler_params=_cp(("arbitrary",)),
        name="ada",
    )(c, w, b.reshape(1, n))


def _inproj_kernel(x_hbm, mod_ref, g_ref, wt_ref, wgt_ref, proj_ref, gate_ref, h_ref, x_buf, x_sem):
    i = pl.program_id(0)
    tm = x_buf.shape[0]

    def x_copy(blk):
        return pltpu.make_async_copy(x_hbm.at[pl.ds(pl.multiple_of(blk * tm, tm), tm)], x_buf, x_sem)

    @pl.when(pl.program_id(1) == 0)
    def _():
        @pl.when(i == 0)
        def _():
            x_copy(i).start()

        x_copy(i).wait()
        x = x_buf[...]
        y = x * lax.rsqrt(jnp.mean(x * x, axis=-1, keepdims=True) + EPS) * g_ref[...]
        h = (y * (1.0 + mod_ref[0, 1:2, :]) + mod_ref[0, 0:1, :]).astype(BF16)
        h_ref[...] = h
        gate_ref[...] = _dot_nt(h, wgt_ref[...])

        @pl.when(i + 1 < pl.num_programs(0))
        def _():
            x_copy(i + 1).start()

    proj_ref[...] = _dot_nt(h_ref[...], wt_ref[...].astype(BF16)).astype(BF16)


def _inproj(x2, mod3, g, w_t, wg_t, seq):
    n, d = x2.shape
    tm = min(INPROJ_TM, seq)
    tn = INPROJ_TN
    nj = PROJ_MAIN // tn
    return pl.pallas_call(
        _inproj_kernel,
        grid=(n // tm, nj),
        in_specs=[pl.BlockSpec(memory_space=pl.ANY),
                  pl.BlockSpec((1, 6, d), lambda i, j: ((i * tm) // seq, 0, 0)),
                  pl.BlockSpec((1, d), lambda i, j: (0, 0)),
                  pl.BlockSpec((tn, d), lambda i, j: (j, 0)),
                  pl.BlockSpec((LANES, d), lambda i, j: (0, 0))],
        out_specs=[pl.BlockSpec((tm, tn), lambda i, j: (i, j)),
                   pl.BlockSpec((tm, LANES), lambda i, j: (i, 0))],
        out_shape=[jax.ShapeDtypeStruct((n, PROJ_MAIN), BF16),
                   jax.ShapeDtypeStruct((n, LANES), F32)],
        scratch_shapes=[pltpu.VMEM((tm, d), BF16), pltpu.VMEM((tm, d), F32),
                        pltpu.SemaphoreType.DMA(())],
        compiler_params=_cp(("arbitrary", "arbitrary")),
        name="inproj",
    )(x2, mod3, g.reshape(1, d), w_t, wg_t)


def _retention_kernel(q_ref, k_ref, v_ref, g_ref, cos_ref, sin_ref, din_ref, zeta_ref,
                      qdec_ref, cdec_ref, gn_ref, o_ref, s_ref):
    @pl.when(pl.program_id(1) == 0)
    def _():
        s_ref[...] = jnp.zeros_like(s_ref)

    half = RET_DK // 2
    c = RET_CHUNK

    for sub in range(q_ref.shape[0] // c):
        rows = slice(sub * c, (sub + 1) * c)
        cos = cos_ref[rows, :]
        sin = sin_ref[rows, :]

        def rot(a):
            a1, a2 = a[:, :half], a[:, half:]
            return jnp.concatenate([a1 * cos - a2 * sin, a1 * sin + a2 * cos], axis=1)

        for h in range(RET_HEADS):
            qs = slice(h * RET_DK, (h + 1) * RET_DK)
            vs = slice(h * RET_DV, (h + 1) * RET_DV)
            q = rot(q_ref[rows, qs].astype(F32))
            k = rot(k_ref[rows, qs].astype(F32)) * (RET_DK ** -0.5)
            v = v_ref[rows, vs]
            qb = q.astype(BF16)
            kb = k.astype(BF16)
            s = _dot_nt(qb, kb) * din_ref[h]
            inner = _dot(s.astype(BF16), v)
            s_prev = s_ref[h]
            cross = _dot(qb, s_prev.astype(BF16)) * qdec_ref[h]
            kv = _dot_tn((k * zeta_ref[h]).astype(BF16), v)
            s_ref[h] = cdec_ref[h] * s_prev + kv
            o = inner + cross
            mu = jnp.mean(o, axis=-1, keepdims=True)
            oc = o - mu
            var = jnp.mean(oc * oc, axis=-1, keepdims=True)
            o = oc * lax.rsqrt(var + EPS) * gn_ref[:, vs]
            o_ref[rows, vs] = (o * _silu(g_ref[rows, vs].astype(F32))).astype(BF16)


def _retention(proj, gn_g, bsz, seq):
    n = proj.shape[0]
    c = RET_CHUNK
    nc = seq // c
    hw = RET_HEADS * RET_DK
    half = RET_DK // 2
    pos = jnp.arange(seq, dtype=F32)
    inv = ROPE_BASE ** (-jnp.arange(half, dtype=F32) / half)
    ang = pos[:, None] * inv[None, :]
    cos, sin = jnp.cos(ang), jnp.sin(ang)
    log_gamma = jnp.log1p(-jnp.exp2(-5.0 - jnp.arange(RET_HEADS, dtype=F32)))
    idx = jnp.arange(c, dtype=F32)
    rel = idx[:, None] - idx[None, :]
    decay_in = jnp.where(rel >= 0, jnp.exp(log_gamma[:, None, None] * jnp.maximum(rel, 0.0)), 0.0)
    zeta = jnp.exp(log_gamma[:, None] * (c - 1 - idx)[None, :])[:, :, None]
    q_decay = jnp.exp(log_gamma[:, None] * (idx + 1)[None, :])[:, :, None]
    chunk_decay = jnp.exp(log_gamma * c)[:, None, None]
    rb = min(RET_CHUNKS * c, seq)
    ns = seq // rb
    row = lambda b, t: (b * ns + t)
    return pl.pallas_call(
        _retention_kernel,
        grid=(bsz, ns),
        in_specs=[pl.BlockSpec((rb, hw), lambda b, t: (row(b, t), 0)),
                  pl.BlockSpec((rb, hw), lambda b, t: (row(b, t), 1)),
                  pl.BlockSpec((rb, hw), lambda b, t: (row(b, t), 2)),
                  pl.BlockSpec((rb, hw), lambda b, t: (row(b, t), 3)),
                  pl.BlockSpec((rb, half), lambda b, t: (t, 0)),
                  pl.BlockSpec((rb, half), lambda b, t: (t, 0)),
                  pl.BlockSpec((RET_HEADS, c, c), lambda b, t: (0, 0, 0)),
                  pl.BlockSpec((RET_HEADS, c, 1), lambda b, t: (0, 0, 0)),
                  pl.BlockSpec((RET_HEADS, c, 1), lambda b, t: (0, 0, 0)),
                  pl.BlockSpec((RET_HEADS, 1, 1), lambda b, t: (0, 0, 0)),
                  pl.BlockSpec((1, hw), lambda b, t: (0, 0))],
        out_specs=pl.BlockSpec((rb, hw), lambda b, t: (row(b, t), 0)),
        out_shape=jax.ShapeDtypeStruct((n, hw), BF16),
        scratch_shapes=[pltpu.VMEM((RET_HEADS, RET_DK, RET_DV), F32)],
        compiler_params=_cp(("arbitrary", "arbitrary")),
        name="retention",
    )(proj, proj, proj, proj, cos, sin, decay_in, zeta, q_decay, chunk_decay,
      gn_g.reshape(1, hw))


def _compress_kernel(ak_ref, av_ref, pek_ref, pev_ref, w1k_ref, w1v_ref, w2k_ref, w2v_ref,
                     kc_ref, vc_ref, a_scr):
    seq = ak_ref.shape[0]
    nblk = seq // CMP_STRIDE
    a_scr[seq:, :] = jnp.zeros((a_scr.shape[0] - seq, NSA_D), F32)

    def one(a_ref, pe_ref, w1_ref, w2_ref, o_ref):
        a_scr[:seq, :] = a_ref[...].astype(F32)
        flat = jnp.concatenate([a_scr[pl.ds(l, nblk, stride=CMP_STRIDE), :] for l in range(CMP_BLOCK)],
                               axis=1)
        pre = _dot((flat + pe_ref[...]).astype(BF16), w1_ref[...].astype(BF16))
        o_ref[...] = _dot(_silu(pre).astype(BF16), w2_ref[...].astype(BF16)).astype(BF16)

    one(ak_ref, pek_ref, w1k_ref, w2k_ref, kc_ref)
    one(av_ref, pev_ref, w1v_ref, w2v_ref, vc_ref)


def _compress(proj, pek, pev, w1k, w1v, w2k, w2v, bsz, seq):
    g_ = NSA_KV_GROUPS
    nblk = seq // CMP_STRIDE
    full = lambda a: pl.BlockSpec(a.shape, lambda i: (0,) * a.ndim)
    kblk = pl.BlockSpec((seq, NSA_D), lambda i: (i // g_, _C_KC + i % g_))
    vblk = pl.BlockSpec((seq, NSA_D), lambda i: (i // g_, _C_VC + i % g_))
    oblk = pl.BlockSpec((nblk, NSA_D), lambda i: (i, 0))
    return pl.pallas_call(
        _compress_kernel,
        grid=(bsz * g_,),
        in_specs=[kblk, vblk, full(pek), full(pev), full(w1k), full(w1v), full(w2k), full(w2v)],
        out_specs=[oblk, oblk],
        out_shape=[jax.ShapeDtypeStruct((bsz * g_ * nblk, NSA_D), BF16)] * 2,
        scratch_shapes=[pltpu.VMEM((seq + CMP_BLOCK, NSA_D), F32)],
        compiler_params=_cp(("arbitrary",)),
        name="compress",
    )(proj, proj, pek, pev, w1k, w1v, w2k, w2v)


def _nsa_kernel(q_ref, kc_ref, vc_ref, ks_ref, vs_ref, kw_ref, vw_ref, gate_ref, ovt_ref,
                o_ref, kaug_ref, vst_ref, vwt_ref, *, tq, tk, vt, seq, ncmp, wlen):
    crow = kc_ref.shape[0]
    i = pl.program_id(2)
    hg = NSA_HPG
    d = NSA_D
    r = hg * tq

    @pl.when(i == 0)
    def _():
        kaug_ref[:, :d] = ks_ref[...]
        blk = lax.broadcasted_iota(I32, (seq, LANES), 0) // SEL_BLOCK
        lane = lax.broadcasted_iota(I32, (seq, LANES), 1)
        kaug_ref[:, d:] = (blk == lane).astype(BF16)
        for c in range(seq // vt):
            vst_ref[c] = vs_ref[c * vt:(c + 1) * vt, :].astype(F32).T.astype(BF16)
            vwt_ref[c] = vw_ref[c * vt:(c + 1) * vt, :].astype(F32).T.astype(BF16)

    def pv_t(vt_ref, first_tile, p):
        out = None
        for c in range(p.shape[0] // vt):
            part = _dot(vt_ref[first_tile + c], p[c * vt:(c + 1) * vt, :])
            out = part if out is None else out + part
        return out

    q = q_ref[...]
    qh = [(q[:, h * d:(h + 1) * d].astype(F32) * (d ** -0.5)).astype(BF16) for h in range(hg)]
    qa = jnp.concatenate(qh, axis=0)
    t1 = i * tq + lax.broadcasted_iota(I32, (tq, 1), 0)
    tpos = jnp.concatenate([t1] * hg, axis=0)
    t_row = i * tq + lax.broadcasted_iota(I32, (1, tq), 1)
    tpos_row = jnp.concatenate([t_row] * hg, axis=1)

    w0 = pl.multiple_of(jnp.maximum(i * tq + tq - wlen, 0), vt)
    sw = _dot_nt(kw_ref[pl.ds(w0, wlen), :], qa)
    delta = tpos_row - (w0 + lax.broadcasted_iota(I32, (wlen, 1), 0))
    sw = jnp.where((delta >= 0) & (delta < WIN), sw, NEG)
    e_w = jnp.exp(sw - jnp.max(sw, axis=0, keepdims=True))
    o_win = (pv_t(vwt_ref, w0 // vt, e_w.astype(BF16)) / jnp.sum(e_w, axis=0, keepdims=True)).T

    sct = _dot_nt(kc_ref[...], qa)
    cidx = lax.broadcasted_iota(I32, (crow, 1), 0)
    cmask = (cidx * CMP_STRIDE + (CMP_BLOCK - 1) <= tpos_row) & (cidx < ncmp)
    sct = jnp.where(cmask, sct, NEG)
    e_c = jnp.where(cmask, jnp.exp(sct - jnp.max(sct, axis=0, keepdims=True)), 0.0)
    den_c = jnp.sum(e_c, axis=0, keepdims=True)
    p_t = jnp.where(den_c > 0.0, e_c / jnp.where(den_c > 0.0, den_c, 1.0), 0.0)
    o_cmp = _dot_tn(p_t.astype(BF16), vc_ref[...])

    psum_t = p_t[:, 0:tq]
    for h in range(1, hg):
        psum_t = psum_t + p_t[:, h * tq:(h + 1) * tq]
    nsel_blocks = seq // SEL_BLOCK
    nsb = ovt_ref.shape[0]
    imp_t = jnp.dot(ovt_ref[...], psum_t, preferred_element_type=F32,
                    precision=lax.Precision.HIGHEST)
    sidx = lax.broadcasted_iota(I32, (nsb, 1), 0)
    cur = t_row // SEL_BLOCK
    valid = sidx <= cur
    forced = (sidx == 0) | (sidx == cur) | (sidx == cur - 1)
    score = jnp.where(valid, imp_t + jnp.where(forced, FORCE_BONUS, 0.0), -1.0)
    rank = jnp.zeros((nsb, tq), F32)
    for s2 in range(nsel_blocks):
        row = score[s2:s2 + 1, :]
        beats = (row > score) | ((row == score) & (sidx > s2))
        rank = rank + beats.astype(F32)
    sel = valid & (rank < float(min(SEL_COUNT, nsel_blocks)))
    selb_t = jnp.where(sel, 0.0, NEG)
    selb_t = jnp.concatenate([selb_t, jnp.zeros((LANES - nsb, tq), F32)], axis=0)
    selb = selb_t.T.astype(BF16)
    q_aug = jnp.concatenate([jnp.concatenate([qh[h], selb], axis=1) for h in range(hg)], axis=0)

    n_full = (i * tq + 1) // tk
    kl = lax.broadcasted_iota(I32, (tk, 1), 0)

    def tile(j, carry, masked):
        m, l, acc = carry
        k0 = pl.multiple_of(j * tk, tk)
        s = _dot_nt(kaug_ref[pl.ds(k0, tk), :], q_aug)
        if masked:
            s = jnp.where(k0 + kl <= tpos_row, s, NEG)
        m_new = jnp.maximum(m, jnp.max(s, axis=0, keepdims=True))
        alpha = jnp.exp(m - m_new)
        p = jnp.exp(s - m_new)
        l = alpha * l + jnp.sum(p, axis=0, keepdims=True)
        acc = alpha * acc + pv_t(vst_ref, j * (tk // vt), p.astype(BF16))
        return m_new, l, acc

    m0 = jnp.full((1, r), NEG, F32)
    l0 = jnp.zeros((1, r), F32)
    a0 = jnp.zeros((d, r), F32)
    carry = lax.fori_loop(0, n_full, functools.partial(tile, masked=False), (m0, l0, a0))
    _, l_s, acc_s = tile(n_full, carry, True)
    o_slc = (acc_s / l_s).T

    gt = jax.nn.sigmoid(gate_ref[...])
    grp = pl.program_id(1)
    for g in range(1, NSA_KV_GROUPS):
        gt = jnp.where(grp == g, pltpu.roll(gt, LANES - g * 3 * hg, 1), gt)
    for h in range(hg):
        rows = slice(h * tq, (h + 1) * tq)
        o = (gt[:, 3 * h:3 * h + 1] * o_cmp[rows] + gt[:, 3 * h + 1:3 * h + 2] * o_slc[rows]
             + gt[:, 3 * h + 2:3 * h + 3] * o_win[rows])
        o_ref[:, h * d:(h + 1) * d] = o.astype(BF16)


def _nsa(proj, kc, vc, gates_g, bsz, seq):
    n = proj.shape[0]
    g_ = NSA_KV_GROUPS
    tq = NSA_TQ
    tk = min(NSA_TK, seq)
    nq = seq // tq
    ncmp = (seq - CMP_BLOCK) // CMP_STRIDE + 1
    wlen = min(WIN + tq, seq)
    nsel = seq // SEL_BLOCK
    crow = seq // CMP_STRIDE
    assert nsel <= LANES and crow <= LANES and tq <= tk and tk % tq == 0
    nsb = -(-nsel // 8) * 8
    ss = np.arange(nsb)[:, None] * SEL_BLOCK
    cs = np.arange(crow)[None, :] * CMP_STRIDE
    ov = ((cs < ss + SEL_BLOCK) & (cs + CMP_BLOCK > ss)
          & (np.arange(crow)[None, :] < ncmp) & (np.arange(nsb)[:, None] < nsel))
    ov = jnp.asarray(ov.astype(np.float32))
    kvspec = lambda c0: pl.BlockSpec((seq, NSA_D), lambda b, g, i: (b, c0 + g))
    vt = NSA_VT
    assert tk % vt == 0 and wlen % vt == 0 and tq % vt == 0 and seq % vt == 0
    kern = functools.partial(_nsa_kernel, tq=tq, tk=tk, vt=vt, seq=seq, ncmp=ncmp, wlen=wlen)
    return pl.pallas_call(
        kern,
        grid=(bsz, g_, nq),
        in_specs=[pl.BlockSpec((tq, NSA_HPG * NSA_D), lambda b, g, i: (b * nq + i, _C_NQ // NSA_HPG + g)),
                  pl.BlockSpec((crow, NSA_D), lambda b, g, i: (b * g_ + g, 0)),
                  pl.BlockSpec((crow, NSA_D), lambda b, g, i: (b * g_ + g, 0)),
                  kvspec(_C_KS), kvspec(_C_VS), kvspec(_C_KW), kvspec(_C_VW),
                  pl.BlockSpec((tq, LANES), lambda b, g, i: (b * nq + i, 0)),
                  pl.BlockSpec((nsb, crow), lambda b, g, i: (0, 0))],
        out_specs=pl.BlockSpec((tq, NSA_HPG * NSA_D), lambda b, g, i: (b * nq + i, g)),
        out_shape=jax.ShapeDtypeStruct((n, NSA_HEADS * NSA_D), BF16),
        scratch_shapes=[pltpu.VMEM((seq, 2 * NSA_D), BF16),
                        pltpu.VMEM((seq // vt, NSA_D, vt), BF16),
                        pltpu.VMEM((seq // vt, NSA_D, vt), BF16)],
        compiler_params=_cp(("arbitrary", "arbitrary", "arbitrary")),
        name="nsa",
    )(proj, kc, vc, proj, proj, proj, proj, gates_g, ov)


def _outproj_kernel(oret_ref, onsa_ref, x_ref, mod_ref, g_ref, w_ref, wr_ref,
                    x1_ref, h2_ref, lt_ref):
    hw = oret_ref.shape[1]
    mix = _dot(oret_ref[...], w_ref[:hw, :]) + _dot(onsa_ref[...], w_ref[hw:, :])
    x1 = x_ref[...] + mod_ref[0, 2:3, :] * mix
    x1_ref[...] = x1
    y = x1 * lax.rsqrt(jnp.mean(x1 * x1, axis=-1, keepdims=True) + EPS) * g_ref[...]
    h2 = y * (1.0 + mod_ref[0, 4:5, :]) + mod_ref[0, 3:4, :]
    h2_ref[...] = h2
    lt_ref[...] = _dot_nt(wr_ref[...], h2.astype(BF16))


def _outproj(o_ret, o_nsa, x2, mod3, g, w_bf, wr_bf, seq):
    n, d = x2.shape
    tm = min(OUTPROJ_TM, seq)
    hw = o_ret.shape[1]
    return pl.pallas_call(
        _outproj_kernel,
        grid=(n // tm,),
        in_specs=[pl.BlockSpec((tm, hw), lambda i: (i, 0)),
                  pl.BlockSpec((tm, o_nsa.shape[1]), lambda i: (i, 0)),
                  pl.BlockSpec((tm, d), lambda i: (i, 0)),
                  pl.BlockSpec((1, 6, d), lambda i: ((i * tm) // seq, 0, 0)),
                  pl.BlockSpec((1, d), lambda i: (0, 0)),
                  pl.BlockSpec(w_bf.shape, lambda i: (0, 0)),
                  pl.BlockSpec(wr_bf.shape, lambda i: (0, 0))],
        out_specs=[pl.BlockSpec((tm, d), lambda i: (i, 0)),
                   pl.BlockSpec((tm, d), lambda i: (i, 0)),
                   pl.BlockSpec((LANES, tm), lambda i: (0, i))],
        out_shape=[jax.ShapeDtypeStruct((n, d), F32),
                   jax.ShapeDtypeStruct((n, d), F32),
                   jax.ShapeDtypeStruct((LANES, n), F32)],
        compiler_params=_cp(("arbitrary",)),
        name="outproj",
    )(o_ret, o_nsa, x2, mod3, g.reshape(1, d), w_bf, wr_bf)


def _route_kernel(lt_ref, b_ref, tri_ref, ids_ref, wts_ref, cnt_ref, carry_ref, *, sub):
    @pl.when(pl.program_id(0) == 0)
    def _():
        carry_ref[...] = jnp.zeros_like(carry_ref)

    ng, ne = N_GROUPS, EXP_PER_GROUP
    l = lt_ref[...] + b_ref[:, 0:1]
    tc = l.shape[1]
    ridx = lax.broadcasted_iota(I32, (ng, tc), 0).astype(F32)

    def softmax0(v):
        e = jnp.exp(v - jnp.max(v, axis=0, keepdims=True))
        return e / jnp.sum(e, axis=0, keepdims=True)

    def top1(p):
        top = jnp.max(p, axis=0, keepdims=True)
        idx = jnp.min(jnp.where(p == top, ridx, float(ng)), axis=0, keepdims=True)
        return top, idx

    pg_top, grp = top1(softmax0(l[0:ng]))
    leg = jnp.zeros((ne, tc), F32)
    for g in range(ng):
        leg = jnp.where(grp == float(g), l[ng + g * ne:ng + (g + 1) * ne], leg)
    pe = softmax0(leg)
    p1, i1 = top1(pe)
    p2, i2 = top1(jnp.where(ridx == i1, -1.0, pe))
    den = p1 + p2
    w1 = pg_top * p1 / den
    w2 = pg_top * p2 / den
    e1 = grp * float(ne) + i1
    e2 = grp * float(ne) + i2

    eio = lax.broadcasted_iota(I32, (N_EXPERTS, sub), 0).astype(F32)
    r1 = []
    r2 = []
    carry = carry_ref[:, 0:1]
    for c in range(tc // sub):
        cs = slice(c * sub, (c + 1) * sub)
        oh1 = (eio == e1[:, cs]).astype(F32)
        oh2 = (eio == e2[:, cs]).astype(F32)
        oh = oh1 + oh2
        before = carry + _dot(oh.astype(BF16), tri_ref[...])
        r1.append(jnp.sum(oh1 * before, axis=0, keepdims=True))
        r2.append(jnp.sum(oh2 * before, axis=0, keepdims=True))
        carry = carry + jnp.sum(oh, axis=1, keepdims=True)
    carry_ref[...] = jnp.broadcast_to(carry, carry_ref.shape)
    cnt_ref[...] = jnp.broadcast_to(carry, cnt_ref.shape).astype(I32)
    r1 = jnp.concatenate(r1, axis=1)
    r2 = jnp.concatenate(r2, axis=1)
    zf = jnp.zeros((4, tc), F32)
    ids_ref[...] = jnp.concatenate([e1, e2, r1, r2, zf], axis=0).astype(I32)
    wts_ref[...] = jnp.concatenate([w1, w2, jnp.zeros((6, tc), F32)], axis=0)


def _route(lt, bias_col):
    n = lt.shape[1]
    tc = min(ROUTE_TC, n)
    sub = min(ROUTE_SUB, tc)
    tri =jnp.asarray(np.triu(np.ones((sub, sub), np.float32), 1), BF16)
    return pl.pallas_call(
        functools.partial(_route_kernel, sub=sub),
        grid=(n // tc,),
        in_specs=[pl.BlockSpec((LANES, tc), lambda i: (0, i)),
                  pl.BlockSpec((LANES, LANES), lambda i: (0, 0)),
                  pl.BlockSpec((sub, sub), lambda i: (0, 0))],
        out_specs=[pl.BlockSpec((8, tc), lambda i: (0, i)),
                   pl.BlockSpec((8, tc), lambda i: (0, i)),
                   pl.BlockSpec((N_EXPERTS, LANES), lambda i: (0, 0))],
        out_shape=[jax.ShapeDtypeStruct((8, n), I32),
                   jax.ShapeDtypeStruct((8, n), F32),
                   jax.ShapeDtypeStruct((N_EXPERTS, LANES), I32)],
        scratch_shapes=[pltpu.VMEM((N_EXPERTS, LANES), F32)],
        compiler_params=_cp(("arbitrary",)),
        name="route",
    )(lt, bias_col, tri)


def _dest_kernel(ids_ref, ps_ref, o_ref):
    ids = ids_ref[...].astype(F32)
    tc = ids.shape[1]
    eio = lax.broadcasted_iota(I32, (N_EXPERTS, tc), 0).astype(F32)
    ps = ps_ref[:, 0:1]
    rows = [jnp.sum(jnp.where(eio == ids[k:k + 1], ps, 0.0), axis=0, keepdims=True) + ids[2 + k:3 + k]
            for k in range(2)]
    o_ref[...] = jnp.concatenate(rows + [jnp.zeros((6, tc), F32)], axis=0).astype(I32)


def _dest(ids, pad_start):
    n = ids.shape[1]
    tc = min(ROUTE_TC, n)
    ps = jnp.broadcast_to(pad_start.astype(F32)[:, None], (N_EXPERTS, LANES))
    dest = pl.pallas_call(
        _dest_kernel,
        grid=(n // tc,),
        in_specs=[pl.BlockSpec((8, tc), lambda i: (0, i)),
                  pl.BlockSpec((N_EXPERTS, LANES), lambda i: (0, 0))],
        out_specs=pl.BlockSpec((8, tc), lambda i: (0, i)),
        out_shape=jax.ShapeDtypeStruct((8, n), I32),
        compiler_params=_cp(("arbitrary",)),
        name="dest",
    )(ids, ps)
    return dest[:2].T.reshape(2 * n)


def _dispatch_kernel(dst_ref, ps_ref, cnt_ref, h_ref, xs_ref, zbuf, sem, zsem, *, tcd, tm, zr):
    @pl.when(pl.program_id(0) == 0)
    def _():
        zbuf[...] = jnp.zeros_like(zbuf)
        sizes = [zr >> b for b in range(zr.bit_length()) if (zr >> b) >= SUBLANES]

        def zero_rows(row, size):
            return pltpu.make_async_copy(zbuf.at[pl.ds(0, size)],
                                         xs_ref.at[pl.ds(pl.multiple_of(row, SUBLANES), size)], zsem)

        def zero_row(row):
            return pltpu.make_async_copy(zbuf.at[pl.ds(0, 1)], xs_ref.at[pl.ds(row, 1)], zsem)

        def fill(e, wait):
            cnt = cnt_ref[e]
            cnt8 = lax.div(cnt + (SUBLANES - 1), SUBLANES) * SUBLANES
            gap = lax.rem(tm - lax.rem(cnt8, tm), tm)
            base = ps_ref[e]

            def single(c, _):
                cp = zero_row(base + cnt + c)
                cp.wait() if wait else cp.start()
                return 0

            lax.fori_loop(0, cnt8 - cnt, single, 0)
            row = base + cnt8
            for size in sizes:
                has = lax.rem(lax.div(gap, size), 2) == 1

                @pl.when(has)
                def _():
                    cp = zero_rows(row, size)
                    cp.wait() if wait else cp.start()

                row = row + jnp.where(has, size, 0)

        last = N_EXPERTS - 1
        used_rows = ps_ref[last] + lax.div(cnt_ref[last] + (tm - 1), tm) * tm
        n_tail = lax.div(xs_ref.shape[0] - used_rows, zr)

        def tail(c, wait):
            cp = zero_rows(used_rows + c * zr, zr)
            cp.wait() if wait else cp.start()

        for wait in (False, True):
            lax.fori_loop(0, N_EXPERTS, lambda e, _: (fill(e, wait), 0)[1], 0)
            lax.fori_loop(0, n_tail, lambda c, _: (tail(c, wait), 0)[1], 0)

    def issue(t, _):
        for k in range(2):
            dst = dst_ref[2 * t + k]
            pltpu.make_async_copy(h_ref.at[pl.ds(t, 1)], xs_ref.at[pl.ds(dst, 1)], sem).start()
        return 0

    lax.fori_loop(0, tcd, issue, 0, unroll=8)
    for k in range(2):
        pltpu.make_async_copy(h_ref, xs_ref.at[pl.ds(0, tcd)], sem).wait()


def _dispatch(dest, pad_start, counts, h2, rows, tcd, tm):
    n, d = h2.shape
    assert tm & (tm - 1) == 0 and tm >= 2 * SUBLANES
    zr = tm // 2
    return pl.pallas_call(
        functools.partial(_dispatch_kernel, tcd=tcd, tm=tm, zr=zr),
        grid=(n // tcd,),
        in_specs=[pl.BlockSpec((2 * tcd,), lambda i: (i,), memory_space=pltpu.SMEM),
                  pl.BlockSpec(memory_space=pltpu.SMEM),
                  pl.BlockSpec(memory_space=pltpu.SMEM),
                  pl.BlockSpec((tcd, d), lambda i: (i, 0))],
        out_specs=pl.BlockSpec(memory_space=pl.ANY),
        out_shape=jax.ShapeDtypeStruct((rows, d), h2.dtype),
        scratch_shapes=[pltpu.VMEM((zr, d), h2.dtype), pltpu.SemaphoreType.DMA(()),
                        pltpu.SemaphoreType.DMA(())],
        compiler_params=_cp(("arbitrary",)),
        name="dispatch",
    )(dest, pad_start, counts, h2)


def _experts_kernel(be_ref, ord_ref, nxt_ref, nu_ref, xs_ref, wg_hbm, wu_hbm, wd_hbm, ys_ref,
                    g_f, u_f, d_f, g_s, u_s, d_s, sem):
    i = pl.program_id(0)
    used = i < nu_ref[0]
    e = be_ref[i]
    fresh = (i == 0) | (e != be_ref[jnp.maximum(i - 1, 0)])
    slot = lax.rem(ord_ref[i], 2)

    def fetch(expert, s):
        return [pltpu.make_async_copy(w.at[expert], f.at[s], sem.at[s, k])
                for k, (w, f) in enumerate(((wg_hbm, g_f), (wu_hbm, u_f), (wd_hbm, d_f)))]

    @pl.when(used & (i == 0))
    def _():
        for c in fetch(e, slot):
            c.start()

    @pl.when(used & fresh)
    def _():
        for c in fetch(e, slot):
            c.wait()
        nxt = nxt_ref[i]

        @pl.when(nxt >= 0)
        def _():
            for c in fetch(nxt, 1 - slot):
                c.start()

        g_s[...] = g_f[slot].astype(BF16)
        u_s[...] = u_f[slot].astype(BF16)
        d_s[...] = d_f[slot].astype(BF16)

    @pl.when(used)
    def _():
        x = xs_ref[...].astype(BF16)
        a = _silu(_dot(x, g_s[...])) * _dot(x, u_s[...])
        ys_ref[...] = _dot(a.astype(BF16), d_s[...])

    @pl.when(jnp.logical_not(used))
    def _():
        ys_ref[...] = jnp.zeros_like(ys_ref)


def _experts(blk_exp, blk_ord, blk_nxt, n_used, xs, w_gate, w_up, w_down, tm):
    d = xs.shape[1]
    de = w_gate.shape[2]
    nb = blk_exp.shape[0]
    row = lambda i, be, od, nx, nu: (jnp.minimum(i, nu[0] - 1), 0)
    anyspec = pl.BlockSpec(memory_space=pl.ANY)
    grid_spec = pltpu.PrefetchScalarGridSpec(
        num_scalar_prefetch=4,
        grid=(nb,),
        in_specs=[pl.BlockSpec((tm, d), row), anyspec, anyspec, anyspec],
        out_specs=pl.BlockSpec((tm, d), lambda i, be, od, nx, nu: (i, 0)),
        scratch_shapes=[pltpu.VMEM((2, d, de), F32), pltpu.VMEM((2, d, de), F32),
                        pltpu.VMEM((2, de, d), F32),
                        pltpu.VMEM((d, de), BF16), pltpu.VMEM((d, de), BF16),
                        pltpu.VMEM((de, d), BF16),
                        pltpu.SemaphoreType.DMA((2, 3))],
    )
    return pl.pallas_call(
        _experts_kernel,
        grid_spec=grid_spec,
        out_shape=jax.ShapeDtypeStruct((nb * tm, d), F32),
        compiler_params=_cp(("arbitrary",)),
        name="experts",
    )(blk_exp, blk_ord, blk_nxt, n_used, xs, w_gate, w_up, w_down)


def _combine_kernel(ids_ref, idn_ref, ys_ref, x1_ref, wt_ref, mod_ref, g_ref, o_ref,
                    buf, sem, *, tc):
    i = pl.program_id(0)
    slot = lax.rem(i, 2)

    def gather(id_ref, s):
        def issue(t, _):
            for k in range(2):
                src = id_ref[2 * t + k]
                pltpu.make_async_copy(ys_ref.at[pl.ds(src, 1)], buf.at[s, k, pl.ds(t, 1)],
                                      sem.at[s]).start()
            return 0

        lax.fori_loop(0, tc, issue, 0, unroll=8)

    @pl.when(i == 0)
    def _():
        gather(ids_ref, 0)

    @pl.when(i + 1 < pl.num_programs(0))
    def _():
        gather(idn_ref, 1 - slot)

    for k in range(2):
        pltpu.make_async_copy(ys_ref.at[pl.ds(0, tc)], buf.at[slot, k], sem.at[slot]).wait()

    moe = buf[slot, 0] * wt_ref[:, 0:1] + buf[slot, 1] * wt_ref[:, 1:2]
    x2 = x1_ref[...] + mod_ref[0, 5:6, :] * moe
    o_ref[...] = x2 * lax.rsqrt(jnp.mean(x2 * x2, axis=-1, keepdims=True) + EPS) * g_ref[...]


def _combine(dest, ys, x1, wts_t, mod3, final_g, seq, tc):
    n, d = x1.shape
    last = n // tc - 1
    return pl.pallas_call(
        functools.partial(_combine_kernel, tc=tc),
        grid=(n // tc,),
        in_specs=[pl.BlockSpec((2 * tc,), lambda i: (i,), memory_space=pltpu.SMEM),
                  pl.BlockSpec((2 * tc,), lambda i: (jnp.minimum(i + 1, last),),
                               memory_space=pltpu.SMEM),
                  pl.BlockSpec(memory_space=pl.ANY),
                  pl.BlockSpec((tc, d), lambda i: (i, 0)),
                  pl.BlockSpec((tc, LANES), lambda i: (i, 0)),
                  pl.BlockSpec((1, 6, d), lambda i: ((i * tc) // seq, 0, 0)),
                  pl.BlockSpec((1, d), lambda i: (0, 0))],
        out_specs=pl.BlockSpec((tc, d), lambda i: (i, 0)),
        out_shape=jax.ShapeDtypeStruct((n, d), F32),
        scratch_shapes=[pltpu.VMEM((2, 2, tc, d), F32), pltpu.SemaphoreType.DMA((2,))],
        compiler_params=_cp(("arbitrary",)),
        name="combine",
    )(dest, dest, ys, x1, wts_t, mod3, final_g.reshape(1, d))


def _token_mixer(x2, mod3, norm1_g, w_in, ret_gn_g, cmp_pos_k, cmp_w1_k, cmp_w2_k,
                 cmp_pos_v, cmp_w1_v, cmp_w2_v, bsz, seq):
    n, d = x2.shape
    w_t = w_in.T
    w_gate_t = jnp.pad(w_t[PROJ_MAIN:], ((0, LANES - N_GATE_COLS), (0, 0))).astype(BF16)
    proj, gate_logits = _inproj(x2, mod3, norm1_g, w_t, w_gate_t, seq)

    o_ret = _retention(proj, ret_gn_g, bsz, seq)

    g_ = NSA_KV_GROUPS
    crow = seq // CMP_STRIDE

    pe_flat = lambda pe: pe.reshape(1, CMP_BLOCK * NSA_D)
    kc, vc = _compress(proj, pe_flat(cmp_pos_k), pe_flat(cmp_pos_v),
                       cmp_w1_k, cmp_w1_v, cmp_w2_k, cmp_w2_v, bsz, seq)

    o_nsa = _nsa(proj, kc, vc, gate_logits, bsz, seq)
    return o_ret, o_nsa


def _moe(h2, lt, x1, mod3, final_g, b_grp, b_exp, w_gate, w_up, w_down, seq, tm):
    n, d = h2.shape
    bias_col = jnp.zeros((LANES,), F32).at[:N_GROUPS].set(b_grp).at[N_GROUPS:N_GROUPS + N_EXPERTS].set(b_exp)
    bias_col = jnp.broadcast_to(bias_col[:, None], (LANES, LANES))
    ids, wts, cnt = _route(lt, bias_col)

    counts = cnt[:, 0]
    padded = (counts + tm - 1) // tm * tm
    pad_end = jnp.cumsum(padded)
    pad_start = (pad_end - padded).astype(I32)
    nb = (2 * n) // tm + N_EXPERTS
    n_used = (pad_end[-1] // tm).astype(I32).reshape(1)
    blk_start = jnp.arange(nb, dtype=I32) * tm
    blk_exp = jnp.minimum(jnp.sum((pad_end[None, :] <= blk_start[:, None]).astype(I32), axis=1),
                          N_EXPERTS - 1).astype(I32)
    last_exp = blk_exp[jnp.maximum(n_used[0] - 1, 0)]
    blk_exp = jnp.where(jnp.arange(nb) < n_used[0], blk_exp, last_exp)
    eid = jnp.arange(N_EXPERTS, dtype=I32)
    has = counts > 0
    exp_ord = jnp.sum((has[None, :] & (eid[None, :] < eid[:, None])).astype(I32), axis=1)
    exp_nxt = jnp.min(jnp.where(has[None, :] & (eid[None, :] > eid[:, None]), eid[None, :], N_EXPERTS), axis=1)
    exp_nxt = jnp.where(exp_nxt < N_EXPERTS, exp_nxt, -1).astype(I32)
    blk_ord = exp_ord[blk_exp]
    blk_nxt = exp_nxt[blk_exp]

    dest = _dest(ids, pad_start)
    xs = _dispatch(dest, pad_start, counts.astype(I32), h2, nb * tm, min(DISPATCH_TC, n), tm)
    ys = _experts(blk_exp, blk_ord, blk_nxt, n_used, xs, w_gate, w_up, w_down, tm)
    wts_t = jnp.pad(wts[:2].T, ((0, 0), (0, LANES - 2)))
    return _combine(dest, ys, x1, wts_t, mod3, final_g, seq, min(COMBINE_TC, n))


def kernel(x, c, w_ada, b_ada, norm1_g, norm2_g, final_g, w_in, ret_gn_g, cmp_pos_k, cmp_w1_k,
           cmp_w2_k, cmp_pos_v, cmp_w1_v, cmp_w2_v, w_out, w_grp, b_grp, w_exp, b_exp, w_gate,
           w_up, w_down):
    bsz, seq, d = x.shape
    assert w_ada.shape[0] == 1, "single-layer block"
    n = bsz * seq
    x2 = x.reshape(n, d)
    mod3 = _ada(c, w_ada[0], b_ada[0]).reshape(bsz, 6, d)

    o_ret, o_nsa = _token_mixer(x2, mod3, norm1_g[0], w_in[0], ret_gn_g[0], cmp_pos_k[0],
                                cmp_w1_k[0], cmp_w2_k[0], cmp_pos_v[0], cmp_w1_v[0], cmp_w2_v[0],
                                bsz, seq)

    w_route = jnp.concatenate([w_grp[0], w_exp[0]], axis=1)
    w_route = jnp.pad(w_route, ((0, 0), (0, LANES - w_route.shape[1]))).T.astype(BF16)
    x1, h2, lt = _outproj(o_ret, o_nsa, x2, mod3, norm2_g[0], w_out[0].astype(BF16), w_route, seq)

    out = _moe(h2, lt, x1, mod3, final_g, b_grp[0], b_exp[0], w_gate[0], w_up[0], w_down[0],
               seq, MOE_TM)
    return out.reshape(bsz, seq, d)
```

```python
import functools

import numpy as np
import jax
import jax.numpy as jnp
from jax import lax
from jax.experimental import pallas as pl
from jax.experimental.pallas import tpu as pltpu

F32 = jnp.float32
BF16 = jnp.bfloat16
I32 = jnp.int32

RET_HEADS = 4
RET_DK = 256
RET_DV = 256
RET_CHUNK = 128
NSA_HEADS = 8
NSA_KV_GROUPS = 2
NSA_HPG = NSA_HEADS // NSA_KV_GROUPS
NSA_D = 128
CMP_BLOCK = 32
CMP_STRIDE = 16
SEL_BLOCK = 64
SEL_COUNT = 16
WIN = 512
N_GROUPS = 8
EXP_PER_GROUP = 8
N_EXPERTS = N_GROUPS * EXP_PER_GROUP
ROPE_BASE = 10000.0
EPS = 1e-6
NEG = -1e30
FORCE_BONUS = 1e4

LANES = 128
SUBLANES = 8
MXU_DIM = 256
VMEM_LIMIT = 52 * 1024 * 1024

ADA_TN = 1024
INPROJ_TM = 2048
INPROJ_TN = 512
RET_CHUNKS = 4
NSA_TQ = 256
NSA_TK = 512
NSA_VT = MXU_DIM
OUTPROJ_TM = 512
ROUTE_TC = 2048
ROUTE_SUB = 512
MOE_TM = 256
DISPATCH_TC = 1024
COMBINE_TC = 256

_C_NQ = 32
_C_KC, _C_VC, _C_KS, _C_VS, _C_KW, _C_VW = 40, 42, 44, 46, 48, 50
_C_GATE = 52
PROJ_MAIN = _C_GATE * LANES
N_GATE_COLS = NSA_HEADS * 3


def _cp(sem, vmem=VMEM_LIMIT):
    return pltpu.CompilerParams(dimension_semantics=sem, vmem_limit_bytes=vmem)


def _silu(v):
    return v * jax.nn.sigmoid(v)


def _dot(a, b):
    return jnp.dot(a, b, preferred_element_type=F32)


def _dot_nt(a, b):
    return lax.dot_general(a, b, (((1,), (1,)), ((), ())), preferred_element_type=F32)


def _dot_tn(a, b):
    return lax.dot_general(a, b, (((0,), (0,)), ((), ())), preferred_element_type=F32)


def _ada_kernel(c_ref, w_ref, b_ref, o_ref):
    ca = _silu(c_ref[...]).astype(BF16)
    o_ref[...] = _dot(ca, w_ref[...].astype(BF16)) + b_ref[...]


def _ada(c, w, b):
    bsz, d = c.shape
    n = w.shape[1]
    tn = min(ADA_TN, n)
    return pl.pallas_call(
        _ada_kernel,
        grid=(n // tn,),
        in_specs=[pl.BlockSpec((bsz, d), lambda j: (0, 0)),
                  pl.BlockSpec((d, tn), lambda j: (0, j)),
                  pl.BlockSpec((1, tn), lambda j: (0, j))],
        out_specs=pl.BlockSpec((bsz, tn), lambda j: (0, j)),
        out_shape=jax.ShapeDtypeStruct((bsz, n), F32),
        compiler_params=_cp(("arbitrary",)),
        name="ada",
    )(c, w, b.reshape(1, n))


def _inproj_kernel(x_hbm, mod_ref, g_ref, wt_ref, wgt_ref, proj_ref, gate_ref, h_ref, x_buf, x_sem):
    i = pl.program_id(0)
    tm = x_buf.shape[0]

    def x_copy(blk):
        return pltpu.make_async_copy(x_hbm.at[pl.ds(pl.multiple_of(blk * tm, tm), tm)], x_buf, x_sem)

    @pl.when(pl.program_id(1) == 0)
    def _():
        @pl.when(i == 0)
        def _():
            x_copy(i).start()

        x_copy(i).wait()
        x = x_buf[...]
        y = x * lax.rsqrt(jnp.mean(x * x, axis=-1, keepdims=True) + EPS) * g_ref[...]
        h = (y * (1.0 + mod_ref[0, 1:2, :]) + mod_ref[0, 0:1, :]).astype(BF16)
        h_ref[...] = h
        gate_ref[...] = _dot_nt(h, wgt_ref[...])

        @pl.when(i + 1 < pl.num_programs(0))
        def _():
            x_copy(i + 1).start(priority=1)

    proj_ref[...] = _dot_nt(h_ref[...], wt_ref[...].astype(BF16)).astype(BF16)


def _inproj(x2, mod3, g, w_t, wg_t, seq):
    n, d = x2.shape
    tm = min(INPROJ_TM, seq)
    tn = INPROJ_TN
    nj = PROJ_MAIN // tn
    return pl.pallas_call(
        _inproj_kernel,
        grid=(n // tm, nj),
        in_specs=[pl.BlockSpec(memory_space=pl.ANY),
                  pl.BlockSpec((1, 6, d), lambda i, j: ((i * tm) // seq, 0, 0)),
                  pl.BlockSpec((1, d), lambda i, j: (0, 0)),
                  pl.BlockSpec((tn, d), lambda i, j: (j, 0)),
                  pl.BlockSpec((LANES, d), lambda i, j: (0, 0))],
        out_specs=[pl.BlockSpec((tm, tn), lambda i, j: (i, j)),
                   pl.BlockSpec((tm, LANES), lambda i, j: (i, 0))],
        out_shape=[jax.ShapeDtypeStruct((n, PROJ_MAIN), BF16),
                   jax.ShapeDtypeStruct((n, LANES), F32)],
        scratch_shapes=[pltpu.VMEM((tm, d), BF16), pltpu.VMEM((tm, d), F32),
                        pltpu.SemaphoreType.DMA(())],
        compiler_params=_cp(("arbitrary", "arbitrary")),
        name="inproj",
    )(x2, mod3, g.reshape(1, d), w_t, wg_t)


def _retention_kernel(q_ref, k_ref, v_ref, g_ref, cos_ref, sin_ref, din_ref, zeta_ref,
                      qdec_ref, cdec_ref, gn_ref, o_ref, s_ref):
    @pl.when(pl.program_id(1) == 0)
    def _():
        s_ref[...] = jnp.zeros_like(s_ref)

    half = RET_DK // 2
    c = RET_CHUNK

    for sub in range(q_ref.shape[0] // c):
        rows = slice(sub * c, (sub + 1) * c)
        cos = cos_ref[rows, :]
        sin = sin_ref[rows, :]

        def rot(a):
            a1, a2 = a[:, :half], a[:, half:]
            return jnp.concatenate([a1 * cos - a2 * sin, a1 * sin + a2 * cos], axis=1)

        for h in range(RET_HEADS):
            qs = slice(h * RET_DK, (h + 1) * RET_DK)
            vs = slice(h * RET_DV, (h + 1) * RET_DV)
            q = rot(q_ref[rows, qs].astype(F32))
            k = rot(k_ref[rows, qs].astype(F32)) * (RET_DK ** -0.5)
            v = v_ref[rows, vs]
            qb = q.astype(BF16)
            kb = k.astype(BF16)
            s = _dot_nt(qb, kb) * din_ref[h]
            inner = _dot(s.astype(BF16), v)
            s_prev = s_ref[h]
            cross = _dot(qb, s_prev.astype(BF16)) * qdec_ref[h]
            kv = _dot_tn((k * zeta_ref[h]).astype(BF16), v)
            s_ref[h] = cdec_ref[h] * s_prev + kv
            o = inner + cross
            mu = jnp.mean(o, axis=-1, keepdims=True)
            oc = o - mu
            var = jnp.mean(oc * oc, axis=-1, keepdims=True)
            o = oc * lax.rsqrt(var + EPS) * gn_ref[:, vs]
            o_ref[rows, vs] = (o * _silu(g_ref[rows, vs].astype(F32))).astype(BF16)


def _retention(proj, gn_g, bsz, seq):
    n = proj.shape[0]
    c = RET_CHUNK
    nc = seq // c
    hw = RET_HEADS * RET_DK
    half = RET_DK // 2
    pos = jnp.arange(seq, dtype=F32)
    inv = ROPE_BASE ** (-jnp.arange(half, dtype=F32) / half)
    ang = pos[:, None] * inv[None, :]
    cos, sin = jnp.cos(ang), jnp.sin(ang)
    log_gamma = jnp.log1p(-jnp.exp2(-5.0 - jnp.arange(RET_HEADS, dtype=F32)))
    idx = jnp.arange(c, dtype=F32)
    rel = idx[:, None] - idx[None, :]
    decay_in = jnp.where(rel >= 0, jnp.exp(log_gamma[:, None, None] * jnp.maximum(rel, 0.0)), 0.0)
    zeta = jnp.exp(log_gamma[:, None] * (c - 1 - idx)[None, :])[:, :, None]
    q_decay = jnp.exp(log_gamma[:, None] * (idx + 1)[None, :])[:, :, None]
    chunk_decay = jnp.exp(log_gamma * c)[:, None, None]
    rb = min(RET_CHUNKS * c, seq)
    ns = seq // rb
    row = lambda b, t: (b * ns + t)
    return pl.pallas_call(
        _retention_kernel,
        grid=(bsz, ns),
        in_specs=[pl.BlockSpec((rb, hw), lambda b, t: (row(b, t), 0)),
                  pl.BlockSpec((rb, hw), lambda b, t: (row(b, t), 1)),
                  pl.BlockSpec((rb, hw), lambda b, t: (row(b, t), 2)),
                  pl.BlockSpec((rb, hw), lambda b, t: (row(b, t), 3)),
                  pl.BlockSpec((rb, half), lambda b, t: (t, 0)),
                  pl.BlockSpec((rb, half), lambda b, t: (t, 0)),
                  pl.BlockSpec((RET_HEADS, c, c), lambda b, t: (0, 0, 0)),
                  pl.BlockSpec((RET_HEADS, c, 1), lambda b, t: (0, 0, 0)),
                  pl.BlockSpec((RET_HEADS, c, 1), lambda b, t: (0, 0, 0)),
                  pl.BlockSpec((RET_HEADS, 1, 1), lambda b, t: (0, 0, 0)),
                  pl.BlockSpec((1, hw), lambda b, t: (0, 0))],
        out_specs=pl.BlockSpec((rb, hw), lambda b, t: (row(b, t), 0)),
        out_shape=jax.ShapeDtypeStruct((n, hw), BF16),
        scratch_shapes=[pltpu.VMEM((RET_HEADS, RET_DK, RET_DV), F32)],
        compiler_params=_cp(("arbitrary", "arbitrary")),
        name="retention",
    )(proj, proj, proj, proj, cos, sin, decay_in, zeta, q_decay, chunk_decay,
      gn_g.reshape(1, hw))


def _compress_kernel(ak_ref, av_ref, pek_ref, pev_ref, w1k_ref, w1v_ref, w2k_ref, w2v_ref,
                     kc_ref, vc_ref, a_scr):
    seq = ak_ref.shape[0]
    nblk = seq // CMP_STRIDE
    a_scr[seq:, :] = jnp.zeros((a_scr.shape[0] - seq, NSA_D), F32)

    def one(a_ref, pe_ref, w1_ref, w2_ref, o_ref):
        a_scr[:seq, :] = a_ref[...].astype(F32)
        flat = jnp.concatenate([a_scr[pl.ds(l, nblk, stride=CMP_STRIDE), :] for l in range(CMP_BLOCK)],
                               axis=1)
        pre = _dot((flat + pe_ref[...]).astype(BF16), w1_ref[...].astype(BF16))
        o_ref[...] = _dot(_silu(pre).astype(BF16), w2_ref[...].astype(BF16)).astype(BF16)

    one(ak_ref, pek_ref, w1k_ref, w2k_ref, kc_ref)
    one(av_ref, pev_ref, w1v_ref, w2v_ref, vc_ref)


def _compress(proj, pek, pev, w1k, w1v, w2k, w2v, bsz, seq):
    g_ = NSA_KV_GROUPS
    nblk = seq // CMP_STRIDE
    full = lambda a: pl.BlockSpec(a.shape, lambda i: (0,) * a.ndim)
    kblk = pl.BlockSpec((seq, NSA_D), lambda i: (i // g_, _C_KC + i % g_))
    vblk = pl.BlockSpec((seq, NSA_D), lambda i: (i // g_, _C_VC + i % g_))
    oblk = pl.BlockSpec((nblk, NSA_D), lambda i: (i, 0))
    return pl.pallas_call(
        _compress_kernel,
        grid=(bsz * g_,),
        in_specs=[kblk, vblk, full(pek), full(pev), full(w1k), full(w1v), full(w2k), full(w2v)],
        out_specs=[oblk, oblk],
        out_shape=[jax.ShapeDtypeStruct((bsz * g_ * nblk, NSA_D), BF16)] * 2,
        scratch_shapes=[pltpu.VMEM((seq + CMP_BLOCK, NSA_D), F32)],
        compiler_params=_cp(("arbitrary",)),
        name="compress",
    )(proj, proj, pek, pev, w1k, w1v, w2k, w2v)


def _nsa_kernel(q_ref, kc_ref, vc_ref, ks_ref, vs_ref, kw_ref, vw_ref, gate_ref, ovt_ref,
                o_ref, kaug_ref, vst_ref, vwt_ref, *, tq, tk, vt, seq, ncmp, wlen):
    crow = kc_ref.shape[0]
    i = pl.program_id(2)
    hg = NSA_HPG
    d = NSA_D
    r = hg * tq

    @pl.when(i == 0)
    def _():
        kaug_ref[:, :d] = ks_ref[...]
        blk = lax.broadcasted_iota(I32, (seq, LANES), 0) // SEL_BLOCK
        lane = lax.broadcasted_iota(I32, (seq, LANES), 1)
        kaug_ref[:, d:] = (blk == lane).astype(BF16)
        for c in range(seq // vt):
            vst_ref[c] = vs_ref[c * vt:(c + 1) * vt, :].astype(F32).T.astype(BF16)
            vwt_ref[c] = vw_ref[c * vt:(c + 1) * vt, :].astype(F32).T.astype(BF16)

    def pv_t(vt_ref, first_tile, p):
        out = None
        for c in range(p.shape[0] // vt):
            part = _dot(vt_ref[first_tile + c], p[c * vt:(c + 1) * vt, :])
            out = part if out is None else out + part
        return out

    q = q_ref[...]
    qh = [(q[:, h * d:(h + 1) * d].astype(F32) * (d ** -0.5)).astype(BF16) for h in range(hg)]
    qa = jnp.concatenate(qh, axis=0)
    t1 = i * tq + lax.broadcasted_iota(I32, (tq, 1), 0)
    tpos = jnp.concatenate([t1] * hg, axis=0)
    t_row = i * tq + lax.broadcasted_iota(I32, (1, tq), 1)
    tpos_row = jnp.concatenate([t_row] * hg, axis=1)

    w0 = pl.multiple_of(jnp.maximum(i * tq + tq - wlen, 0), vt)
    sw = _dot_nt(kw_ref[pl.ds(w0, wlen), :], qa)
    delta = tpos_row - (w0 + lax.broadcasted_iota(I32, (wlen, 1), 0))
    sw = jnp.where((delta >= 0) & (delta < WIN), sw, NEG)
    e_w = jnp.exp(sw - jnp.max(sw, axis=0, keepdims=True))
    o_win = (pv_t(vwt_ref, w0 // vt, e_w.astype(BF16)) / jnp.sum(e_w, axis=0, keepdims=True)).T

    sct = _dot_nt(kc_ref[...], qa)
    cidx = lax.broadcasted_iota(I32, (crow, 1), 0)
    cmask = (cidx * CMP_STRIDE + (CMP_BLOCK - 1) <= tpos_row) & (cidx < ncmp)
    sct = jnp.where(cmask, sct, NEG)
    e_c = jnp.where(cmask, jnp.exp(sct - jnp.max(sct, axis=0, keepdims=True)), 0.0)
    den_c = jnp.sum(e_c, axis=0, keepdims=True)
    p_t = jnp.where(den_c > 0.0, e_c / jnp.where(den_c > 0.0, den_c, 1.0), 0.0)
    o_cmp = _dot_tn(p_t.astype(BF16), vc_ref[...])

    psum_t = p_t[:, 0:tq]
    for h in range(1, hg):
        psum_t = psum_t + p_t[:, h * tq:(h + 1) * tq]
    nsel_blocks = seq // SEL_BLOCK
    nsb = ovt_ref.shape[0]
    imp_t = jnp.dot(ovt_ref[...], psum_t, preferred_element_type=F32,
                    precision=lax.Precision.HIGHEST)
    sidx = lax.broadcasted_iota(I32, (nsb, 1), 0)
    cur = t_row // SEL_BLOCK
    valid = sidx <= cur
    forced = (sidx == 0) | (sidx == cur) | (sidx == cur - 1)
    score = jnp.where(valid, imp_t + jnp.where(forced, FORCE_BONUS, 0.0), -1.0)
    rank = jnp.zeros((nsb, tq), F32)
    for s2 in range(nsel_blocks):
        row = score[s2:s2 + 1, :]
        beats = (row > score) | ((row == score) & (sidx > s2))
        rank = rank + beats.astype(F32)
    sel = valid & (rank < float(min(SEL_COUNT, nsel_blocks)))
    selb_t = jnp.where(sel, 0.0, NEG)
    selb_t = jnp.concatenate([selb_t, jnp.zeros((LANES - nsb, tq), F32)], axis=0)
    selb = selb_t.T.astype(BF16)
    q_aug = jnp.concatenate([jnp.concatenate([qh[h], selb], axis=1) for h in range(hg)], axis=0)

    n_full = (i * tq + 1) // tk
    kl = lax.broadcasted_iota(I32, (tk, 1), 0)

    def tile(j, carry, masked):
        m, l, acc = carry
        k0 = pl.multiple_of(j * tk, tk)
        s = _dot_nt(kaug_ref[pl.ds(k0, tk), :], q_aug)
        if masked:
            s = jnp.where(k0 + kl <= tpos_row, s, NEG)
        m_new = jnp.maximum(m, jnp.max(s, axis=0, keepdims=True))
        alpha = jnp.exp(m - m_new)
        p = jnp.exp(s - m_new)
        l = alpha * l + jnp.sum(p, axis=0, keepdims=True)
        acc = alpha * acc + pv_t(vst_ref, j * (tk // vt), p.astype(BF16))
        return m_new, l, acc

    m0 = jnp.full((1, r), NEG, F32)
    l0 = jnp.zeros((1, r), F32)
    a0 = jnp.zeros((d, r), F32)
    carry = lax.fori_loop(0, n_full, functools.partial(tile, masked=False), (m0, l0, a0))
    _, l_s, acc_s = tile(n_full, carry, True)
    o_slc = (acc_s / l_s).T

    gt = jax.nn.sigmoid(gate_ref[...])
    grp = pl.program_id(1)
    for g in range(1, NSA_KV_GROUPS):
        gt = jnp.where(grp == g, pltpu.roll(gt, LANES - g * 3 * hg, 1), gt)
    for h in range(hg):
        rows = slice(h * tq, (h + 1) * tq)
        o = (gt[:, 3 * h:3 * h + 1] * o_cmp[rows] + gt[:, 3 * h + 1:3 * h + 2] * o_slc[rows]
             + gt[:, 3 * h + 2:3 * h + 3] * o_win[rows])
        o_ref[:, h * d:(h + 1) * d] = o.astype(BF16)


def _nsa(proj, kc, vc, gates_g, bsz, seq):
    n = proj.shape[0]
    g_ = NSA_KV_GROUPS
    tq = NSA_TQ
    tk = min(NSA_TK, seq)
    nq = seq // tq
    ncmp = (seq - CMP_BLOCK) // CMP_STRIDE + 1
    wlen = min(WIN + tq, seq)
    nsel = seq // SEL_BLOCK
    crow = seq // CMP_STRIDE
    assert nsel <= LANES and crow <= LANES and tq <= tk and tk % tq == 0
    nsb = -(-nsel // 8) * 8
    ss = np.arange(nsb)[:, None] * SEL_BLOCK
    cs = np.arange(crow)[None, :] * CMP_STRIDE
    ov = ((cs < ss + SEL_BLOCK) & (cs + CMP_BLOCK > ss)
          & (np.arange(crow)[None, :] < ncmp) & (np.arange(nsb)[:, None] < nsel))
    ov = jnp.asarray(ov.astype(np.float32))
    kvspec = lambda c0: pl.BlockSpec((seq, NSA_D), lambda b, g, i: (b, c0 + g))
    vt = NSA_VT
    assert tk % vt == 0 and wlen % vt == 0 and tq % vt == 0 and seq % vt == 0
    kern = functools.partial(_nsa_kernel, tq=tq, tk=tk, vt=vt, seq=seq, ncmp=ncmp, wlen=wlen)
    return pl.pallas_call(
        kern,
        grid=(bsz, g_, nq),
        in_specs=[pl.BlockSpec((tq, NSA_HPG * NSA_D), lambda b, g, i: (b * nq + i, _C_NQ // NSA_HPG + g)),
                  pl.BlockSpec((crow, NSA_D), lambda b, g, i: (b * g_ + g, 0)),
                  pl.BlockSpec((crow, NSA_D), lambda b, g, i: (b * g_ + g, 0)),
                  kvspec(_C_KS), kvspec(_C_VS), kvspec(_C_KW), kvspec(_C_VW),
                  pl.BlockSpec((tq, LANES), lambda b, g, i: (b * nq + i, 0)),
                  pl.BlockSpec((nsb, crow), lambda b, g, i: (0, 0))],
        out_specs=pl.BlockSpec((tq, NSA_HPG * NSA_D), lambda b, g, i: (b * nq + i, g)),
        out_shape=jax.ShapeDtypeStruct((n, NSA_HEADS * NSA_D), BF16),
        scratch_shapes=[pltpu.VMEM((seq, 2 * NSA_D), BF16),
                        pltpu.VMEM((seq // vt, NSA_D, vt), BF16),
                        pltpu.VMEM((seq // vt, NSA_D, vt), BF16)],
        compiler_params=_cp(("arbitrary", "arbitrary", "arbitrary")),
        name="nsa",
    )(proj, kc, vc, proj, proj, proj, proj, gates_g, ov)


def _outproj_kernel(oret_ref, onsa_ref, x_ref, mod_ref, g_ref, w_ref, wr_ref,
                    x1_ref, h2_ref, lt_ref):
    hw = oret_ref.shape[1]
    mix = _dot(oret_ref[...], w_ref[:hw, :]) + _dot(onsa_ref[...], w_ref[hw:, :])
    x1 = x_ref[...] + mod_ref[0, 2:3, :] * mix
    x1_ref[...] = x1
    y = x1 * lax.rsqrt(jnp.mean(x1 * x1, axis=-1, keepdims=True) + EPS) * g_ref[...]
    h2 = y * (1.0 + mod_ref[0, 4:5, :]) + mod_ref[0, 3:4, :]
    h2_ref[...] = h2
    lt_ref[...] = _dot_nt(wr_ref[...], h2.astype(BF16))


def _outproj(o_ret, o_nsa, x2, mod3, g, w_bf, wr_bf, seq):
    n, d = x2.shape
    tm = min(OUTPROJ_TM, seq)
    hw = o_ret.shape[1]
    return pl.pallas_call(
        _outproj_kernel,
        grid=(n // tm,),
        in_specs=[pl.BlockSpec((tm, hw), lambda i: (i, 0)),
                  pl.BlockSpec((tm, o_nsa.shape[1]), lambda i: (i, 0)),
                  pl.BlockSpec((tm, d), lambda i: (i, 0)),
                  pl.BlockSpec((1, 6, d), lambda i: ((i * tm) // seq, 0, 0)),
                  pl.BlockSpec((1, d), lambda i: (0, 0)),
                  pl.BlockSpec(w_bf.shape, lambda i: (0, 0)),
                  pl.BlockSpec(wr_bf.shape, lambda i: (0, 0))],
        out_specs=[pl.BlockSpec((tm, d), lambda i: (i, 0)),
                   pl.BlockSpec((tm, d), lambda i: (i, 0)),
                   pl.BlockSpec((LANES, tm), lambda i: (0, i))],
        out_shape=[jax.ShapeDtypeStruct((n, d), F32),
                   jax.ShapeDtypeStruct((n, d), F32),
                   jax.ShapeDtypeStruct((LANES, n), F32)],
        compiler_params=_cp(("arbitrary",)),
        name="outproj",
    )(o_ret, o_nsa, x2, mod3, g.reshape(1, d), w_bf, wr_bf)


def _route_kernel(lt_ref, b_ref, tri_ref, ids_ref, wts_ref, cnt_ref, carry_ref, *, sub):
    @pl.when(pl.program_id(0) == 0)
    def _():
        carry_ref[...] = jnp.zeros_like(carry_ref)

    ng, ne = N_GROUPS, EXP_PER_GROUP
    l = lt_ref[...] + b_ref[:, 0:1]
    tc = l.shape[1]
    ridx = lax.broadcasted_iota(I32, (ng, tc), 0).astype(F32)

    def softmax0(v):
        e = jnp.exp(v - jnp.max(v, axis=0, keepdims=True))
        return e / jnp.sum(e, axis=0, keepdims=True)

    def top1(p):
        top = jnp.max(p, axis=0, keepdims=True)
        idx = jnp.min(jnp.where(p == top, ridx, float(ng)), axis=0, keepdims=True)
        return top, idx

    pg_top, grp = top1(softmax0(l[0:ng]))
    leg = jnp.zeros((ne, tc), F32)
    for g in range(ng):
        leg = jnp.where(grp == float(g), l[ng + g * ne:ng + (g + 1) * ne], leg)
    pe = softmax0(leg)
    p1, i1 = top1(pe)
    p2, i2 = top1(jnp.where(ridx == i1, -1.0, pe))
    den = p1 + p2
    w1 = pg_top * p1 / den
    w2 = pg_top * p2 / den
    e1 = grp * float(ne) + i1
    e2 = grp * float(ne) + i2

    eio = lax.broadcasted_iota(I32, (N_EXPERTS, sub), 0).astype(F32)
    r1 = []
    r2 = []
    carry = carry_ref[:, 0:1]
    for c in range(tc // sub):
        cs = slice(c * sub, (c + 1) * sub)
        oh1 = (eio == e1[:, cs]).astype(F32)
        oh2 = (eio == e2[:, cs]).astype(F32)
        oh = oh1 + oh2
        before = carry + _dot(oh.astype(BF16), tri_ref[...])
        r1.append(jnp.sum(oh1 * before, axis=0, keepdims=True))
        r2.append(jnp.sum(oh2 * before, axis=0, keepdims=True))
        carry = carry + jnp.sum(oh, axis=1, keepdims=True)
    carry_ref[...] = jnp.broadcast_to(carry, carry_ref.shape)
    cnt_ref[...] = jnp.broadcast_to(carry, cnt_ref.shape).astype(I32)
    r1 = jnp.concatenate(r1, axis=1)
    r2 = jnp.concatenate(r2, axis=1)
    zf = jnp.zeros((4, tc), F32)
    ids_ref[...] = jnp.concatenate([e1, e2, r1, r2, zf], axis=0).astype(I32)
    wts_ref[...] = jnp.concatenate([w1, w2, jnp.zeros((6, tc), F32)], axis=0)


def _route(lt, bias_col):
    n = lt.shape[1]
    tc = min(ROUTE_TC, n)
    sub = min(ROUTE_SUB, tc)
    tri =jnp.asarray(np.triu(np.ones((sub, sub), np.float32), 1), BF16)
    return pl.pallas_call(
        functools.partial(_route_kernel, sub=sub),
        grid=(n // tc,),
        in_specs=[pl.BlockSpec((LANES, tc), lambda i: (0, i)),
                  pl.BlockSpec((LANES, LANES), lambda i: (0, 0)),
                  pl.BlockSpec((sub, sub), lambda i: (0, 0))],
        out_specs=[pl.BlockSpec((8, tc), lambda i: (0, i)),
                   pl.BlockSpec((8, tc), lambda i: (0, i)),
                   pl.BlockSpec((N_EXPERTS, LANES), lambda i: (0, 0))],
        out_shape=[jax.ShapeDtypeStruct((8, n), I32),
                   jax.ShapeDtypeStruct((8, n), F32),
                   jax.ShapeDtypeStruct((N_EXPERTS, LANES), I32)],
        scratch_shapes=[pltpu.VMEM((N_EXPERTS, LANES), F32)],
        compiler_params=_cp(("arbitrary",)),
        name="route",
    )(lt, bias_col, tri)


def _dest_kernel(ids_ref, ps_ref, o_ref):
    ids = ids_ref[...].astype(F32)
    tc = ids.shape[1]
    eio = lax.broadcasted_iota(I32, (N_EXPERTS, tc), 0).astype(F32)
    ps = ps_ref[:, 0:1]
    rows = [jnp.sum(jnp.where(eio == ids[k:k + 1], ps, 0.0), axis=0, keepdims=True) + ids[2 + k:3 + k]
            for k in range(2)]
    o_ref[...] = jnp.concatenate(rows + [jnp.zeros((6, tc), F32)], axis=0).astype(I32)


def _dest(ids, pad_start):
    n = ids.shape[1]
    tc = min(ROUTE_TC, n)
    ps = jnp.broadcast_to(pad_start.astype(F32)[:, None], (N_EXPERTS, LANES))
    dest = pl.pallas_call(
        _dest_kernel,
        grid=(n // tc,),
        in_specs=[pl.BlockSpec((8, tc), lambda i: (0, i)),
                  pl.BlockSpec((N_EXPERTS, LANES), lambda i: (0, 0))],
        out_specs=pl.BlockSpec((8, tc), lambda i: (0, i)),
        out_shape=jax.ShapeDtypeStruct((8, n), I32),
        compiler_params=_cp(("arbitrary",)),
        name="dest",
    )(ids, ps)
    return dest[:2].T.reshape(2 * n)


def _dispatch_kernel(dst_ref, ps_ref, cnt_ref, h_ref, xs_ref, zbuf, sem, zsem, *, tcd, tm, zr):
    @pl.when(pl.program_id(0) == 0)
    def _():
        zbuf[...] = jnp.zeros_like(zbuf)
        sizes = [zr >> b for b in range(zr.bit_length()) if (zr >> b) >= SUBLANES]

        def zero_rows(row, size):
            return pltpu.make_async_copy(zbuf.at[pl.ds(0, size)],
                                         xs_ref.at[pl.ds(pl.multiple_of(row, SUBLANES), size)], zsem)

        def zero_row(row):
            return pltpu.make_async_copy(zbuf.at[pl.ds(0, 1)], xs_ref.at[pl.ds(row, 1)], zsem)

        def fill(e, wait):
            cnt = cnt_ref[e]
            cnt8 = lax.div(cnt + (SUBLANES - 1), SUBLANES) * SUBLANES
            gap = lax.rem(tm - lax.rem(cnt8, tm), tm)
            base = ps_ref[e]

            def single(c, _):
                cp = zero_row(base + cnt + c)
                cp.wait() if wait else cp.start()
                return 0

            lax.fori_loop(0, cnt8 - cnt, single, 0)
            row = base + cnt8
            for size in sizes:
                has = lax.rem(lax.div(gap, size), 2) == 1

                @pl.when(has)
                def _():
                    cp = zero_rows(row, size)
                    cp.wait() if wait else cp.start()

                row = row + jnp.where(has, size, 0)

        last = N_EXPERTS - 1
        used_rows = ps_ref[last] + lax.div(cnt_ref[last] + (tm - 1), tm) * tm
        n_tail = lax.div(xs_ref.shape[0] - used_rows, zr)

        def tail(c, wait):
            cp = zero_rows(used_rows + c * zr, zr)
            cp.wait() if wait else cp.start()

        for wait in (False, True):
            lax.fori_loop(0, N_EXPERTS, lambda e, _: (fill(e, wait), 0)[1], 0)
            lax.fori_loop(0, n_tail, lambda c, _: (tail(c, wait), 0)[1], 0)

    def issue(t, _):
        for k in range(2):
            dst = dst_ref[2 * t + k]
            pltpu.make_async_copy(h_ref.at[pl.ds(t, 1)], xs_ref.at[pl.ds(dst, 1)], sem).start(priority=k)
        return 0

    lax.fori_loop(0, tcd, issue, 0, unroll=8)
    for k in range(2):
        pltpu.make_async_copy(h_ref, xs_ref.at[pl.ds(0, tcd)], sem).wait()


def _dispatch(dest, pad_start, counts, h2, rows, tcd, tm):
    n, d = h2.shape
    assert tm & (tm - 1) == 0 and tm >= 2 * SUBLANES
    zr = tm // 2
    return pl.pallas_call(
        functools.partial(_dispatch_kernel, tcd=tcd, tm=tm, zr=zr),
        grid=(n // tcd,),
        in_specs=[pl.BlockSpec((2 * tcd,), lambda i: (i,), memory_space=pltpu.SMEM),
                  pl.BlockSpec(memory_space=pltpu.SMEM),
                  pl.BlockSpec(memory_space=pltpu.SMEM),
                  pl.BlockSpec((tcd, d), lambda i: (i, 0))],
        out_specs=pl.BlockSpec(memory_space=pl.ANY),
        out_shape=jax.ShapeDtypeStruct((rows, d), h2.dtype),
        scratch_shapes=[pltpu.VMEM((zr, d), h2.dtype), pltpu.SemaphoreType.DMA(()),
                        pltpu.SemaphoreType.DMA(())],
        compiler_params=_cp(("arbitrary",)),
        name="dispatch",
    )(dest, pad_start, counts, h2)


def _experts_kernel(be_ref, ord_ref, nxt_ref, nu_ref, xs_ref, wg_hbm, wu_hbm, wd_hbm, ys_ref,
                    g_f, u_f, d_f, g_s, u_s, d_s, sem):
    i = pl.program_id(0)
    used = i < nu_ref[0]
    e = be_ref[i]
    fresh = (i == 0) | (e != be_ref[jnp.maximum(i - 1, 0)])
    slot = lax.rem(ord_ref[i], 2)

    def fetch(expert, s):
        return [pltpu.make_async_copy(w.at[expert], f.at[s], sem.at[s, k])
                for k, (w, f) in enumerate(((wg_hbm, g_f), (wu_hbm, u_f), (wd_hbm, d_f)))]

    @pl.when(used & (i == 0))
    def _():
        for c in fetch(e, slot):
            c.start(priority=1)

    @pl.when(used & fresh)
    def _():
        for c in fetch(e, slot):
            c.wait()
        nxt = nxt_ref[i]

        @pl.when(nxt >= 0)
        def _():
            for c in fetch(nxt, 1 - slot):
                c.start(priority=1)

        g_s[...] = g_f[slot].astype(BF16)
        u_s[...] = u_f[slot].astype(BF16)
        d_s[...] = d_f[slot].astype(BF16)

    @pl.when(used)
    def _():
        x = xs_ref[...].astype(BF16)
        a = _silu(_dot(x, g_s[...])) * _dot(x, u_s[...])
        ys_ref[...] = _dot(a.astype(BF16), d_s[...])

    @pl.when(jnp.logical_not(used))
    def _():
        ys_ref[...] = jnp.zeros_like(ys_ref)


def _experts(blk_exp, blk_ord, blk_nxt, n_used, xs, w_gate, w_up, w_down, tm):
    d = xs.shape[1]
    de = w_gate.shape[2]
    nb = blk_exp.shape[0]
    row = lambda i, be, od, nx, nu: (jnp.minimum(i, nu[0] - 1), 0)
    anyspec = pl.BlockSpec(memory_space=pl.ANY)
    grid_spec = pltpu.PrefetchScalarGridSpec(
        num_scalar_prefetch=4,
        grid=(nb,),
        in_specs=[pl.BlockSpec((tm, d), row), anyspec, anyspec, anyspec],
        out_specs=pl.BlockSpec((tm, d), lambda i, be, od, nx, nu: (i, 0)),
        scratch_shapes=[pltpu.VMEM((2, d, de), F32), pltpu.VMEM((2, d, de), F32),
                        pltpu.VMEM((2, de, d), F32),
                        pltpu.VMEM((d, de), BF16), pltpu.VMEM((d, de), BF16),
                        pltpu.VMEM((de, d), BF16),
                        pltpu.SemaphoreType.DMA((2, 3))],
    )
    return pl.pallas_call(
        _experts_kernel,
        grid_spec=grid_spec,
        out_shape=jax.ShapeDtypeStruct((nb * tm, d), F32),
        compiler_params=_cp(("arbitrary",)),
        name="experts",
    )(blk_exp, blk_ord, blk_nxt, n_used, xs, w_gate, w_up, w_down)


def _combine_kernel(ids_ref, idn_ref, ys_ref, x1_ref, wt_ref, mod_ref, g_ref, o_ref,
                    buf, sem, *, tc):
    i = pl.program_id(0)
    slot = lax.rem(i, 2)

    def gather(id_ref, s):
        def issue(t, _):
            for k in range(2):
                src = id_ref[2 * t + k]
                pltpu.make_async_copy(ys_ref.at[pl.ds(src, 1)], buf.at[s, k, pl.ds(t, 1)],
                                      sem.at[s]).start(priority=k)
            return 0

        lax.fori_loop(0, tc, issue, 0, unroll=8)

    @pl.when(i == 0)
    def _():
        gather(ids_ref, 0)

    @pl.when(i + 1 < pl.num_programs(0))
    def _():
        gather(idn_ref, 1 - slot)

    for k in range(2):
        pltpu.make_async_copy(ys_ref.at[pl.ds(0, tc)], buf.at[slot, k], sem.at[slot]).wait()

    moe = buf[slot, 0] * wt_ref[:, 0:1] + buf[slot, 1] * wt_ref[:, 1:2]
    x2 = x1_ref[...] + mod_ref[0, 5:6, :] * moe
    o_ref[...] = x2 * lax.rsqrt(jnp.mean(x2 * x2, axis=-1, keepdims=True) + EPS) * g_ref[...]


def _combine(dest, ys, x1, wts_t, mod3, final_g, seq, tc):
    n, d = x1.shape
    last = n // tc - 1
    return pl.pallas_call(
        functools.partial(_combine_kernel, tc=tc),
        grid=(n // tc,),
        in_specs=[pl.BlockSpec((2 * tc,), lambda i: (i,), memory_space=pltpu.SMEM),
                  pl.BlockSpec((2 * tc,), lambda i: (jnp.minimum(i + 1, last),),
                               memory_space=pltpu.SMEM),
                  pl.BlockSpec(memory_space=pl.ANY),
                  pl.BlockSpec((tc, d), lambda i: (i, 0)),
                  pl.BlockSpec((tc, LANES), lambda i: (i, 0)),
                  pl.BlockSpec((1, 6, d), lambda i: ((i * tc) // seq, 0, 0)),
                  pl.BlockSpec((1, d), lambda i: (0, 0))],
        out_specs=pl.BlockSpec((tc, d), lambda i: (i, 0)),
        out_shape=jax.ShapeDtypeStruct((n, d), F32),
        scratch_shapes=[pltpu.VMEM((2, 2, tc, d), F32), pltpu.SemaphoreType.DMA((2,))],
        compiler_params=_cp(("arbitrary",)),
        name="combine",
    )(dest, dest, ys, x1, wts_t, mod3, final_g.reshape(1, d))


def _token_mixer(x2, mod3, norm1_g, w_in, ret_gn_g, cmp_pos_k, cmp_w1_k, cmp_w2_k,
                 cmp_pos_v, cmp_w1_v, cmp_w2_v, bsz, seq):
    n, d = x2.shape
    w_t = w_in.T
    w_gate_t = jnp.pad(w_t[PROJ_MAIN:], ((0, LANES - N_GATE_COLS), (0, 0))).astype(BF16)
    proj, gate_logits = _inproj(x2, mod3, norm1_g, w_t, w_gate_t, seq)

    o_ret = _retention(proj, ret_gn_g, bsz, seq)

    g_ = NSA_KV_GROUPS
    crow = seq // CMP_STRIDE

    pe_flat = lambda pe: pe.reshape(1, CMP_BLOCK * NSA_D)
    kc, vc = _compress(proj, pe_flat(cmp_pos_k), pe_flat(cmp_pos_v),
                       cmp_w1_k, cmp_w1_v, cmp_w2_k, cmp_w2_v, bsz, seq)

    o_nsa = _nsa(proj, kc, vc, gate_logits, bsz, seq)
    return o_ret, o_nsa


def _moe(h2, lt, x1, mod3, final_g, b_grp, b_exp, w_gate, w_up, w_down, seq, tm):
    n, d = h2.shape
    bias_col = jnp.zeros((LANES,), F32).at[:N_GROUPS].set(b_grp).at[N_GROUPS:N_GROUPS + N_EXPERTS].set(b_exp)
    bias_col = jnp.broadcast_to(bias_col[:, None], (LANES, LANES))
    ids, wts, cnt = _route(lt, bias_col)

    counts = cnt[:, 0]
    padded = (counts + tm - 1) // tm * tm
    pad_end = jnp.cumsum(padded)
    pad_start = (pad_end - padded).astype(I32)
    nb = (2 * n) // tm + N_EXPERTS
    n_used = (pad_end[-1] // tm).astype(I32).reshape(1)
    blk_start = jnp.arange(nb, dtype=I32) * tm
    blk_exp = jnp.minimum(jnp.sum((pad_end[None, :] <= blk_start[:, None]).astype(I32), axis=1),
                          N_EXPERTS - 1).astype(I32)
    last_exp = blk_exp[jnp.maximum(n_used[0] - 1, 0)]
    blk_exp = jnp.where(jnp.arange(nb) < n_used[0], blk_exp, last_exp)
    eid = jnp.arange(N_EXPERTS, dtype=I32)
    has = counts > 0
    exp_ord = jnp.sum((has[None, :] & (eid[None, :] < eid[:, None])).astype(I32), axis=1)
    exp_nxt = jnp.min(jnp.where(has[None, :] & (eid[None, :] > eid[:, None]), eid[None, :], N_EXPERTS), axis=1)
    exp_nxt = jnp.where(exp_nxt < N_EXPERTS, exp_nxt, -1).astype(I32)
    blk_ord = exp_ord[blk_exp]
    blk_nxt = exp_nxt[blk_exp]

    dest = _dest(ids, pad_start)
    xs = _dispatch(dest, pad_start, counts.astype(I32), h2, nb * tm, min(DISPATCH_TC, n), tm)
    ys = _experts(blk_exp, blk_ord, blk_nxt, n_used, xs, w_gate, w_up, w_down, tm)
    wts_t = jnp.pad(wts[:2].T, ((0, 0), (0, LANES - 2)))
    return _combine(dest, ys, x1, wts_t, mod3, final_g, seq, min(COMBINE_TC, n))


def kernel(x, c, w_ada, b_ada, norm1_g, norm2_g, final_g, w_in, ret_gn_g, cmp_pos_k, cmp_w1_k,
           cmp_w2_k, cmp_pos_v, cmp_w1_v, cmp_w2_v, w_out, w_grp, b_grp, w_exp, b_exp, w_gate,
           w_up, w_down):
    bsz, seq, d = x.shape
    assert w_ada.shape[0] == 1, "single-layer block"
    n = bsz * seq
    x2 = x.reshape(n, d)
    mod3 = _ada(c, w_ada[0], b_ada[0]).reshape(bsz, 6, d)

    o_ret, o_nsa = _token_mixer(x2, mod3, norm1_g[0], w_in[0], ret_gn_g[0], cmp_pos_k[0],
                                cmp_w1_k[0], cmp_w2_k[0], cmp_pos_v[0], cmp_w1_v[0], cmp_w2_v[0],
                                bsz, seq)

    w_route = jnp.concatenate([w_grp[0], w_exp[0]], axis=1)
    w_route = jnp.pad(w_route, ((0, 0), (0, LANES - w_route.shape[1]))).T.astype(BF16)
    x1, h2, lt = _outproj(o_ret, o_nsa, x2, mod3, norm2_g[0], w_out[0].astype(BF16), w_route, seq)

    out = _moe(h2, lt, x1, mod3, final_g, b_grp[0], b_exp[0], w_gate[0], w_up[0], w_down[0],
               seq, MOE_TM)
    return out.reshape(bsz, seq, d)
```

```python
import functools

import numpy as np
import jax
import jax.numpy as jnp
from jax import lax
from jax.experimental import pallas as pl
from jax.experimental.pallas import tpu as pltpu

F32 = jnp.float32
BF16 = jnp.bfloat16
I32 = jnp.int32

RET_HEADS = 4
RET_DK = 256
RET_DV = 256
RET_CHUNK = 128
NSA_HEADS = 8
NSA_KV_GROUPS = 2
NSA_HPG = NSA_HEADS // NSA_KV_GROUPS
NSA_D = 128
CMP_BLOCK = 32
CMP_STRIDE = 16
SEL_BLOCK = 64
SEL_COUNT = 16
WIN = 512
N_GROUPS = 8
EXP_PER_GROUP = 8
N_EXPERTS = N_GROUPS * EXP_PER_GROUP
ROPE_BASE = 10000.0
EPS = 1e-6
NEG = -1e30
FORCE_BONUS = 1e4

LANES = 128
SUBLANES = 8
MXU_DIM = 256
VMEM_LIMIT = 52 * 1024 * 1024

ADA_TN = 1024
INPROJ_TM = 2048
INPROJ_TN = 512
RET_CHUNKS = 4
NSA_TQ = 512
NSA_TK = 512
NSA_VT = MXU_DIM
OUTPROJ_TM = 512
ROUTE_TC = 2048
ROUTE_SUB = 512
MOE_TM = 256
DISPATCH_TC = 1024
COMBINE_TC = 256

_C_NQ = 32
_C_KC, _C_VC, _C_KS, _C_VS, _C_KW, _C_VW = 40, 42, 44, 46, 48, 50
_C_GATE = 52
PROJ_MAIN = _C_GATE * LANES
N_GATE_COLS = NSA_HEADS * 3


def _cp(sem, vmem=VMEM_LIMIT):
    return pltpu.CompilerParams(dimension_semantics=sem, vmem_limit_bytes=vmem)


def _silu(v):
    return v * jax.nn.sigmoid(v)


def _dot(a, b):
    return jnp.dot(a, b, preferred_element_type=F32)


def _dot_nt(a, b):
    return lax.dot_general(a, b, (((1,), (1,)), ((), ())), preferred_element_type=F32)


def _dot_tn(a, b):
    return lax.dot_general(a, b, (((0,), (0,)), ((), ())), preferred_element_type=F32)


def _ada_kernel(c_ref, w_ref, b_ref, o_ref):
    ca = _silu(c_ref[...]).astype(BF16)
    o_ref[...] = _dot(ca, w_ref[...].astype(BF16)) + b_ref[...]


def _ada(c, w, b):
    bsz, d = c.shape
    n = w.shape[1]
    tn = min(ADA_TN, n)
    return pl.pallas_call(
        _ada_kernel,
        grid=(n // tn,),
        in_specs=[pl.BlockSpec((bsz, d), lambda j: (0, 0)),
                  pl.BlockSpec((d, tn), lambda j: (0, j)),
                  pl.BlockSpec((1, tn), lambda j: (0, j))],
        out_specs=pl.BlockSpec((bsz, tn), lambda j: (0, j)),
        out_shape=jax.ShapeDtypeStruct((bsz, n), F32),
        compiler_params=_cp(("arbitrary",)),
        name="ada",
    )(c, w, b.reshape(1, n))


def _inproj_kernel(x_hbm, mod_ref, g_ref, wt_ref, wgt_ref, proj_ref, gate_ref, h_ref, x_buf, x_sem):
    i = pl.program_id(0)
    tm = x_buf.shape[0]

    def x_copy(blk):
        return pltpu.make_async_copy(x_hbm.at[pl.ds(pl.multiple_of(blk * tm, tm), tm)], x_buf, x_sem)

    @pl.when(pl.program_id(1) == 0)
    def _():
        @pl.when(i == 0)
        def _():
            x_copy(i).start()

        x_copy(i).wait()
        x = x_buf[...]
        y = x * lax.rsqrt(jnp.mean(x * x, axis=-1, keepdims=True) + EPS) * g_ref[...]
        h = (y * (1.0 + mod_ref[0, 1:2, :]) + mod_ref[0, 0:1, :]).astype(BF16)
        h_ref[...] = h
        gate_ref[...] = _dot_nt(h, wgt_ref[...])

        @pl.when(i + 1 < pl.num_programs(0))
        def _():
            x_copy(i + 1).start(priority=1)

    proj_ref[...] = _dot_nt(h_ref[...], wt_ref[...].astype(BF16)).astype(BF16)


def _inproj(x2, mod3, g, w_t, wg_t, seq):
    n, d = x2.shape
    tm = min(INPROJ_TM, seq)
    tn = INPROJ_TN
    nj = PROJ_MAIN // tn
    return pl.pallas_call(
        _inproj_kernel,
        grid=(n // tm, nj),
        in_specs=[pl.BlockSpec(memory_space=pl.ANY),
                  pl.BlockSpec((1, 6, d), lambda i, j: ((i * tm) // seq, 0, 0)),
                  pl.BlockSpec((1, d), lambda i, j: (0, 0)),
                  pl.BlockSpec((tn, d), lambda i, j: (j, 0)),
                  pl.BlockSpec((LANES, d), lambda i, j: (0, 0))],
        out_specs=[pl.BlockSpec((tm, tn), lambda i, j: (i, j)),
                   pl.BlockSpec((tm, LANES), lambda i, j: (i, 0))],
        out_shape=[jax.ShapeDtypeStruct((n, PROJ_MAIN), BF16),
                   jax.ShapeDtypeStruct((n, LANES), F32)],
        scratch_shapes=[pltpu.VMEM((tm, d), BF16), pltpu.VMEM((tm, d), F32),
                        pltpu.SemaphoreType.DMA(())],
        compiler_params=_cp(("arbitrary", "arbitrary")),
        name="inproj",
    )(x2, mod3, g.reshape(1, d), w_t, wg_t)


def _retention_kernel(q_ref, k_ref, v_ref, g_ref, cos_ref, sin_ref, din_ref, zeta_ref,
                      qdec_ref, cdec_ref, gn_ref, o_ref, s_ref):
    @pl.when(pl.program_id(1) == 0)
    def _():
        s_ref[...] = jnp.zeros_like(s_ref)

    half = RET_DK // 2
    c = RET_CHUNK

    for sub in range(q_ref.shape[0] // c):
        rows = slice(sub * c, (sub + 1) * c)
        cos = cos_ref[rows, :]
        sin = sin_ref[rows, :]

        def rot(a):
            a1, a2 = a[:, :half], a[:, half:]
            return jnp.concatenate([a1 * cos - a2 * sin, a1 * sin + a2 * cos], axis=1)

        for h in range(RET_HEADS):
            qs = slice(h * RET_DK, (h + 1) * RET_DK)
            vs = slice(h * RET_DV, (h + 1) * RET_DV)
            q = rot(q_ref[rows, qs].astype(F32))
            k = rot(k_ref[rows, qs].astype(F32)) * (RET_DK ** -0.5)
            v = v_ref[rows, vs]
            qb = q.astype(BF16)
            kb = k.astype(BF16)
            s = _dot_nt(qb, kb) * din_ref[h]
            inner = _dot(s.astype(BF16), v)
            s_prev = s_ref[h]
            cross = _dot(qb, s_prev.astype(BF16)) * qdec_ref[h]
            kv = _dot_tn((k * zeta_ref[h]).astype(BF16), v)
            s_ref[h] = cdec_ref[h] * s_prev + kv
            o = inner + cross
            mu = jnp.mean(o, axis=-1, keepdims=True)
            oc = o - mu
            var = jnp.mean(oc * oc, axis=-1, keepdims=True)
            o = oc * lax.rsqrt(var + EPS) * gn_ref[:, vs]
            o_ref[rows, vs] = (o * _silu(g_ref[rows, vs].astype(F32))).astype(BF16)


def _retention(proj, gn_g, bsz, seq):
    n = proj.shape[0]
    c = RET_CHUNK
    nc = seq // c
    hw = RET_HEADS * RET_DK
    half = RET_DK // 2
    pos = jnp.arange(seq, dtype=F32)
    inv = ROPE_BASE ** (-jnp.arange(half, dtype=F32) / half)
    ang = pos[:, None] * inv[None, :]
    cos, sin = jnp.cos(ang), jnp.sin(ang)
    log_gamma = jnp.log1p(-jnp.exp2(-5.0 - jnp.arange(RET_HEADS, dtype=F32)))
    idx = jnp.arange(c, dtype=F32)
    rel = idx[:, None] - idx[None, :]
    decay_in = jnp.where(rel >= 0, jnp.exp(log_gamma[:, None, None] * jnp.maximum(rel, 0.0)), 0.0)
    zeta = jnp.exp(log_gamma[:, None] * (c - 1 - idx)[None, :])[:, :, None]
    q_decay = jnp.exp(log_gamma[:, None] * (idx + 1)[None, :])[:, :, None]
    chunk_decay = jnp.exp(log_gamma * c)[:, None, None]
    rb = min(RET_CHUNKS * c, seq)
    ns = seq // rb
    row = lambda b, t: (b * ns + t)
    return pl.pallas_call(
        _retention_kernel,
        grid=(bsz, ns),
        in_specs=[pl.BlockSpec((rb, hw), lambda b, t: (row(b, t), 0)),
                  pl.BlockSpec((rb, hw), lambda b, t: (row(b, t), 1)),
                  pl.BlockSpec((rb, hw), lambda b, t: (row(b, t), 2)),
                  pl.BlockSpec((rb, hw), lambda b, t: (row(b, t), 3)),
                  pl.BlockSpec((rb, half), lambda b, t: (t, 0)),
                  pl.BlockSpec((rb, half), lambda b, t: (t, 0)),
                  pl.BlockSpec((RET_HEADS, c, c), lambda b, t: (0, 0, 0)),
                  pl.BlockSpec((RET_HEADS, c, 1), lambda b, t: (0, 0, 0)),
                  pl.BlockSpec((RET_HEADS, c, 1), lambda b, t: (0, 0, 0)),
                  pl.BlockSpec((RET_HEADS, 1, 1), lambda b, t: (0, 0, 0)),
                  pl.BlockSpec((1, hw), lambda b, t: (0, 0))],
        out_specs=pl.BlockSpec((rb, hw), lambda b, t: (row(b, t), 0)),
        out_shape=jax.ShapeDtypeStruct((n, hw), BF16),
        scratch_shapes=[pltpu.VMEM((RET_HEADS, RET_DK, RET_DV), F32)],
        compiler_params=_cp(("arbitrary", "arbitrary")),
        name="retention",
    )(proj, proj, proj, proj, cos, sin, decay_in, zeta, q_decay, chunk_decay,
      gn_g.reshape(1, hw))


def _compress_kernel(ak_ref, av_ref, pek_ref, pev_ref, w1k_ref, w1v_ref, w2k_ref, w2v_ref,
                     kc_ref, vc_ref, a_scr):
    seq = ak_ref.shape[0]
    nblk = seq // CMP_STRIDE
    a_scr[seq:, :] = jnp.zeros((a_scr.shape[0] - seq, NSA_D), F32)

    def one(a_ref, pe_ref, w1_ref, w2_ref, o_ref):
        a_scr[:seq, :] = a_ref[...].astype(F32)
        flat = jnp.concatenate([a_scr[pl.ds(l, nblk, stride=CMP_STRIDE), :] for l in range(CMP_BLOCK)],
                               axis=1)
        pre = _dot((flat + pe_ref[...]).astype(BF16), w1_ref[...].astype(BF16))
        o_ref[...] = _dot(_silu(pre).astype(BF16), w2_ref[...].astype(BF16)).astype(BF16)

    one(ak_ref, pek_ref, w1k_ref, w2k_ref, kc_ref)
    one(av_ref, pev_ref, w1v_ref, w2v_ref, vc_ref)


def _compress(proj, pek, pev, w1k, w1v, w2k, w2v, bsz, seq):
    g_ = NSA_KV_GROUPS
    nblk = seq // CMP_STRIDE
    full = lambda a: pl.BlockSpec(a.shape, lambda i: (0,) * a.ndim)
    kblk = pl.BlockSpec((seq, NSA_D), lambda i: (i // g_, _C_KC + i % g_))
    vblk = pl.BlockSpec((seq, NSA_D), lambda i: (i // g_, _C_VC + i % g_))
    oblk = pl.BlockSpec((nblk, NSA_D), lambda i: (i, 0))
    return pl.pallas_call(
        _compress_kernel,
        grid=(bsz * g_,),
        in_specs=[kblk, vblk, full(pek), full(pev), full(w1k), full(w1v), full(w2k), full(w2v)],
        out_specs=[oblk, oblk],
        out_shape=[jax.ShapeDtypeStruct((bsz * g_ * nblk, NSA_D), BF16)] * 2,
        scratch_shapes=[pltpu.VMEM((seq + CMP_BLOCK, NSA_D), F32)],
        compiler_params=_cp(("arbitrary",)),
        name="compress",
    )(proj, proj, pek, pev, w1k, w1v, w2k, w2v)


def _nsa_kernel(q_ref, kc_ref, vc_ref, ks_ref, vs_ref, kw_ref, vw_ref, gate_ref, ovt_ref,
                o_ref, kaug_ref, vst_ref, vwt_ref, *, tq, tk, vt, seq, ncmp, wlen):
    crow = kc_ref.shape[0]
    i = pl.program_id(2)
    hg = NSA_HPG
    d = NSA_D
    r = hg * tq

    @pl.when(i == 0)
    def _():
        kaug_ref[:, :d] = ks_ref[...]
        blk = lax.broadcasted_iota(I32, (seq, LANES), 0) // SEL_BLOCK
        lane = lax.broadcasted_iota(I32, (seq, LANES), 1)
        kaug_ref[:, d:] = (blk == lane).astype(BF16)
        for c in range(seq // vt):
            vst_ref[c] = vs_ref[c * vt:(c + 1) * vt, :].astype(F32).T.astype(BF16)
            vwt_ref[c] = vw_ref[c * vt:(c + 1) * vt, :].astype(F32).T.astype(BF16)

    def pv_t(vt_ref, first_tile, p):
        out = None
        for c in range(p.shape[0] // vt):
            part = _dot(vt_ref[first_tile + c], p[c * vt:(c + 1) * vt, :])
            out = part if out is None else out + part
        return out

    q = q_ref[...]
    qh = [(q[:, h * d:(h + 1) * d].astype(F32) * (d ** -0.5)).astype(BF16) for h in range(hg)]
    qa = jnp.concatenate(qh, axis=0)
    t1 = i * tq + lax.broadcasted_iota(I32, (tq, 1), 0)
    tpos = jnp.concatenate([t1] * hg, axis=0)
    t_row = i * tq + lax.broadcasted_iota(I32, (1, tq), 1)
    tpos_row = jnp.concatenate([t_row] * hg, axis=1)

    w0 = pl.multiple_of(jnp.maximum(i * tq + tq - wlen, 0), vt)
    sw = _dot_nt(kw_ref[pl.ds(w0, wlen), :], qa)
    delta = tpos_row - (w0 + lax.broadcasted_iota(I32, (wlen, 1), 0))
    sw = jnp.where((delta >= 0) & (delta < WIN), sw, NEG)
    e_w = jnp.exp(sw - jnp.max(sw, axis=0, keepdims=True))
    o_win = (pv_t(vwt_ref, w0 // vt, e_w.astype(BF16)) / jnp.sum(e_w, axis=0, keepdims=True)).T

    sct = _dot_nt(kc_ref[...], qa)
    cidx = lax.broadcasted_iota(I32, (crow, 1), 0)
    cmask = (cidx * CMP_STRIDE + (CMP_BLOCK - 1) <= tpos_row) & (cidx < ncmp)
    sct = jnp.where(cmask, sct, NEG)
    e_c = jnp.where(cmask, jnp.exp(sct - jnp.max(sct, axis=0, keepdims=True)), 0.0)
    den_c = jnp.sum(e_c, axis=0, keepdims=True)
    p_t = jnp.where(den_c > 0.0, e_c / jnp.where(den_c > 0.0, den_c, 1.0), 0.0)
    o_cmp = _dot_tn(p_t.astype(BF16), vc_ref[...])

    psum_t = p_t[:, 0:tq]
    for h in range(1, hg):
        psum_t = psum_t + p_t[:, h * tq:(h + 1) * tq]
    nsel_blocks = seq // SEL_BLOCK
    nsb = ovt_ref.shape[0]
    imp_t = jnp.dot(ovt_ref[...], psum_t, preferred_element_type=F32,
                    precision=lax.Precision.HIGHEST)
    sidx = lax.broadcasted_iota(I32, (nsb, 1), 0)
    cur = t_row // SEL_BLOCK
    valid = sidx <= cur
    forced = (sidx == 0) | (sidx == cur) | (sidx == cur - 1)
    score = jnp.where(valid, imp_t + jnp.where(forced, FORCE_BONUS, 0.0), -1.0)
    rank = jnp.zeros((nsb, tq), F32)
    for s2 in range(nsel_blocks):
        row = score[s2:s2 + 1, :]
        beats = (row > score) | ((row == score) & (sidx > s2))
        rank = rank + beats.astype(F32)
    sel = valid & (rank < float(min(SEL_COUNT, nsel_blocks)))
    selb_t = jnp.where(sel, 0.0, NEG)
    selb_t = jnp.concatenate([selb_t, jnp.zeros((LANES - nsb, tq), F32)], axis=0)
    selb = selb_t.T.astype(BF16)
    q_aug = jnp.concatenate([jnp.concatenate([qh[h], selb], axis=1) for h in range(hg)], axis=0)

    n_full = (i * tq + 1) // tk
    kl = lax.broadcasted_iota(I32, (tk, 1), 0)

    def tile(j, carry, masked):
        m, l, acc = carry
        k0 = pl.multiple_of(j * tk, tk)
        s = _dot_nt(kaug_ref[pl.ds(k0, tk), :], q_aug)
        if masked:
            s = jnp.where(k0 + kl <= tpos_row, s, NEG)
        m_new = jnp.maximum(m, jnp.max(s, axis=0, keepdims=True))
        alpha = jnp.exp(m - m_new)
        p = jnp.exp(s - m_new)
        l = alpha * l + jnp.sum(p, axis=0, keepdims=True)
        acc = alpha * acc + pv_t(vst_ref, j * (tk // vt), p.astype(BF16))
        return m_new, l, acc

    m0 = jnp.full((1, r), NEG, F32)
    l0 = jnp.zeros((1, r), F32)
    a0 = jnp.zeros((d, r), F32)
    carry = lax.fori_loop(0, n_full, functools.partial(tile, masked=False), (m0, l0, a0))
    _, l_s, acc_s = tile(n_full, carry, True)
    o_slc = (acc_s / l_s).T

    gt = jax.nn.sigmoid(gate_ref[...])
    grp = pl.program_id(1)
    for g in range(1, NSA_KV_GROUPS):
        gt = jnp.where(grp == g, pltpu.roll(gt, LANES - g * 3 * hg, 1), gt)
    for h in range(hg):
        rows = slice(h * tq, (h + 1) * tq)
        o = (gt[:, 3 * h:3 * h + 1] * o_cmp[rows] + gt[:, 3 * h + 1:3 * h + 2] * o_slc[rows]
             + gt[:, 3 * h + 2:3 * h + 3] * o_win[rows])
        o_ref[:, h * d:(h + 1) * d] = o.astype(BF16)


def _nsa(proj, kc, vc, gates_g, bsz, seq):
    n = proj.shape[0]
    g_ = NSA_KV_GROUPS
    tq = NSA_TQ
    tk = min(NSA_TK, seq)
    nq = seq // tq
    ncmp = (seq - CMP_BLOCK) // CMP_STRIDE + 1
    wlen = min(WIN + tq, seq)
    nsel = seq // SEL_BLOCK
    crow = seq // CMP_STRIDE
    assert nsel <= LANES and crow <= LANES and tq <= tk and tk % tq == 0
    nsb = -(-nsel // 8) * 8
    ss = np.arange(nsb)[:, None] * SEL_BLOCK
    cs = np.arange(crow)[None, :] * CMP_STRIDE
    ov = ((cs < ss + SEL_BLOCK) & (cs + CMP_BLOCK > ss)
          & (np.arange(crow)[None, :] < ncmp) & (np.arange(nsb)[:, None] < nsel))
    ov = jnp.asarray(ov.astype(np.float32))
    kvspec = lambda c0: pl.BlockSpec((seq, NSA_D), lambda b, g, i: (b, c0 + g))
    vt = NSA_VT
    assert tk % vt == 0 and wlen % vt == 0 and tq % vt == 0 and seq % vt == 0
    kern = functools.partial(_nsa_kernel, tq=tq, tk=tk, vt=vt, seq=seq, ncmp=ncmp, wlen=wlen)
    return pl.pallas_call(
        kern,
        grid=(bsz, g_, nq),
        in_specs=[pl.BlockSpec((tq, NSA_HPG * NSA_D), lambda b, g, i: (b * nq + i, _C_NQ // NSA_HPG + g)),
                  pl.BlockSpec((crow, NSA_D), lambda b, g, i: (b * g_ + g, 0)),
                  pl.BlockSpec((crow, NSA_D), lambda b, g, i: (b * g_ + g, 0)),
                  kvspec(_C_KS), kvspec(_C_VS), kvspec(_C_KW), kvspec(_C_VW),
                  pl.BlockSpec((tq, LANES), lambda b, g, i: (b * nq + i, 0)),
                  pl.BlockSpec((nsb, crow), lambda b, g, i: (0, 0))],
        out_specs=pl.BlockSpec((tq, NSA_HPG * NSA_D), lambda b, g, i: (b * nq + i, g)),
        out_shape=jax.ShapeDtypeStruct((n, NSA_HEADS * NSA_D), BF16),
        scratch_shapes=[pltpu.VMEM((seq, 2 * NSA_D), BF16),
                        pltpu.VMEM((seq // vt, NSA_D, vt), BF16),
                        pltpu.VMEM((seq // vt, NSA_D, vt), BF16)],
        compiler_params=_cp(("arbitrary", "arbitrary", "arbitrary")),
        name="nsa",
    )(proj, kc, vc, proj, proj, proj, proj, gates_g, ov)


def _outproj_kernel(oret_ref, onsa_ref, x_ref, mod_ref, g_ref, w_ref, wr_ref,
                    x1_ref, h2_ref, lt_ref):
    hw = oret_ref.shape[1]
    mix = _dot(oret_ref[...], w_ref[:hw, :]) + _dot(onsa_ref[...], w_ref[hw:, :])
    x1 = x_ref[...] + mod_ref[0, 2:3, :] * mix
    x1_ref[...] = x1
    y = x1 * lax.rsqrt(jnp.mean(x1 * x1, axis=-1, keepdims=True) + EPS) * g_ref[...]
    h2 = y * (1.0 + mod_ref[0, 4:5, :]) + mod_ref[0, 3:4, :]
    h2_ref[...] = h2
    lt_ref[...] = _dot_nt(wr_ref[...], h2.astype(BF16))


def _outproj(o_ret, o_nsa, x2, mod3, g, w_bf, wr_bf, seq):
    n, d = x2.shape
    tm = min(OUTPROJ_TM, seq)
    hw = o_ret.shape[1]
    return pl.pallas_call(
        _outproj_kernel,
        grid=(n // tm,),
        in_specs=[pl.BlockSpec((tm, hw), lambda i: (i, 0)),
                  pl.BlockSpec((tm, o_nsa.shape[1]), lambda i: (i, 0)),
                  pl.BlockSpec((tm, d), lambda i: (i, 0)),
                  pl.BlockSpec((1, 6, d), lambda i: ((i * tm) // seq, 0, 0)),
                  pl.BlockSpec((1, d), lambda i: (0, 0)),
                  pl.BlockSpec(w_bf.shape, lambda i: (0, 0)),
                  pl.BlockSpec(wr_bf.shape, lambda i: (0, 0))],
        out_specs=[pl.BlockSpec((tm, d), lambda i: (i, 0)),
                   pl.BlockSpec((tm, d), lambda i: (i, 0)),
                   pl.BlockSpec((LANES, tm), lambda i: (0, i))],
        out_shape=[jax.ShapeDtypeStruct((n, d), F32),
                   jax.ShapeDtypeStruct((n, d), F32),
                   jax.ShapeDtypeStruct((LANES, n), F32)],
        compiler_params=_cp(("arbitrary",)),
        name="outproj",
    )(o_ret, o_nsa, x2, mod3, g.reshape(1, d), w_bf, wr_bf)


def _route_kernel(lt_ref, b_ref, tri_ref, ids_ref, wts_ref, cnt_ref, carry_ref, *, sub):
    @pl.when(pl.program_id(0) == 0)
    def _():
        carry_ref[...] = jnp.zeros_like(carry_ref)

    ng, ne = N_GROUPS, EXP_PER_GROUP
    l = lt_ref[...] + b_ref[:, 0:1]
    tc = l.shape[1]
    ridx = lax.broadcasted_iota(I32, (ng, tc), 0).astype(F32)

    def softmax0(v):
        e = jnp.exp(v - jnp.max(v, axis=0, keepdims=True))
        return e / jnp.sum(e, axis=0, keepdims=True)

    def top1(p):
        top = jnp.max(p, axis=0, keepdims=True)
        idx = jnp.min(jnp.where(p == top, ridx, float(ng)), axis=0, keepdims=True)
        return top, idx

    pg_top, grp = top1(softmax0(l[0:ng]))
    leg = jnp.zeros((ne, tc), F32)
    for g in range(ng):
        leg = jnp.where(grp == float(g), l[ng + g * ne:ng + (g + 1) * ne], leg)
    pe = softmax0(leg)
    p1, i1 = top1(pe)
    p2, i2 = top1(jnp.where(ridx == i1, -1.0, pe))
    den = p1 + p2
    w1 = pg_top * p1 / den
    w2 = pg_top * p2 / den
    e1 = grp * float(ne) + i1
    e2 = grp * float(ne) + i2

    eio = lax.broadcasted_iota(I32, (N_EXPERTS, sub), 0).astype(F32)
    r1 = []
    r2 = []
    carry = carry_ref[:, 0:1]
    for c in range(tc // sub):
        cs = slice(c * sub, (c + 1) * sub)
        oh1 = (eio == e1[:, cs]).astype(F32)
        oh2 = (eio == e2[:, cs]).astype(F32)
        oh = oh1 + oh2
        before = carry + _dot(oh.astype(BF16), tri_ref[...])
        r1.append(jnp.sum(oh1 * before, axis=0, keepdims=True))
        r2.append(jnp.sum(oh2 * before, axis=0, keepdims=True))
        carry = carry + jnp.sum(oh, axis=1, keepdims=True)
    carry_ref[...] = jnp.broadcast_to(carry, carry_ref.shape)
    cnt_ref[...] = jnp.broadcast_to(carry, cnt_ref.shape).astype(I32)
    r1 = jnp.concatenate(r1, axis=1)
    r2 = jnp.concatenate(r2, axis=1)
    zf = jnp.zeros((4, tc), F32)
    ids_ref[...] = jnp.concatenate([e1, e2, r1, r2, zf], axis=0).astype(I32)
    wts_ref[...] = jnp.concatenate([w1, w2, jnp.zeros((6, tc), F32)], axis=0)


def _route(lt, bias_col):
    n = lt.shape[1]
    tc = min(ROUTE_TC, n)
    sub = min(ROUTE_SUB, tc)
    tri =jnp.asarray(np.triu(np.ones((sub, sub), np.float32), 1), BF16)
    return pl.pallas_call(
        functools.partial(_route_kernel, sub=sub),
        grid=(n // tc,),
        in_specs=[pl.BlockSpec((LANES, tc), lambda i: (0, i)),
                  pl.BlockSpec((LANES, LANES), lambda i: (0, 0)),
                  pl.BlockSpec((sub, sub), lambda i: (0, 0))],
        out_specs=[pl.BlockSpec((8, tc), lambda i: (0, i)),
                   pl.BlockSpec((8, tc), lambda i: (0, i)),
                   pl.BlockSpec((N_EXPERTS, LANES), lambda i: (0, 0))],
        out_shape=[jax.ShapeDtypeStruct((8, n), I32),
                   jax.ShapeDtypeStruct((8, n), F32),
                   jax.ShapeDtypeStruct((N_EXPERTS, LANES), I32)],
        scratch_shapes=[pltpu.VMEM((N_EXPERTS, LANES), F32)],
        compiler_params=_cp(("arbitrary",)),
        name="route",
    )(lt, bias_col, tri)


def _dest_kernel(ids_ref, ps_ref, o_ref):
    ids = ids_ref[...].astype(F32)
    tc = ids.shape[1]
    eio = lax.broadcasted_iota(I32, (N_EXPERTS, tc), 0).astype(F32)
    ps = ps_ref[:, 0:1]
    rows = [jnp.sum(jnp.where(eio == ids[k:k + 1], ps, 0.0), axis=0, keepdims=True) + ids[2 + k:3 + k]
            for k in range(2)]
    o_ref[...] = jnp.concatenate(rows + [jnp.zeros((6, tc), F32)], axis=0).astype(I32)


def _dest(ids, pad_start):
    n = ids.shape[1]
    tc = min(ROUTE_TC, n)
    ps = jnp.broadcast_to(pad_start.astype(F32)[:, None], (N_EXPERTS, LANES))
    dest = pl.pallas_call(
        _dest_kernel,
        grid=(n // tc,),
        in_specs=[pl.BlockSpec((8, tc), lambda i: (0, i)),
                  pl.BlockSpec((N_EXPERTS, LANES), lambda i: (0, 0))],
        out_specs=pl.BlockSpec((8, tc), lambda i: (0, i)),
        out_shape=jax.ShapeDtypeStruct((8, n), I32),
        compiler_params=_cp(("arbitrary",)),
        name="dest",
    )(ids, ps)
    return dest[:2].T.reshape(2 * n)


def _dispatch_kernel(dst_ref, ps_ref, cnt_ref, h_ref, xs_ref, zbuf, sem, zsem, *, tcd, tm, zr):
    @pl.when(pl.program_id(0) == 0)
    def _():
        zbuf[...] = jnp.zeros_like(zbuf)
        sizes = [zr >> b for b in range(zr.bit_length()) if (zr >> b) >= SUBLANES]

        def zero_rows(row, size):
            return pltpu.make_async_copy(zbuf.at[pl.ds(0, size)],
                                         xs_ref.at[pl.ds(pl.multiple_of(row, SUBLANES), size)], zsem)

        def zero_row(row):
            return pltpu.make_async_copy(zbuf.at[pl.ds(0, 1)], xs_ref.at[pl.ds(row, 1)], zsem)

        def fill(e, wait):
            cnt = cnt_ref[e]
            cnt8 = lax.div(cnt + (SUBLANES - 1), SUBLANES) * SUBLANES
            gap = lax.rem(tm - lax.rem(cnt8, tm), tm)
            base = ps_ref[e]

            def single(c, _):
                cp = zero_row(base + cnt + c)
                cp.wait() if wait else cp.start()
                return 0

            lax.fori_loop(0, cnt8 - cnt, single, 0)
            row = base + cnt8
            for size in sizes:
                has = lax.rem(lax.div(gap, size), 2) == 1

                @pl.when(has)
                def _():
                    cp = zero_rows(row, size)
                    cp.wait() if wait else cp.start()

                row = row + jnp.where(has, size, 0)

        last = N_EXPERTS - 1
        used_rows = ps_ref[last] + lax.div(cnt_ref[last] + (tm - 1), tm) * tm
        n_tail = lax.div(xs_ref.shape[0] - used_rows, zr)

        def tail(c, wait):
            cp = zero_rows(used_rows + c * zr, zr)
            cp.wait() if wait else cp.start()

        for wait in (False, True):
            lax.fori_loop(0, N_EXPERTS, lambda e, _: (fill(e, wait), 0)[1], 0)
            lax.fori_loop(0, n_tail, lambda c, _: (tail(c, wait), 0)[1], 0)

    def issue(t, _):
        for k in range(2):
            dst = dst_ref[2 * t + k]
            pltpu.make_async_copy(h_ref.at[pl.ds(t, 1)], xs_ref.at[pl.ds(dst, 1)], sem).start(priority=k)
        return 0

    lax.fori_loop(0, tcd, issue, 0, unroll=8)
    for k in range(2):
        pltpu.make_async_copy(h_ref, xs_ref.at[pl.ds(0, tcd)], sem).wait()


def _dispatch(dest, pad_start, counts, h2, rows, tcd, tm):
    n, d = h2.shape
    assert tm & (tm - 1) == 0 and tm >= 2 * SUBLANES
    zr = tm // 2
    return pl.pallas_call(
        functools.partial(_dispatch_kernel, tcd=tcd, tm=tm, zr=zr),
        grid=(n // tcd,),
        in_specs=[pl.BlockSpec((2 * tcd,), lambda i: (i,), memory_space=pltpu.SMEM),
                  pl.BlockSpec(memory_space=pltpu.SMEM),
                  pl.BlockSpec(memory_space=pltpu.SMEM),
                  pl.BlockSpec((tcd, d), lambda i: (i, 0))],
        out_specs=pl.BlockSpec(memory_space=pl.ANY),
        out_shape=jax.ShapeDtypeStruct((rows, d), h2.dtype),
        scratch_shapes=[pltpu.VMEM((zr, d), h2.dtype), pltpu.SemaphoreType.DMA(()),
                        pltpu.SemaphoreType.DMA(())],
        compiler_params=_cp(("arbitrary",)),
        name="dispatch",
    )(dest, pad_start, counts, h2)


def _experts_kernel(be_ref, ord_ref, nxt_ref, nu_ref, xs_ref, wg_hbm, wu_hbm, wd_hbm, ys_ref,
                    g_f, u_f, d_f, g_s, u_s, d_s, sem):
    i = pl.program_id(0)
    used = i < nu_ref[0]
    e = be_ref[i]
    fresh = (i == 0) | (e != be_ref[jnp.maximum(i - 1, 0)])
    slot = lax.rem(ord_ref[i], 2)

    def fetch(expert, s):
        return [pltpu.make_async_copy(w.at[expert], f.at[s], sem.at[s, k])
                for k, (w, f) in enumerate(((wg_hbm, g_f), (wu_hbm, u_f), (wd_hbm, d_f)))]

    @pl.when(used & (i == 0))
    def _():
        for c in fetch(e, slot):
            c.start(priority=1)

    @pl.when(used & fresh)
    def _():
        for c in fetch(e, slot):
            c.wait()
        nxt = nxt_ref[i]

        @pl.when(nxt >= 0)
        def _():
            for c in fetch(nxt, 1 - slot):
                c.start(priority=1)

        g_s[...] = g_f[slot].astype(BF16)
        u_s[...] = u_f[slot].astype(BF16)
        d_s[...] = d_f[slot].astype(BF16)

    @pl.when(used)
    def _():
        x = xs_ref[...].astype(BF16)
        a = _silu(_dot(x, g_s[...])) * _dot(x, u_s[...])
        ys_ref[...] = _dot(a.astype(BF16), d_s[...])

    @pl.when(jnp.logical_not(used))
    def _():
        ys_ref[...] = jnp.zeros_like(ys_ref)


def _experts(blk_exp, blk_ord, blk_nxt, n_used, xs, w_gate, w_up, w_down, tm):
    d = xs.shape[1]
    de = w_gate.shape[2]
    nb = blk_exp.shape[0]
    row = lambda i, be, od, nx, nu: (jnp.minimum(i, nu[0] - 1), 0)
    anyspec = pl.BlockSpec(memory_space=pl.ANY)
    grid_spec = pltpu.PrefetchScalarGridSpec(
        num_scalar_prefetch=4,
        grid=(nb,),
        in_specs=[pl.BlockSpec((tm, d), row), anyspec, anyspec, anyspec],
        out_specs=pl.BlockSpec((tm, d), lambda i, be, od, nx, nu: (i, 0)),
        scratch_shapes=[pltpu.VMEM((2, d, de), F32), pltpu.VMEM((2, d, de), F32),
                        pltpu.VMEM((2, de, d), F32),
                        pltpu.VMEM((d, de), BF16), pltpu.VMEM((d, de), BF16),
                        pltpu.VMEM((de, d), BF16),
                        pltpu.SemaphoreType.DMA((2, 3))],
    )
    return pl.pallas_call(
        _experts_kernel,
        grid_spec=grid_spec,
        out_shape=jax.ShapeDtypeStruct((nb * tm, d), F32),
        compiler_params=_cp(("arbitrary",)),
        name="experts",
    )(blk_exp, blk_ord, blk_nxt, n_used, xs, w_gate, w_up, w_down)


def _combine_kernel(ids_ref, idn_ref, ys_ref, x1_ref, wt_ref, mod_ref, g_ref, o_ref,
                    buf, sem, *, tc):
    i = pl.program_id(0)
    slot = lax.rem(i, 2)

    def gather(id_ref, s):
        def issue(t, _):
            for k in range(2):
                src = id_ref[2 * t + k]
                pltpu.make_async_copy(ys_ref.at[pl.ds(src, 1)], buf.at[s, k, pl.ds(t, 1)],
                                      sem.at[s]).start(priority=k)
            return 0

        lax.fori_loop(0, tc, issue, 0, unroll=8)

    @pl.when(i == 0)
    def _():
        gather(ids_ref, 0)

    @pl.when(i + 1 < pl.num_programs(0))
    def _():
        gather(idn_ref, 1 - slot)

    for k in range(2):
        pltpu.make_async_copy(ys_ref.at[pl.ds(0, tc)], buf.at[slot, k], sem.at[slot]).wait()

    moe = buf[slot, 0] * wt_ref[:, 0:1] + buf[slot, 1] * wt_ref[:, 1:2]
    x2 = x1_ref[...] + mod_ref[0, 5:6, :] * moe
    o_ref[...] = x2 * lax.rsqrt(jnp.mean(x2 * x2, axis=-1, keepdims=True) + EPS) * g_ref[...]


def _combine(dest, ys, x1, wts_t, mod3, final_g, seq, tc):
    n, d = x1.shape
    last = n // tc - 1
    return pl.pallas_call(
        functools.partial(_combine_kernel, tc=tc),
        grid=(n // tc,),
        in_specs=[pl.BlockSpec((2 * tc,), lambda i: (i,), memory_space=pltpu.SMEM),
                  pl.BlockSpec((2 * tc,), lambda i: (jnp.minimum(i + 1, last),),
                               memory_space=pltpu.SMEM),
                  pl.BlockSpec(memory_space=pl.ANY),
                  pl.BlockSpec((tc, d), lambda i: (i, 0)),
                  pl.BlockSpec((tc, LANES), lambda i: (i, 0)),
                  pl.BlockSpec((1, 6, d), lambda i: ((i * tc) // seq, 0, 0)),
                  pl.BlockSpec((1, d), lambda i: (0, 0))],
        out_specs=pl.BlockSpec((tc, d), lambda i: (i, 0)),
        out_shape=jax.ShapeDtypeStruct((n, d), F32),
        scratch_shapes=[pltpu.VMEM((2, 2, tc, d), F32), pltpu.SemaphoreType.DMA((2,))],
        compiler_params=_cp(("arbitrary",)),
        name="combine",
    )(dest, dest, ys, x1, wts_t, mod3, final_g.reshape(1, d))


def _token_mixer(x2, mod3, norm1_g, w_in, ret_gn_g, cmp_pos_k, cmp_w1_k, cmp_w2_k,
                 cmp_pos_v, cmp_w1_v, cmp_w2_v, bsz, seq):
    n, d = x2.shape
    w_t = w_in.T
    w_gate_t = jnp.pad(w_t[PROJ_MAIN:], ((0, LANES - N_GATE_COLS), (0, 0))).astype(BF16)
    proj, gate_logits = _inproj(x2, mod3, norm1_g, w_t, w_gate_t, seq)

    o_ret = _retention(proj, ret_gn_g, bsz, seq)

    g_ = NSA_KV_GROUPS
    crow = seq // CMP_STRIDE

    pe_flat = lambda pe: pe.reshape(1, CMP_BLOCK * NSA_D)
    kc, vc = _compress(proj, pe_flat(cmp_pos_k), pe_flat(cmp_pos_v),
                       cmp_w1_k, cmp_w1_v, cmp_w2_k, cmp_w2_v, bsz, seq)

    o_nsa = _nsa(proj, kc, vc, gate_logits, bsz, seq)
    return o_ret, o_nsa


def _moe(h2, lt, x1, mod3, final_g, b_grp, b_exp, w_gate, w_up, w_down, seq, tm):
    n, d = h2.shape
    bias_col = jnp.zeros((LANES,), F32).at[:N_GROUPS].set(b_grp).at[N_GROUPS:N_GROUPS + N_EXPERTS].set(b_exp)
    bias_col = jnp.broadcast_to(bias_col[:, None], (LANES, LANES))
    ids, wts, cnt = _route(lt, bias_col)

    counts = cnt[:, 0]
    padded = (counts + tm - 1) // tm * tm
    pad_end = jnp.cumsum(padded)
    pad_start = (pad_end - padded).astype(I32)
    nb = (2 * n) // tm + N_EXPERTS
    n_used = (pad_end[-1] // tm).astype(I32).reshape(1)
    blk_start = jnp.arange(nb, dtype=I32) * tm
    blk_exp = jnp.minimum(jnp.sum((pad_end[None, :] <= blk_start[:, None]).astype(I32), axis=1),
                          N_EXPERTS - 1).astype(I32)
    last_exp = blk_exp[jnp.maximum(n_used[0] - 1, 0)]
    blk_exp = jnp.where(jnp.arange(nb) < n_used[0], blk_exp, last_exp)
    eid = jnp.arange(N_EXPERTS, dtype=I32)
    has = counts > 0
    exp_ord = jnp.sum((has[None, :] & (eid[None, :] < eid[:, None])).astype(I32), axis=1)
    exp_nxt = jnp.min(jnp.where(has[None, :] & (eid[None, :] > eid[:, None]), eid[None, :], N_EXPERTS), axis=1)
    exp_nxt = jnp.where(exp_nxt < N_EXPERTS, exp_nxt, -1).astype(I32)
    blk_ord = exp_ord[blk_exp]
    blk_nxt = exp_nxt[blk_exp]

    dest = _dest(ids, pad_start)
    xs = _dispatch(dest, pad_start, counts.astype(I32), h2, nb * tm, min(DISPATCH_TC, n), tm)
    ys = _experts(blk_exp, blk_ord, blk_nxt, n_used, xs, w_gate, w_up, w_down, tm)
    wts_t = jnp.pad(wts[:2].T, ((0, 0), (0, LANES - 2)))
    return _combine(dest, ys, x1, wts_t, mod3, final_g, seq, min(COMBINE_TC, n))


def kernel(x, c, w_ada, b_ada, norm1_g, norm2_g, final_g, w_in, ret_gn_g, cmp_pos_k, cmp_w1_k,
           cmp_w2_k, cmp_pos_v, cmp_w1_v, cmp_w2_v, w_out, w_grp, b_grp, w_exp, b_exp, w_gate,
           w_up, w_down):
    bsz, seq, d = x.shape
    assert w_ada.shape[0] == 1, "single-layer block"
    n = bsz * seq
    x2 = x.reshape(n, d)
    mod3 = _ada(c, w_ada[0], b_ada[0]).reshape(bsz, 6, d)

    o_ret, o_nsa = _token_mixer(x2, mod3, norm1_g[0], w_in[0], ret_gn_g[0], cmp_pos_k[0],
                                cmp_w1_k[0], cmp_w2_k[0], cmp_pos_v[0], cmp_w1_v[0], cmp_w2_v[0],
                                bsz, seq)

    w_route = jnp.concatenate([w_grp[0], w_exp[0]], axis=1)
    w_route = jnp.pad(w_route, ((0, 0), (0, LANES - w_route.shape[1]))).T.astype(BF16)
    x1, h2, lt = _outproj(o_ret, o_nsa, x2, mod3, norm2_g[0], w_out[0].astype(BF16), w_route, seq)

    out = _moe(h2, lt, x1, mod3, final_g, b_grp[0], b_exp[0], w_gate[0], w_up[0], w_down[0],
               seq, MOE_TM)
    return out.reshape(bsz, seq, d)
```

```python
import functools

import numpy as np
import jax
import jax.numpy as jnp
from jax import lax
from jax.experimental import pallas as pl
from jax.experimental.pallas import tpu as pltpu

F32 = jnp.float32
BF16 = jnp.bfloat16
I32 = jnp.int32

RET_HEADS = 4
RET_DK = 256
RET_DV = 256
RET_CHUNK = 128
NSA_HEADS = 8
NSA_KV_GROUPS = 2
NSA_HPG = NSA_HEADS // NSA_KV_GROUPS
NSA_D = 128
CMP_BLOCK = 32
CMP_STRIDE = 16
SEL_BLOCK = 64
SEL_COUNT = 16
WIN = 512
N_GROUPS = 8
EXP_PER_GROUP = 8
N_EXPERTS = N_GROUPS * EXP_PER_GROUP
ROPE_BASE = 10000.0
EPS = 1e-6
NEG = -1e30
FORCE_BONUS = 1e4

LANES = 128
SUBLANES = 8
MXU_DIM = 256
VMEM_LIMIT = 52 * 1024 * 1024

ADA_TN = 1024
INPROJ_TM = 2048
INPROJ_TN = 512
RET_CHUNKS = 4
NSA_TQ = 512
NSA_TK = 512
NSA_VT = MXU_DIM
OUTPROJ_TM = 512
ROUTE_TC = 2048
ROUTE_SUB = 512
MOE_TM = 256
DISPATCH_TC = 1024
COMBINE_TC = 256

_C_NQ = 32
_C_KC, _C_VC, _C_KS, _C_VS, _C_KW, _C_VW = 40, 42, 44, 46, 48, 50
_C_GATE = 52
PROJ_MAIN = _C_GATE * LANES
N_GATE_COLS = NSA_HEADS * 3


def _cp(sem, vmem=VMEM_LIMIT):
    return pltpu.CompilerParams(dimension_semantics=sem, vmem_limit_bytes=vmem)


def _silu(v):
    return v * jax.nn.sigmoid(v)


def _dot(a, b):
    return jnp.dot(a, b, preferred_element_type=F32)


def _dot_nt(a, b):
    return lax.dot_general(a, b, (((1,), (1,)), ((), ())), preferred_element_type=F32)


def _dot_tn(a, b):
    return lax.dot_general(a, b, (((0,), (0,)), ((), ())), preferred_element_type=F32)


def _ada_kernel(c_ref, w_ref, b_ref, o_ref):
    ca = _silu(c_ref[...]).astype(BF16)
    o_ref[...] = _dot(ca, w_ref[...].astype(BF16)) + b_ref[...]


def _ada(c, w, b):
    bsz, d = c.shape
    n = w.shape[1]
    tn = min(ADA_TN, n)
    return pl.pallas_call(
        _ada_kernel,
        grid=(n // tn,),
        in_specs=[pl.BlockSpec((bsz, d), lambda j: (0, 0)),
                  pl.BlockSpec((d, tn), lambda j: (0, j)),
                  pl.BlockSpec((1, tn), lambda j: (0, j))],
        out_specs=pl.BlockSpec((bsz, tn), lambda j: (0, j)),
        out_shape=jax.ShapeDtypeStruct((bsz, n), F32),
        compiler_params=_cp(("arbitrary",)),
        name="ada",
    )(c, w, b.reshape(1, n))


def _inproj_kernel(x_hbm, mod_ref, g_ref, wt_ref, wgt_ref, proj_ref, gate_ref, h_ref, x_buf, x_sem):
    i = pl.program_id(0)
    tm = x_buf.shape[0]

    def x_copy(blk):
        return pltpu.make_async_copy(x_hbm.at[pl.ds(pl.multiple_of(blk * tm, tm), tm)], x_buf, x_sem)

    @pl.when(pl.program_id(1) == 0)
    def _():
        @pl.when(i == 0)
        def _():
            x_copy(i).start()

        x_copy(i).wait()
        x = x_buf[...]
        y = x * lax.rsqrt(jnp.mean(x * x, axis=-1, keepdims=True) + EPS) * g_ref[...]
        h = (y * (1.0 + mod_ref[0, 1:2, :]) + mod_ref[0, 0:1, :]).astype(BF16)
        h_ref[...] = h
        gate_ref[...] = _dot_nt(h, wgt_ref[...])

        @pl.when(i + 1 < pl.num_programs(0))
        def _():
            x_copy(i + 1).start(priority=1)

    proj_ref[...] = _dot_nt(h_ref[...], wt_ref[...].astype(BF16)).astype(BF16)


def _inproj(x2, mod3, g, w_t, wg_t, seq):
    n, d = x2.shape
    tm = min(INPROJ_TM, seq)
    tn = INPROJ_TN
    nj = PROJ_MAIN // tn
    return pl.pallas_call(
        _inproj_kernel,
        grid=(n // tm, nj),
        in_specs=[pl.BlockSpec(memory_space=pl.ANY),
                  pl.BlockSpec((1, 6, d), lambda i, j: ((i * tm) // seq, 0, 0)),
                  pl.BlockSpec((1, d), lambda i, j: (0, 0)),
                  pl.BlockSpec((tn, d), lambda i, j: (j, 0)),
                  pl.BlockSpec((LANES, d), lambda i, j: (0, 0))],
        out_specs=[pl.BlockSpec((tm, tn), lambda i, j: (i, j)),
                   pl.BlockSpec((tm, LANES), lambda i, j: (i, 0))],
        out_shape=[jax.ShapeDtypeStruct((n, PROJ_MAIN), BF16),
                   jax.ShapeDtypeStruct((n, LANES), F32)],
        scratch_shapes=[pltpu.VMEM((tm, d), BF16), pltpu.VMEM((tm, d), F32),
                        pltpu.SemaphoreType.DMA(())],
        compiler_params=_cp(("arbitrary", "arbitrary")),
        name="inproj",
    )(x2, mod3, g.reshape(1, d), w_t, wg_t)


def _retention_kernel(q_ref, k_ref, v_ref, g_ref, cos_ref, sin_ref, din_ref, zeta_ref,
                      qdec_ref, cdec_ref, gn_ref, o_ref, s_ref):
    @pl.when(pl.program_id(1) == 0)
    def _():
        s_ref[...] = jnp.zeros_like(s_ref)

    half = RET_DK // 2
    c = RET_CHUNK

    for sub in range(q_ref.shape[0] // c):
        rows = slice(sub * c, (sub + 1) * c)
        cos = cos_ref[rows, :]
        sin = sin_ref[rows, :]

        def rot(a):
            a1, a2 = a[:, :half], a[:, half:]
            return jnp.concatenate([a1 * cos - a2 * sin, a1 * sin + a2 * cos], axis=1)

        for h in range(RET_HEADS):
            qs = slice(h * RET_DK, (h + 1) * RET_DK)
            vs = slice(h * RET_DV, (h + 1) * RET_DV)
            q = rot(q_ref[rows, qs].astype(F32))
            k = rot(k_ref[rows, qs].astype(F32)) * (RET_DK ** -0.5)
            v = v_ref[rows, vs]
            qb = q.astype(BF16)
            kb = k.astype(BF16)
            s = _dot_nt(qb, kb) * din_ref[h]
            inner = _dot(s.astype(BF16), v)
            s_prev = s_ref[h]
            cross = _dot(qb, s_prev.astype(BF16)) * qdec_ref[h]
            kv = _dot_tn((k * zeta_ref[h]).astype(BF16), v)
            s_ref[h] = cdec_ref[h] * s_prev + kv
            o = inner + cross
            mu = jnp.mean(o, axis=-1, keepdims=True)
            oc = o - mu
            var = jnp.mean(oc * oc, axis=-1, keepdims=True)
            o = oc * lax.rsqrt(var + EPS) * gn_ref[:, vs]
            o_ref[rows, vs] = (o * _silu(g_ref[rows, vs].astype(F32))).astype(BF16)


def _retention(proj, gn_g, bsz, seq):
    n = proj.shape[0]
    c = RET_CHUNK
    nc = seq // c
    hw = RET_HEADS * RET_DK
    half = RET_DK // 2
    pos = jnp.arange(seq, dtype=F32)
    inv = ROPE_BASE ** (-jnp.arange(half, dtype=F32) / half)
    ang = pos[:, None] * inv[None, :]
    cos, sin = jnp.cos(ang), jnp.sin(ang)
    log_gamma = jnp.log1p(-jnp.exp2(-5.0 - jnp.arange(RET_HEADS, dtype=F32)))
    idx = jnp.arange(c, dtype=F32)
    rel = idx[:, None] - idx[None, :]
    decay_in = jnp.where(rel >= 0, jnp.exp(log_gamma[:, None, None] * jnp.maximum(rel, 0.0)), 0.0)
    zeta = jnp.exp(log_gamma[:, None] * (c - 1 - idx)[None, :])[:, :, None]
    q_decay = jnp.exp(log_gamma[:, None] * (idx + 1)[None, :])[:, :, None]
    chunk_decay = jnp.exp(log_gamma * c)[:, None, None]
    rb = min(RET_CHUNKS * c, seq)
    ns = seq // rb
    row = lambda b, t: (b * ns + t)
    return pl.pallas_call(
        _retention_kernel,
        grid=(bsz, ns),
        in_specs=[pl.BlockSpec((rb, hw), lambda b, t: (row(b, t), 0)),
                  pl.BlockSpec((rb, hw), lambda b, t: (row(b, t), 1)),
                  pl.BlockSpec((rb, hw), lambda b, t: (row(b, t), 2)),
                  pl.BlockSpec((rb, hw), lambda b, t: (row(b, t), 3)),
                  pl.BlockSpec((rb, half), lambda b, t: (t, 0)),
                  pl.BlockSpec((rb, half), lambda b, t: (t, 0)),
                  pl.BlockSpec((RET_HEADS, c, c), lambda b, t: (0, 0, 0)),
                  pl.BlockSpec((RET_HEADS, c, 1), lambda b, t: (0, 0, 0)),
                  pl.BlockSpec((RET_HEADS, c, 1), lambda b, t: (0, 0, 0)),
                  pl.BlockSpec((RET_HEADS, 1, 1), lambda b, t: (0, 0, 0)),
                  pl.BlockSpec((1, hw), lambda b, t: (0, 0))],
        out_specs=pl.BlockSpec((rb, hw), lambda b, t: (row(b, t), 0)),
        out_shape=jax.ShapeDtypeStruct((n, hw), BF16),
        scratch_shapes=[pltpu.VMEM((RET_HEADS, RET_DK, RET_DV), F32)],
        compiler_params=_cp(("arbitrary", "arbitrary")),
        name="retention",
    )(proj, proj, proj, proj, cos, sin, decay_in, zeta, q_decay, chunk_decay,
      gn_g.reshape(1, hw))


def _compress_kernel(ak_ref, av_ref, pek_ref, pev_ref, w1k_ref, w1v_ref, w2k_ref, w2v_ref,
                     kc_ref, vc_ref, a_scr):
    seq = ak_ref.shape[0]
    nblk = seq // CMP_STRIDE
    a_scr[seq:, :] = jnp.zeros((a_scr.shape[0] - seq, NSA_D), F32)

    def one(a_ref, pe_ref, w1_ref, w2_ref, o_ref):
        a_scr[:seq, :] = a_ref[...].astype(F32)
        flat = jnp.concatenate([a_scr[pl.ds(l, nblk, stride=CMP_STRIDE), :] for l in range(CMP_BLOCK)],
                               axis=1)
        pre = _dot((flat + pe_ref[...]).astype(BF16), w1_ref[...].astype(BF16))
        o_ref[...] = _dot(_silu(pre).astype(BF16), w2_ref[...].astype(BF16)).astype(BF16)

    one(ak_ref, pek_ref, w1k_ref, w2k_ref, kc_ref)
    one(av_ref, pev_ref, w1v_ref, w2v_ref, vc_ref)


def _compress(proj, pek, pev, w1k, w1v, w2k, w2v, bsz, seq):
    g_ = NSA_KV_GROUPS
    nblk = seq // CMP_STRIDE
    full = lambda a: pl.BlockSpec(a.shape, lambda i: (0,) * a.ndim)
    kblk = pl.BlockSpec((seq, NSA_D), lambda i: (i // g_, _C_KC + i % g_))
    vblk = pl.BlockSpec((seq, NSA_D), lambda i: (i // g_, _C_VC + i % g_))
    oblk = pl.BlockSpec((nblk, NSA_D), lambda i: (i, 0))
    return pl.pallas_call(
        _compress_kernel,
        grid=(bsz * g_,),
        in_specs=[kblk, vblk, full(pek), full(pev), full(w1k), full(w1v), full(w2k), full(w2v)],
        out_specs=[oblk, oblk],
        out_shape=[jax.ShapeDtypeStruct((bsz * g_ * nblk, NSA_D), BF16)] * 2,
        scratch_shapes=[pltpu.VMEM((seq + CMP_BLOCK, NSA_D), F32)],
        compiler_params=_cp(("arbitrary",)),
        name="compress",
    )(proj, proj, pek, pev, w1k, w1v, w2k, w2v)


def _nsa_kernel(q_ref, kc_ref, vc_ref, ks_ref, vs_ref, kw_ref, vw_ref, gate_ref, ovt_ref,
                o_ref, kaug_ref, vst_ref, vwt_ref, *, tq, tk, vt, seq, ncmp, wlen):
    crow = kc_ref.shape[0]
    i = pl.program_id(2)
    hg = NSA_HPG
    d = NSA_D
    r = hg * tq

    @pl.when(i == 0)
    def _():
        kaug_ref[:, :d] = ks_ref[...]
        blk = lax.broadcasted_iota(I32, (seq, LANES), 0) // SEL_BLOCK
        lane = lax.broadcasted_iota(I32, (seq, LANES), 1)
        kaug_ref[:, d:] = (blk == lane).astype(BF16)
        for c in range(seq // vt):
            vst_ref[c] = vs_ref[c * vt:(c + 1) * vt, :].astype(F32).T.astype(BF16)
            vwt_ref[c] = vw_ref[c * vt:(c + 1) * vt, :].astype(F32).T.astype(BF16)

    def pv_t(vt_ref, first_tile, p):
        out = None
        for c in range(p.shape[0] // vt):
            part = _dot(vt_ref[first_tile + c], p[c * vt:(c + 1) * vt, :])
            out = part if out is None else out + part
        return out

    q = q_ref[...]
    qh = [(q[:, h * d:(h + 1) * d].astype(F32) * (d ** -0.5)).astype(BF16) for h in range(hg)]
    qa = jnp.concatenate(qh, axis=0)
    t1 = i * tq + lax.broadcasted_iota(I32, (tq, 1), 0)
    tpos = jnp.concatenate([t1] * hg, axis=0)
    t_row = i * tq + lax.broadcasted_iota(I32, (1, tq), 1)
    tpos_row = jnp.concatenate([t_row] * hg, axis=1)

    w0 = pl.multiple_of(jnp.maximum(i * tq + tq - wlen, 0), vt)
    sw = _dot_nt(kw_ref[pl.ds(w0, wlen), :], qa)
    delta = tpos_row - (w0 + lax.broadcasted_iota(I32, (wlen, 1), 0))
    sw = jnp.where((delta >= 0) & (delta < WIN), sw, NEG)
    e_w = jnp.exp(sw - jnp.max(sw, axis=0, keepdims=True))
    o_win = (pv_t(vwt_ref, w0 // vt, e_w.astype(BF16)) / jnp.sum(e_w, axis=0, keepdims=True)).T

    sct = _dot_nt(kc_ref[...], qa)
    cidx = lax.broadcasted_iota(I32, (crow, 1), 0)
    cmask = (cidx * CMP_STRIDE + (CMP_BLOCK - 1) <= tpos_row) & (cidx < ncmp)
    sct = jnp.where(cmask, sct, NEG)
    e_c = jnp.where(cmask, jnp.exp(sct - jnp.max(sct, axis=0, keepdims=True)), 0.0)
    den_c = jnp.sum(e_c, axis=0, keepdims=True)
    p_t = jnp.where(den_c > 0.0, e_c / jnp.where(den_c > 0.0, den_c, 1.0), 0.0)
    o_cmp = _dot_tn(p_t.astype(BF16), vc_ref[...])

    psum_t = p_t[:, 0:tq]
    for h in range(1, hg):
        psum_t = psum_t + p_t[:, h * tq:(h + 1) * tq]
    nsel_blocks = seq // SEL_BLOCK
    nsb = ovt_ref.shape[0]
    imp_t = jnp.dot(ovt_ref[...], psum_t, preferred_element_type=F32,
                    precision=lax.Precision.HIGHEST)
    sidx = lax.broadcasted_iota(I32, (nsb, 1), 0)
    cur = t_row // SEL_BLOCK
    valid = sidx <= cur
    forced = (sidx == 0) | (sidx == cur) | (sidx == cur - 1)
    score = jnp.where(valid, imp_t + jnp.where(forced, FORCE_BONUS, 0.0), -1.0)
    rank = jnp.zeros((nsb, tq), F32)
    for s2 in range(nsel_blocks):
        row = score[s2:s2 + 1, :]
        beats = (row > score) | ((row == score) & (sidx > s2))
        rank = rank + beats.astype(F32)
    sel = valid & (rank < float(min(SEL_COUNT, nsel_blocks)))
    selb_t = jnp.where(sel, 0.0, NEG)
    selb_t = jnp.concatenate([selb_t, jnp.zeros((LANES - nsb, tq), F32)], axis=0)
    selb = selb_t.T.astype(BF16)
    q_aug = jnp.concatenate([jnp.concatenate([qh[h], selb], axis=1) for h in range(hg)], axis=0)

    n_full = (i * tq + 1) // tk
    kl = lax.broadcasted_iota(I32, (tk, 1), 0)

    def tile(j, carry, masked):
        m, l, acc = carry
        k0 = pl.multiple_of(j * tk, tk)
        s = _dot_nt(kaug_ref[pl.ds(k0, tk), :], q_aug)
        if masked:
            s = jnp.where(k0 + kl <= tpos_row, s, NEG)
        m_new = jnp.maximum(m, jnp.max(s, axis=0, keepdims=True))
        alpha = jnp.exp(m - m_new)
        p = jnp.exp(s - m_new)
        l = alpha * l + jnp.sum(p, axis=0, keepdims=True)
        acc = alpha * acc + pv_t(vst_ref, j * (tk // vt), p.astype(BF16))
        return m_new, l, acc

    m0 = jnp.full((1, r), NEG, F32)
    l0 = jnp.zeros((1, r), F32)
    a0 = jnp.zeros((d, r), F32)
    carry = lax.fori_loop(0, n_full, functools.partial(tile, masked=False), (m0, l0, a0))
    _, l_s, acc_s = tile(n_full, carry, True)
    o_slc = (acc_s / l_s).T

    gt = jax.nn.sigmoid(gate_ref[...])
    grp = pl.program_id(1)
    for g in range(1, NSA_KV_GROUPS):
        gt = jnp.where(grp == g, pltpu.roll(gt, LANES - g * 3 * hg, 1), gt)
    for h in range(hg):
        rows = slice(h * tq, (h + 1) * tq)
        o = (gt[:, 3 * h:3 * h + 1] * o_cmp[rows] + gt[:, 3 * h + 1:3 * h + 2] * o_slc[rows]
             + gt[:, 3 * h + 2:3 * h + 3] * o_win[rows])
        o_ref[:, h * d:(h + 1) * d] = o.astype(BF16)


def _nsa(proj, kc, vc, gates_g, bsz, seq):
    n = proj.shape[0]
    g_ = NSA_KV_GROUPS
    tq = NSA_TQ
    tk = min(NSA_TK, seq)
    nq = seq // tq
    ncmp = (seq - CMP_BLOCK) // CMP_STRIDE + 1
    wlen = min(WIN + tq, seq)
    nsel = seq // SEL_BLOCK
    crow = seq // CMP_STRIDE
    assert nsel <= LANES and crow <= LANES and tq <= tk and tk % tq == 0
    nsb = -(-nsel // 8) * 8
    ss = np.arange(nsb)[:, None] * SEL_BLOCK
    cs = np.arange(crow)[None, :] * CMP_STRIDE
    ov = ((cs < ss + SEL_BLOCK) & (cs + CMP_BLOCK > ss)
          & (np.arange(crow)[None, :] < ncmp) & (np.arange(nsb)[:, None] < nsel))
    ov = jnp.asarray(ov.astype(np.float32))
    kvspec = lambda c0: pl.BlockSpec((seq, NSA_D), lambda b, g, i: (b, c0 + g))
    vt = NSA_VT
    assert tk % vt == 0 and wlen % vt == 0 and tq % vt == 0 and seq % vt == 0
    kern = functools.partial(_nsa_kernel, tq=tq, tk=tk, vt=vt, seq=seq, ncmp=ncmp, wlen=wlen)
    return pl.pallas_call(
        kern,
        grid=(bsz, g_, nq),
        in_specs=[pl.BlockSpec((tq, NSA_HPG * NSA_D), lambda b, g, i: (b * nq + i, _C_NQ // NSA_HPG + g)),
                  pl.BlockSpec((crow, NSA_D), lambda b, g, i: (b * g_ + g, 0)),
                  pl.BlockSpec((crow, NSA_D), lambda b, g, i: (b * g_ + g, 0)),
                  kvspec(_C_KS), kvspec(_C_VS), kvspec(_C_KW), kvspec(_C_VW),
                  pl.BlockSpec((tq, LANES), lambda b, g, i: (b * nq + i, 0)),
                  pl.BlockSpec((nsb, crow), lambda b, g, i: (0, 0))],
        out_specs=pl.BlockSpec((tq, NSA_HPG * NSA_D), lambda b, g, i: (b * nq + i, g)),
        out_shape=jax.ShapeDtypeStruct((n, NSA_HEADS * NSA_D), BF16),
        scratch_shapes=[pltpu.VMEM((seq, 2 * NSA_D), BF16),
                        pltpu.VMEM((seq // vt, NSA_D, vt), BF16),
                        pltpu.VMEM((seq // vt, NSA_D, vt), BF16)],
        compiler_params=_cp(("arbitrary", "arbitrary", "arbitrary")),
        name="nsa",
    )(proj, kc, vc, proj, proj, proj, proj, gates_g, ov)


def _outproj_kernel(oret_ref, onsa_ref, x_ref, mod_ref, g_ref, w_ref, wr_ref,
                    x1_ref, h2_ref, lt_ref):
    hw = oret_ref.shape[1]
    mix = _dot(oret_ref[...], w_ref[:hw, :]) + _dot(onsa_ref[...], w_ref[hw:, :])
    x1 = x_ref[...] + mod_ref[0, 2:3, :] * mix
    x1_ref[...] = x1
    y = x1 * lax.rsqrt(jnp.mean(x1 * x1, axis=-1, keepdims=True) + EPS) * g_ref[...]
    h2 = y * (1.0 + mod_ref[0, 4:5, :]) + mod_ref[0, 3:4, :]
    h2_ref[...] = h2
    lt_ref[...] = _dot_nt(wr_ref[...], h2.astype(BF16))


def _outproj(o_ret, o_nsa, x2, mod3, g, w_bf, wr_bf, seq):
    n, d = x2.shape
    tm = min(OUTPROJ_TM, seq)
    hw = o_ret.shape[1]
    return pl.pallas_call(
        _outproj_kernel,
        grid=(n // tm,),
        in_specs=[pl.BlockSpec((tm, hw), lambda i: (i, 0)),
                  pl.BlockSpec((tm, o_nsa.shape[1]), lambda i: (i, 0)),
                  pl.BlockSpec((tm, d), lambda i: (i, 0)),
                  pl.BlockSpec((1, 6, d), lambda i: ((i * tm) // seq, 0, 0)),
                  pl.BlockSpec((1, d), lambda i: (0, 0)),
                  pl.BlockSpec(w_bf.shape, lambda i: (0, 0)),
                  pl.BlockSpec(wr_bf.shape, lambda i: (0, 0))],
        out_specs=[pl.BlockSpec((tm, d), lambda i: (i, 0)),
                   pl.BlockSpec((tm, d), lambda i: (i, 0)),
                   pl.BlockSpec((LANES, tm), lambda i: (0, i))],
        out_shape=[jax.ShapeDtypeStruct((n, d), F32),
                   jax.ShapeDtypeStruct((n, d), F32),
                   jax.ShapeDtypeStruct((LANES, n), F32)],
        compiler_params=_cp(("arbitrary",)),
        name="outproj",
    )(o_ret, o_nsa, x2, mod3, g.reshape(1, d), w_bf, wr_bf)


def _route_kernel(lt_ref, b_ref, tri_ref, ids_ref, wts_ref, cnt_ref, carry_ref, *, sub):
    @pl.when(pl.program_id(0) == 0)
    def _():
        carry_ref[...] = jnp.zeros_like(carry_ref)

    ng, ne = N_GROUPS, EXP_PER_GROUP
    l = lt_ref[...] + b_ref[:, 0:1]
    tc = l.shape[1]
    ridx = lax.broadcasted_iota(I32, (ng, tc), 0).astype(F32)

    def softmax0(v):
        e = jnp.exp(v - jnp.max(v, axis=0, keepdims=True))
        return e / jnp.sum(e, axis=0, keepdims=True)

    def top1(p):
        top = jnp.max(p, axis=0, keepdims=True)
        idx = jnp.min(jnp.where(p == top, ridx, float(ng)), axis=0, keepdims=True)
        return top, idx

    pg_top, grp = top1(softmax0(l[0:ng]))
    leg = jnp.zeros((ne, tc), F32)
    for g in range(ng):
        leg = jnp.where(grp == float(g), l[ng + g * ne:ng + (g + 1) * ne], leg)
    pe = softmax0(leg)
    p1, i1 = top1(pe)
    p2, i2 = top1(jnp.where(ridx == i1, -1.0, pe))
    den = p1 + p2
    w1 = pg_top * p1 / den
    w2 = pg_top * p2 / den
    e1 = grp * float(ne) + i1
    e2 = grp * float(ne) + i2

    eio = lax.broadcasted_iota(I32, (N_EXPERTS, sub), 0).astype(F32)
    r1 = []
    r2 = []
    carry = carry_ref[:, 0:1]
    for c in range(tc // sub):
        cs = slice(c * sub, (c + 1) * sub)
        oh1 = (eio == e1[:, cs]).astype(F32)
        oh2 = (eio == e2[:, cs]).astype(F32)
        oh = oh1 + oh2
        before = carry + _dot(oh.astype(BF16), tri_ref[...])
        r1.append(jnp.sum(oh1 * before, axis=0, keepdims=True))
        r2.append(jnp.sum(oh2 * before, axis=0, keepdims=True))
        carry = carry + jnp.sum(oh, axis=1, keepdims=True)
    carry_ref[...] = jnp.broadcast_to(carry, carry_ref.shape)
    cnt_ref[...] = jnp.broadcast_to(carry, cnt_ref.shape).astype(I32)
    r1 = jnp.concatenate(r1, axis=1)
    r2 = jnp.concatenate(r2, axis=1)
    zf = jnp.zeros((4, tc), F32)
    ids_ref[...] = jnp.concatenate([e1, e2, r1, r2, zf], axis=0).astype(I32)
    wts_ref[...] = jnp.concatenate([w1, w2, jnp.zeros((6, tc), F32)], axis=0)


def _route(lt, bias_col):
    n = lt.shape[1]
    tc = min(ROUTE_TC, n)
    sub = min(ROUTE_SUB, tc)
    tri =jnp.asarray(np.triu(np.ones((sub, sub), np.float32), 1), BF16)
    return pl.pallas_call(
        functools.partial(_route_kernel, sub=sub),
        grid=(n // tc,),
        in_specs=[pl.BlockSpec((LANES, tc), lambda i: (0, i)),
                  pl.BlockSpec((LANES, LANES), lambda i: (0, 0)),
                  pl.BlockSpec((sub, sub), lambda i: (0, 0))],
        out_specs=[pl.BlockSpec((8, tc), lambda i: (0, i)),
                   pl.BlockSpec((8, tc), lambda i: (0, i)),
                   pl.BlockSpec((N_EXPERTS, LANES), lambda i: (0, 0))],
        out_shape=[jax.ShapeDtypeStruct((8, n), I32),
                   jax.ShapeDtypeStruct((8, n), F32),
                   jax.ShapeDtypeStruct((N_EXPERTS, LANES), I32)],
        scratch_shapes=[pltpu.VMEM((N_EXPERTS, LANES), F32)],
        compiler_params=_cp(("arbitrary",)),
        name="route",
    )(lt, bias_col, tri)


def _dest_kernel(ids_ref, ps_ref, o_ref):
    ids = ids_ref[...].astype(F32)
    tc = ids.shape[1]
    eio = lax.broadcasted_iota(I32, (N_EXPERTS, tc), 0).astype(F32)
    ps = ps_ref[:, 0:1]
    rows = [jnp.sum(jnp.where(eio == ids[k:k + 1], ps, 0.0), axis=0, keepdims=True) + ids[2 + k:3 + k]
            for k in range(2)]
    o_ref[...] = jnp.concatenate(rows + [jnp.zeros((6, tc), F32)], axis=0).astype(I32)


def _dest(ids, pad_start):
    n = ids.shape[1]
    tc = min(ROUTE_TC, n)
    ps = jnp.broadcast_to(pad_start.astype(F32)[:, None], (N_EXPERTS, LANES))
    dest = pl.pallas_call(
        _dest_kernel,
        grid=(n // tc,),
        in_specs=[pl.BlockSpec((8, tc), lambda i: (0, i)),
                  pl.BlockSpec((N_EXPERTS, LANES), lambda i: (0, 0))],
        out_specs=pl.BlockSpec((8, tc), lambda i: (0, i)),
        out_shape=jax.ShapeDtypeStruct((8, n), I32),
        compiler_params=_cp(("arbitrary",)),
        name="dest",
    )(ids, ps)
    return dest[:2].reshape(2 * n)


def _dispatch_kernel(dst0_ref, dst1_ref, ps_ref, cnt_ref, h_ref, xs_ref, zbuf, sem, zsem, *, tcd, tm, zr):
    @pl.when(pl.program_id(0) == 0)
    def _():
        zbuf[...] = jnp.zeros_like(zbuf)
        sizes = [zr >> b for b in range(zr.bit_length()) if (zr >> b) >= SUBLANES]

        def zero_rows(row, size):
            return pltpu.make_async_copy(zbuf.at[pl.ds(0, size)],
                                         xs_ref.at[pl.ds(pl.multiple_of(row, SUBLANES), size)], zsem)

        def zero_row(row):
            return pltpu.make_async_copy(zbuf.at[pl.ds(0, 1)], xs_ref.at[pl.ds(row, 1)], zsem)

        def fill(e, wait):
            cnt = cnt_ref[e]
            cnt8 = lax.div(cnt + (SUBLANES - 1), SUBLANES) * SUBLANES
            gap = lax.rem(tm - lax.rem(cnt8, tm), tm)
            base = ps_ref[e]

            def single(c, _):
                cp = zero_row(base + cnt + c)
                cp.wait() if wait else cp.start()
                return 0

            lax.fori_loop(0, cnt8 - cnt, single, 0)
            row = base + cnt8
            for size in sizes:
                has = lax.rem(lax.div(gap, size), 2) == 1

                @pl.when(has)
                def _():
                    cp = zero_rows(row, size)
                    cp.wait() if wait else cp.start()

                row = row + jnp.where(has, size, 0)

        last = N_EXPERTS - 1
        used_rows = ps_ref[last] + lax.div(cnt_ref[last] + (tm - 1), tm) * tm
        n_tail = lax.div(xs_ref.shape[0] - used_rows, zr)

        def tail(c, wait):
            cp = zero_rows(used_rows + c * zr, zr)
            cp.wait() if wait else cp.start()

        for wait in (False, True):
            lax.fori_loop(0, N_EXPERTS, lambda e, _: (fill(e, wait), 0)[1], 0)
            lax.fori_loop(0, n_tail, lambda c, _: (tail(c, wait), 0)[1], 0)

    def issue(t, _):
        for k, dst_ref in enumerate((dst0_ref, dst1_ref)):
            dst = dst_ref[t]
            pltpu.make_async_copy(h_ref.at[pl.ds(t, 1)], xs_ref.at[pl.ds(dst, 1)], sem).start(priority=k)
        return 0

    lax.fori_loop(0, tcd, issue, 0, unroll=8)
    for k in range(2):
        pltpu.make_async_copy(h_ref, xs_ref.at[pl.ds(0, tcd)], sem).wait()


def _dispatch(dest, pad_start, counts, h2, rows, tcd, tm):
    n, d = h2.shape
    assert tm & (tm - 1) == 0 and tm >= 2 * SUBLANES
    zr = tm // 2
    return pl.pallas_call(
        functools.partial(_dispatch_kernel, tcd=tcd, tm=tm, zr=zr),
        grid=(n // tcd,),
        in_specs=[pl.BlockSpec((tcd,), lambda i: (i,), memory_space=pltpu.SMEM),
                  pl.BlockSpec((tcd,), lambda i: (n // tcd + i,), memory_space=pltpu.SMEM),
                  pl.BlockSpec(memory_space=pltpu.SMEM),
                  pl.BlockSpec(memory_space=pltpu.SMEM),
                  pl.BlockSpec((tcd, d), lambda i: (i, 0))],
        out_specs=pl.BlockSpec(memory_space=pl.ANY),
        out_shape=jax.ShapeDtypeStruct((rows, d), h2.dtype),
        scratch_shapes=[pltpu.VMEM((zr, d), h2.dtype), pltpu.SemaphoreType.DMA(()),
                        pltpu.SemaphoreType.DMA(())],
        compiler_params=_cp(("arbitrary",)),
        name="dispatch",
    )(dest, dest, pad_start, counts, h2)


def _experts_kernel(be_ref, ord_ref, nxt_ref, nu_ref, xs_ref, wg_hbm, wu_hbm, wd_hbm, ys_ref,
                    g_f, u_f, d_f, g_s, u_s, d_s, sem):
    i = pl.program_id(0)
    used = i < nu_ref[0]
    e = be_ref[i]
    fresh = (i == 0) | (e != be_ref[jnp.maximum(i - 1, 0)])
    slot = lax.rem(ord_ref[i], 2)

    def fetch(expert, s):
        return [pltpu.make_async_copy(w.at[expert], f.at[s], sem.at[s, k])
                for k, (w, f) in enumerate(((wg_hbm, g_f), (wu_hbm, u_f), (wd_hbm, d_f)))]

    @pl.when(used & (i == 0))
    def _():
        for c in fetch(e, slot):
            c.start(priority=1)

    @pl.when(used & fresh)
    def _():
        for c in fetch(e, slot):
            c.wait()
        nxt = nxt_ref[i]

        @pl.when(nxt >= 0)
        def _():
            for c in fetch(nxt, 1 - slot):
                c.start(priority=1)

        g_s[...] = g_f[slot].astype(BF16)
        u_s[...] = u_f[slot].astype(BF16)
        d_s[...] = d_f[slot].astype(BF16)

    @pl.when(used)
    def _():
        x = xs_ref[...].astype(BF16)
        a = _silu(_dot(x, g_s[...])) * _dot(x, u_s[...])
        ys_ref[...] = _dot(a.astype(BF16), d_s[...])

    @pl.when(jnp.logical_not(used))
    def _():
        ys_ref[...] = jnp.zeros_like(ys_ref)


def _experts(blk_exp, blk_ord, blk_nxt, n_used, xs, w_gate, w_up, w_down, tm):
    d = xs.shape[1]
    de = w_gate.shape[2]
    nb = blk_exp.shape[0]
    row = lambda i, be, od, nx, nu: (jnp.minimum(i, nu[0] - 1), 0)
    anyspec = pl.BlockSpec(memory_space=pl.ANY)
    grid_spec = pltpu.PrefetchScalarGridSpec(
        num_scalar_prefetch=4,
        grid=(nb,),
        in_specs=[pl.BlockSpec((tm, d), row), anyspec, anyspec, anyspec],
        out_specs=pl.BlockSpec((tm, d), lambda i, be, od, nx, nu: (i, 0)),
        scratch_shapes=[pltpu.VMEM((2, d, de), F32), pltpu.VMEM((2, d, de), F32),
                        pltpu.VMEM((2, de, d), F32),
                        pltpu.VMEM((d, de), BF16), pltpu.VMEM((d, de), BF16),
                        pltpu.VMEM((de, d), BF16),
                        pltpu.SemaphoreType.DMA((2, 3))],
    )
    return pl.pallas_call(
        _experts_kernel,
        grid_spec=grid_spec,
        out_shape=jax.ShapeDtypeStruct((nb * tm, d), F32),
        compiler_params=_cp(("arbitrary",)),
        name="experts",
    )(blk_exp, blk_ord, blk_nxt, n_used, xs, w_gate, w_up, w_down)


def _combine_kernel(cur0_ref, cur1_ref, nxt0_ref, nxt1_ref, ys_ref, x1_ref, wt_ref, mod_ref, g_ref,
                    o_ref, buf, sem, *, tc):
    i = pl.program_id(0)
    slot = lax.rem(i, 2)

    def gather(id_refs, s):
        def issue(t, _):
            for k, id_ref in enumerate(id_refs):
                src = id_ref[t]
                pltpu.make_async_copy(ys_ref.at[pl.ds(src, 1)], buf.at[s, k, pl.ds(t, 1)],
                                      sem.at[s]).start(priority=k)
            return 0

        lax.fori_loop(0, tc, issue, 0, unroll=8)

    @pl.when(i == 0)
    def _():
        gather((cur0_ref, cur1_ref), 0)

    @pl.when(i + 1 < pl.num_programs(0))
    def _():
        gather((nxt0_ref, nxt1_ref), 1 - slot)

    for k in range(2):
        pltpu.make_async_copy(ys_ref.at[pl.ds(0, tc)], buf.at[slot, k], sem.at[slot]).wait()

    w_rows = wt_ref[...]
    w_cols = jnp.concatenate([w_rows, jnp.zeros((LANES - w_rows.shape[0], tc), F32)], axis=0).T
    moe = buf[slot, 0] * w_cols[:, 0:1] + buf[slot, 1] * w_cols[:, 1:2]
    x2 = x1_ref[...] + mod_ref[0, 5:6, :] * moe
    o_ref[...] = x2 * lax.rsqrt(jnp.mean(x2 * x2, axis=-1, keepdims=True) + EPS) * g_ref[...]


def _combine(dest, ys, x1, wts, mod3, final_g, seq, tc):
    n, d = x1.shape
    nt = n // tc
    tbl = lambda k, off: pl.BlockSpec((tc,), lambda i: (k * nt + jnp.minimum(i + off, nt - 1),),
                                      memory_space=pltpu.SMEM)
    return pl.pallas_call(
        functools.partial(_combine_kernel, tc=tc),
        grid=(nt,),
        in_specs=[tbl(0, 0), tbl(1, 0), tbl(0, 1), tbl(1, 1),
                  pl.BlockSpec(memory_space=pl.ANY),
                  pl.BlockSpec((tc, d), lambda i: (i, 0)),
                  pl.BlockSpec((8, tc), lambda i: (0, i)),
                  pl.BlockSpec((1, 6, d), lambda i: ((i * tc) // seq, 0, 0)),
                  pl.BlockSpec((1, d), lambda i: (0, 0))],
        out_specs=pl.BlockSpec((tc, d), lambda i: (i, 0)),
        out_shape=jax.ShapeDtypeStruct((n, d), F32),
        scratch_shapes=[pltpu.VMEM((2, 2, tc, d), F32), pltpu.SemaphoreType.DMA((2,))],
        compiler_params=_cp(("arbitrary",)),
        name="combine",
    )(dest, dest, dest, dest, ys, x1, wts, mod3, final_g.reshape(1, d))


def _token_mixer(x2, mod3, norm1_g, w_in, ret_gn_g, cmp_pos_k, cmp_w1_k, cmp_w2_k,
                 cmp_pos_v, cmp_w1_v, cmp_w2_v, bsz, seq):
    n, d = x2.shape
    w_t = w_in.T
    w_gate_t = jnp.pad(w_t[PROJ_MAIN:], ((0, LANES - N_GATE_COLS), (0, 0))).astype(BF16)
    proj, gate_logits = _inproj(x2, mod3, norm1_g, w_t, w_gate_t, seq)

    o_ret = _retention(proj, ret_gn_g, bsz, seq)

    g_ = NSA_KV_GROUPS
    crow = seq // CMP_STRIDE

    pe_flat = lambda pe: pe.reshape(1, CMP_BLOCK * NSA_D)
    kc, vc = _compress(proj, pe_flat(cmp_pos_k), pe_flat(cmp_pos_v),
                       cmp_w1_k, cmp_w1_v, cmp_w2_k, cmp_w2_v, bsz, seq)

    o_nsa = _nsa(proj, kc, vc, gate_logits, bsz, seq)
    return o_ret, o_nsa


def _moe(h2, lt, x1, mod3, final_g, b_grp, b_exp, w_gate, w_up, w_down, seq, tm):
    n, d = h2.shape
    bias_col = jnp.zeros((LANES,), F32).at[:N_GROUPS].set(b_grp).at[N_GROUPS:N_GROUPS + N_EXPERTS].set(b_exp)
    bias_col = jnp.broadcast_to(bias_col[:, None], (LANES, LANES))
    ids, wts, cnt = _route(lt, bias_col)

    counts = cnt[:, 0]
    padded = (counts + tm - 1) // tm * tm
    pad_end = jnp.cumsum(padded)
    pad_start = (pad_end - padded).astype(I32)
    nb = (2 * n) // tm + N_EXPERTS
    n_used = (pad_end[-1] // tm).astype(I32).reshape(1)
    blk_start = jnp.arange(nb, dtype=I32) * tm
    blk_exp = jnp.minimum(jnp.sum((pad_end[None, :] <= blk_start[:, None]).astype(I32), axis=1),
                          N_EXPERTS - 1).astype(I32)
    last_exp = blk_exp[jnp.maximum(n_used[0] - 1, 0)]
    blk_exp = jnp.where(jnp.arange(nb) < n_used[0], blk_exp, last_exp)
    eid = jnp.arange(N_EXPERTS, dtype=I32)
    has = counts > 0
    exp_ord = jnp.sum((has[None, :] & (eid[None, :] < eid[:, None])).astype(I32), axis=1)
    exp_nxt = jnp.min(jnp.where(has[None, :] & (eid[None, :] > eid[:, None]), eid[None, :], N_EXPERTS), axis=1)
    exp_nxt = jnp.where(exp_nxt < N_EXPERTS, exp_nxt, -1).astype(I32)
    blk_ord = exp_ord[blk_exp]
    blk_nxt = exp_nxt[blk_exp]

    dest = _dest(ids, pad_start)
    xs = _dispatch(dest, pad_start, counts.astype(I32), h2, nb * tm, min(DISPATCH_TC, n), tm)
    ys = _experts(blk_exp, blk_ord, blk_nxt, n_used, xs, w_gate, w_up, w_down, tm)
    return _combine(dest, ys, x1, wts, mod3, final_g, seq, min(COMBINE_TC, n))


def kernel(x, c, w_ada, b_ada, norm1_g, norm2_g, final_g, w_in, ret_gn_g, cmp_pos_k, cmp_w1_k,
           cmp_w2_k, cmp_pos_v, cmp_w1_v, cmp_w2_v, w_out, w_grp, b_grp, w_exp, b_exp, w_gate,
           w_up, w_down):
    bsz, seq, d = x.shape
    assert w_ada.shape[0] == 1, "single-layer block"
    n = bsz * seq
    x2 = x.reshape(n, d)
    mod3 = _ada(c, w_ada[0], b_ada[0]).reshape(bsz, 6, d)

    o_ret, o_nsa = _token_mixer(x2, mod3, norm1_g[0], w_in[0], ret_gn_g[0], cmp_pos_k[0],
                                cmp_w1_k[0], cmp_w2_k[0], cmp_pos_v[0], cmp_w1_v[0], cmp_w2_v[0],
                                bsz, seq)

    w_route = jnp.concatenate([w_grp[0], w_exp[0]], axis=1)
    w_route = jnp.pad(w_route, ((0, 0), (0, LANES - w_route.shape[1]))).T.astype(BF16)
    x1, h2, lt = _outproj(o_ret, o_nsa, x2, mod3, norm2_g[0], w_out[0].astype(BF16), w_route, seq)

    out = _moe(h2, lt, x1, mod3, final_g, b_grp[0], b_exp[0], w_gate[0], w_up[0], w_down[0],
               seq, MOE_TM)
    return out.reshape(bsz, seq, d)
```

```python
import functools

import numpy as np
import jax
import jax.numpy as jnp
from jax import lax
from jax.experimental import pallas as pl
from jax.experimental.pallas import tpu as pltpu

F32 = jnp.float32
BF16 = jnp.bfloat16
I32 = jnp.int32

RET_HEADS = 4
RET_DK = 256
RET_DV = 256
RET_CHUNK = 128
NSA_HEADS = 8
NSA_KV_GROUPS = 2
NSA_HPG = NSA_HEADS // NSA_KV_GROUPS
NSA_D = 128
CMP_BLOCK = 32
CMP_STRIDE = 16
SEL_BLOCK = 64
SEL_COUNT = 16
WIN = 512
N_GROUPS = 8
EXP_PER_GROUP = 8
N_EXPERTS = N_GROUPS * EXP_PER_GROUP
ROPE_BASE = 10000.0
EPS = 1e-6
NEG = -1e30
FORCE_BONUS = 1e4

LANES = 128
SUBLANES = 8
MXU_DIM = 256
VMEM_LIMIT = 52 * 1024 * 1024

ADA_TN = 1024
INPROJ_TM = 2048
INPROJ_TN = 512
RET_CHUNKS = 4
NSA_TQ = 512
NSA_TK = 512
NSA_VT = MXU_DIM
OUTPROJ_TM = 512
ROUTE_TC = 2048
ROUTE_SUB = 512
MOE_TM = 256
DISPATCH_TC = 1024
COMBINE_TC = 256

_C_NQ = 32
_C_KC, _C_VC, _C_KS, _C_VS, _C_KW, _C_VW = 40, 42, 44, 46, 48, 50
_C_GATE = 52
PROJ_MAIN = _C_GATE * LANES
N_GATE_COLS = NSA_HEADS * 3


def _cp(sem, vmem=VMEM_LIMIT):
    return pltpu.CompilerParams(dimension_semantics=sem, vmem_limit_bytes=vmem)


def _silu(v):
    return v * jax.nn.sigmoid(v)


def _dot(a, b):
    return jnp.dot(a, b, preferred_element_type=F32)


def _dot_nt(a, b):
    return lax.dot_general(a, b, (((1,), (1,)), ((), ())), preferred_element_type=F32)


def _dot_tn(a, b):
    return lax.dot_general(a, b, (((0,), (0,)), ((), ())), preferred_element_type=F32)


def _ada_kernel(c_ref, w_ref, b_ref, o_ref):
    ca = _silu(c_ref[...]).astype(BF16)
    o_ref[...] = _dot(ca, w_ref[...].astype(BF16)) + b_ref[...]


def _ada(c, w, b):
    bsz, d = c.shape
    n = w.shape[1]
    tn = min(ADA_TN, n)
    return pl.pallas_call(
        _ada_kernel,
        grid=(n // tn,),
        in_specs=[pl.BlockSpec((bsz, d), lambda j: (0, 0)),
                  pl.BlockSpec((d, tn), lambda j: (0, j)),
                  pl.BlockSpec((1, tn), lambda j: (0, j))],
        out_specs=pl.BlockSpec((bsz, tn), lambda j: (0, j)),
        out_shape=jax.ShapeDtypeStruct((bsz, n), F32),
        compiler_params=_cp(("arbitrary",)),
        name="ada",
    )(c, w, b.reshape(1, n))


def _inproj_kernel(x_hbm, mod_ref, g_ref, wt_ref, wgt_ref, proj_ref, gate_ref, h_ref, x_buf, x_sem):
    i = pl.program_id(0)
    tm = x_buf.shape[0]

    def x_copy(blk):
        return pltpu.make_async_copy(x_hbm.at[pl.ds(pl.multiple_of(blk * tm, tm), tm)], x_buf, x_sem)

    @pl.when(pl.program_id(1) == 0)
    def _():
        @pl.when(i == 0)
        def _():
            x_copy(i).start()

        x_copy(i).wait()
        x = x_buf[...]
        y = x * lax.rsqrt(jnp.mean(x * x, axis=-1, keepdims=True) + EPS) * g_ref[...]
        h = (y * (1.0 + mod_ref[0, 1:2, :]) + mod_ref[0, 0:1, :]).astype(BF16)
        h_ref[...] = h
        gate_ref[...] = _dot_nt(h, wgt_ref[...])

        @pl.when(i + 1 < pl.num_programs(0))
        def _():
            x_copy(i + 1).start(priority=1)

    proj_ref[...] = _dot_nt(h_ref[...], wt_ref[...].astype(BF16)).astype(BF16)


def _inproj(x2, mod3, g, w_t, wg_t, seq):
    n, d = x2.shape
    tm = min(INPROJ_TM, seq)
    tn = INPROJ_TN
    nj = PROJ_MAIN // tn
    return pl.pallas_call(
        _inproj_kernel,
        grid=(n // tm, nj),
        in_specs=[pl.BlockSpec(memory_space=pl.ANY),
                  pl.BlockSpec((1, 6, d), lambda i, j: ((i * tm) // seq, 0, 0)),
                  pl.BlockSpec((1, d), lambda i, j: (0, 0)),
                  pl.BlockSpec((tn, d), lambda i, j: (j, 0)),
                  pl.BlockSpec((LANES, d), lambda i, j: (0, 0))],
        out_specs=[pl.BlockSpec((tm, tn), lambda i, j: (i, j)),
                   pl.BlockSpec((tm, LANES), lambda i, j: (i, 0))],
        out_shape=[jax.ShapeDtypeStruct((n, PROJ_MAIN), BF16),
                   jax.ShapeDtypeStruct((n, LANES), F32)],
        scratch_shapes=[pltpu.VMEM((tm, d), BF16), pltpu.VMEM((tm, d), F32),
                        pltpu.SemaphoreType.DMA(())],
        compiler_params=_cp(("arbitrary", "arbitrary")),
        name="inproj",
    )(x2, mod3, g.reshape(1, d), w_t, wg_t)


def _retention_kernel(q_ref, k_ref, v_ref, g_ref, cos_ref, sin_ref, din_ref, zeta_ref,
                      qdec_ref, cdec_ref, gn_ref, o_ref, s_ref):
    @pl.when(pl.program_id(1) == 0)
    def _():
        s_ref[...] = jnp.zeros_like(s_ref)

    half = RET_DK // 2
    c = RET_CHUNK

    for sub in range(q_ref.shape[0] // c):
        rows = slice(sub * c, (sub + 1) * c)
        cos = cos_ref[rows, :]
        sin = sin_ref[rows, :]

        def rot(a):
            a1, a2 = a[:, :half], a[:, half:]
            return jnp.concatenate([a1 * cos - a2 * sin, a1 * sin + a2 * cos], axis=1)

        for h in range(RET_HEADS):
            qs = slice(h * RET_DK, (h + 1) * RET_DK)
            vs = slice(h * RET_DV, (h + 1) * RET_DV)
            q = rot(q_ref[rows, qs].astype(F32))
            k = rot(k_ref[rows, qs].astype(F32)) * (RET_DK ** -0.5)
            v = v_ref[rows, vs]
            qb = q.astype(BF16)
            kb = k.astype(BF16)
            s = _dot_nt(qb, kb) * din_ref[h]
            inner = _dot(s.astype(BF16), v)
            s_prev = s_ref[h]
            cross = _dot(qb, s_prev.astype(BF16)) * qdec_ref[h]
            kv = _dot_tn((k * zeta_ref[h]).astype(BF16), v)
            s_ref[h] = cdec_ref[h] * s_prev + kv
            o = inner + cross
            mu = jnp.mean(o, axis=-1, keepdims=True)
            oc = o - mu
            var = jnp.mean(oc * oc, axis=-1, keepdims=True)
            o = oc * lax.rsqrt(var + EPS) * gn_ref[:, vs]
            o_ref[rows, vs] = (o * _silu(g_ref[rows, vs].astype(F32))).astype(BF16)


def _retention(proj, gn_g, bsz, seq):
    n = proj.shape[0]
    c = RET_CHUNK
    nc = seq // c
    hw = RET_HEADS * RET_DK
    half = RET_DK // 2
    pos = jnp.arange(seq, dtype=F32)
    inv = ROPE_BASE ** (-jnp.arange(half, dtype=F32) / half)
    ang = pos[:, None] * inv[None, :]
    cos, sin = jnp.cos(ang), jnp.sin(ang)
    log_gamma = jnp.log1p(-jnp.exp2(-5.0 - jnp.arange(RET_HEADS, dtype=F32)))
    idx = jnp.arange(c, dtype=F32)
    rel = idx[:, None] - idx[None, :]
    decay_in = jnp.where(rel >= 0, jnp.exp(log_gamma[:, None, None] * jnp.maximum(rel, 0.0)), 0.0)
    zeta = jnp.exp(log_gamma[:, None] * (c - 1 - idx)[None, :])[:, :, None]
    q_decay = jnp.exp(log_gamma[:, None] * (idx + 1)[None, :])[:, :, None]
    chunk_decay = jnp.exp(log_gamma * c)[:, None, None]
    rb = min(RET_CHUNKS * c, seq)
    ns = seq // rb
    row = lambda b, t: (b * ns + t)
    return pl.pallas_call(
        _retention_kernel,
        grid=(bsz, ns),
        in_specs=[pl.BlockSpec((rb, hw), lambda b, t: (row(b, t), 0)),
                  pl.BlockSpec((rb, hw), lambda b, t: (row(b, t), 1)),
                  pl.BlockSpec((rb, hw), lambda b, t: (row(b, t), 2)),
                  pl.BlockSpec((rb, hw), lambda b, t: (row(b, t), 3)),
                  pl.BlockSpec((rb, half), lambda b, t: (t, 0)),
                  pl.BlockSpec((rb, half), lambda b, t: (t, 0)),
                  pl.BlockSpec((RET_HEADS, c, c), lambda b, t: (0, 0, 0)),
                  pl.BlockSpec((RET_HEADS, c, 1), lambda b, t: (0, 0, 0)),
                  pl.BlockSpec((RET_HEADS, c, 1), lambda b, t: (0, 0, 0)),
                  pl.BlockSpec((RET_HEADS, 1, 1), lambda b, t: (0, 0, 0)),
                  pl.BlockSpec((1, hw), lambda b, t: (0, 0))],
        out_specs=pl.BlockSpec((rb, hw), lambda b, t: (row(b, t), 0)),
        out_shape=jax.ShapeDtypeStruct((n, hw), BF16),
        scratch_shapes=[pltpu.VMEM((RET_HEADS, RET_DK, RET_DV), F32)],
        compiler_params=_cp(("arbitrary", "arbitrary")),
        name="retention",
    )(proj, proj, proj, proj, cos, sin, decay_in, zeta, q_decay, chunk_decay,
      gn_g.reshape(1, hw))


def _compress_kernel(ak_ref, av_ref, pek_ref, pev_ref, w1k_ref, w1v_ref, w2k_ref, w2v_ref,
                     kc_ref, vc_ref, a_scr):
    seq = ak_ref.shape[0]
    nblk = seq // CMP_STRIDE
    a_scr[seq:, :] = jnp.zeros((a_scr.shape[0] - seq, NSA_D), F32)

    def one(a_ref, pe_ref, w1_ref, w2_ref, o_ref):
        a_scr[:seq, :] = a_ref[...].astype(F32)
        flat = jnp.concatenate([a_scr[pl.ds(l, nblk, stride=CMP_STRIDE), :] for l in range(CMP_BLOCK)],
                               axis=1)
        pre = _dot((flat + pe_ref[...]).astype(BF16), w1_ref[...].astype(BF16))
        o_ref[...] = _dot(_silu(pre).astype(BF16), w2_ref[...].astype(BF16)).astype(BF16)

    one(ak_ref, pek_ref, w1k_ref, w2k_ref, kc_ref)
    one(av_ref, pev_ref, w1v_ref, w2v_ref, vc_ref)


def _compress(proj, pek, pev, w1k, w1v, w2k, w2v, bsz, seq):
    g_ = NSA_KV_GROUPS
    nblk = seq // CMP_STRIDE
    full = lambda a: pl.BlockSpec(a.shape, lambda i: (0,) * a.ndim)
    kblk = pl.BlockSpec((seq, NSA_D), lambda i: (i // g_, _C_KC + i % g_))
    vblk = pl.BlockSpec((seq, NSA_D), lambda i: (i // g_, _C_VC + i % g_))
    oblk = pl.BlockSpec((nblk, NSA_D), lambda i: (i, 0))
    return pl.pallas_call(
        _compress_kernel,
        grid=(bsz * g_,),
        in_specs=[kblk, vblk, full(pek), full(pev), full(w1k), full(w1v), full(w2k), full(w2v)],
        out_specs=[oblk, oblk],
        out_shape=[jax.ShapeDtypeStruct((bsz * g_ * nblk, NSA_D), BF16)] * 2,
        scratch_shapes=[pltpu.VMEM((seq + CMP_BLOCK, NSA_D), F32)],
        compiler_params=_cp(("arbitrary",)),
        name="compress",
    )(proj, proj, pek, pev, w1k, w1v, w2k, w2v)


def _nsa_kernel(q_ref, kc_ref, vc_ref, ks_ref, vs_ref, kw_ref, vw_ref, gate_ref, ovt_ref,
                o_ref, kaug_ref, vst_ref, vwt_ref, *, tq, tk, vt, seq, ncmp, wlen):
    crow = kc_ref.shape[0]
    i = pl.program_id(2)
    hg = NSA_HPG
    d = NSA_D
    r = hg * tq

    @pl.when(i == 0)
    def _():
        kaug_ref[:, :d] = ks_ref[...]
        blk = lax.broadcasted_iota(I32, (seq, LANES), 0) // SEL_BLOCK
        lane = lax.broadcasted_iota(I32, (seq, LANES), 1)
        kaug_ref[:, d:] = (blk == lane).astype(BF16)
        for c in range(seq // vt):
            vst_ref[c] = vs_ref[c * vt:(c + 1) * vt, :].astype(F32).T.astype(BF16)
            vwt_ref[c] = vw_ref[c * vt:(c + 1) * vt, :].astype(F32).T.astype(BF16)

    def pv_t(vt_ref, first_tile, p):
        out = None
        for c in range(p.shape[0] // vt):
            part = _dot(vt_ref[first_tile + c], p[c * vt:(c + 1) * vt, :])
            out = part if out is None else out + part
        return out

    q = q_ref[...]
    qh = [(q[:, h * d:(h + 1) * d].astype(F32) * (d ** -0.5)).astype(BF16) for h in range(hg)]
    qa = jnp.concatenate(qh, axis=0)
    t1 = i * tq + lax.broadcasted_iota(I32, (tq, 1), 0)
    tpos = jnp.concatenate([t1] * hg, axis=0)
    t_row = i * tq + lax.broadcasted_iota(I32, (1, tq), 1)
    tpos_row = jnp.concatenate([t_row] * hg, axis=1)

    w0 = pl.multiple_of(jnp.maximum(i * tq + tq - wlen, 0), vt)
    sw = _dot_nt(kw_ref[pl.ds(w0, wlen), :], qa)
    delta = tpos_row - (w0 + lax.broadcasted_iota(I32, (wlen, 1), 0))
    sw = jnp.where((delta >= 0) & (delta < WIN), sw, NEG)
    e_w = jnp.exp(sw - jnp.max(sw, axis=0, keepdims=True))
    o_win = (pv_t(vwt_ref, w0 // vt, e_w.astype(BF16)) / jnp.sum(e_w, axis=0, keepdims=True)).T

    sct = _dot_nt(kc_ref[...], qa)
    cidx = lax.broadcasted_iota(I32, (crow, 1), 0)
    cmask = (cidx * CMP_STRIDE + (CMP_BLOCK - 1) <= tpos_row) & (cidx < ncmp)
    sct = jnp.where(cmask, sct, NEG)
    e_c = jnp.where(cmask, jnp.exp(sct - jnp.max(sct, axis=0, keepdims=True)), 0.0)
    den_c = jnp.sum(e_c, axis=0, keepdims=True)
    p_t = jnp.where(den_c > 0.0, e_c / jnp.where(den_c > 0.0, den_c, 1.0), 0.0)
    o_cmp = _dot_tn(p_t.astype(BF16), vc_ref[...])

    psum_t = p_t[:, 0:tq]
    for h in range(1, hg):
        psum_t = psum_t + p_t[:, h * tq:(h + 1) * tq]
    nsel_blocks = seq // SEL_BLOCK
    nsb = ovt_ref.shape[0]
    imp_t = jnp.dot(ovt_ref[...], psum_t, preferred_element_type=F32,
                    precision=lax.Precision.HIGHEST)
    sidx = lax.broadcasted_iota(I32, (nsb, 1), 0)
    cur = t_row // SEL_BLOCK
    valid = sidx <= cur
    forced = (sidx == 0) | (sidx == cur) | (sidx == cur - 1)
    score = jnp.where(valid, imp_t + jnp.where(forced, FORCE_BONUS, 0.0), -1.0)
    rank = jnp.zeros((nsb, tq), F32)
    for s2 in range(nsel_blocks):
        row = score[s2:s2 + 1, :]
        beats = (row > score) | ((row == score) & (sidx > s2))
        rank = rank + beats.astype(F32)
    sel = valid & (rank < float(min(SEL_COUNT, nsel_blocks)))
    selb_t = jnp.where(sel, 0.0, NEG)
    selb_t = jnp.concatenate([selb_t, jnp.zeros((LANES - nsb, tq), F32)], axis=0)
    selb = selb_t.T.astype(BF16)
    q_aug = jnp.concatenate([jnp.concatenate([qh[h], selb], axis=1) for h in range(hg)], axis=0)

    n_full = (i * tq + 1) // tk
    kl = lax.broadcasted_iota(I32, (tk, 1), 0)

    def tile(j, carry, masked):
        m, l, acc = carry
        k0 = pl.multiple_of(j * tk, tk)
        s = _dot_nt(kaug_ref[pl.ds(k0, tk), :], q_aug)
        if masked:
            s = jnp.where(k0 + kl <= tpos_row, s, NEG)
        m_new = jnp.maximum(m, jnp.max(s, axis=0, keepdims=True))
        alpha = jnp.exp(m - m_new)
        p = jnp.exp(s - m_new)
        l = alpha * l + jnp.sum(p, axis=0, keepdims=True)
        acc = alpha * acc + pv_t(vst_ref, j * (tk // vt), p.astype(BF16))
        return m_new, l, acc

    m0 = jnp.full((1, r), NEG, F32)
    l0 = jnp.zeros((1, r), F32)
    a0 = jnp.zeros((d, r), F32)
    carry = lax.fori_loop(0, n_full, functools.partial(tile, masked=False), (m0, l0, a0))
    _, l_s, acc_s = tile(n_full, carry, True)
    o_slc = (acc_s / l_s).T

    gt = jax.nn.sigmoid(gate_ref[...])
    grp = pl.program_id(1)
    for g in range(1, NSA_KV_GROUPS):
        gt = jnp.where(grp == g, pltpu.roll(gt, LANES - g * 3 * hg, 1), gt)
    for h in range(hg):
        rows = slice(h * tq, (h + 1) * tq)
        o = (gt[:, 3 * h:3 * h + 1] * o_cmp[rows] + gt[:, 3 * h + 1:3 * h + 2] * o_slc[rows]
             + gt[:, 3 * h + 2:3 * h + 3] * o_win[rows])
        o_ref[:, h * d:(h + 1) * d] = o.astype(BF16)


def _nsa(proj, kc, vc, gates_g, bsz, seq):
    n = proj.shape[0]
    g_ = NSA_KV_GROUPS
    tq = NSA_TQ
    tk = min(NSA_TK, seq)
    nq = seq // tq
    ncmp = (seq - CMP_BLOCK) // CMP_STRIDE + 1
    wlen = min(WIN + tq, seq)
    nsel = seq // SEL_BLOCK
    crow = seq // CMP_STRIDE
    assert nsel <= LANES and crow <= LANES and tq <= tk and tk % tq == 0
    nsb = -(-nsel // 8) * 8
    ss = np.arange(nsb)[:, None] * SEL_BLOCK
    cs = np.arange(crow)[None, :] * CMP_STRIDE
    ov = ((cs < ss + SEL_BLOCK) & (cs + CMP_BLOCK > ss)
          & (np.arange(crow)[None, :] < ncmp) & (np.arange(nsb)[:, None] < nsel))
    ov = jnp.asarray(ov.astype(np.float32))
    kvspec = lambda c0: pl.BlockSpec((seq, NSA_D), lambda b, g, i: (b, c0 + g))
    vt = NSA_VT
    assert tk % vt == 0 and wlen % vt == 0 and tq % vt == 0 and seq % vt == 0
    kern = functools.partial(_nsa_kernel, tq=tq, tk=tk, vt=vt, seq=seq, ncmp=ncmp, wlen=wlen)
    return pl.pallas_call(
        kern,
        grid=(bsz, g_, nq),
        in_specs=[pl.BlockSpec((tq, NSA_HPG * NSA_D), lambda b, g, i: (b * nq + i, _C_NQ // NSA_HPG + g)),
                  pl.BlockSpec((crow, NSA_D), lambda b, g, i: (b * g_ + g, 0)),
                  pl.BlockSpec((crow, NSA_D), lambda b, g, i: (b * g_ + g, 0)),
                  kvspec(_C_KS), kvspec(_C_VS), kvspec(_C_KW), kvspec(_C_VW),
                  pl.BlockSpec((tq, LANES), lambda b, g, i: (b * nq + i, 0)),
                  pl.BlockSpec((nsb, crow), lambda b, g, i: (0, 0))],
        out_specs=pl.BlockSpec((tq, NSA_HPG * NSA_D), lambda b, g, i: (b * nq + i, g)),
        out_shape=jax.ShapeDtypeStruct((n, NSA_HEADS * NSA_D), BF16),
        scratch_shapes=[pltpu.VMEM((seq, 2 * NSA_D), BF16),
                        pltpu.VMEM((seq // vt, NSA_D, vt), BF16),
                        pltpu.VMEM((seq // vt, NSA_D, vt), BF16)],
        compiler_params=_cp(("arbitrary", "arbitrary", "arbitrary")),
        name="nsa",
    )(proj, kc, vc, proj, proj, proj, proj, gates_g, ov)


def _outproj_kernel(oret_ref, onsa_ref, x_ref, mod_ref, g_ref, w_ref, wr_ref,
                    x1_ref, h2_ref, lt_ref):
    hw = oret_ref.shape[1]
    mix = (_dot(oret_ref[...], w_ref[:hw, :].astype(BF16))
           + _dot(onsa_ref[...], w_ref[hw:, :].astype(BF16)))
    x1 = x_ref[...] + mod_ref[0, 2:3, :] * mix
    x1_ref[...] = x1
    y = x1 * lax.rsqrt(jnp.mean(x1 * x1, axis=-1, keepdims=True) + EPS) * g_ref[...]
    h2 = y * (1.0 + mod_ref[0, 4:5, :]) + mod_ref[0, 3:4, :]
    h2_ref[...] = h2
    lt_ref[...] = _dot_nt(wr_ref[...], h2.astype(BF16))


def _outproj(o_ret, o_nsa, x2, mod3, g, w, wr_bf, seq):
    n, d = x2.shape
    tm = min(OUTPROJ_TM, seq)
    hw = o_ret.shape[1]
    return pl.pallas_call(
        _outproj_kernel,
        grid=(n // tm,),
        in_specs=[pl.BlockSpec((tm, hw), lambda i: (i, 0)),
                  pl.BlockSpec((tm, o_nsa.shape[1]), lambda i: (i, 0)),
                  pl.BlockSpec((tm, d), lambda i: (i, 0)),
                  pl.BlockSpec((1, 6, d), lambda i: ((i * tm) // seq, 0, 0)),
                  pl.BlockSpec((1, d), lambda i: (0, 0)),
                  pl.BlockSpec(w.shape, lambda i: (0, 0), pipeline_mode=pl.Buffered(1)),
                  pl.BlockSpec(wr_bf.shape, lambda i: (0, 0))],
        out_specs=[pl.BlockSpec((tm, d), lambda i: (i, 0)),
                   pl.BlockSpec((tm, d), lambda i: (i, 0)),
                   pl.BlockSpec((LANES, tm), lambda i: (0, i))],
        out_shape=[jax.ShapeDtypeStruct((n, d), F32),
                   jax.ShapeDtypeStruct((n, d), F32),
                   jax.ShapeDtypeStruct((LANES, n), F32)],
        compiler_params=_cp(("arbitrary",)),
        name="outproj",
    )(o_ret, o_nsa, x2, mod3, g.reshape(1, d), w, wr_bf)


def _route_kernel(lt_ref, b_ref, tri_ref, ids_ref, wts_ref, cnt_ref, carry_ref, *, sub):
    @pl.when(pl.program_id(0) == 0)
    def _():
        carry_ref[...] = jnp.zeros_like(carry_ref)

    ng, ne = N_GROUPS, EXP_PER_GROUP
    l = lt_ref[...] + b_ref[:, 0:1]
    tc = l.shape[1]
    ridx = lax.broadcasted_iota(I32, (ng, tc), 0).astype(F32)

    def softmax0(v):
        e = jnp.exp(v - jnp.max(v, axis=0, keepdims=True))
        return e / jnp.sum(e, axis=0, keepdims=True)

    def top1(p):
        top = jnp.max(p, axis=0, keepdims=True)
        idx = jnp.min(jnp.where(p == top, ridx, float(ng)), axis=0, keepdims=True)
        return top, idx

    pg_top, grp = top1(softmax0(l[0:ng]))
    leg = jnp.zeros((ne, tc), F32)
    for g in range(ng):
        leg = jnp.where(grp == float(g), l[ng + g * ne:ng + (g + 1) * ne], leg)
    pe = softmax0(leg)
    p1, i1 = top1(pe)
    p2, i2 = top1(jnp.where(ridx == i1, -1.0, pe))
    den = p1 + p2
    w1 = pg_top * p1 / den
    w2 = pg_top * p2 / den
    e1 = grp * float(ne) + i1
    e2 = grp * float(ne) + i2

    eio = lax.broadcasted_iota(I32, (N_EXPERTS, sub), 0).astype(F32)
    r1 = []
    r2 = []
    carry = carry_ref[:, 0:1]
    for c in range(tc // sub):
        cs = slice(c * sub, (c + 1) * sub)
        oh1 = (eio == e1[:, cs]).astype(F32)
        oh2 = (eio == e2[:, cs]).astype(F32)
        oh = oh1 + oh2
        before = carry + _dot(oh.astype(BF16), tri_ref[...])
        r1.append(jnp.sum(oh1 * before, axis=0, keepdims=True))
        r2.append(jnp.sum(oh2 * before, axis=0, keepdims=True))
        carry = carry + jnp.sum(oh, axis=1, keepdims=True)
    carry_ref[...] = jnp.broadcast_to(carry, carry_ref.shape)
    cnt_ref[...] = jnp.broadcast_to(carry, cnt_ref.shape).astype(I32)
    r1 = jnp.concatenate(r1, axis=1)
    r2 = jnp.concatenate(r2, axis=1)
    zf = jnp.zeros((4, tc), F32)
    ids_ref[...] = jnp.concatenate([e1, e2, r1, r2, zf], axis=0).astype(I32)
    wts_ref[...] = jnp.concatenate([w1, w2, jnp.zeros((6, tc), F32)], axis=0)


def _route(lt, bias_col):
    n = lt.shape[1]
    tc = min(ROUTE_TC, n)
    sub = min(ROUTE_SUB, tc)
    tri =jnp.asarray(np.triu(np.ones((sub, sub), np.float32), 1), BF16)
    return pl.pallas_call(
        functools.partial(_route_kernel, sub=sub),
        grid=(n // tc,),
        in_specs=[pl.BlockSpec((LANES, tc), lambda i: (0, i)),
                  pl.BlockSpec((LANES, LANES), lambda i: (0, 0)),
                  pl.BlockSpec((sub, sub), lambda i: (0, 0))],
        out_specs=[pl.BlockSpec((8, tc), lambda i: (0, i)),
                   pl.BlockSpec((8, tc), lambda i: (0, i)),
                   pl.BlockSpec((N_EXPERTS, LANES), lambda i: (0, 0))],
        out_shape=[jax.ShapeDtypeStruct((8, n), I32),
                   jax.ShapeDtypeStruct((8, n), F32),
                   jax.ShapeDtypeStruct((N_EXPERTS, LANES), I32)],
        scratch_shapes=[pltpu.VMEM((N_EXPERTS, LANES), F32)],
        compiler_params=_cp(("arbitrary",)),
        name="route",
    )(lt, bias_col, tri)


def _dest_kernel(ids_ref, ps_ref, o_ref):
    ids = ids_ref[...].astype(F32)
    tc = ids.shape[1]
    eio = lax.broadcasted_iota(I32, (N_EXPERTS, tc), 0).astype(F32)
    ps = ps_ref[:, 0:1]
    rows = [jnp.sum(jnp.where(eio == ids[k:k + 1], ps, 0.0), axis=0, keepdims=True) + ids[2 + k:3 + k]
            for k in range(2)]
    o_ref[...] = jnp.concatenate(rows + [jnp.zeros((6, tc), F32)], axis=0).astype(I32)


def _dest(ids, pad_start):
    n = ids.shape[1]
    tc = min(ROUTE_TC, n)
    ps = jnp.broadcast_to(pad_start.astype(F32)[:, None], (N_EXPERTS, LANES))
    dest = pl.pallas_call(
        _dest_kernel,
        grid=(n // tc,),
        in_specs=[pl.BlockSpec((8, tc), lambda i: (0, i)),
                  pl.BlockSpec((N_EXPERTS, LANES), lambda i: (0, 0))],
        out_specs=pl.BlockSpec((8, tc), lambda i: (0, i)),
        out_shape=jax.ShapeDtypeStruct((8, n), I32),
        compiler_params=_cp(("arbitrary",)),
        name="dest",
    )(ids, ps)
    return dest[:2].reshape(2 * n)


def _dispatch_kernel(dst0_ref, dst1_ref, ps_ref, cnt_ref, h_ref, xs_ref, zbuf, sem, zsem, *, tcd, tm, zr):
    @pl.when(pl.program_id(0) == 0)
    def _():
        zbuf[...] = jnp.zeros_like(zbuf)
        sizes = [zr >> b for b in range(zr.bit_length()) if (zr >> b) >= SUBLANES]

        def zero_rows(row, size):
            return pltpu.make_async_copy(zbuf.at[pl.ds(0, size)],
                                         xs_ref.at[pl.ds(pl.multiple_of(row, SUBLANES), size)], zsem)

        def zero_row(row):
            return pltpu.make_async_copy(zbuf.at[pl.ds(0, 1)], xs_ref.at[pl.ds(row, 1)], zsem)

        def fill(e, wait):
            cnt = cnt_ref[e]
            cnt8 = lax.div(cnt + (SUBLANES - 1), SUBLANES) * SUBLANES
            gap = lax.rem(tm - lax.rem(cnt8, tm), tm)
            base = ps_ref[e]

            def single(c, _):
                cp = zero_row(base + cnt + c)
                cp.wait() if wait else cp.start()
                return 0

            lax.fori_loop(0, cnt8 - cnt, single, 0)
            row = base + cnt8
            for size in sizes:
                has = lax.rem(lax.div(gap, size), 2) == 1

                @pl.when(has)
                def _():
                    cp = zero_rows(row, size)
                    cp.wait() if wait else cp.start()

                row = row + jnp.where(has, size, 0)

        last = N_EXPERTS - 1
        used_rows = ps_ref[last] + lax.div(cnt_ref[last] + (tm - 1), tm) * tm
        n_tail = lax.div(xs_ref.shape[0] - used_rows, zr)

        def tail(c, wait):
            cp = zero_rows(used_rows + c * zr, zr)
            cp.wait() if wait else cp.start()

        for wait in (False, True):
            lax.fori_loop(0, N_EXPERTS, lambda e, _: (fill(e, wait), 0)[1], 0)
            lax.fori_loop(0, n_tail, lambda c, _: (tail(c, wait), 0)[1], 0)

    def issue(t, _):
        for k, dst_ref in enumerate((dst0_ref, dst1_ref)):
            dst = dst_ref[t]
            pltpu.make_async_copy(h_ref.at[pl.ds(t, 1)], xs_ref.at[pl.ds(dst, 1)], sem).start(priority=k)
        return 0

    lax.fori_loop(0, tcd, issue, 0, unroll=8)
    for k in range(2):
        pltpu.make_async_copy(h_ref, xs_ref.at[pl.ds(0, tcd)], sem).wait()


def _dispatch(dest, pad_start, counts, h2, rows, tcd, tm):
    n, d = h2.shape
    assert tm & (tm - 1) == 0 and tm >= 2 * SUBLANES
    zr = tm // 2
    return pl.pallas_call(
        functools.partial(_dispatch_kernel, tcd=tcd, tm=tm, zr=zr),
        grid=(n // tcd,),
        in_specs=[pl.BlockSpec((tcd,), lambda i: (i,), memory_space=pltpu.SMEM),
                  pl.BlockSpec((tcd,), lambda i: (n // tcd + i,), memory_space=pltpu.SMEM),
                  pl.BlockSpec(memory_space=pltpu.SMEM),
                  pl.BlockSpec(memory_space=pltpu.SMEM),
                  pl.BlockSpec((tcd, d), lambda i: (i, 0))],
        out_specs=pl.BlockSpec(memory_space=pl.ANY),
        out_shape=jax.ShapeDtypeStruct((rows, d), h2.dtype),
        scratch_shapes=[pltpu.VMEM((zr, d), h2.dtype), pltpu.SemaphoreType.DMA(()),
                        pltpu.SemaphoreType.DMA(())],
        compiler_params=_cp(("arbitrary",)),
        name="dispatch",
    )(dest, dest, pad_start, counts, h2)


def _experts_kernel(be_ref, ord_ref, nxt_ref, nu_ref, xs_ref, wg_hbm, wu_hbm, wd_hbm, ys_ref,
                    g_f, u_f, d_f, g_s, u_s, d_s, sem):
    i = pl.program_id(0)
    used = i < nu_ref[0]
    e = be_ref[i]
    fresh = (i == 0) | (e != be_ref[jnp.maximum(i - 1, 0)])
    slot = lax.rem(ord_ref[i], 2)

    def fetch(expert, s):
        return [pltpu.make_async_copy(w.at[expert], f.at[s], sem.at[s, k])
                for k, (w, f) in enumerate(((wg_hbm, g_f), (wu_hbm, u_f), (wd_hbm, d_f)))]

    @pl.when(used & (i == 0))
    def _():
        for c in fetch(e, slot):
            c.start(priority=1)

    @pl.when(used & fresh)
    def _():
        for c in fetch(e, slot):
            c.wait()
        nxt = nxt_ref[i]

        @pl.when(nxt >= 0)
        def _():
            for c in fetch(nxt, 1 - slot):
                c.start(priority=1)

        g_s[...] = g_f[slot].astype(BF16)
        u_s[...] = u_f[slot].astype(BF16)
        d_s[...] = d_f[slot].astype(BF16)

    @pl.when(used)
    def _():
        x = xs_ref[...].astype(BF16)
        a = _silu(_dot(x, g_s[...])) * _dot(x, u_s[...])
        ys_ref[...] = _dot(a.astype(BF16), d_s[...])

    @pl.when(jnp.logical_not(used))
    def _():
        ys_ref[...] = jnp.zeros_like(ys_ref)


def _experts(blk_exp, blk_ord, blk_nxt, n_used, xs, w_gate, w_up, w_down, tm):
    d = xs.shape[1]
    de = w_gate.shape[2]
    nb = blk_exp.shape[0]
    row = lambda i, be, od, nx, nu: (jnp.minimum(i, nu[0] - 1), 0)
    anyspec = pl.BlockSpec(memory_space=pl.ANY)
    grid_spec = pltpu.PrefetchScalarGridSpec(
        num_scalar_prefetch=4,
        grid=(nb,),
        in_specs=[pl.BlockSpec((tm, d), row), anyspec, anyspec, anyspec],
        out_specs=pl.BlockSpec((tm, d), lambda i, be, od, nx, nu: (i, 0)),
        scratch_shapes=[pltpu.VMEM((2, d, de), F32), pltpu.VMEM((2, d, de), F32),
                        pltpu.VMEM((2, de, d), F32),
                        pltpu.VMEM((d, de), BF16), pltpu.VMEM((d, de), BF16),
                        pltpu.VMEM((de, d), BF16),
                        pltpu.SemaphoreType.DMA((2, 3))],
    )
    return pl.pallas_call(
        _experts_kernel,
        grid_spec=grid_spec,
        out_shape=jax.ShapeDtypeStruct((nb * tm, d), F32),
        compiler_params=_cp(("arbitrary",)),
        name="experts",
    )(blk_exp, blk_ord, blk_nxt, n_used, xs, w_gate, w_up, w_down)


def _combine_kernel(cur0_ref, cur1_ref, nxt0_ref, nxt1_ref, ys_ref, x1_ref, wt_ref, mod_ref, g_ref,
                    o_ref, buf, sem, *, tc):
    i = pl.program_id(0)
    slot = lax.rem(i, 2)

    def gather(id_refs, s):
        def issue(t, _):
            for k, id_ref in enumerate(id_refs):
                src = id_ref[t]
                pltpu.make_async_copy(ys_ref.at[pl.ds(src, 1)], buf.at[s, k, pl.ds(t, 1)],
                                      sem.at[s]).start(priority=k)
            return 0

        lax.fori_loop(0, tc, issue, 0, unroll=8)

    @pl.when(i == 0)
    def _():
        gather((cur0_ref, cur1_ref), 0)

    @pl.when(i + 1 < pl.num_programs(0))
    def _():
        gather((nxt0_ref, nxt1_ref), 1 - slot)

    for k in range(2):
        pltpu.make_async_copy(ys_ref.at[pl.ds(0, tc)], buf.at[slot, k], sem.at[slot]).wait()

    w_rows = wt_ref[...]
    w_cols = jnp.concatenate([w_rows, jnp.zeros((LANES - w_rows.shape[0], tc), F32)], axis=0).T
    moe = buf[slot, 0] * w_cols[:, 0:1] + buf[slot, 1] * w_cols[:, 1:2]
    x2 = x1_ref[...] + mod_ref[0, 5:6, :] * moe
    o_ref[...] = x2 * lax.rsqrt(jnp.mean(x2 * x2, axis=-1, keepdims=True) + EPS) * g_ref[...]


def _combine(dest, ys, x1, wts, mod3, final_g, seq, tc):
    n, d = x1.shape
    nt = n // tc
    tbl = lambda k, off: pl.BlockSpec((tc,), lambda i: (k * nt + jnp.minimum(i + off, nt - 1),),
                                      memory_space=pltpu.SMEM)
    return pl.pallas_call(
        functools.partial(_combine_kernel, tc=tc),
        grid=(nt,),
        in_specs=[tbl(0, 0), tbl(1, 0), tbl(0, 1), tbl(1, 1),
                  pl.BlockSpec(memory_space=pl.ANY),
                  pl.BlockSpec((tc, d), lambda i: (i, 0)),
                  pl.BlockSpec((8, tc), lambda i: (0, i)),
                  pl.BlockSpec((1, 6, d), lambda i: ((i * tc) // seq, 0, 0)),
                  pl.BlockSpec((1, d), lambda i: (0, 0))],
        out_specs=pl.BlockSpec((tc, d), lambda i: (i, 0)),
        out_shape=jax.ShapeDtypeStruct((n, d), F32),
        scratch_shapes=[pltpu.VMEM((2, 2, tc, d), F32), pltpu.SemaphoreType.DMA((2,))],
        compiler_params=_cp(("arbitrary",)),
        name="combine",
    )(dest, dest, dest, dest, ys, x1, wts, mod3, final_g.reshape(1, d))


def _token_mixer(x2, mod3, norm1_g, w_in, ret_gn_g, cmp_pos_k, cmp_w1_k, cmp_w2_k,
                 cmp_pos_v, cmp_w1_v, cmp_w2_v, bsz, seq):
    n, d = x2.shape
    w_t = w_in.T
    w_gate_t = jnp.pad(w_t[PROJ_MAIN:], ((0, LANES - N_GATE_COLS), (0, 0))).astype(BF16)
    proj, gate_logits = _inproj(x2, mod3, norm1_g, w_t, w_gate_t, seq)

    o_ret = _retention(proj, ret_gn_g, bsz, seq)

    g_ = NSA_KV_GROUPS
    crow = seq // CMP_STRIDE

    pe_flat = lambda pe: pe.reshape(1, CMP_BLOCK * NSA_D)
    kc, vc = _compress(proj, pe_flat(cmp_pos_k), pe_flat(cmp_pos_v),
                       cmp_w1_k, cmp_w1_v, cmp_w2_k, cmp_w2_v, bsz, seq)

    o_nsa = _nsa(proj, kc, vc, gate_logits, bsz, seq)
    return o_ret, o_nsa


def _moe(h2, lt, x1, mod3, final_g, b_grp, b_exp, w_gate, w_up, w_down, seq, tm):
    n, d = h2.shape
    bias_col = jnp.zeros((LANES,), F32).at[:N_GROUPS].set(b_grp).at[N_GROUPS:N_GROUPS + N_EXPERTS].set(b_exp)
    bias_col = jnp.broadcast_to(bias_col[:, None], (LANES, LANES))
    ids, wts, cnt = _route(lt, bias_col)

    counts = cnt[:, 0]
    padded = (counts + tm - 1) // tm * tm
    pad_end = jnp.cumsum(padded)
    pad_start = (pad_end - padded).astype(I32)
    nb = (2 * n) // tm + N_EXPERTS
    n_used = (pad_end[-1] // tm).astype(I32).reshape(1)
    blk_start = jnp.arange(nb, dtype=I32) * tm
    blk_exp = jnp.minimum(jnp.sum((pad_end[None, :] <= blk_start[:, None]).astype(I32), axis=1),
                          N_EXPERTS - 1).astype(I32)
    last_exp = blk_exp[jnp.maximum(n_used[0] - 1, 0)]
    blk_exp = jnp.where(jnp.arange(nb) < n_used[0], blk_exp, last_exp)
    eid = jnp.arange(N_EXPERTS, dtype=I32)
    has = counts > 0
    exp_ord = jnp.sum((has[None, :] & (eid[None, :] < eid[:, None])).astype(I32), axis=1)
    exp_nxt = jnp.min(jnp.where(has[None, :] & (eid[None, :] > eid[:, None]), eid[None, :], N_EXPERTS), axis=1)
    exp_nxt = jnp.where(exp_nxt < N_EXPERTS, exp_nxt, -1).astype(I32)
    blk_ord = exp_ord[blk_exp]
    blk_nxt = exp_nxt[blk_exp]

    dest = _dest(ids, pad_start)
    xs = _dispatch(dest, pad_start, counts.astype(I32), h2, nb * tm, min(DISPATCH_TC, n), tm)
    ys = _experts(blk_exp, blk_ord, blk_nxt, n_used, xs, w_gate, w_up, w_down, tm)
    return _combine(dest, ys, x1, wts, mod3, final_g, seq, min(COMBINE_TC, n))


def kernel(x, c, w_ada, b_ada, norm1_g, norm2_g, final_g, w_in, ret_gn_g, cmp_pos_k, cmp_w1_k,
           cmp_w2_k, cmp_pos_v, cmp_w1_v, cmp_w2_v, w_out, w_grp, b_grp, w_exp, b_exp, w_gate,
           w_up, w_down):
    bsz, seq, d = x.shape
    assert w_ada.shape[0] == 1, "single-layer block"
    n = bsz * seq
    x2 = x.reshape(n, d)
    mod3 = _ada(c, w_ada[0], b_ada[0]).reshape(bsz, 6, d)

    o_ret, o_nsa = _token_mixer(x2, mod3, norm1_g[0], w_in[0], ret_gn_g[0], cmp_pos_k[0],
                                cmp_w1_k[0], cmp_w2_k[0], cmp_pos_v[0], cmp_w1_v[0], cmp_w2_v[0],
                                bsz, seq)

    w_route = jnp.concatenate([w_grp[0], w_exp[0]], axis=1)
    w_route = jnp.pad(w_route, ((0, 0), (0, LANES - w_route.shape[1]))).T.astype(BF16)
    x1, h2, lt = _outproj(o_ret, o_nsa, x2, mod3, norm2_g[0], w_out[0], w_route, seq)

    out = _moe(h2, lt, x1, mod3, final_g, b_grp[0], b_exp[0], w_gate[0], w_up[0], w_down[0],
               seq, MOE_TM)
    return out.reshape(bsz, seq, d)
```

```python
import functools

import numpy as np
import jax
import jax.numpy as jnp
from jax import lax
from jax.experimental import pallas as pl
from jax.experimental.pallas import tpu as pltpu

F32 = jnp.float32
BF16 = jnp.bfloat16
I32 = jnp.int32

RET_HEADS = 4
RET_DK = 256
RET_DV = 256
RET_CHUNK = 128
NSA_HEADS = 8
NSA_KV_GROUPS = 2
NSA_HPG = NSA_HEADS // NSA_KV_GROUPS
NSA_D = 128
CMP_BLOCK = 32
CMP_STRIDE = 16
SEL_BLOCK = 64
SEL_COUNT = 16
WIN = 512
N_GROUPS = 8
EXP_PER_GROUP = 8
N_EXPERTS = N_GROUPS * EXP_PER_GROUP
ROPE_BASE = 10000.0
EPS = 1e-6
NEG = -1e30
FORCE_BONUS = 1e4

LANES = 128
SUBLANES = 8
MXU_DIM = 256
VMEM_LIMIT = 52 * 1024 * 1024

ADA_TN = 1024
INPROJ_TM = 2048
INPROJ_TN = 512
RET_CHUNKS = 4
NSA_TQ = 512
NSA_TK = 512
NSA_VT = MXU_DIM
OUTPROJ_TM = 512
ROUTE_TC = 2048
ROUTE_SUB = 512
MOE_TM = 256
DISPATCH_TC = 1024
COMBINE_TC = 256

_C_NQ = 32
_C_KC, _C_VC, _C_KS, _C_VS, _C_KW, _C_VW = 40, 42, 44, 46, 48, 50
_C_GATE = 52
PROJ_MAIN = _C_GATE * LANES
N_GATE_COLS = NSA_HEADS * 3


def _cp(sem, vmem=VMEM_LIMIT):
    return pltpu.CompilerParams(dimension_semantics=sem, vmem_limit_bytes=vmem)


def _silu(v):
    return v * jax.nn.sigmoid(v)


def _dot(a, b):
    return jnp.dot(a, b, preferred_element_type=F32)


def _dot_nt(a, b):
    return lax.dot_general(a, b, (((1,), (1,)), ((), ())), preferred_element_type=F32)


def _dot_tn(a, b):
    return lax.dot_general(a, b, (((0,), (0,)), ((), ())), preferred_element_type=F32)


def _ada_kernel(c_ref, w_ref, b_ref, o_ref):
    ca = _silu(c_ref[...]).astype(BF16)
    o_ref[...] = _dot(ca, w_ref[...].astype(BF16)) + b_ref[...]


def _ada(c, w, b):
    bsz, d = c.shape
    n = w.shape[1]
    tn = min(ADA_TN, n)
    return pl.pallas_call(
        _ada_kernel,
        grid=(n // tn,),
        in_specs=[pl.BlockSpec((bsz, d), lambda j: (0, 0)),
                  pl.BlockSpec((d, tn), lambda j: (0, j)),
                  pl.BlockSpec((1, tn), lambda j: (0, j))],
        out_specs=pl.BlockSpec((bsz, tn), lambda j: (0, j)),
        out_shape=jax.ShapeDtypeStruct((bsz, n), F32),
        compiler_params=_cp(("arbitrary",)),
        name="ada",
    )(c, w, b.reshape(1, n))


def _inproj_kernel(x_hbm, mod_ref, g_ref, wt_ref, wgt_ref, proj_ref, gate_ref, h_ref, x_buf, x_sem):
    i = pl.program_id(0)
    tm = x_buf.shape[0]

    def x_copy(blk):
        return pltpu.make_async_copy(x_hbm.at[pl.ds(pl.multiple_of(blk * tm, tm), tm)], x_buf, x_sem)

    @pl.when(pl.program_id(1) == 0)
    def _():
        @pl.when(i == 0)
        def _():
            x_copy(i).start()

        x_copy(i).wait()
        x = x_buf[...]
        y = x * lax.rsqrt(jnp.mean(x * x, axis=-1, keepdims=True) + EPS) * g_ref[...]
        h = (y * (1.0 + mod_ref[0, 1:2, :]) + mod_ref[0, 0:1, :]).astype(BF16)
        h_ref[...] = h
        gate_ref[...] = _dot_nt(h, wgt_ref[...])

        @pl.when(i + 1 < pl.num_programs(0))
        def _():
            x_copy(i + 1).start(priority=1)

    proj_ref[...] = _dot_nt(h_ref[...], wt_ref[...].astype(BF16)).astype(BF16)


def _inproj(x2, mod3, g, w_t, wg_t, seq):
    n, d = x2.shape
    tm = min(INPROJ_TM, seq)
    tn = INPROJ_TN
    nj = PROJ_MAIN // tn
    return pl.pallas_call(
        _inproj_kernel,
        grid=(n // tm, nj),
        in_specs=[pl.BlockSpec(memory_space=pl.ANY),
                  pl.BlockSpec((1, 6, d), lambda i, j: ((i * tm) // seq, 0, 0)),
                  pl.BlockSpec((1, d), lambda i, j: (0, 0)),
                  pl.BlockSpec((tn, d), lambda i, j: (j, 0)),
                  pl.BlockSpec((LANES, d), lambda i, j: (0, 0))],
        out_specs=[pl.BlockSpec((tm, tn), lambda i, j: (i, j)),
                   pl.BlockSpec((tm, LANES), lambda i, j: (i, 0))],
        out_shape=[jax.ShapeDtypeStruct((n, PROJ_MAIN), BF16),
                   jax.ShapeDtypeStruct((n, LANES), F32)],
        scratch_shapes=[pltpu.VMEM((tm, d), BF16), pltpu.VMEM((tm, d), F32),
                        pltpu.SemaphoreType.DMA(())],
        compiler_params=_cp(("arbitrary", "arbitrary")),
        name="inproj",
    )(x2, mod3, g.reshape(1, d), w_t, wg_t)


def _retention_kernel(q_ref, k_ref, v_ref, g_ref, cos_ref, sin_ref, din_ref, zeta_ref,
                      qdec_ref, cdec_ref, gn_ref, o_ref, s_ref):
    @pl.when(pl.program_id(1) == 0)
    def _():
        s_ref[...] = jnp.zeros_like(s_ref)

    half = RET_DK // 2
    c = RET_CHUNK

    for sub in range(q_ref.shape[0] // c):
        rows = slice(sub * c, (sub + 1) * c)
        cos = cos_ref[rows, :]
        sin = sin_ref[rows, :]

        def rot(a):
            a1, a2 = a[:, :half], a[:, half:]
            return jnp.concatenate([a1 * cos - a2 * sin, a1 * sin + a2 * cos], axis=1)

        for h in range(RET_HEADS):
            qs = slice(h * RET_DK, (h + 1) * RET_DK)
            vs = slice(h * RET_DV, (h + 1) * RET_DV)
            q = rot(q_ref[rows, qs].astype(F32))
            k = rot(k_ref[rows, qs].astype(F32)) * (RET_DK ** -0.5)
            v = v_ref[rows, vs]
            qb = q.astype(BF16)
            kb = k.astype(BF16)
            s = _dot_nt(qb, kb) * din_ref[h]
            inner = _dot(s.astype(BF16), v)
            s_prev = s_ref[h]
            cross = _dot(qb, s_prev.astype(BF16)) * qdec_ref[h]
            kv = _dot_tn((k * zeta_ref[h]).astype(BF16), v)
            s_ref[h] = cdec_ref[h] * s_prev + kv
            o = inner + cross
            mu = jnp.mean(o, axis=-1, keepdims=True)
            oc = o - mu
            var = jnp.mean(oc * oc, axis=-1, keepdims=True)
            o = oc * lax.rsqrt(var + EPS) * gn_ref[:, vs]
            o_ref[rows, vs] = (o * _silu(g_ref[rows, vs].astype(F32))).astype(BF16)


def _retention(proj, gn_g, bsz, seq):
    n = proj.shape[0]
    c = RET_CHUNK
    nc = seq // c
    hw = RET_HEADS * RET_DK
    half = RET_DK // 2
    pos = jnp.arange(seq, dtype=F32)
    inv = ROPE_BASE ** (-jnp.arange(half, dtype=F32) / half)
    ang = pos[:, None] * inv[None, :]
    cos, sin = jnp.cos(ang), jnp.sin(ang)
    log_gamma = jnp.log1p(-jnp.exp2(-5.0 - jnp.arange(RET_HEADS, dtype=F32)))
    idx = jnp.arange(c, dtype=F32)
    rel = idx[:, None] - idx[None, :]
    decay_in = jnp.where(rel >= 0, jnp.exp(log_gamma[:, None, None] * jnp.maximum(rel, 0.0)), 0.0)
    zeta = jnp.exp(log_gamma[:, None] * (c - 1 - idx)[None, :])[:, :, None]
    q_decay = jnp.exp(log_gamma[:, None] * (idx + 1)[None, :])[:, :, None]
    chunk_decay = jnp.exp(log_gamma * c)[:, None, None]
    rb = min(RET_CHUNKS * c, seq)
    ns = seq // rb
    row = lambda b, t: (b * ns + t)
    return pl.pallas_call(
        _retention_kernel,
        grid=(bsz, ns),
        in_specs=[pl.BlockSpec((rb, hw), lambda b, t: (row(b, t), 0)),
                  pl.BlockSpec((rb, hw), lambda b, t: (row(b, t), 1)),
                  pl.BlockSpec((rb, hw), lambda b, t: (row(b, t), 2)),
                  pl.BlockSpec((rb, hw), lambda b, t: (row(b, t), 3)),
                  pl.BlockSpec((rb, half), lambda b, t: (t, 0)),
                  pl.BlockSpec((rb, half), lambda b, t: (t, 0)),
                  pl.BlockSpec((RET_HEADS, c, c), lambda b, t: (0, 0, 0)),
                  pl.BlockSpec((RET_HEADS, c, 1), lambda b, t: (0, 0, 0)),
                  pl.BlockSpec((RET_HEADS, c, 1), lambda b, t: (0, 0, 0)),
                  pl.BlockSpec((RET_HEADS, 1, 1), lambda b, t: (0, 0, 0)),
                  pl.BlockSpec((1, hw), lambda b, t: (0, 0))],
        out_specs=pl.BlockSpec((rb, hw), lambda b, t: (row(b, t), 0)),
        out_shape=jax.ShapeDtypeStruct((n, hw), BF16),
        scratch_shapes=[pltpu.VMEM((RET_HEADS, RET_DK, RET_DV), F32)],
        compiler_params=_cp(("arbitrary", "arbitrary")),
        name="retention",
    )(proj, proj, proj, proj, cos, sin, decay_in, zeta, q_decay, chunk_decay,
      gn_g.reshape(1, hw))


def _compress_kernel(ak_ref, av_ref, pek_ref, pev_ref, w1k_ref, w1v_ref, w2k_ref, w2v_ref,
                     kc_ref, vc_ref, a_scr):
    seq = ak_ref.shape[0]
    nblk = seq // CMP_STRIDE
    a_scr[seq:, :] = jnp.zeros((a_scr.shape[0] - seq, NSA_D), F32)

    def one(a_ref, pe_ref, w1_ref, w2_ref, o_ref):
        a_scr[:seq, :] = a_ref[...].astype(F32)
        flat = jnp.concatenate([a_scr[pl.ds(l, nblk, stride=CMP_STRIDE), :] for l in range(CMP_BLOCK)],
                               axis=1)
        pre = _dot((flat + pe_ref[...]).astype(BF16), w1_ref[...].astype(BF16))
        o_ref[...] = _dot(_silu(pre).astype(BF16), w2_ref[...].astype(BF16)).astype(BF16)

    one(ak_ref, pek_ref, w1k_ref, w2k_ref, kc_ref)
    one(av_ref, pev_ref, w1v_ref, w2v_ref, vc_ref)


def _compress(proj, pek, pev, w1k, w1v, w2k, w2v, bsz, seq):
    g_ = NSA_KV_GROUPS
    nblk = seq // CMP_STRIDE
    full = lambda a: pl.BlockSpec(a.shape, lambda i: (0,) * a.ndim)
    kblk = pl.BlockSpec((seq, NSA_D), lambda i: (i // g_, _C_KC + i % g_))
    vblk = pl.BlockSpec((seq, NSA_D), lambda i: (i // g_, _C_VC + i % g_))
    oblk = pl.BlockSpec((nblk, NSA_D), lambda i: (i, 0))
    return pl.pallas_call(
        _compress_kernel,
        grid=(bsz * g_,),
        in_specs=[kblk, vblk, full(pek), full(pev), full(w1k), full(w1v), full(w2k), full(w2v)],
        out_specs=[oblk, oblk],
        out_shape=[jax.ShapeDtypeStruct((bsz * g_ * nblk, NSA_D), BF16)] * 2,
        scratch_shapes=[pltpu.VMEM((seq + CMP_BLOCK, NSA_D), F32)],
        compiler_params=_cp(("arbitrary",)),
        name="compress",
    )(proj, proj, pek, pev, w1k, w1v, w2k, w2v)


def _nsa_kernel(q_ref, kc_ref, vc_ref, ks_ref, vs_ref, kw_ref, vw_ref, gate_ref, ovt_ref,
                o_ref, kaug_ref, vst_ref, vwt_ref, *, tq, tk, vt, seq, ncmp, wlen):
    crow = kc_ref.shape[0]
    i = pl.program_id(2)
    hg = NSA_HPG
    d = NSA_D
    r = hg * tq

    @pl.when(i == 0)
    def _():
        kaug_ref[:, :d] = ks_ref[...]
        blk = lax.broadcasted_iota(I32, (seq, LANES), 0) // SEL_BLOCK
        lane = lax.broadcasted_iota(I32, (seq, LANES), 1)
        kaug_ref[:, d:] = (blk == lane).astype(BF16)
        for c in range(seq // vt):
            vst_ref[c] = vs_ref[c * vt:(c + 1) * vt, :].astype(F32).T.astype(BF16)
            vwt_ref[c] = vw_ref[c * vt:(c + 1) * vt, :].astype(F32).T.astype(BF16)

    def pv_t(vt_ref, first_tile, p):
        out = None
        for c in range(p.shape[0] // vt):
            part = _dot(vt_ref[first_tile + c], p[c * vt:(c + 1) * vt, :])
            out = part if out is None else out + part
        return out

    q = q_ref[...]
    qh = [(q[:, h * d:(h + 1) * d].astype(F32) * (d ** -0.5)).astype(BF16) for h in range(hg)]
    qa = jnp.concatenate(qh, axis=0)
    t_row = i * tq + lax.broadcasted_iota(I32, (1, tq), 1)
    tpos_row = jnp.concatenate([t_row] * hg, axis=1)

    w0 = pl.multiple_of(jnp.maximum(i * tq + tq - wlen, 0), vt)
    sw = _dot_nt(kw_ref[pl.ds(w0, wlen), :], qa)
    delta = tpos_row - (w0 + lax.broadcasted_iota(I32, (wlen, 1), 0))
    sw = jnp.where((delta >= 0) & (delta < WIN), sw, NEG)
    e_w = jnp.exp(sw - jnp.max(sw, axis=0, keepdims=True))
    o_win = (pv_t(vwt_ref, w0 // vt, e_w.astype(BF16)) / jnp.sum(e_w, axis=0, keepdims=True)).T

    sct = _dot_nt(kc_ref[...], qa)
    cidx = lax.broadcasted_iota(I32, (crow, 1), 0)
    cmask = (cidx * CMP_STRIDE + (CMP_BLOCK - 1) <= tpos_row) & (cidx < ncmp)
    sct = jnp.where(cmask, sct, NEG)
    e_c = jnp.where(cmask, jnp.exp(sct - jnp.max(sct, axis=0, keepdims=True)), 0.0)
    den_c = jnp.sum(e_c, axis=0, keepdims=True)
    p_t = jnp.where(den_c > 0.0, e_c / jnp.where(den_c > 0.0, den_c, 1.0), 0.0)
    o_cmp = _dot_tn(p_t.astype(BF16), vc_ref[...])

    psum_t = p_t[:, 0:tq]
    for h in range(1, hg):
        psum_t = psum_t + p_t[:, h * tq:(h + 1) * tq]
    nsel_blocks = seq // SEL_BLOCK
    nsb = ovt_ref.shape[0]
    imp_t = jnp.dot(ovt_ref[...], psum_t, preferred_element_type=F32,
                    precision=lax.Precision.HIGHEST)
    sidx = lax.broadcasted_iota(I32, (nsb, 1), 0)
    cur = t_row // SEL_BLOCK
    valid = sidx <= cur
    forced = (sidx == 0) | (sidx == cur) | (sidx == cur - 1)
    score = jnp.where(valid, imp_t + jnp.where(forced, FORCE_BONUS, 0.0), -1.0)
    rank = jnp.zeros((nsb, tq), F32)
    for s2 in range(nsel_blocks):
        row = score[s2:s2 + 1, :]
        beats = (row > score) | ((row == score) & (sidx > s2))
        rank = rank + beats.astype(F32)
    sel = valid & (rank < float(min(SEL_COUNT, nsel_blocks)))
    selb_t = jnp.where(sel, 0.0, NEG)
    selb_t = jnp.concatenate([selb_t, jnp.zeros((LANES - nsb, tq), F32)], axis=0)
    selb = selb_t.T.astype(BF16)
    q_aug = jnp.concatenate([jnp.concatenate([qh[h], selb], axis=1) for h in range(hg)], axis=0)

    n_full = (i * tq + 1) // tk
    kl = lax.broadcasted_iota(I32, (tk, 1), 0)

    def tile(j, carry, masked):
        m, l, acc = carry
        k0 = pl.multiple_of(j * tk, tk)
        s = _dot_nt(kaug_ref[pl.ds(k0, tk), :], q_aug)
        if masked:
            s = jnp.where(k0 + kl <= tpos_row, s, NEG)
        m_new = jnp.maximum(m, jnp.max(s, axis=0, keepdims=True))
        alpha = jnp.exp(m - m_new)
        p = jnp.exp(s - m_new)
        l = alpha * l + jnp.sum(p, axis=0, keepdims=True)
        acc = alpha * acc + pv_t(vst_ref, j * (tk // vt), p.astype(BF16))
        return m_new, l, acc

    m0 = jnp.full((1, r), NEG, F32)
    l0 = jnp.zeros((1, r), F32)
    a0 = jnp.zeros((d, r), F32)
    carry = lax.fori_loop(0, n_full, functools.partial(tile, masked=False), (m0, l0, a0))
    _, l_s, acc_s = tile(n_full, carry, True)
    o_slc = (acc_s / l_s).T

    gt = jax.nn.sigmoid(gate_ref[...])
    grp = pl.program_id(1)
    for g in range(1, NSA_KV_GROUPS):
        gt = jnp.where(grp == g, pltpu.roll(gt, LANES - g * 3 * hg, 1), gt)
    for h in range(hg):
        rows = slice(h * tq, (h + 1) * tq)
        o = (gt[:, 3 * h:3 * h + 1] * o_cmp[rows] + gt[:, 3 * h + 1:3 * h + 2] * o_slc[rows]
             + gt[:, 3 * h + 2:3 * h + 3] * o_win[rows])
        o_ref[:, h * d:(h + 1) * d] = o.astype(BF16)


def _nsa(proj, kc, vc, gates_g, bsz, seq):
    n = proj.shape[0]
    g_ = NSA_KV_GROUPS
    tq = NSA_TQ
    tk = min(NSA_TK, seq)
    nq = seq // tq
    ncmp = (seq - CMP_BLOCK) // CMP_STRIDE + 1
    wlen = min(WIN + tq, seq)
    nsel = seq // SEL_BLOCK
    crow = seq // CMP_STRIDE
    assert nsel <= LANES and crow <= LANES and tq <= tk and tk % tq == 0
    nsb = -(-nsel // 8) * 8
    ss = np.arange(nsb)[:, None] * SEL_BLOCK
    cs = np.arange(crow)[None, :] * CMP_STRIDE
    ov = ((cs < ss + SEL_BLOCK) & (cs + CMP_BLOCK > ss)
          & (np.arange(crow)[None, :] < ncmp) & (np.arange(nsb)[:, None] < nsel))
    ov = jnp.asarray(ov.astype(np.float32))
    kvspec = lambda c0: pl.BlockSpec((seq, NSA_D), lambda b, g, i: (b, c0 + g))
    vt = NSA_VT
    assert tk % vt == 0 and wlen % vt == 0 and tq % vt == 0 and seq % vt == 0
    kern = functools.partial(_nsa_kernel, tq=tq, tk=tk, vt=vt, seq=seq, ncmp=ncmp, wlen=wlen)
    return pl.pallas_call(
        kern,
        grid=(bsz, g_, nq),
        in_specs=[pl.BlockSpec((tq, NSA_HPG * NSA_D), lambda b, g, i: (b * nq + i, _C_NQ // NSA_HPG + g)),
                  pl.BlockSpec((crow, NSA_D), lambda b, g, i: (b * g_ + g, 0)),
                  pl.BlockSpec((crow, NSA_D), lambda b, g, i: (b * g_ + g, 0)),
                  kvspec(_C_KS), kvspec(_C_VS), kvspec(_C_KW), kvspec(_C_VW),
                  pl.BlockSpec((tq, LANES), lambda b, g, i: (b * nq + i, 0)),
                  pl.BlockSpec((nsb, crow), lambda b, g, i: (0, 0))],
        out_specs=pl.BlockSpec((tq, NSA_HPG * NSA_D), lambda b, g, i: (b * nq + i, g)),
        out_shape=jax.ShapeDtypeStruct((n, NSA_HEADS * NSA_D), BF16),
        scratch_shapes=[pltpu.VMEM((seq, 2 * NSA_D), BF16),
                        pltpu.VMEM((seq // vt, NSA_D, vt), BF16),
                        pltpu.VMEM((seq // vt, NSA_D, vt), BF16)],
        compiler_params=_cp(("arbitrary", "arbitrary", "arbitrary")),
        name="nsa",
    )(proj, kc, vc, proj, proj, proj, proj, gates_g, ov)


def _outproj_kernel(oret_ref, onsa_ref, x_ref, mod_ref, g_ref, w_ref, wr_ref,
                    x1_ref, h2_ref, lt_ref):
    hw = oret_ref.shape[1]
    mix = (_dot(oret_ref[...], w_ref[:hw, :].astype(BF16))
           + _dot(onsa_ref[...], w_ref[hw:, :].astype(BF16)))
    x1 = x_ref[...] + mod_ref[0, 2:3, :] * mix
    x1_ref[...] = x1
    y = x1 * lax.rsqrt(jnp.mean(x1 * x1, axis=-1, keepdims=True) + EPS) * g_ref[...]
    h2 = y * (1.0 + mod_ref[0, 4:5, :]) + mod_ref[0, 3:4, :]
    h2_ref[...] = h2
    lt_ref[...] = _dot_nt(wr_ref[...], h2.astype(BF16))


def _outproj(o_ret, o_nsa, x2, mod3, g, w, wr_bf, seq):
    n, d = x2.shape
    tm = min(OUTPROJ_TM, seq)
    hw = o_ret.shape[1]
    return pl.pallas_call(
        _outproj_kernel,
        grid=(n // tm,),
        in_specs=[pl.BlockSpec((tm, hw), lambda i: (i, 0)),
                  pl.BlockSpec((tm, o_nsa.shape[1]), lambda i: (i, 0)),
                  pl.BlockSpec((tm, d), lambda i: (i, 0)),
                  pl.BlockSpec((1, 6, d), lambda i: ((i * tm) // seq, 0, 0)),
                  pl.BlockSpec((1, d), lambda i: (0, 0)),
                  pl.BlockSpec(w.shape, lambda i: (0, 0), pipeline_mode=pl.Buffered(1)),
                  pl.BlockSpec(wr_bf.shape, lambda i: (0, 0))],
        out_specs=[pl.BlockSpec((tm, d), lambda i: (i, 0)),
                   pl.BlockSpec((tm, d), lambda i: (i, 0)),
                   pl.BlockSpec((LANES, tm), lambda i: (0, i))],
        out_shape=[jax.ShapeDtypeStruct((n, d), F32),
                   jax.ShapeDtypeStruct((n, d), F32),
                   jax.ShapeDtypeStruct((LANES, n), F32)],
        compiler_params=_cp(("arbitrary",)),
        name="outproj",
    )(o_ret, o_nsa, x2, mod3, g.reshape(1, d), w, wr_bf)


def _route_kernel(lt_ref, b_ref, tri_ref, ids_ref, wts_ref, cnt_ref, carry_ref, *, sub):
    @pl.when(pl.program_id(0) == 0)
    def _():
        carry_ref[...] = jnp.zeros_like(carry_ref)

    ng, ne = N_GROUPS, EXP_PER_GROUP
    l = lt_ref[...] + b_ref[:, 0:1]
    tc = l.shape[1]
    ridx = lax.broadcasted_iota(I32, (ng, tc), 0).astype(F32)

    def softmax0(v):
        e = jnp.exp(v - jnp.max(v, axis=0, keepdims=True))
        return e / jnp.sum(e, axis=0, keepdims=True)

    def top1(p):
        top = jnp.max(p, axis=0, keepdims=True)
        idx = jnp.min(jnp.where(p == top, ridx, float(ng)), axis=0, keepdims=True)
        return top, idx

    pg_top, grp = top1(softmax0(l[0:ng]))
    leg = jnp.zeros((ne, tc), F32)
    for g in range(ng):
        leg = jnp.where(grp == float(g), l[ng + g * ne:ng + (g + 1) * ne], leg)
    pe = softmax0(leg)
    p1, i1 = top1(pe)
    p2, i2 = top1(jnp.where(ridx == i1, -1.0, pe))
    den = p1 + p2
    w1 = pg_top * p1 / den
    w2 = pg_top * p2 / den
    e1 = grp * float(ne) + i1
    e2 = grp * float(ne) + i2

    eio = lax.broadcasted_iota(I32, (N_EXPERTS, sub), 0).astype(F32)
    r1 = []
    r2 = []
    carry = carry_ref[:, 0:1]
    for c in range(tc // sub):
        cs = slice(c * sub, (c + 1) * sub)
        oh1 = (eio == e1[:, cs]).astype(F32)
        oh2 = (eio == e2[:, cs]).astype(F32)
        oh = oh1 + oh2
        before = carry + _dot(oh.astype(BF16), tri_ref[...])
        r1.append(jnp.sum(oh1 * before, axis=0, keepdims=True))
        r2.append(jnp.sum(oh2 * before, axis=0, keepdims=True))
        carry = carry + jnp.sum(oh, axis=1, keepdims=True)
    carry_ref[...] = jnp.broadcast_to(carry, carry_ref.shape)
    cnt_ref[...] = jnp.broadcast_to(carry, cnt_ref.shape).astype(I32)
    r1 = jnp.concatenate(r1, axis=1)
    r2 = jnp.concatenate(r2, axis=1)
    zf = jnp.zeros((4, tc), F32)
    ids_ref[...] = jnp.concatenate([e1, e2, r1, r2, zf], axis=0).astype(I32)
    wts_ref[...] = jnp.concatenate([w1, w2, jnp.zeros((6, tc), F32)], axis=0)


def _route(lt, bias_col):
    n = lt.shape[1]
    tc = min(ROUTE_TC, n)
    sub = min(ROUTE_SUB, tc)
    tri =jnp.asarray(np.triu(np.ones((sub, sub), np.float32), 1), BF16)
    return pl.pallas_call(
        functools.partial(_route_kernel, sub=sub),
        grid=(n // tc,),
        in_specs=[pl.BlockSpec((LANES, tc), lambda i: (0, i)),
                  pl.BlockSpec((LANES, LANES), lambda i: (0, 0)),
                  pl.BlockSpec((sub, sub), lambda i: (0, 0))],
        out_specs=[pl.BlockSpec((8, tc), lambda i: (0, i)),
                   pl.BlockSpec((8, tc), lambda i: (0, i)),
                   pl.BlockSpec((N_EXPERTS, LANES), lambda i: (0, 0))],
        out_shape=[jax.ShapeDtypeStruct((8, n), I32),
                   jax.ShapeDtypeStruct((8, n), F32),
                   jax.ShapeDtypeStruct((N_EXPERTS, LANES), I32)],
        scratch_shapes=[pltpu.VMEM((N_EXPERTS, LANES), F32)],
        compiler_params=_cp(("arbitrary",)),
        name="route",
    )(lt, bias_col, tri)


def _dest_kernel(ids_ref, ps_ref, o_ref):
    ids = ids_ref[...].astype(F32)
    tc = ids.shape[1]
    eio = lax.broadcasted_iota(I32, (N_EXPERTS, tc), 0).astype(F32)
    ps = ps_ref[:, 0:1]
    rows = [jnp.sum(jnp.where(eio == ids[k:k + 1], ps, 0.0), axis=0, keepdims=True) + ids[2 + k:3 + k]
            for k in range(2)]
    o_ref[...] = jnp.concatenate(rows + [jnp.zeros((6, tc), F32)], axis=0).astype(I32)


def _dest(ids, pad_start):
    n = ids.shape[1]
    tc = min(ROUTE_TC, n)
    ps = jnp.broadcast_to(pad_start.astype(F32)[:, None], (N_EXPERTS, LANES))
    dest = pl.pallas_call(
        _dest_kernel,
        grid=(n // tc,),
        in_specs=[pl.BlockSpec((8, tc), lambda i: (0, i)),
                  pl.BlockSpec((N_EXPERTS, LANES), lambda i: (0, 0))],
        out_specs=pl.BlockSpec((8, tc), lambda i: (0, i)),
        out_shape=jax.ShapeDtypeStruct((8, n), I32),
        compiler_params=_cp(("arbitrary",)),
        name="dest",
    )(ids, ps)
    return dest[:2].reshape(2 * n)


def _dispatch_kernel(dst0_ref, dst1_ref, ps_ref, cnt_ref, h_ref, xs_ref, zbuf, sem, zsem, *, tcd, tm, zr):
    @pl.when(pl.program_id(0) == 0)
    def _():
        zbuf[...] = jnp.zeros_like(zbuf)
        sizes = [zr >> b for b in range(zr.bit_length()) if (zr >> b) >= SUBLANES]

        def zero_rows(row, size):
            return pltpu.make_async_copy(zbuf.at[pl.ds(0, size)],
                                         xs_ref.at[pl.ds(pl.multiple_of(row, SUBLANES), size)], zsem)

        def zero_row(row):
            return pltpu.make_async_copy(zbuf.at[pl.ds(0, 1)], xs_ref.at[pl.ds(row, 1)], zsem)

        def fill(e, wait):
            cnt = cnt_ref[e]
            cnt8 = lax.div(cnt + (SUBLANES - 1), SUBLANES) * SUBLANES
            gap = lax.rem(tm - lax.rem(cnt8, tm), tm)
            base = ps_ref[e]

            def single(c, _):
                cp = zero_row(base + cnt + c)
                cp.wait() if wait else cp.start()
                return 0

            lax.fori_loop(0, cnt8 - cnt, single, 0)
            row = base + cnt8
            for size in sizes:
                has = lax.rem(lax.div(gap, size), 2) == 1

                @pl.when(has)
                def _():
                    cp = zero_rows(row, size)
                    cp.wait() if wait else cp.start()

                row = row + jnp.where(has, size, 0)

        last = N_EXPERTS - 1
        used_rows = ps_ref[last] + lax.div(cnt_ref[last] + (tm - 1), tm) * tm
        n_tail = lax.div(xs_ref.shape[0] - used_rows, zr)

        def tail(c, wait):
            cp = zero_rows(used_rows + c * zr, zr)
            cp.wait() if wait else cp.start()

        for wait in (False, True):
            lax.fori_loop(0, N_EXPERTS, lambda e, _: (fill(e, wait), 0)[1], 0)
            lax.fori_loop(0, n_tail, lambda c, _: (tail(c, wait), 0)[1], 0)

    def issue(t, _):
        for k, dst_ref in enumerate((dst0_ref, dst1_ref)):
            dst = dst_ref[t]
            pltpu.make_async_copy(h_ref.at[pl.ds(t, 1)], xs_ref.at[pl.ds(dst, 1)], sem).start(priority=k)
        return 0

    lax.fori_loop(0, tcd, issue, 0, unroll=8)
    for k in range(2):
        pltpu.make_async_copy(h_ref, xs_ref.at[pl.ds(0, tcd)], sem).wait()


def _dispatch(dest, pad_start, counts, h2, rows, tcd, tm):
    n, d = h2.shape
    assert tm & (tm - 1) == 0 and tm >= 2 * SUBLANES
    zr = tm // 2
    return pl.pallas_call(
        functools.partial(_dispatch_kernel, tcd=tcd, tm=tm, zr=zr),
        grid=(n // tcd,),
        in_specs=[pl.BlockSpec((tcd,), lambda i: (i,), memory_space=pltpu.SMEM),
                  pl.BlockSpec((tcd,), lambda i: (n // tcd + i,), memory_space=pltpu.SMEM),
                  pl.BlockSpec(memory_space=pltpu.SMEM),
                  pl.BlockSpec(memory_space=pltpu.SMEM),
                  pl.BlockSpec((tcd, d), lambda i: (i, 0))],
        out_specs=pl.BlockSpec(memory_space=pl.ANY),
        out_shape=jax.ShapeDtypeStruct((rows, d), h2.dtype),
        scratch_shapes=[pltpu.VMEM((zr, d), h2.dtype), pltpu.SemaphoreType.DMA(()),
                        pltpu.SemaphoreType.DMA(())],
        compiler_params=_cp(("arbitrary",)),
        name="dispatch",
    )(dest, dest, pad_start, counts, h2)


def _experts_kernel(be_ref, ord_ref, nxt_ref, nu_ref, xs_ref, wg_hbm, wu_hbm, wd_hbm, ys_ref,
                    g_f, u_f, d_f, g_s, u_s, d_s, sem):
    i = pl.program_id(0)
    used = i < nu_ref[0]
    e = be_ref[i]
    fresh = (i == 0) | (e != be_ref[jnp.maximum(i - 1, 0)])
    slot = lax.rem(ord_ref[i], 2)

    def fetch(expert, s):
        return [pltpu.make_async_copy(w.at[expert], f.at[s], sem.at[s, k])
                for k, (w, f) in enumerate(((wg_hbm, g_f), (wu_hbm, u_f), (wd_hbm, d_f)))]

    @pl.when(used & (i == 0))
    def _():
        for c in fetch(e, slot):
            c.start(priority=1)

    @pl.when(used & fresh)
    def _():
        for c in fetch(e, slot):
            c.wait()
        nxt = nxt_ref[i]

        @pl.when(nxt >= 0)
        def _():
            for c in fetch(nxt, 1 - slot):
                c.start(priority=1)

        g_s[...] = g_f[slot].astype(BF16)
        u_s[...] = u_f[slot].astype(BF16)
        d_s[...] = d_f[slot].astype(BF16)

    @pl.when(used)
    def _():
        x = xs_ref[...].astype(BF16)
        a = _silu(_dot(x, g_s[...])) * _dot(x, u_s[...])
        ys_ref[...] = _dot(a.astype(BF16), d_s[...])

    @pl.when(jnp.logical_not(used))
    def _():
        ys_ref[...] = jnp.zeros_like(ys_ref)


def _experts(blk_exp, blk_ord, blk_nxt, n_used, xs, w_gate, w_up, w_down, tm):
    d = xs.shape[1]
    de = w_gate.shape[2]
    nb = blk_exp.shape[0]
    row = lambda i, be, od, nx, nu: (jnp.minimum(i, nu[0] - 1), 0)
    anyspec = pl.BlockSpec(memory_space=pl.ANY)
    grid_spec = pltpu.PrefetchScalarGridSpec(
        num_scalar_prefetch=4,
        grid=(nb,),
        in_specs=[pl.BlockSpec((tm, d), row), anyspec, anyspec, anyspec],
        out_specs=pl.BlockSpec((tm, d), lambda i, be, od, nx, nu: (i, 0)),
        scratch_shapes=[pltpu.VMEM((2, d, de), F32), pltpu.VMEM((2, d, de), F32),
                        pltpu.VMEM((2, de, d), F32),
                        pltpu.VMEM((d, de), BF16), pltpu.VMEM((d, de), BF16),
                        pltpu.VMEM((de, d), BF16),
                        pltpu.SemaphoreType.DMA((2, 3))],
    )
    return pl.pallas_call(
        _experts_kernel,
        grid_spec=grid_spec,
        out_shape=jax.ShapeDtypeStruct((nb * tm, d), F32),
        compiler_params=_cp(("arbitrary",)),
        name="experts",
    )(blk_exp, blk_ord, blk_nxt, n_used, xs, w_gate, w_up, w_down)


def _combine_kernel(cur0_ref, cur1_ref, nxt0_ref, nxt1_ref, ys_ref, x1_ref, wt_ref, mod_ref, g_ref,
                    o_ref, buf, sem, *, tc):
    i = pl.program_id(0)
    slot = lax.rem(i, 2)

    def gather(id_refs, s):
        def issue(t, _):
            for k, id_ref in enumerate(id_refs):
                src = id_ref[t]
                pltpu.make_async_copy(ys_ref.at[pl.ds(src, 1)], buf.at[s, k, pl.ds(t, 1)],
                                      sem.at[s]).start(priority=k)
            return 0

        lax.fori_loop(0, tc, issue, 0, unroll=8)

    @pl.when(i == 0)
    def _():
        gather((cur0_ref, cur1_ref), 0)

    @pl.when(i + 1 < pl.num_programs(0))
    def _():
        gather((nxt0_ref, nxt1_ref), 1 - slot)

    for k in range(2):
        pltpu.make_async_copy(ys_ref.at[pl.ds(0, tc)], buf.at[slot, k], sem.at[slot]).wait()

    w_rows = wt_ref[...]
    w_cols = jnp.concatenate([w_rows, jnp.zeros((LANES - w_rows.shape[0], tc), F32)], axis=0).T
    moe = buf[slot, 0] * w_cols[:, 0:1] + buf[slot, 1] * w_cols[:, 1:2]
    x2 = x1_ref[...] + mod_ref[0, 5:6, :] * moe
    o_ref[...] = x2 * lax.rsqrt(jnp.mean(x2 * x2, axis=-1, keepdims=True) + EPS) * g_ref[...]


def _combine(dest, ys, x1, wts, mod3, final_g, seq, tc):
    n, d = x1.shape
    nt = n // tc
    tbl = lambda k, off: pl.BlockSpec((tc,), lambda i: (k * nt + jnp.minimum(i + off, nt - 1),),
                                      memory_space=pltpu.SMEM)
    return pl.pallas_call(
        functools.partial(_combine_kernel, tc=tc),
        grid=(nt,),
        in_specs=[tbl(0, 0), tbl(1, 0), tbl(0, 1), tbl(1, 1),
                  pl.BlockSpec(memory_space=pl.ANY),
                  pl.BlockSpec((tc, d), lambda i: (i, 0)),
                  pl.BlockSpec((8, tc), lambda i: (0, i)),
                  pl.BlockSpec((1, 6, d), lambda i: ((i * tc) // seq, 0, 0)),
                  pl.BlockSpec((1, d), lambda i: (0, 0))],
        out_specs=pl.BlockSpec((tc, d), lambda i: (i, 0)),
        out_shape=jax.ShapeDtypeStruct((n, d), F32),
        scratch_shapes=[pltpu.VMEM((2, 2, tc, d), F32), pltpu.SemaphoreType.DMA((2,))],
        compiler_params=_cp(("arbitrary",)),
        name="combine",
    )(dest, dest, dest, dest, ys, x1, wts, mod3, final_g.reshape(1, d))


def _token_mixer(x2, mod3, norm1_g, w_in, ret_gn_g, cmp_pos_k, cmp_w1_k, cmp_w2_k,
                 cmp_pos_v, cmp_w1_v, cmp_w2_v, bsz, seq):
    n, d = x2.shape
    w_t = w_in.T
    w_gate_t = jnp.pad(w_t[PROJ_MAIN:], ((0, LANES - N_GATE_COLS), (0, 0))).astype(BF16)
    proj, gate_logits = _inproj(x2, mod3, norm1_g, w_t, w_gate_t, seq)

    o_ret = _retention(proj, ret_gn_g, bsz, seq)

    g_ = NSA_KV_GROUPS
    crow = seq // CMP_STRIDE

    pe_flat = lambda pe: pe.reshape(1, CMP_BLOCK * NSA_D)
    kc, vc = _compress(proj, pe_flat(cmp_pos_k), pe_flat(cmp_pos_v),
                       cmp_w1_k, cmp_w1_v, cmp_w2_k, cmp_w2_v, bsz, seq)

    o_nsa = _nsa(proj, kc, vc, gate_logits, bsz, seq)
    return o_ret, o_nsa


def _moe(h2, lt, x1, mod3, final_g, b_grp, b_exp, w_gate, w_up, w_down, seq, tm):
    n, d = h2.shape
    bias_col = jnp.zeros((LANES,), F32).at[:N_GROUPS].set(b_grp).at[N_GROUPS:N_GROUPS + N_EXPERTS].set(b_exp)
    bias_col = jnp.broadcast_to(bias_col[:, None], (LANES, LANES))
    ids, wts, cnt = _route(lt, bias_col)

    counts = cnt[:, 0]
    padded = (counts + tm - 1) // tm * tm
    pad_end = jnp.cumsum(padded)
    pad_start = (pad_end - padded).astype(I32)
    nb = (2 * n) // tm + N_EXPERTS
    n_used = (pad_end[-1] // tm).astype(I32).reshape(1)
    blk_start = jnp.arange(nb, dtype=I32) * tm
    blk_exp = jnp.minimum(jnp.sum((pad_end[None, :] <= blk_start[:, None]).astype(I32), axis=1),
                          N_EXPERTS - 1).astype(I32)
    last_exp = blk_exp[jnp.maximum(n_used[0] - 1, 0)]
    blk_exp = jnp.where(jnp.arange(nb) < n_used[0], blk_exp, last_exp)
    eid = jnp.arange(N_EXPERTS, dtype=I32)
    has = counts > 0
    exp_ord = jnp.sum((has[None, :] & (eid[None, :] < eid[:, None])).astype(I32), axis=1)
    exp_nxt = jnp.min(jnp.where(has[None, :] & (eid[None, :] > eid[:, None]), eid[None, :], N_EXPERTS), axis=1)
    exp_nxt = jnp.where(exp_nxt < N_EXPERTS, exp_nxt, -1).astype(I32)
    blk_ord = exp_ord[blk_exp]
    blk_nxt = exp_nxt[blk_exp]

    dest = _dest(ids, pad_start)
    xs = _dispatch(dest, pad_start, counts.astype(I32), h2, nb * tm, min(DISPATCH_TC, n), tm)
    ys = _experts(blk_exp, blk_ord, blk_nxt, n_used, xs, w_gate, w_up, w_down, tm)
    return _combine(dest, ys, x1, wts, mod3, final_g, seq, min(COMBINE_TC, n))


def kernel(x, c, w_ada, b_ada, norm1_g, norm2_g, final_g, w_in, ret_gn_g, cmp_pos_k, cmp_w1_k,
           cmp_w2_k, cmp_pos_v, cmp_w1_v, cmp_w2_v, w_out, w_grp, b_grp, w_exp, b_exp, w_gate,
           w_up, w_down):
    bsz, seq, d = x.shape
    assert w_ada.shape[0] == 1, "single-layer block"
    n = bsz * seq
    x2 = x.reshape(n, d)
    mod3 = _ada(c, w_ada[0], b_ada[0]).reshape(bsz, 6, d)

    o_ret, o_nsa = _token_mixer(x2, mod3, norm1_g[0], w_in[0], ret_gn_g[0], cmp_pos_k[0],
                                cmp_w1_k[0], cmp_w2_k[0], cmp_pos_v[0], cmp_w1_v[0], cmp_w2_v[0],
                                bsz, seq)

    w_route = jnp.concatenate([w_grp[0], w_exp[0]], axis=1)
    w_route = jnp.pad(w_route, ((0, 0), (0, LANES - w_route.shape[1]))).T.astype(BF16)
    x1, h2, lt = _outproj(o_ret, o_nsa, x2, mod3, norm2_g[0], w_out[0], w_route, seq)

    out = _moe(h2, lt, x1, mod3, final_g, b_grp[0], b_exp[0], w_gate[0], w_up[0], w_down[0],
               seq, MOE_TM)
    return out.reshape(bsz, seq, d)
```

```python
import functools

import numpy as np
import jax
import jax.numpy as jnp
from jax import lax
from jax.experimental import pallas as pl
from jax.experimental.pallas import tpu as pltpu

F32 = jnp.float32
BF16 = jnp.bfloat16
I32 = jnp.int32

RET_HEADS = 4
RET_DK = 256
RET_DV = 256
RET_CHUNK = 128
NSA_HEADS = 8
NSA_KV_GROUPS = 2
NSA_HPG = NSA_HEADS // NSA_KV_GROUPS
NSA_D = 128
CMP_BLOCK = 32
CMP_STRIDE = 16
SEL_BLOCK = 64
SEL_COUNT = 16
WIN = 512
N_GROUPS = 8
EXP_PER_GROUP = 8
N_EXPERTS = N_GROUPS * EXP_PER_GROUP
ROPE_BASE = 10000.0
EPS = 1e-6
NEG = -1e30
FORCE_BONUS = 1e4

LANES = 128
SUBLANES = 8
MXU_DIM = 256
VMEM_LIMIT = 52 * 1024 * 1024

ADA_TN = 1024
INPROJ_TM = 2048
INPROJ_TN = 512
RET_CHUNKS = 4
NSA_TQ = 512
NSA_TK = 512
NSA_VT = MXU_DIM
OUTPROJ_TM = 512
ROUTE_TC = 2048
ROUTE_SUB = 512
MOE_TM = 256
DISPATCH_TC = 1024
COMBINE_TC = 256

_C_NQ = 32
_C_KC, _C_VC, _C_KS, _C_VS, _C_KW, _C_VW = 40, 42, 44, 46, 48, 50
_C_GATE = 52
PROJ_MAIN = _C_GATE * LANES
N_GATE_COLS = NSA_HEADS * 3


def _cp(sem, vmem=VMEM_LIMIT):
    return pltpu.CompilerParams(dimension_semantics=sem, vmem_limit_bytes=vmem)


def _silu(v):
    return v * jax.nn.sigmoid(v)


def _dot(a, b):
    return jnp.dot(a, b, preferred_element_type=F32)


def _dot_nt(a, b):
    return lax.dot_general(a, b, (((1,), (1,)), ((), ())), preferred_element_type=F32)


def _dot_tn(a, b):
    return lax.dot_general(a, b, (((0,), (0,)), ((), ())), preferred_element_type=F32)


def _ada_kernel(c_ref, w_ref, b_ref, o_ref):
    ca = _silu(c_ref[...]).astype(BF16)
    o_ref[...] = _dot(ca, w_ref[...].astype(BF16)) + b_ref[...]


def _ada(c, w, b):
    bsz, d = c.shape
    n = w.shape[1]
    tn = min(ADA_TN, n)
    return pl.pallas_call(
        _ada_kernel,
        grid=(n // tn,),
        in_specs=[pl.BlockSpec((bsz, d), lambda j: (0, 0)),
                  pl.BlockSpec((d, tn), lambda j: (0, j)),
                  pl.BlockSpec((1, tn), lambda j: (0, j))],
        out_specs=pl.BlockSpec((bsz, tn), lambda j: (0, j)),
        out_shape=jax.ShapeDtypeStruct((bsz, n), F32),
        compiler_params=_cp(("arbitrary",)),
        name="ada",
    )(c, w, b.reshape(1, n))


def _inproj_kernel(x_hbm, mod_ref, g_ref, wt_ref, wgt_ref, proj_ref, gate_ref, h_ref, x_buf, x_sem):
    i = pl.program_id(0)
    tm = x_buf.shape[0]

    def x_copy(blk):
        return pltpu.make_async_copy(x_hbm.at[pl.ds(pl.multiple_of(blk * tm, tm), tm)], x_buf, x_sem)

    @pl.when(pl.program_id(1) == 0)
    def _():
        @pl.when(i == 0)
        def _():
            x_copy(i).start()

        x_copy(i).wait()
        x = x_buf[...]
        y = x * lax.rsqrt(jnp.mean(x * x, axis=-1, keepdims=True) + EPS) * g_ref[...]
        h = (y * (1.0 + mod_ref[0, 1:2, :]) + mod_ref[0, 0:1, :]).astype(BF16)
        h_ref[...] = h
        gate_ref[...] = _dot_nt(h, wgt_ref[...])

        @pl.when(i + 1 < pl.num_programs(0))
        def _():
            x_copy(i + 1).start(priority=1)

    proj_ref[...] = _dot_nt(h_ref[...], wt_ref[...].astype(BF16)).astype(BF16)


def _inproj(x2, mod3, g, w_t, wg_t, seq):
    n, d = x2.shape
    tm = min(INPROJ_TM, seq)
    tn = INPROJ_TN
    nj = PROJ_MAIN // tn
    return pl.pallas_call(
        _inproj_kernel,
        grid=(n // tm, nj),
        in_specs=[pl.BlockSpec(memory_space=pl.ANY),
                  pl.BlockSpec((1, 6, d), lambda i, j: ((i * tm) // seq, 0, 0)),
                  pl.BlockSpec((1, d), lambda i, j: (0, 0)),
                  pl.BlockSpec((tn, d), lambda i, j: (j, 0)),
                  pl.BlockSpec((LANES, d), lambda i, j: (0, 0))],
        out_specs=[pl.BlockSpec((tm, tn), lambda i, j: (i, j)),
                   pl.BlockSpec((tm, LANES), lambda i, j: (i, 0))],
        out_shape=[jax.ShapeDtypeStruct((n, PROJ_MAIN), BF16),
                   jax.ShapeDtypeStruct((n, LANES), F32)],
        scratch_shapes=[pltpu.VMEM((tm, d), BF16), pltpu.VMEM((tm, d), F32),
                        pltpu.SemaphoreType.DMA(())],
        compiler_params=_cp(("arbitrary", "arbitrary")),
        name="inproj",
    )(x2, mod3, g.reshape(1, d), w_t, wg_t)


def _retention_kernel(q_ref, k_ref, v_ref, g_ref, cos_ref, sin_ref, din_ref, zeta_ref,
                      qdec_ref, cdec_ref, gn_ref, o_ref, s_ref):
    @pl.when(pl.program_id(1) == 0)
    def _():
        s_ref[...] = jnp.zeros_like(s_ref)

    half = RET_DK // 2
    c = RET_CHUNK

    for sub in range(q_ref.shape[0] // c):
        rows = slice(sub * c, (sub + 1) * c)
        cos = cos_ref[rows, :]
        sin = sin_ref[rows, :]

        def rot(a):
            a1, a2 = a[:, :half], a[:, half:]
            return jnp.concatenate([a1 * cos - a2 * sin, a1 * sin + a2 * cos], axis=1)

        for h in range(RET_HEADS):
            qs = slice(h * RET_DK, (h + 1) * RET_DK)
            vs = slice(h * RET_DV, (h + 1) * RET_DV)
            q = rot(q_ref[rows, qs].astype(F32))
            k = rot(k_ref[rows, qs].astype(F32)) * (RET_DK ** -0.5)
            v = v_ref[rows, vs]
            qb = q.astype(BF16)
            kb = k.astype(BF16)
            s = _dot_nt(qb, kb) * din_ref[h]
            inner = _dot(s.astype(BF16), v)
            s_prev = s_ref[h]
            cross = _dot(qb, s_prev.astype(BF16)) * qdec_ref[h]
            kv = _dot_tn((k * zeta_ref[h]).astype(BF16), v)
            s_ref[h] = cdec_ref[h] * s_prev + kv
            o = inner + cross
            mu = jnp.mean(o, axis=-1, keepdims=True)
            oc = o - mu
            var = jnp.mean(oc * oc, axis=-1, keepdims=True)
            o = oc * lax.rsqrt(var + EPS) * gn_ref[:, vs]
            o_ref[rows, vs] = (o * _silu(g_ref[rows, vs].astype(F32))).astype(BF16)


def _retention(proj, gn_g, bsz, seq):
    n = proj.shape[0]
    c = RET_CHUNK
    nc = seq // c
    hw = RET_HEADS * RET_DK
    half = RET_DK // 2
    pos = jnp.arange(seq, dtype=F32)
    inv = ROPE_BASE ** (-jnp.arange(half, dtype=F32) / half)
    ang = pos[:, None] * inv[None, :]
    cos, sin = jnp.cos(ang), jnp.sin(ang)
    log_gamma = jnp.log1p(-jnp.exp2(-5.0 - jnp.arange(RET_HEADS, dtype=F32)))
    idx = jnp.arange(c, dtype=F32)
    rel = idx[:, None] - idx[None, :]
    decay_in = jnp.where(rel >= 0, jnp.exp(log_gamma[:, None, None] * jnp.maximum(rel, 0.0)), 0.0)
    zeta = jnp.exp(log_gamma[:, None] * (c - 1 - idx)[None, :])[:, :, None]
    q_decay = jnp.exp(log_gamma[:, None] * (idx + 1)[None, :])[:, :, None]
    chunk_decay = jnp.exp(log_gamma * c)[:, None, None]
    rb = min(RET_CHUNKS * c, seq)
    ns = seq // rb
    row = lambda b, t: (b * ns + t)
    return pl.pallas_call(
        _retention_kernel,
        grid=(bsz, ns),
        in_specs=[pl.BlockSpec((rb, hw), lambda b, t: (row(b, t), 0)),
                  pl.BlockSpec((rb, hw), lambda b, t: (row(b, t), 1)),
                  pl.BlockSpec((rb, hw), lambda b, t: (row(b, t), 2)),
                  pl.BlockSpec((rb, hw), lambda b, t: (row(b, t), 3)),
                  pl.BlockSpec((rb, half), lambda b, t: (t, 0)),
                  pl.BlockSpec((rb, half), lambda b, t: (t, 0)),
                  pl.BlockSpec((RET_HEADS, c, c), lambda b, t: (0, 0, 0)),
                  pl.BlockSpec((RET_HEADS, c, 1), lambda b, t: (0, 0, 0)),
                  pl.BlockSpec((RET_HEADS, c, 1), lambda b, t: (0, 0, 0)),
                  pl.BlockSpec((RET_HEADS, 1, 1), lambda b, t: (0, 0, 0)),
                  pl.BlockSpec((1, hw), lambda b, t: (0, 0))],
        out_specs=pl.BlockSpec((rb, hw), lambda b, t: (row(b, t), 0)),
        out_shape=jax.ShapeDtypeStruct((n, hw), BF16),
        scratch_shapes=[pltpu.VMEM((RET_HEADS, RET_DK, RET_DV), F32)],
        compiler_params=_cp(("arbitrary", "arbitrary")),
        name="retention",
    )(proj, proj, proj, proj, cos, sin, decay_in, zeta, q_decay, chunk_decay,
      gn_g.reshape(1, hw))


def _compress_kernel(ak_ref, av_ref, pek_ref, pev_ref, w1k_ref, w1v_ref, w2k_ref, w2v_ref,
                     kc_ref, vc_ref, a_scr):
    seq = ak_ref.shape[0]
    nblk = seq // CMP_STRIDE
    a_scr[seq:, :] = jnp.zeros((a_scr.shape[0] - seq, NSA_D), F32)

    def one(a_ref, pe_ref, w1_ref, w2_ref, o_ref):
        a_scr[:seq, :] = a_ref[...].astype(F32)
        flat = jnp.concatenate([a_scr[pl.ds(l, nblk, stride=CMP_STRIDE), :] for l in range(CMP_BLOCK)],
                               axis=1)
        pre = _dot((flat + pe_ref[...]).astype(BF16), w1_ref[...].astype(BF16))
        o_ref[...] = _dot(_silu(pre).astype(BF16), w2_ref[...].astype(BF16)).astype(BF16)

    one(ak_ref, pek_ref, w1k_ref, w2k_ref, kc_ref)
    one(av_ref, pev_ref, w1v_ref, w2v_ref, vc_ref)


def _compress(proj, pek, pev, w1k, w1v, w2k, w2v, bsz, seq):
    g_ = NSA_KV_GROUPS
    nblk = seq // CMP_STRIDE
    full = lambda a: pl.BlockSpec(a.shape, lambda i: (0,) * a.ndim)
    kblk = pl.BlockSpec((seq, NSA_D), lambda i: (i // g_, _C_KC + i % g_))
    vblk = pl.BlockSpec((seq, NSA_D), lambda i: (i // g_, _C_VC + i % g_))
    oblk = pl.BlockSpec((nblk, NSA_D), lambda i: (i, 0))
    return pl.pallas_call(
        _compress_kernel,
        grid=(bsz * g_,),
        in_specs=[kblk, vblk, full(pek), full(pev), full(w1k), full(w1v), full(w2k), full(w2v)],
        out_specs=[oblk, oblk],
        out_shape=[jax.ShapeDtypeStruct((bsz * g_ * nblk, NSA_D), BF16)] * 2,
        scratch_shapes=[pltpu.VMEM((seq + CMP_BLOCK, NSA_D), F32)],
        compiler_params=_cp(("arbitrary",)),
        name="compress",
    )(proj, proj, pek, pev, w1k, w1v, w2k, w2v)


def _nsa_kernel(q_ref, kc_ref, vc_ref, ks_ref, vs_ref, kw_ref, vw_ref, gate_ref, ovt_ref,
                o_ref, kaug_ref, vst_ref, vwt_ref, *, tq, tk, vt, seq, ncmp, wlen):
    crow = kc_ref.shape[0]
    i = pl.program_id(2)
    hg = NSA_HPG
    d = NSA_D
    r = hg * tq

    @pl.when(i == 0)
    def _():
        kaug_ref[:, :d] = ks_ref[...]
        blk = lax.broadcasted_iota(I32, (seq, LANES), 0) // SEL_BLOCK
        lane = lax.broadcasted_iota(I32, (seq, LANES), 1)
        kaug_ref[:, d:] = (blk == lane).astype(BF16)
        ones = jnp.ones((vst_ref.shape[1] - d, vt), BF16)
        for c in range(seq // vt):
            vst_ref[c, :d, :] = vs_ref[c * vt:(c + 1) * vt, :].astype(F32).T.astype(BF16)
            vwt_ref[c, :d, :] = vw_ref[c * vt:(c + 1) * vt, :].astype(F32).T.astype(BF16)
            vst_ref[c, d:, :] = ones
            vwt_ref[c, d:, :] = ones

    def pv_t(vt_ref, first_tile, p):
        out = None
        for c in range(p.shape[0] // vt):
            part = _dot(vt_ref[first_tile + c], p[c * vt:(c + 1) * vt, :])
            out = part if out is None else out + part
        return out

    q = q_ref[...]
    qh = [(q[:, h * d:(h + 1) * d].astype(F32) * (d ** -0.5)).astype(BF16) for h in range(hg)]
    qa = jnp.concatenate(qh, axis=0)
    t_row = i * tq + lax.broadcasted_iota(I32, (1, tq), 1)
    tpos_row = jnp.concatenate([t_row] * hg, axis=1)

    w0 = pl.multiple_of(jnp.maximum(i * tq + tq - wlen, 0), vt)
    sw = _dot_nt(kw_ref[pl.ds(w0, wlen), :], qa)
    delta = tpos_row - (w0 + lax.broadcasted_iota(I32, (wlen, 1), 0))
    sw = jnp.where((delta >= 0) & (delta < WIN), sw, NEG)
    e_w = jnp.exp(sw - jnp.max(sw, axis=0, keepdims=True))
    pv_w = pv_t(vwt_ref, w0 // vt, e_w.astype(BF16))
    o_win = (pv_w[:d] / pv_w[d:d + 1]).T

    sct = _dot_nt(kc_ref[...], qa)
    cidx = lax.broadcasted_iota(I32, (crow, 1), 0)
    cmask = (cidx * CMP_STRIDE + (CMP_BLOCK - 1) <= tpos_row) & (cidx < ncmp)
    sct = jnp.where(cmask, sct, NEG)
    e_c = jnp.where(cmask, jnp.exp(sct - jnp.max(sct, axis=0, keepdims=True)), 0.0)
    den_c = jnp.sum(e_c, axis=0, keepdims=True)
    p_t = jnp.where(den_c > 0.0, e_c / jnp.where(den_c > 0.0, den_c, 1.0), 0.0)
    o_cmp = _dot_tn(p_t.astype(BF16), vc_ref[...])

    psum_t = p_t[:, 0:tq]
    for h in range(1, hg):
        psum_t = psum_t + p_t[:, h * tq:(h + 1) * tq]
    nsel_blocks = seq // SEL_BLOCK
    nsb = ovt_ref.shape[0]
    imp_t = jnp.dot(ovt_ref[...], psum_t, preferred_element_type=F32,
                    precision=lax.Precision.HIGHEST)
    sidx = lax.broadcasted_iota(I32, (nsb, 1), 0)
    cur = t_row // SEL_BLOCK
    valid = sidx <= cur
    forced = (sidx == 0) | (sidx == cur) | (sidx == cur - 1)
    score = jnp.where(valid, imp_t + jnp.where(forced, FORCE_BONUS, 0.0), -1.0)
    rank = jnp.zeros((nsb, tq), F32)
    for s2 in range(nsel_blocks):
        row = score[s2:s2 + 1, :]
        beats = (row > score) | ((row == score) & (sidx > s2))
        rank = rank + beats.astype(F32)
    sel = valid & (rank < float(min(SEL_COUNT, nsel_blocks)))
    selb_t = jnp.where(sel, 0.0, NEG)
    selb_t = jnp.concatenate([selb_t, jnp.zeros((LANES - nsb, tq), F32)], axis=0)
    selb = selb_t.T.astype(BF16)
    q_aug = jnp.concatenate([jnp.concatenate([qh[h], selb], axis=1) for h in range(hg)], axis=0)

    n_full = (i * tq + 1) // tk
    kl = lax.broadcasted_iota(I32, (tk, 1), 0)

    def tile(j, carry, masked):
        m, acc = carry
        k0 = pl.multiple_of(j * tk, tk)
        s = _dot_nt(kaug_ref[pl.ds(k0, tk), :], q_aug)
        if masked:
            s = jnp.where(k0 + kl <= tpos_row, s, NEG)
        m_new = jnp.maximum(m, jnp.max(s, axis=0, keepdims=True))
        p = jnp.exp(s - m_new)
        acc = jnp.exp(m - m_new) * acc + pv_t(vst_ref, j * (tk // vt), p.astype(BF16))
        return m_new, acc

    m0 = jnp.full((1, r), NEG, F32)
    a0 = jnp.zeros((vst_ref.shape[1], r), F32)
    carry = lax.fori_loop(0, n_full, functools.partial(tile, masked=False), (m0, a0))
    _, acc_s = tile(n_full, carry, True)
    o_slc = (acc_s[:d] / acc_s[d:d + 1]).T

    gt = jax.nn.sigmoid(gate_ref[...])
    grp = pl.program_id(1)
    for g in range(1, NSA_KV_GROUPS):
        gt = jnp.where(grp == g, pltpu.roll(gt, LANES - g * 3 * hg, 1), gt)
    for h in range(hg):
        rows = slice(h * tq, (h + 1) * tq)
        o = (gt[:, 3 * h:3 * h + 1] * o_cmp[rows] + gt[:, 3 * h + 1:3 * h + 2] * o_slc[rows]
             + gt[:, 3 * h + 2:3 * h + 3] * o_win[rows])
        o_ref[:, h * d:(h + 1) * d] = o.astype(BF16)


def _nsa(proj, kc, vc, gates_g, bsz, seq):
    n = proj.shape[0]
    g_ = NSA_KV_GROUPS
    tq = NSA_TQ
    tk = min(NSA_TK, seq)
    nq = seq // tq
    ncmp = (seq - CMP_BLOCK) // CMP_STRIDE + 1
    wlen = min(WIN + tq, seq)
    nsel = seq // SEL_BLOCK
    crow = seq // CMP_STRIDE
    assert nsel <= LANES and crow <= LANES and tq <= tk and tk % tq == 0
    nsb = -(-nsel // 8) * 8
    ss = np.arange(nsb)[:, None] * SEL_BLOCK
    cs = np.arange(crow)[None, :] * CMP_STRIDE
    ov = ((cs < ss + SEL_BLOCK) & (cs + CMP_BLOCK > ss)
          & (np.arange(crow)[None, :] < ncmp) & (np.arange(nsb)[:, None] < nsel))
    ov = jnp.asarray(ov.astype(np.float32))
    kvspec = lambda c0: pl.BlockSpec((seq, NSA_D), lambda b, g, i: (b, c0 + g))
    vt = NSA_VT
    assert tk % vt == 0 and wlen % vt == 0 and tq % vt == 0 and seq % vt == 0
    kern = functools.partial(_nsa_kernel, tq=tq, tk=tk, vt=vt, seq=seq, ncmp=ncmp, wlen=wlen)
    return pl.pallas_call(
        kern,
        grid=(bsz, g_, nq),
        in_specs=[pl.BlockSpec((tq, NSA_HPG * NSA_D), lambda b, g, i: (b * nq + i, _C_NQ // NSA_HPG + g)),
                  pl.BlockSpec((crow, NSA_D), lambda b, g, i: (b * g_ + g, 0)),
                  pl.BlockSpec((crow, NSA_D), lambda b, g, i: (b * g_ + g, 0)),
                  kvspec(_C_KS), kvspec(_C_VS), kvspec(_C_KW), kvspec(_C_VW),
                  pl.BlockSpec((tq, LANES), lambda b, g, i: (b * nq + i, 0)),
                  pl.BlockSpec((nsb, crow), lambda b, g, i: (0, 0))],
        out_specs=pl.BlockSpec((tq, NSA_HPG * NSA_D), lambda b, g, i: (b * nq + i, g)),
        out_shape=jax.ShapeDtypeStruct((n, NSA_HEADS * NSA_D), BF16),
        scratch_shapes=[pltpu.VMEM((seq, 2 * NSA_D), BF16),
                        pltpu.VMEM((seq // vt, NSA_D + 2 * SUBLANES, vt), BF16),
                        pltpu.VMEM((seq // vt, NSA_D + 2 * SUBLANES, vt), BF16)],
        compiler_params=_cp(("arbitrary", "arbitrary", "arbitrary")),
        name="nsa",
    )(proj, kc, vc, proj, proj, proj, proj, gates_g, ov)


def _outproj_kernel(oret_ref, onsa_ref, x_ref, mod_ref, g_ref, w_ref, wr_ref,
                    x1_ref, h2_ref, lt_ref):
    hw = oret_ref.shape[1]
    mix = (_dot(oret_ref[...], w_ref[:hw, :].astype(BF16))
           + _dot(onsa_ref[...], w_ref[hw:, :].astype(BF16)))
    x1 = x_ref[...] + mod_ref[0, 2:3, :] * mix
    x1_ref[...] = x1
    y = x1 * lax.rsqrt(jnp.mean(x1 * x1, axis=-1, keepdims=True) + EPS) * g_ref[...]
    h2 = y * (1.0 + mod_ref[0, 4:5, :]) + mod_ref[0, 3:4, :]
    h2_ref[...] = h2
    lt_ref[...] = _dot_nt(wr_ref[...], h2.astype(BF16))


def _outproj(o_ret, o_nsa, x2, mod3, g, w, wr_bf, seq):
    n, d = x2.shape
    tm = min(OUTPROJ_TM, seq)
    hw = o_ret.shape[1]
    return pl.pallas_call(
        _outproj_kernel,
        grid=(n // tm,),
        in_specs=[pl.BlockSpec((tm, hw), lambda i: (i, 0)),
                  pl.BlockSpec((tm, o_nsa.shape[1]), lambda i: (i, 0)),
                  pl.BlockSpec((tm, d), lambda i: (i, 0)),
                  pl.BlockSpec((1, 6, d), lambda i: ((i * tm) // seq, 0, 0)),
                  pl.BlockSpec((1, d), lambda i: (0, 0)),
                  pl.BlockSpec(w.shape, lambda i: (0, 0), pipeline_mode=pl.Buffered(1)),
                  pl.BlockSpec(wr_bf.shape, lambda i: (0, 0))],
        out_specs=[pl.BlockSpec((tm, d), lambda i: (i, 0)),
                   pl.BlockSpec((tm, d), lambda i: (i, 0)),
                   pl.BlockSpec((LANES, tm), lambda i: (0, i))],
        out_shape=[jax.ShapeDtypeStruct((n, d), F32),
                   jax.ShapeDtypeStruct((n, d), F32),
                   jax.ShapeDtypeStruct((LANES, n), F32)],
        compiler_params=_cp(("arbitrary",)),
        name="outproj",
    )(o_ret, o_nsa, x2, mod3, g.reshape(1, d), w, wr_bf)


def _route_kernel(lt_ref, b_ref, tri_ref, ids_ref, wts_ref, cnt_ref, carry_ref, *, sub):
    @pl.when(pl.program_id(0) == 0)
    def _():
        carry_ref[...] = jnp.zeros_like(carry_ref)

    ng, ne = N_GROUPS, EXP_PER_GROUP
    l = lt_ref[...] + b_ref[:, 0:1]
    tc = l.shape[1]
    ridx = lax.broadcasted_iota(I32, (ng, tc), 0).astype(F32)

    def softmax0(v):
        e = jnp.exp(v - jnp.max(v, axis=0, keepdims=True))
        return e / jnp.sum(e, axis=0, keepdims=True)

    def top1(p):
        top = jnp.max(p, axis=0, keepdims=True)
        idx = jnp.min(jnp.where(p == top, ridx, float(ng)), axis=0, keepdims=True)
        return top, idx

    pg_top, grp = top1(softmax0(l[0:ng]))
    leg = jnp.zeros((ne, tc), F32)
    for g in range(ng):
        leg = jnp.where(grp == float(g), l[ng + g * ne:ng + (g + 1) * ne], leg)
    pe = softmax0(leg)
    p1, i1 = top1(pe)
    p2, i2 = top1(jnp.where(ridx == i1, -1.0, pe))
    den = p1 + p2
    w1 = pg_top * p1 / den
    w2 = pg_top * p2 / den
    e1 = grp * float(ne) + i1
    e2 = grp * float(ne) + i2

    eio = lax.broadcasted_iota(I32, (N_EXPERTS, sub), 0).astype(F32)
    r1 = []
    r2 = []
    carry = carry_ref[:, 0:1]
    for c in range(tc // sub):
        cs = slice(c * sub, (c + 1) * sub)
        oh1 = (eio == e1[:, cs]).astype(F32)
        oh2 = (eio == e2[:, cs]).astype(F32)
        oh = oh1 + oh2
        before = carry + _dot(oh.astype(BF16), tri_ref[...])
        r1.append(jnp.sum(oh1 * before, axis=0, keepdims=True))
        r2.append(jnp.sum(oh2 * before, axis=0, keepdims=True))
        carry = carry + jnp.sum(oh, axis=1, keepdims=True)
    carry_ref[...] = jnp.broadcast_to(carry, carry_ref.shape)
    cnt_ref[...] = jnp.broadcast_to(carry, cnt_ref.shape).astype(I32)
    r1 = jnp.concatenate(r1, axis=1)
    r2 = jnp.concatenate(r2, axis=1)
    zf = jnp.zeros((4, tc), F32)
    ids_ref[...] = jnp.concatenate([e1, e2, r1, r2, zf], axis=0).astype(I32)
    wts_ref[...] = jnp.concatenate([w1, w2, jnp.zeros((6, tc), F32)], axis=0)


def _route(lt, bias_col):
    n = lt.shape[1]
    tc = min(ROUTE_TC, n)
    sub = min(ROUTE_SUB, tc)
    tri =jnp.asarray(np.triu(np.ones((sub, sub), np.float32), 1), BF16)
    return pl.pallas_call(
        functools.partial(_route_kernel, sub=sub),
        grid=(n // tc,),
        in_specs=[pl.BlockSpec((LANES, tc), lambda i: (0, i)),
                  pl.BlockSpec((LANES, LANES), lambda i: (0, 0)),
                  pl.BlockSpec((sub, sub), lambda i: (0, 0))],
        out_specs=[pl.BlockSpec((8, tc), lambda i: (0, i)),
                   pl.BlockSpec((8, tc), lambda i: (0, i)),
                   pl.BlockSpec((N_EXPERTS, LANES), lambda i: (0, 0))],
        out_shape=[jax.ShapeDtypeStruct((8, n), I32),
                   jax.ShapeDtypeStruct((8, n), F32),
                   jax.ShapeDtypeStruct((N_EXPERTS, LANES), I32)],
        scratch_shapes=[pltpu.VMEM((N_EXPERTS, LANES), F32)],
        compiler_params=_cp(("arbitrary",)),
        name="route",
    )(lt, bias_col, tri)


def _dest_kernel(ids_ref, ps_ref, o_ref):
    ids = ids_ref[...].astype(F32)
    tc = ids.shape[1]
    eio = lax.broadcasted_iota(I32, (N_EXPERTS, tc), 0).astype(F32)
    ps = ps_ref[:, 0:1]
    rows = [jnp.sum(jnp.where(eio == ids[k:k + 1], ps, 0.0), axis=0, keepdims=True) + ids[2 + k:3 + k]
            for k in range(2)]
    o_ref[...] = jnp.concatenate(rows + [jnp.zeros((6, tc), F32)], axis=0).astype(I32)


def _dest(ids, pad_start):
    n = ids.shape[1]
    tc = min(ROUTE_TC, n)
    ps = jnp.broadcast_to(pad_start.astype(F32)[:, None], (N_EXPERTS, LANES))
    dest = pl.pallas_call(
        _dest_kernel,
        grid=(n // tc,),
        in_specs=[pl.BlockSpec((8, tc), lambda i: (0, i)),
                  pl.BlockSpec((N_EXPERTS, LANES), lambda i: (0, 0))],
        out_specs=pl.BlockSpec((8, tc), lambda i: (0, i)),
        out_shape=jax.ShapeDtypeStruct((8, n), I32),
        compiler_params=_cp(("arbitrary",)),
        name="dest",
    )(ids, ps)
    return dest[:2].reshape(2 * n)


def _dispatch_kernel(dst0_ref, dst1_ref, ps_ref, cnt_ref, h_ref, xs_ref, zbuf, sem, zsem, *, tcd, tm, zr):
    @pl.when(pl.program_id(0) == 0)
    def _():
        zbuf[...] = jnp.zeros_like(zbuf)
        sizes = [zr >> b for b in range(zr.bit_length()) if (zr >> b) >= SUBLANES]

        def zero_rows(row, size):
            return pltpu.make_async_copy(zbuf.at[pl.ds(0, size)],
                                         xs_ref.at[pl.ds(pl.multiple_of(row, SUBLANES), size)], zsem)

        def zero_row(row):
            return pltpu.make_async_copy(zbuf.at[pl.ds(0, 1)], xs_ref.at[pl.ds(row, 1)], zsem)

        def fill(e, wait):
            cnt = cnt_ref[e]
            cnt8 = lax.div(cnt + (SUBLANES - 1), SUBLANES) * SUBLANES
            gap = lax.rem(tm - lax.rem(cnt8, tm), tm)
            base = ps_ref[e]

            def single(c, _):
                cp = zero_row(base + cnt + c)
                cp.wait() if wait else cp.start()
                return 0

            lax.fori_loop(0, cnt8 - cnt, single, 0)
            row = base + cnt8
            for size in sizes:
                has = lax.rem(lax.div(gap, size), 2) == 1

                @pl.when(has)
                def _():
                    cp = zero_rows(row, size)
                    cp.wait() if wait else cp.start()

                row = row + jnp.where(has, size, 0)

        last = N_EXPERTS - 1
        used_rows = ps_ref[last] + lax.div(cnt_ref[last] + (tm - 1), tm) * tm
        n_tail = lax.div(xs_ref.shape[0] - used_rows, zr)

        def tail(c, wait):
            cp = zero_rows(used_rows + c * zr, zr)
            cp.wait() if wait else cp.start()

        for wait in (False, True):
            lax.fori_loop(0, N_EXPERTS, lambda e, _: (fill(e, wait), 0)[1], 0)
            lax.fori_loop(0, n_tail, lambda c, _: (tail(c, wait), 0)[1], 0)

    def issue(t, _):
        for k, dst_ref in enumerate((dst0_ref, dst1_ref)):
            dst = dst_ref[t]
            pltpu.make_async_copy(h_ref.at[pl.ds(t, 1)], xs_ref.at[pl.ds(dst, 1)], sem).start(priority=k)
        return 0

    lax.fori_loop(0, tcd, issue, 0, unroll=8)
    for k in range(2):
        pltpu.make_async_copy(h_ref, xs_ref.at[pl.ds(0, tcd)], sem).wait()


def _dispatch(dest, pad_start, counts, h2, rows, tcd, tm):
    n, d = h2.shape
    assert tm & (tm - 1) == 0 and tm >= 2 * SUBLANES
    zr = tm // 2
    return pl.pallas_call(
        functools.partial(_dispatch_kernel, tcd=tcd, tm=tm, zr=zr),
        grid=(n // tcd,),
        in_specs=[pl.BlockSpec((tcd,), lambda i: (i,), memory_space=pltpu.SMEM),
                  pl.BlockSpec((tcd,), lambda i: (n // tcd + i,), memory_space=pltpu.SMEM),
                  pl.BlockSpec(memory_space=pltpu.SMEM),
                  pl.BlockSpec(memory_space=pltpu.SMEM),
                  pl.BlockSpec((tcd, d), lambda i: (i, 0))],
        out_specs=pl.BlockSpec(memory_space=pl.ANY),
        out_shape=jax.ShapeDtypeStruct((rows, d), h2.dtype),
        scratch_shapes=[pltpu.VMEM((zr, d), h2.dtype), pltpu.SemaphoreType.DMA(()),
                        pltpu.SemaphoreType.DMA(())],
        compiler_params=_cp(("arbitrary",)),
        name="dispatch",
    )(dest, dest, pad_start, counts, h2)


def _experts_kernel(be_ref, ord_ref, nxt_ref, nu_ref, xs_ref, wg_hbm, wu_hbm, wd_hbm, ys_ref,
                    g_f, u_f, d_f, g_s, u_s, d_s, sem):
    i = pl.program_id(0)
    used = i < nu_ref[0]
    e = be_ref[i]
    fresh = (i == 0) | (e != be_ref[jnp.maximum(i - 1, 0)])
    slot = lax.rem(ord_ref[i], 2)

    def fetch(expert, s):
        return [pltpu.make_async_copy(w.at[expert], f.at[s], sem.at[s, k])
                for k, (w, f) in enumerate(((wg_hbm, g_f), (wu_hbm, u_f), (wd_hbm, d_f)))]

    @pl.when(used & (i == 0))
    def _():
        for c in fetch(e, slot):
            c.start(priority=1)

    @pl.when(used & fresh)
    def _():
        for c in fetch(e, slot):
            c.wait()
        nxt = nxt_ref[i]

        @pl.when(nxt >= 0)
        def _():
            for c in fetch(nxt, 1 - slot):
                c.start(priority=1)

        g_s[...] = g_f[slot].astype(BF16)
        u_s[...] = u_f[slot].astype(BF16)
        d_s[...] = d_f[slot].astype(BF16)

    @pl.when(used)
    def _():
        x = xs_ref[...].astype(BF16)
        a = _silu(_dot(x, g_s[...])) * _dot(x, u_s[...])
        ys_ref[...] = _dot(a.astype(BF16), d_s[...])

    @pl.when(jnp.logical_not(used))
    def _():
        ys_ref[...] = jnp.zeros_like(ys_ref)


def _experts(blk_exp, blk_ord, blk_nxt, n_used, xs, w_gate, w_up, w_down, tm):
    d = xs.shape[1]
    de = w_gate.shape[2]
    nb = blk_exp.shape[0]
    row = lambda i, be, od, nx, nu: (jnp.minimum(i, nu[0] - 1), 0)
    anyspec = pl.BlockSpec(memory_space=pl.ANY)
    grid_spec = pltpu.PrefetchScalarGridSpec(
        num_scalar_prefetch=4,
        grid=(nb,),
        in_specs=[pl.BlockSpec((tm, d), row), anyspec, anyspec, anyspec],
        out_specs=pl.BlockSpec((tm, d), lambda i, be, od, nx, nu: (i, 0)),
        scratch_shapes=[pltpu.VMEM((2, d, de), F32), pltpu.VMEM((2, d, de), F32),
                        pltpu.VMEM((2, de, d), F32),
                        pltpu.VMEM((d, de), BF16), pltpu.VMEM((d, de), BF16),
                        pltpu.VMEM((de, d), BF16),
                        pltpu.SemaphoreType.DMA((2, 3))],
    )
    return pl.pallas_call(
        _experts_kernel,
        grid_spec=grid_spec,
        out_shape=jax.ShapeDtypeStruct((nb * tm, d), F32),
        compiler_params=_cp(("arbitrary",)),
        name="experts",
    )(blk_exp, blk_ord, blk_nxt, n_used, xs, w_gate, w_up, w_down)


def _combine_kernel(cur0_ref, cur1_ref, nxt0_ref, nxt1_ref, ys_ref, x1_ref, wt_ref, mod_ref, g_ref,
                    o_ref, buf, sem, *, tc):
    i = pl.program_id(0)
    slot = lax.rem(i, 2)

    def gather(id_refs, s):
        def issue(t, _):
            for k, id_ref in enumerate(id_refs):
                src = id_ref[t]
                pltpu.make_async_copy(ys_ref.at[pl.ds(src, 1)], buf.at[s, k, pl.ds(t, 1)],
                                      sem.at[s]).start(priority=k)
            return 0

        lax.fori_loop(0, tc, issue, 0, unroll=8)

    @pl.when(i == 0)
    def _():
        gather((cur0_ref, cur1_ref), 0)

    @pl.when(i + 1 < pl.num_programs(0))
    def _():
        gather((nxt0_ref, nxt1_ref), 1 - slot)

    for k in range(2):
        pltpu.make_async_copy(ys_ref.at[pl.ds(0, tc)], buf.at[slot, k], sem.at[slot]).wait()

    w_rows = wt_ref[...]
    w_cols = jnp.concatenate([w_rows, jnp.zeros((LANES - w_rows.shape[0], tc), F32)], axis=0).T
    moe = buf[slot, 0] * w_cols[:, 0:1] + buf[slot, 1] * w_cols[:, 1:2]
    x2 = x1_ref[...] + mod_ref[0, 5:6, :] * moe
    o_ref[...] = x2 * lax.rsqrt(jnp.mean(x2 * x2, axis=-1, keepdims=True) + EPS) * g_ref[...]


def _combine(dest, ys, x1, wts, mod3, final_g, seq, tc):
    n, d = x1.shape
    nt = n // tc
    tbl = lambda k, off: pl.BlockSpec((tc,), lambda i: (k * nt + jnp.minimum(i + off, nt - 1),),
                                      memory_space=pltpu.SMEM)
    return pl.pallas_call(
        functools.partial(_combine_kernel, tc=tc),
        grid=(nt,),
        in_specs=[tbl(0, 0), tbl(1, 0), tbl(0, 1), tbl(1, 1),
                  pl.BlockSpec(memory_space=pl.ANY),
                  pl.BlockSpec((tc, d), lambda i: (i, 0)),
                  pl.BlockSpec((8, tc), lambda i: (0, i)),
                  pl.BlockSpec((1, 6, d), lambda i: ((i * tc) // seq, 0, 0)),
                  pl.BlockSpec((1, d), lambda i: (0, 0))],
        out_specs=pl.BlockSpec((tc, d), lambda i: (i, 0)),
        out_shape=jax.ShapeDtypeStruct((n, d), F32),
        scratch_shapes=[pltpu.VMEM((2, 2, tc, d), F32), pltpu.SemaphoreType.DMA((2,))],
        compiler_params=_cp(("arbitrary",)),
        name="combine",
    )(dest, dest, dest, dest, ys, x1, wts, mod3, final_g.reshape(1, d))


def _token_mixer(x2, mod3, norm1_g, w_in, ret_gn_g, cmp_pos_k, cmp_w1_k, cmp_w2_k,
                 cmp_pos_v, cmp_w1_v, cmp_w2_v, bsz, seq):
    n, d = x2.shape
    w_t = w_in.T
    w_gate_t = jnp.pad(w_t[PROJ_MAIN:], ((0, LANES - N_GATE_COLS), (0, 0))).astype(BF16)
    proj, gate_logits = _inproj(x2, mod3, norm1_g, w_t, w_gate_t, seq)

    o_ret = _retention(proj, ret_gn_g, bsz, seq)

    g_ = NSA_KV_GROUPS
    crow = seq // CMP_STRIDE

    pe_flat = lambda pe: pe.reshape(1, CMP_BLOCK * NSA_D)
    kc, vc = _compress(proj, pe_flat(cmp_pos_k), pe_flat(cmp_pos_v),
                       cmp_w1_k, cmp_w1_v, cmp_w2_k, cmp_w2_v, bsz, seq)

    o_nsa = _nsa(proj, kc, vc, gate_logits, bsz, seq)
    return o_ret, o_nsa


def _moe(h2, lt, x1, mod3, final_g, b_grp, b_exp, w_gate, w_up, w_down, seq, tm):
    n, d = h2.shape
    bias_col = jnp.zeros((LANES,), F32).at[:N_GROUPS].set(b_grp).at[N_GROUPS:N_GROUPS + N_EXPERTS].set(b_exp)
    bias_col = jnp.broadcast_to(bias_col[:, None], (LANES, LANES))
    ids, wts, cnt = _route(lt, bias_col)

    counts = cnt[:, 0]
    padded = (counts + tm - 1) // tm * tm
    pad_end = jnp.cumsum(padded)
    pad_start = (pad_end - padded).astype(I32)
    nb = (2 * n) // tm + N_EXPERTS
    n_used = (pad_end[-1] // tm).astype(I32).reshape(1)
    blk_start = jnp.arange(nb, dtype=I32) * tm
    blk_exp = jnp.minimum(jnp.sum((pad_end[None, :] <= blk_start[:, None]).astype(I32), axis=1),
                          N_EXPERTS - 1).astype(I32)
    last_exp = blk_exp[jnp.maximum(n_used[0] - 1, 0)]
    blk_exp = jnp.where(jnp.arange(nb) < n_used[0], blk_exp, last_exp)
    eid = jnp.arange(N_EXPERTS, dtype=I32)
    has = counts > 0
    exp_ord = jnp.sum((has[None, :] & (eid[None, :] < eid[:, None])).astype(I32), axis=1)
    exp_nxt = jnp.min(jnp.where(has[None, :] & (eid[None, :] > eid[:, None]), eid[None, :], N_EXPERTS), axis=1)
    exp_nxt = jnp.where(exp_nxt < N_EXPERTS, exp_nxt, -1).astype(I32)
    blk_ord = exp_ord[blk_exp]
    blk_nxt = exp_nxt[blk_exp]

    dest = _dest(ids, pad_start)
    xs = _dispatch(dest, pad_start, counts.astype(I32), h2, nb * tm, min(DISPATCH_TC, n), tm)
    ys = _experts(blk_exp, blk_ord, blk_nxt, n_used, xs, w_gate, w_up, w_down, tm)
    return _combine(dest, ys, x1, wts, mod3, final_g, seq, min(COMBINE_TC, n))


def kernel(x, c, w_ada, b_ada, norm1_g, norm2_g, final_g, w_in, ret_gn_g, cmp_pos_k, cmp_w1_k,
           cmp_w2_k, cmp_pos_v, cmp_w1_v, cmp_w2_v, w_out, w_grp, b_grp, w_exp, b_exp, w_gate,
           w_up, w_down):
    bsz, seq, d = x.shape
    assert w_ada.shape[0] == 1, "single-layer block"
    n = bsz * seq
    x2 = x.reshape(n, d)
    mod3 = _ada(c, w_ada[0], b_ada[0]).reshape(bsz, 6, d)

    o_ret, o_nsa = _token_mixer(x2, mod3, norm1_g[0], w_in[0], ret_gn_g[0], cmp_pos_k[0],
                                cmp_w1_k[0], cmp_w2_k[0], cmp_pos_v[0], cmp_w1_v[0], cmp_w2_v[0],
                                bsz, seq)

    w_route = jnp.concatenate([w_grp[0], w_exp[0]], axis=1)
    w_route = jnp.pad(w_route, ((0, 0), (0, LANES - w_route.shape[1]))).T.astype(BF16)
    x1, h2, lt = _outproj(o_ret, o_nsa, x2, mod3, norm2_g[0], w_out[0], w_route, seq)

    out = _moe(h2, lt, x1, mod3, final_g, b_grp[0], b_exp[0], w_gate[0], w_up[0], w_down[0],
               seq, MOE_TM)
    return out.reshape(bsz, seq, d)
```

```python
import functools

import numpy as np
import jax
import jax.numpy as jnp
from jax import lax
from jax.experimental import pallas as pl
from jax.experimental.pallas import tpu as pltpu

F32 = jnp.float32
BF16 = jnp.bfloat16
I32 = jnp.int32

RET_HEADS = 4
RET_DK = 256
RET_DV = 256
RET_CHUNK = 128
NSA_HEADS = 8
NSA_KV_GROUPS = 2
NSA_HPG = NSA_HEADS // NSA_KV_GROUPS
NSA_D = 128
CMP_BLOCK = 32
CMP_STRIDE = 16
SEL_BLOCK = 64
SEL_COUNT = 16
WIN = 512
N_GROUPS = 8
EXP_PER_GROUP = 8
N_EXPERTS = N_GROUPS * EXP_PER_GROUP
ROPE_BASE = 10000.0
EPS = 1e-6
NEG = -1e30
FORCE_BONUS = 1e4

LANES = 128
SUBLANES = 8
MXU_DIM = 256
VMEM_LIMIT = 52 * 1024 * 1024

ADA_TN = 1024
INPROJ_TM = 2048
INPROJ_TN = 512
RET_CHUNKS = 4
NSA_TQ = 512
NSA_TK = 512
NSA_VT = MXU_DIM
OUTPROJ_TM = 512
ROUTE_TC = 2048
ROUTE_SUB = 512
MOE_TM = 256
DISPATCH_TC = 1024
COMBINE_TC = 256

_C_NQ = 32
_C_KC, _C_VC, _C_KS, _C_VS, _C_KW, _C_VW = 40, 42, 44, 46, 48, 50
_C_GATE = 52
PROJ_MAIN = _C_GATE * LANES
N_GATE_COLS = NSA_HEADS * 3


def _cp(sem, vmem=VMEM_LIMIT):
    return pltpu.CompilerParams(dimension_semantics=sem, vmem_limit_bytes=vmem)


def _silu(v):
    return v * jax.nn.sigmoid(v)


def _dot(a, b):
    return jnp.dot(a, b, preferred_element_type=F32)


def _dot_nt(a, b):
    return lax.dot_general(a, b, (((1,), (1,)), ((), ())), preferred_element_type=F32)


def _dot_tn(a, b):
    return lax.dot_general(a, b, (((0,), (0,)), ((), ())), preferred_element_type=F32)


def _ada_kernel(c_ref, w_ref, b_ref, o_ref):
    ca = _silu(c_ref[...]).astype(BF16)
    o_ref[...] = _dot(ca, w_ref[...].astype(BF16)) + b_ref[...]


def _ada(c, w, b):
    bsz, d = c.shape
    n = w.shape[1]
    tn = min(ADA_TN, n)
    return pl.pallas_call(
        _ada_kernel,
        grid=(n // tn,),
        in_specs=[pl.BlockSpec((bsz, d), lambda j: (0, 0)),
                  pl.BlockSpec((d, tn), lambda j: (0, j)),
                  pl.BlockSpec((1, tn), lambda j: (0, j))],
        out_specs=pl.BlockSpec((bsz, tn), lambda j: (0, j)),
        out_shape=jax.ShapeDtypeStruct((bsz, n), F32),
        compiler_params=_cp(("arbitrary",)),
        name="ada",
    )(c, w, b.reshape(1, n))


def _inproj_kernel(x_hbm, mod_ref, g_ref, wt_ref, wgt_ref, proj_ref, gate_ref, h_ref, x_buf, x_sem):
    i = pl.program_id(0)
    tm = x_buf.shape[0]

    def x_copy(blk):
        return pltpu.make_async_copy(x_hbm.at[pl.ds(pl.multiple_of(blk * tm, tm), tm)], x_buf, x_sem)

    @pl.when(pl.program_id(1) == 0)
    def _():
        @pl.when(i == 0)
        def _():
            x_copy(i).start()

        x_copy(i).wait()
        x = x_buf[...]
        y = x * lax.rsqrt(jnp.mean(x * x, axis=-1, keepdims=True) + EPS) * g_ref[...]
        h = (y * (1.0 + mod_ref[0, 1:2, :]) + mod_ref[0, 0:1, :]).astype(BF16)
        h_ref[...] = h
        gate_ref[...] = _dot_nt(h, wgt_ref[...])

        @pl.when(i + 1 < pl.num_programs(0))
        def _():
            x_copy(i + 1).start(priority=1)

    proj_ref[...] = _dot_nt(h_ref[...], wt_ref[...].astype(BF16)).astype(BF16)


def _inproj(x2, mod3, g, w_t, wg_t, seq):
    n, d = x2.shape
    tm = min(INPROJ_TM, seq)
    tn = INPROJ_TN
    nj = PROJ_MAIN // tn
    return pl.pallas_call(
        _inproj_kernel,
        grid=(n // tm, nj),
        in_specs=[pl.BlockSpec(memory_space=pl.ANY),
                  pl.BlockSpec((1, 6, d), lambda i, j: ((i * tm) // seq, 0, 0)),
                  pl.BlockSpec((1, d), lambda i, j: (0, 0)),
                  pl.BlockSpec((tn, d), lambda i, j: (j, 0)),
                  pl.BlockSpec((LANES, d), lambda i, j: (0, 0))],
        out_specs=[pl.BlockSpec((tm, tn), lambda i, j: (i, j)),
                   pl.BlockSpec((tm, LANES), lambda i, j: (i, 0))],
        out_shape=[jax.ShapeDtypeStruct((n, PROJ_MAIN), BF16),
                   jax.ShapeDtypeStruct((n, LANES), F32)],
        scratch_shapes=[pltpu.VMEM((tm, d), BF16), pltpu.VMEM((tm, d), F32),
                        pltpu.SemaphoreType.DMA(())],
        compiler_params=_cp(("arbitrary", "arbitrary")),
        name="inproj",
    )(x2, mod3, g.reshape(1, d), w_t, wg_t)


def _retention_kernel(q_ref, k_ref, v_ref, g_ref, cos_ref, sin_ref, din_ref, zeta_ref,
                      qdec_ref, cdec_ref, gn_ref, o_ref, s_ref):
    @pl.when(pl.program_id(1) == 0)
    def _():
        s_ref[...] = jnp.zeros_like(s_ref)

    half = RET_DK // 2
    c = RET_CHUNK

    for sub in range(q_ref.shape[0] // c):
        rows = slice(sub * c, (sub + 1) * c)
        cos = cos_ref[rows, :]
        sin = sin_ref[rows, :]

        def rot(a):
            a1, a2 = a[:, :half], a[:, half:]
            return jnp.concatenate([a1 * cos - a2 * sin, a1 * sin + a2 * cos], axis=1)

        for h in range(RET_HEADS):
            qs = slice(h * RET_DK, (h + 1) * RET_DK)
            vs = slice(h * RET_DV, (h + 1) * RET_DV)
            q = rot(q_ref[rows, qs].astype(F32))
            k = rot(k_ref[rows, qs].astype(F32)) * (RET_DK ** -0.5)
            v = v_ref[rows, vs]
            qb = q.astype(BF16)
            kb = k.astype(BF16)
            s = _dot_nt(qb, kb) * din_ref[h]
            inner = _dot(s.astype(BF16), v)
            s_prev = s_ref[h]
            cross = _dot(qb, s_prev.astype(BF16)) * qdec_ref[h]
            kv = _dot_tn((k * zeta_ref[h]).astype(BF16), v)
            s_ref[h] = cdec_ref[h] * s_prev + kv
            o = inner + cross
            mu = jnp.mean(o, axis=-1, keepdims=True)
            oc = o - mu
            var = jnp.mean(oc * oc, axis=-1, keepdims=True)
            o = oc * lax.rsqrt(var + EPS) * gn_ref[:, vs]
            o_ref[rows, vs] = (o * _silu(g_ref[rows, vs].astype(F32))).astype(BF16)


def _retention(proj, gn_g, bsz, seq):
    n = proj.shape[0]
    c = RET_CHUNK
    nc = seq // c
    hw = RET_HEADS * RET_DK
    half = RET_DK // 2
    pos = jnp.arange(seq, dtype=F32)
    inv = ROPE_BASE ** (-jnp.arange(half, dtype=F32) / half)
    ang = pos[:, None] * inv[None, :]
    cos, sin = jnp.cos(ang), jnp.sin(ang)
    log_gamma = jnp.log1p(-jnp.exp2(-5.0 - jnp.arange(RET_HEADS, dtype=F32)))
    idx = jnp.arange(c, dtype=F32)
    rel = idx[:, None] - idx[None, :]
    decay_in = jnp.where(rel >= 0, jnp.exp(log_gamma[:, None, None] * jnp.maximum(rel, 0.0)), 0.0)
    zeta = jnp.exp(log_gamma[:, None] * (c - 1 - idx)[None, :])[:, :, None]
    q_decay = jnp.exp(log_gamma[:, None] * (idx + 1)[None, :])[:, :, None]
    chunk_decay = jnp.exp(log_gamma * c)[:, None, None]
    rb = min(RET_CHUNKS * c, seq)
    ns = seq // rb
    row = lambda b, t: (b * ns + t)
    return pl.pallas_call(
        _retention_kernel,
        grid=(bsz, ns),
        in_specs=[pl.BlockSpec((rb, hw), lambda b, t: (row(b, t), 0)),
                  pl.BlockSpec((rb, hw), lambda b, t: (row(b, t), 1)),
                  pl.BlockSpec((rb, hw), lambda b, t: (row(b, t), 2)),
                  pl.BlockSpec((rb, hw), lambda b, t: (row(b, t), 3)),
                  pl.BlockSpec((rb, half), lambda b, t: (t, 0)),
                  pl.BlockSpec((rb, half), lambda b, t: (t, 0)),
                  pl.BlockSpec((RET_HEADS, c, c), lambda b, t: (0, 0, 0)),
                  pl.BlockSpec((RET_HEADS, c, 1), lambda b, t: (0, 0, 0)),
                  pl.BlockSpec((RET_HEADS, c, 1), lambda b, t: (0, 0, 0)),
                  pl.BlockSpec((RET_HEADS, 1, 1), lambda b, t: (0, 0, 0)),
                  pl.BlockSpec((1, hw), lambda b, t: (0, 0))],
        out_specs=pl.BlockSpec((rb, hw), lambda b, t: (row(b, t), 0)),
        out_shape=jax.ShapeDtypeStruct((n, hw), BF16),
        scratch_shapes=[pltpu.VMEM((RET_HEADS, RET_DK, RET_DV), F32)],
        compiler_params=_cp(("arbitrary", "arbitrary")),
        name="retention",
    )(proj, proj, proj, proj, cos, sin, decay_in, zeta, q_decay, chunk_decay,
      gn_g.reshape(1, hw))


def _compress_kernel(ak_ref, av_ref, pek_ref, pev_ref, w1k_ref, w1v_ref, w2k_ref, w2v_ref,
                     kc_ref, vc_ref, a_scr):
    seq = ak_ref.shape[0]
    nblk = seq // CMP_STRIDE
    a_scr[seq:, :] = jnp.zeros((a_scr.shape[0] - seq, NSA_D), F32)

    def one(a_ref, pe_ref, w1_ref, w2_ref, o_ref):
        a_scr[:seq, :] = a_ref[...].astype(F32)
        flat = jnp.concatenate([a_scr[pl.ds(l, nblk, stride=CMP_STRIDE), :] for l in range(CMP_BLOCK)],
                               axis=1)
        pre = _dot((flat + pe_ref[...]).astype(BF16), w1_ref[...].astype(BF16))
        o_ref[...] = _dot(_silu(pre).astype(BF16), w2_ref[...].astype(BF16)).astype(BF16)

    one(ak_ref, pek_ref, w1k_ref, w2k_ref, kc_ref)
    one(av_ref, pev_ref, w1v_ref, w2v_ref, vc_ref)


def _compress(proj, pek, pev, w1k, w1v, w2k, w2v, bsz, seq):
    g_ = NSA_KV_GROUPS
    nblk = seq // CMP_STRIDE
    full = lambda a: pl.BlockSpec(a.shape, lambda i: (0,) * a.ndim)
    kblk = pl.BlockSpec((seq, NSA_D), lambda i: (i // g_, _C_KC + i % g_))
    vblk = pl.BlockSpec((seq, NSA_D), lambda i: (i // g_, _C_VC + i % g_))
    oblk = pl.BlockSpec((nblk, NSA_D), lambda i: (i, 0))
    return pl.pallas_call(
        _compress_kernel,
        grid=(bsz * g_,),
        in_specs=[kblk, vblk, full(pek), full(pev), full(w1k), full(w1v), full(w2k), full(w2v)],
        out_specs=[oblk, oblk],
        out_shape=[jax.ShapeDtypeStruct((bsz * g_ * nblk, NSA_D), BF16)] * 2,
        scratch_shapes=[pltpu.VMEM((seq + CMP_BLOCK, NSA_D), F32)],
        compiler_params=_cp(("arbitrary",)),
        name="compress",
    )(proj, proj, pek, pev, w1k, w1v, w2k, w2v)


def _nsa_kernel(q_ref, kc_ref, vc_ref, ks_ref, vs_ref, kw_ref, vw_ref, gate_ref, ovt_ref, wbias_ref, dbias_ref,
                o_ref, kaug_ref, vst_ref, vwt_ref, *, tq, tk, vt, seq, ncmp, wlen):
    crow = kc_ref.shape[0]
    i = pl.program_id(2)
    hg = NSA_HPG
    d = NSA_D
    r = hg * tq

    @pl.when(i == 0)
    def _():
        kaug_ref[:, :d] = ks_ref[...]
        blk = lax.broadcasted_iota(I32, (seq, LANES), 0) // SEL_BLOCK
        lane = lax.broadcasted_iota(I32, (seq, LANES), 1)
        kaug_ref[:, d:] = (blk == lane).astype(BF16)
        ones = jnp.ones((vst_ref.shape[1] - d, vt), BF16)
        for c in range(seq // vt):
            vst_ref[c, :d, :] = vs_ref[c * vt:(c + 1) * vt, :].astype(F32).T.astype(BF16)
            vwt_ref[c, :d, :] = vw_ref[c * vt:(c + 1) * vt, :].astype(F32).T.astype(BF16)
            vst_ref[c, d:, :] = ones
            vwt_ref[c, d:, :] = ones

    def pv_t(vt_ref, first_tile, p):
        out = None
        for c in range(p.shape[0] // vt):
            part = _dot(vt_ref[first_tile + c], p[c * vt:(c + 1) * vt, :])
            out = part if out is None else out + part
        return out

    q = q_ref[...]
    qh = [(q[:, h * d:(h + 1) * d].astype(F32) * (d ** -0.5)).astype(BF16) for h in range(hg)]
    qa = jnp.concatenate(qh, axis=0)
    t_row = i * tq + lax.broadcasted_iota(I32, (1, tq), 1)
    tpos_row = jnp.concatenate([t_row] * hg, axis=1)

    w0 = pl.multiple_of(jnp.maximum(i * tq + tq - wlen, 0), vt)
    sw = _dot_nt(kw_ref[pl.ds(w0, wlen), :], qa)
    wbias = wbias_ref[jnp.minimum(i, wbias_ref.shape[0] - 1)]
    sw = sw + jnp.concatenate([wbias] * hg, axis=1)
    e_w = jnp.exp(sw - jnp.max(sw, axis=0, keepdims=True))
    pv_w = pv_t(vwt_ref, w0 // vt, e_w.astype(BF16))
    o_win = (pv_w[:d] / pv_w[d:d + 1]).T

    sct = _dot_nt(kc_ref[...], qa)
    cidx = lax.broadcasted_iota(I32, (crow, 1), 0)
    cmask = (cidx * CMP_STRIDE + (CMP_BLOCK - 1) <= tpos_row) & (cidx < ncmp)
    sct = jnp.where(cmask, sct, NEG)
    e_c = jnp.where(cmask, jnp.exp(sct - jnp.max(sct, axis=0, keepdims=True)), 0.0)
    den_c = jnp.sum(e_c, axis=0, keepdims=True)
    p_t = jnp.where(den_c > 0.0, e_c / jnp.where(den_c > 0.0, den_c, 1.0), 0.0)
    o_cmp = _dot_tn(p_t.astype(BF16), vc_ref[...])

    psum_t = p_t[:, 0:tq]
    for h in range(1, hg):
        psum_t = psum_t + p_t[:, h * tq:(h + 1) * tq]
    nsel_blocks = seq // SEL_BLOCK
    nsb = ovt_ref.shape[0]
    imp_t = jnp.dot(ovt_ref[...], psum_t, preferred_element_type=F32,
                    precision=lax.Precision.HIGHEST)
    sidx = lax.broadcasted_iota(I32, (nsb, 1), 0)
    cur = t_row // SEL_BLOCK
    valid = sidx <= cur
    forced = (sidx == 0) | (sidx == cur) | (sidx == cur - 1)
    score = jnp.where(valid, imp_t + jnp.where(forced, FORCE_BONUS, 0.0), -1.0)
    rank = jnp.zeros((nsb, tq), F32)
    for s2 in range(nsel_blocks):
        row = score[s2:s2 + 1, :]
        beats = (row > score) | ((row == score) & (sidx > s2))
        rank = rank + beats.astype(F32)
    sel = valid & (rank < float(min(SEL_COUNT, nsel_blocks)))
    selb_t = jnp.where(sel, 0.0, NEG)
    selb_t = jnp.concatenate([selb_t, jnp.zeros((LANES - nsb, tq), F32)], axis=0)
    selb = selb_t.T.astype(BF16)
    q_aug = jnp.concatenate([jnp.concatenate([qh[h], selb], axis=1) for h in range(hg)], axis=0)

    n_full = (i * tq + 1) // tk
    kl = lax.broadcasted_iota(I32, (tk, 1), 0)

    def tile(j, carry, masked):
        m, acc = carry
        k0 = pl.multiple_of(j * tk, tk)
        s = _dot_nt(kaug_ref[pl.ds(k0, tk), :], q_aug)
        if masked and tq == tk:
            s = s + jnp.concatenate([dbias_ref[...]] * hg, axis=1)
        elif masked:
            s = jnp.where(k0 + kl <= tpos_row, s, NEG)
        m_new = jnp.maximum(m, jnp.max(s, axis=0, keepdims=True))
        p = jnp.exp(s - m_new)
        acc = jnp.exp(m - m_new) * acc + pv_t(vst_ref, j * (tk // vt), p.astype(BF16))
        return m_new, acc

    m0 = jnp.full((1, r), NEG, F32)
    a0 = jnp.zeros((vst_ref.shape[1], r), F32)
    carry = lax.fori_loop(0, n_full, functools.partial(tile, masked=False), (m0, a0))
    _, acc_s = tile(n_full, carry, True)
    o_slc = (acc_s[:d] / acc_s[d:d + 1]).T

    gt = jax.nn.sigmoid(gate_ref[...])
    grp = pl.program_id(1)
    for g in range(1, NSA_KV_GROUPS):
        gt = jnp.where(grp == g, pltpu.roll(gt, LANES - g * 3 * hg, 1), gt)
    for h in range(hg):
        rows = slice(h * tq, (h + 1) * tq)
        o = (gt[:, 3 * h:3 * h + 1] * o_cmp[rows] + gt[:, 3 * h + 1:3 * h + 2] * o_slc[rows]
             + gt[:, 3 * h + 2:3 * h + 3] * o_win[rows])
        o_ref[:, h * d:(h + 1) * d] = o.astype(BF16)


def _nsa(proj, kc, vc, gates_g, bsz, seq):
    n = proj.shape[0]
    g_ = NSA_KV_GROUPS
    tq = NSA_TQ
    tk = min(NSA_TK, seq)
    nq = seq // tq
    ncmp = (seq - CMP_BLOCK) // CMP_STRIDE + 1
    wlen = min(WIN + tq, seq)
    nsel = seq // SEL_BLOCK
    crow = seq // CMP_STRIDE
    assert nsel <= LANES and crow <= LANES and tq <= tk and tk % tq == 0
    nsb = -(-nsel // 8) * 8
    ss = np.arange(nsb)[:, None] * SEL_BLOCK
    cs = np.arange(crow)[None, :] * CMP_STRIDE
    ov = ((cs < ss + SEL_BLOCK) & (cs + CMP_BLOCK > ss)
          & (np.arange(crow)[None, :] < ncmp) & (np.arange(nsb)[:, None] < nsel))
    ov = jnp.asarray(ov.astype(np.float32))
    dq = np.arange(tq)[None, :]
    npat = min(nq, (wlen - tq) // tq + 1)
    kwin = np.arange(wlen)[:, None]
    pats = []
    for p in range(npat):
        delta = (p * tq + dq) - (max(p * tq + tq - wlen, 0) + kwin)
        pats.append(np.where((delta >= 0) & (delta < WIN), 0.0, NEG))
    wbias = jnp.asarray(np.stack(pats).astype(np.float32))
    dbias = jnp.asarray(np.where(np.arange(tk)[:, None] <= dq, 0.0, NEG).astype(np.float32))
    kvspec = lambda c0: pl.BlockSpec((seq, NSA_D), lambda b, g, i: (b, c0 + g))
    vt = NSA_VT
    assert tk % vt == 0 and wlen % vt == 0 and tq % vt == 0 and seq % vt == 0
    kern = functools.partial(_nsa_kernel, tq=tq, tk=tk, vt=vt, seq=seq, ncmp=ncmp, wlen=wlen)
    return pl.pallas_call(
        kern,
        grid=(bsz, g_, nq),
        in_specs=[pl.BlockSpec((tq, NSA_HPG * NSA_D), lambda b, g, i: (b * nq + i, _C_NQ // NSA_HPG + g)),
                  pl.BlockSpec((crow, NSA_D), lambda b, g, i: (b * g_ + g, 0)),
                  pl.BlockSpec((crow, NSA_D), lambda b, g, i: (b * g_ + g, 0)),
                  kvspec(_C_KS), kvspec(_C_VS), kvspec(_C_KW), kvspec(_C_VW),
                  pl.BlockSpec((tq, LANES), lambda b, g, i: (b * nq + i, 0)),
                  pl.BlockSpec((nsb, crow), lambda b, g, i: (0, 0)),
                  pl.BlockSpec((npat, wlen, tq), lambda b, g, i: (0, 0, 0), pipeline_mode=pl.Buffered(1)),
                  pl.BlockSpec((tk, tq), lambda b, g, i: (0, 0), pipeline_mode=pl.Buffered(1))],
        out_specs=pl.BlockSpec((tq, NSA_HPG * NSA_D), lambda b, g, i: (b * nq + i, g)),
        out_shape=jax.ShapeDtypeStruct((n, NSA_HEADS * NSA_D), BF16),
        scratch_shapes=[pltpu.VMEM((seq, 2 * NSA_D), BF16),
                        pltpu.VMEM((seq // vt, NSA_D + 2 * SUBLANES, vt), BF16),
                        pltpu.VMEM((seq // vt, NSA_D + 2 * SUBLANES, vt), BF16)],
        compiler_params=_cp(("arbitrary", "arbitrary", "arbitrary")),
        name="nsa",
    )(proj, kc, vc, proj, proj, proj, proj, gates_g, ov, wbias, dbias)


def _outproj_kernel(oret_ref, onsa_ref, x_ref, mod_ref, g_ref, w_ref, wr_ref,
                    x1_ref, h2_ref, lt_ref):
    hw = oret_ref.shape[1]
    mix = (_dot(oret_ref[...], w_ref[:hw, :].astype(BF16))
           + _dot(onsa_ref[...], w_ref[hw:, :].astype(BF16)))
    x1 = x_ref[...] + mod_ref[0, 2:3, :] * mix
    x1_ref[...] = x1
    y = x1 * lax.rsqrt(jnp.mean(x1 * x1, axis=-1, keepdims=True) + EPS) * g_ref[...]
    h2 = y * (1.0 + mod_ref[0, 4:5, :]) + mod_ref[0, 3:4, :]
    h2_ref[...] = h2
    lt_ref[...] = _dot_nt(wr_ref[...], h2.astype(BF16))


def _outproj(o_ret, o_nsa, x2, mod3, g, w, wr_bf, seq):
    n, d = x2.shape
    tm = min(OUTPROJ_TM, seq)
    hw = o_ret.shape[1]
    return pl.pallas_call(
        _outproj_kernel,
        grid=(n // tm,),
        in_specs=[pl.BlockSpec((tm, hw), lambda i: (i, 0)),
                  pl.BlockSpec((tm, o_nsa.shape[1]), lambda i: (i, 0)),
                  pl.BlockSpec((tm, d), lambda i: (i, 0)),
                  pl.BlockSpec((1, 6, d), lambda i: ((i * tm) // seq, 0, 0)),
                  pl.BlockSpec((1, d), lambda i: (0, 0)),
                  pl.BlockSpec(w.shape, lambda i: (0, 0), pipeline_mode=pl.Buffered(1)),
                  pl.BlockSpec(wr_bf.shape, lambda i: (0, 0))],
        out_specs=[pl.BlockSpec((tm, d), lambda i: (i, 0)),
                   pl.BlockSpec((tm, d), lambda i: (i, 0)),
                   pl.BlockSpec((LANES, tm), lambda i: (0, i))],
        out_shape=[jax.ShapeDtypeStruct((n, d), F32),
                   jax.ShapeDtypeStruct((n, d), F32),
                   jax.ShapeDtypeStruct((LANES, n), F32)],
        compiler_params=_cp(("arbitrary",)),
        name="outproj",
    )(o_ret, o_nsa, x2, mod3, g.reshape(1, d), w, wr_bf)


def _route_kernel(lt_ref, b_ref, tri_ref, ids_ref, wts_ref, cnt_ref, carry_ref, *, sub):
    @pl.when(pl.program_id(0) == 0)
    def _():
        carry_ref[...] = jnp.zeros_like(carry_ref)

    ng, ne = N_GROUPS, EXP_PER_GROUP
    l = lt_ref[...] + b_ref[:, 0:1]
    tc = l.shape[1]
    ridx = lax.broadcasted_iota(I32, (ng, tc), 0).astype(F32)

    def softmax0(v):
        e = jnp.exp(v - jnp.max(v, axis=0, keepdims=True))
        return e / jnp.sum(e, axis=0, keepdims=True)

    def top1(p):
        top = jnp.max(p, axis=0, keepdims=True)
        idx = jnp.min(jnp.where(p == top, ridx, float(ng)), axis=0, keepdims=True)
        return top, idx

    pg_top, grp = top1(softmax0(l[0:ng]))
    leg = jnp.zeros((ne, tc), F32)
    for g in range(ng):
        leg = jnp.where(grp == float(g), l[ng + g * ne:ng + (g + 1) * ne], leg)
    pe = softmax0(leg)
    p1, i1 = top1(pe)
    p2, i2 = top1(jnp.where(ridx == i1, -1.0, pe))
    den = p1 + p2
    w1 = pg_top * p1 / den
    w2 = pg_top * p2 / den
    e1 = grp * float(ne) + i1
    e2 = grp * float(ne) + i2

    eio = lax.broadcasted_iota(I32, (N_EXPERTS, sub), 0).astype(F32)
    r1 = []
    r2 = []
    carry = carry_ref[:, 0:1]
    for c in range(tc // sub):
        cs = slice(c * sub, (c + 1) * sub)
        oh1 = (eio == e1[:, cs]).astype(F32)
        oh2 = (eio == e2[:, cs]).astype(F32)
        oh = oh1 + oh2
        before = carry + _dot(oh.astype(BF16), tri_ref[...])
        r1.append(jnp.sum(oh1 * before, axis=0, keepdims=True))
        r2.append(jnp.sum(oh2 * before, axis=0, keepdims=True))
        carry = carry + jnp.sum(oh, axis=1, keepdims=True)
    carry_ref[...] = jnp.broadcast_to(carry, carry_ref.shape)
    cnt_ref[...] = jnp.broadcast_to(carry, cnt_ref.shape).astype(I32)
    r1 = jnp.concatenate(r1, axis=1)
    r2 = jnp.concatenate(r2, axis=1)
    zf = jnp.zeros((4, tc), F32)
    ids_ref[...] = jnp.concatenate([e1, e2, r1, r2, zf], axis=0).astype(I32)
    wts_ref[...] = jnp.concatenate([w1, w2, jnp.zeros((6, tc), F32)], axis=0)


def _route(lt, bias_col):
    n = lt.shape[1]
    tc = min(ROUTE_TC, n)
    sub = min(ROUTE_SUB, tc)
    tri =jnp.asarray(np.triu(np.ones((sub, sub), np.float32), 1), BF16)
    return pl.pallas_call(
        functools.partial(_route_kernel, sub=sub),
        grid=(n // tc,),
        in_specs=[pl.BlockSpec((LANES, tc), lambda i: (0, i)),
                  pl.BlockSpec((LANES, LANES), lambda i: (0, 0)),
                  pl.BlockSpec((sub, sub), lambda i: (0, 0))],
        out_specs=[pl.BlockSpec((8, tc), lambda i: (0, i)),
                   pl.BlockSpec((8, tc), lambda i: (0, i)),
                   pl.BlockSpec((N_EXPERTS, LANES), lambda i: (0, 0))],
        out_shape=[jax.ShapeDtypeStruct((8, n), I32),
                   jax.ShapeDtypeStruct((8, n), F32),
                   jax.ShapeDtypeStruct((N_EXPERTS, LANES), I32)],
        scratch_shapes=[pltpu.VMEM((N_EXPERTS, LANES), F32)],
        compiler_params=_cp(("arbitrary",)),
        name="route",
    )(lt, bias_col, tri)


def _dest_kernel(ids_ref, ps_ref, o_ref):
    ids = ids_ref[...].astype(F32)
    tc = ids.shape[1]
    eio = lax.broadcasted_iota(I32, (N_EXPERTS, tc), 0).astype(F32)
    ps = ps_ref[:, 0:1]
    rows = [jnp.sum(jnp.where(eio == ids[k:k + 1], ps, 0.0), axis=0, keepdims=True) + ids[2 + k:3 + k]
            for k in range(2)]
    o_ref[...] = jnp.concatenate(rows + [jnp.zeros((6, tc), F32)], axis=0).astype(I32)


def _dest(ids, pad_start):
    n = ids.shape[1]
    tc = min(ROUTE_TC, n)
    ps = jnp.broadcast_to(pad_start.astype(F32)[:, None], (N_EXPERTS, LANES))
    dest = pl.pallas_call(
        _dest_kernel,
        grid=(n // tc,),
        in_specs=[pl.BlockSpec((8, tc), lambda i: (0, i)),
                  pl.BlockSpec((N_EXPERTS, LANES), lambda i: (0, 0))],
        out_specs=pl.BlockSpec((8, tc), lambda i: (0, i)),
        out_shape=jax.ShapeDtypeStruct((8, n), I32),
        compiler_params=_cp(("arbitrary",)),
        name="dest",
    )(ids, ps)
    return dest[:2].reshape(2 * n)


def _dispatch_kernel(dst0_ref, dst1_ref, ps_ref, cnt_ref, h_ref, xs_ref, zbuf, sem, zsem, *, tcd, tm, zr):
    @pl.when(pl.program_id(0) == 0)
    def _():
        zbuf[...] = jnp.zeros_like(zbuf)
        sizes = [zr >> b for b in range(zr.bit_length()) if (zr >> b) >= SUBLANES]

        def zero_rows(row, size):
            return pltpu.make_async_copy(zbuf.at[pl.ds(0, size)],
                                         xs_ref.at[pl.ds(pl.multiple_of(row, SUBLANES), size)], zsem)

        def zero_row(row):
            return pltpu.make_async_copy(zbuf.at[pl.ds(0, 1)], xs_ref.at[pl.ds(row, 1)], zsem)

        def fill(e, wait):
            cnt = cnt_ref[e]
            cnt8 = lax.div(cnt + (SUBLANES - 1), SUBLANES) * SUBLANES
            gap = lax.rem(tm - lax.rem(cnt8, tm), tm)
            base = ps_ref[e]

            def single(c, _):
                cp = zero_row(base + cnt + c)
                cp.wait() if wait else cp.start()
                return 0

            lax.fori_loop(0, cnt8 - cnt, single, 0)
            row = base + cnt8
            for size in sizes:
                has = lax.rem(lax.div(gap, size), 2) == 1

                @pl.when(has)
                def _():
                    cp = zero_rows(row, size)
                    cp.wait() if wait else cp.start()

                row = row + jnp.where(has, size, 0)

        last = N_EXPERTS - 1
        used_rows = ps_ref[last] + lax.div(cnt_ref[last] + (tm - 1), tm) * tm
        n_tail = lax.div(xs_ref.shape[0] - used_rows, zr)

        def tail(c, wait):
            cp = zero_rows(used_rows + c * zr, zr)
            cp.wait() if wait else cp.start()

        for wait in (False, True):
            lax.fori_loop(0, N_EXPERTS, lambda e, _: (fill(e, wait), 0)[1], 0)
            lax.fori_loop(0, n_tail, lambda c, _: (tail(c, wait), 0)[1], 0)

    def issue(t, _):
        for k, dst_ref in enumerate((dst0_ref, dst1_ref)):
            dst = dst_ref[t]
            pltpu.make_async_copy(h_ref.at[pl.ds(t, 1)], xs_ref.at[pl.ds(dst, 1)], sem).start(priority=k)
        return 0

    lax.fori_loop(0, tcd, issue, 0, unroll=8)
    for k in range(2):
        pltpu.make_async_copy(h_ref, xs_ref.at[pl.ds(0, tcd)], sem).wait()


def _dispatch(dest, pad_start, counts, h2, rows, tcd, tm):
    n, d = h2.shape
    assert tm & (tm - 1) == 0 and tm >= 2 * SUBLANES
    zr = tm // 2
    return pl.pallas_call(
        functools.partial(_dispatch_kernel, tcd=tcd, tm=tm, zr=zr),
        grid=(n // tcd,),
        in_specs=[pl.BlockSpec((tcd,), lambda i: (i,), memory_space=pltpu.SMEM),
                  pl.BlockSpec((tcd,), lambda i: (n // tcd + i,), memory_space=pltpu.SMEM),
                  pl.BlockSpec(memory_space=pltpu.SMEM),
                  pl.BlockSpec(memory_space=pltpu.SMEM),
                  pl.BlockSpec((tcd, d), lambda i: (i, 0))],
        out_specs=pl.BlockSpec(memory_space=pl.ANY),
        out_shape=jax.ShapeDtypeStruct((rows, d), h2.dtype),
        scratch_shapes=[pltpu.VMEM((zr, d), h2.dtype), pltpu.SemaphoreType.DMA(()),
                        pltpu.SemaphoreType.DMA(())],
        compiler_params=_cp(("arbitrary",)),
        name="dispatch",
    )(dest, dest, pad_start, counts, h2)


def _experts_kernel(be_ref, ord_ref, nxt_ref, nu_ref, xs_ref, wg_hbm, wu_hbm, wd_hbm, ys_ref,
                    g_f, u_f, d_f, g_s, u_s, d_s, sem):
    i = pl.program_id(0)
    used = i < nu_ref[0]
    e = be_ref[i]
    fresh = (i == 0) | (e != be_ref[jnp.maximum(i - 1, 0)])
    slot = lax.rem(ord_ref[i], 2)

    def fetch(expert, s):
        return [pltpu.make_async_copy(w.at[expert], f.at[s], sem.at[s, k])
                for k, (w, f) in enumerate(((wg_hbm, g_f), (wu_hbm, u_f), (wd_hbm, d_f)))]

    @pl.when(used & (i == 0))
    def _():
        for c in fetch(e, slot):
            c.start(priority=1)

    @pl.when(used & fresh)
    def _():
        for c in fetch(e, slot):
            c.wait()
        nxt = nxt_ref[i]

        @pl.when(nxt >= 0)
        def _():
            for c in fetch(nxt, 1 - slot):
                c.start(priority=1)

        g_s[...] = g_f[slot].astype(BF16)
        u_s[...] = u_f[slot].astype(BF16)
        d_s[...] = d_f[slot].astype(BF16)

    @pl.when(used)
    def _():
        x = xs_ref[...].astype(BF16)
        a = _silu(_dot(x, g_s[...])) * _dot(x, u_s[...])
        ys_ref[...] = _dot(a.astype(BF16), d_s[...])

    @pl.when(jnp.logical_not(used))
    def _():
        ys_ref[...] = jnp.zeros_like(ys_ref)


def _experts(blk_exp, blk_ord, blk_nxt, n_used, xs, w_gate, w_up, w_down, tm):
    d = xs.shape[1]
    de = w_gate.shape[2]
    nb = blk_exp.shape[0]
    row = lambda i, be, od, nx, nu: (jnp.minimum(i, nu[0] - 1), 0)
    anyspec = pl.BlockSpec(memory_space=pl.ANY)
    grid_spec = pltpu.PrefetchScalarGridSpec(
        num_scalar_prefetch=4,
        grid=(nb,),
        in_specs=[pl.BlockSpec((tm, d), row), anyspec, anyspec, anyspec],
        out_specs=pl.BlockSpec((tm, d), lambda i, be, od, nx, nu: (i, 0)),
        scratch_shapes=[pltpu.VMEM((2, d, de), F32), pltpu.VMEM((2, d, de), F32),
                        pltpu.VMEM((2, de, d), F32),
                        pltpu.VMEM((d, de), BF16), pltpu.VMEM((d, de), BF16),
                        pltpu.VMEM((de, d), BF16),
                        pltpu.SemaphoreType.DMA((2, 3))],
    )
    return pl.pallas_call(
        _experts_kernel,
        grid_spec=grid_spec,
        out_shape=jax.ShapeDtypeStruct((nb * tm, d), F32),
        compiler_params=_cp(("arbitrary",)),
        name="experts",
    )(blk_exp, blk_ord, blk_nxt, n_used, xs, w_gate, w_up, w_down)


def _combine_kernel(cur0_ref, cur1_ref, nxt0_ref, nxt1_ref, ys_ref, x1_ref, wt_ref, mod_ref, g_ref,
                    o_ref, buf, sem, *, tc):
    i = pl.program_id(0)
    slot = lax.rem(i, 2)

    def gather(id_refs, s):
        def issue(t, _):
            for k, id_ref in enumerate(id_refs):
                src = id_ref[t]
                pltpu.make_async_copy(ys_ref.at[pl.ds(src, 1)], buf.at[s, k, pl.ds(t, 1)],
                                      sem.at[s]).start(priority=k)
            return 0

        lax.fori_loop(0, tc, issue, 0, unroll=8)

    @pl.when(i == 0)
    def _():
        gather((cur0_ref, cur1_ref), 0)

    @pl.when(i + 1 < pl.num_programs(0))
    def _():
        gather((nxt0_ref, nxt1_ref), 1 - slot)

    for k in range(2):
        pltpu.make_async_copy(ys_ref.at[pl.ds(0, tc)], buf.at[slot, k], sem.at[slot]).wait()

    w_rows = wt_ref[...]
    w_cols = jnp.concatenate([w_rows, jnp.zeros((LANES - w_rows.shape[0], tc), F32)], axis=0).T
    moe = buf[slot, 0] * w_cols[:, 0:1] + buf[slot, 1] * w_cols[:, 1:2]
    x2 = x1_ref[...] + mod_ref[0, 5:6, :] * moe
    o_ref[...] = x2 * lax.rsqrt(jnp.mean(x2 * x2, axis=-1, keepdims=True) + EPS) * g_ref[...]


def _combine(dest, ys, x1, wts, mod3, final_g, seq, tc):
    n, d = x1.shape
    nt = n // tc
    tbl = lambda k, off: pl.BlockSpec((tc,), lambda i: (k * nt + jnp.minimum(i + off, nt - 1),),
                                      memory_space=pltpu.SMEM)
    return pl.pallas_call(
        functools.partial(_combine_kernel, tc=tc),
        grid=(nt,),
        in_specs=[tbl(0, 0), tbl(1, 0), tbl(0, 1), tbl(1, 1),
                  pl.BlockSpec(memory_space=pl.ANY),
                  pl.BlockSpec((tc, d), lambda i: (i, 0)),
                  pl.BlockSpec((8, tc), lambda i: (0, i)),
                  pl.BlockSpec((1, 6, d), lambda i: ((i * tc) // seq, 0, 0)),
                  pl.BlockSpec((1, d), lambda i: (0, 0))],
        out_specs=pl.BlockSpec((tc, d), lambda i: (i, 0)),
        out_shape=jax.ShapeDtypeStruct((n, d), F32),
        scratch_shapes=[pltpu.VMEM((2, 2, tc, d), F32), pltpu.SemaphoreType.DMA((2,))],
        compiler_params=_cp(("arbitrary",)),
        name="combine",
    )(dest, dest, dest, dest, ys, x1, wts, mod3, final_g.reshape(1, d))


def _token_mixer(x2, mod3, norm1_g, w_in, ret_gn_g, cmp_pos_k, cmp_w1_k, cmp_w2_k,
                 cmp_pos_v, cmp_w1_v, cmp_w2_v, bsz, seq):
    n, d = x2.shape
    w_t = w_in.T
    w_gate_t = jnp.pad(w_t[PROJ_MAIN:], ((0, LANES - N_GATE_COLS), (0, 0))).astype(BF16)
    proj, gate_logits = _inproj(x2, mod3, norm1_g, w_t, w_gate_t, seq)

    o_ret = _retention(proj, ret_gn_g, bsz, seq)

    g_ = NSA_KV_GROUPS
    crow = seq // CMP_STRIDE

    pe_flat = lambda pe: pe.reshape(1, CMP_BLOCK * NSA_D)
    kc, vc = _compress(proj, pe_flat(cmp_pos_k), pe_flat(cmp_pos_v),
                       cmp_w1_k, cmp_w1_v, cmp_w2_k, cmp_w2_v, bsz, seq)

    o_nsa = _nsa(proj, kc, vc, gate_logits, bsz, seq)
    return o_ret, o_nsa


def _moe(h2, lt, x1, mod3, final_g, b_grp, b_exp, w_gate, w_up, w_down, seq, tm):
    n, d = h2.shape
    bias_col = jnp.zeros((LANES,), F32).at[:N_GROUPS].set(b_grp).at[N_GROUPS:N_GROUPS + N_EXPERTS].set(b_exp)
    bias_col = jnp.broadcast_to(bias_col[:, None], (LANES, LANES))
    ids, wts, cnt = _route(lt, bias_col)

    counts = cnt[:, 0]
    padded = (counts + tm - 1) // tm * tm
    pad_end = jnp.cumsum(padded)
    pad_start = (pad_end - padded).astype(I32)
    nb = (2 * n) // tm + N_EXPERTS
    n_used = (pad_end[-1] // tm).astype(I32).reshape(1)
    blk_start = jnp.arange(nb, dtype=I32) * tm
    blk_exp = jnp.minimum(jnp.sum((pad_end[None, :] <= blk_start[:, None]).astype(I32), axis=1),
                          N_EXPERTS - 1).astype(I32)
    last_exp = blk_exp[jnp.maximum(n_used[0] - 1, 0)]
    blk_exp = jnp.where(jnp.arange(nb) < n_used[0], blk_exp, last_exp)
    eid = jnp.arange(N_EXPERTS, dtype=I32)
    has = counts > 0
    exp_ord = jnp.sum((has[None, :] & (eid[None, :] < eid[:, None])).astype(I32), axis=1)
    exp_nxt = jnp.min(jnp.where(has[None, :] & (eid[None, :] > eid[:, None]), eid[None, :], N_EXPERTS), axis=1)
    exp_nxt = jnp.where(exp_nxt < N_EXPERTS, exp_nxt, -1).astype(I32)
    blk_ord = exp_ord[blk_exp]
    blk_nxt = exp_nxt[blk_exp]

    dest = _dest(ids, pad_start)
    xs = _dispatch(dest, pad_start, counts.astype(I32), h2, nb * tm, min(DISPATCH_TC, n), tm)
    ys = _experts(blk_exp, blk_ord, blk_nxt, n_used, xs, w_gate, w_up, w_down, tm)
    return _combine(dest, ys, x1, wts, mod3, final_g, seq, min(COMBINE_TC, n))


def kernel(x, c, w_ada, b_ada, norm1_g, norm2_g, final_g, w_in, ret_gn_g, cmp_pos_k, cmp_w1_k,
           cmp_w2_k, cmp_pos_v, cmp_w1_v, cmp_w2_v, w_out, w_grp, b_grp, w_exp, b_exp, w_gate,
           w_up, w_down):
    bsz, seq, d = x.shape
    assert w_ada.shape[0] == 1, "single-layer block"
    n = bsz * seq
    x2 = x.reshape(n, d)
    mod3 = _ada(c, w_ada[0], b_ada[0]).reshape(bsz, 6, d)

    o_ret, o_nsa = _token_mixer(x2, mod3, norm1_g[0], w_in[0], ret_gn_g[0], cmp_pos_k[0],
                                cmp_w1_k[0], cmp_w2_k[0], cmp_pos_v[0], cmp_w1_v[0], cmp_w2_v[0],
                                bsz, seq)

    w_route = jnp.concatenate([w_grp[0], w_exp[0]], axis=1)
    w_route = jnp.pad(w_route, ((0, 0), (0, LANES - w_route.shape[1]))).T.astype(BF16)
    x1, h2, lt = _outproj(o_ret, o_nsa, x2, mod3, norm2_g[0], w_out[0], w_route, seq)

    out = _moe(h2, lt, x1, mod3, final_g, b_grp[0], b_exp[0], w_gate[0], w_up[0], w_down[0],
               seq, MOE_TM)
    return out.reshape(bsz, seq, d)
```

```python
import functools

import numpy as np
import jax
import jax.numpy as jnp
from jax import lax
from jax.experimental import pallas as pl
from jax.experimental.pallas import tpu as pltpu

F32 = jnp.float32
BF16 = jnp.bfloat16
I32 = jnp.int32

RET_HEADS = 4
RET_DK = 256
RET_DV = 256
RET_CHUNK = 128
NSA_HEADS = 8
NSA_KV_GROUPS = 2
NSA_HPG = NSA_HEADS // NSA_KV_GROUPS
NSA_D = 128
CMP_BLOCK = 32
CMP_STRIDE = 16
SEL_BLOCK = 64
SEL_COUNT = 16
WIN = 512
N_GROUPS = 8
EXP_PER_GROUP = 8
N_EXPERTS = N_GROUPS * EXP_PER_GROUP
ROPE_BASE = 10000.0
EPS = 1e-6
NEG = -1e30
FORCE_BONUS = 1e4

LANES = 128
SUBLANES = 8
MXU_DIM = 256
VMEM_LIMIT = 52 * 1024 * 1024

ADA_TN = 1024
INPROJ_TM = 2048
INPROJ_TN = 512
RET_CHUNKS = 4
NSA_TQ = 512
NSA_TK = 512
NSA_VT = MXU_DIM
OUTPROJ_TM = 512
ROUTE_TC = 2048
ROUTE_SUB = 512
MOE_TM = 256
DISPATCH_TC = 1024
COMBINE_TC = 256

_C_NQ = 32
_C_KC, _C_VC, _C_KS, _C_VS, _C_KW, _C_VW = 40, 42, 44, 46, 48, 50
_C_GATE = 52
PROJ_MAIN = _C_GATE * LANES
N_GATE_COLS = NSA_HEADS * 3


def _cp(sem, vmem=VMEM_LIMIT):
    return pltpu.CompilerParams(dimension_semantics=sem, vmem_limit_bytes=vmem)


def _silu(v):
    return v * jax.nn.sigmoid(v)


def _dot(a, b):
    return jnp.dot(a, b, preferred_element_type=F32)


def _dot_nt(a, b):
    return lax.dot_general(a, b, (((1,), (1,)), ((), ())), preferred_element_type=F32)


def _dot_tn(a, b):
    return lax.dot_general(a, b, (((0,), (0,)), ((), ())), preferred_element_type=F32)


def _ada_kernel(c_ref, w_ref, b_ref, o_ref):
    ca = _silu(c_ref[...]).astype(BF16)
    o_ref[...] = _dot(ca, w_ref[...].astype(BF16)) + b_ref[...]


def _ada(c, w, b):
    bsz, d = c.shape
    n = w.shape[1]
    tn = min(ADA_TN, n)
    return pl.pallas_call(
        _ada_kernel,
        grid=(n // tn,),
        in_specs=[pl.BlockSpec((bsz, d), lambda j: (0, 0)),
                  pl.BlockSpec((d, tn), lambda j: (0, j)),
                  pl.BlockSpec((1, tn), lambda j: (0, j))],
        out_specs=pl.BlockSpec((bsz, tn), lambda j: (0, j)),
        out_shape=jax.ShapeDtypeStruct((bsz, n), F32),
        compiler_params=_cp(("arbitrary",)),
        name="ada",
    )(c, w, b.reshape(1, n))


def _inproj_kernel(x_hbm, mod_ref, g_ref, wt_ref, wgt_ref, proj_ref, gate_ref, h_ref, x_buf, x_sem):
    i = pl.program_id(0)
    tm = x_buf.shape[0]

    def x_copy(blk):
        return pltpu.make_async_copy(x_hbm.at[pl.ds(pl.multiple_of(blk * tm, tm), tm)], x_buf, x_sem)

    @pl.when(pl.program_id(1) == 0)
    def _():
        @pl.when(i == 0)
        def _():
            x_copy(i).start()

        x_copy(i).wait()
        x = x_buf[...]
        y = x * lax.rsqrt(jnp.mean(x * x, axis=-1, keepdims=True) + EPS) * g_ref[...]
        h = (y * (1.0 + mod_ref[0, 1:2, :]) + mod_ref[0, 0:1, :]).astype(BF16)
        h_ref[...] = h
        gate_ref[...] = _dot_nt(h, wgt_ref[...])

        @pl.when(i + 1 < pl.num_programs(0))
        def _():
            x_copy(i + 1).start(priority=1)

    proj_ref[...] = _dot_nt(h_ref[...], wt_ref[...].astype(BF16)).astype(BF16)


def _inproj(x2, mod3, g, w_t, wg_t, seq):
    n, d = x2.shape
    tm = min(INPROJ_TM, seq)
    tn = INPROJ_TN
    nj = PROJ_MAIN // tn
    return pl.pallas_call(
        _inproj_kernel,
        grid=(n // tm, nj),
        in_specs=[pl.BlockSpec(memory_space=pl.ANY),
                  pl.BlockSpec((1, 6, d), lambda i, j: ((i * tm) // seq, 0, 0)),
                  pl.BlockSpec((1, d), lambda i, j: (0, 0)),
                  pl.BlockSpec((tn, d), lambda i, j: (j, 0)),
                  pl.BlockSpec((LANES, d), lambda i, j: (0, 0))],
        out_specs=[pl.BlockSpec((tm, tn), lambda i, j: (i, j)),
                   pl.BlockSpec((tm, LANES), lambda i, j: (i, 0))],
        out_shape=[jax.ShapeDtypeStruct((n, PROJ_MAIN), BF16),
                   jax.ShapeDtypeStruct((n, LANES), F32)],
        scratch_shapes=[pltpu.VMEM((tm, d), BF16), pltpu.VMEM((tm, d), F32),
                        pltpu.SemaphoreType.DMA(())],
        compiler_params=_cp(("arbitrary", "arbitrary")),
        name="inproj",
    )(x2, mod3, g.reshape(1, d), w_t, wg_t)


def _retention_kernel(q_ref, k_ref, v_ref, g_ref, cos_ref, sin_ref, din_ref, zeta_ref,
                      qdec_ref, cdec_ref, gn_ref, o_ref, s_ref):
    @pl.when(pl.program_id(1) == 0)
    def _():
        s_ref[...] = jnp.zeros_like(s_ref)

    half = RET_DK // 2
    c = RET_CHUNK

    for sub in range(q_ref.shape[0] // c):
        rows = slice(sub * c, (sub + 1) * c)
        cos = cos_ref[rows, :]
        sin = sin_ref[rows, :]

        def rot(a):
            a1, a2 = a[:, :half], a[:, half:]
            return jnp.concatenate([a1 * cos - a2 * sin, a1 * sin + a2 * cos], axis=1)

        for h in range(RET_HEADS):
            qs = slice(h * RET_DK, (h + 1) * RET_DK)
            vs = slice(h * RET_DV, (h + 1) * RET_DV)
            q = rot(q_ref[rows, qs].astype(F32))
            k = rot(k_ref[rows, qs].astype(F32)) * (RET_DK ** -0.5)
            v = v_ref[rows, vs]
            qb = q.astype(BF16)
            kb = k.astype(BF16)
            s = _dot_nt(qb, kb) * din_ref[h]
            inner = _dot(s.astype(BF16), v)
            s_prev = s_ref[h]
            cross = _dot(qb, s_prev.astype(BF16)) * qdec_ref[h]
            kv = _dot_tn((k * zeta_ref[h]).astype(BF16), v)
            s_ref[h] = cdec_ref[h] * s_prev + kv
            o = inner + cross
            mu = jnp.mean(o, axis=-1, keepdims=True)
            oc = o - mu
            var = jnp.mean(oc * oc, axis=-1, keepdims=True)
            o = oc * lax.rsqrt(var + EPS) * gn_ref[:, vs]
            o_ref[rows, vs] = (o * _silu(g_ref[rows, vs].astype(F32))).astype(BF16)


def _retention(proj, gn_g, bsz, seq):
    n = proj.shape[0]
    c = RET_CHUNK
    nc = seq // c
    hw = RET_HEADS * RET_DK
    half = RET_DK // 2
    pos = jnp.arange(seq, dtype=F32)
    inv = ROPE_BASE ** (-jnp.arange(half, dtype=F32) / half)
    ang = pos[:, None] * inv[None, :]
    cos, sin = jnp.cos(ang), jnp.sin(ang)
    log_gamma = jnp.log1p(-jnp.exp2(-5.0 - jnp.arange(RET_HEADS, dtype=F32)))
    idx = jnp.arange(c, dtype=F32)
    rel = idx[:, None] - idx[None, :]
    decay_in = jnp.where(rel >= 0, jnp.exp(log_gamma[:, None, None] * jnp.maximum(rel, 0.0)), 0.0)
    zeta = jnp.exp(log_gamma[:, None] * (c - 1 - idx)[None, :])[:, :, None]
    q_decay = jnp.exp(log_gamma[:, None] * (idx + 1)[None, :])[:, :, None]
    chunk_decay = jnp.exp(log_gamma * c)[:, None, None]
    rb = min(RET_CHUNKS * c, seq)
    ns = seq // rb
    row = lambda b, t: (b * ns + t)
    return pl.pallas_call(
        _retention_kernel,
        grid=(bsz, ns),
        in_specs=[pl.BlockSpec((rb, hw), lambda b, t: (row(b, t), 0)),
                  pl.BlockSpec((rb, hw), lambda b, t: (row(b, t), 1)),
                  pl.BlockSpec((rb, hw), lambda b, t: (row(b, t), 2)),
                  pl.BlockSpec((rb, hw), lambda b, t: (row(b, t), 3)),
                  pl.BlockSpec((rb, half), lambda b, t: (t, 0)),
                  pl.BlockSpec((rb, half), lambda b, t: (t, 0)),
                  pl.BlockSpec((RET_HEADS, c, c), lambda b, t: (0, 0, 0)),
                  pl.BlockSpec((RET_HEADS, c, 1), lambda b, t: (0, 0, 0)),
                  pl.BlockSpec((RET_HEADS, c, 1), lambda b, t: (0, 0, 0)),
                  pl.BlockSpec((RET_HEADS, 1, 1), lambda b, t: (0, 0, 0)),
                  pl.BlockSpec((1, hw), lambda b, t: (0, 0))],
        out_specs=pl.BlockSpec((rb, hw), lambda b, t: (row(b, t), 0)),
        out_shape=jax.ShapeDtypeStruct((n, hw), BF16),
        scratch_shapes=[pltpu.VMEM((RET_HEADS, RET_DK, RET_DV), F32)],
        compiler_params=_cp(("arbitrary", "arbitrary")),
        name="retention",
    )(proj, proj, proj, proj, cos, sin, decay_in, zeta, q_decay, chunk_decay,
      gn_g.reshape(1, hw))


def _compress_kernel(ak_ref, av_ref, pek_ref, pev_ref, w1k_ref, w1v_ref, w2k_ref, w2v_ref,
                     kc_ref, vc_ref, a_scr):
    seq = ak_ref.shape[0]
    nblk = seq // CMP_STRIDE
    a_scr[seq:, :] = jnp.zeros((a_scr.shape[0] - seq, NSA_D), F32)

    def one(a_ref, pe_ref, w1_ref, w2_ref, o_ref):
        a_scr[:seq, :] = a_ref[...].astype(F32)
        flat = jnp.concatenate([a_scr[pl.ds(l, nblk, stride=CMP_STRIDE), :] for l in range(CMP_BLOCK)],
                               axis=1)
        pre = _dot((flat + pe_ref[...]).astype(BF16), w1_ref[...].astype(BF16))
        o_ref[...] = _dot(_silu(pre).astype(BF16), w2_ref[...].astype(BF16)).astype(BF16)

    one(ak_ref, pek_ref, w1k_ref, w2k_ref, kc_ref)
    one(av_ref, pev_ref, w1v_ref, w2v_ref, vc_ref)


def _compress(proj, pek, pev, w1k, w1v, w2k, w2v, bsz, seq):
    g_ = NSA_KV_GROUPS
    nblk = seq // CMP_STRIDE
    full = lambda a: pl.BlockSpec(a.shape, lambda i: (0,) * a.ndim)
    kblk = pl.BlockSpec((seq, NSA_D), lambda i: (i // g_, _C_KC + i % g_))
    vblk = pl.BlockSpec((seq, NSA_D), lambda i: (i // g_, _C_VC + i % g_))
    oblk = pl.BlockSpec((nblk, NSA_D), lambda i: (i, 0))
    return pl.pallas_call(
        _compress_kernel,
        grid=(bsz * g_,),
        in_specs=[kblk, vblk, full(pek), full(pev), full(w1k), full(w1v), full(w2k), full(w2v)],
        out_specs=[oblk, oblk],
        out_shape=[jax.ShapeDtypeStruct((bsz * g_ * nblk, NSA_D), BF16)] * 2,
        scratch_shapes=[pltpu.VMEM((seq + CMP_BLOCK, NSA_D), F32)],
        compiler_params=_cp(("arbitrary",)),
        name="compress",
    )(proj, proj, pek, pev, w1k, w1v, w2k, w2v)


def _nsa_kernel(q_ref, kc_ref, vc_ref, ks_ref, vs_ref, kw_ref, vw_ref, gate_ref, ovt_ref, wbias_ref, dbias_ref,
                cbias_ref,
                o_ref, kaug_ref, vst_ref, vwt_ref, *, tq, tk, vt, seq, wlen):
    crow = kc_ref.shape[0]
    i = pl.program_id(2)
    hg = NSA_HPG
    d = NSA_D
    r = hg * tq

    @pl.when(i == 0)
    def _():
        kaug_ref[:, :d] = ks_ref[...]
        blk = lax.broadcasted_iota(I32, (seq, LANES), 0) // SEL_BLOCK
        lane = lax.broadcasted_iota(I32, (seq, LANES), 1)
        kaug_ref[:, d:] = (blk == lane).astype(BF16)
        ones = jnp.ones((vst_ref.shape[1] - d, vt), BF16)
        for c in range(seq // vt):
            vst_ref[c, :d, :] = vs_ref[c * vt:(c + 1) * vt, :].astype(F32).T.astype(BF16)
            vwt_ref[c, :d, :] = vw_ref[c * vt:(c + 1) * vt, :].astype(F32).T.astype(BF16)
            vst_ref[c, d:, :] = ones
            vwt_ref[c, d:, :] = ones

    def pv_t(vt_ref, first_tile, p):
        out = None
        for c in range(p.shape[0] // vt):
            part = _dot(vt_ref[first_tile + c], p[c * vt:(c + 1) * vt, :])
            out = part if out is None else out + part
        return out

    q = q_ref[...]
    qh = [(q[:, h * d:(h + 1) * d].astype(F32) * (d ** -0.5)).astype(BF16) for h in range(hg)]
    qa = jnp.concatenate(qh, axis=0)
    t_row = i * tq + lax.broadcasted_iota(I32, (1, tq), 1)
    tpos_row = jnp.concatenate([t_row] * hg, axis=1)

    w0 = pl.multiple_of(jnp.maximum(i * tq + tq - wlen, 0), vt)
    sw = _dot_nt(kw_ref[pl.ds(w0, wlen), :], qa)
    wbias = wbias_ref[jnp.minimum(i, wbias_ref.shape[0] - 1)]
    sw = sw + jnp.concatenate([wbias] * hg, axis=1)
    e_w = jnp.exp(sw - jnp.max(sw, axis=0, keepdims=True))
    pv_w = pv_t(vwt_ref, w0 // vt, e_w.astype(BF16))

    sct = _dot_nt(kc_ref[...], qa)
    sct = sct + jnp.concatenate([cbias_ref[i]] * hg, axis=1)
    e_c = jnp.exp(sct - jnp.max(sct, axis=0, keepdims=True))
    den_c = jnp.sum(e_c, axis=0, keepdims=True)
    seen = tpos_row >= CMP_BLOCK - 1
    p_t = jnp.where(seen, e_c / den_c, 0.0)
    o_cmp = _dot_tn(p_t.astype(BF16), vc_ref[...])

    psum_t = p_t[:, 0:tq]
    for h in range(1, hg):
        psum_t = psum_t + p_t[:, h * tq:(h + 1) * tq]
    nsel_blocks = seq // SEL_BLOCK
    nsb = ovt_ref.shape[0]
    imp_t = jnp.dot(ovt_ref[...], psum_t, preferred_element_type=F32,
                    precision=lax.Precision.HIGHEST)
    sidx = lax.broadcasted_iota(I32, (nsb, 1), 0)
    cur = t_row // SEL_BLOCK
    valid = sidx <= cur
    forced = (sidx == 0) | (sidx == cur) | (sidx == cur - 1)
    score = jnp.where(valid, imp_t + jnp.where(forced, FORCE_BONUS, 0.0), -1.0)
    rank = jnp.zeros((nsb, tq), F32)
    for s2 in range(nsel_blocks):
        row = score[s2:s2 + 1, :]
        beats = (row > score) | ((row == score) & (sidx > s2))
        rank = rank + beats.astype(F32)
    sel = valid & (rank < float(min(SEL_COUNT, nsel_blocks)))
    selb_t = jnp.where(sel, 0.0, NEG)
    selb_t = jnp.concatenate([selb_t, jnp.zeros((LANES - nsb, tq), F32)], axis=0)
    selb = selb_t.T.astype(BF16)
    q_aug = jnp.concatenate([jnp.concatenate([qh[h], selb], axis=1) for h in range(hg)], axis=0)

    n_full = (i * tq + 1) // tk
    kl = lax.broadcasted_iota(I32, (tk, 1), 0)

    def tile(j, carry, masked):
        m, acc = carry
        k0 = pl.multiple_of(j * tk, tk)
        s = _dot_nt(kaug_ref[pl.ds(k0, tk), :], q_aug)
        if masked and tq == tk:
            s = s + jnp.concatenate([dbias_ref[...]] * hg, axis=1)
        elif masked:
            s = jnp.where(k0 + kl <= tpos_row, s, NEG)
        m_new = jnp.maximum(m, jnp.max(s, axis=0, keepdims=True))
        p = jnp.exp(s - m_new)
        acc = jnp.exp(m - m_new) * acc + pv_t(vst_ref, j * (tk // vt), p.astype(BF16))
        return m_new, acc

    m0 = jnp.full((1, r), NEG, F32)
    a0 = jnp.zeros((vst_ref.shape[1], r), F32)
    carry = lax.fori_loop(0, n_full, functools.partial(tile, masked=False), (m0, a0))
    _, acc_s = tile(n_full, carry, True)

    gt = jax.nn.sigmoid(gate_ref[...])
    grp = pl.program_id(1)
    for g in range(1, NSA_KV_GROUPS):
        gt = jnp.where(grp == g, pltpu.roll(gt, LANES - g * 3 * hg, 1), gt)
    gt_t = gt.T
    gate_row = lambda j: jnp.concatenate([gt_t[3 * h + j:3 * h + j + 1, :] for h in range(hg)], axis=1)
    o_sw = (acc_s[:d] * (gate_row(1) / acc_s[d:d + 1]) + pv_w[:d] * (gate_row(2) / pv_w[d:d + 1])).T
    for h in range(hg):
        rows = slice(h * tq, (h + 1) * tq)
        o = gt[:, 3 * h:3 * h + 1] * o_cmp[rows] + o_sw[rows]
        o_ref[:, h * d:(h + 1) * d] = o.astype(BF16)


def _nsa(proj, kc, vc, gates_g, bsz, seq):
    n = proj.shape[0]
    g_ = NSA_KV_GROUPS
    tq = NSA_TQ
    tk = min(NSA_TK, seq)
    nq = seq // tq
    ncmp = (seq - CMP_BLOCK) // CMP_STRIDE + 1
    wlen = min(WIN + tq, seq)
    nsel = seq // SEL_BLOCK
    crow = seq // CMP_STRIDE
    assert nsel <= LANES and crow <= LANES and tq <= tk and tk % tq == 0
    nsb = -(-nsel // 8) * 8
    ss = np.arange(nsb)[:, None] * SEL_BLOCK
    cs = np.arange(crow)[None, :] * CMP_STRIDE
    ov = ((cs < ss + SEL_BLOCK) & (cs + CMP_BLOCK > ss)
          & (np.arange(crow)[None, :] < ncmp) & (np.arange(nsb)[:, None] < nsel))
    ov = jnp.asarray(ov.astype(np.float32))
    dq = np.arange(tq)[None, :]
    npat = min(nq, (wlen - tq) // tq + 1)
    kwin = np.arange(wlen)[:, None]
    pats = []
    for p in range(npat):
        delta = (p * tq + dq) - (max(p * tq + tq - wlen, 0) + kwin)
        pats.append(np.where((delta >= 0) & (delta < WIN), 0.0, NEG))
    wbias = jnp.asarray(np.stack(pats).astype(np.float32))
    dbias = jnp.asarray(np.where(np.arange(tk)[:, None] <= dq, 0.0, NEG).astype(np.float32))
    cend = (np.arange(crow) * CMP_STRIDE + CMP_BLOCK - 1)[None, :, None]
    tabs = (np.arange(nq)[:, None, None] * tq + dq[None])
    cbias = jnp.asarray(np.where((cend <= tabs) & (np.arange(crow)[None, :, None] < ncmp), 0.0, NEG)
                        .astype(np.float32))
    kvspec = lambda c0: pl.BlockSpec((seq, NSA_D), lambda b, g, i: (b, c0 + g))
    vt = NSA_VT
    assert tk % vt == 0 and wlen % vt == 0 and tq % vt == 0 and seq % vt == 0
    kern = functools.partial(_nsa_kernel, tq=tq, tk=tk, vt=vt, seq=seq, wlen=wlen)
    return pl.pallas_call(
        kern,
        grid=(bsz, g_, nq),
        in_specs=[pl.BlockSpec((tq, NSA_HPG * NSA_D), lambda b, g, i: (b * nq + i, _C_NQ // NSA_HPG + g)),
                  pl.BlockSpec((crow, NSA_D), lambda b, g, i: (b * g_ + g, 0)),
                  pl.BlockSpec((crow, NSA_D), lambda b, g, i: (b * g_ + g, 0)),
                  kvspec(_C_KS), kvspec(_C_VS), kvspec(_C_KW), kvspec(_C_VW),
                  pl.BlockSpec((tq, LANES), lambda b, g, i: (b * nq + i, 0)),
                  pl.BlockSpec((nsb, crow), lambda b, g, i: (0, 0)),
                  pl.BlockSpec((npat, wlen, tq), lambda b, g, i: (0, 0, 0), pipeline_mode=pl.Buffered(1)),
                  pl.BlockSpec((tk, tq), lambda b, g, i: (0, 0), pipeline_mode=pl.Buffered(1)),
                  pl.BlockSpec((nq, crow, tq), lambda b, g, i: (0, 0, 0), pipeline_mode=pl.Buffered(1))],
        out_specs=pl.BlockSpec((tq, NSA_HPG * NSA_D), lambda b, g, i: (b * nq + i, g)),
        out_shape=jax.ShapeDtypeStruct((n, NSA_HEADS * NSA_D), BF16),
        scratch_shapes=[pltpu.VMEM((seq, 2 * NSA_D), BF16),
                        pltpu.VMEM((seq // vt, NSA_D + 2 * SUBLANES, vt), BF16),
                        pltpu.VMEM((seq // vt, NSA_D + 2 * SUBLANES, vt), BF16)],
        compiler_params=_cp(("arbitrary", "arbitrary", "arbitrary")),
        name="nsa",
    )(proj, kc, vc, proj, proj, proj, proj, gates_g, ov, wbias, dbias, cbias)


def _outproj_kernel(oret_ref, onsa_ref, x_ref, mod_ref, g_ref, w_ref, wr_ref,
                    x1_ref, h2_ref, lt_ref):
    hw = oret_ref.shape[1]
    mix = (_dot(oret_ref[...], w_ref[:hw, :].astype(BF16))
           + _dot(onsa_ref[...], w_ref[hw:, :].astype(BF16)))
    x1 = x_ref[...] + mod_ref[0, 2:3, :] * mix
    x1_ref[...] = x1
    y = x1 * lax.rsqrt(jnp.mean(x1 * x1, axis=-1, keepdims=True) + EPS) * g_ref[...]
    h2 = y * (1.0 + mod_ref[0, 4:5, :]) + mod_ref[0, 3:4, :]
    h2_ref[...] = h2
    lt_ref[...] = _dot_nt(wr_ref[...], h2.astype(BF16))


def _outproj(o_ret, o_nsa, x2, mod3, g, w, wr_bf, seq):
    n, d = x2.shape
    tm = min(OUTPROJ_TM, seq)
    hw = o_ret.shape[1]
    return pl.pallas_call(
        _outproj_kernel,
        grid=(n // tm,),
        in_specs=[pl.BlockSpec((tm, hw), lambda i: (i, 0)),
                  pl.BlockSpec((tm, o_nsa.shape[1]), lambda i: (i, 0)),
                  pl.BlockSpec((tm, d), lambda i: (i, 0)),
                  pl.BlockSpec((1, 6, d), lambda i: ((i * tm) // seq, 0, 0)),
                  pl.BlockSpec((1, d), lambda i: (0, 0)),
                  pl.BlockSpec(w.shape, lambda i: (0, 0), pipeline_mode=pl.Buffered(1)),
                  pl.BlockSpec(wr_bf.shape, lambda i: (0, 0))],
        out_specs=[pl.BlockSpec((tm, d), lambda i: (i, 0)),
                   pl.BlockSpec((tm, d), lambda i: (i, 0)),
                   pl.BlockSpec((LANES, tm), lambda i: (0, i))],
        out_shape=[jax.ShapeDtypeStruct((n, d), F32),
                   jax.ShapeDtypeStruct((n, d), F32),
                   jax.ShapeDtypeStruct((LANES, n), F32)],
        compiler_params=_cp(("arbitrary",)),
        name="outproj",
    )(o_ret, o_nsa, x2, mod3, g.reshape(1, d), w, wr_bf)


def _route_kernel(lt_ref, b_ref, tri_ref, ids_ref, wts_ref, cnt_ref, carry_ref, *, sub):
    @pl.when(pl.program_id(0) == 0)
    def _():
        carry_ref[...] = jnp.zeros_like(carry_ref)

    ng, ne = N_GROUPS, EXP_PER_GROUP
    l = lt_ref[...] + b_ref[:, 0:1]
    tc = l.shape[1]
    ridx = lax.broadcasted_iota(I32, (ng, tc), 0).astype(F32)

    def softmax0(v):
        e = jnp.exp(v - jnp.max(v, axis=0, keepdims=True))
        return e / jnp.sum(e, axis=0, keepdims=True)

    def top1(p):
        top = jnp.max(p, axis=0, keepdims=True)
        idx = jnp.min(jnp.where(p == top, ridx, float(ng)), axis=0, keepdims=True)
        return top, idx

    pg_top, grp = top1(softmax0(l[0:ng]))
    leg = jnp.zeros((ne, tc), F32)
    for g in range(ng):
        leg = jnp.where(grp == float(g), l[ng + g * ne:ng + (g + 1) * ne], leg)
    pe = softmax0(leg)
    p1, i1 = top1(pe)
    p2, i2 = top1(jnp.where(ridx == i1, -1.0, pe))
    den = p1 + p2
    w1 = pg_top * p1 / den
    w2 = pg_top * p2 / den
    e1 = grp * float(ne) + i1
    e2 = grp * float(ne) + i2

    eio = lax.broadcasted_iota(I32, (N_EXPERTS, sub), 0).astype(F32)
    r1 = []
    r2 = []
    carry = carry_ref[:, 0:1]
    for c in range(tc // sub):
        cs = slice(c * sub, (c + 1) * sub)
        oh1 = (eio == e1[:, cs]).astype(F32)
        oh2 = (eio == e2[:, cs]).astype(F32)
        oh = oh1 + oh2
        before = carry + _dot(oh.astype(BF16), tri_ref[...])
        r1.append(jnp.sum(oh1 * before, axis=0, keepdims=True))
        r2.append(jnp.sum(oh2 * before, axis=0, keepdims=True))
        carry = carry + jnp.sum(oh, axis=1, keepdims=True)
    carry_ref[...] = jnp.broadcast_to(carry, carry_ref.shape)
    cnt_ref[...] = jnp.broadcast_to(carry, cnt_ref.shape).astype(I32)
    r1 = jnp.concatenate(r1, axis=1)
    r2 = jnp.concatenate(r2, axis=1)
    zf = jnp.zeros((4, tc), F32)
    ids_ref[...] = jnp.concatenate([e1, e2, r1, r2, zf], axis=0).astype(I32)
    wts_ref[...] = jnp.concatenate([w1, w2, jnp.zeros((6, tc), F32)], axis=0)


def _route(lt, bias_col):
    n = lt.shape[1]
    tc = min(ROUTE_TC, n)
    sub = min(ROUTE_SUB, tc)
    tri =jnp.asarray(np.triu(np.ones((sub, sub), np.float32), 1), BF16)
    return pl.pallas_call(
        functools.partial(_route_kernel, sub=sub),
        grid=(n // tc,),
        in_specs=[pl.BlockSpec((LANES, tc), lambda i: (0, i)),
                  pl.BlockSpec((LANES, LANES), lambda i: (0, 0)),
                  pl.BlockSpec((sub, sub), lambda i: (0, 0))],
        out_specs=[pl.BlockSpec((8, tc), lambda i: (0, i)),
                   pl.BlockSpec((8, tc), lambda i: (0, i)),
                   pl.BlockSpec((N_EXPERTS, LANES), lambda i: (0, 0))],
        out_shape=[jax.ShapeDtypeStruct((8, n), I32),
                   jax.ShapeDtypeStruct((8, n), F32),
                   jax.ShapeDtypeStruct((N_EXPERTS, LANES), I32)],
        scratch_shapes=[pltpu.VMEM((N_EXPERTS, LANES), F32)],
        compiler_params=_cp(("arbitrary",)),
        name="route",
    )(lt, bias_col, tri)


def _dest_kernel(ids_ref, ps_ref, o_ref):
    ids = ids_ref[...].astype(F32)
    tc = ids.shape[1]
    eio = lax.broadcasted_iota(I32, (N_EXPERTS, tc), 0).astype(F32)
    ps = ps_ref[:, 0:1]
    rows = [jnp.sum(jnp.where(eio == ids[k:k + 1], ps, 0.0), axis=0, keepdims=True) + ids[2 + k:3 + k]
            for k in range(2)]
    o_ref[...] = jnp.concatenate(rows + [jnp.zeros((6, tc), F32)], axis=0).astype(I32)


def _dest(ids, pad_start):
    n = ids.shape[1]
    tc = min(ROUTE_TC, n)
    ps = jnp.broadcast_to(pad_start.astype(F32)[:, None], (N_EXPERTS, LANES))
    dest = pl.pallas_call(
        _dest_kernel,
        grid=(n // tc,),
        in_specs=[pl.BlockSpec((8, tc), lambda i: (0, i)),
                  pl.BlockSpec((N_EXPERTS, LANES), lambda i: (0, 0))],
        out_specs=pl.BlockSpec((8, tc), lambda i: (0, i)),
        out_shape=jax.ShapeDtypeStruct((8, n), I32),
        compiler_params=_cp(("arbitrary",)),
        name="dest",
    )(ids, ps)
    return dest[:2].reshape(2 * n)


def _dispatch_kernel(dst0_ref, dst1_ref, ps_ref, cnt_ref, h_ref, xs_ref, zbuf, sem, zsem, *, tcd, tm, zr):
    @pl.when(pl.program_id(0) == 0)
    def _():
        zbuf[...] = jnp.zeros_like(zbuf)
        sizes = [zr >> b for b in range(zr.bit_length()) if (zr >> b) >= SUBLANES]

        def zero_rows(row, size):
            return pltpu.make_async_copy(zbuf.at[pl.ds(0, size)],
                                         xs_ref.at[pl.ds(pl.multiple_of(row, SUBLANES), size)], zsem)

        def zero_row(row):
            return pltpu.make_async_copy(zbuf.at[pl.ds(0, 1)], xs_ref.at[pl.ds(row, 1)], zsem)

        def fill(e, wait):
            cnt = cnt_ref[e]
            cnt8 = lax.div(cnt + (SUBLANES - 1), SUBLANES) * SUBLANES
            gap = lax.rem(tm - lax.rem(cnt8, tm), tm)
            base = ps_ref[e]

            def single(c, _):
                cp = zero_row(base + cnt + c)
                cp.wait() if wait else cp.start()
                return 0

            lax.fori_loop(0, cnt8 - cnt, single, 0)
            row = base + cnt8
            for size in sizes:
                has = lax.rem(lax.div(gap, size), 2) == 1

                @pl.when(has)
                def _():
                    cp = zero_rows(row, size)
                    cp.wait() if wait else cp.start()

                row = row + jnp.where(has, size, 0)

        last = N_EXPERTS - 1
        used_rows = ps_ref[last] + lax.div(cnt_ref[last] + (tm - 1), tm) * tm
        n_tail = lax.div(xs_ref.shape[0] - used_rows, zr)

        def tail(c, wait):
            cp = zero_rows(used_rows + c * zr, zr)
            cp.wait() if wait else cp.start()

        for wait in (False, True):
            lax.fori_loop(0, N_EXPERTS, lambda e, _: (fill(e, wait), 0)[1], 0)
            lax.fori_loop(0, n_tail, lambda c, _: (tail(c, wait), 0)[1], 0)

    def issue(t, _):
        for k, dst_ref in enumerate((dst0_ref, dst1_ref)):
            dst = dst_ref[t]
            pltpu.make_async_copy(h_ref.at[pl.ds(t, 1)], xs_ref.at[pl.ds(dst, 1)], sem).start(priority=k)
        return 0

    lax.fori_loop(0, tcd, issue, 0, unroll=8)
    for k in range(2):
        pltpu.make_async_copy(h_ref, xs_ref.at[pl.ds(0, tcd)], sem).wait()


def _dispatch(dest, pad_start, counts, h2, rows, tcd, tm):
    n, d = h2.shape
    assert tm & (tm - 1) == 0 and tm >= 2 * SUBLANES
    zr = tm // 2
    return pl.pallas_call(
        functools.partial(_dispatch_kernel, tcd=tcd, tm=tm, zr=zr),
        grid=(n // tcd,),
        in_specs=[pl.BlockSpec((tcd,), lambda i: (i,), memory_space=pltpu.SMEM),
                  pl.BlockSpec((tcd,), lambda i: (n // tcd + i,), memory_space=pltpu.SMEM),
                  pl.BlockSpec(memory_space=pltpu.SMEM),
                  pl.BlockSpec(memory_space=pltpu.SMEM),
                  pl.BlockSpec((tcd, d), lambda i: (i, 0))],
        out_specs=pl.BlockSpec(memory_space=pl.ANY),
        out_shape=jax.ShapeDtypeStruct((rows, d), h2.dtype),
        scratch_shapes=[pltpu.VMEM((zr, d), h2.dtype), pltpu.SemaphoreType.DMA(()),
                        pltpu.SemaphoreType.DMA(())],
        compiler_params=_cp(("arbitrary",)),
        name="dispatch",
    )(dest, dest, pad_start, counts, h2)


def _experts_kernel(be_ref, ord_ref, nxt_ref, nu_ref, xs_ref, wg_hbm, wu_hbm, wd_hbm, ys_ref,
                    g_f, u_f, d_f, g_s, u_s, d_s, sem):
    i = pl.program_id(0)
    used = i < nu_ref[0]
    e = be_ref[i]
    fresh = (i == 0) | (e != be_ref[jnp.maximum(i - 1, 0)])
    slot = lax.rem(ord_ref[i], 2)

    def fetch(expert, s):
        return [pltpu.make_async_copy(w.at[expert], f.at[s], sem.at[s, k])
                for k, (w, f) in enumerate(((wg_hbm, g_f), (wu_hbm, u_f), (wd_hbm, d_f)))]

    @pl.when(used & (i == 0))
    def _():
        for c in fetch(e, slot):
            c.start(priority=1)

    @pl.when(used & fresh)
    def _():
        for c in fetch(e, slot):
            c.wait()
        nxt = nxt_ref[i]

        @pl.when(nxt >= 0)
        def _():
            for c in fetch(nxt, 1 - slot):
                c.start(priority=1)

        g_s[...] = g_f[slot].astype(BF16)
        u_s[...] = u_f[slot].astype(BF16)
        d_s[...] = d_f[slot].astype(BF16)

    @pl.when(used)
    def _():
        x = xs_ref[...].astype(BF16)
        a = _silu(_dot(x, g_s[...])) * _dot(x, u_s[...])
        ys_ref[...] = _dot(a.astype(BF16), d_s[...])

    @pl.when(jnp.logical_not(used))
    def _():
        ys_ref[...] = jnp.zeros_like(ys_ref)


def _experts(blk_exp, blk_ord, blk_nxt, n_used, xs, w_gate, w_up, w_down, tm):
    d = xs.shape[1]
    de = w_gate.shape[2]
    nb = blk_exp.shape[0]
    row = lambda i, be, od, nx, nu: (jnp.minimum(i, nu[0] - 1), 0)
    anyspec = pl.BlockSpec(memory_space=pl.ANY)
    grid_spec = pltpu.PrefetchScalarGridSpec(
        num_scalar_prefetch=4,
        grid=(nb,),
        in_specs=[pl.BlockSpec((tm, d), row), anyspec, anyspec, anyspec],
        out_specs=pl.BlockSpec((tm, d), lambda i, be, od, nx, nu: (i, 0)),
        scratch_shapes=[pltpu.VMEM((2, d, de), F32), pltpu.VMEM((2, d, de), F32),
                        pltpu.VMEM((2, de, d), F32),
                        pltpu.VMEM((d, de), BF16), pltpu.VMEM((d, de), BF16),
                        pltpu.VMEM((de, d), BF16),
                        pltpu.SemaphoreType.DMA((2, 3))],
    )
    return pl.pallas_call(
        _experts_kernel,
        grid_spec=grid_spec,
        out_shape=jax.ShapeDtypeStruct((nb * tm, d), F32),
        compiler_params=_cp(("arbitrary",)),
        name="experts",
    )(blk_exp, blk_ord, blk_nxt, n_used, xs, w_gate, w_up, w_down)


def _combine_kernel(cur0_ref, cur1_ref, nxt0_ref, nxt1_ref, ys_ref, x1_ref, wt_ref, mod_ref, g_ref,
                    o_ref, buf, sem, *, tc):
    i = pl.program_id(0)
    slot = lax.rem(i, 2)

    def gather(id_refs, s):
        def issue(t, _):
            for k, id_ref in enumerate(id_refs):
                src = id_ref[t]
                pltpu.make_async_copy(ys_ref.at[pl.ds(src, 1)], buf.at[s, k, pl.ds(t, 1)],
                                      sem.at[s]).start(priority=k)
            return 0

        lax.fori_loop(0, tc, issue, 0, unroll=8)

    @pl.when(i == 0)
    def _():
        gather((cur0_ref, cur1_ref), 0)

    @pl.when(i + 1 < pl.num_programs(0))
    def _():
        gather((nxt0_ref, nxt1_ref), 1 - slot)

    for k in range(2):
        pltpu.make_async_copy(ys_ref.at[pl.ds(0, tc)], buf.at[slot, k], sem.at[slot]).wait()

    w_rows = wt_ref[...]
    w_cols = jnp.concatenate([w_rows, jnp.zeros((LANES - w_rows.shape[0], tc), F32)], axis=0).T
    moe = buf[slot, 0] * w_cols[:, 0:1] + buf[slot, 1] * w_cols[:, 1:2]
    x2 = x1_ref[...] + mod_ref[0, 5:6, :] * moe
    o_ref[...] = x2 * lax.rsqrt(jnp.mean(x2 * x2, axis=-1, keepdims=True) + EPS) * g_ref[...]


def _combine(dest, ys, x1, wts, mod3, final_g, seq, tc):
    n, d = x1.shape
    nt = n // tc
    tbl = lambda k, off: pl.BlockSpec((tc,), lambda i: (k * nt + jnp.minimum(i + off, nt - 1),),
                                      memory_space=pltpu.SMEM)
    return pl.pallas_call(
        functools.partial(_combine_kernel, tc=tc),
        grid=(nt,),
        in_specs=[tbl(0, 0), tbl(1, 0), tbl(0, 1), tbl(1, 1),
                  pl.BlockSpec(memory_space=pl.ANY),
                  pl.BlockSpec((tc, d), lambda i: (i, 0)),
                  pl.BlockSpec((8, tc), lambda i: (0, i)),
                  pl.BlockSpec((1, 6, d), lambda i: ((i * tc) // seq, 0, 0)),
                  pl.BlockSpec((1, d), lambda i: (0, 0))],
        out_specs=pl.BlockSpec((tc, d), lambda i: (i, 0)),
        out_shape=jax.ShapeDtypeStruct((n, d), F32),
        scratch_shapes=[pltpu.VMEM((2, 2, tc, d), F32), pltpu.SemaphoreType.DMA((2,))],
        compiler_params=_cp(("arbitrary",)),
        name="combine",
    )(dest, dest, dest, dest, ys, x1, wts, mod3, final_g.reshape(1, d))


def _token_mixer(x2, mod3, norm1_g, w_in, ret_gn_g, cmp_pos_k, cmp_w1_k, cmp_w2_k,
                 cmp_pos_v, cmp_w1_v, cmp_w2_v, bsz, seq):
    n, d = x2.shape
    w_t = w_in.T
    w_gate_t = jnp.pad(w_t[PROJ_MAIN:], ((0, LANES - N_GATE_COLS), (0, 0))).astype(BF16)
    proj, gate_logits = _inproj(x2, mod3, norm1_g, w_t, w_gate_t, seq)

    o_ret = _retention(proj, ret_gn_g, bsz, seq)

    g_ = NSA_KV_GROUPS
    crow = seq // CMP_STRIDE

    pe_flat = lambda pe: pe.reshape(1, CMP_BLOCK * NSA_D)
    kc, vc = _compress(proj, pe_flat(cmp_pos_k), pe_flat(cmp_pos_v),
                       cmp_w1_k, cmp_w1_v, cmp_w2_k, cmp_w2_v, bsz, seq)

    o_nsa = _nsa(proj, kc, vc, gate_logits, bsz, seq)
    return o_ret, o_nsa


def _moe(h2, lt, x1, mod3, final_g, b_grp, b_exp, w_gate, w_up, w_down, seq, tm):
    n, d = h2.shape
    bias_col = jnp.zeros((LANES,), F32).at[:N_GROUPS].set(b_grp).at[N_GROUPS:N_GROUPS + N_EXPERTS].set(b_exp)
    bias_col = jnp.broadcast_to(bias_col[:, None], (LANES, LANES))
    ids, wts, cnt = _route(lt, bias_col)

    counts = cnt[:, 0]
    padded = (counts + tm - 1) // tm * tm
    pad_end = jnp.cumsum(padded)
    pad_start = (pad_end - padded).astype(I32)
    nb = (2 * n) // tm + N_EXPERTS
    n_used = (pad_end[-1] // tm).astype(I32).reshape(1)
    blk_start = jnp.arange(nb, dtype=I32) * tm
    blk_exp = jnp.minimum(jnp.sum((pad_end[None, :] <= blk_start[:, None]).astype(I32), axis=1),
                          N_EXPERTS - 1).astype(I32)
    last_exp = blk_exp[jnp.maximum(n_used[0] - 1, 0)]
    blk_exp = jnp.where(jnp.arange(nb) < n_used[0], blk_exp, last_exp)
    eid = jnp.arange(N_EXPERTS, dtype=I32)
    has = counts > 0
    exp_ord = jnp.sum((has[None, :] & (eid[None, :] < eid[:, None])).astype(I32), axis=1)
    exp_nxt = jnp.min(jnp.where(has[None, :] & (eid[None, :] > eid[:, None]), eid[None, :], N_EXPERTS), axis=1)
    exp_nxt = jnp.where(exp_nxt < N_EXPERTS, exp_nxt, -1).astype(I32)
    blk_ord = exp_ord[blk_exp]
    blk_nxt = exp_nxt[blk_exp]

    dest = _dest(ids, pad_start)
    xs = _dispatch(dest, pad_start, counts.astype(I32), h2, nb * tm, min(DISPATCH_TC, n), tm)
    ys = _experts(blk_exp, blk_ord, blk_nxt, n_used, xs, w_gate, w_up, w_down, tm)
    return _combine(dest, ys, x1, wts, mod3, final_g, seq, min(COMBINE_TC, n))


def kernel(x, c, w_ada, b_ada, norm1_g, norm2_g, final_g, w_in, ret_gn_g, cmp_pos_k, cmp_w1_k,
           cmp_w2_k, cmp_pos_v, cmp_w1_v, cmp_w2_v, w_out, w_grp, b_grp, w_exp, b_exp, w_gate,
           w_up, w_down):
    bsz, seq, d = x.shape
    assert w_ada.shape[0] == 1, "single-layer block"
    n = bsz * seq
    x2 = x.reshape(n, d)
    mod3 = _ada(c, w_ada[0], b_ada[0]).reshape(bsz, 6, d)

    o_ret, o_nsa = _token_mixer(x2, mod3, norm1_g[0], w_in[0], ret_gn_g[0], cmp_pos_k[0],
                                cmp_w1_k[0], cmp_w2_k[0], cmp_pos_v[0], cmp_w1_v[0], cmp_w2_v[0],
                                bsz, seq)

    w_route = jnp.concatenate([w_grp[0], w_exp[0]], axis=1)
    w_route = jnp.pad(w_route, ((0, 0), (0, LANES - w_route.shape[1]))).T.astype(BF16)
    x1, h2, lt = _outproj(o_ret, o_nsa, x2, mod3, norm2_g[0], w_out[0], w_route, seq)

    out = _moe(h2, lt, x1, mod3, final_g, b_grp[0], b_exp[0], w_gate[0], w_up[0], w_down[0],
               seq, MOE_TM)
    return out.reshape(bsz, seq, d)
```

```python
import functools

import numpy as np
import jax
import jax.numpy as jnp
from jax import lax
from jax.experimental import pallas as pl
from jax.experimental.pallas import tpu as pltpu

F32 = jnp.float32
BF16 = jnp.bfloat16
I32 = jnp.int32

RET_HEADS = 4
RET_DK = 256
RET_DV = 256
RET_CHUNK = 128
NSA_HEADS = 8
NSA_KV_GROUPS = 2
NSA_HPG = NSA_HEADS // NSA_KV_GROUPS
NSA_D = 128
CMP_BLOCK = 32
CMP_STRIDE = 16
SEL_BLOCK = 64
SEL_COUNT = 16
WIN = 512
N_GROUPS = 8
EXP_PER_GROUP = 8
N_EXPERTS = N_GROUPS * EXP_PER_GROUP
ROPE_BASE = 10000.0
EPS = 1e-6
NEG = -1e30
FORCE_BONUS = 1e4

LANES = 128
SUBLANES = 8
MXU_DIM = 256
VMEM_LIMIT = 52 * 1024 * 1024

ADA_TN = 1024
INPROJ_TM = 2048
INPROJ_TN = 512
RET_CHUNKS = 4
NSA_TQ = 512
NSA_TK = 512
NSA_VT = MXU_DIM
OUTPROJ_TM = 512
ROUTE_TC = 2048
ROUTE_SUB = 512
MOE_TM = 256
DISPATCH_TC = 1024
COMBINE_TC = 256

_C_NQ = 32
_C_KC, _C_VC, _C_KS, _C_VS, _C_KW, _C_VW = 40, 42, 44, 46, 48, 50
_C_GATE = 52
PROJ_MAIN = _C_GATE * LANES
N_GATE_COLS = NSA_HEADS * 3


def _cp(sem, vmem=VMEM_LIMIT):
    return pltpu.CompilerParams(dimension_semantics=sem, vmem_limit_bytes=vmem)


def _silu(v):
    return v * jax.nn.sigmoid(v)


def _dot(a, b):
    return jnp.dot(a, b, preferred_element_type=F32)


def _dot_nt(a, b):
    return lax.dot_general(a, b, (((1,), (1,)), ((), ())), preferred_element_type=F32)


def _dot_tn(a, b):
    return lax.dot_general(a, b, (((0,), (0,)), ((), ())), preferred_element_type=F32)


def _ada_kernel(c_ref, w_ref, b_ref, o_ref):
    ca = _silu(c_ref[...]).astype(BF16)
    o_ref[...] = _dot(ca, w_ref[...].astype(BF16)) + b_ref[...]


def _ada(c, w, b):
    bsz, d = c.shape
    n = w.shape[1]
    tn = min(ADA_TN, n)
    return pl.pallas_call(
        _ada_kernel,
        grid=(n // tn,),
        in_specs=[pl.BlockSpec((bsz, d), lambda j: (0, 0)),
                  pl.BlockSpec((d, tn), lambda j: (0, j)),
                  pl.BlockSpec((1, tn), lambda j: (0, j))],
        out_specs=pl.BlockSpec((bsz, tn), lambda j: (0, j)),
        out_shape=jax.ShapeDtypeStruct((bsz, n), F32),
        compiler_params=_cp(("arbitrary",)),
        name="ada",
    )(c, w, b.reshape(1, n))


def _inproj_kernel(x_hbm, mod_ref, g_ref, wt_ref, wgt_ref, proj_ref, gate_ref, h_ref, x_buf, x_sem):
    i = pl.program_id(0)
    tm = x_buf.shape[0]

    def x_copy(blk):
        return pltpu.make_async_copy(x_hbm.at[pl.ds(pl.multiple_of(blk * tm, tm), tm)], x_buf, x_sem)

    @pl.when(pl.program_id(1) == 0)
    def _():
        @pl.when(i == 0)
        def _():
            x_copy(i).start()

        x_copy(i).wait()
        x = x_buf[...]
        y = x * lax.rsqrt(jnp.mean(x * x, axis=-1, keepdims=True) + EPS) * g_ref[...]
        h = (y * (1.0 + mod_ref[0, 1:2, :]) + mod_ref[0, 0:1, :]).astype(BF16)
        h_ref[...] = h
        gate_ref[...] = _dot_nt(h, wgt_ref[...])

        @pl.when(i + 1 < pl.num_programs(0))
        def _():
            x_copy(i + 1).start(priority=1)

    proj_ref[...] = _dot_nt(h_ref[...], wt_ref[...].astype(BF16)).astype(BF16)


def _inproj(x2, mod3, g, w_t, wg_t, seq):
    n, d = x2.shape
    tm = min(INPROJ_TM, seq)
    tn = INPROJ_TN
    nj = PROJ_MAIN // tn
    return pl.pallas_call(
        _inproj_kernel,
        grid=(n // tm, nj),
        in_specs=[pl.BlockSpec(memory_space=pl.ANY),
                  pl.BlockSpec((1, 6, d), lambda i, j: ((i * tm) // seq, 0, 0)),
                  pl.BlockSpec((1, d), lambda i, j: (0, 0)),
                  pl.BlockSpec((tn, d), lambda i, j: (j, 0)),
                  pl.BlockSpec((LANES, d), lambda i, j: (0, 0))],
        out_specs=[pl.BlockSpec((tm, tn), lambda i, j: (i, j)),
                   pl.BlockSpec((tm, LANES), lambda i, j: (i, 0))],
        out_shape=[jax.ShapeDtypeStruct((n, PROJ_MAIN), BF16),
                   jax.ShapeDtypeStruct((n, LANES), F32)],
        scratch_shapes=[pltpu.VMEM((tm, d), BF16), pltpu.VMEM((tm, d), F32),
                        pltpu.SemaphoreType.DMA(())],
        compiler_params=_cp(("arbitrary", "arbitrary")),
        name="inproj",
    )(x2, mod3, g.reshape(1, d), w_t, wg_t)


def _retention_kernel(q_ref, k_ref, v_ref, g_ref, cos_ref, sin_ref, din_ref, zeta_ref,
                      qdec_ref, cdec_ref, gn_ref, o_ref, s_ref):
    @pl.when(pl.program_id(1) == 0)
    def _():
        s_ref[...] = jnp.zeros_like(s_ref)

    half = RET_DK // 2
    c = RET_CHUNK

    for sub in range(q_ref.shape[0] // c):
        rows = slice(sub * c, (sub + 1) * c)
        cos = cos_ref[rows, :]
        sin = sin_ref[rows, :]

        def rot(a):
            a1, a2 = a[:, :half], a[:, half:]
            return jnp.concatenate([a1 * cos - a2 * sin, a1 * sin + a2 * cos], axis=1)

        for h in range(RET_HEADS):
            qs = slice(h * RET_DK, (h + 1) * RET_DK)
            vs = slice(h * RET_DV, (h + 1) * RET_DV)
            q = rot(q_ref[rows, qs].astype(F32))
            k = rot(k_ref[rows, qs].astype(F32)) * (RET_DK ** -0.5)
            v = v_ref[rows, vs]
            qb = q.astype(BF16)
            kb = k.astype(BF16)
            s = _dot_nt(qb, kb) * din_ref[h]
            inner = _dot(s.astype(BF16), v)
            s_prev = s_ref[h]
            cross = _dot(qb, s_prev.astype(BF16)) * qdec_ref[h]
            kv = _dot_tn((k * zeta_ref[h]).astype(BF16), v)
            s_ref[h] = cdec_ref[h] * s_prev + kv
            o = inner + cross
            mu = jnp.mean(o, axis=-1, keepdims=True)
            oc = o - mu
            var = jnp.mean(oc * oc, axis=-1, keepdims=True)
            o = oc * lax.rsqrt(var + EPS) * gn_ref[:, vs]
            o_ref[rows, vs] = (o * _silu(g_ref[rows, vs].astype(F32))).astype(BF16)


def _retention(proj, gn_g, bsz, seq):
    n = proj.shape[0]
    c = RET_CHUNK
    nc = seq // c
    hw = RET_HEADS * RET_DK
    half = RET_DK // 2
    pos = jnp.arange(seq, dtype=F32)
    inv = ROPE_BASE ** (-jnp.arange(half, dtype=F32) / half)
    ang = pos[:, None] * inv[None, :]
    cos, sin = jnp.cos(ang), jnp.sin(ang)
    log_gamma = jnp.log1p(-jnp.exp2(-5.0 - jnp.arange(RET_HEADS, dtype=F32)))
    idx = jnp.arange(c, dtype=F32)
    rel = idx[:, None] - idx[None, :]
    decay_in = jnp.where(rel >= 0, jnp.exp(log_gamma[:, None, None] * jnp.maximum(rel, 0.0)), 0.0)
    zeta = jnp.exp(log_gamma[:, None] * (c - 1 - idx)[None, :])[:, :, None]
    q_decay = jnp.exp(log_gamma[:, None] * (idx + 1)[None, :])[:, :, None]
    chunk_decay = jnp.exp(log_gamma * c)[:, None, None]
    rb = min(RET_CHUNKS * c, seq)
    ns = seq // rb
    row = lambda b, t: (b * ns + t)
    return pl.pallas_call(
        _retention_kernel,
        grid=(bsz, ns),
        in_specs=[pl.BlockSpec((rb, hw), lambda b, t: (row(b, t), 0)),
                  pl.BlockSpec((rb, hw), lambda b, t: (row(b, t), 1)),
                  pl.BlockSpec((rb, hw), lambda b, t: (row(b, t), 2)),
                  pl.BlockSpec((rb, hw), lambda b, t: (row(b, t), 3)),
                  pl.BlockSpec((rb, half), lambda b, t: (t, 0)),
                  pl.BlockSpec((rb, half), lambda b, t: (t, 0)),
                  pl.BlockSpec((RET_HEADS, c, c), lambda b, t: (0, 0, 0)),
                  pl.BlockSpec((RET_HEADS, c, 1), lambda b, t: (0, 0, 0)),
                  pl.BlockSpec((RET_HEADS, c, 1), lambda b, t: (0, 0, 0)),
                  pl.BlockSpec((RET_HEADS, 1, 1), lambda b, t: (0, 0, 0)),
                  pl.BlockSpec((1, hw), lambda b, t: (0, 0))],
        out_specs=pl.BlockSpec((rb, hw), lambda b, t: (row(b, t), 0)),
        out_shape=jax.ShapeDtypeStruct((n, hw), BF16),
        scratch_shapes=[pltpu.VMEM((RET_HEADS, RET_DK, RET_DV), F32)],
        compiler_params=_cp(("arbitrary", "arbitrary")),
        name="retention",
    )(proj, proj, proj, proj, cos, sin, decay_in, zeta, q_decay, chunk_decay,
      gn_g.reshape(1, hw))


def _compress_kernel(ak_ref, av_ref, pek_ref, pev_ref, w1k_ref, w1v_ref, w2k_ref, w2v_ref,
                     kc_ref, vc_ref, a_scr):
    seq = ak_ref.shape[0]
    nblk = seq // CMP_STRIDE
    a_scr[seq:, :] = jnp.zeros((a_scr.shape[0] - seq, NSA_D), F32)

    def one(a_ref, pe_ref, w1_ref, w2_ref, o_ref):
        a_scr[:seq, :] = a_ref[...].astype(F32)
        flat = jnp.concatenate([a_scr[pl.ds(l, nblk, stride=CMP_STRIDE), :] for l in range(CMP_BLOCK)],
                               axis=1)
        pre = _dot((flat + pe_ref[...]).astype(BF16), w1_ref[...].astype(BF16))
        o_ref[...] = _dot(_silu(pre).astype(BF16), w2_ref[...].astype(BF16)).astype(BF16)

    one(ak_ref, pek_ref, w1k_ref, w2k_ref, kc_ref)
    one(av_ref, pev_ref, w1v_ref, w2v_ref, vc_ref)


def _compress(proj, pek, pev, w1k, w1v, w2k, w2v, bsz, seq):
    g_ = NSA_KV_GROUPS
    nblk = seq // CMP_STRIDE
    full = lambda a: pl.BlockSpec(a.shape, lambda i: (0,) * a.ndim)
    kblk = pl.BlockSpec((seq, NSA_D), lambda i: (i // g_, _C_KC + i % g_))
    vblk = pl.BlockSpec((seq, NSA_D), lambda i: (i // g_, _C_VC + i % g_))
    oblk = pl.BlockSpec((nblk, NSA_D), lambda i: (i, 0))
    return pl.pallas_call(
        _compress_kernel,
        grid=(bsz * g_,),
        in_specs=[kblk, vblk, full(pek), full(pev), full(w1k), full(w1v), full(w2k), full(w2v)],
        out_specs=[oblk, oblk],
        out_shape=[jax.ShapeDtypeStruct((bsz * g_ * nblk, NSA_D), BF16)] * 2,
        scratch_shapes=[pltpu.VMEM((seq + CMP_BLOCK, NSA_D), F32)],
        compiler_params=_cp(("arbitrary",)),
        name="compress",
    )(proj, proj, pek, pev, w1k, w1v, w2k, w2v)


def _nsa_kernel(q_ref, kc_ref, vc_ref, ks_ref, vs_ref, kw_ref, vw_ref, gate_ref, ovt_ref, wbias_ref, dbias_ref,
                cbias_ref,
                o_ref, kaug_ref, vst_ref, vwt_ref, *, tq, tk, vt, seq, wlen):
    crow = kc_ref.shape[0]
    i = pl.program_id(2)
    hg = NSA_HPG
    d = NSA_D
    r = hg * tq

    @pl.when(i == 0)
    def _():
        kaug_ref[:, :d] = ks_ref[...]
        blk = lax.broadcasted_iota(I32, (seq, LANES), 0) // SEL_BLOCK
        lane = lax.broadcasted_iota(I32, (seq, LANES), 1)
        kaug_ref[:, d:] = (blk == lane).astype(BF16)
        ones = jnp.ones((vst_ref.shape[1] - d, vt), BF16)
        for c in range(seq // vt):
            vst_ref[c, :d, :] = vs_ref[c * vt:(c + 1) * vt, :].astype(F32).T.astype(BF16)
            vwt_ref[c, :d, :] = vw_ref[c * vt:(c + 1) * vt, :].astype(F32).T.astype(BF16)
            vst_ref[c, d:, :] = ones
            vwt_ref[c, d:, :] = ones

    def pv_t(vt_ref, first_tile, p):
        out = None
        for c in range(p.shape[0] // vt):
            part = _dot(vt_ref[first_tile + c], p[c * vt:(c + 1) * vt, :])
            out = part if out is None else out + part
        return out

    q = q_ref[...]
    qh = [(q[:, h * d:(h + 1) * d].astype(F32) * (d ** -0.5)).astype(BF16) for h in range(hg)]
    qa = jnp.concatenate(qh, axis=0)
    t_row = i * tq + lax.broadcasted_iota(I32, (1, tq), 1)
    tpos_row = jnp.concatenate([t_row] * hg, axis=1)

    w0 = pl.multiple_of(jnp.maximum(i * tq + tq - wlen, 0), vt)
    sw = _dot_nt(kw_ref[pl.ds(w0, wlen), :], qa)
    wbias = wbias_ref[jnp.minimum(i, wbias_ref.shape[0] - 1)]
    sw = sw + jnp.concatenate([wbias] * hg, axis=1)
    e_w = jnp.exp(sw - jnp.max(sw, axis=0, keepdims=True))
    pv_w = pv_t(vwt_ref, w0 // vt, e_w.astype(BF16))

    sct = _dot_nt(kc_ref[...], qa)
    sct = sct + jnp.concatenate([cbias_ref[i]] * hg, axis=1)
    e_c = jnp.exp(sct - jnp.max(sct, axis=0, keepdims=True))
    den_c = jnp.sum(e_c, axis=0, keepdims=True)
    seen = tpos_row >= CMP_BLOCK - 1
    p_t = jnp.where(seen, e_c / den_c, 0.0)
    o_cmp = _dot_tn(p_t.astype(BF16), vc_ref[...])

    psum_t = p_t[:, 0:tq]
    for h in range(1, hg):
        psum_t = psum_t + p_t[:, h * tq:(h + 1) * tq]
    nsel_blocks = seq // SEL_BLOCK
    nsb = ovt_ref.shape[0]
    imp_t = jnp.dot(ovt_ref[...], psum_t, preferred_element_type=F32,
                    precision=lax.Precision.HIGHEST)
    sidx = lax.broadcasted_iota(I32, (nsb, 1), 0)
    cur = t_row // SEL_BLOCK
    valid = sidx <= cur
    forced = (sidx == 0) | (sidx == cur) | (sidx == cur - 1)
    score = jnp.where(valid, imp_t + jnp.where(forced, FORCE_BONUS, 0.0), -1.0)
    rank = jnp.zeros((nsb, tq), F32)
    for s2 in range(nsel_blocks):
        row = score[s2:s2 + 1, :]
        beats = (row > score) | ((row == score) & (sidx > s2))
        rank = rank + beats.astype(F32)
    sel = valid & (rank < float(min(SEL_COUNT, nsel_blocks)))
    selb_t = jnp.where(sel, 0.0, NEG)
    selb_t = jnp.concatenate([selb_t, jnp.zeros((LANES - nsb, tq), F32)], axis=0)
    selb = selb_t.T.astype(BF16)
    q_aug = jnp.concatenate([jnp.concatenate([qh[h], selb], axis=1) for h in range(hg)], axis=0)

    n_full = (i * tq + 1) // tk
    kl = lax.broadcasted_iota(I32, (tk, 1), 0)

    def tile(j, carry, masked):
        m, acc = carry
        k0 = pl.multiple_of(j * tk, tk)
        s = _dot_nt(kaug_ref[pl.ds(k0, tk), :], q_aug)
        if masked and tq == tk:
            s = s + jnp.concatenate([dbias_ref[...]] * hg, axis=1)
        elif masked:
            s = jnp.where(k0 + kl <= tpos_row, s, NEG)
        m_new = jnp.maximum(m, jnp.max(s, axis=0, keepdims=True))
        p = jnp.exp(s - m_new)
        acc = jnp.exp(m - m_new) * acc + pv_t(vst_ref, j * (tk // vt), p.astype(BF16))
        return m_new, acc

    m0 = jnp.full((1, r), NEG, F32)
    a0 = jnp.zeros((vst_ref.shape[1], r), F32)
    carry = lax.fori_loop(0, n_full, functools.partial(tile, masked=False), (m0, a0))
    _, acc_s = tile(n_full, carry, True)

    gt = jax.nn.sigmoid(gate_ref[...])
    grp = pl.program_id(1)
    for g in range(1, NSA_KV_GROUPS):
        gt = jnp.where(grp == g, pltpu.roll(gt, LANES - g * 3 * hg, 1), gt)
    gt_t = gt.T
    gate_row = lambda j: jnp.concatenate([gt_t[3 * h + j:3 * h + j + 1, :] for h in range(hg)], axis=1)
    o_sw = (acc_s[:d] * (gate_row(1) / acc_s[d:d + 1]) + pv_w[:d] * (gate_row(2) / pv_w[d:d + 1])).T
    for h in range(hg):
        rows = slice(h * tq, (h + 1) * tq)
        o = gt[:, 3 * h:3 * h + 1] * o_cmp[rows] + o_sw[rows]
        o_ref[:, h * d:(h + 1) * d] = o.astype(BF16)


def _nsa(proj, kc, vc, gates_g, bsz, seq):
    n = proj.shape[0]
    g_ = NSA_KV_GROUPS
    tq = NSA_TQ
    tk = min(NSA_TK, seq)
    nq = seq // tq
    ncmp = (seq - CMP_BLOCK) // CMP_STRIDE + 1
    wlen = min(WIN + tq, seq)
    nsel = seq // SEL_BLOCK
    crow = seq // CMP_STRIDE
    assert nsel <= LANES and crow <= LANES and tq <= tk and tk % tq == 0
    nsb = -(-nsel // 8) * 8
    ss = np.arange(nsb)[:, None] * SEL_BLOCK
    cs = np.arange(crow)[None, :] * CMP_STRIDE
    ov = ((cs < ss + SEL_BLOCK) & (cs + CMP_BLOCK > ss)
          & (np.arange(crow)[None, :] < ncmp) & (np.arange(nsb)[:, None] < nsel))
    ov = jnp.asarray(ov.astype(np.float32))
    dq = np.arange(tq)[None, :]
    npat = min(nq, (wlen - tq) // tq + 1)
    kwin = np.arange(wlen)[:, None]
    pats = []
    for p in range(npat):
        delta = (p * tq + dq) - (max(p * tq + tq - wlen, 0) + kwin)
        pats.append(np.where((delta >= 0) & (delta < WIN), 0.0, NEG))
    wbias = jnp.asarray(np.stack(pats).astype(np.float32))
    dbias = jnp.asarray(np.where(np.arange(tk)[:, None] <= dq, 0.0, NEG).astype(np.float32))
    cend = (np.arange(crow) * CMP_STRIDE + CMP_BLOCK - 1)[None, :, None]
    tabs = (np.arange(nq)[:, None, None] * tq + dq[None])
    cbias = jnp.asarray(np.where((cend <= tabs) & (np.arange(crow)[None, :, None] < ncmp), 0.0, NEG)
                        .astype(np.float32))
    kvspec = lambda c0: pl.BlockSpec((seq, NSA_D), lambda b, g, i: (b, c0 + g))
    vt = NSA_VT
    assert tk % vt == 0 and wlen % vt == 0 and tq % vt == 0 and seq % vt == 0
    kern = functools.partial(_nsa_kernel, tq=tq, tk=tk, vt=vt, seq=seq, wlen=wlen)
    return pl.pallas_call(
        kern,
        grid=(bsz, g_, nq),
        in_specs=[pl.BlockSpec((tq, NSA_HPG * NSA_D), lambda b, g, i: (b * nq + i, _C_NQ // NSA_HPG + g)),
                  pl.BlockSpec((crow, NSA_D), lambda b, g, i: (b * g_ + g, 0)),
                  pl.BlockSpec((crow, NSA_D), lambda b, g, i: (b * g_ + g, 0)),
                  kvspec(_C_KS), kvspec(_C_VS), kvspec(_C_KW), kvspec(_C_VW),
                  pl.BlockSpec((tq, LANES), lambda b, g, i: (b * nq + i, 0)),
                  pl.BlockSpec((nsb, crow), lambda b, g, i: (0, 0)),
                  pl.BlockSpec((npat, wlen, tq), lambda b, g, i: (0, 0, 0), pipeline_mode=pl.Buffered(1)),
                  pl.BlockSpec((tk, tq), lambda b, g, i: (0, 0), pipeline_mode=pl.Buffered(1)),
                  pl.BlockSpec((nq, crow, tq), lambda b, g, i: (0, 0, 0), pipeline_mode=pl.Buffered(1))],
        out_specs=pl.BlockSpec((tq, NSA_HPG * NSA_D), lambda b, g, i: (b * nq + i, g)),
        out_shape=jax.ShapeDtypeStruct((n, NSA_HEADS * NSA_D), BF16),
        scratch_shapes=[pltpu.VMEM((seq, 2 * NSA_D), BF16),
                        pltpu.VMEM((seq // vt, NSA_D + 2 * SUBLANES, vt), BF16),
                        pltpu.VMEM((seq // vt, NSA_D + 2 * SUBLANES, vt), BF16)],
        compiler_params=_cp(("arbitrary", "arbitrary", "arbitrary")),
        name="nsa",
    )(proj, kc, vc, proj, proj, proj, proj, gates_g, ov, wbias, dbias, cbias)


def _outproj_kernel(oret_ref, onsa_ref, x_ref, mod_ref, g_ref, w_ref, wr_ref,
                    x1_ref, h2_ref, lt_ref):
    hw = oret_ref.shape[1]
    mix = (_dot(oret_ref[...], w_ref[:hw, :].astype(BF16))
           + _dot(onsa_ref[...], w_ref[hw:, :].astype(BF16)))
    x1 = x_ref[...] + mod_ref[0, 2:3, :] * mix
    x1_ref[...] = x1
    y = x1 * lax.rsqrt(jnp.mean(x1 * x1, axis=-1, keepdims=True) + EPS) * g_ref[...]
    h2 = y * (1.0 + mod_ref[0, 4:5, :]) + mod_ref[0, 3:4, :]
    h2_ref[...] = h2
    lt_ref[...] = _dot_nt(wr_ref[...], h2.astype(BF16))


def _outproj(o_ret, o_nsa, x2, mod3, g, w, wr_bf, seq):
    n, d = x2.shape
    tm = min(OUTPROJ_TM, seq)
    hw = o_ret.shape[1]
    return pl.pallas_call(
        _outproj_kernel,
        grid=(n // tm,),
        in_specs=[pl.BlockSpec((tm, hw), lambda i: (i, 0)),
                  pl.BlockSpec((tm, o_nsa.shape[1]), lambda i: (i, 0)),
                  pl.BlockSpec((tm, d), lambda i: (i, 0)),
                  pl.BlockSpec((1, 6, d), lambda i: ((i * tm) // seq, 0, 0)),
                  pl.BlockSpec((1, d), lambda i: (0, 0)),
                  pl.BlockSpec(w.shape, lambda i: (0, 0), pipeline_mode=pl.Buffered(1)),
                  pl.BlockSpec(wr_bf.shape, lambda i: (0, 0))],
        out_specs=[pl.BlockSpec((tm, d), lambda i: (i, 0)),
                   pl.BlockSpec((tm, d), lambda i: (i, 0)),
                   pl.BlockSpec((LANES, tm), lambda i: (0, i))],
        out_shape=[jax.ShapeDtypeStruct((n, d), F32),
                   jax.ShapeDtypeStruct((n, d), F32),
                   jax.ShapeDtypeStruct((LANES, n), F32)],
        compiler_params=_cp(("arbitrary",)),
        name="outproj",
    )(o_ret, o_nsa, x2, mod3, g.reshape(1, d), w, wr_bf)


def _route_kernel(lt_ref, b_ref, tri_ref, ids_ref, wts_ref, cnt_ref, carry_ref, *, sub):
    @pl.when(pl.program_id(0) == 0)
    def _():
        carry_ref[...] = jnp.zeros_like(carry_ref)

    ng, ne = N_GROUPS, EXP_PER_GROUP
    l = lt_ref[...] + b_ref[:, 0:1]
    tc = l.shape[1]
    ridx = lax.broadcasted_iota(I32, (ng, tc), 0).astype(F32)

    def softmax0(v):
        e = jnp.exp(v - jnp.max(v, axis=0, keepdims=True))
        return e / jnp.sum(e, axis=0, keepdims=True)

    def top1(p):
        top = jnp.max(p, axis=0, keepdims=True)
        idx = jnp.min(jnp.where(p == top, ridx, float(ng)), axis=0, keepdims=True)
        return top, idx

    pg_top, grp = top1(softmax0(l[0:ng]))
    leg = jnp.zeros((ne, tc), F32)
    for g in range(ng):
        leg = jnp.where(grp == float(g), l[ng + g * ne:ng + (g + 1) * ne], leg)
    pe = softmax0(leg)
    p1, i1 = top1(pe)
    p2, i2 = top1(jnp.where(ridx == i1, -1.0, pe))
    den = p1 + p2
    w1 = pg_top * p1 / den
    w2 = pg_top * p2 / den
    e1 = grp * float(ne) + i1
    e2 = grp * float(ne) + i2

    eio = lax.broadcasted_iota(I32, (N_EXPERTS, sub), 0).astype(F32)
    r1 = []
    r2 = []
    carry = carry_ref[:, 0:1]
    for c in range(tc // sub):
        cs = slice(c * sub, (c + 1) * sub)
        oh1 = (eio == e1[:, cs]).astype(F32)
        oh2 = (eio == e2[:, cs]).astype(F32)
        oh = oh1 + oh2
        before = carry + _dot(oh.astype(BF16), tri_ref[...])
        r1.append(jnp.sum(oh1 * before, axis=0, keepdims=True))
        r2.append(jnp.sum(oh2 * before, axis=0, keepdims=True))
        carry = carry + jnp.sum(oh, axis=1, keepdims=True)
    carry_ref[...] = jnp.broadcast_to(carry, carry_ref.shape)
    cnt_ref[...] = jnp.broadcast_to(carry, cnt_ref.shape).astype(I32)
    r1 = jnp.concatenate(r1, axis=1)
    r2 = jnp.concatenate(r2, axis=1)
    zf = jnp.zeros((4, tc), F32)
    ids_ref[...] = jnp.concatenate([e1, e2, r1, r2, zf], axis=0).astype(I32)
    wts_ref[...] = jnp.concatenate([w1, w2, jnp.zeros((6, tc), F32)], axis=0)


def _route(lt, bias_col):
    n = lt.shape[1]
    tc = min(ROUTE_TC, n)
    sub = min(ROUTE_SUB, tc)
    tri =jnp.asarray(np.triu(np.ones((sub, sub), np.float32), 1), BF16)
    return pl.pallas_call(
        functools.partial(_route_kernel, sub=sub),
        grid=(n // tc,),
        in_specs=[pl.BlockSpec((LANES, tc), lambda i: (0, i)),
                  pl.BlockSpec((LANES, LANES), lambda i: (0, 0)),
                  pl.BlockSpec((sub, sub), lambda i: (0, 0))],
        out_specs=[pl.BlockSpec((8, tc), lambda i: (0, i)),
                   pl.BlockSpec((8, tc), lambda i: (0, i)),
                   pl.BlockSpec((N_EXPERTS, LANES), lambda i: (0, 0))],
        out_shape=[jax.ShapeDtypeStruct((8, n), I32),
                   jax.ShapeDtypeStruct((8, n), F32),
                   jax.ShapeDtypeStruct((N_EXPERTS, LANES), I32)],
        scratch_shapes=[pltpu.VMEM((N_EXPERTS, LANES), F32)],
        compiler_params=_cp(("arbitrary",)),
        name="route",
    )(lt, bias_col, tri)


def _dest_kernel(ids_ref, ps_ref, o_ref):
    ids = ids_ref[...].astype(F32)
    tc = ids.shape[1]
    eio = lax.broadcasted_iota(I32, (N_EXPERTS, tc), 0).astype(F32)
    ps = ps_ref[:, 0:1]
    rows = [jnp.sum(jnp.where(eio == ids[k:k + 1], ps, 0.0), axis=0, keepdims=True) + ids[2 + k:3 + k]
            for k in range(2)]
    o_ref[...] = jnp.concatenate(rows + [jnp.zeros((6, tc), F32)], axis=0).astype(I32)


def _dest(ids, pad_start):
    n = ids.shape[1]
    tc = min(ROUTE_TC, n)
    ps = jnp.broadcast_to(pad_start.astype(F32)[:, None], (N_EXPERTS, LANES))
    dest = pl.pallas_call(
        _dest_kernel,
        grid=(n // tc,),
        in_specs=[pl.BlockSpec((8, tc), lambda i: (0, i)),
                  pl.BlockSpec((N_EXPERTS, LANES), lambda i: (0, 0))],
        out_specs=pl.BlockSpec((8, tc), lambda i: (0, i)),
        out_shape=jax.ShapeDtypeStruct((8, n), I32),
        compiler_params=_cp(("arbitrary",)),
        name="dest",
    )(ids, ps)
    return dest[:2].reshape(2 * n)


def _dispatch_kernel(dst0_ref, dst1_ref, ps_ref, cnt_ref, h_ref, xs_ref, zbuf, sem, zsem, *, tcd, tm, zr):
    def zero_fill(wait):
        sizes = [zr >> b for b in range(zr.bit_length()) if (zr >> b) >= SUBLANES]

        def zero_rows(row, size):
            return pltpu.make_async_copy(zbuf.at[pl.ds(0, size)],
                                         xs_ref.at[pl.ds(pl.multiple_of(row, SUBLANES), size)], zsem)

        def zero_row(row):
            return pltpu.make_async_copy(zbuf.at[pl.ds(0, 1)], xs_ref.at[pl.ds(row, 1)], zsem)

        def fill(e, _):
            cnt = cnt_ref[e]
            cnt8 = lax.div(cnt + (SUBLANES - 1), SUBLANES) * SUBLANES
            gap = lax.rem(tm - lax.rem(cnt8, tm), tm)
            base = ps_ref[e]

            def single(c, _):
                cp = zero_row(base + cnt + c)
                cp.wait() if wait else cp.start()
                return 0

            lax.fori_loop(0, cnt8 - cnt, single, 0)
            row = base + cnt8
            for size in sizes:
                has = lax.rem(lax.div(gap, size), 2) == 1

                @pl.when(has)
                def _():
                    cp = zero_rows(row, size)
                    cp.wait() if wait else cp.start()

                row = row + jnp.where(has, size, 0)
            return 0

        last = N_EXPERTS - 1
        used_rows = ps_ref[last] + lax.div(cnt_ref[last] + (tm - 1), tm) * tm
        n_tail = lax.div(xs_ref.shape[0] - used_rows, zr)

        def tail(c, _):
            cp = zero_rows(used_rows + c * zr, zr)
            cp.wait() if wait else cp.start()
            return 0

        lax.fori_loop(0, N_EXPERTS, fill, 0)
        lax.fori_loop(0, n_tail, tail, 0)

    @pl.when(pl.program_id(0) == 0)
    def _():
        zbuf[...] = jnp.zeros_like(zbuf)
        zero_fill(wait=False)

    def issue(t, _):
        for k, dst_ref in enumerate((dst0_ref, dst1_ref)):
            dst = dst_ref[t]
            pltpu.make_async_copy(h_ref.at[pl.ds(t, 1)], xs_ref.at[pl.ds(dst, 1)], sem).start(priority=k)
        return 0

    lax.fori_loop(0, tcd, issue, 0, unroll=8)
    for k in range(2):
        pltpu.make_async_copy(h_ref, xs_ref.at[pl.ds(0, tcd)], sem).wait()

    @pl.when(pl.program_id(0) == pl.num_programs(0) - 1)
    def _():
        zero_fill(wait=True)


def _dispatch(dest, pad_start, counts, h2, rows, tcd, tm):
    n, d = h2.shape
    assert tm & (tm - 1) == 0 and tm >= 2 * SUBLANES
    zr = tm // 2
    return pl.pallas_call(
        functools.partial(_dispatch_kernel, tcd=tcd, tm=tm, zr=zr),
        grid=(n // tcd,),
        in_specs=[pl.BlockSpec((tcd,), lambda i: (i,), memory_space=pltpu.SMEM),
                  pl.BlockSpec((tcd,), lambda i: (n // tcd + i,), memory_space=pltpu.SMEM),
                  pl.BlockSpec(memory_space=pltpu.SMEM),
                  pl.BlockSpec(memory_space=pltpu.SMEM),
                  pl.BlockSpec((tcd, d), lambda i: (i, 0))],
        out_specs=pl.BlockSpec(memory_space=pl.ANY),
        out_shape=jax.ShapeDtypeStruct((rows, d), h2.dtype),
        scratch_shapes=[pltpu.VMEM((zr, d), h2.dtype), pltpu.SemaphoreType.DMA(()),
                        pltpu.SemaphoreType.DMA(())],
        compiler_params=_cp(("arbitrary",)),
        name="dispatch",
    )(dest, dest, pad_start, counts, h2)


def _experts_kernel(be_ref, ord_ref, nxt_ref, nu_ref, xs_ref, wg_hbm, wu_hbm, wd_hbm, ys_ref,
                    g_f, u_f, d_f, g_s, u_s, d_s, sem):
    i = pl.program_id(0)
    used = i < nu_ref[0]
    e = be_ref[i]
    fresh = (i == 0) | (e != be_ref[jnp.maximum(i - 1, 0)])
    slot = lax.rem(ord_ref[i], 2)

    def fetch(expert, s):
        return [pltpu.make_async_copy(w.at[expert], f.at[s], sem.at[s, k])
                for k, (w, f) in enumerate(((wg_hbm, g_f), (wu_hbm, u_f), (wd_hbm, d_f)))]

    @pl.when(used & (i == 0))
    def _():
        for c in fetch(e, slot):
            c.start(priority=1)

    @pl.when(used & fresh)
    def _():
        for c in fetch(e, slot):
            c.wait()
        nxt = nxt_ref[i]

        @pl.when(nxt >= 0)
        def _():
            for c in fetch(nxt, 1 - slot):
                c.start(priority=1)

        g_s[...] = g_f[slot].astype(BF16)
        u_s[...] = u_f[slot].astype(BF16)
        d_s[...] = d_f[slot].astype(BF16)

    @pl.when(used)
    def _():
        x = xs_ref[...].astype(BF16)
        a = _silu(_dot(x, g_s[...])) * _dot(x, u_s[...])
        ys_ref[...] = _dot(a.astype(BF16), d_s[...])

    @pl.when(jnp.logical_not(used))
    def _():
        ys_ref[...] = jnp.zeros_like(ys_ref)


def _experts(blk_exp, blk_ord, blk_nxt, n_used, xs, w_gate, w_up, w_down, tm):
    d = xs.shape[1]
    de = w_gate.shape[2]
    nb = blk_exp.shape[0]
    row = lambda i, be, od, nx, nu: (jnp.minimum(i, nu[0] - 1), 0)
    anyspec = pl.BlockSpec(memory_space=pl.ANY)
    grid_spec = pltpu.PrefetchScalarGridSpec(
        num_scalar_prefetch=4,
        grid=(nb,),
        in_specs=[pl.BlockSpec((tm, d), row), anyspec, anyspec, anyspec],
        out_specs=pl.BlockSpec((tm, d), lambda i, be, od, nx, nu: (i, 0)),
        scratch_shapes=[pltpu.VMEM((2, d, de), F32), pltpu.VMEM((2, d, de), F32),
                        pltpu.VMEM((2, de, d), F32),
                        pltpu.VMEM((d, de), BF16), pltpu.VMEM((d, de), BF16),
                        pltpu.VMEM((de, d), BF16),
                        pltpu.SemaphoreType.DMA((2, 3))],
    )
    return pl.pallas_call(
        _experts_kernel,
        grid_spec=grid_spec,
        out_shape=jax.ShapeDtypeStruct((nb * tm, d), F32),
        compiler_params=_cp(("arbitrary",)),
        name="experts",
    )(blk_exp, blk_ord, blk_nxt, n_used, xs, w_gate, w_up, w_down)


def _combine_kernel(cur0_ref, cur1_ref, nxt0_ref, nxt1_ref, ys_ref, x1_ref, wt_ref, mod_ref, g_ref,
                    o_ref, buf, sem, *, tc):
    i = pl.program_id(0)
    slot = lax.rem(i, 2)

    def gather(id_refs, s):
        def issue(t, _):
            for k, id_ref in enumerate(id_refs):
                src = id_ref[t]
                pltpu.make_async_copy(ys_ref.at[pl.ds(src, 1)], buf.at[s, k, pl.ds(t, 1)],
                                      sem.at[s]).start(priority=k)
            return 0

        lax.fori_loop(0, tc, issue, 0, unroll=8)

    @pl.when(i == 0)
    def _():
        gather((cur0_ref, cur1_ref), 0)

    @pl.when(i + 1 < pl.num_programs(0))
    def _():
        gather((nxt0_ref, nxt1_ref), 1 - slot)

    for k in range(2):
        pltpu.make_async_copy(ys_ref.at[pl.ds(0, tc)], buf.at[slot, k], sem.at[slot]).wait()

    w_rows = wt_ref[...]
    w_cols = jnp.concatenate([w_rows, jnp.zeros((LANES - w_rows.shape[0], tc), F32)], axis=0).T
    moe = buf[slot, 0] * w_cols[:, 0:1] + buf[slot, 1] * w_cols[:, 1:2]
    x2 = x1_ref[...] + mod_ref[0, 5:6, :] * moe
    o_ref[...] = x2 * lax.rsqrt(jnp.mean(x2 * x2, axis=-1, keepdims=True) + EPS) * g_ref[...]


def _combine(dest, ys, x1, wts, mod3, final_g, seq, tc):
    n, d = x1.shape
    nt = n // tc
    tbl = lambda k, off: pl.BlockSpec((tc,), lambda i: (k * nt + jnp.minimum(i + off, nt - 1),),
                                      memory_space=pltpu.SMEM)
    return pl.pallas_call(
        functools.partial(_combine_kernel, tc=tc),
        grid=(nt,),
        in_specs=[tbl(0, 0), tbl(1, 0), tbl(0, 1), tbl(1, 1),
                  pl.BlockSpec(memory_space=pl.ANY),
                  pl.BlockSpec((tc, d), lambda i: (i, 0)),
                  pl.BlockSpec((8, tc), lambda i: (0, i)),
                  pl.BlockSpec((1, 6, d), lambda i: ((i * tc) // seq, 0, 0)),
                  pl.BlockSpec((1, d), lambda i: (0, 0))],
        out_specs=pl.BlockSpec((tc, d), lambda i: (i, 0)),
        out_shape=jax.ShapeDtypeStruct((n, d), F32),
        scratch_shapes=[pltpu.VMEM((2, 2, tc, d), F32), pltpu.SemaphoreType.DMA((2,))],
        compiler_params=_cp(("arbitrary",)),
        name="combine",
    )(dest, dest, dest, dest, ys, x1, wts, mod3, final_g.reshape(1, d))


def _token_mixer(x2, mod3, norm1_g, w_in, ret_gn_g, cmp_pos_k, cmp_w1_k, cmp_w2_k,
                 cmp_pos_v, cmp_w1_v, cmp_w2_v, bsz, seq):
    n, d = x2.shape
    w_t = w_in.T
    w_gate_t = jnp.pad(w_t[PROJ_MAIN:], ((0, LANES - N_GATE_COLS), (0, 0))).astype(BF16)
    proj, gate_logits = _inproj(x2, mod3, norm1_g, w_t, w_gate_t, seq)

    o_ret = _retention(proj, ret_gn_g, bsz, seq)

    g_ = NSA_KV_GROUPS
    crow = seq // CMP_STRIDE

    pe_flat = lambda pe: pe.reshape(1, CMP_BLOCK * NSA_D)
    kc, vc = _compress(proj, pe_flat(cmp_pos_k), pe_flat(cmp_pos_v),
                       cmp_w1_k, cmp_w1_v, cmp_w2_k, cmp_w2_v, bsz, seq)

    o_nsa = _nsa(proj, kc, vc, gate_logits, bsz, seq)
    return o_ret, o_nsa


def _moe(h2, lt, x1, mod3, final_g, b_grp, b_exp, w_gate, w_up, w_down, seq, tm):
    n, d = h2.shape
    bias_col = jnp.zeros((LANES,), F32).at[:N_GROUPS].set(b_grp).at[N_GROUPS:N_GROUPS + N_EXPERTS].set(b_exp)
    bias_col = jnp.broadcast_to(bias_col[:, None], (LANES, LANES))
    ids, wts, cnt = _route(lt, bias_col)

    counts = cnt[:, 0]
    padded = (counts + tm - 1) // tm * tm
    pad_end = jnp.cumsum(padded)
    pad_start = (pad_end - padded).astype(I32)
    nb = (2 * n) // tm + N_EXPERTS
    n_used = (pad_end[-1] // tm).astype(I32).reshape(1)
    blk_start = jnp.arange(nb, dtype=I32) * tm
    blk_exp = jnp.minimum(jnp.sum((pad_end[None, :] <= blk_start[:, None]).astype(I32), axis=1),
                          N_EXPERTS - 1).astype(I32)
    last_exp = blk_exp[jnp.maximum(n_used[0] - 1, 0)]
    blk_exp = jnp.where(jnp.arange(nb) < n_used[0], blk_exp, last_exp)
    eid = jnp.arange(N_EXPERTS, dtype=I32)
    has = counts > 0
    exp_ord = jnp.sum((has[None, :] & (eid[None, :] < eid[:, None])).astype(I32), axis=1)
    exp_nxt = jnp.min(jnp.where(has[None, :] & (eid[None, :] > eid[:, None]), eid[None, :], N_EXPERTS), axis=1)
    exp_nxt = jnp.where(exp_nxt < N_EXPERTS, exp_nxt, -1).astype(I32)
    blk_ord = exp_ord[blk_exp]
    blk_nxt = exp_nxt[blk_exp]

    dest = _dest(ids, pad_start)
    xs = _dispatch(dest, pad_start, counts.astype(I32), h2, nb * tm, min(DISPATCH_TC, n), tm)
    ys = _experts(blk_exp, blk_ord, blk_nxt, n_used, xs, w_gate, w_up, w_down, tm)
    return _combine(dest, ys, x1, wts, mod3, final_g, seq, min(COMBINE_TC, n))


def kernel(x, c, w_ada, b_ada, norm1_g, norm2_g, final_g, w_in, ret_gn_g, cmp_pos_k, cmp_w1_k,
           cmp_w2_k, cmp_pos_v, cmp_w1_v, cmp_w2_v, w_out, w_grp, b_grp, w_exp, b_exp, w_gate,
           w_up, w_down):
    bsz, seq, d = x.shape
    assert w_ada.shape[0] == 1, "single-layer block"
    n = bsz * seq
    x2 = x.reshape(n, d)
    mod3 = _ada(c, w_ada[0], b_ada[0]).reshape(bsz, 6, d)

    o_ret, o_nsa = _token_mixer(x2, mod3, norm1_g[0], w_in[0], ret_gn_g[0], cmp_pos_k[0],
                                cmp_w1_k[0], cmp_w2_k[0], cmp_pos_v[0], cmp_w1_v[0], cmp_w2_v[0],
                                bsz, seq)

    w_route = jnp.concatenate([w_grp[0], w_exp[0]], axis=1)
    w_route = jnp.pad(w_route, ((0, 0), (0, LANES - w_route.shape[1]))).T.astype(BF16)
    x1, h2, lt = _outproj(o_ret, o_nsa, x2, mod3, norm2_g[0], w_out[0], w_route, seq)

    out = _moe(h2, lt, x1, mod3, final_g, b_grp[0], b_exp[0], w_gate[0], w_up[0], w_down[0],
               seq, MOE_TM)
    return out.reshape(bsz, seq, d)
```

```python
import functools

import numpy as np
import jax
import jax.numpy as jnp
from jax import lax
from jax.experimental import pallas as pl
from jax.experimental.pallas import tpu as pltpu

F32 = jnp.float32
BF16 = jnp.bfloat16
I32 = jnp.int32

RET_HEADS = 4
RET_DK = 256
RET_DV = 256
RET_CHUNK = 128
NSA_HEADS = 8
NSA_KV_GROUPS = 2
NSA_HPG = NSA_HEADS // NSA_KV_GROUPS
NSA_D = 128
CMP_BLOCK = 32
CMP_STRIDE = 16
SEL_BLOCK = 64
SEL_COUNT = 16
WIN = 512
N_GROUPS = 8
EXP_PER_GROUP = 8
N_EXPERTS = N_GROUPS * EXP_PER_GROUP
ROPE_BASE = 10000.0
EPS = 1e-6
NEG = -1e30
FORCE_BONUS = 1e4

LANES = 128
SUBLANES = 8
MXU_DIM = 256
VMEM_LIMIT = 52 * 1024 * 1024

ADA_TN = 1024
INPROJ_TM = 2048
INPROJ_TN = 512
RET_CHUNKS = 2
NSA_TQ = 512
NSA_TK = 512
NSA_VT = MXU_DIM
OUTPROJ_TM = 512
ROUTE_TC = 2048
ROUTE_SUB = 512
MOE_TM = 256
DISPATCH_TC = 2048
COMBINE_TC = 256

_C_NQ = 32
_C_KC, _C_VC, _C_KS, _C_VS, _C_KW, _C_VW = 40, 42, 44, 46, 48, 50
_C_GATE = 52
PROJ_MAIN = _C_GATE * LANES
N_GATE_COLS = NSA_HEADS * 3


def _cp(sem, vmem=VMEM_LIMIT):
    return pltpu.CompilerParams(dimension_semantics=sem, vmem_limit_bytes=vmem)


def _silu(v):
    return v * jax.nn.sigmoid(v)


def _dot(a, b):
    return jnp.dot(a, b, preferred_element_type=F32)


def _dot_nt(a, b):
    return lax.dot_general(a, b, (((1,), (1,)), ((), ())), preferred_element_type=F32)


def _dot_tn(a, b):
    return lax.dot_general(a, b, (((0,), (0,)), ((), ())), preferred_element_type=F32)


def _ada_kernel(c_ref, w_ref, b_ref, o_ref):
    ca = _silu(c_ref[...]).astype(BF16)
    o_ref[...] = _dot(ca, w_ref[...].astype(BF16)) + b_ref[...]


def _ada(c, w, b):
    bsz, d = c.shape
    n = w.shape[1]
    tn = min(ADA_TN, n)
    return pl.pallas_call(
        _ada_kernel,
        grid=(n // tn,),
        in_specs=[pl.BlockSpec((bsz, d), lambda j: (0, 0)),
                  pl.BlockSpec((d, tn), lambda j: (0, j)),
                  pl.BlockSpec((1, tn), lambda j: (0, j))],
        out_specs=pl.BlockSpec((bsz, tn), lambda j: (0, j)),
        out_shape=jax.ShapeDtypeStruct((bsz, n), F32),
        compiler_params=_cp(("arbitrary",)),
        name="ada",
    )(c, w, b.reshape(1, n))


def _inproj_kernel(x_hbm, mod_ref, g_ref, wt_ref, wgt_ref, proj_ref, gate_ref, h_ref, x_buf, x_sem):
    i = pl.program_id(0)
    tm = x_buf.shape[0]

    def x_copy(blk):
        return pltpu.make_async_copy(x_hbm.at[pl.ds(pl.multiple_of(blk * tm, tm), tm)], x_buf, x_sem)

    @pl.when(pl.program_id(1) == 0)
    def _():
        @pl.when(i == 0)
        def _():
            x_copy(i).start()

        x_copy(i).wait()
        x = x_buf[...]
        y = x * lax.rsqrt(jnp.mean(x * x, axis=-1, keepdims=True) + EPS) * g_ref[...]
        h = (y * (1.0 + mod_ref[0, 1:2, :]) + mod_ref[0, 0:1, :]).astype(BF16)
        h_ref[...] = h
        gate_ref[...] = _dot_nt(h, wgt_ref[...])

        @pl.when(i + 1 < pl.num_programs(0))
        def _():
            x_copy(i + 1).start(priority=1)

    proj_ref[...] = _dot_nt(h_ref[...], wt_ref[...].astype(BF16)).astype(BF16)


def _inproj(x2, mod3, g, w_t, wg_t, seq):
    n, d = x2.shape
    tm = min(INPROJ_TM, seq)
    tn = INPROJ_TN
    nj = PROJ_MAIN // tn
    return pl.pallas_call(
        _inproj_kernel,
        grid=(n // tm, nj),
        in_specs=[pl.BlockSpec(memory_space=pl.ANY),
                  pl.BlockSpec((1, 6, d), lambda i, j: ((i * tm) // seq, 0, 0)),
                  pl.BlockSpec((1, d), lambda i, j: (0, 0)),
                  pl.BlockSpec((tn, d), lambda i, j: (j, 0)),
                  pl.BlockSpec((LANES, d), lambda i, j: (0, 0))],
        out_specs=[pl.BlockSpec((tm, tn), lambda i, j: (i, j)),
                   pl.BlockSpec((tm, LANES), lambda i, j: (i, 0))],
        out_shape=[jax.ShapeDtypeStruct((n, PROJ_MAIN), BF16),
                   jax.ShapeDtypeStruct((n, LANES), F32)],
        scratch_shapes=[pltpu.VMEM((tm, d), BF16), pltpu.VMEM((tm, d), F32),
                        pltpu.SemaphoreType.DMA(())],
        compiler_params=_cp(("arbitrary", "arbitrary")),
        name="inproj",
    )(x2, mod3, g.reshape(1, d), w_t, wg_t)


def _retention_kernel(q_ref, k_ref, v_ref, g_ref, cos_ref, sin_ref, din_ref, zeta_ref,
                      qdec_ref, cdec_ref, gn_ref, o_ref, s_ref):
    @pl.when(pl.program_id(1) == 0)
    def _():
        s_ref[...] = jnp.zeros_like(s_ref)

    half = RET_DK // 2
    c = RET_CHUNK

    for sub in range(q_ref.shape[0] // c):
        rows = slice(sub * c, (sub + 1) * c)
        cos = cos_ref[rows, :]
        sin = sin_ref[rows, :]

        def rot(a):
            a1, a2 = a[:, :half], a[:, half:]
            return jnp.concatenate([a1 * cos - a2 * sin, a1 * sin + a2 * cos], axis=1)

        for h in range(RET_HEADS):
            qs = slice(h * RET_DK, (h + 1) * RET_DK)
            vs = slice(h * RET_DV, (h + 1) * RET_DV)
            q = rot(q_ref[rows, qs].astype(F32))
            k = rot(k_ref[rows, qs].astype(F32)) * (RET_DK ** -0.5)
            v = v_ref[rows, vs]
            qb = q.astype(BF16)
            kb = k.astype(BF16)
            s = _dot_nt(qb, kb) * din_ref[h]
            inner = _dot(s.astype(BF16), v)
            s_prev = s_ref[h]
            cross = _dot(qb, s_prev.astype(BF16)) * qdec_ref[h]
            kv = _dot_tn((k * zeta_ref[h]).astype(BF16), v)
            s_ref[h] = cdec_ref[h] * s_prev + kv
            o = inner + cross
            mu = jnp.mean(o, axis=-1, keepdims=True)
            oc = o - mu
            var = jnp.mean(oc * oc, axis=-1, keepdims=True)
            o = oc * lax.rsqrt(var + EPS) * gn_ref[:, vs]
            o_ref[rows, vs] = (o * _silu(g_ref[rows, vs].astype(F32))).astype(BF16)


def _retention(proj, gn_g, bsz, seq):
    n = proj.shape[0]
    c = RET_CHUNK
    nc = seq // c
    hw = RET_HEADS * RET_DK
    half = RET_DK // 2
    pos = jnp.arange(seq, dtype=F32)
    inv = ROPE_BASE ** (-jnp.arange(half, dtype=F32) / half)
    ang = pos[:, None] * inv[None, :]
    cos, sin = jnp.cos(ang), jnp.sin(ang)
    log_gamma = jnp.log1p(-jnp.exp2(-5.0 - jnp.arange(RET_HEADS, dtype=F32)))
    idx = jnp.arange(c, dtype=F32)
    rel = idx[:, None] - idx[None, :]
    decay_in = jnp.where(rel >= 0, jnp.exp(log_gamma[:, None, None] * jnp.maximum(rel, 0.0)), 0.0)
    zeta = jnp.exp(log_gamma[:, None] * (c - 1 - idx)[None, :])[:, :, None]
    q_decay = jnp.exp(log_gamma[:, None] * (idx + 1)[None, :])[:, :, None]
    chunk_decay = jnp.exp(log_gamma * c)[:, None, None]
    rb = min(RET_CHUNKS * c, seq)
    ns = seq // rb
    row = lambda b, t: (b * ns + t)
    return pl.pallas_call(
        _retention_kernel,
        grid=(bsz, ns),
        in_specs=[pl.BlockSpec((rb, hw), lambda b, t: (row(b, t), 0)),
                  pl.BlockSpec((rb, hw), lambda b, t: (row(b, t), 1)),
                  pl.BlockSpec((rb, hw), lambda b, t: (row(b, t), 2)),
                  pl.BlockSpec((rb, hw), lambda b, t: (row(b, t), 3)),
                  pl.BlockSpec((rb, half), lambda b, t: (t, 0)),
                  pl.BlockSpec((rb, half), lambda b, t: (t, 0)),
                  pl.BlockSpec((RET_HEADS, c, c), lambda b, t: (0, 0, 0)),
                  pl.BlockSpec((RET_HEADS, c, 1), lambda b, t: (0, 0, 0)),
                  pl.BlockSpec((RET_HEADS, c, 1), lambda b, t: (0, 0, 0)),
                  pl.BlockSpec((RET_HEADS, 1, 1), lambda b, t: (0, 0, 0)),
                  pl.BlockSpec((1, hw), lambda b, t: (0, 0))],
        out_specs=pl.BlockSpec((rb, hw), lambda b, t: (row(b, t), 0)),
        out_shape=jax.ShapeDtypeStruct((n, hw), BF16),
        scratch_shapes=[pltpu.VMEM((RET_HEADS, RET_DK, RET_DV), F32)],
        compiler_params=_cp(("arbitrary", "arbitrary")),
        name="retention",
    )(proj, proj, proj, proj, cos, sin, decay_in, zeta, q_decay, chunk_decay,
      gn_g.reshape(1, hw))


def _compress_kernel(ak_ref, av_ref, pek_ref, pev_ref, w1k_ref, w1v_ref, w2k_ref, w2v_ref,
                     kc_ref, vc_ref, a_scr):
    seq = ak_ref.shape[0]
    nblk = seq // CMP_STRIDE
    a_scr[seq:, :] = jnp.zeros((a_scr.shape[0] - seq, NSA_D), F32)

    def one(a_ref, pe_ref, w1_ref, w2_ref, o_ref):
        a_scr[:seq, :] = a_ref[...].astype(F32)
        flat = jnp.concatenate([a_scr[pl.ds(l, nblk, stride=CMP_STRIDE), :] for l in range(CMP_BLOCK)],
                               axis=1)
        pre = _dot((flat + pe_ref[...]).astype(BF16), w1_ref[...].astype(BF16))
        o_ref[...] = _dot(_silu(pre).astype(BF16), w2_ref[...].astype(BF16)).astype(BF16)

    one(ak_ref, pek_ref, w1k_ref, w2k_ref, kc_ref)
    one(av_ref, pev_ref, w1v_ref, w2v_ref, vc_ref)


def _compress(proj, pek, pev, w1k, w1v, w2k, w2v, bsz, seq):
    g_ = NSA_KV_GROUPS
    nblk = seq // CMP_STRIDE
    full = lambda a: pl.BlockSpec(a.shape, lambda i: (0,) * a.ndim)
    kblk = pl.BlockSpec((seq, NSA_D), lambda i: (i // g_, _C_KC + i % g_))
    vblk = pl.BlockSpec((seq, NSA_D), lambda i: (i // g_, _C_VC + i % g_))
    oblk = pl.BlockSpec((nblk, NSA_D), lambda i: (i, 0))
    return pl.pallas_call(
        _compress_kernel,
        grid=(bsz * g_,),
        in_specs=[kblk, vblk, full(pek), full(pev), full(w1k), full(w1v), full(w2k), full(w2v)],
        out_specs=[oblk, oblk],
        out_shape=[jax.ShapeDtypeStruct((bsz * g_ * nblk, NSA_D), BF16)] * 2,
        scratch_shapes=[pltpu.VMEM((seq + CMP_BLOCK, NSA_D), F32)],
        compiler_params=_cp(("arbitrary",)),
        name="compress",
    )(proj, proj, pek, pev, w1k, w1v, w2k, w2v)


def _nsa_kernel(q_ref, kc_ref, vc_ref, ks_ref, vs_ref, kw_ref, vw_ref, gate_ref, ovt_ref, wbias_ref, dbias_ref,
                cbias_ref,
                o_ref, kaug_ref, vst_ref, vwt_ref, *, tq, tk, vt, seq, wlen):
    crow = kc_ref.shape[0]
    i = pl.program_id(2)
    hg = NSA_HPG
    d = NSA_D
    r = hg * tq

    @pl.when(i == 0)
    def _():
        kaug_ref[:, :d] = ks_ref[...]
        blk = lax.broadcasted_iota(I32, (seq, LANES), 0) // SEL_BLOCK
        lane = lax.broadcasted_iota(I32, (seq, LANES), 1)
        kaug_ref[:, d:] = (blk == lane).astype(BF16)
        ones = jnp.ones((vst_ref.shape[1] - d, vt), BF16)
        for c in range(seq // vt):
            vst_ref[c, :d, :] = vs_ref[c * vt:(c + 1) * vt, :].astype(F32).T.astype(BF16)
            vwt_ref[c, :d, :] = vw_ref[c * vt:(c + 1) * vt, :].astype(F32).T.astype(BF16)
            vst_ref[c, d:, :] = ones
            vwt_ref[c, d:, :] = ones

    def pv_t(vt_ref, first_tile, p):
        out = None
        for c in range(p.shape[0] // vt):
            part = _dot(vt_ref[first_tile + c], p[c * vt:(c + 1) * vt, :])
            out = part if out is None else out + part
        return out

    q = q_ref[...]
    qh = [(q[:, h * d:(h + 1) * d].astype(F32) * (d ** -0.5)).astype(BF16) for h in range(hg)]
    qa = jnp.concatenate(qh, axis=0)
    t_row = i * tq + lax.broadcasted_iota(I32, (1, tq), 1)
    tpos_row = jnp.concatenate([t_row] * hg, axis=1)

    w0 = pl.multiple_of(jnp.maximum(i * tq + tq - wlen, 0), vt)
    sw = _dot_nt(kw_ref[pl.ds(w0, wlen), :], qa)
    wbias = wbias_ref[jnp.minimum(i, wbias_ref.shape[0] - 1)]
    sw = sw + jnp.concatenate([wbias] * hg, axis=1)
    e_w = jnp.exp(sw - jnp.max(sw, axis=0, keepdims=True))
    pv_w = pv_t(vwt_ref, w0 // vt, e_w.astype(BF16))

    sct = _dot_nt(kc_ref[...], qa)
    sct = sct + jnp.concatenate([cbias_ref[i]] * hg, axis=1)
    e_c = jnp.exp(sct - jnp.max(sct, axis=0, keepdims=True))
    den_c = jnp.sum(e_c, axis=0, keepdims=True)
    seen = tpos_row >= CMP_BLOCK - 1
    p_t = jnp.where(seen, e_c / den_c, 0.0)
    o_cmp = _dot_tn(p_t.astype(BF16), vc_ref[...])

    psum_t = p_t[:, 0:tq]
    for h in range(1, hg):
        psum_t = psum_t + p_t[:, h * tq:(h + 1) * tq]
    nsel_blocks = seq // SEL_BLOCK
    nsb = ovt_ref.shape[0]
    imp_t = jnp.dot(ovt_ref[...], psum_t, preferred_element_type=F32,
                    precision=lax.Precision.HIGHEST)
    sidx = lax.broadcasted_iota(I32, (nsb, 1), 0)
    cur = t_row // SEL_BLOCK
    valid = sidx <= cur
    forced = (sidx == 0) | (sidx == cur) | (sidx == cur - 1)
    score = jnp.where(valid, imp_t + jnp.where(forced, FORCE_BONUS, 0.0), -1.0)
    rank = jnp.zeros((nsb, tq), F32)
    for s2 in range(nsel_blocks):
        row = score[s2:s2 + 1, :]
        beats = (row > score) | ((row == score) & (sidx > s2))
        rank = rank + beats.astype(F32)
    sel = valid & (rank < float(min(SEL_COUNT, nsel_blocks)))
    selb_t = jnp.where(sel, 0.0, NEG)
    selb_t = jnp.concatenate([selb_t, jnp.zeros((LANES - nsb, tq), F32)], axis=0)
    selb = selb_t.T.astype(BF16)
    q_aug = jnp.concatenate([jnp.concatenate([qh[h], selb], axis=1) for h in range(hg)], axis=0)

    n_full = (i * tq + 1) // tk
    kl = lax.broadcasted_iota(I32, (tk, 1), 0)

    def tile(j, carry, masked):
        m, acc = carry
        k0 = pl.multiple_of(j * tk, tk)
        s = _dot_nt(kaug_ref[pl.ds(k0, tk), :], q_aug)
        if masked and tq == tk:
            s = s + jnp.concatenate([dbias_ref[...]] * hg, axis=1)
        elif masked:
            s = jnp.where(k0 + kl <= tpos_row, s, NEG)
        m_new = jnp.maximum(m, jnp.max(s, axis=0, keepdims=True))
        p = jnp.exp(s - m_new)
        acc = jnp.exp(m - m_new) * acc + pv_t(vst_ref, j * (tk // vt), p.astype(BF16))
        return m_new, acc

    m0 = jnp.full((1, r), NEG, F32)
    a0 = jnp.zeros((vst_ref.shape[1], r), F32)
    carry = lax.fori_loop(0, n_full, functools.partial(tile, masked=False), (m0, a0))
    _, acc_s = tile(n_full, carry, True)

    gt = jax.nn.sigmoid(gate_ref[...])
    grp = pl.program_id(1)
    for g in range(1, NSA_KV_GROUPS):
        gt = jnp.where(grp == g, pltpu.roll(gt, LANES - g * 3 * hg, 1), gt)
    gt_t = gt.T
    gate_row = lambda j: jnp.concatenate([gt_t[3 * h + j:3 * h + j + 1, :] for h in range(hg)], axis=1)
    o_sw = (acc_s[:d] * (gate_row(1) / acc_s[d:d + 1]) + pv_w[:d] * (gate_row(2) / pv_w[d:d + 1])).T
    for h in range(hg):
        rows = slice(h * tq, (h + 1) * tq)
        o = gt[:, 3 * h:3 * h + 1] * o_cmp[rows] + o_sw[rows]
        o_ref[:, h * d:(h + 1) * d] = o.astype(BF16)


def _nsa(proj, kc, vc, gates_g, bsz, seq):
    n = proj.shape[0]
    g_ = NSA_KV_GROUPS
    tq = NSA_TQ
    tk = min(NSA_TK, seq)
    nq = seq // tq
    ncmp = (seq - CMP_BLOCK) // CMP_STRIDE + 1
    wlen = min(WIN + tq, seq)
    nsel = seq // SEL_BLOCK
    crow = seq // CMP_STRIDE
    assert nsel <= LANES and crow <= LANES and tq <= tk and tk % tq == 0
    nsb = -(-nsel // 8) * 8
    ss = np.arange(nsb)[:, None] * SEL_BLOCK
    cs = np.arange(crow)[None, :] * CMP_STRIDE
    ov = ((cs < ss + SEL_BLOCK) & (cs + CMP_BLOCK > ss)
          & (np.arange(crow)[None, :] < ncmp) & (np.arange(nsb)[:, None] < nsel))
    ov = jnp.asarray(ov.astype(np.float32))
    dq = np.arange(tq)[None, :]
    npat = min(nq, (wlen - tq) // tq + 1)
    kwin = np.arange(wlen)[:, None]
    pats = []
    for p in range(npat):
        delta = (p * tq + dq) - (max(p * tq + tq - wlen, 0) + kwin)
        pats.append(np.where((delta >= 0) & (delta < WIN), 0.0, NEG))
    wbias = jnp.asarray(np.stack(pats).astype(np.float32))
    dbias = jnp.asarray(np.where(np.arange(tk)[:, None] <= dq, 0.0, NEG).astype(np.float32))
    cend = (np.arange(crow) * CMP_STRIDE + CMP_BLOCK - 1)[None, :, None]
    tabs = (np.arange(nq)[:, None, None] * tq + dq[None])
    cbias = jnp.asarray(np.where((cend <= tabs) & (np.arange(crow)[None, :, None] < ncmp), 0.0, NEG)
                        .astype(np.float32))
    kvspec = lambda c0: pl.BlockSpec((seq, NSA_D), lambda b, g, i: (b, c0 + g))
    vt = NSA_VT
    assert tk % vt == 0 and wlen % vt == 0 and tq % vt == 0 and seq % vt == 0
    kern = functools.partial(_nsa_kernel, tq=tq, tk=tk, vt=vt, seq=seq, wlen=wlen)
    return pl.pallas_call(
        kern,
        grid=(bsz, g_, nq),
        in_specs=[pl.BlockSpec((tq, NSA_HPG * NSA_D), lambda b, g, i: (b * nq + i, _C_NQ // NSA_HPG + g)),
                  pl.BlockSpec((crow, NSA_D), lambda b, g, i: (b * g_ + g, 0)),
                  pl.BlockSpec((crow, NSA_D), lambda b, g, i: (b * g_ + g, 0)),
                  kvspec(_C_KS), kvspec(_C_VS), kvspec(_C_KW), kvspec(_C_VW),
                  pl.BlockSpec((tq, LANES), lambda b, g, i: (b * nq + i, 0)),
                  pl.BlockSpec((nsb, crow), lambda b, g, i: (0, 0)),
                  pl.BlockSpec((npat, wlen, tq), lambda b, g, i: (0, 0, 0), pipeline_mode=pl.Buffered(1)),
                  pl.BlockSpec((tk, tq), lambda b, g, i: (0, 0), pipeline_mode=pl.Buffered(1)),
                  pl.BlockSpec((nq, crow, tq), lambda b, g, i: (0, 0, 0), pipeline_mode=pl.Buffered(1))],
        out_specs=pl.BlockSpec((tq, NSA_HPG * NSA_D), lambda b, g, i: (b * nq + i, g)),
        out_shape=jax.ShapeDtypeStruct((n, NSA_HEADS * NSA_D), BF16),
        scratch_shapes=[pltpu.VMEM((seq, 2 * NSA_D), BF16),
                        pltpu.VMEM((seq // vt, NSA_D + 2 * SUBLANES, vt), BF16),
                        pltpu.VMEM((seq // vt, NSA_D + 2 * SUBLANES, vt), BF16)],
        compiler_params=_cp(("arbitrary", "arbitrary", "arbitrary")),
        name="nsa",
    )(proj, kc, vc, proj, proj, proj, proj, gates_g, ov, wbias, dbias, cbias)


def _outproj_kernel(oret_ref, onsa_ref, x_ref, mod_ref, g_ref, w_ref, wr_ref,
                    x1_ref, h2_ref, lt_ref):
    hw = oret_ref.shape[1]
    mix = (_dot(oret_ref[...], w_ref[:hw, :].astype(BF16))
           + _dot(onsa_ref[...], w_ref[hw:, :].astype(BF16)))
    x1 = x_ref[...] + mod_ref[0, 2:3, :] * mix
    x1_ref[...] = x1
    y = x1 * lax.rsqrt(jnp.mean(x1 * x1, axis=-1, keepdims=True) + EPS) * g_ref[...]
    h2 = y * (1.0 + mod_ref[0, 4:5, :]) + mod_ref[0, 3:4, :]
    h2_ref[...] = h2
    lt_ref[...] = _dot_nt(wr_ref[...], h2.astype(BF16))


def _outproj(o_ret, o_nsa, x2, mod3, g, w, wr_bf, seq):
    n, d = x2.shape
    tm = min(OUTPROJ_TM, seq)
    hw = o_ret.shape[1]
    return pl.pallas_call(
        _outproj_kernel,
        grid=(n // tm,),
        in_specs=[pl.BlockSpec((tm, hw), lambda i: (i, 0)),
                  pl.BlockSpec((tm, o_nsa.shape[1]), lambda i: (i, 0)),
                  pl.BlockSpec((tm, d), lambda i: (i, 0)),
                  pl.BlockSpec((1, 6, d), lambda i: ((i * tm) // seq, 0, 0)),
                  pl.BlockSpec((1, d), lambda i: (0, 0)),
                  pl.BlockSpec(w.shape, lambda i: (0, 0), pipeline_mode=pl.Buffered(1)),
                  pl.BlockSpec(wr_bf.shape, lambda i: (0, 0))],
        out_specs=[pl.BlockSpec((tm, d), lambda i: (i, 0)),
                   pl.BlockSpec((tm, d), lambda i: (i, 0)),
                   pl.BlockSpec((LANES, tm), lambda i: (0, i))],
        out_shape=[jax.ShapeDtypeStruct((n, d), F32),
                   jax.ShapeDtypeStruct((n, d), F32),
                   jax.ShapeDtypeStruct((LANES, n), F32)],
        compiler_params=_cp(("arbitrary",)),
        name="outproj",
    )(o_ret, o_nsa, x2, mod3, g.reshape(1, d), w, wr_bf)


def _route_kernel(lt_ref, b_ref, tri_ref, ids_ref, wts_ref, cnt_ref, carry_ref, *, sub):
    @pl.when(pl.program_id(0) == 0)
    def _():
        carry_ref[...] = jnp.zeros_like(carry_ref)

    ng, ne = N_GROUPS, EXP_PER_GROUP
    l = lt_ref[...] + b_ref[:, 0:1]
    tc = l.shape[1]
    ridx = lax.broadcasted_iota(I32, (ng, tc), 0).astype(F32)

    def softmax0(v):
        e = jnp.exp(v - jnp.max(v, axis=0, keepdims=True))
        return e / jnp.sum(e, axis=0, keepdims=True)

    def top1(p):
        top = jnp.max(p, axis=0, keepdims=True)
        idx = jnp.min(jnp.where(p == top, ridx, float(ng)), axis=0, keepdims=True)
        return top, idx

    pg_top, grp = top1(softmax0(l[0:ng]))
    leg = jnp.zeros((ne, tc), F32)
    for g in range(ng):
        leg = jnp.where(grp == float(g), l[ng + g * ne:ng + (g + 1) * ne], leg)
    pe = softmax0(leg)
    p1, i1 = top1(pe)
    p2, i2 = top1(jnp.where(ridx == i1, -1.0, pe))
    den = p1 + p2
    w1 = pg_top * p1 / den
    w2 = pg_top * p2 / den
    e1 = grp * float(ne) + i1
    e2 = grp * float(ne) + i2

    eio = lax.broadcasted_iota(I32, (N_EXPERTS, sub), 0).astype(F32)
    r1 = []
    r2 = []
    carry = carry_ref[:, 0:1]
    for c in range(tc // sub):
        cs = slice(c * sub, (c + 1) * sub)
        oh1 = (eio == e1[:, cs]).astype(F32)
        oh2 = (eio == e2[:, cs]).astype(F32)
        oh = oh1 + oh2
        before = carry + _dot(oh.astype(BF16), tri_ref[...])
        r1.append(jnp.sum(oh1 * before, axis=0, keepdims=True))
        r2.append(jnp.sum(oh2 * before, axis=0, keepdims=True))
        carry = carry + jnp.sum(oh, axis=1, keepdims=True)
    carry_ref[...] = jnp.broadcast_to(carry, carry_ref.shape)
    cnt_ref[...] = jnp.broadcast_to(carry, cnt_ref.shape).astype(I32)
    r1 = jnp.concatenate(r1, axis=1)
    r2 = jnp.concatenate(r2, axis=1)
    zf = jnp.zeros((4, tc), F32)
    ids_ref[...] = jnp.concatenate([e1, e2, r1, r2, zf], axis=0).astype(I32)
    wts_ref[...] = jnp.concatenate([w1, w2, jnp.zeros((6, tc), F32)], axis=0)


def _route(lt, bias_col):
    n = lt.shape[1]
    tc = min(ROUTE_TC, n)
    sub = min(ROUTE_SUB, tc)
    tri =jnp.asarray(np.triu(np.ones((sub, sub), np.float32), 1), BF16)
    return pl.pallas_call(
        functools.partial(_route_kernel, sub=sub),
        grid=(n // tc,),
        in_specs=[pl.BlockSpec((LANES, tc), lambda i: (0, i)),
                  pl.BlockSpec((LANES, LANES), lambda i: (0, 0)),
                  pl.BlockSpec((sub, sub), lambda i: (0, 0))],
        out_specs=[pl.BlockSpec((8, tc), lambda i: (0, i)),
                   pl.BlockSpec((8, tc), lambda i: (0, i)),
                   pl.BlockSpec((N_EXPERTS, LANES), lambda i: (0, 0))],
        out_shape=[jax.ShapeDtypeStruct((8, n), I32),
                   jax.ShapeDtypeStruct((8, n), F32),
                   jax.ShapeDtypeStruct((N_EXPERTS, LANES), I32)],
        scratch_shapes=[pltpu.VMEM((N_EXPERTS, LANES), F32)],
        compiler_params=_cp(("arbitrary",)),
        name="route",
    )(lt, bias_col, tri)


def _dest_kernel(ids_ref, ps_ref, o_ref):
    ids = ids_ref[...].astype(F32)
    tc = ids.shape[1]
    eio = lax.broadcasted_iota(I32, (N_EXPERTS, tc), 0).astype(F32)
    ps = ps_ref[:, 0:1]
    rows = [jnp.sum(jnp.where(eio == ids[k:k + 1], ps, 0.0), axis=0, keepdims=True) + ids[2 + k:3 + k]
            for k in range(2)]
    o_ref[...] = jnp.concatenate(rows + [jnp.zeros((6, tc), F32)], axis=0).astype(I32)


def _dest(ids, pad_start):
    n = ids.shape[1]
    tc = min(ROUTE_TC, n)
    ps = jnp.broadcast_to(pad_start.astype(F32)[:, None], (N_EXPERTS, LANES))
    dest = pl.pallas_call(
        _dest_kernel,
        grid=(n // tc,),
        in_specs=[pl.BlockSpec((8, tc), lambda i: (0, i)),
                  pl.BlockSpec((N_EXPERTS, LANES), lambda i: (0, 0))],
        out_specs=pl.BlockSpec((8, tc), lambda i: (0, i)),
        out_shape=jax.ShapeDtypeStruct((8, n), I32),
        compiler_params=_cp(("arbitrary",)),
        name="dest",
    )(ids, ps)
    return dest[:2].reshape(2 * n)


def _dispatch_kernel(dst0_ref, dst1_ref, ps_ref, cnt_ref, h_ref, xs_ref, zbuf, sem, zsem, *, tcd, tm, zr):
    def zero_fill(wait):
        sizes = [zr >> b for b in range(zr.bit_length()) if (zr >> b) >= SUBLANES]

        def zero_rows(row, size):
            return pltpu.make_async_copy(zbuf.at[pl.ds(0, size)],
                                         xs_ref.at[pl.ds(pl.multiple_of(row, SUBLANES), size)], zsem)

        def zero_row(row):
            return pltpu.make_async_copy(zbuf.at[pl.ds(0, 1)], xs_ref.at[pl.ds(row, 1)], zsem)

        def fill(e, _):
            cnt = cnt_ref[e]
            cnt8 = lax.div(cnt + (SUBLANES - 1), SUBLANES) * SUBLANES
            gap = lax.rem(tm - lax.rem(cnt8, tm), tm)
            base = ps_ref[e]

            def single(c, _):
                cp = zero_row(base + cnt + c)
                cp.wait() if wait else cp.start()
                return 0

            lax.fori_loop(0, cnt8 - cnt, single, 0)
            row = base + cnt8
            for size in sizes:
                has = lax.rem(lax.div(gap, size), 2) == 1

                @pl.when(has)
                def _():
                    cp = zero_rows(row, size)
                    cp.wait() if wait else cp.start()

                row = row + jnp.where(has, size, 0)
            return 0

        last = N_EXPERTS - 1
        used_rows = ps_ref[last] + lax.div(cnt_ref[last] + (tm - 1), tm) * tm
        n_tail = lax.div(xs_ref.shape[0] - used_rows, zr)

        def tail(c, _):
            cp = zero_rows(used_rows + c * zr, zr)
            cp.wait() if wait else cp.start()
            return 0

        lax.fori_loop(0, N_EXPERTS, fill, 0)
        lax.fori_loop(0, n_tail, tail, 0)

    @pl.when(pl.program_id(0) == 0)
    def _():
        zbuf[...] = jnp.zeros_like(zbuf)
        zero_fill(wait=False)

    def issue(t, _):
        for k, dst_ref in enumerate((dst0_ref, dst1_ref)):
            dst = dst_ref[t]
            pltpu.make_async_copy(h_ref.at[pl.ds(t, 1)], xs_ref.at[pl.ds(dst, 1)], sem).start(priority=k)
        return 0

    lax.fori_loop(0, tcd, issue, 0, unroll=8)
    for k in range(2):
        pltpu.make_async_copy(h_ref, xs_ref.at[pl.ds(0, tcd)], sem).wait()

    @pl.when(pl.program_id(0) == pl.num_programs(0) - 1)
    def _():
        zero_fill(wait=True)


def _dispatch(dest, pad_start, counts, h2, rows, tcd, tm):
    n, d = h2.shape
    assert tm & (tm - 1) == 0 and tm >= 2 * SUBLANES
    zr = tm // 2
    return pl.pallas_call(
        functools.partial(_dispatch_kernel, tcd=tcd, tm=tm, zr=zr),
        grid=(n // tcd,),
        in_specs=[pl.BlockSpec((tcd,), lambda i: (i,), memory_space=pltpu.SMEM),
                  pl.BlockSpec((tcd,), lambda i: (n // tcd + i,), memory_space=pltpu.SMEM),
                  pl.BlockSpec(memory_space=pltpu.SMEM),
                  pl.BlockSpec(memory_space=pltpu.SMEM),
                  pl.BlockSpec((tcd, d), lambda i: (i, 0))],
        out_specs=pl.BlockSpec(memory_space=pl.ANY),
        out_shape=jax.ShapeDtypeStruct((rows, d), h2.dtype),
        scratch_shapes=[pltpu.VMEM((zr, d), h2.dtype), pltpu.SemaphoreType.DMA(()),
                        pltpu.SemaphoreType.DMA(())],
        compiler_params=_cp(("arbitrary",)),
        name="dispatch",
    )(dest, dest, pad_start, counts, h2)


def _experts_kernel(be_ref, ord_ref, nxt_ref, nu_ref, xs_ref, wg_hbm, wu_hbm, wd_hbm, ys_ref,
                    g_f, u_f, d_f, g_s, u_s, d_s, sem):
    i = pl.program_id(0)
    used = i < nu_ref[0]
    e = be_ref[i]
    fresh = (i == 0) | (e != be_ref[jnp.maximum(i - 1, 0)])
    slot = lax.rem(ord_ref[i], 2)

    def fetch(expert, s):
        return [pltpu.make_async_copy(w.at[expert], f.at[s], sem.at[s, k])
                for k, (w, f) in enumerate(((wg_hbm, g_f), (wu_hbm, u_f), (wd_hbm, d_f)))]

    @pl.when(used & (i == 0))
    def _():
        for c in fetch(e, slot):
            c.start(priority=1)

    @pl.when(used & fresh)
    def _():
        for c in fetch(e, slot):
            c.wait()
        nxt = nxt_ref[i]

        @pl.when(nxt >= 0)
        def _():
            for c in fetch(nxt, 1 - slot):
                c.start(priority=1)

        g_s[...] = g_f[slot].astype(BF16)
        u_s[...] = u_f[slot].astype(BF16)
        d_s[...] = d_f[slot].astype(BF16)

    @pl.when(used)
    def _():
        x = xs_ref[...].astype(BF16)
        a = _silu(_dot(x, g_s[...])) * _dot(x, u_s[...])
        ys_ref[...] = _dot(a.astype(BF16), d_s[...])

    @pl.when(jnp.logical_not(used))
    def _():
        ys_ref[...] = jnp.zeros_like(ys_ref)


def _experts(blk_exp, blk_ord, blk_nxt, n_used, xs, w_gate, w_up, w_down, tm):
    d = xs.shape[1]
    de = w_gate.shape[2]
    nb = blk_exp.shape[0]
    row = lambda i, be, od, nx, nu: (jnp.minimum(i, nu[0] - 1), 0)
    anyspec = pl.BlockSpec(memory_space=pl.ANY)
    grid_spec = pltpu.PrefetchScalarGridSpec(
        num_scalar_prefetch=4,
        grid=(nb,),
        in_specs=[pl.BlockSpec((tm, d), row), anyspec, anyspec, anyspec],
        out_specs=pl.BlockSpec((tm, d), lambda i, be, od, nx, nu: (i, 0)),
        scratch_shapes=[pltpu.VMEM((2, d, de), F32), pltpu.VMEM((2, d, de), F32),
                        pltpu.VMEM((2, de, d), F32),
                        pltpu.VMEM((d, de), BF16), pltpu.VMEM((d, de), BF16),
                        pltpu.VMEM((de, d), BF16),
                        pltpu.SemaphoreType.DMA((2, 3))],
    )
    return pl.pallas_call(
        _experts_kernel,
        grid_spec=grid_spec,
        out_shape=jax.ShapeDtypeStruct((nb * tm, d), F32),
        compiler_params=_cp(("arbitrary",)),
        name="experts",
    )(blk_exp, blk_ord, blk_nxt, n_used, xs, w_gate, w_up, w_down)


def _combine_kernel(cur0_ref, cur1_ref, nxt0_ref, nxt1_ref, ys_ref, x1_ref, wt_ref, mod_ref, g_ref,
                    o_ref, buf, sem, *, tc):
    i = pl.program_id(0)
    slot = lax.rem(i, 2)

    def gather(id_refs, s):
        def issue(t, _):
            for k, id_ref in enumerate(id_refs):
                src = id_ref[t]
                pltpu.make_async_copy(ys_ref.at[pl.ds(src, 1)], buf.at[s, k, pl.ds(t, 1)],
                                      sem.at[s]).start(priority=k)
            return 0

        lax.fori_loop(0, tc, issue, 0, unroll=8)

    @pl.when(i == 0)
    def _():
        gather((cur0_ref, cur1_ref), 0)

    @pl.when(i + 1 < pl.num_programs(0))
    def _():
        gather((nxt0_ref, nxt1_ref), 1 - slot)

    for k in range(2):
        pltpu.make_async_copy(ys_ref.at[pl.ds(0, tc)], buf.at[slot, k], sem.at[slot]).wait()

    w_rows = wt_ref[...]
    w_cols = jnp.concatenate([w_rows, jnp.zeros((LANES - w_rows.shape[0], tc), F32)], axis=0).T
    moe = buf[slot, 0] * w_cols[:, 0:1] + buf[slot, 1] * w_cols[:, 1:2]
    x2 = x1_ref[...] + mod_ref[0, 5:6, :] * moe
    o_ref[...] = x2 * lax.rsqrt(jnp.mean(x2 * x2, axis=-1, keepdims=True) + EPS) * g_ref[...]


def _combine(dest, ys, x1, wts, mod3, final_g, seq, tc):
    n, d = x1.shape
    nt = n // tc
    tbl = lambda k, off: pl.BlockSpec((tc,), lambda i: (k * nt + jnp.minimum(i + off, nt - 1),),
                                      memory_space=pltpu.SMEM)
    return pl.pallas_call(
        functools.partial(_combine_kernel, tc=tc),
        grid=(nt,),
        in_specs=[tbl(0, 0), tbl(1, 0), tbl(0, 1), tbl(1, 1),
                  pl.BlockSpec(memory_space=pl.ANY),
                  pl.BlockSpec((tc, d), lambda i: (i, 0)),
                  pl.BlockSpec((8, tc), lambda i: (0, i)),
                  pl.BlockSpec((1, 6, d), lambda i: ((i * tc) // seq, 0, 0)),
                  pl.BlockSpec((1, d), lambda i: (0, 0))],
        out_specs=pl.BlockSpec((tc, d), lambda i: (i, 0)),
        out_shape=jax.ShapeDtypeStruct((n, d), F32),
        scratch_shapes=[pltpu.VMEM((2, 2, tc, d), F32), pltpu.SemaphoreType.DMA((2,))],
        compiler_params=_cp(("arbitrary",)),
        name="combine",
    )(dest, dest, dest, dest, ys, x1, wts, mod3, final_g.reshape(1, d))


def _token_mixer(x2, mod3, norm1_g, w_in, ret_gn_g, cmp_pos_k, cmp_w1_k, cmp_w2_k,
                 cmp_pos_v, cmp_w1_v, cmp_w2_v, bsz, seq):
    n, d = x2.shape
    w_t = w_in.T
    w_gate_t = jnp.pad(w_t[PROJ_MAIN:], ((0, LANES - N_GATE_COLS), (0, 0))).astype(BF16)
    proj, gate_logits = _inproj(x2, mod3, norm1_g, w_t, w_gate_t, seq)

    o_ret = _retention(proj, ret_gn_g, bsz, seq)

    g_ = NSA_KV_GROUPS
    crow = seq // CMP_STRIDE

    pe_flat = lambda pe: pe.reshape(1, CMP_BLOCK * NSA_D)
    kc, vc = _compress(proj, pe_flat(cmp_pos_k), pe_flat(cmp_pos_v),
                       cmp_w1_k, cmp_w1_v, cmp_w2_k, cmp_w2_v, bsz, seq)

    o_nsa = _nsa(proj, kc, vc, gate_logits, bsz, seq)
    return o_ret, o_nsa


def _moe(h2, lt, x1, mod3, final_g, b_grp, b_exp, w_gate, w_up, w_down, seq, tm):
    n, d = h2.shape
    bias_col = jnp.zeros((LANES,), F32).at[:N_GROUPS].set(b_grp).at[N_GROUPS:N_GROUPS + N_EXPERTS].set(b_exp)
    bias_col = jnp.broadcast_to(bias_col[:, None], (LANES, LANES))
    ids, wts, cnt = _route(lt, bias_col)

    counts = cnt[:, 0]
    padded = (counts + tm - 1) // tm * tm
    pad_end = jnp.cumsum(padded)
    pad_start = (pad_end - padded).astype(I32)
    nb = (2 * n) // tm + N_EXPERTS
    n_used = (pad_end[-1] // tm).astype(I32).reshape(1)
    blk_start = jnp.arange(nb, dtype=I32) * tm
    blk_exp = jnp.minimum(jnp.sum((pad_end[None, :] <= blk_start[:, None]).astype(I32), axis=1),
                          N_EXPERTS - 1).astype(I32)
    last_exp = blk_exp[jnp.maximum(n_used[0] - 1, 0)]
    blk_exp = jnp.where(jnp.arange(nb) < n_used[0], blk_exp, last_exp)
    eid = jnp.arange(N_EXPERTS, dtype=I32)
    has = counts > 0
    exp_ord = jnp.sum((has[None, :] & (eid[None, :] < eid[:, None])).astype(I32), axis=1)
    exp_nxt = jnp.min(jnp.where(has[None, :] & (eid[None, :] > eid[:, None]), eid[None, :], N_EXPERTS), axis=1)
    exp_nxt = jnp.where(exp_nxt < N_EXPERTS, exp_nxt, -1).astype(I32)
    blk_ord = exp_ord[blk_exp]
    blk_nxt = exp_nxt[blk_exp]

    dest = _dest(ids, pad_start)
    xs = _dispatch(dest, pad_start, counts.astype(I32), h2, nb * tm, min(DISPATCH_TC, n), tm)
    ys = _experts(blk_exp, blk_ord, blk_nxt, n_used, xs, w_gate, w_up, w_down, tm)
    return _combine(dest, ys, x1, wts, mod3, final_g, seq, min(COMBINE_TC, n))


def kernel(x, c, w_ada, b_ada, norm1_g, norm2_g, final_g, w_in, ret_gn_g, cmp_pos_k, cmp_w1_k,
           cmp_w2_k, cmp_pos_v, cmp_w1_v, cmp_w2_v, w_out, w_grp, b_grp, w_exp, b_exp, w_gate,
           w_up, w_down):
    bsz, seq, d = x.shape
    assert w_ada.shape[0] == 1, "single-layer block"
    n = bsz * seq
    x2 = x.reshape(n, d)
    mod3 = _ada(c, w_ada[0], b_ada[0]).reshape(bsz, 6, d)

    o_ret, o_nsa = _token_mixer(x2, mod3, norm1_g[0], w_in[0], ret_gn_g[0], cmp_pos_k[0],
                                cmp_w1_k[0], cmp_w2_k[0], cmp_pos_v[0], cmp_w1_v[0], cmp_w2_v[0],
                                bsz, seq)

    w_route = jnp.concatenate([w_grp[0], w_exp[0]], axis=1)
    w_route = jnp.pad(w_route, ((0, 0), (0, LANES - w_route.shape[1]))).T.astype(BF16)
    x1, h2, lt = _outproj(o_ret, o_nsa, x2, mod3, norm2_g[0], w_out[0], w_route, seq)

    out = _moe(h2, lt, x1, mod3, final_g, b_grp[0], b_exp[0], w_gate[0], w_up[0], w_down[0],
               seq, MOE_TM)
    return out.reshape(bsz, seq, d)
```
